```python
import math
import jax, jax.numpy as jnp
from jax import lax
import numpy as np

D_MODEL = 1024
BATCH = 8
SEQ = 4096
DEPTH = 2

CHUNK = 64
MIX_WIDTH = D_MODEL
ATT_HEADS = 8
HEAD_DIM = 64
ATT_WIDTH = ATT_HEADS * HEAD_DIM
CONV_CH = MIX_WIDTH - ATT_WIDTH
CONV_WIDTH = 31
D_FF = 4 * D_MODEL
QBLK = 128
N_IN = 3 * ATT_WIDTH + ATT_HEADS + 2 * CONV_CH
EPS = 1e-6

kernel_name = "fox_conformer_hybrid_trunk"


def rms_norm(x, g):
    xf = x.astype(jnp.float32)
    y = xf * lax.rsqrt(jnp.mean(xf * xf, axis=-1, keepdims=True) + EPS)
    return (y * g.astype(jnp.float32)).astype(x.dtype)


def layer_norm(x, g, b):
    xf = x.astype(jnp.float32)
    mu = jnp.mean(xf, axis=-1, keepdims=True)
    var = jnp.mean(jnp.square(xf - mu), axis=-1, keepdims=True)
    y = (xf - mu) * lax.rsqrt(var + EPS)
    return (y * g.astype(jnp.float32) + b.astype(jnp.float32)).astype(x.dtype)


def forgetting_attention(q, k, v, logf):
    S = q.shape[2]
    scale = 1.0 / math.sqrt(q.shape[-1])
    c = jnp.cumsum(logf.astype(jnp.float32), axis=-1)
    outs = []
    for blk in range(S // QBLK):
        q0, q1 = blk * QBLK, (blk + 1) * QBLK
        qb = q[:, :, q0:q1]
        kb = k[:, :, :q1]
        vb = v[:, :, :q1]
        s = (jnp.einsum('bhqd,bhkd->bhqk', qb, kb).astype(jnp.float32) * scale
             + (c[:, :, q0:q1, None] - c[:, :, None, :q1]))
        mask = jnp.arange(q0, q1)[:, None] >= jnp.arange(q1)[None, :]
        s = jnp.where(mask, s, -jnp.inf)
        p = jax.nn.softmax(s, axis=-1)
        outs.append(jnp.einsum('bhqk,bhkd->bhqd', p.astype(vb.dtype), vb))
    return jnp.concatenate(outs, axis=2)


def causal_depthwise_conv(x, w, b):
    W, C = w.shape
    y = lax.conv_general_dilated(
        x, w.reshape(W, 1, C).astype(x.dtype), window_strides=(1,), padding=[(W - 1, 0)],
        dimension_numbers=('NWC', 'WIO', 'NWC'), feature_group_count=C)
    return y + b.astype(x.dtype)


def hybrid_mixer(x, norm_g, w_in, b_f, q_norm_g, k_norm_g, conv_w, conv_b, conv_ln_g, conv_ln_b, w_o):
    B, S, _ = x.shape
    u = rms_norm(x, norm_g)
    proj = jnp.einsum('bsd,dn->bsn', u, w_in)
    o1 = ATT_WIDTH; o2 = 2 * ATT_WIDTH; o3 = 3 * ATT_WIDTH; o4 = o3 + ATT_HEADS
    q = proj[..., :o1].reshape(B, S, ATT_HEADS, HEAD_DIM)
    k = proj[..., o1:o2].reshape(B, S, ATT_HEADS, HEAD_DIM)
    v = proj[..., o2:o3].reshape(B, S, ATT_HEADS, HEAD_DIM)
    f_logit = proj[..., o3:o4]
    glu_in = proj[..., o4:]

    q = rms_norm(q, q_norm_g).transpose(0, 2, 1, 3)
    k = rms_norm(k, k_norm_g).transpose(0, 2, 1, 3)
    v = v.transpose(0, 2, 1, 3)
    logf = jax.nn.log_sigmoid(f_logit.astype(jnp.float32) + b_f.astype(jnp.float32)).transpose(0, 2, 1)
    att = forgetting_attention(q, k, v, logf).transpose(0, 2, 1, 3).reshape(B, S, ATT_WIDTH)

    a, g = jnp.split(glu_in, 2, axis=-1)
    h = a * jax.nn.sigmoid(g)
    h = causal_depthwise_conv(h, conv_w, conv_b)
    h = layer_norm(h, conv_ln_g, conv_ln_b)
    h = jax.nn.silu(h)

    mixed = jnp.concatenate([att, h.astype(att.dtype)], axis=-1)
    return x + jnp.einsum('bsm,md->bsd', mixed, w_o)


def sq_relu_mlp(x, norm_g, w1, w2):
    u = rms_norm(x, norm_g)
    h = jnp.square(jax.nn.relu(jnp.einsum('bsd,df->bsf', u, w1)))
    return x + jnp.einsum('bsf,fd->bsd', h, w2)


def _fwd_setup_inputs(seed: int = 0) -> dict:
    key = jax.random.key(seed)
    ks = jax.random.split(key, 16)
    f32 = jnp.float32
    nrm = lambda k, shape, s: jax.random.normal(k, shape, f32) * s
    return {
        "x": jax.random.normal(ks[0], (BATCH, SEQ, D_MODEL), f32),
        "norm1_g": 1.0 + nrm(ks[1], (DEPTH, D_MODEL), 0.02),
        "w_in": nrm(ks[2], (DEPTH, D_MODEL, N_IN), D_MODEL ** -0.5),
        "b_f": 3.0 + nrm(ks[3], (DEPTH, ATT_HEADS), 0.1),
        "q_norm_g": 1.0 + nrm(ks[4], (DEPTH, HEAD_DIM), 0.02),
        "k_norm_g": 1.0 + nrm(ks[5], (DEPTH, HEAD_DIM), 0.02),
        "conv_w": nrm(ks[6], (DEPTH, CONV_WIDTH, CONV_CH), CONV_WIDTH ** -0.5),
        "conv_b": nrm(ks[7], (DEPTH, CONV_CH), 0.01),
        "conv_ln_g": 1.0 + nrm(ks[8], (DEPTH, CONV_CH), 0.02),
        "conv_ln_b": nrm(ks[9], (DEPTH, CONV_CH), 0.01),
        "w_o": nrm(ks[10], (DEPTH, MIX_WIDTH, D_MODEL), MIX_WIDTH ** -0.5),
        "norm2_g": 1.0 + nrm(ks[11], (DEPTH, D_MODEL), 0.02),
        "w_mlp_in": nrm(ks[12], (DEPTH, D_MODEL, D_FF), D_MODEL ** -0.5),
        "w_mlp_out": nrm(ks[13], (DEPTH, D_FF, D_MODEL), D_FF ** -0.5),
    }


def _fwd_reference(x, norm1_g, w_in, b_f, q_norm_g, k_norm_g, conv_w, conv_b, conv_ln_g, conv_ln_b,
              w_o, norm2_g, w_mlp_in, w_mlp_out):
    for l in range(DEPTH):
        x = hybrid_mixer(x, norm1_g[l], w_in[l], b_f[l], q_norm_g[l], k_norm_g[l], conv_w[l],
                         conv_b[l], conv_ln_g[l], conv_ln_b[l], w_o[l])
        x = sq_relu_mlp(x, norm2_g[l], w_mlp_in[l], w_mlp_out[l])
    return x


import jax as _jax
import jax.numpy as _jnp

TWIN_FORMAT = 'train_step'
FWD_PARAMS = ['x', 'norm1_g', 'w_in', 'b_f', 'q_norm_g', 'k_norm_g', 'conv_w', 'conv_b', 'conv_ln_g', 'conv_ln_b', 'w_o', 'norm2_g', 'w_mlp_in', 'w_mlp_out']
TWIN_WEIGHTS = ['norm1_g', 'w_in', 'b_f', 'q_norm_g', 'k_norm_g', 'conv_w', 'conv_b', 'conv_ln_g', 'conv_ln_b', 'w_o', 'norm2_g', 'w_mlp_in', 'w_mlp_out']
TWIN_DIFF_INPUT = 'x'
TWIN_INPUTS = ['x', 'norm1_g', 'w_in', 'b_f', 'q_norm_g', 'k_norm_g', 'conv_w', 'conv_b', 'conv_ln_g', 'conv_ln_b', 'w_o', 'norm2_g', 'w_mlp_in', 'w_mlp_out', 'loss_target', 'm_norm1_g', 'm_w_in', 'm_b_f', 'm_q_norm_g', 'm_k_norm_g', 'm_conv_w', 'm_conv_b', 'm_conv_ln_g', 'm_conv_ln_b', 'm_w_o', 'm_norm2_g', 'm_w_mlp_in', 'm_w_mlp_out', 'v_norm1_g', 'v_w_in', 'v_b_f', 'v_q_norm_g', 'v_k_norm_g', 'v_conv_w', 'v_conv_b', 'v_conv_ln_g', 'v_conv_ln_b', 'v_w_o', 'v_norm2_g', 'v_w_mlp_in', 'v_w_mlp_out']
TWIN_OUTPUTS = ['loss', 'grad_x', 'grad_norm1_g', 'grad_w_in', 'grad_b_f', 'grad_q_norm_g', 'grad_k_norm_g', 'grad_conv_w', 'grad_conv_b', 'grad_conv_ln_g', 'grad_conv_ln_b', 'grad_w_o', 'grad_norm2_g', 'grad_w_mlp_in', 'grad_w_mlp_out', 'delta_norm1_g', 'delta_w_in', 'delta_b_f', 'delta_q_norm_g', 'delta_k_norm_g', 'delta_conv_w', 'delta_conv_b', 'delta_conv_ln_g', 'delta_conv_ln_b', 'delta_w_o', 'delta_norm2_g', 'delta_w_mlp_in', 'delta_w_mlp_out', 'new_m_norm1_g', 'new_m_w_in', 'new_m_b_f', 'new_m_q_norm_g', 'new_m_k_norm_g', 'new_m_conv_w', 'new_m_conv_b', 'new_m_conv_ln_g', 'new_m_conv_ln_b', 'new_m_w_o', 'new_m_norm2_g', 'new_m_w_mlp_in', 'new_m_w_mlp_out', 'new_v_norm1_g', 'new_v_w_in', 'new_v_b_f', 'new_v_q_norm_g', 'new_v_k_norm_g', 'new_v_conv_w', 'new_v_conv_b', 'new_v_conv_ln_g', 'new_v_conv_ln_b', 'new_v_w_o', 'new_v_norm2_g', 'new_v_w_mlp_in', 'new_v_w_mlp_out']
TWIN_LEAF_KINDS = {'loss': 'loss', 'grad_x': 'grad_x', 'grad_norm1_g': 'grad_w', 'grad_w_in': 'grad_w', 'grad_b_f': 'grad_w', 'grad_q_norm_g': 'grad_w', 'grad_k_norm_g': 'grad_w', 'grad_conv_w': 'grad_w', 'grad_conv_b': 'grad_w', 'grad_conv_ln_g': 'grad_w', 'grad_conv_ln_b': 'grad_w', 'grad_w_o': 'grad_w', 'grad_norm2_g': 'grad_w', 'grad_w_mlp_in': 'grad_w', 'grad_w_mlp_out': 'grad_w', 'delta_norm1_g': 'delta_w', 'delta_w_in': 'delta_w', 'delta_b_f': 'delta_w', 'delta_q_norm_g': 'delta_w', 'delta_k_norm_g': 'delta_w', 'delta_conv_w': 'delta_w', 'delta_conv_b': 'delta_w', 'delta_conv_ln_g': 'delta_w', 'delta_conv_ln_b': 'delta_w', 'delta_w_o': 'delta_w', 'delta_norm2_g': 'delta_w', 'delta_w_mlp_in': 'delta_w', 'delta_w_mlp_out': 'delta_w', 'new_m_norm1_g': 'new_m', 'new_m_w_in': 'new_m', 'new_m_b_f': 'new_m', 'new_m_q_norm_g': 'new_m', 'new_m_k_norm_g': 'new_m', 'new_m_conv_w': 'new_m', 'new_m_conv_b': 'new_m', 'new_m_conv_ln_g': 'new_m', 'new_m_conv_ln_b': 'new_m', 'new_m_w_o': 'new_m', 'new_m_norm2_g': 'new_m', 'new_m_w_mlp_in': 'new_m', 'new_m_w_mlp_out': 'new_m', 'new_v_norm1_g': 'new_v', 'new_v_w_in': 'new_v', 'new_v_b_f': 'new_v', 'new_v_q_norm_g': 'new_v', 'new_v_k_norm_g': 'new_v', 'new_v_conv_w': 'new_v', 'new_v_conv_b': 'new_v', 'new_v_conv_ln_g': 'new_v', 'new_v_conv_ln_b': 'new_v', 'new_v_w_o': 'new_v', 'new_v_norm2_g': 'new_v', 'new_v_w_mlp_in': 'new_v', 'new_v_w_mlp_out': 'new_v'}


def _forward(args):
    return _fwd_reference(*[args[k] for k in FWD_PARAMS])


def _output_shape():
    out = _jax.eval_shape(lambda: _forward(_fwd_setup_inputs(0)))
    return out.shape, out.dtype

N_MICROBATCH = 1
ADAM_LR = 0.001
ADAM_B1 = 0.9
ADAM_B2 = 0.999
ADAM_EPS = 1e-08
ADAM_WD = 0.01
ADAM_STEP = 10
PER_EXAMPLE_BATCH_AXIS = {'x': 0, 'loss_target': 0}
SHARED_INPUTS = []
_WEIGHT_DTYPES = {'norm1_g': _jnp.float32, 'w_in': _jnp.float32, 'b_f': _jnp.float32, 'q_norm_g': _jnp.float32, 'k_norm_g': _jnp.float32, 'conv_w': _jnp.float32, 'conv_b': _jnp.float32, 'conv_ln_g': _jnp.float32, 'conv_ln_b': _jnp.float32, 'w_o': _jnp.float32, 'norm2_g': _jnp.float32, 'w_mlp_in': _jnp.float32, 'w_mlp_out': _jnp.float32}
MOMENT_SCALE = {'norm1_g': 6.373832e+00, 'w_in': 3.615179e+00, 'b_f': 5.595150e+01, 'q_norm_g': 9.470431e+00, 'k_norm_g': 9.489543e+00, 'conv_w': 4.354530e+00, 'conv_b': 3.923990e+01, 'conv_ln_g': 2.224589e+01, 'conv_ln_b': 2.482889e+01, 'w_o': 8.767209e+00, 'norm2_g': 9.718867e+01, 'w_mlp_in': 4.905697e+00, 'w_mlp_out': 1.785275e+01}


def _to_microbatches(a, axis):
    t = _jnp.moveaxis(a, axis, 0)
    t = t.reshape((N_MICROBATCH, t.shape[0] // N_MICROBATCH) + t.shape[1:])
    return _jnp.moveaxis(t, 1, axis + 1)


def setup_inputs(seed: int = 0) -> dict:
    inp = _fwd_setup_inputs(seed)
    key = _jax.random.fold_in(_jax.random.key(seed), 7919)
    shape, _ = _output_shape()
    out = dict(inp)
    out["loss_target"] = _jax.random.normal(_jax.random.fold_in(key, 0), shape, _jnp.float32)
    for i, name in enumerate(TWIN_WEIGHTS):
        w = inp[name].astype(_jnp.float32)
        if MOMENT_SCALE is None:
            s = _jnp.sqrt(_jnp.mean(_jnp.square(w)) + 1e-30)
        else:
            s = MOMENT_SCALE[name]
        km, kv = _jax.random.split(_jax.random.fold_in(key, i + 1))
        out[name] = w
        out["m_" + name] = s * _jax.random.normal(km, w.shape, _jnp.float32)
        out["v_" + name] = (s * s) * _jax.random.uniform(kv, w.shape, _jnp.float32, 0.5, 1.5)
    if N_MICROBATCH > 1:
        for name, axis in PER_EXAMPLE_BATCH_AXIS.items():
            out[name] = _to_microbatches(out[name], axis)
    return {'x': out['x'], 'norm1_g': out['norm1_g'], 'w_in': out['w_in'], 'b_f': out['b_f'], 'q_norm_g': out['q_norm_g'], 'k_norm_g': out['k_norm_g'], 'conv_w': out['conv_w'], 'conv_b': out['conv_b'], 'conv_ln_g': out['conv_ln_g'], 'conv_ln_b': out['conv_ln_b'], 'w_o': out['w_o'], 'norm2_g': out['norm2_g'], 'w_mlp_in': out['w_mlp_in'], 'w_mlp_out': out['w_mlp_out'], 'loss_target': out['loss_target'], 'm_norm1_g': out['m_norm1_g'], 'm_w_in': out['m_w_in'], 'm_b_f': out['m_b_f'], 'm_q_norm_g': out['m_q_norm_g'], 'm_k_norm_g': out['m_k_norm_g'], 'm_conv_w': out['m_conv_w'], 'm_conv_b': out['m_conv_b'], 'm_conv_ln_g': out['m_conv_ln_g'], 'm_conv_ln_b': out['m_conv_ln_b'], 'm_w_o': out['m_w_o'], 'm_norm2_g': out['m_norm2_g'], 'm_w_mlp_in': out['m_w_mlp_in'], 'm_w_mlp_out': out['m_w_mlp_out'], 'v_norm1_g': out['v_norm1_g'], 'v_w_in': out['v_w_in'], 'v_b_f': out['v_b_f'], 'v_q_norm_g': out['v_q_norm_g'], 'v_k_norm_g': out['v_k_norm_g'], 'v_conv_w': out['v_conv_w'], 'v_conv_b': out['v_conv_b'], 'v_conv_ln_g': out['v_conv_ln_g'], 'v_conv_ln_b': out['v_conv_ln_b'], 'v_w_o': out['v_w_o'], 'v_norm2_g': out['v_norm2_g'], 'v_w_mlp_in': out['v_w_mlp_in'], 'v_w_mlp_out': out['v_w_mlp_out']}


def _loss(weights, diff, rest, loss_target):
    with _jax.named_scope("forward"):
        args = {**rest, TWIN_DIFF_INPUT: diff, **{k: w.astype(_WEIGHT_DTYPES[k]) for k, w in weights.items()}}
        y = _forward(args)
    with _jax.named_scope("loss_head"):
        err = _jnp.square(y.astype(_jnp.float32) - loss_target)
        return 0.5 * _jnp.sum(_jnp.mean(err, axis=-1)) if err.ndim else 0.5 * err


def _adamw(w, g, m, v):
    m = ADAM_B1 * m + (1.0 - ADAM_B1) * g
    v = ADAM_B2 * v + (1.0 - ADAM_B2) * _jnp.square(g)
    m_hat = m / (1.0 - ADAM_B1 ** ADAM_STEP)
    v_hat = v / (1.0 - ADAM_B2 ** ADAM_STEP)
    delta = -ADAM_LR * (m_hat / (_jnp.sqrt(v_hat) + ADAM_EPS) + ADAM_WD * w)
    return delta, m, v


def reference(x, norm1_g, w_in, b_f, q_norm_g, k_norm_g, conv_w, conv_b, conv_ln_g, conv_ln_b, w_o, norm2_g, w_mlp_in, w_mlp_out, loss_target, m_norm1_g, m_w_in, m_b_f, m_q_norm_g, m_k_norm_g, m_conv_w, m_conv_b, m_conv_ln_g, m_conv_ln_b, m_w_o, m_norm2_g, m_w_mlp_in, m_w_mlp_out, v_norm1_g, v_w_in, v_b_f, v_q_norm_g, v_k_norm_g, v_conv_w, v_conv_b, v_conv_ln_g, v_conv_ln_b, v_w_o, v_norm2_g, v_w_mlp_in, v_w_mlp_out):
    given = dict(x=x, norm1_g=norm1_g, w_in=w_in, b_f=b_f, q_norm_g=q_norm_g, k_norm_g=k_norm_g, conv_w=conv_w, conv_b=conv_b, conv_ln_g=conv_ln_g, conv_ln_b=conv_ln_b, w_o=w_o, norm2_g=norm2_g, w_mlp_in=w_mlp_in, w_mlp_out=w_mlp_out, loss_target=loss_target, m_norm1_g=m_norm1_g, m_w_in=m_w_in, m_b_f=m_b_f, m_q_norm_g=m_q_norm_g, m_k_norm_g=m_k_norm_g, m_conv_w=m_conv_w, m_conv_b=m_conv_b, m_conv_ln_g=m_conv_ln_g, m_conv_ln_b=m_conv_ln_b, m_w_o=m_w_o, m_norm2_g=m_norm2_g, m_w_mlp_in=m_w_mlp_in, m_w_mlp_out=m_w_mlp_out, v_norm1_g=v_norm1_g, v_w_in=v_w_in, v_b_f=v_b_f, v_q_norm_g=v_q_norm_g, v_k_norm_g=v_k_norm_g, v_conv_w=v_conv_w, v_conv_b=v_conv_b, v_conv_ln_g=v_conv_ln_g, v_conv_ln_b=v_conv_ln_b, v_w_o=v_w_o, v_norm2_g=v_norm2_g, v_w_mlp_in=v_w_mlp_in, v_w_mlp_out=v_w_mlp_out)
    weights = {n: given[n] for n in TWIN_WEIGHTS}
    shared = {n: given[n] for n in SHARED_INPUTS}
    per_example = {n: given[n] for n in ['x']}
    grad_fn = _jax.value_and_grad(_loss, argnums=(0, 1))

    def one_microbatch(ex, loss_target):
        ex = dict(ex)
        diff = ex.pop(TWIN_DIFF_INPUT)
        return grad_fn(weights, diff, {**shared, **ex}, loss_target)

    if N_MICROBATCH == 1:
        loss, (grad_w, grad_x) = one_microbatch(per_example, given["loss_target"])
    else:
        def body(carry, xs):
            loss_sum, grad_sum = carry
            l_k, (gw_k, gx_k) = one_microbatch(xs[0], xs[1])
            with _jax.named_scope("update"):
                return (loss_sum + l_k, _jax.tree.map(_jnp.add, grad_sum, gw_k)), gx_k

        init = (_jnp.zeros((), _jnp.float32), _jax.tree.map(_jnp.zeros_like, weights))
        (loss, grad_w), grad_x = _jax.lax.scan(body, init, (per_example, given["loss_target"]))
    with _jax.named_scope("update"):
        delta_w, new_m, new_v = {}, {}, {}
        for n in TWIN_WEIGHTS:
            delta_w[n], new_m[n], new_v[n] = _adamw(weights[n], grad_w[n], given["m_" + n], given["v_" + n])
    return (loss, grad_x, *[grad_w[n] for n in TWIN_WEIGHTS], *[delta_w[n] for n in TWIN_WEIGHTS],
            *[new_m[n] for n in TWIN_WEIGHTS], *[new_v[n] for n in TWIN_WEIGHTS])
```

```python
import functools

import jax
import jax.numpy as jnp
from jax import lax
from jax.experimental import pallas as pl
from jax.experimental.pallas import tpu as pltpu

F32 = jnp.float32
BF16 = jnp.bfloat16

D_MODEL = 1024
ATT_HEADS = 8
HEAD_DIM = 64
ATT_W = ATT_HEADS * HEAD_DIM
CONV_CH = 512
CONV_TAPS = 31
CONV_HALO = 32
D_FF = 4 * D_MODEL
N_IN = 3 * ATT_W + ATT_HEADS + 2 * CONV_CH
O_Q, O_K, O_V, O_A, O_G, O_F = 0, 512, 1024, 1536, 2048, 2560
N_INP = O_F + 128
EPS = 1e-6
QK_SCALE = 0.125

ADAM_LR = 0.001
ADAM_B1 = 0.9
ADAM_B2 = 0.999
ADAM_EPS = 1e-08
ADAM_WD = 0.01
ADAM_STEP = 10

N_CHIPS = 4
VMEM_LIMIT = 52 * 1024 * 1024
MESH = pl.DeviceIdType.MESH
ANY = pl.BlockSpec(memory_space=pl.ANY)


def _params(n_axes, **kw):
    return pltpu.CompilerParams(dimension_semantics=("arbitrary",) * n_axes,
                                vmem_limit_bytes=VMEM_LIMIT, **kw)


def _dot(a, b):
    return jnp.dot(a, b, preferred_element_type=F32)


def _dot_nt(a, b):
    return lax.dot_general(a, b, (((1,), (1,)), ((), ())), preferred_element_type=F32)


def _dot_tn(a, b):
    return lax.dot_general(a, b, (((0,), (0,)), ((), ())), preferred_element_type=F32)


def _split3(a):
    a1 = a.astype(BF16)
    r = a - a1.astype(F32)
    a2 = r.astype(BF16)
    a3 = (r - a2.astype(F32)).astype(BF16)
    return a1, a2, a3


def _dot_hi_r(a, b_exact):
    return sum(_dot(p, b_exact) for p in _split3(a))


def _dot_hi_l(a_exact, b):
    return sum(_dot(a_exact, p) for p in _split3(b))


def _sigmoid(x):
    return 1.0 / (1.0 + jnp.exp(-x))


def _head_blockdiag():
    i = jnp.arange(ATT_W) // HEAD_DIM
    return (i[:, None] == i[None, :]).astype(BF16)


def _head_expand():
    h = jnp.arange(128)[:, None]
    j = jnp.arange(ATT_W)[None, :] // HEAD_DIM
    return (h == j).astype(BF16)


def _head_fold():
    i = jnp.arange(ATT_W)[:, None] % HEAD_DIM
    j = jnp.arange(128)[None, :]
    return (i == j).astype(BF16)


def _head_pick():
    i = jnp.arange(ATT_W)[:, None]
    h = jnp.arange(128)[None, :]
    return (i == h * HEAD_DIM).astype(BF16)


def _tril(n):
    r = jnp.arange(n)
    return (r[:, None] >= r[None, :]).astype(BF16)


def _mixer_in_fwd(x, g1, win, qg, kg, bf, cw, cvb, lng, lnb):
    S = x.shape[0]
    TM = min(512, S)
    nb = S // TM

    def body(x_ref, g1_ref, win_ref, qg_ref, kg_ref, bf_ref, cw_ref, cvb_ref, lng_ref, lnb_ref,
             bd_ref, ex_ref, tri_ref,
             u1_ref, proj_ref, q_ref, k_ref, v_ref, cb_ref, ct_ref, h0_ref, h1_ref, h3_ref,
             carry_ref, hbuf_ref):
        i = pl.program_id(0)

        @pl.when(i == 0)
        def _():
            carry_ref[...] = jnp.zeros_like(carry_ref)
            hbuf_ref[0:CONV_HALO, :] = jnp.zeros((CONV_HALO, CONV_CH), F32)

        @pl.when(i > 0)
        def _():
            hbuf_ref[0:CONV_HALO, :] = hbuf_ref[TM:TM + CONV_HALO, :]

        xv = x_ref[...]
        r = lax.rsqrt(jnp.mean(xv * xv, axis=-1, keepdims=True) + EPS)
        u = (xv * r * g1_ref[...]).astype(BF16)
        u1_ref[...] = u
        proj_ref[...] = _dot(u, win_ref[...])

        def headnorm(raw, gain):
            ss = _dot_hi_r(raw * raw, bd_ref[...]) * (1.0 / HEAD_DIM)
            return raw * lax.rsqrt(ss + EPS) * gain

        q_ref[...] = (headnorm(proj_ref[:, O_Q:O_Q + ATT_W], qg_ref[...]) * QK_SCALE).astype(BF16)
        k_ref[...] = headnorm(proj_ref[:, O_K:O_K + ATT_W], kg_ref[...]).astype(BF16)
        v_ref[...] = proj_ref[:, O_V:O_V + ATT_W].astype(BF16)

        zf = proj_ref[:, O_F:O_F + 128] + bf_ref[...]
        logf = jnp.minimum(zf, 0.0) - jnp.log(1.0 + jnp.exp(-jnp.abs(zf)))
        lane = lax.broadcasted_iota(jnp.int32, (TM, 128), 1)
        logf = jnp.where(lane < ATT_HEADS, logf, 0.0)
        c8 = _dot_hi_l(tri_ref[...], logf) + carry_ref[...]
        carry_ref[...] = c8[TM - 1:TM, :]
        cb_ref[...] = _dot_hi_r(c8, ex_ref[...])
        ct_ref[...] = c8.T[0:ATT_HEADS, :]

        h0 = proj_ref[:, O_A:O_A + CONV_CH] * _sigmoid(proj_ref[:, O_G:O_G + CONV_CH])
        h0_ref[...] = h0
        hbuf_ref[CONV_HALO:CONV_HALO + TM, :] = h0
        acc = jnp.zeros((TM, CONV_CH), F32) + cvb_ref[...]
        for j in range(CONV_TAPS):
            acc = acc + cw_ref[j:j + 1, :] * hbuf_ref[pl.ds(CONV_HALO - CONV_TAPS + 1 + j, TM), :]
        h1_ref[...] = acc
        mu = jnp.mean(acc, axis=-1, keepdims=True)
        d = acc - mu
        var = jnp.mean(d * d, axis=-1, keepdims=True)
        h2 = d * lax.rsqrt(var + EPS) * lng_ref[...] + lnb_ref[...]
        h3_ref[...] = (h2 * _sigmoid(h2)).astype(BF16)

    row = lambda w: pl.BlockSpec((TM, w), lambda i: (i, 0))
    full = lambda a: pl.BlockSpec(a.shape, lambda i: (0,) * a.ndim)
    bd, ex, tri = _head_blockdiag(), _head_expand(), _tril(TM)
    ins = (x, g1, win, qg, kg, bf, cw, cvb, lng, lnb, bd, ex, tri)
    return pl.pallas_call(
        body, name="mixer_in_fwd", grid=(nb,),
        in_specs=[row(D_MODEL)] + [full(a) for a in ins[1:]],
        out_specs=[row(D_MODEL), row(N_INP), row(ATT_W), row(ATT_W), row(ATT_W), row(ATT_W),
                   pl.BlockSpec((ATT_HEADS, TM), lambda i: (0, i)),
                   row(CONV_CH), row(CONV_CH), row(CONV_CH)],
        out_shape=[jax.ShapeDtypeStruct((S, D_MODEL), BF16),
                   jax.ShapeDtypeStruct((S, N_INP), F32),
                   jax.ShapeDtypeStruct((S, ATT_W), BF16),
                   jax.ShapeDtypeStruct((S, ATT_W), BF16),
                   jax.ShapeDtypeStruct((S, ATT_W), BF16),
                   jax.ShapeDtypeStruct((S, ATT_W), F32),
                   jax.ShapeDtypeStruct((ATT_HEADS, S), F32),
                   jax.ShapeDtypeStruct((S, CONV_CH), F32),
                   jax.ShapeDtypeStruct((S, CONV_CH), F32),
                   jax.ShapeDtypeStruct((S, CONV_CH), BF16)],
        scratch_shapes=[pltpu.VMEM((1, 128), F32), pltpu.VMEM((TM + CONV_HALO, CONV_CH), F32)],
        compiler_params=_params(1),
    )(*ins)


def _attn_fwd(q, k, v, cb, ct4):
    S = q.shape[0]
    T = min(512, S)
    nq = S // T

    def body(q_ref, k_ref, v_ref, cb_ref, ct_ref, o_ref, lse_ref):
        qi = pl.program_id(1)
        lo = lax.broadcasted_iota(jnp.int32, (T, 128), 1) < HEAD_DIM
        qv = q_ref[...]
        zero = jnp.zeros_like(qv)
        qm = (jnp.where(lo, qv, zero), jnp.where(lo, zero, qv))
        cbv = cb_ref[...]
        ctq = (cbv[:, 0:1], cbv[:, HEAD_DIM:HEAD_DIM + 1])
        tril = (lax.broadcasted_iota(jnp.int32, (T, T), 0) >= lax.broadcasted_iota(jnp.int32, (T, T), 1))

        def step(kj, carry, masked):
            off = pl.multiple_of(kj * T, T)
            kb = k_ref[pl.ds(off, T), :]
            vb = v_ref[pl.ds(off, T), :]
            new = []
            for h in range(2):
                m, l, acc = carry[3 * h:3 * h + 3]
                s = _dot_nt(qm[h], kb) + (ctq[h] - ct_ref[h:h + 1, pl.ds(off, T)])
                if masked:
                    s = jnp.where(tril, s, -1e30)
                m_new = jnp.maximum(m, jnp.max(s, axis=-1, keepdims=True))
                alpha = jnp.exp(m - m_new)
                p = jnp.exp(s - m_new)
                l = alpha * l + jnp.sum(p, axis=-1, keepdims=True)
                acc = alpha * acc + _dot(p.astype(BF16), vb)
                new += [m_new, l, acc]
            return tuple(new)

        init = (jnp.full((T, 1), -1e30, F32), jnp.zeros((T, 1), F32), jnp.zeros((T, 128), F32)) * 2
        carry = lax.fori_loop(0, qi, lambda kj, c: step(kj, c, False), init)
        m0, l0, a0, m1, l1, a1 = step(qi, carry, True)
        o_ref[...] = jnp.where(lo, a0 / l0, a1 / l1).astype(BF16)
        lse_ref[...] = jnp.where(lo, m0 + jnp.log(l0), m1 + jnp.log(l1))

    qblk = pl.BlockSpec((T, 128), lambda hp, qi: (qi, hp))
    seq = pl.BlockSpec((S, 128), lambda hp, qi: (0, hp))
    return pl.pallas_call(
        body, name="attn_fwd", grid=(ATT_HEADS // 2, nq),
        in_specs=[qblk, seq, seq, qblk, pl.BlockSpec((None, 2, S), lambda hp, qi: (hp, 0, 0))],
        out_specs=[qblk, qblk],
        out_shape=[jax.ShapeDtypeStruct((S, ATT_W), BF16), jax.ShapeDtypeStruct((S, ATT_W), F32)],
        compiler_params=_params(2),
    )(q, k, v, cb, ct4)


def _attn_bwd(q, k, v, do, lse, dlt, cb, ct4):
    S = q.shape[0]
    T = min(512, S)
    nq = S // T

    def body(q_ref, do_ref, lse_ref, dlt_ref, cb_ref, k_ref, v_ref, ct_ref, dq_ref, dk_ref, dv_ref, dc_ref, dr_ref):
        kj = pl.program_id(1)

        @pl.when(kj == 0)
        def _():
            dq_ref[...] = jnp.zeros_like(dq_ref)
            dr_ref[...] = jnp.zeros_like(dr_ref)

        lo = lax.broadcasted_iota(jnp.int32, (T, 128), 1) < HEAD_DIM
        tril = (lax.broadcasted_iota(jnp.int32, (T, T), 0) >= lax.broadcasted_iota(jnp.int32, (T, T), 1))
        kb = k_ref[...]
        vb = v_ref[...]
        offk = pl.multiple_of(kj * T, T)
        cs = (ct_ref[0:1, pl.ds(offk, T)], ct_ref[1:2, pl.ds(offk, T)])

        def step(qi, carry, masked):
            off = pl.multiple_of(qi * T, T)
            qb = q_ref[pl.ds(off, T), :]
            dob = do_ref[pl.ds(off, T), :]
            lseb = lse_ref[pl.ds(off, T), :]
            dltb = dlt_ref[pl.ds(off, T), :]
            cbb = cb_ref[pl.ds(off, T), :]
            zero = jnp.zeros_like(qb)
            new, dqs, drs = [], [], []
            for h in range(2):
                dk_a, dv_a, dc_a = carry[3 * h:3 * h + 3]
                cl = HEAD_DIM * h
                qmh = jnp.where(lo, qb, zero) if h == 0 else jnp.where(lo, zero, qb)
                dom = jnp.where(lo, dob, zero) if h == 0 else jnp.where(lo, zero, dob)
                s = _dot_nt(qmh, kb) + (cbb[:, cl:cl + 1] - cs[h])
                if masked:
                    s = jnp.where(tril, s, -1e30)
                p = jnp.exp(s - lseb[:, cl:cl + 1])
                dp = _dot_nt(dom, vb)
                ds = p * (dp - dltb[:, cl:cl + 1])
                pb = p.astype(BF16)
                dsb = ds.astype(BF16)
                dv_a = dv_a + _dot_tn(pb, dob)
                dk_a = dk_a + _dot_tn(dsb, qb)
                dc_a = dc_a + jnp.sum(ds, axis=0, keepdims=True)
                dqs.append(_dot(dsb, kb))
                drs.append(jnp.sum(ds, axis=1, keepdims=True))
                new += [dk_a, dv_a, dc_a]
            dq_ref[pl.ds(off, T), :] += jnp.where(lo, dqs[0], dqs[1])
            dr_ref[pl.ds(off, T), :] += jnp.where(lo, drs[0], drs[1])
            return tuple(new)

        init = (jnp.zeros((T, 128), F32), jnp.zeros((T, 128), F32), jnp.zeros((1, T), F32)) * 2
        carry = step(kj, init, True)
        carry = lax.fori_loop(kj + 1, nq, lambda qi, c: step(qi, c, False), carry)
        dk_ref[...] = jnp.where(lo, carry[0], carry[3])
        dv_ref[...] = jnp.where(lo, carry[1], carry[4])
        dc_ref[0:1, :] = -carry[2]
        dc_ref[1:2, :] = -carry[5]

    seq = pl.BlockSpec((S, 128), lambda hp, kj: (0, hp))
    kblk = pl.BlockSpec((T, 128), lambda hp, kj: (kj, hp))
    return pl.pallas_call(
        body, name="attn_bwd", grid=(ATT_HEADS // 2, nq),
        in_specs=[seq, seq, seq, seq, seq, kblk, kblk,
                  pl.BlockSpec((None, 2, S), lambda hp, kj: (hp, 0, 0))],
        out_specs=[seq, kblk, kblk, pl.BlockSpec((None, 2, T), lambda hp, kj: (hp, 0, kj)), seq],
        out_shape=[jax.ShapeDtypeStruct((S, ATT_W), F32), jax.ShapeDtypeStruct((S, ATT_W), F32),
                   jax.ShapeDtypeStruct((S, ATT_W), F32), jax.ShapeDtypeStruct((ATT_HEADS // 2, 2, S), F32),
                   jax.ShapeDtypeStruct((S, ATT_W), F32)],
        compiler_params=_params(2),
    )(q, do, lse, dlt, cb, k, v, ct4)


def _mixer_out_fwd(x, att, h3, wo, g2):
    S = x.shape[0]
    TM = min(512, S)

    def body(x_ref, att_ref, h3_ref, wo_ref, g2_ref, x2_ref, u2_ref):
        x2 = x_ref[...] + _dot(att_ref[...], wo_ref[0:ATT_W, :]) + _dot(h3_ref[...], wo_ref[ATT_W:2 * ATT_W, :])
        x2_ref[...] = x2
        r = lax.rsqrt(jnp.mean(x2 * x2, axis=-1, keepdims=True) + EPS)
        u2_ref[...] = (x2 * r * g2_ref[...]).astype(BF16)

    row = lambda w: pl.BlockSpec((TM, w), lambda i: (i, 0))
    full = lambda a: pl.BlockSpec(a.shape, lambda i: (0,) * a.ndim)
    return pl.pallas_call(
        body, name="mixer_out_fwd", grid=(S // TM,),
        in_specs=[row(D_MODEL), row(ATT_W), row(CONV_CH), full(wo), full(g2)],
        out_specs=[row(D_MODEL), row(D_MODEL)],
        out_shape=[jax.ShapeDtypeStruct((S, D_MODEL), F32), jax.ShapeDtypeStruct((S, D_MODEL), BF16)],
        compiler_params=_params(1),
    )(x, att, h3, wo, g2)


def _mlp_fwd(x2, u2, w1, w2):
    S = x2.shape[0]
    TM = min(512, S)
    TF = 1024

    def body(x2_ref, u2_ref, w1_ref, w2_ref, x3_ref, z_ref, hh_ref):
        f = pl.program_id(1)

        @pl.when(f == 0)
        def _():
            x3_ref[...] = x2_ref[...]

        z = _dot(u2_ref[...], w1_ref[...])
        z_ref[...] = z
        zr = jnp.maximum(z, 0.0)
        hh = (zr * zr).astype(BF16)
        hh_ref[...] = hh
        x3_ref[...] += _dot(hh, w2_ref[...])

    return pl.pallas_call(
        body, name="mlp_fwd", grid=(S // TM, D_FF // TF),
        in_specs=[pl.BlockSpec((TM, D_MODEL), lambda i, f: (i, 0)), pl.BlockSpec((TM, D_MODEL), lambda i, f: (i, 0)),
                  pl.BlockSpec((D_MODEL, TF), lambda i, f: (0, f)), pl.BlockSpec((TF, D_MODEL), lambda i, f: (f, 0))],
        out_specs=[pl.BlockSpec((TM, D_MODEL), lambda i, f: (i, 0)), pl.BlockSpec((TM, TF), lambda i, f: (i, f)),
                   pl.BlockSpec((TM, TF), lambda i, f: (i, f))],
        out_shape=[jax.ShapeDtypeStruct((S, D_MODEL), F32), jax.ShapeDtypeStruct((S, D_FF), F32),
                   jax.ShapeDtypeStruct((S, D_FF), BF16)],
        compiler_params=_params(2),
    )(x2, u2, w1, w2)


def _loss_fwd_bwd(y, t):
    S = y.shape[0]
    TM = min(512, S)

    def body(y_ref, t_ref, dy_ref, loss_ref):
        @pl.when(pl.program_id(0) == 0)
        def _():
            loss_ref[...] = jnp.zeros_like(loss_ref)

        d = y_ref[...] - t_ref[...]
        dy_ref[...] = d * (1.0 / D_MODEL)
        loss_ref[...] += jnp.sum(d * d)

    row = pl.BlockSpec((TM, D_MODEL), lambda i: (i, 0))
    return pl.pallas_call(
        body, name="loss", grid=(S // TM,),
        in_specs=[row, row], out_specs=[row, pl.BlockSpec((8, 128), lambda i: (0, 0))],
        out_shape=[jax.ShapeDtypeStruct((S, D_MODEL), F32), jax.ShapeDtypeStruct((8, 128), F32)],
        compiler_params=_params(1),
    )(y, t)


def _mlp_bwd(dx3, z, x2, g2, w1, w2):
    S = dx3.shape[0]
    TM = min(512, S)
    TF = 1024
    nf = D_FF // TF

    def body(dx3_ref, z_ref, x2_ref, g2_ref, w1_ref, w2_ref, dz_ref, dx2_ref, dg2_ref, du2_ref):
        i = pl.program_id(0)
        f = pl.program_id(1)

        @pl.when((i == 0) & (f == 0))
        def _():
            dg2_ref[...] = jnp.zeros_like(dg2_ref)

        @pl.when(f == 0)
        def _():
            du2_ref[...] = jnp.zeros_like(du2_ref)

        dhh = _dot_nt(dx3_ref[...].astype(BF16), w2_ref[...])
        dz = (dhh * (2.0 * jnp.maximum(z_ref[...], 0.0))).astype(BF16)
        dz_ref[...] = dz
        du2_ref[...] += _dot_nt(dz, w1_ref[...])

        @pl.when(f == nf - 1)
        def _():
            x2 = x2_ref[...]
            r = lax.rsqrt(jnp.mean(x2 * x2, axis=-1, keepdims=True) + EPS)
            n = x2 * r
            du2 = du2_ref[...]
            t = du2 * g2_ref[...]
            dx2_ref[...] = dx3_ref[...] + r * (t - n * jnp.mean(t * n, axis=-1, keepdims=True))
            dg2_ref[0:1, :] += jnp.sum(du2 * n, axis=0, keepdims=True)

    rowi = pl.BlockSpec((TM, D_MODEL), lambda i, f: (i, 0))
    return pl.pallas_call(
        body, name="mlp_bwd", grid=(S // TM, nf),
        in_specs=[rowi, pl.BlockSpec((TM, TF), lambda i, f: (i, f)), rowi,
                  pl.BlockSpec((1, D_MODEL), lambda i, f: (0, 0)),
                  pl.BlockSpec((D_MODEL, TF), lambda i, f: (0, f)), pl.BlockSpec((TF, D_MODEL), lambda i, f: (f, 0))],
        out_specs=[pl.BlockSpec((TM, TF), lambda i, f: (i, f)), rowi, pl.BlockSpec((8, D_MODEL), lambda i, f: (0, 0))],
        out_shape=[jax.ShapeDtypeStruct((S, D_FF), BF16), jax.ShapeDtypeStruct((S, D_MODEL), F32),
                   jax.ShapeDtypeStruct((8, D_MODEL), F32)],
        scratch_shapes=[pltpu.VMEM((TM, D_MODEL), F32)],
        compiler_params=_params(2),
    )(dx3, z, x2, g2, w1, w2)


def _matmul_tn(a, b, col_shards=1):
    S, I = a.shape
    J = b.shape[1]
    TI = min(I, 1024)
    TJ = 1024 if J % 1024 == 0 else 896
    TS = min(S, 512)
    nk = S // TS
    per = J // col_shards // TJ

    def body(a_ref, b_ref, o_ref):
        @pl.when(pl.program_id(2) == 0)
        def _():
            o_ref[...] = jnp.zeros_like(o_ref)

        o_ref[...] += _dot_tn(a_ref[...].astype(BF16), b_ref[...].astype(BF16))

    return pl.pallas_call(
        body, name="matmul_tn", grid=(I // TI, J // TJ, nk),
        in_specs=[pl.BlockSpec((TS, TI), lambda i, j, k: (k, i)), pl.BlockSpec((TS, TJ), lambda i, j, k: (k, j))],
        out_specs=pl.BlockSpec((None, TI, TJ), lambda i, j, k: (j // per, i, j % per)),
        out_shape=jax.ShapeDtypeStruct((col_shards, I, J // col_shards), F32),
        compiler_params=_params(3),
    )(a, b)


def _mixer_out_bwd(dx2, wo, att, h1, lng, lnb):
    S = dx2.shape[0]
    TM = min(512, S)

    def body(dx2_ref, wo_ref, att_ref, h1_ref, lng_ref, lnb_ref, bd_ref, datt_ref, dlt_ref, dh1_ref, sm_ref):
        @pl.when(pl.program_id(0) == 0)
        def _():
            sm_ref[...] = jnp.zeros_like(sm_ref)

        dxb = dx2_ref[...].astype(BF16)
        datt = _dot_nt(dxb, wo_ref[0:ATT_W, :])
        datt_ref[...] = datt.astype(BF16)
        dlt_ref[...] = _dot_hi_r(datt * att_ref[...].astype(F32), bd_ref[...])
        dh3 = _dot_nt(dxb, wo_ref[ATT_W:2 * ATT_W, :])
        h1 = h1_ref[...]
        mu = jnp.mean(h1, axis=-1, keepdims=True)
        d = h1 - mu
        rstd = lax.rsqrt(jnp.mean(d * d, axis=-1, keepdims=True) + EPS)
        n = d * rstd
        h2 = n * lng_ref[...] + lnb_ref[...]
        sg = _sigmoid(h2)
        dh2 = dh3 * (sg * (1.0 + h2 * (1.0 - sg)))
        dn = dh2 * lng_ref[...]
        dh1 = rstd * (dn - jnp.mean(dn, axis=-1, keepdims=True) - n * jnp.mean(dn * n, axis=-1, keepdims=True))
        dh1_ref[...] = dh1
        sm_ref[0:1, :] += jnp.sum(dh2 * n, axis=0, keepdims=True)
        sm_ref[1:2, :] += jnp.sum(dh2, axis=0, keepdims=True)
        sm_ref[2:3, :] += jnp.sum(dh1, axis=0, keepdims=True)

    row = lambda w: pl.BlockSpec((TM, w), lambda i: (i, 0))
    full = lambda a: pl.BlockSpec(a.shape, lambda i: (0,) * a.ndim)
    bd = _head_blockdiag()
    return pl.pallas_call(
        body, name="mixer_out_bwd", grid=(S // TM,),
        in_specs=[row(D_MODEL), full(wo), row(ATT_W), row(CONV_CH), full(lng), full(lnb), full(bd)],
        out_specs=[row(ATT_W), row(ATT_W), row(CONV_CH), pl.BlockSpec((8, CONV_CH), lambda i: (0, 0))],
        out_shape=[jax.ShapeDtypeStruct((S, ATT_W), BF16), jax.ShapeDtypeStruct((S, ATT_W), F32),
                   jax.ShapeDtypeStruct((S, CONV_CH), F32), jax.ShapeDtypeStruct((8, CONV_CH), F32)],
        compiler_params=_params(1),
    )(dx2, wo, att, h1, lng, lnb, bd)


def _conv_glu_bwd(dh1, h0, proj, cw):
    S = dh1.shape[0]
    TM = min(256, S)
    nb = S // TM
    lead = CONV_HALO - CONV_TAPS + 1

    def body(dh1_ref, dnx_ref, h0_ref, hpv_ref, a_ref, g_ref, cw_ref, dag_ref, dcw_ref, dbuf_ref, hbuf_ref):
        i = pl.program_id(0)

        @pl.when(i == 0)
        def _():
            dcw_ref[...] = jnp.zeros_like(dcw_ref)

        dh1 = dh1_ref[...]
        dbuf_ref[0:TM, :] = dh1
        dbuf_ref[TM:TM + CONV_HALO, :] = jnp.where(i < nb - 1, dnx_ref[0:CONV_HALO, :], 0.0)
        hbuf_ref[0:CONV_HALO, :] = jnp.where(i > 0, hpv_ref[TM - CONV_HALO:TM, :], 0.0)
        hbuf_ref[CONV_HALO:CONV_HALO + TM, :] = h0_ref[...]
        dh0 = jnp.zeros((TM, CONV_CH), F32)
        for j in range(CONV_TAPS):
            dh0 = dh0 + cw_ref[j:j + 1, :] * dbuf_ref[pl.ds(CONV_TAPS - 1 - j, TM), :]
            dcw_ref[j:j + 1, :] += jnp.sum(dh1 * hbuf_ref[pl.ds(lead + j, TM), :], axis=0, keepdims=True)
        sg = _sigmoid(g_ref[...])
        dag_ref[:, 0:CONV_CH] = (dh0 * sg).astype(BF16)
        dag_ref[:, CONV_CH:2 * CONV_CH] = (dh0 * a_ref[...] * sg * (1.0 - sg)).astype(BF16)

    blk = lambda fn: pl.BlockSpec((TM, CONV_CH), fn)
    return pl.pallas_call(
        body, name="conv_glu_bwd", grid=(nb,),
        in_specs=[blk(lambda i: (i, 0)), blk(lambda i: (jnp.minimum(i + 1, nb - 1), 0)),
                  blk(lambda i: (i, 0)), blk(lambda i: (jnp.maximum(i - 1, 0), 0)),
                  blk(lambda i: (i, O_A // CONV_CH)), blk(lambda i: (i, O_G // CONV_CH)),
                  pl.BlockSpec(cw.shape, lambda i: (0, 0))],
        out_specs=[pl.BlockSpec((TM, 2 * CONV_CH), lambda i: (i, 0)), pl.BlockSpec((CONV_HALO, CONV_CH), lambda i: (0, 0))],
        out_shape=[jax.ShapeDtypeStruct((S, 2 * CONV_CH), BF16), jax.ShapeDtypeStruct((CONV_HALO, CONV_CH), F32)],
        scratch_shapes=[pltpu.VMEM((TM + CONV_HALO, CONV_CH), F32), pltpu.VMEM((TM + CONV_HALO, CONV_CH), F32)],
        compiler_params=_params(1),
    )(dh1, dh1, h0, h0, proj, proj, cw)


def _mixer_in_bwd(x, dx2, proj, dq, dk, dv, dag, dct, drb, g1, win, qg, kg, bf):
    S = x.shape[0]
    TM = min(256, S)
    nb = S // TM

    def body(x_ref, dx2_ref, qr_ref, kr_ref, fz_ref, dq_ref, dk_ref, dv_ref, dag_ref, dct_ref, drb_ref,
             g1_ref, win_ref, qg_ref, kg_ref, bf_ref, bd_ref, fold_ref, triu_ref, pick_ref,
             dproj_ref, dx_ref, dg1_ref, sm_ref, carry_ref, gsum_ref):
        i = pl.program_id(0)

        @pl.when(i == 0)
        def _():
            carry_ref[...] = jnp.zeros_like(carry_ref)
            gsum_ref[...] = jnp.zeros_like(gsum_ref)
            dg1_ref[...] = jnp.zeros_like(dg1_ref)
            sm_ref[...] = jnp.zeros_like(sm_ref)

        def headnorm_bwd(raw, dy, gain, scale, row):
            rs = lax.rsqrt(_dot_hi_r(raw * raw, bd_ref[...]) * (1.0 / HEAD_DIM) + EPS)
            n = raw * rs
            gsum_ref[row:row + 1, :] += jnp.sum(dy * n, axis=0, keepdims=True) * scale
            dn = dy * (gain * scale)
            return rs * (dn - n * (_dot_hi_r(dn * n, bd_ref[...]) * (1.0 / HEAD_DIM)))

        dproj_ref[:, O_Q:O_Q + ATT_W] = headnorm_bwd(qr_ref[...], dq_ref[...], qg_ref[...], QK_SCALE, 0).astype(BF16)
        dproj_ref[:, O_K:O_K + ATT_W] = headnorm_bwd(kr_ref[...], dk_ref[...], kg_ref[...], 1.0, 1).astype(BF16)
        dproj_ref[:, O_V:O_V + ATT_W] = dv_ref[...].astype(BF16)
        dproj_ref[:, O_A:O_A + 2 * CONV_CH] = dag_ref[...]

        dc8 = jnp.concatenate([dct_ref[...], jnp.zeros((128 - ATT_HEADS, TM), F32)], axis=0).T
        dc8 = dc8 + _dot_hi_r(drb_ref[...], pick_ref[...])
        dlogf = _dot_hi_l(triu_ref[...], dc8) + carry_ref[...]
        carry_ref[...] = dlogf[0:1, :]
        df = dlogf * _sigmoid(-(fz_ref[...] + bf_ref[...]))
        dproj_ref[:, O_F:O_F + 128] = df.astype(BF16)
        sm_ref[2:3, :] += jnp.sum(df, axis=0, keepdims=True)

        du1 = _dot_nt(dproj_ref[...], win_ref[...])
        xv = x_ref[...]
        r = lax.rsqrt(jnp.mean(xv * xv, axis=-1, keepdims=True) + EPS)
        n1 = xv * r
        t = du1 * g1_ref[...]
        dx_ref[...] = dx2_ref[...] + r * (t - n1 * jnp.mean(t * n1, axis=-1, keepdims=True))
        dg1_ref[0:1, :] += jnp.sum(du1 * n1, axis=0, keepdims=True)

        @pl.when(i == nb - 1)
        def _():
            sm_ref[0:2, :] = _dot_hi_r(gsum_ref[0:8, :], fold_ref[...])[0:2, :]

    rev = lambda w, cb=0: pl.BlockSpec((TM, w), lambda i: (nb - 1 - i, cb))
    full = lambda a: pl.BlockSpec(a.shape, lambda i: (0,) * a.ndim)
    bd, fold, triu = _head_blockdiag(), _head_fold(), _tril(TM).T
    consts = (g1, win, qg, kg, bf, bd, fold, triu, _head_pick())
    return pl.pallas_call(
        body, name="mixer_in_bwd", grid=(nb,),
        in_specs=[rev(D_MODEL), rev(D_MODEL), rev(ATT_W, O_Q // ATT_W), rev(ATT_W, O_K // ATT_W), rev(128, O_F // 128),
                  rev(ATT_W), rev(ATT_W), rev(ATT_W), rev(2 * CONV_CH),
                  pl.BlockSpec((ATT_HEADS, TM), lambda i: (0, nb - 1 - i)), rev(ATT_W)] + [full(a) for a in consts],
        out_specs=[rev(N_INP), rev(D_MODEL), pl.BlockSpec((8, D_MODEL), lambda i: (0, 0)),
                   pl.BlockSpec((8, 128), lambda i: (0, 0))],
        out_shape=[jax.ShapeDtypeStruct((S, N_INP), BF16), jax.ShapeDtypeStruct((S, D_MODEL), F32),
                   jax.ShapeDtypeStruct((8, D_MODEL), F32), jax.ShapeDtypeStruct((8, 128), F32)],
        scratch_shapes=[pltpu.VMEM((1, 128), F32), pltpu.VMEM((8, ATT_W), F32)],
        compiler_params=_params(1),
    )(x, dx2, proj, proj, proj, dq, dk, dv, dag, dct, drb, *consts)


def _layer_fwd(x, p):
    u1, proj, q, k, v, cb, ct, h0, h1, h3 = _mixer_in_fwd(
        x, p["g1"], p["win"], p["qg"], p["kg"], p["bf"], p["cw"], p["cvb"], p["lng"], p["lnb"])
    ct4 = ct.reshape(ATT_HEADS // 2, 2, ct.shape[1])
    att, lse = _attn_fwd(q, k, v, cb, ct4)
    x2, u2 = _mixer_out_fwd(x, att, h3, p["wo"], p["g2"])
    x3, z, hh = _mlp_fwd(x2, u2, p["w1"], p["w2"])
    saved = dict(x=x, u1=u1, proj=proj, q=q, k=k, v=v, cb=cb, ct4=ct4, h0=h0, h1=h1, h3=h3, att=att, lse=lse,
                 x2=x2, u2=u2, z=z, hh=hh)
    return x3, saved


def _layer_bwd(dx3, s, p):
    dz, dx2, dg2 = _mlp_bwd(dx3, s["z"], s["x2"], p["g2"], p["w1"], p["w2"])
    g_w2 = _matmul_tn(s["hh"], dx3)
    g_w1 = _matmul_tn(s["u2"], dz, col_shards=N_CHIPS)
    datt, dlt, dh1, sm_c = _mixer_out_bwd(dx2, p["wo"], s["att"], s["h1"], p["lng"], p["lnb"])
    g_wo = jnp.concatenate([_matmul_tn(s["att"], dx2)[0], _matmul_tn(s["h3"], dx2)[0]], axis=0)
    dag, dcw = _conv_glu_bwd(dh1, s["h0"], s["proj"], p["cw"])
    dq, dk, dv, dc4, drb = _attn_bwd(s["q"], s["k"], s["v"], datt, s["lse"], dlt, s["cb"], s["ct4"])
    dct = dc4.reshape(ATT_HEADS, dc4.shape[2])
    dproj, dx, dg1, sm_a = _mixer_in_bwd(s["x"], dx2, s["proj"], dq, dk, dv, dag, dct, drb,
                                         p["g1"], p["win"], p["qg"], p["kg"], p["bf"])
    g_win = _matmul_tn(s["u1"], dproj)[0]
    big = dict(win=g_win, wo=g_wo, w1=g_w1, w2=g_w2[0])
    small = dict(g1=dg1[0], g2=dg2[0], lng=sm_c[0], lnb=sm_c[1], cvb=sm_c[2], cw=dcw[0:CONV_TAPS],
                 qg=sm_a[0, 0:HEAD_DIM], kg=sm_a[1, 0:HEAD_DIM], bf=sm_a[2, 0:ATT_HEADS])
    return dx, big, small


def _local_step(x, target, layers):
    saved = []
    h = x
    for p in layers:
        h, s = _layer_fwd(h, p)
        saved.append(s)
    dy, loss_acc = _loss_fwd_bwd(h, target)
    loss = loss_acc[0, 0] * (0.5 / D_MODEL)
    bigs, smalls = [], []
    d = dy
    for p, s in zip(reversed(layers), reversed(saved)):
        d, big, small = _layer_bwd(d, s, p)
        bigs.append(big)
        smalls.append(small)
    return loss, d, bigs[::-1], smalls[::-1]


def _win_to_internal(w):
    pad = jnp.zeros(w.shape[:-1] + (N_INP - N_IN,), w.dtype)
    return jnp.concatenate([w[..., :1536], w[..., 1544:], w[..., 1536:1544], pad], axis=-1)


def _win_to_global(g):
    return jnp.concatenate([g[..., :1536], g[..., O_F:O_F + ATT_HEADS], g[..., 1536:O_F]], axis=-1)


def _layer_params(l, win, wo, w1, w2, cw, norm1_g, b_f, q_norm_g, k_norm_g, conv_b, conv_ln_g, conv_ln_b, norm2_g):
    row = lambda a: a.reshape(1, -1)
    return dict(
        win=win, wo=wo, w1=w1, w2=w2,
        cw=jnp.pad(cw, ((0, CONV_HALO - CONV_TAPS), (0, 0))),
        g1=row(norm1_g[l]), g2=row(norm2_g[l]),
        qg=row(jnp.tile(q_norm_g[l], ATT_HEADS)), kg=row(jnp.tile(k_norm_g[l], ATT_HEADS)),
        bf=row(jnp.pad(b_f[l], (0, 128 - ATT_HEADS))),
        cvb=row(conv_b[l]), lng=row(conv_ln_g[l]), lnb=row(conv_ln_b[l]))


def _place():
    x, y, c = lax.axis_index("x"), lax.axis_index("y"), lax.axis_index("c")
    chips = [(1 - x, y), (x, 1 - y), (1 - x, 1 - y)]
    return x, y, c, chips


def _gather_weights(shards):
    n = len(shards)

    def body(*refs):
        ins, outs = refs[:n], refs[n:2 * n]
        send_sems, recv_sems, local_sems = refs[2 * n:]
        x, y, c, chips = _place()
        me = 2 * x + y
        sibling = (x, y, 1 - c)

        def remote(k, src, dst, to):
            return pltpu.make_async_remote_copy(src_ref=src, dst_ref=dst, send_sem=send_sems.at[k],
                                                recv_sem=recv_sems.at[k], device_id=to, device_id_type=MESH)

        local = [pltpu.make_async_copy(ins[a], outs[a].at[me], local_sems.at[a]) for a in range(n)]
        for cp in local:
            cp.start()
        first = [remote(3 * a + j, ins[a].at[c], outs[a].at[me, c], (*chip, c))
                 for a in range(n) for j, chip in enumerate(chips)]
        for cp in first:
            cp.start()
        passed = []
        for a in range(n):
            for j, (px, py) in enumerate(chips):
                slot = outs[a].at[2 * px + py, c]
                remote(3 * a + j, slot, slot, sibling).wait_recv()
                fwd = remote(3 * n + 3 * a + j, slot, slot, sibling)
                fwd.start()
                passed.append(fwd)
        for a in range(n):
            for j, (px, py) in enumerate(chips):
                slot = outs[a].at[2 * px + py, 1 - c]
                remote(3 * n + 3 * a + j, slot, slot, sibling).wait_recv()
        for cp in first + passed:
            cp.wait_send()
        for cp in local:
            cp.wait()

    return pl.pallas_call(
        body, name="gather_weights",
        in_specs=[ANY] * n, out_specs=[ANY] * n,
        out_shape=[jax.ShapeDtypeStruct((N_CHIPS,) + s.shape, s.dtype) for s in shards],
        scratch_shapes=[pltpu.SemaphoreType.DMA((6 * n,)), pltpu.SemaphoreType.DMA((6 * n,)),
                        pltpu.SemaphoreType.DMA((n,))],
        compiler_params=pltpu.CompilerParams(has_side_effects=True),
    )(*shards)


def _pair_swap(g0, g1):
    n = len(g0)

    def body(*refs):
        r0, r1, outs = refs[:n], refs[n:2 * n], refs[2 * n:3 * n]
        send_sems, recv_sems = refs[3 * n:]
        x, y, c, _ = _place()
        sibling = (x, y, 1 - c)

        def remote(a, src):
            return pltpu.make_async_remote_copy(src_ref=src, dst_ref=outs[a], send_sem=send_sems.at[a],
                                                recv_sem=recv_sems.at[a], device_id=sibling, device_id_type=MESH)

        @pl.when(c == 0)
        def _():
            for a in range(n):
                remote(a, r1[a]).start()

        @pl.when(c == 1)
        def _():
            for a in range(n):
                remote(a, r0[a]).start()

        for a in range(n):
            remote(a, r0[a]).wait()

    return pl.pallas_call(
        body, name="pair_swap",
        in_specs=[ANY] * (2 * n), out_specs=[ANY] * n,
        out_shape=[jax.ShapeDtypeStruct(a.shape, a.dtype) for a in g0],
        scratch_shapes=[pltpu.SemaphoreType.DMA((n,)), pltpu.SemaphoreType.DMA((n,))],
        compiler_params=pltpu.CompilerParams(has_side_effects=True),
    )(*g0, *g1)


def _shard_tiles(arrays, steps):
    return [a.shape[-2] // steps for a in arrays]


def _pair_add(g0, g1, got):
    n = len(got)
    steps = 16
    tiles = _shard_tiles(got, steps)

    def body(*refs):
        r0, r1, rg, outs = refs[:n], refs[n:2 * n], refs[2 * n:3 * n], refs[3 * n:]
        c = lax.axis_index("c")

        @pl.when(c == 0)
        def _():
            for a in range(n):
                outs[a][...] = (r0[a][...] + rg[a][...]).astype(BF16)

        @pl.when(c == 1)
        def _():
            for a in range(n):
                outs[a][...] = (r1[a][...] + rg[a][...]).astype(BF16)

    specs = [pl.BlockSpec((N_CHIPS, t, a.shape[-1]), lambda i: (0, i, 0)) for a, t in zip(got, tiles)]
    return pl.pallas_call(
        body, name="pair_add", grid=(steps,),
        in_specs=specs * 3, out_specs=specs,
        out_shape=[jax.ShapeDtypeStruct(a.shape, BF16) for a in got],
        compiler_params=_params(1),
    )(*g0, *g1, *got)


def _chip_exchange(parts):
    n = len(parts)

    def body(*refs):
        ins, outs = refs[:n], refs[n:2 * n]
        send_sems, recv_sems, local_sems = refs[2 * n:]
        x, y, c, chips = _place()
        me = 2 * x + y

        def remote(a, j, px, py):
            return pltpu.make_async_remote_copy(src_ref=ins[a].at[2 * px + py], dst_ref=outs[a].at[me],
                                                send_sem=send_sems.at[3 * a + j], recv_sem=recv_sems.at[3 * a + j],
                                                device_id=(px, py, c), device_id_type=MESH)

        local = [pltpu.make_async_copy(ins[a].at[me], outs[a].at[me], local_sems.at[a]) for a in range(n)]
        for cp in local:
            cp.start()
        sends = [remote(a, j, px, py) for a in range(n) for j, (px, py) in enumerate(chips)]
        for cp in sends:
            cp.start()
        for a in range(n):
            for j, (px, py) in enumerate(chips):
                slot = outs[a].at[2 * px + py]
                pltpu.make_async_remote_copy(src_ref=slot, dst_ref=slot, send_sem=send_sems.at[3 * a + j],
                                             recv_sem=recv_sems.at[3 * a + j], device_id=(px, py, c),
                                             device_id_type=MESH).wait_recv()
        for cp in sends:
            cp.wait_send()
        for cp in local:
            cp.wait()

    return pl.pallas_call(
        body, name="chip_exchange",
        in_specs=[ANY] * n, out_specs=[ANY] * n,
        out_shape=[jax.ShapeDtypeStruct(a.shape, a.dtype) for a in parts],
        scratch_shapes=[pltpu.SemaphoreType.DMA((3 * n,)), pltpu.SemaphoreType.DMA((3 * n,)),
                        pltpu.SemaphoreType.DMA((n,))],
        compiler_params=pltpu.CompilerParams(has_side_effects=True),
    )(*parts)


def _chip_sum(parts):
    n = len(parts)
    steps = 16
    tiles = _shard_tiles(parts, steps)

    def body(*refs):
        ins, outs = refs[:n], refs[n:]
        for a in range(n):
            tot = ins[a][0].astype(F32)
            for k in range(1, N_CHIPS):
                tot = tot + ins[a][k].astype(F32)
            outs[a][...] = tot

    return pl.pallas_call(
        body, name="chip_sum", grid=(steps,),
        in_specs=[pl.BlockSpec((N_CHIPS, t, a.shape[-1]), lambda i: (0, i, 0)) for a, t in zip(parts, tiles)],
        out_specs=[pl.BlockSpec((t, a.shape[-1]), lambda i: (i, 0)) for a, t in zip(parts, tiles)],
        out_shape=[jax.ShapeDtypeStruct(a.shape[1:], F32) for a in parts],
        compiler_params=_params(1),
    )(*parts)


def _layer_swap(totals):
    n = len(totals)

    def body(*refs):
        ins, outs = refs[:n], refs[n:2 * n]
        send_sems, recv_sems, local_sems = refs[2 * n:]
        x, y, c, _ = _place()
        local = [pltpu.make_async_copy(ins[a], outs[a].at[c], local_sems.at[a]) for a in range(n)]
        sends = [pltpu.make_async_remote_copy(src_ref=ins[a], dst_ref=outs[a].at[c], send_sem=send_sems.at[a],
                                              recv_sem=recv_sems.at[a], device_id=(x, y, 1 - c), device_id_type=MESH)
                 for a in range(n)]
        for cp in local + sends:
            cp.start()
        for a in range(n):
            slot = outs[a].at[1 - c]
            pltpu.make_async_remote_copy(src_ref=slot, dst_ref=slot, send_sem=send_sems.at[a], recv_sem=recv_sems.at[a],
                                         device_id=(x, y, 1 - c), device_id_type=MESH).wait_recv()
        for cp in sends:
            cp.wait_send()
        for cp in local:
            cp.wait()

    return pl.pallas_call(
        body, name="layer_swap",
        in_specs=[ANY] * n, out_specs=[ANY] * n,
        out_shape=[jax.ShapeDtypeStruct((2,) + a.shape, a.dtype) for a in totals],
        scratch_shapes=[pltpu.SemaphoreType.DMA((n,)), pltpu.SemaphoreType.DMA((n,)), pltpu.SemaphoreType.DMA((n,))],
        compiler_params=pltpu.CompilerParams(has_side_effects=True),
    )(*totals)


def _adamw_math(w, g, m, v):
    m = ADAM_B1 * m + (1.0 - ADAM_B1) * g
    v = ADAM_B2 * v + (1.0 - ADAM_B2) * (g * g)
    m_hat = m / (1.0 - ADAM_B1 ** ADAM_STEP)
    v_hat = v / (1.0 - ADAM_B2 ** ADAM_STEP)
    delta = -ADAM_LR * (m_hat / (jnp.sqrt(v_hat) + ADAM_EPS) + ADAM_WD * w)
    return delta, m, v


def _adamw(ws, gs, ms, vs):
    n = len(ws)
    steps = 16
    tiles = _shard_tiles(ws, steps)

    def body(*refs):
        w_r, g_r, m_r, v_r = refs[:n], refs[n:2 * n], refs[2 * n:3 * n], refs[3 * n:4 * n]
        d_o, m_o, v_o = refs[4 * n:5 * n], refs[5 * n:6 * n], refs[6 * n:]
        for a in range(n):
            d, m, v = _adamw_math(w_r[a][...], g_r[a][...], m_r[a][...], v_r[a][...])
            d_o[a][...] = d
            m_o[a][...] = m
            v_o[a][...] = v

    specs = [pl.BlockSpec((2, t, a.shape[-1]), lambda i: (0, i, 0)) for a, t in zip(ws, tiles)]
    outs = pl.pallas_call(
        body, name="adamw", grid=(steps,),
        in_specs=specs * 4, out_specs=specs * 3,
        out_shape=[jax.ShapeDtypeStruct(a.shape, F32) for a in ws] * 3,
        compiler_params=_params(1),
    )(*ws, *gs, *ms, *vs)
    return outs[:n], outs[n:2 * n], outs[2 * n:]


SMALL_W = 512


def _small_allreduce_adamw(g, w, m, v, cw_w, cw_m, cw_v, cw_row0):
    R = g.shape[0]
    n_l = cw_w.shape[0]

    def body(g_ref, w_ref, m_ref, v_ref, cww_ref, cwm_ref, cwv_ref,
             gs_ref, d_ref, mo_ref, vo_ref, cg_ref, cd_ref, cmo_ref, cvo_ref,
             slots_ref, send_sems, recv_sems):
        x, y, c, _ = _place()
        me = 4 * x + 2 * y + c
        slots_ref[me] = g_ref[...]
        sends = []
        for d in range(1, 8):
            px, py, pc = x ^ (d >> 2), y ^ ((d >> 1) & 1), c ^ (d & 1)
            cp = pltpu.make_async_remote_copy(src_ref=g_ref, dst_ref=slots_ref.at[me], send_sem=send_sems.at[d - 1],
                                              recv_sem=recv_sems.at[d - 1], device_id=(px, py, pc), device_id_type=MESH)
            cp.start()
            sends.append(cp)
        for d in range(1, 8):
            px, py, pc = x ^ (d >> 2), y ^ ((d >> 1) & 1), c ^ (d & 1)
            slot = slots_ref.at[4 * px + 2 * py + pc]
            pltpu.make_async_remote_copy(src_ref=slot, dst_ref=slot, send_sem=send_sems.at[d - 1],
                                         recv_sem=recv_sems.at[d - 1], device_id=(px, py, pc),
                                         device_id_type=MESH).wait_recv()
        for cp in sends:
            cp.wait_send()
        tot = slots_ref[0]
        for k in range(1, 8):
            tot = tot + slots_ref[k]
        gs_ref[...] = tot
        dl, mn, vn = _adamw_math(w_ref[...], tot, m_ref[...], v_ref[...])
        d_ref[...] = dl
        mo_ref[...] = mn
        vo_ref[...] = vn
        chip = 2 * x + y
        for l in range(n_l):
            rows = tot[cw_row0[l]:cw_row0[l] + CONV_HALO, :]
            mine = rows[:, 0:128]
            for k in range(1, N_CHIPS):
                mine = jnp.where(chip == k, rows[:, 128 * k:128 * (k + 1)], mine)
            cg_ref[l] = mine
            dl, mn, vn = _adamw_math(cww_ref[l], mine, cwm_ref[l], cwv_ref[l])
            cd_ref[l] = dl
            cmo_ref[l] = mn
            cvo_ref[l] = vn

    vm = pl.BlockSpec(memory_space=pltpu.VMEM)
    small = jax.ShapeDtypeStruct((R, SMALL_W), F32)
    conv = jax.ShapeDtypeStruct(cw_w.shape, F32)
    return pl.pallas_call(
        body, name="small_allreduce_adamw",
        in_specs=[vm] * 7, out_specs=[vm] * 8,
        out_shape=[small] * 4 + [conv] * 4,
        scratch_shapes=[pltpu.VMEM((8, R, SMALL_W), F32), pltpu.SemaphoreType.DMA((7,)), pltpu.SemaphoreType.DMA((7,))],
        compiler_params=pltpu.CompilerParams(has_side_effects=True, vmem_limit_bytes=VMEM_LIMIT),
    )(g, w, m, v, cw_w, cw_m, cw_v)


SMALL_LAYOUT = (("conv_w", CONV_HALO), ("norm1_g", 2), ("norm2_g", 2), ("conv_b", 1), ("conv_ln_g", 1),
                ("conv_ln_b", 1), ("q_norm_g", 1), ("k_norm_g", 1), ("b_f", 1))
SMALL_ROWS = sum(r for _, r in SMALL_LAYOUT)
SMALL_ROWS_PAD = 48


def _pack_small(per_layer):
    blocks = []
    for d in per_layer:
        rows = []
        for name, r in SMALL_LAYOUT:
            if name == "conv_w":
                a = d.get(name)
                a = jnp.zeros((r, SMALL_W), F32) if a is None else jnp.pad(a, ((0, r - a.shape[0]), (0, 0)))
            else:
                a = d[name].reshape(-1)
                a = jnp.pad(a, (0, r * SMALL_W - a.shape[0])).reshape(r, SMALL_W)
            rows.append(a)
        rows.append(jnp.zeros((SMALL_ROWS_PAD - SMALL_ROWS, SMALL_W), F32))
        blocks.append(jnp.concatenate(rows, axis=0))
    return jnp.concatenate(blocks, axis=0)


def _unpack_small(packed, name, size):
    n_l = packed.shape[0] // SMALL_ROWS_PAD
    row0 = 0
    for nm, r in SMALL_LAYOUT:
        if nm == name:
            break
        row0 += r
    out = [packed[l * SMALL_ROWS_PAD + row0:l * SMALL_ROWS_PAD + row0 + r].reshape(-1)[:size] for l in range(n_l)]
    return jnp.stack(out)


SMALL_SIZES = dict(norm1_g=D_MODEL, norm2_g=D_MODEL, conv_b=CONV_CH, conv_ln_g=CONV_CH, conv_ln_b=CONV_CH,
                   q_norm_g=HEAD_DIM, k_norm_g=HEAD_DIM, b_f=ATT_HEADS)
SMALL_KEYS = dict(norm1_g="g1", norm2_g="g2", conv_b="cvb", conv_ln_g="lng", conv_ln_b="lnb",
                  q_norm_g="qg", k_norm_g="kg", b_f="bf", conv_w="cw")
CONV_W_ROW0 = 0


def kernel(x, norm1_g, w_in, b_f, q_norm_g, k_norm_g, conv_w, conv_b, conv_ln_g, conv_ln_b, w_o, norm2_g, w_mlp_in, w_mlp_out, loss_target, m_norm1_g, m_w_in, m_b_f, m_q_norm_g, m_k_norm_g, m_conv_w, m_conv_b, m_conv_ln_g, m_conv_ln_b, m_w_o, m_norm2_g, m_w_mlp_in, m_w_mlp_out, v_norm1_g, v_w_in, v_b_f, v_q_norm_g, v_k_norm_g, v_conv_w, v_conv_b, v_conv_ln_g, v_conv_ln_b, v_w_o, v_norm2_g, v_w_mlp_in, v_w_mlp_out):
    n_l = w_in.shape[0]
    wide = w_in.shape[2]

    gw_in, gw_o, gw_1, gw_2, g_cw = _gather_weights(
        [w_in.astype(BF16), w_o.astype(BF16), w_mlp_in.astype(BF16), w_mlp_out.astype(BF16), conv_w])
    layers = []
    for l in range(n_l):
        win = _win_to_internal(jnp.concatenate([gw_in[k, l] for k in range(N_CHIPS)], axis=-1))
        wo = gw_o[:, l].reshape(D_MODEL, D_MODEL)
        w1 = jnp.concatenate([gw_1[k, l] for k in range(N_CHIPS)], axis=-1)
        w2 = gw_2[:, l].reshape(D_FF, D_MODEL)
        cw = jnp.concatenate([g_cw[k, l] for k in range(N_CHIPS)], axis=-1)
        layers.append(_layer_params(l, win, wo, w1, w2, cw, norm1_g, b_f, q_norm_g, k_norm_g,
                                    conv_b, conv_ln_g, conv_ln_b, norm2_g))

    loss, dx, bigs, smalls = _local_step(x[0], loss_target[0], layers)
    loss = lax.psum(loss, ("x", "y", "c"))

    def shard_major(b):
        gin = _win_to_global(b["win"]).reshape(D_MODEL, N_CHIPS, wide).transpose(1, 0, 2)
        return [gin, b["wo"].reshape(N_CHIPS, D_MODEL // N_CHIPS, D_MODEL), b["w1"],
                b["w2"].reshape(N_CHIPS, D_FF // N_CHIPS, D_MODEL)]

    g0, g1 = shard_major(bigs[0]), shard_major(bigs[1])
    got = _pair_swap(g0, g1)
    parts = _pair_add(g0, g1, got)
    recv = _chip_exchange(parts)
    totals = _chip_sum(recv)
    g_big = _layer_swap(totals)
    big_w = [w_in, w_o, w_mlp_in, w_mlp_out]
    big_m = [m_w_in, m_w_o, m_w_mlp_in, m_w_mlp_out]
    big_v = [v_w_in, v_w_o, v_w_mlp_in, v_w_mlp_out]
    d_big, nm_big, nv_big = _adamw(big_w, g_big, big_m, big_v)

    env = dict(norm1_g=(norm1_g, m_norm1_g, v_norm1_g), norm2_g=(norm2_g, m_norm2_g, v_norm2_g),
               conv_b=(conv_b, m_conv_b, v_conv_b), conv_ln_g=(conv_ln_g, m_conv_ln_g, v_conv_ln_g),
               conv_ln_b=(conv_ln_b, m_conv_ln_b, v_conv_ln_b), q_norm_g=(q_norm_g, m_q_norm_g, v_q_norm_g),
               k_norm_g=(k_norm_g, m_k_norm_g, v_k_norm_g), b_f=(b_f, m_b_f, v_b_f))
    g_pack = _pack_small([{nm: s[key] for nm, key in SMALL_KEYS.items()} for s in smalls])
    packs = [_pack_small([{nm: env[nm][t][l] for nm in env} for l in range(n_l)]) for t in range(3)]
    pad_cw = lambda a: jnp.pad(a, ((0, 0), (0, CONV_HALO - CONV_TAPS), (0, 0)))
    cw_row0 = tuple(l * SMALL_ROWS_PAD + CONV_W_ROW0 for l in range(n_l))
    gs, ds, ms, vs, cg, cd, cm, cv = _small_allreduce_adamw(
        g_pack, packs[0], packs[1], packs[2], pad_cw(conv_w), pad_cw(m_conv_w), pad_cw(v_conv_w), cw_row0)

    def small_out(packed, conv):
        o = {nm: _unpack_small(packed, nm, sz) for nm, sz in SMALL_SIZES.items()}
        o["conv_w"] = conv[:, 0:CONV_TAPS, :]
        return o

    def ordered(small, big):
        return (small["norm1_g"], big[0], small["b_f"], small["q_norm_g"], small["k_norm_g"], small["conv_w"],
                small["conv_b"], small["conv_ln_g"], small["conv_ln_b"], big[1], small["norm2_g"], big[2], big[3])

    return (loss, dx[None],
            *ordered(small_out(gs, cg), g_big), *ordered(small_out(ds, cd), d_big),
            *ordered(small_out(ms, cm), nm_big), *ordered(small_out(vs, cv), nv_big))
```

```python
import functools

import jax
import jax.numpy as jnp
from jax import lax
from jax.experimental import pallas as pl
from jax.experimental.pallas import tpu as pltpu

F32 = jnp.float32
BF16 = jnp.bfloat16

D_MODEL = 1024
ATT_HEADS = 8
HEAD_DIM = 64
ATT_W = ATT_HEADS * HEAD_DIM
CONV_CH = 512
CONV_TAPS = 31
CONV_HALO = 32
D_FF = 4 * D_MODEL
N_IN = 3 * ATT_W + ATT_HEADS + 2 * CONV_CH
O_Q, O_K, O_V, O_A, O_G, O_F = 0, 512, 1024, 1536, 2048, 2560
N_INP = O_F + 128
EPS = 1e-6
QK_SCALE = 0.125

ADAM_LR = 0.001
ADAM_B1 = 0.9
ADAM_B2 = 0.999
ADAM_EPS = 1e-08
ADAM_WD = 0.01
ADAM_STEP = 10

N_CHIPS = 4
VMEM_LIMIT = 52 * 1024 * 1024
MESH = pl.DeviceIdType.MESH
ANY = pl.BlockSpec(memory_space=pl.ANY)


def _params(n_axes, **kw):
    return pltpu.CompilerParams(dimension_semantics=("arbitrary",) * n_axes,
                                vmem_limit_bytes=VMEM_LIMIT, **kw)


def _dot(a, b):
    return jnp.dot(a, b, preferred_element_type=F32)


def _dot_nt(a, b):
    return lax.dot_general(a, b, (((1,), (1,)), ((), ())), preferred_element_type=F32)


def _dot_tn(a, b):
    return lax.dot_general(a, b, (((0,), (0,)), ((), ())), preferred_element_type=F32)


def _split3(a):
    a1 = a.astype(BF16)
    r = a - a1.astype(F32)
    a2 = r.astype(BF16)
    a3 = (r - a2.astype(F32)).astype(BF16)
    return a1, a2, a3


def _dot_hi_r(a, b_exact):
    return sum(_dot(p, b_exact) for p in _split3(a))


def _dot_hi_l(a_exact, b):
    return sum(_dot(a_exact, p) for p in _split3(b))


def _sigmoid(x):
    return 1.0 / (1.0 + jnp.exp(-x))


def _head_blockdiag():
    i = jnp.arange(ATT_W) // HEAD_DIM
    return (i[:, None] == i[None, :]).astype(BF16)


def _head_expand():
    h = jnp.arange(128)[:, None]
    j = jnp.arange(ATT_W)[None, :] // HEAD_DIM
    return (h == j).astype(BF16)


def _head_fold():
    i = jnp.arange(ATT_W)[:, None] % HEAD_DIM
    j = jnp.arange(128)[None, :]
    return (i == j).astype(BF16)


def _head_pick():
    i = jnp.arange(ATT_W)[:, None]
    h = jnp.arange(128)[None, :]
    return (i == h * HEAD_DIM).astype(BF16)


def _tril(n):
    r = jnp.arange(n)
    return (r[:, None] >= r[None, :]).astype(BF16)


def _mixer_in_fwd(x, g1, win, qg, kg, bf, cw, cvb, lng, lnb):
    S = x.shape[0]
    TM = min(512, S)
    nb = S // TM

    def body(x_ref, g1_ref, win_ref, qg_ref, kg_ref, bf_ref, cw_ref, cvb_ref, lng_ref, lnb_ref,
             bd_ref, ex_ref, tri_ref,
             u1_ref, proj_ref, q_ref, k_ref, v_ref, cb_ref, ct_ref, h0_ref, h1_ref, h3_ref,
             carry_ref, hbuf_ref):
        i = pl.program_id(0)

        @pl.when(i == 0)
        def _():
            carry_ref[...] = jnp.zeros_like(carry_ref)
            hbuf_ref[0:CONV_HALO, :] = jnp.zeros((CONV_HALO, CONV_CH), F32)

        @pl.when(i > 0)
        def _():
            hbuf_ref[0:CONV_HALO, :] = hbuf_ref[TM:TM + CONV_HALO, :]

        xv = x_ref[...]
        r = lax.rsqrt(jnp.mean(xv * xv, axis=-1, keepdims=True) + EPS)
        u = (xv * r * g1_ref[...]).astype(BF16)
        u1_ref[...] = u
        proj_ref[...] = _dot(u, win_ref[...])

        def headnorm(raw, gain):
            ss = _dot_hi_r(raw * raw, bd_ref[...]) * (1.0 / HEAD_DIM)
            return raw * lax.rsqrt(ss + EPS) * gain

        q_ref[...] = (headnorm(proj_ref[:, O_Q:O_Q + ATT_W], qg_ref[...]) * QK_SCALE).astype(BF16)
        k_ref[...] = headnorm(proj_ref[:, O_K:O_K + ATT_W], kg_ref[...]).astype(BF16)
        v_ref[...] = proj_ref[:, O_V:O_V + ATT_W].astype(BF16)

        zf = proj_ref[:, O_F:O_F + 128] + bf_ref[...]
        logf = jnp.minimum(zf, 0.0) - jnp.log(1.0 + jnp.exp(-jnp.abs(zf)))
        lane = lax.broadcasted_iota(jnp.int32, (TM, 128), 1)
        logf = jnp.where(lane < ATT_HEADS, logf, 0.0)
        c8 = _dot_hi_l(tri_ref[...], logf) + carry_ref[...]
        carry_ref[...] = c8[TM - 1:TM, :]
        cb_ref[...] = _dot_hi_r(c8, ex_ref[...])
        ct_ref[...] = c8.T[0:ATT_HEADS, :]

        h0 = proj_ref[:, O_A:O_A + CONV_CH] * _sigmoid(proj_ref[:, O_G:O_G + CONV_CH])
        h0_ref[...] = h0
        hbuf_ref[CONV_HALO:CONV_HALO + TM, :] = h0
        acc = jnp.zeros((TM, CONV_CH), F32) + cvb_ref[...]
        for j in range(CONV_TAPS):
            acc = acc + cw_ref[j:j + 1, :] * hbuf_ref[pl.ds(CONV_HALO - CONV_TAPS + 1 + j, TM), :]
        h1_ref[...] = acc
        mu = jnp.mean(acc, axis=-1, keepdims=True)
        d = acc - mu
        var = jnp.mean(d * d, axis=-1, keepdims=True)
        h2 = d * lax.rsqrt(var + EPS) * lng_ref[...] + lnb_ref[...]
        h3_ref[...] = (h2 * _sigmoid(h2)).astype(BF16)

    row = lambda w: pl.BlockSpec((TM, w), lambda i: (i, 0))
    full = lambda a: pl.BlockSpec(a.shape, lambda i: (0,) * a.ndim)
    bd, ex, tri = _head_blockdiag(), _head_expand(), _tril(TM)
    ins = (x, g1, win, qg, kg, bf, cw, cvb, lng, lnb, bd, ex, tri)
    return pl.pallas_call(
        body, name="mixer_in_fwd", grid=(nb,),
        in_specs=[row(D_MODEL)] + [full(a) for a in ins[1:]],
        out_specs=[row(D_MODEL), row(N_INP), row(ATT_W), row(ATT_W), row(ATT_W), row(ATT_W),
                   pl.BlockSpec((ATT_HEADS, TM), lambda i: (0, i)),
                   row(CONV_CH), row(CONV_CH), row(CONV_CH)],
        out_shape=[jax.ShapeDtypeStruct((S, D_MODEL), BF16),
                   jax.ShapeDtypeStruct((S, N_INP), F32),
                   jax.ShapeDtypeStruct((S, ATT_W), BF16),
                   jax.ShapeDtypeStruct((S, ATT_W), BF16),
                   jax.ShapeDtypeStruct((S, ATT_W), BF16),
                   jax.ShapeDtypeStruct((S, ATT_W), F32),
                   jax.ShapeDtypeStruct((ATT_HEADS, S), F32),
                   jax.ShapeDtypeStruct((S, CONV_CH), F32),
                   jax.ShapeDtypeStruct((S, CONV_CH), F32),
                   jax.ShapeDtypeStruct((S, CONV_CH), BF16)],
        scratch_shapes=[pltpu.VMEM((1, 128), F32), pltpu.VMEM((TM + CONV_HALO, CONV_CH), F32)],
        compiler_params=_params(1),
    )(*ins)


def _attn_fwd(q, k, v, cb, ct4):
    S = q.shape[0]
    T = min(512, S)
    nq = S // T

    def body(q_ref, k_ref, v_ref, cb_ref, ct_ref, o_ref, lse_ref):
        qi = pl.program_id(1)
        lo = lax.broadcasted_iota(jnp.int32, (T, 128), 1) < HEAD_DIM
        qv = q_ref[...]
        zero = jnp.zeros_like(qv)
        qm = (jnp.where(lo, qv, zero), jnp.where(lo, zero, qv))
        cbv = cb_ref[...]
        ctq = (cbv[:, 0:1], cbv[:, HEAD_DIM:HEAD_DIM + 1])
        tril = (lax.broadcasted_iota(jnp.int32, (T, T), 0) >= lax.broadcasted_iota(jnp.int32, (T, T), 1))

        def step(kj, carry, masked):
            off = pl.multiple_of(kj * T, T)
            kb = k_ref[pl.ds(off, T), :]
            vb = v_ref[pl.ds(off, T), :]
            new = []
            for h in range(2):
                m, l, acc = carry[3 * h:3 * h + 3]
                s = _dot_nt(qm[h], kb) + (ctq[h] - ct_ref[h:h + 1, pl.ds(off, T)])
                if masked:
                    s = jnp.where(tril, s, -1e30)
                m_new = jnp.maximum(m, jnp.max(s, axis=-1, keepdims=True))
                alpha = jnp.exp(m - m_new)
                p = jnp.exp(s - m_new)
                l = alpha * l + jnp.sum(p, axis=-1, keepdims=True)
                acc = alpha * acc + _dot(p.astype(BF16), vb)
                new += [m_new, l, acc]
            return tuple(new)

        init = (jnp.full((T, 1), -1e30, F32), jnp.zeros((T, 1), F32), jnp.zeros((T, 128), F32)) * 2
        carry = lax.fori_loop(0, qi, lambda kj, c: step(kj, c, False), init)
        m0, l0, a0, m1, l1, a1 = step(qi, carry, True)
        o_ref[...] = jnp.where(lo, a0 / l0, a1 / l1).astype(BF16)
        lse_ref[...] = jnp.where(lo, m0 + jnp.log(l0), m1 + jnp.log(l1))

    qblk = pl.BlockSpec((T, 128), lambda hp, qi: (qi, hp))
    seq = pl.BlockSpec((S, 128), lambda hp, qi: (0, hp))
    return pl.pallas_call(
        body, name="attn_fwd", grid=(ATT_HEADS // 2, nq),
        in_specs=[qblk, seq, seq, qblk, pl.BlockSpec((None, 2, S), lambda hp, qi: (hp, 0, 0))],
        out_specs=[qblk, qblk],
        out_shape=[jax.ShapeDtypeStruct((S, ATT_W), BF16), jax.ShapeDtypeStruct((S, ATT_W), F32)],
        compiler_params=_params(2),
    )(q, k, v, cb, ct4)


def _attn_bwd(q, k, v, do, lse, dlt, cb, ct4):
    S = q.shape[0]
    T = min(512, S)
    nq = S // T

    def body(q_ref, do_ref, lse_ref, dlt_ref, cb_ref, k_ref, v_ref, ct_ref, dq_ref, dk_ref, dv_ref, dc_ref, dr_ref):
        kj = pl.program_id(1)

        @pl.when(kj == 0)
        def _():
            dq_ref[...] = jnp.zeros_like(dq_ref)
            dr_ref[...] = jnp.zeros_like(dr_ref)

        lo = lax.broadcasted_iota(jnp.int32, (T, 128), 1) < HEAD_DIM
        tril = (lax.broadcasted_iota(jnp.int32, (T, T), 0) >= lax.broadcasted_iota(jnp.int32, (T, T), 1))
        kb = k_ref[...]
        vb = v_ref[...]
        offk = pl.multiple_of(kj * T, T)
        cs = (ct_ref[0:1, pl.ds(offk, T)], ct_ref[1:2, pl.ds(offk, T)])

        def step(qi, carry, masked):
            off = pl.multiple_of(qi * T, T)
            qb = q_ref[pl.ds(off, T), :]
            dob = do_ref[pl.ds(off, T), :]
            lseb = lse_ref[pl.ds(off, T), :]
            dltb = dlt_ref[pl.ds(off, T), :]
            cbb = cb_ref[pl.ds(off, T), :]
            zero = jnp.zeros_like(qb)
            new, dqs, drs = [], [], []
            for h in range(2):
                dk_a, dv_a, dc_a = carry[3 * h:3 * h + 3]
                cl = HEAD_DIM * h
                qmh = jnp.where(lo, qb, zero) if h == 0 else jnp.where(lo, zero, qb)
                dom = jnp.where(lo, dob, zero) if h == 0 else jnp.where(lo, zero, dob)
                s = _dot_nt(qmh, kb) + (cbb[:, cl:cl + 1] - cs[h])
                if masked:
                    s = jnp.where(tril, s, -1e30)
                p = jnp.exp(s - lseb[:, cl:cl + 1])
                dp = _dot_nt(dom, vb)
                ds = p * (dp - dltb[:, cl:cl + 1])
                pb = p.astype(BF16)
                dsb = ds.astype(BF16)
                dv_a = dv_a + _dot_tn(pb, dob)
                dk_a = dk_a + _dot_tn(dsb, qb)
                dc_a = dc_a + jnp.sum(ds, axis=0, keepdims=True)
                dqs.append(_dot(dsb, kb))
                drs.append(jnp.sum(ds, axis=1, keepdims=True))
                new += [dk_a, dv_a, dc_a]
            dq_ref[pl.ds(off, T), :] += jnp.where(lo, dqs[0], dqs[1])
            dr_ref[pl.ds(off, T), :] += jnp.where(lo, drs[0], drs[1])
            return tuple(new)

        init = (jnp.zeros((T, 128), F32), jnp.zeros((T, 128), F32), jnp.zeros((1, T), F32)) * 2
        carry = step(kj, init, True)
        carry = lax.fori_loop(kj + 1, nq, lambda qi, c: step(qi, c, False), carry)
        dk_ref[...] = jnp.where(lo, carry[0], carry[3])
        dv_ref[...] = jnp.where(lo, carry[1], carry[4])
        dc_ref[0:1, :] = -carry[2]
        dc_ref[1:2, :] = -carry[5]

    seq = pl.BlockSpec((S, 128), lambda hp, kj: (0, hp))
    kblk = pl.BlockSpec((T, 128), lambda hp, kj: (kj, hp))
    return pl.pallas_call(
        body, name="attn_bwd", grid=(ATT_HEADS // 2, nq),
        in_specs=[seq, seq, seq, seq, seq, kblk, kblk,
                  pl.BlockSpec((None, 2, S), lambda hp, kj: (hp, 0, 0))],
        out_specs=[seq, kblk, kblk, pl.BlockSpec((None, 2, T), lambda hp, kj: (hp, 0, kj)), seq],
        out_shape=[jax.ShapeDtypeStruct((S, ATT_W), F32), jax.ShapeDtypeStruct((S, ATT_W), F32),
                   jax.ShapeDtypeStruct((S, ATT_W), F32), jax.ShapeDtypeStruct((ATT_HEADS // 2, 2, S), F32),
                   jax.ShapeDtypeStruct((S, ATT_W), F32)],
        compiler_params=_params(2),
    )(q, do, lse, dlt, cb, k, v, ct4)


def _mixer_out_fwd(x, att, h3, wo, g2):
    S = x.shape[0]
    TM = min(512, S)

    def body(x_ref, att_ref, h3_ref, wo_ref, g2_ref, x2_ref, u2_ref):
        x2 = x_ref[...] + _dot(att_ref[...], wo_ref[0:ATT_W, :]) + _dot(h3_ref[...], wo_ref[ATT_W:2 * ATT_W, :])
        x2_ref[...] = x2
        r = lax.rsqrt(jnp.mean(x2 * x2, axis=-1, keepdims=True) + EPS)
        u2_ref[...] = (x2 * r * g2_ref[...]).astype(BF16)

    row = lambda w: pl.BlockSpec((TM, w), lambda i: (i, 0))
    full = lambda a: pl.BlockSpec(a.shape, lambda i: (0,) * a.ndim)
    return pl.pallas_call(
        body, name="mixer_out_fwd", grid=(S // TM,),
        in_specs=[row(D_MODEL), row(ATT_W), row(CONV_CH), full(wo), full(g2)],
        out_specs=[row(D_MODEL), row(D_MODEL)],
        out_shape=[jax.ShapeDtypeStruct((S, D_MODEL), F32), jax.ShapeDtypeStruct((S, D_MODEL), BF16)],
        compiler_params=_params(1),
    )(x, att, h3, wo, g2)


def _mlp_fwd(x2, u2, w1, w2):
    S = x2.shape[0]
    TM = min(512, S)
    TF = 1024

    def body(x2_ref, u2_ref, w1_ref, w2_ref, x3_ref, z_ref, hh_ref):
        f = pl.program_id(1)

        @pl.when(f == 0)
        def _():
            x3_ref[...] = x2_ref[...]

        z = _dot(u2_ref[...], w1_ref[...])
        z_ref[...] = z
        zr = jnp.maximum(z, 0.0)
        hh = (zr * zr).astype(BF16)
        hh_ref[...] = hh
        x3_ref[...] += _dot(hh, w2_ref[...])

    return pl.pallas_call(
        body, name="mlp_fwd", grid=(S // TM, D_FF // TF),
        in_specs=[pl.BlockSpec((TM, D_MODEL), lambda i, f: (i, 0)), pl.BlockSpec((TM, D_MODEL), lambda i, f: (i, 0)),
                  pl.BlockSpec((D_MODEL, TF), lambda i, f: (0, f)), pl.BlockSpec((TF, D_MODEL), lambda i, f: (f, 0))],
        out_specs=[pl.BlockSpec((TM, D_MODEL), lambda i, f: (i, 0)), pl.BlockSpec((TM, TF), lambda i, f: (i, f)),
                   pl.BlockSpec((TM, TF), lambda i, f: (i, f))],
        out_shape=[jax.ShapeDtypeStruct((S, D_MODEL), F32), jax.ShapeDtypeStruct((S, D_FF), F32),
                   jax.ShapeDtypeStruct((S, D_FF), BF16)],
        compiler_params=_params(2),
    )(x2, u2, w1, w2)


def _loss_fwd_bwd(y, t):
    S = y.shape[0]
    TM = min(512, S)

    def body(y_ref, t_ref, dy_ref, loss_ref):
        @pl.when(pl.program_id(0) == 0)
        def _():
            loss_ref[...] = jnp.zeros_like(loss_ref)

        d = y_ref[...] - t_ref[...]
        dy_ref[...] = d * (1.0 / D_MODEL)
        loss_ref[...] += jnp.sum(d * d)

    row = pl.BlockSpec((TM, D_MODEL), lambda i: (i, 0))
    return pl.pallas_call(
        body, name="loss", grid=(S // TM,),
        in_specs=[row, row], out_specs=[row, pl.BlockSpec((8, 128), lambda i: (0, 0))],
        out_shape=[jax.ShapeDtypeStruct((S, D_MODEL), F32), jax.ShapeDtypeStruct((8, 128), F32)],
        compiler_params=_params(1),
    )(y, t)


def _mlp_bwd(dx3, z, x2, g2, w1, w2):
    S = dx3.shape[0]
    TM = min(512, S)
    TF = 1024
    nf = D_FF // TF

    def body(dx3_ref, z_ref, x2_ref, g2_ref, w1_ref, w2_ref, dz_ref, dx2_ref, dg2_ref, du2_ref):
        i = pl.program_id(0)
        f = pl.program_id(1)

        @pl.when((i == 0) & (f == 0))
        def _():
            dg2_ref[...] = jnp.zeros_like(dg2_ref)

        @pl.when(f == 0)
        def _():
            du2_ref[...] = jnp.zeros_like(du2_ref)

        dhh = _dot_nt(dx3_ref[...].astype(BF16), w2_ref[...])
        dz = (dhh * (2.0 * jnp.maximum(z_ref[...], 0.0))).astype(BF16)
        dz_ref[...] = dz
        du2_ref[...] += _dot_nt(dz, w1_ref[...])

        @pl.when(f == nf - 1)
        def _():
            x2 = x2_ref[...]
            r = lax.rsqrt(jnp.mean(x2 * x2, axis=-1, keepdims=True) + EPS)
            n = x2 * r
            du2 = du2_ref[...]
            t = du2 * g2_ref[...]
            dx2_ref[...] = dx3_ref[...] + r * (t - n * jnp.mean(t * n, axis=-1, keepdims=True))
            dg2_ref[0:1, :] += jnp.sum(du2 * n, axis=0, keepdims=True)

    rowi = pl.BlockSpec((TM, D_MODEL), lambda i, f: (i, 0))
    return pl.pallas_call(
        body, name="mlp_bwd", grid=(S // TM, nf),
        in_specs=[rowi, pl.BlockSpec((TM, TF), lambda i, f: (i, f)), rowi,
                  pl.BlockSpec((1, D_MODEL), lambda i, f: (0, 0)),
                  pl.BlockSpec((D_MODEL, TF), lambda i, f: (0, f)), pl.BlockSpec((TF, D_MODEL), lambda i, f: (f, 0))],
        out_specs=[pl.BlockSpec((TM, TF), lambda i, f: (i, f)), rowi, pl.BlockSpec((8, D_MODEL), lambda i, f: (0, 0))],
        out_shape=[jax.ShapeDtypeStruct((S, D_FF), BF16), jax.ShapeDtypeStruct((S, D_MODEL), F32),
                   jax.ShapeDtypeStruct((8, D_MODEL), F32)],
        scratch_shapes=[pltpu.VMEM((TM, D_MODEL), F32)],
        compiler_params=_params(2),
    )(dx3, z, x2, g2, w1, w2)


def _matmul_tn(a, b, col_shards=1):
    S, I = a.shape
    J = b.shape[1]
    TI = min(I, 1024)
    TJ = 1024 if J % 1024 == 0 else 896
    TS = min(S, 512)
    nk = S // TS
    per = J // col_shards // TJ

    def body(a_ref, b_ref, o_ref):
        @pl.when(pl.program_id(2) == 0)
        def _():
            o_ref[...] = jnp.zeros_like(o_ref)

        o_ref[...] += _dot_tn(a_ref[...].astype(BF16), b_ref[...].astype(BF16))

    return pl.pallas_call(
        body, name="matmul_tn", grid=(I // TI, J // TJ, nk),
        in_specs=[pl.BlockSpec((TS, TI), lambda i, j, k: (k, i)), pl.BlockSpec((TS, TJ), lambda i, j, k: (k, j))],
        out_specs=pl.BlockSpec((None, TI, TJ), lambda i, j, k: (j // per, i, j % per)),
        out_shape=jax.ShapeDtypeStruct((col_shards, I, J // col_shards), F32),
        compiler_params=_params(3),
    )(a, b)


def _mixer_out_bwd(dx2, wo, att, h1, lng, lnb):
    S = dx2.shape[0]
    TM = min(512, S)

    def body(dx2_ref, wo_ref, att_ref, h1_ref, lng_ref, lnb_ref, bd_ref, datt_ref, dlt_ref, dh1_ref, sm_ref):
        @pl.when(pl.program_id(0) == 0)
        def _():
            sm_ref[...] = jnp.zeros_like(sm_ref)

        dxb = dx2_ref[...].astype(BF16)
        datt = _dot_nt(dxb, wo_ref[0:ATT_W, :])
        datt_ref[...] = datt.astype(BF16)
        dlt_ref[...] = _dot_hi_r(datt * att_ref[...].astype(F32), bd_ref[...])
        dh3 = _dot_nt(dxb, wo_ref[ATT_W:2 * ATT_W, :])
        h1 = h1_ref[...]
        mu = jnp.mean(h1, axis=-1, keepdims=True)
        d = h1 - mu
        rstd = lax.rsqrt(jnp.mean(d * d, axis=-1, keepdims=True) + EPS)
        n = d * rstd
        h2 = n * lng_ref[...] + lnb_ref[...]
        sg = _sigmoid(h2)
        dh2 = dh3 * (sg * (1.0 + h2 * (1.0 - sg)))
        dn = dh2 * lng_ref[...]
        dh1 = rstd * (dn - jnp.mean(dn, axis=-1, keepdims=True) - n * jnp.mean(dn * n, axis=-1, keepdims=True))
        dh1_ref[...] = dh1
        sm_ref[0:1, :] += jnp.sum(dh2 * n, axis=0, keepdims=True)
        sm_ref[1:2, :] += jnp.sum(dh2, axis=0, keepdims=True)
        sm_ref[2:3, :] += jnp.sum(dh1, axis=0, keepdims=True)

    row = lambda w: pl.BlockSpec((TM, w), lambda i: (i, 0))
    full = lambda a: pl.BlockSpec(a.shape, lambda i: (0,) * a.ndim)
    bd = _head_blockdiag()
    return pl.pallas_call(
        body, name="mixer_out_bwd", grid=(S // TM,),
        in_specs=[row(D_MODEL), full(wo), row(ATT_W), row(CONV_CH), full(lng), full(lnb), full(bd)],
        out_specs=[row(ATT_W), row(ATT_W), row(CONV_CH), pl.BlockSpec((8, CONV_CH), lambda i: (0, 0))],
        out_shape=[jax.ShapeDtypeStruct((S, ATT_W), BF16), jax.ShapeDtypeStruct((S, ATT_W), F32),
                   jax.ShapeDtypeStruct((S, CONV_CH), F32), jax.ShapeDtypeStruct((8, CONV_CH), F32)],
        compiler_params=_params(1),
    )(dx2, wo, att, h1, lng, lnb, bd)


def _conv_glu_bwd(dh1, h0, proj, cw):
    S = dh1.shape[0]
    TM = min(256, S)
    nb = S // TM
    lead = CONV_HALO - CONV_TAPS + 1

    def body(dh1_ref, dnx_ref, h0_ref, hpv_ref, a_ref, g_ref, cw_ref, dag_ref, dcw_ref, dbuf_ref, hbuf_ref):
        i = pl.program_id(0)

        @pl.when(i == 0)
        def _():
            dcw_ref[...] = jnp.zeros_like(dcw_ref)

        dh1 = dh1_ref[...]
        dbuf_ref[0:TM, :] = dh1
        dbuf_ref[TM:TM + CONV_HALO, :] = jnp.where(i < nb - 1, dnx_ref[0:CONV_HALO, :], 0.0)
        hbuf_ref[0:CONV_HALO, :] = jnp.where(i > 0, hpv_ref[TM - CONV_HALO:TM, :], 0.0)
        hbuf_ref[CONV_HALO:CONV_HALO + TM, :] = h0_ref[...]
        dh0 = jnp.zeros((TM, CONV_CH), F32)
        for j in range(CONV_TAPS):
            dh0 = dh0 + cw_ref[j:j + 1, :] * dbuf_ref[pl.ds(CONV_TAPS - 1 - j, TM), :]
            dcw_ref[j:j + 1, :] += jnp.sum(dh1 * hbuf_ref[pl.ds(lead + j, TM), :], axis=0, keepdims=True)
        sg = _sigmoid(g_ref[...])
        dag_ref[:, 0:CONV_CH] = (dh0 * sg).astype(BF16)
        dag_ref[:, CONV_CH:2 * CONV_CH] = (dh0 * a_ref[...] * sg * (1.0 - sg)).astype(BF16)

    blk = lambda fn: pl.BlockSpec((TM, CONV_CH), fn)
    return pl.pallas_call(
        body, name="conv_glu_bwd", grid=(nb,),
        in_specs=[blk(lambda i: (i, 0)), blk(lambda i: (jnp.minimum(i + 1, nb - 1), 0)),
                  blk(lambda i: (i, 0)), blk(lambda i: (jnp.maximum(i - 1, 0), 0)),
                  blk(lambda i: (i, O_A // CONV_CH)), blk(lambda i: (i, O_G // CONV_CH)),
                  pl.BlockSpec(cw.shape, lambda i: (0, 0))],
        out_specs=[pl.BlockSpec((TM, 2 * CONV_CH), lambda i: (i, 0)), pl.BlockSpec((CONV_HALO, CONV_CH), lambda i: (0, 0))],
        out_shape=[jax.ShapeDtypeStruct((S, 2 * CONV_CH), BF16), jax.ShapeDtypeStruct((CONV_HALO, CONV_CH), F32)],
        scratch_shapes=[pltpu.VMEM((TM + CONV_HALO, CONV_CH), F32), pltpu.VMEM((TM + CONV_HALO, CONV_CH), F32)],
        compiler_params=_params(1),
    )(dh1, dh1, h0, h0, proj, proj, cw)


def _mixer_in_bwd(x, dx2, proj, dq, dk, dv, dag, dct, drb, g1, win, qg, kg, bf):
    S = x.shape[0]
    TM = min(256, S)
    nb = S // TM

    def body(x_ref, dx2_ref, qr_ref, kr_ref, fz_ref, dq_ref, dk_ref, dv_ref, dag_ref, dct_ref, drb_ref,
             g1_ref, win_ref, qg_ref, kg_ref, bf_ref, bd_ref, fold_ref, triu_ref, pick_ref,
             dproj_ref, dx_ref, dg1_ref, sm_ref, carry_ref, gsum_ref):
        i = pl.program_id(0)

        @pl.when(i == 0)
        def _():
            carry_ref[...] = jnp.zeros_like(carry_ref)
            gsum_ref[...] = jnp.zeros_like(gsum_ref)
            dg1_ref[...] = jnp.zeros_like(dg1_ref)
            sm_ref[...] = jnp.zeros_like(sm_ref)

        def headnorm_bwd(raw, dy, gain, scale, row):
            rs = lax.rsqrt(_dot_hi_r(raw * raw, bd_ref[...]) * (1.0 / HEAD_DIM) + EPS)
            n = raw * rs
            gsum_ref[row:row + 1, :] += jnp.sum(dy * n, axis=0, keepdims=True) * scale
            dn = dy * (gain * scale)
            return rs * (dn - n * (_dot_hi_r(dn * n, bd_ref[...]) * (1.0 / HEAD_DIM)))

        dproj_ref[:, O_Q:O_Q + ATT_W] = headnorm_bwd(qr_ref[...], dq_ref[...], qg_ref[...], QK_SCALE, 0).astype(BF16)
        dproj_ref[:, O_K:O_K + ATT_W] = headnorm_bwd(kr_ref[...], dk_ref[...], kg_ref[...], 1.0, 1).astype(BF16)
        dproj_ref[:, O_V:O_V + ATT_W] = dv_ref[...].astype(BF16)
        dproj_ref[:, O_A:O_A + 2 * CONV_CH] = dag_ref[...]

        dc8 = jnp.concatenate([dct_ref[...], jnp.zeros((128 - ATT_HEADS, TM), F32)], axis=0).T
        dc8 = dc8 + _dot_hi_r(drb_ref[...], pick_ref[...])
        dlogf = _dot_hi_l(triu_ref[...], dc8) + carry_ref[...]
        carry_ref[...] = dlogf[0:1, :]
        df = dlogf * _sigmoid(-(fz_ref[...] + bf_ref[...]))
        dproj_ref[:, O_F:O_F + 128] = df.astype(BF16)
        sm_ref[2:3, :] += jnp.sum(df, axis=0, keepdims=True)

        du1 = _dot_nt(dproj_ref[...], win_ref[...])
        xv = x_ref[...]
        r = lax.rsqrt(jnp.mean(xv * xv, axis=-1, keepdims=True) + EPS)
        n1 = xv * r
        t = du1 * g1_ref[...]
        dx_ref[...] = dx2_ref[...] + r * (t - n1 * jnp.mean(t * n1, axis=-1, keepdims=True))
        dg1_ref[0:1, :] += jnp.sum(du1 * n1, axis=0, keepdims=True)

        @pl.when(i == nb - 1)
        def _():
            sm_ref[0:2, :] = _dot_hi_r(gsum_ref[0:8, :], fold_ref[...])[0:2, :]

    rev = lambda w, cb=0: pl.BlockSpec((TM, w), lambda i: (nb - 1 - i, cb))
    full = lambda a: pl.BlockSpec(a.shape, lambda i: (0,) * a.ndim)
    bd, fold, triu = _head_blockdiag(), _head_fold(), _tril(TM).T
    consts = (g1, win, qg, kg, bf, bd, fold, triu, _head_pick())
    return pl.pallas_call(
        body, name="mixer_in_bwd", grid=(nb,),
        in_specs=[rev(D_MODEL), rev(D_MODEL), rev(ATT_W, O_Q // ATT_W), rev(ATT_W, O_K // ATT_W), rev(128, O_F // 128),
                  rev(ATT_W), rev(ATT_W), rev(ATT_W), rev(2 * CONV_CH),
                  pl.BlockSpec((ATT_HEADS, TM), lambda i: (0, nb - 1 - i)), rev(ATT_W)] + [full(a) for a in consts],
        out_specs=[rev(N_INP), rev(D_MODEL), pl.BlockSpec((8, D_MODEL), lambda i: (0, 0)),
                   pl.BlockSpec((8, 128), lambda i: (0, 0))],
        out_shape=[jax.ShapeDtypeStruct((S, N_INP), BF16), jax.ShapeDtypeStruct((S, D_MODEL), F32),
                   jax.ShapeDtypeStruct((8, D_MODEL), F32), jax.ShapeDtypeStruct((8, 128), F32)],
        scratch_shapes=[pltpu.VMEM((1, 128), F32), pltpu.VMEM((8, ATT_W), F32)],
        compiler_params=_params(1),
    )(x, dx2, proj, proj, proj, dq, dk, dv, dag, dct, drb, *consts)


def _layer_fwd(x, p):
    u1, proj, q, k, v, cb, ct, h0, h1, h3 = _mixer_in_fwd(
        x, p["g1"], p["win"], p["qg"], p["kg"], p["bf"], p["cw"], p["cvb"], p["lng"], p["lnb"])
    ct4 = ct.reshape(ATT_HEADS // 2, 2, ct.shape[1])
    att, lse = _attn_fwd(q, k, v, cb, ct4)
    x2, u2 = _mixer_out_fwd(x, att, h3, p["wo"], p["g2"])
    x3, z, hh = _mlp_fwd(x2, u2, p["w1"], p["w2"])
    saved = dict(x=x, u1=u1, proj=proj, q=q, k=k, v=v, cb=cb, ct4=ct4, h0=h0, h1=h1, h3=h3, att=att, lse=lse,
                 x2=x2, u2=u2, z=z, hh=hh)
    return x3, saved


def _layer_bwd(dx3, s, p):
    dz, dx2, dg2 = _mlp_bwd(dx3, s["z"], s["x2"], p["g2"], p["w1"], p["w2"])
    g_w2 = _matmul_tn(s["hh"], dx3)
    g_w1 = _matmul_tn(s["u2"], dz, col_shards=N_CHIPS)
    datt, dlt, dh1, sm_c = _mixer_out_bwd(dx2, p["wo"], s["att"], s["h1"], p["lng"], p["lnb"])
    g_wo = jnp.concatenate([_matmul_tn(s["att"], dx2)[0], _matmul_tn(s["h3"], dx2)[0]], axis=0)
    dag, dcw = _conv_glu_bwd(dh1, s["h0"], s["proj"], p["cw"])
    dq, dk, dv, dc4, drb = _attn_bwd(s["q"], s["k"], s["v"], datt, s["lse"], dlt, s["cb"], s["ct4"])
    dct = dc4.reshape(ATT_HEADS, dc4.shape[2])
    dproj, dx, dg1, sm_a = _mixer_in_bwd(s["x"], dx2, s["proj"], dq, dk, dv, dag, dct, drb,
                                         p["g1"], p["win"], p["qg"], p["kg"], p["bf"])
    g_win = _matmul_tn(s["u1"], dproj)[0]
    big = dict(win=g_win, wo=g_wo, w1=g_w1, w2=g_w2[0])
    small = dict(g1=dg1[0], g2=dg2[0], lng=sm_c[0], lnb=sm_c[1], cvb=sm_c[2], cw=dcw[0:CONV_TAPS],
                 qg=sm_a[0, 0:HEAD_DIM], kg=sm_a[1, 0:HEAD_DIM], bf=sm_a[2, 0:ATT_HEADS])
    return dx, big, small


def _local_step(x, target, layers):
    saved = []
    h = x
    for p in layers:
        h, s = _layer_fwd(h, p)
        saved.append(s)
    dy, loss_acc = _loss_fwd_bwd(h, target)
    loss = loss_acc[0, 0] * (0.5 / D_MODEL)
    bigs, smalls = [], []
    d = dy
    for p, s in zip(reversed(layers), reversed(saved)):
        d, big, small = _layer_bwd(d, s, p)
        bigs.append(big)
        smalls.append(small)
    return loss, d, bigs[::-1], smalls[::-1]


def _win_to_internal(w):
    pad = jnp.zeros(w.shape[:-1] + (N_INP - N_IN,), w.dtype)
    return jnp.concatenate([w[..., :1536], w[..., 1544:], w[..., 1536:1544], pad], axis=-1)


def _win_to_global(g):
    return jnp.concatenate([g[..., :1536], g[..., O_F:O_F + ATT_HEADS], g[..., 1536:O_F]], axis=-1)


def _layer_params(l, win, wo, w1, w2, cw, norm1_g, b_f, q_norm_g, k_norm_g, conv_b, conv_ln_g, conv_ln_b, norm2_g):
    row = lambda a: a.reshape(1, -1)
    return dict(
        win=win, wo=wo, w1=w1, w2=w2,
        cw=jnp.pad(cw, ((0, CONV_HALO - CONV_TAPS), (0, 0))),
        g1=row(norm1_g[l]), g2=row(norm2_g[l]),
        qg=row(jnp.tile(q_norm_g[l], ATT_HEADS)), kg=row(jnp.tile(k_norm_g[l], ATT_HEADS)),
        bf=row(jnp.pad(b_f[l], (0, 128 - ATT_HEADS))),
        cvb=row(conv_b[l]), lng=row(conv_ln_g[l]), lnb=row(conv_ln_b[l]))


def _place():
    x, y, c = lax.axis_index("x"), lax.axis_index("y"), lax.axis_index("c")
    chips = [(1 - x, y), (x, 1 - y), (1 - x, 1 - y)]
    return x, y, c, chips


def _gather_weights(shards):
    n = len(shards)

    def body(*refs):
        ins, outs = refs[:n], refs[n:2 * n]
        send_sems, recv_sems = refs[2 * n:]
        x, y, c, chips = _place()
        me = 2 * x + y
        sibling = (x, y, 1 - c)

        def remote(k, src, dst, to):
            return pltpu.make_async_remote_copy(src_ref=src, dst_ref=dst, send_sem=send_sems.at[k],
                                                recv_sem=recv_sems.at[k], device_id=to, device_id_type=MESH)

        first = [remote(3 * a + j, ins[a].at[c], outs[a].at[me, c], (*chip, c))
                 for a in range(n) for j, chip in enumerate(chips)]
        for cp in first:
            cp.start()
        own = [remote(6 * n + a, ins[a], outs[a].at[me], sibling) for a in range(n)]
        for cp in own:
            cp.start()
        passed = []
        for a in range(n):
            for j, (px, py) in enumerate(chips):
                slot = outs[a].at[2 * px + py, c]
                remote(3 * a + j, slot, slot, sibling).wait_recv()
                fwd = remote(3 * n + 3 * a + j, slot, slot, sibling)
                fwd.start()
                passed.append(fwd)
        for a in range(n):
            for j, (px, py) in enumerate(chips):
                slot = outs[a].at[2 * px + py, 1 - c]
                remote(3 * n + 3 * a + j, slot, slot, sibling).wait_recv()
        for cp in own:
            cp.wait()
        for cp in first + passed:
            cp.wait_send()

    return pl.pallas_call(
        body, name="gather_weights",
        in_specs=[ANY] * n, out_specs=[ANY] * n,
        out_shape=[jax.ShapeDtypeStruct((N_CHIPS,) + s.shape, s.dtype) for s in shards],
        scratch_shapes=[pltpu.SemaphoreType.DMA((7 * n,)), pltpu.SemaphoreType.DMA((7 * n,))],
        compiler_params=pltpu.CompilerParams(has_side_effects=True),
    )(*shards)


def _pair_swap(g0, g1):
    n = len(g0)

    def body(*refs):
        r0, r1, outs = refs[:n], refs[n:2 * n], refs[2 * n:3 * n]
        send_sems, recv_sems = refs[3 * n:]
        x, y, c, _ = _place()
        sibling = (x, y, 1 - c)

        def remote(a, src):
            return pltpu.make_async_remote_copy(src_ref=src, dst_ref=outs[a], send_sem=send_sems.at[a],
                                                recv_sem=recv_sems.at[a], device_id=sibling, device_id_type=MESH)

        @pl.when(c == 0)
        def _():
            for a in range(n):
                remote(a, r1[a]).start()

        @pl.when(c == 1)
        def _():
            for a in range(n):
                remote(a, r0[a]).start()

        for a in range(n):
            remote(a, r0[a]).wait()

    return pl.pallas_call(
        body, name="pair_swap",
        in_specs=[ANY] * (2 * n), out_specs=[ANY] * n,
        out_shape=[jax.ShapeDtypeStruct(a.shape, a.dtype) for a in g0],
        scratch_shapes=[pltpu.SemaphoreType.DMA((n,)), pltpu.SemaphoreType.DMA((n,))],
        compiler_params=pltpu.CompilerParams(has_side_effects=True),
    )(*g0, *g1)


def _shard_tiles(arrays, steps):
    return [a.shape[-2] // steps for a in arrays]


def _pair_add(g0, g1, got):
    n = len(got)
    steps = 16
    tiles = _shard_tiles(got, steps)

    def body(*refs):
        r0, r1, rg, outs = refs[:n], refs[n:2 * n], refs[2 * n:3 * n], refs[3 * n:]
        c = lax.axis_index("c")

        @pl.when(c == 0)
        def _():
            for a in range(n):
                outs[a][...] = (r0[a][...] + rg[a][...]).astype(BF16)

        @pl.when(c == 1)
        def _():
            for a in range(n):
                outs[a][...] = (r1[a][...] + rg[a][...]).astype(BF16)

    specs = [pl.BlockSpec((N_CHIPS, t, a.shape[-1]), lambda i: (0, i, 0)) for a, t in zip(got, tiles)]
    return pl.pallas_call(
        body, name="pair_add", grid=(steps,),
        in_specs=specs * 3, out_specs=specs,
        out_shape=[jax.ShapeDtypeStruct(a.shape, BF16) for a in got],
        compiler_params=_params(1),
    )(*g0, *g1, *got)


def _chip_exchange(parts):
    n = len(parts)

    def body(*refs):
        ins, outs = refs[:n], refs[n:2 * n]
        send_sems, recv_sems = refs[2 * n:]
        x, y, c, chips = _place()

        def remote(a, j, px, py):
            return pltpu.make_async_remote_copy(src_ref=ins[a].at[2 * px + py], dst_ref=outs[a].at[j],
                                                send_sem=send_sems.at[3 * a + j], recv_sem=recv_sems.at[3 * a + j],
                                                device_id=(px, py, c), device_id_type=MESH)

        sends = [remote(a, j, px, py) for a in range(n) for j, (px, py) in enumerate(chips)]
        for cp in sends:
            cp.start()
        for cp in sends:
            cp.wait_recv()
        for cp in sends:
            cp.wait_send()

    return pl.pallas_call(
        body, name="chip_exchange",
        in_specs=[ANY] * n, out_specs=[ANY] * n,
        out_shape=[jax.ShapeDtypeStruct((N_CHIPS - 1,) + a.shape[1:], a.dtype) for a in parts],
        scratch_shapes=[pltpu.SemaphoreType.DMA((3 * n,)), pltpu.SemaphoreType.DMA((3 * n,))],
        compiler_params=pltpu.CompilerParams(has_side_effects=True),
    )(*parts)


def _chip_sum(parts, recv, place):
    n = len(parts)
    steps = 16
    tiles = _shard_tiles(parts, steps)

    def body(place_ref, *refs):
        own, got, outs = refs[:n], refs[n:2 * n], refs[2 * n:]
        for a in range(n):
            tot = own[a][...].astype(F32)
            for j in range(N_CHIPS - 1):
                tot = tot + got[a][j].astype(F32)
            outs[a][...] = tot

    own_specs = [pl.BlockSpec((None, t, a.shape[-1]), lambda i, p: (p[0], i, 0)) for a, t in zip(parts, tiles)]
    got_specs = [pl.BlockSpec((N_CHIPS - 1, t, a.shape[-1]), lambda i, p: (0, i, 0)) for a, t in zip(parts, tiles)]
    out_specs = [pl.BlockSpec((None, t, a.shape[-1]), lambda i, p: (p[1], i, 0)) for a, t in zip(parts, tiles)]
    return pl.pallas_call(
        body, name="chip_sum",
        grid_spec=pltpu.PrefetchScalarGridSpec(num_scalar_prefetch=1, grid=(steps,),
                                               in_specs=own_specs + got_specs, out_specs=out_specs),
        out_shape=[jax.ShapeDtypeStruct((2,) + a.shape[1:], F32) for a in parts],
        compiler_params=_params(1),
    )(place, *parts, *recv)


def _layer_swap(halves):
    n = len(halves)

    def body(*refs):
        ins, outs = refs[:n], refs[n:2 * n]
        send_sems, recv_sems = refs[2 * n:]
        x, y, c, _ = _place()
        sends = [pltpu.make_async_remote_copy(src_ref=ins[a].at[c], dst_ref=outs[a].at[c], send_sem=send_sems.at[a],
                                              recv_sem=recv_sems.at[a], device_id=(x, y, 1 - c), device_id_type=MESH)
                 for a in range(n)]
        for cp in sends:
            cp.start()
        for a in range(n):
            slot = outs[a].at[1 - c]
            pltpu.make_async_remote_copy(src_ref=slot, dst_ref=slot, send_sem=send_sems.at[a], recv_sem=recv_sems.at[a],
                                         device_id=(x, y, 1 - c), device_id_type=MESH).wait_recv()
        for cp in sends:
            cp.wait_send()

    return pl.pallas_call(
        body, name="layer_swap",
        in_specs=[ANY] * n, out_specs=[ANY] * n,
        out_shape=[jax.ShapeDtypeStruct(a.shape, a.dtype) for a in halves],
        input_output_aliases={a: a for a in range(n)},
        scratch_shapes=[pltpu.SemaphoreType.DMA((n,)), pltpu.SemaphoreType.DMA((n,))],
        compiler_params=pltpu.CompilerParams(has_side_effects=True),
    )(*halves)


def _adamw_math(w, g, m, v):
    m = ADAM_B1 * m + (1.0 - ADAM_B1) * g
    v = ADAM_B2 * v + (1.0 - ADAM_B2) * (g * g)
    m_hat = m / (1.0 - ADAM_B1 ** ADAM_STEP)
    v_hat = v / (1.0 - ADAM_B2 ** ADAM_STEP)
    delta = -ADAM_LR * (m_hat / (jnp.sqrt(v_hat) + ADAM_EPS) + ADAM_WD * w)
    return delta, m, v


def _adamw(ws, gs, ms, vs):
    n = len(ws)
    steps = 16
    tiles = _shard_tiles(ws, steps)

    def body(*refs):
        w_r, g_r, m_r, v_r = refs[:n], refs[n:2 * n], refs[2 * n:3 * n], refs[3 * n:4 * n]
        d_o, m_o, v_o = refs[4 * n:5 * n], refs[5 * n:6 * n], refs[6 * n:]
        for a in range(n):
            d, m, v = _adamw_math(w_r[a][...], g_r[a][...], m_r[a][...], v_r[a][...])
            d_o[a][...] = d
            m_o[a][...] = m
            v_o[a][...] = v

    specs = [pl.BlockSpec((2, t, a.shape[-1]), lambda i: (0, i, 0)) for a, t in zip(ws, tiles)]
    outs = pl.pallas_call(
        body, name="adamw", grid=(steps,),
        in_specs=specs * 4, out_specs=specs * 3,
        out_shape=[jax.ShapeDtypeStruct(a.shape, F32) for a in ws] * 3,
        compiler_params=_params(1),
    )(*ws, *gs, *ms, *vs)
    return outs[:n], outs[n:2 * n], outs[2 * n:]


SMALL_W = 512


def _small_allreduce_adamw(g, w, m, v, cw_w, cw_m, cw_v, cw_row0):
    R = g.shape[0]
    n_l = cw_w.shape[0]

    def body(g_ref, w_ref, m_ref, v_ref, cww_ref, cwm_ref, cwv_ref,
             gs_ref, d_ref, mo_ref, vo_ref, cg_ref, cd_ref, cmo_ref, cvo_ref,
             slots_ref, send_sems, recv_sems):
        x, y, c, _ = _place()
        me = 4 * x + 2 * y + c
        slots_ref[me] = g_ref[...]
        sends = []
        for d in range(1, 8):
            px, py, pc = x ^ (d >> 2), y ^ ((d >> 1) & 1), c ^ (d & 1)
            cp = pltpu.make_async_remote_copy(src_ref=g_ref, dst_ref=slots_ref.at[me], send_sem=send_sems.at[d - 1],
                                              recv_sem=recv_sems.at[d - 1], device_id=(px, py, pc), device_id_type=MESH)
            cp.start()
            sends.append(cp)
        for d in range(1, 8):
            px, py, pc = x ^ (d >> 2), y ^ ((d >> 1) & 1), c ^ (d & 1)
            slot = slots_ref.at[4 * px + 2 * py + pc]
            pltpu.make_async_remote_copy(src_ref=slot, dst_ref=slot, send_sem=send_sems.at[d - 1],
                                         recv_sem=recv_sems.at[d - 1], device_id=(px, py, pc),
                                         device_id_type=MESH).wait_recv()
        for cp in sends:
            cp.wait_send()
        tot = slots_ref[0]
        for k in range(1, 8):
            tot = tot + slots_ref[k]
        gs_ref[...] = tot
        dl, mn, vn = _adamw_math(w_ref[...], tot, m_ref[...], v_ref[...])
        d_ref[...] = dl
        mo_ref[...] = mn
        vo_ref[...] = vn
        chip = 2 * x + y
        for l in range(n_l):
            rows = tot[cw_row0[l]:cw_row0[l] + CONV_HALO, :]
            mine = rows[:, 0:128]
            for k in range(1, N_CHIPS):
                mine = jnp.where(chip == k, rows[:, 128 * k:128 * (k + 1)], mine)
            cg_ref[l] = mine
            dl, mn, vn = _adamw_math(cww_ref[l], mine, cwm_ref[l], cwv_ref[l])
            cd_ref[l] = dl
            cmo_ref[l] = mn
            cvo_ref[l] = vn

    vm = pl.BlockSpec(memory_space=pltpu.VMEM)
    small = jax.ShapeDtypeStruct((R, SMALL_W), F32)
    conv = jax.ShapeDtypeStruct(cw_w.shape, F32)
    return pl.pallas_call(
        body, name="small_allreduce_adamw",
        in_specs=[vm] * 7, out_specs=[vm] * 8,
        out_shape=[small] * 4 + [conv] * 4,
        scratch_shapes=[pltpu.VMEM((8, R, SMALL_W), F32), pltpu.SemaphoreType.DMA((7,)), pltpu.SemaphoreType.DMA((7,))],
        compiler_params=pltpu.CompilerParams(has_side_effects=True, vmem_limit_bytes=VMEM_LIMIT),
    )(g, w, m, v, cw_w, cw_m, cw_v)


SMALL_LAYOUT = (("conv_w", CONV_HALO), ("norm1_g", 2), ("norm2_g", 2), ("conv_b", 1), ("conv_ln_g", 1),
                ("conv_ln_b", 1), ("q_norm_g", 1), ("k_norm_g", 1), ("b_f", 1))
SMALL_ROWS = sum(r for _, r in SMALL_LAYOUT)
SMALL_ROWS_PAD = 48


def _pack_small(per_layer):
    blocks = []
    for d in per_layer:
        rows = []
        for name, r in SMALL_LAYOUT:
            if name == "conv_w":
                a = d.get(name)
                a = jnp.zeros((r, SMALL_W), F32) if a is None else jnp.pad(a, ((0, r - a.shape[0]), (0, 0)))
            else:
                a = d[name].reshape(-1)
                a = jnp.pad(a, (0, r * SMALL_W - a.shape[0])).reshape(r, SMALL_W)
            rows.append(a)
        rows.append(jnp.zeros((SMALL_ROWS_PAD - SMALL_ROWS, SMALL_W), F32))
        blocks.append(jnp.concatenate(rows, axis=0))
    return jnp.concatenate(blocks, axis=0)


def _unpack_small(packed, name, size):
    n_l = packed.shape[0] // SMALL_ROWS_PAD
    row0 = 0
    for nm, r in SMALL_LAYOUT:
        if nm == name:
            break
        row0 += r
    out = [packed[l * SMALL_ROWS_PAD + row0:l * SMALL_ROWS_PAD + row0 + r].reshape(-1)[:size] for l in range(n_l)]
    return jnp.stack(out)


SMALL_SIZES = dict(norm1_g=D_MODEL, norm2_g=D_MODEL, conv_b=CONV_CH, conv_ln_g=CONV_CH, conv_ln_b=CONV_CH,
                   q_norm_g=HEAD_DIM, k_norm_g=HEAD_DIM, b_f=ATT_HEADS)
SMALL_KEYS = dict(norm1_g="g1", norm2_g="g2", conv_b="cvb", conv_ln_g="lng", conv_ln_b="lnb",
                  q_norm_g="qg", k_norm_g="kg", b_f="bf", conv_w="cw")
CONV_W_ROW0 = 0


def kernel(x, norm1_g, w_in, b_f, q_norm_g, k_norm_g, conv_w, conv_b, conv_ln_g, conv_ln_b, w_o, norm2_g, w_mlp_in, w_mlp_out, loss_target, m_norm1_g, m_w_in, m_b_f, m_q_norm_g, m_k_norm_g, m_conv_w, m_conv_b, m_conv_ln_g, m_conv_ln_b, m_w_o, m_norm2_g, m_w_mlp_in, m_w_mlp_out, v_norm1_g, v_w_in, v_b_f, v_q_norm_g, v_k_norm_g, v_conv_w, v_conv_b, v_conv_ln_g, v_conv_ln_b, v_w_o, v_norm2_g, v_w_mlp_in, v_w_mlp_out):
    n_l = w_in.shape[0]
    wide = w_in.shape[2]

    gw_in, gw_o, gw_1, gw_2, g_cw = _gather_weights(
        [w_in.astype(BF16), w_o.astype(BF16), w_mlp_in.astype(BF16), w_mlp_out.astype(BF16), conv_w])
    layers = []
    for l in range(n_l):
        win = _win_to_internal(jnp.concatenate([gw_in[k, l] for k in range(N_CHIPS)], axis=-1))
        wo = gw_o[:, l].reshape(D_MODEL, D_MODEL)
        w1 = jnp.concatenate([gw_1[k, l] for k in range(N_CHIPS)], axis=-1)
        w2 = gw_2[:, l].reshape(D_FF, D_MODEL)
        cw = jnp.concatenate([g_cw[k, l] for k in range(N_CHIPS)], axis=-1)
        layers.append(_layer_params(l, win, wo, w1, w2, cw, norm1_g, b_f, q_norm_g, k_norm_g,
                                    conv_b, conv_ln_g, conv_ln_b, norm2_g))

    loss, dx, bigs, smalls = _local_step(x[0], loss_target[0], layers)
    loss = lax.psum(loss, ("x", "y", "c"))

    def shard_major(b):
        gin = _win_to_global(b["win"]).reshape(D_MODEL, N_CHIPS, wide).transpose(1, 0, 2)
        return [gin, b["wo"].reshape(N_CHIPS, D_MODEL // N_CHIPS, D_MODEL), b["w1"],
                b["w2"].reshape(N_CHIPS, D_FF // N_CHIPS, D_MODEL)]

    g0, g1 = shard_major(bigs[0]), shard_major(bigs[1])
    got = _pair_swap(g0, g1)
    parts = _pair_add(g0, g1, got)
    recv = _chip_exchange(parts)
    place = jnp.stack([2 * lax.axis_index("x") + lax.axis_index("y"), lax.axis_index("c")]).astype(jnp.int32)
    g_big = _layer_swap(_chip_sum(parts, recv, place))
    big_w = [w_in, w_o, w_mlp_in, w_mlp_out]
    big_m = [m_w_in, m_w_o, m_w_mlp_in, m_w_mlp_out]
    big_v = [v_w_in, v_w_o, v_w_mlp_in, v_w_mlp_out]
    d_big, nm_big, nv_big = _adamw(big_w, g_big, big_m, big_v)

    env = dict(norm1_g=(norm1_g, m_norm1_g, v_norm1_g), norm2_g=(norm2_g, m_norm2_g, v_norm2_g),
               conv_b=(conv_b, m_conv_b, v_conv_b), conv_ln_g=(conv_ln_g, m_conv_ln_g, v_conv_ln_g),
               conv_ln_b=(conv_ln_b, m_conv_ln_b, v_conv_ln_b), q_norm_g=(q_norm_g, m_q_norm_g, v_q_norm_g),
               k_norm_g=(k_norm_g, m_k_norm_g, v_k_norm_g), b_f=(b_f, m_b_f, v_b_f))
    g_pack = _pack_small([{nm: s[key] for nm, key in SMALL_KEYS.items()} for s in smalls])
    packs = [_pack_small([{nm: env[nm][t][l] for nm in env} for l in range(n_l)]) for t in range(3)]
    pad_cw = lambda a: jnp.pad(a, ((0, 0), (0, CONV_HALO - CONV_TAPS), (0, 0)))
    cw_row0 = tuple(l * SMALL_ROWS_PAD + CONV_W_ROW0 for l in range(n_l))
    gs, ds, ms, vs, cg, cd, cm, cv = _small_allreduce_adamw(
        g_pack, packs[0], packs[1], packs[2], pad_cw(conv_w), pad_cw(m_conv_w), pad_cw(v_conv_w), cw_row0)

    def small_out(packed, conv):
        o = {nm: _unpack_small(packed, nm, sz) for nm, sz in SMALL_SIZES.items()}
        o["conv_w"] = conv[:, 0:CONV_TAPS, :]
        return o

    def ordered(small, big):
        return (small["norm1_g"], big[0], small["b_f"], small["q_norm_g"], small["k_norm_g"], small["conv_w"],
                small["conv_b"], small["conv_ln_g"], small["conv_ln_b"], big[1], small["norm2_g"], big[2], big[3])

    return (loss, dx[None],
            *ordered(small_out(gs, cg), g_big), *ordered(small_out(ds, cd), d_big),
            *ordered(small_out(ms, cm), nm_big), *ordered(small_out(vs, cv), nv_big))
```

```python
import functools

import jax
import jax.numpy as jnp
from jax import lax
from jax.experimental import pallas as pl
from jax.experimental.pallas import tpu as pltpu

F32 = jnp.float32
BF16 = jnp.bfloat16

D_MODEL = 1024
ATT_HEADS = 8
HEAD_DIM = 64
ATT_W = ATT_HEADS * HEAD_DIM
CONV_CH = 512
CONV_TAPS = 31
CONV_HALO = 32
D_FF = 4 * D_MODEL
N_IN = 3 * ATT_W + ATT_HEADS + 2 * CONV_CH
O_Q, O_K, O_V, O_A, O_G, O_F = 0, 512, 1024, 1536, 2048, 2560
N_INP = O_F + 128
EPS = 1e-6
QK_SCALE = 0.125

ADAM_LR = 0.001
ADAM_B1 = 0.9
ADAM_B2 = 0.999
ADAM_EPS = 1e-08
ADAM_WD = 0.01
ADAM_STEP = 10

N_CHIPS = 4
VMEM_LIMIT = 52 * 1024 * 1024
MESH = pl.DeviceIdType.MESH
ANY = pl.BlockSpec(memory_space=pl.ANY)


def _params(n_axes, **kw):
    return pltpu.CompilerParams(dimension_semantics=("arbitrary",) * n_axes,
                                vmem_limit_bytes=VMEM_LIMIT, **kw)


def _dot(a, b):
    return jnp.dot(a, b, preferred_element_type=F32)


def _dot_nt(a, b):
    return lax.dot_general(a, b, (((1,), (1,)), ((), ())), preferred_element_type=F32)


def _dot_tn(a, b):
    return lax.dot_general(a, b, (((0,), (0,)), ((), ())), preferred_element_type=F32)


def _split3(a):
    a1 = a.astype(BF16)
    r = a - a1.astype(F32)
    a2 = r.astype(BF16)
    a3 = (r - a2.astype(F32)).astype(BF16)
    return a1, a2, a3


def _dot_hi_r(a, b_exact):
    return sum(_dot(p, b_exact) for p in _split3(a))


def _dot_hi_l(a_exact, b):
    return sum(_dot(a_exact, p) for p in _split3(b))


def _sigmoid(x):
    return 1.0 / (1.0 + jnp.exp(-x))


def _head_blockdiag():
    i = jnp.arange(ATT_W) // HEAD_DIM
    return (i[:, None] == i[None, :]).astype(BF16)


AUG_LANES = 8


def _aug_place(first):
    piece = jnp.arange(3 * 128)[:, None] // 128
    h = jnp.arange(3 * 128)[:, None] % 128
    lane = jnp.arange(ATT_W)[None, :]
    return ((h < ATT_HEADS) & (lane == 128 * (h // 2) + AUG_LANES * (h % 2) + first + piece)).astype(BF16)


def _aug_ones(first):
    lane = jnp.arange(ATT_W) % 128
    pos = lane % AUG_LANES
    return ((lane < 2 * AUG_LANES) & (pos >= first) & (pos < first + 3)).astype(F32).reshape(1, ATT_W)


def _head_fold():
    i = jnp.arange(ATT_W)[:, None] % HEAD_DIM
    j = jnp.arange(128)[None, :]
    return (i == j).astype(BF16)


def _head_pick():
    i = jnp.arange(ATT_W)[:, None]
    h = jnp.arange(128)[None, :]
    return (i == h * HEAD_DIM).astype(BF16)


def _tril(n):
    r = jnp.arange(n)
    return (r[:, None] >= r[None, :]).astype(BF16)


def _mixer_in_fwd(x, g1, win, qg, kg, bf, cw, cvb, lng, lnb):
    S = x.shape[0]
    TM = min(512, S)
    nb = S // TM

    def body(x_ref, g1_ref, win_ref, qg_ref, kg_ref, bf_ref, cw_ref, cvb_ref, lng_ref, lnb_ref,
             bd_ref, tri_ref, pq_ref, pk_ref, oq_ref, ok_ref,
             u1_ref, proj_ref, q_ref, k_ref, v_ref, qa_ref, ka_ref, h0_ref, h1_ref, h3_ref,
             carry_ref, hbuf_ref):
        i = pl.program_id(0)

        @pl.when(i == 0)
        def _():
            carry_ref[...] = jnp.zeros_like(carry_ref)
            hbuf_ref[0:CONV_HALO, :] = jnp.zeros((CONV_HALO, CONV_CH), F32)

        @pl.when(i > 0)
        def _():
            hbuf_ref[0:CONV_HALO, :] = hbuf_ref[TM:TM + CONV_HALO, :]

        xv = x_ref[...]
        r = lax.rsqrt(jnp.mean(xv * xv, axis=-1, keepdims=True) + EPS)
        u = (xv * r * g1_ref[...]).astype(BF16)
        u1_ref[...] = u
        proj_ref[...] = _dot(u, win_ref[...])

        def headnorm(raw, gain):
            ss = _dot_hi_r(raw * raw, bd_ref[...]) * (1.0 / HEAD_DIM)
            return raw * lax.rsqrt(ss + EPS) * gain

        q_ref[...] = (headnorm(proj_ref[:, O_Q:O_Q + ATT_W], qg_ref[...]) * QK_SCALE).astype(BF16)
        k_ref[...] = headnorm(proj_ref[:, O_K:O_K + ATT_W], kg_ref[...]).astype(BF16)
        v_ref[...] = proj_ref[:, O_V:O_V + ATT_W].astype(BF16)

        zf = proj_ref[:, O_F:O_F + 128] + bf_ref[...]
        logf = jnp.minimum(zf, 0.0) - jnp.log(1.0 + jnp.exp(-jnp.abs(zf)))
        lane = lax.broadcasted_iota(jnp.int32, (TM, 128), 1)
        logf = jnp.where(lane < ATT_HEADS, logf, 0.0)
        c8 = _dot_hi_l(tri_ref[...], logf) + carry_ref[...]
        carry_ref[...] = c8[TM - 1:TM, :]
        pieces = jnp.concatenate(_split3(c8), axis=1)
        qa_ref[...] = (_dot(pieces, pq_ref[...]) + oq_ref[...]).astype(BF16)
        ka_ref[...] = (ok_ref[...] - _dot(pieces, pk_ref[...])).astype(BF16)

        h0 = proj_ref[:, O_A:O_A + CONV_CH] * _sigmoid(proj_ref[:, O_G:O_G + CONV_CH])
        h0_ref[...] = h0
        hbuf_ref[CONV_HALO:CONV_HALO + TM, :] = h0
        acc = jnp.zeros((TM, CONV_CH), F32) + cvb_ref[...]
        for j in range(CONV_TAPS):
            acc = acc + cw_ref[j:j + 1, :] * hbuf_ref[pl.ds(CONV_HALO - CONV_TAPS + 1 + j, TM), :]
        h1_ref[...] = acc
        mu = jnp.mean(acc, axis=-1, keepdims=True)
        d = acc - mu
        var = jnp.mean(d * d, axis=-1, keepdims=True)
        h2 = d * lax.rsqrt(var + EPS) * lng_ref[...] + lnb_ref[...]
        h3_ref[...] = (h2 * _sigmoid(h2)).astype(BF16)

    row = lambda w: pl.BlockSpec((TM, w), lambda i: (i, 0))
    full = lambda a: pl.BlockSpec(a.shape, lambda i: (0,) * a.ndim)
    ins = (x, g1, win, qg, kg, bf, cw, cvb, lng, lnb, _head_blockdiag(), _tril(TM),
           _aug_place(0), _aug_place(3), _aug_ones(3), _aug_ones(0))
    return pl.pallas_call(
        body, name="mixer_in_fwd", grid=(nb,),
        in_specs=[row(D_MODEL)] + [full(a) for a in ins[1:]],
        out_specs=[row(D_MODEL), row(N_INP), row(ATT_W), row(ATT_W), row(ATT_W), row(ATT_W), row(ATT_W),
                   row(CONV_CH), row(CONV_CH), row(CONV_CH)],
        out_shape=[jax.ShapeDtypeStruct((S, D_MODEL), BF16),
                   jax.ShapeDtypeStruct((S, N_INP), F32),
                   jax.ShapeDtypeStruct((S, ATT_W), BF16),
                   jax.ShapeDtypeStruct((S, ATT_W), BF16),
                   jax.ShapeDtypeStruct((S, ATT_W), BF16),
                   jax.ShapeDtypeStruct((S, ATT_W), BF16),
                   jax.ShapeDtypeStruct((S, ATT_W), BF16),
                   jax.ShapeDtypeStruct((S, CONV_CH), F32),
                   jax.ShapeDtypeStruct((S, CONV_CH), F32),
                   jax.ShapeDtypeStruct((S, CONV_CH), BF16)],
        scratch_shapes=[pltpu.VMEM((1, 128), F32), pltpu.VMEM((TM + CONV_HALO, CONV_CH), F32)],
        compiler_params=_params(1),
    )(*ins)


def _pair_heads(lo, alo, x, xa):
    z = jnp.zeros_like(x)
    return (jnp.concatenate([jnp.where(lo, x, z), jnp.where(alo, xa, z)], axis=1),
            jnp.concatenate([jnp.where(lo, z, x), jnp.where(alo, z, xa)], axis=1))


def _attn_fwd(q, qa, k, ka, v):
    S = q.shape[0]
    T = min(1024, S)
    nq = S // T

    def body(q_ref, qa_ref, k_ref, ka_ref, v_ref, o_ref, lse_ref):
        qi = pl.program_id(1)
        lane = lax.broadcasted_iota(jnp.int32, (T, 128), 1)
        lo = lane < HEAD_DIM
        qm = _pair_heads(lo, lane < AUG_LANES, q_ref[...], qa_ref[...])
        tril = (lax.broadcasted_iota(jnp.int32, (T, T), 0) >= lax.broadcasted_iota(jnp.int32, (T, T), 1))

        def step(kj, carry, masked):
            off = pl.multiple_of(kj * T, T)
            kb = jnp.concatenate([k_ref[pl.ds(off, T), :], ka_ref[pl.ds(off, T), :]], axis=1)
            vb = v_ref[pl.ds(off, T), :]
            new = []
            for h in range(2):
                m, l, acc = carry[3 * h:3 * h + 3]
                s = _dot_nt(qm[h], kb)
                if masked:
                    s = jnp.where(tril, s, -1e30)
                m_new = jnp.maximum(m, jnp.max(s, axis=-1, keepdims=True))
                alpha = jnp.exp(m - m_new)
                p = jnp.exp(s - m_new)
                l = alpha * l + jnp.sum(p, axis=-1, keepdims=True)
                acc = alpha * acc + _dot(p.astype(BF16), vb)
                new += [m_new, l, acc]
            return tuple(new)

        init = (jnp.full((T, 1), -1e30, F32), jnp.zeros((T, 1), F32), jnp.zeros((T, 128), F32)) * 2
        carry = lax.fori_loop(0, qi, lambda kj, c: step(kj, c, False), init)
        m0, l0, a0, m1, l1, a1 = step(qi, carry, True)
        o_ref[...] = jnp.where(lo, a0 / l0, a1 / l1).astype(BF16)
        lse_ref[...] = jnp.where(lo, m0 + jnp.log(l0), m1 + jnp.log(l1))

    qblk = pl.BlockSpec((T, 128), lambda hp, qi: (qi, hp))
    seq = pl.BlockSpec((S, 128), lambda hp, qi: (0, hp))
    return pl.pallas_call(
        body, name="attn_fwd", grid=(ATT_HEADS // 2, nq),
        in_specs=[qblk, qblk, seq, seq, seq],
        out_specs=[qblk, qblk],
        out_shape=[jax.ShapeDtypeStruct((S, ATT_W), BF16), jax.ShapeDtypeStruct((S, ATT_W), F32)],
        compiler_params=_params(2),
    )(q, qa, k, ka, v)


def _attn_bwd(q, qa, k, ka, v, do, lse, dlt):
    S = q.shape[0]
    T = min(512, S)
    nq = S // T

    def body(q_ref, qa_ref, do_ref, lse_ref, dlt_ref, k_ref, ka_ref, v_ref, dq_ref, dk_ref, dv_ref, dc_ref, dr_ref):
        kj = pl.program_id(1)

        @pl.when(kj == 0)
        def _():
            dq_ref[...] = jnp.zeros_like(dq_ref)
            dr_ref[...] = jnp.zeros_like(dr_ref)

        lane = lax.broadcasted_iota(jnp.int32, (T, 128), 1)
        lo = lane < HEAD_DIM
        alo = lane < AUG_LANES
        tril = (lax.broadcasted_iota(jnp.int32, (T, T), 0) >= lax.broadcasted_iota(jnp.int32, (T, T), 1))
        kb = k_ref[...]
        kaug = jnp.concatenate([kb, ka_ref[...]], axis=1)
        vb = v_ref[...]

        def step(qi, carry, masked):
            off = pl.multiple_of(qi * T, T)
            qb = q_ref[pl.ds(off, T), :]
            dob = do_ref[pl.ds(off, T), :]
            lseb = lse_ref[pl.ds(off, T), :]
            dltb = dlt_ref[pl.ds(off, T), :]
            qm = _pair_heads(lo, alo, qb, qa_ref[pl.ds(off, T), :])
            zero = jnp.zeros_like(qb)
            new, dqs, drs = [], [], []
            for h in range(2):
                dk_a, dv_a, dc_a = carry[3 * h:3 * h + 3]
                cl = HEAD_DIM * h
                dom = jnp.where(lo, dob, zero) if h == 0 else jnp.where(lo, zero, dob)
                s = _dot_nt(qm[h], kaug)
                if masked:
                    s = jnp.where(tril, s, -1e30)
                p = jnp.exp(s - lseb[:, cl:cl + 1])
                dp = _dot_nt(dom, vb)
                ds = p * (dp - dltb[:, cl:cl + 1])
                pb = p.astype(BF16)
                dsb = ds.astype(BF16)
                dv_a = dv_a + _dot_tn(pb, dob)
                dk_a = dk_a + _dot_tn(dsb, qb)
                dc_a = dc_a + jnp.sum(ds, axis=0, keepdims=True)
                dqs.append(_dot(dsb, kb))
                drs.append(jnp.sum(ds, axis=1, keepdims=True))
                new += [dk_a, dv_a, dc_a]
            dq_ref[pl.ds(off, T), :] += jnp.where(lo, dqs[0], dqs[1])
            dr_ref[pl.ds(off, T), :] += jnp.where(lo, drs[0], drs[1])
            return tuple(new)

        init = (jnp.zeros((T, 128), F32), jnp.zeros((T, 128), F32), jnp.zeros((1, T), F32)) * 2
        carry = step(kj, init, True)
        carry = lax.fori_loop(kj + 1, nq, lambda qi, c: step(qi, c, False), carry)
        dk_ref[...] = jnp.where(lo, carry[0], carry[3])
        dv_ref[...] = jnp.where(lo, carry[1], carry[4])
        dc_ref[0:1, :] = -carry[2]
        dc_ref[1:2, :] = -carry[5]

    seq = pl.BlockSpec((S, 128), lambda hp, kj: (0, hp))
    kblk = pl.BlockSpec((T, 128), lambda hp, kj: (kj, hp))
    return pl.pallas_call(
        body, name="attn_bwd", grid=(ATT_HEADS // 2, nq),
        in_specs=[seq, seq, seq, seq, seq, kblk, kblk, kblk],
        out_specs=[seq, kblk, kblk, pl.BlockSpec((None, 2, T), lambda hp, kj: (hp, 0, kj)), seq],
        out_shape=[jax.ShapeDtypeStruct((S, ATT_W), F32), jax.ShapeDtypeStruct((S, ATT_W), F32),
                   jax.ShapeDtypeStruct((S, ATT_W), F32), jax.ShapeDtypeStruct((ATT_HEADS // 2, 2, S), F32),
                   jax.ShapeDtypeStruct((S, ATT_W), F32)],
        compiler_params=_params(2),
    )(q, qa, do, lse, dlt, k, ka, v)


def _wo_spec(wo4, l):
    return pl.BlockSpec((N_CHIPS, None) + wo4.shape[2:], lambda i: (0, l, 0, 0))


def _wo_halves(wo_ref):
    half = N_CHIPS // 2
    return (wo_ref[0:half].reshape(ATT_W, D_MODEL), wo_ref[half:N_CHIPS].reshape(CONV_CH, D_MODEL))


def _mixer_out_fwd(x, att, h3, wo4, l, g2):
    S = x.shape[0]
    TM = min(512, S)

    def body(x_ref, att_ref, h3_ref, wo_ref, g2_ref, x2_ref, u2_ref):
        wa, wc = _wo_halves(wo_ref)
        x2 = x_ref[...] + _dot(att_ref[...], wa) + _dot(h3_ref[...], wc)
        x2_ref[...] = x2
        r = lax.rsqrt(jnp.mean(x2 * x2, axis=-1, keepdims=True) + EPS)
        u2_ref[...] = (x2 * r * g2_ref[...]).astype(BF16)

    row = lambda w: pl.BlockSpec((TM, w), lambda i: (i, 0))
    full = lambda a: pl.BlockSpec(a.shape, lambda i: (0,) * a.ndim)
    return pl.pallas_call(
        body, name="mixer_out_fwd", grid=(S // TM,),
        in_specs=[row(D_MODEL), row(ATT_W), row(CONV_CH), _wo_spec(wo4, l), full(g2)],
        out_specs=[row(D_MODEL), row(D_MODEL)],
        out_shape=[jax.ShapeDtypeStruct((S, D_MODEL), F32), jax.ShapeDtypeStruct((S, D_MODEL), BF16)],
        compiler_params=_params(1),
    )(x, att, h3, wo4, g2)


def _mlp_w_specs(l):
    return [pl.BlockSpec((None, None, D_MODEL, D_FF // N_CHIPS), lambda i, f: (f, l, 0, 0)),
            pl.BlockSpec((None, None, D_FF // N_CHIPS, D_MODEL), lambda i, f: (f, l, 0, 0))]


def _mlp_fwd(x2, u2, w1, w2, l):
    S = x2.shape[0]
    TM = min(512, S)
    TF = 1024

    def body(x2_ref, u2_ref, w1_ref, w2_ref, x3_ref, z_ref, hh_ref):
        f = pl.program_id(1)

        @pl.when(f == 0)
        def _():
            x3_ref[...] = x2_ref[...]

        z = _dot(u2_ref[...], w1_ref[...])
        z_ref[...] = z
        zr = jnp.maximum(z, 0.0)
        hh = (zr * zr).astype(BF16)
        hh_ref[...] = hh
        x3_ref[...] += _dot(hh, w2_ref[...])

    return pl.pallas_call(
        body, name="mlp_fwd", grid=(S // TM, D_FF // TF),
        in_specs=[pl.BlockSpec((TM, D_MODEL), lambda i, f: (i, 0)), pl.BlockSpec((TM, D_MODEL), lambda i, f: (i, 0))]
        + _mlp_w_specs(l),
        out_specs=[pl.BlockSpec((TM, D_MODEL), lambda i, f: (i, 0)), pl.BlockSpec((TM, TF), lambda i, f: (i, f)),
                   pl.BlockSpec((TM, TF), lambda i, f: (i, f))],
        out_shape=[jax.ShapeDtypeStruct((S, D_MODEL), F32), jax.ShapeDtypeStruct((S, D_FF), F32),
                   jax.ShapeDtypeStruct((S, D_FF), BF16)],
        compiler_params=_params(2),
    )(x2, u2, w1, w2)


def _loss_fwd_bwd(y, t):
    S = y.shape[0]
    TM = min(512, S)

    def body(y_ref, t_ref, dy_ref, loss_ref):
        @pl.when(pl.program_id(0) == 0)
        def _():
            loss_ref[...] = jnp.zeros_like(loss_ref)

        d = y_ref[...] - t_ref[...]
        dy_ref[...] = d * (1.0 / D_MODEL)
        loss_ref[...] += jnp.sum(d * d)

    row = pl.BlockSpec((TM, D_MODEL), lambda i: (i, 0))
    return pl.pallas_call(
        body, name="loss", grid=(S // TM,),
        in_specs=[row, row], out_specs=[row, pl.BlockSpec((8, 128), lambda i: (0, 0))],
        out_shape=[jax.ShapeDtypeStruct((S, D_MODEL), F32), jax.ShapeDtypeStruct((8, 128), F32)],
        compiler_params=_params(1),
    )(y, t)


def _mlp_bwd(dx3, z, x2, g2, w1, w2, l):
    S = dx3.shape[0]
    TM = min(512, S)
    TF = 1024
    nf = D_FF // TF

    def body(dx3_ref, z_ref, x2_ref, g2_ref, w1_ref, w2_ref, dz_ref, dx2_ref, dg2_ref, du2_ref):
        i = pl.program_id(0)
        f = pl.program_id(1)

        @pl.when((i == 0) & (f == 0))
        def _():
            dg2_ref[...] = jnp.zeros_like(dg2_ref)

        @pl.when(f == 0)
        def _():
            du2_ref[...] = jnp.zeros_like(du2_ref)

        dhh = _dot_nt(dx3_ref[...].astype(BF16), w2_ref[...])
        dz = (dhh * (2.0 * jnp.maximum(z_ref[...], 0.0))).astype(BF16)
        dz_ref[...] = dz
        du2_ref[...] += _dot_nt(dz, w1_ref[...])

        @pl.when(f == nf - 1)
        def _():
            x2 = x2_ref[...]
            r = lax.rsqrt(jnp.mean(x2 * x2, axis=-1, keepdims=True) + EPS)
            n = x2 * r
            du2 = du2_ref[...]
            t = du2 * g2_ref[...]
            dx2_ref[...] = dx3_ref[...] + r * (t - n * jnp.mean(t * n, axis=-1, keepdims=True))
            dg2_ref[0:1, :] += jnp.sum(du2 * n, axis=0, keepdims=True)

    rowi = pl.BlockSpec((TM, D_MODEL), lambda i, f: (i, 0))
    return pl.pallas_call(
        body, name="mlp_bwd", grid=(S // TM, nf),
        in_specs=[rowi, pl.BlockSpec((TM, TF), lambda i, f: (i, f)), rowi,
                  pl.BlockSpec((1, D_MODEL), lambda i, f: (0, 0))] + _mlp_w_specs(l),
        out_specs=[pl.BlockSpec((TM, TF), lambda i, f: (i, f)), rowi, pl.BlockSpec((8, D_MODEL), lambda i, f: (0, 0))],
        out_shape=[jax.ShapeDtypeStruct((S, D_FF), BF16), jax.ShapeDtypeStruct((S, D_MODEL), F32),
                   jax.ShapeDtypeStruct((8, D_MODEL), F32)],
        scratch_shapes=[pltpu.VMEM((TM, D_MODEL), F32)],
        compiler_params=_params(2),
    )(dx3, z, x2, g2, w1, w2)


def _matmul_tn(a, b, col_shards=1):
    S, I = a.shape
    J = b.shape[1]
    TI = min(I, 1024)
    TJ = 1024 if J % 1024 == 0 else 896
    TS = min(S, 512)
    nk = S // TS
    per = J // col_shards // TJ

    def body(a_ref, b_ref, o_ref):
        @pl.when(pl.program_id(2) == 0)
        def _():
            o_ref[...] = jnp.zeros_like(o_ref)

        o_ref[...] += _dot_tn(a_ref[...].astype(BF16), b_ref[...].astype(BF16))

    return pl.pallas_call(
        body, name="matmul_tn", grid=(I // TI, J // TJ, nk),
        in_specs=[pl.BlockSpec((TS, TI), lambda i, j, k: (k, i)), pl.BlockSpec((TS, TJ), lambda i, j, k: (k, j))],
        out_specs=pl.BlockSpec((None, TI, TJ), lambda i, j, k: (j // per, i, j % per)),
        out_shape=jax.ShapeDtypeStruct((col_shards, I, J // col_shards), F32),
        compiler_params=_params(3),
    )(a, b)


def _mixer_out_bwd(dx2, wo4, l, att, h1, lng, lnb):
    S = dx2.shape[0]
    TM = min(512, S)

    def body(dx2_ref, wo_ref, att_ref, h1_ref, lng_ref, lnb_ref, bd_ref, datt_ref, dlt_ref, dh1_ref, sm_ref):
        @pl.when(pl.program_id(0) == 0)
        def _():
            sm_ref[...] = jnp.zeros_like(sm_ref)

        dxb = dx2_ref[...].astype(BF16)
        wa, wc = _wo_halves(wo_ref)
        datt = _dot_nt(dxb, wa)
        datt_ref[...] = datt.astype(BF16)
        dlt_ref[...] = _dot_hi_r(datt * att_ref[...].astype(F32), bd_ref[...])
        dh3 = _dot_nt(dxb, wc)
        h1 = h1_ref[...]
        mu = jnp.mean(h1, axis=-1, keepdims=True)
        d = h1 - mu
        rstd = lax.rsqrt(jnp.mean(d * d, axis=-1, keepdims=True) + EPS)
        n = d * rstd
        h2 = n * lng_ref[...] + lnb_ref[...]
        sg = _sigmoid(h2)
        dh2 = dh3 * (sg * (1.0 + h2 * (1.0 - sg)))
        dn = dh2 * lng_ref[...]
        dh1 = rstd * (dn - jnp.mean(dn, axis=-1, keepdims=True) - n * jnp.mean(dn * n, axis=-1, keepdims=True))
        dh1_ref[...] = dh1
        sm_ref[0:1, :] += jnp.sum(dh2 * n, axis=0, keepdims=True)
        sm_ref[1:2, :] += jnp.sum(dh2, axis=0, keepdims=True)
        sm_ref[2:3, :] += jnp.sum(dh1, axis=0, keepdims=True)

    row = lambda w: pl.BlockSpec((TM, w), lambda i: (i, 0))
    full = lambda a: pl.BlockSpec(a.shape, lambda i: (0,) * a.ndim)
    bd = _head_blockdiag()
    return pl.pallas_call(
        body, name="mixer_out_bwd", grid=(S // TM,),
        in_specs=[row(D_MODEL), _wo_spec(wo4, l), row(ATT_W), row(CONV_CH), full(lng), full(lnb), full(bd)],
        out_specs=[row(ATT_W), row(ATT_W), row(CONV_CH), pl.BlockSpec((8, CONV_CH), lambda i: (0, 0))],
        out_shape=[jax.ShapeDtypeStruct((S, ATT_W), BF16), jax.ShapeDtypeStruct((S, ATT_W), F32),
                   jax.ShapeDtypeStruct((S, CONV_CH), F32), jax.ShapeDtypeStruct((8, CONV_CH), F32)],
        compiler_params=_params(1),
    )(dx2, wo4, att, h1, lng, lnb, bd)


def _conv_glu_bwd(dh1, h0, proj, cw):
    S = dh1.shape[0]
    TM = min(256, S)
    nb = S // TM
    lead = CONV_HALO - CONV_TAPS + 1

    def body(dh1_ref, dnx_ref, h0_ref, hpv_ref, a_ref, g_ref, cw_ref, dag_ref, dcw_ref, dbuf_ref, hbuf_ref):
        i = pl.program_id(0)

        @pl.when(i == 0)
        def _():
            dcw_ref[...] = jnp.zeros_like(dcw_ref)

        dh1 = dh1_ref[...]
        dbuf_ref[0:TM, :] = dh1
        dbuf_ref[TM:TM + CONV_HALO, :] = jnp.where(i < nb - 1, dnx_ref[0:CONV_HALO, :], 0.0)
        hbuf_ref[0:CONV_HALO, :] = jnp.where(i > 0, hpv_ref[TM - CONV_HALO:TM, :], 0.0)
        hbuf_ref[CONV_HALO:CONV_HALO + TM, :] = h0_ref[...]
        dh0 = jnp.zeros((TM, CONV_CH), F32)
        for j in range(CONV_TAPS):
            dh0 = dh0 + cw_ref[j:j + 1, :] * dbuf_ref[pl.ds(CONV_TAPS - 1 - j, TM), :]
            dcw_ref[j:j + 1, :] += jnp.sum(dh1 * hbuf_ref[pl.ds(lead + j, TM), :], axis=0, keepdims=True)
        sg = _sigmoid(g_ref[...])
        dag_ref[:, 0:CONV_CH] = (dh0 * sg).astype(BF16)
        dag_ref[:, CONV_CH:2 * CONV_CH] = (dh0 * a_ref[...] * sg * (1.0 - sg)).astype(BF16)

    blk = lambda fn: pl.BlockSpec((TM, CONV_CH), fn)
    return pl.pallas_call(
        body, name="conv_glu_bwd", grid=(nb,),
        in_specs=[blk(lambda i: (i, 0)), blk(lambda i: (jnp.minimum(i + 1, nb - 1), 0)),
                  blk(lambda i: (i, 0)), blk(lambda i: (jnp.maximum(i - 1, 0), 0)),
                  blk(lambda i: (i, O_A // CONV_CH)), blk(lambda i: (i, O_G // CONV_CH)),
                  pl.BlockSpec(cw.shape, lambda i: (0, 0))],
        out_specs=[pl.BlockSpec((TM, 2 * CONV_CH), lambda i: (i, 0)), pl.BlockSpec((CONV_HALO, CONV_CH), lambda i: (0, 0))],
        out_shape=[jax.ShapeDtypeStruct((S, 2 * CONV_CH), BF16), jax.ShapeDtypeStruct((CONV_HALO, CONV_CH), F32)],
        scratch_shapes=[pltpu.VMEM((TM + CONV_HALO, CONV_CH), F32), pltpu.VMEM((TM + CONV_HALO, CONV_CH), F32)],
        compiler_params=_params(1),
    )(dh1, dh1, h0, h0, proj, proj, cw)


def _mixer_in_bwd(x, dx2, proj, dq, dk, dv, dag, dct, drb, g1, win, qg, kg, bf):
    S = x.shape[0]
    TM = min(256, S)
    nb = S // TM

    def body(x_ref, dx2_ref, qr_ref, kr_ref, fz_ref, dq_ref, dk_ref, dv_ref, dag_ref, dct_ref, drb_ref,
             g1_ref, win_ref, qg_ref, kg_ref, bf_ref, bd_ref, fold_ref, triu_ref, pick_ref,
             dproj_ref, dx_ref, dg1_ref, sm_ref, carry_ref, gsum_ref):
        i = pl.program_id(0)

        @pl.when(i == 0)
        def _():
            carry_ref[...] = jnp.zeros_like(carry_ref)
            gsum_ref[...] = jnp.zeros_like(gsum_ref)
            dg1_ref[...] = jnp.zeros_like(dg1_ref)
            sm_ref[...] = jnp.zeros_like(sm_ref)

        def headnorm_bwd(raw, dy, gain, scale, row):
            rs = lax.rsqrt(_dot_hi_r(raw * raw, bd_ref[...]) * (1.0 / HEAD_DIM) + EPS)
            n = raw * rs
            gsum_ref[row:row + 1, :] += jnp.sum(dy * n, axis=0, keepdims=True) * scale
            dn = dy * (gain * scale)
            return rs * (dn - n * (_dot_hi_r(dn * n, bd_ref[...]) * (1.0 / HEAD_DIM)))

        dproj_ref[:, O_Q:O_Q + ATT_W] = headnorm_bwd(qr_ref[...], dq_ref[...], qg_ref[...], QK_SCALE, 0).astype(BF16)
        dproj_ref[:, O_K:O_K + ATT_W] = headnorm_bwd(kr_ref[...], dk_ref[...], kg_ref[...], 1.0, 1).astype(BF16)
        dproj_ref[:, O_V:O_V + ATT_W] = dv_ref[...].astype(BF16)
        dproj_ref[:, O_A:O_A + 2 * CONV_CH] = dag_ref[...]

        dc8 = jnp.concatenate([dct_ref[...], jnp.zeros((128 - ATT_HEADS, TM), F32)], axis=0).T
        dc8 = dc8 + _dot_hi_r(drb_ref[...], pick_ref[...])
        dlogf = _dot_hi_l(triu_ref[...], dc8) + carry_ref[...]
        carry_ref[...] = dlogf[0:1, :]
        df = dlogf * _sigmoid(-(fz_ref[...] + bf_ref[...]))
        dproj_ref[:, O_F:O_F + 128] = df.astype(BF16)
        sm_ref[2:3, :] += jnp.sum(df, axis=0, keepdims=True)

        du1 = _dot_nt(dproj_ref[...], win_ref[...])
        xv = x_ref[...]
        r = lax.rsqrt(jnp.mean(xv * xv, axis=-1, keepdims=True) + EPS)
        n1 = xv * r
        t = du1 * g1_ref[...]
        dx_ref[...] = dx2_ref[...] + r * (t - n1 * jnp.mean(t * n1, axis=-1, keepdims=True))
        dg1_ref[0:1, :] += jnp.sum(du1 * n1, axis=0, keepdims=True)

        @pl.when(i == nb - 1)
        def _():
            sm_ref[0:2, :] = _dot_hi_r(gsum_ref[0:8, :], fold_ref[...])[0:2, :]

    rev = lambda w, cb=0: pl.BlockSpec((TM, w), lambda i: (nb - 1 - i, cb))
    full = lambda a: pl.BlockSpec(a.shape, lambda i: (0,) * a.ndim)
    bd, fold, triu = _head_blockdiag(), _head_fold(), _tril(TM).T
    consts = (g1, win, qg, kg, bf, bd, fold, triu, _head_pick())
    return pl.pallas_call(
        body, name="mixer_in_bwd", grid=(nb,),
        in_specs=[rev(D_MODEL), rev(D_MODEL), rev(ATT_W, O_Q // ATT_W), rev(ATT_W, O_K // ATT_W), rev(128, O_F // 128),
                  rev(ATT_W), rev(ATT_W), rev(ATT_W), rev(2 * CONV_CH),
                  pl.BlockSpec((ATT_HEADS, TM), lambda i: (0, nb - 1 - i)), rev(ATT_W)] + [full(a) for a in consts],
        out_specs=[rev(N_INP), rev(D_MODEL), pl.BlockSpec((8, D_MODEL), lambda i: (0, 0)),
                   pl.BlockSpec((8, 128), lambda i: (0, 0))],
        out_shape=[jax.ShapeDtypeStruct((S, N_INP), BF16), jax.ShapeDtypeStruct((S, D_MODEL), F32),
                   jax.ShapeDtypeStruct((8, D_MODEL), F32), jax.ShapeDtypeStruct((8, 128), F32)],
        scratch_shapes=[pltpu.VMEM((1, 128), F32), pltpu.VMEM((8, ATT_W), F32)],
        compiler_params=_params(1),
    )(x, dx2, proj, proj, proj, dq, dk, dv, dag, dct, drb, *consts)


def _layer_fwd(x, p):
    u1, proj, q, k, v, qa, ka, h0, h1, h3 = _mixer_in_fwd(
        x, p["g1"], p["win"], p["qg"], p["kg"], p["bf"], p["cw"], p["cvb"], p["lng"], p["lnb"])
    att, lse = _attn_fwd(q, qa, k, ka, v)
    x2, u2 = _mixer_out_fwd(x, att, h3, p["wo"], p["l"], p["g2"])
    x3, z, hh = _mlp_fwd(x2, u2, p["w1"], p["w2"], p["l"])
    saved = dict(x=x, u1=u1, proj=proj, q=q, k=k, v=v, qa=qa, ka=ka, h0=h0, h1=h1, h3=h3, att=att, lse=lse,
                 x2=x2, u2=u2, z=z, hh=hh)
    return x3, saved


def _layer_bwd(dx3, s, p):
    dz, dx2, dg2 = _mlp_bwd(dx3, s["z"], s["x2"], p["g2"], p["w1"], p["w2"], p["l"])
    g_w2 = _matmul_tn(s["hh"], dx3)
    g_w1 = _matmul_tn(s["u2"], dz, col_shards=N_CHIPS)
    datt, dlt, dh1, sm_c = _mixer_out_bwd(dx2, p["wo"], p["l"], s["att"], s["h1"], p["lng"], p["lnb"])
    g_wo = jnp.concatenate([_matmul_tn(s["att"], dx2)[0], _matmul_tn(s["h3"], dx2)[0]], axis=0)
    dag, dcw = _conv_glu_bwd(dh1, s["h0"], s["proj"], p["cw"])
    dq, dk, dv, dc4, drb = _attn_bwd(s["q"], s["qa"], s["k"], s["ka"], s["v"], datt, s["lse"], dlt)
    dct = dc4.reshape(ATT_HEADS, dc4.shape[2])
    dproj, dx, dg1, sm_a = _mixer_in_bwd(s["x"], dx2, s["proj"], dq, dk, dv, dag, dct, drb,
                                         p["g1"], p["win"], p["qg"], p["kg"], p["bf"])
    g_win = _matmul_tn(s["u1"], dproj)[0]
    big = dict(win=g_win, wo=g_wo, w1=g_w1, w2=g_w2[0])
    small = dict(g1=dg1[0], g2=dg2[0], lng=sm_c[0], lnb=sm_c[1], cvb=sm_c[2], cw=dcw[0:CONV_TAPS],
                 qg=sm_a[0, 0:HEAD_DIM], kg=sm_a[1, 0:HEAD_DIM], bf=sm_a[2, 0:ATT_HEADS])
    return dx, big, small


def _local_step(x, target, layers):
    saved = []
    h = x
    for p in layers:
        h, s = _layer_fwd(h, p)
        saved.append(s)
    dy, loss_acc = _loss_fwd_bwd(h, target)
    loss = loss_acc[0, 0] * (0.5 / D_MODEL)
    bigs, smalls = [], []
    d = dy
    for p, s in zip(reversed(layers), reversed(saved)):
        d, big, small = _layer_bwd(d, s, p)
        bigs.append(big)
        smalls.append(small)
    return loss, d, bigs[::-1], smalls[::-1]


def _win_to_internal(w):
    pad = jnp.zeros(w.shape[:-1] + (N_INP - N_IN,), w.dtype)
    return jnp.concatenate([w[..., :1536], w[..., 1544:], w[..., 1536:1544], pad], axis=-1)


def _win_to_global(g):
    return jnp.concatenate([g[..., :1536], g[..., O_F:O_F + ATT_HEADS], g[..., 1536:O_F]], axis=-1)


def _layer_params(l, win, wo, w1, w2, cw, norm1_g, b_f, q_norm_g, k_norm_g, conv_b, conv_ln_g, conv_ln_b, norm2_g):
    row = lambda a: a.reshape(1, -1)
    return dict(
        l=l, win=win, wo=wo, w1=w1, w2=w2,
        cw=jnp.pad(cw, ((0, CONV_HALO - CONV_TAPS), (0, 0))),
        g1=row(norm1_g[l]), g2=row(norm2_g[l]),
        qg=row(jnp.tile(q_norm_g[l], ATT_HEADS)), kg=row(jnp.tile(k_norm_g[l], ATT_HEADS)),
        bf=row(jnp.pad(b_f[l], (0, 128 - ATT_HEADS))),
        cvb=row(conv_b[l]), lng=row(conv_ln_g[l]), lnb=row(conv_ln_b[l]))


def _place():
    x, y, c = lax.axis_index("x"), lax.axis_index("y"), lax.axis_index("c")
    chips = [(1 - x, y), (x, 1 - y), (1 - x, 1 - y)]
    return x, y, c, chips


def _gather_weights(shards):
    n = len(shards)

    def body(*refs):
        ins, outs = refs[:n], refs[n:2 * n]
        send_sems, recv_sems = refs[2 * n:]
        x, y, c, chips = _place()
        me = 2 * x + y
        sibling = (x, y, 1 - c)

        def remote(k, src, dst, to):
            return pltpu.make_async_remote_copy(src_ref=src, dst_ref=dst, send_sem=send_sems.at[k],
                                                recv_sem=recv_sems.at[k], device_id=to, device_id_type=MESH)

        first = [remote(3 * a + j, ins[a].at[c], outs[a].at[me, c], (*chip, c))
                 for a in range(n) for j, chip in enumerate(chips)]
        for cp in first:
            cp.start()
        own = [remote(6 * n + a, ins[a], outs[a].at[me], sibling) for a in range(n)]
        for cp in own:
            cp.start()
        passed = []
        for a in range(n):
            for j, (px, py) in enumerate(chips):
                slot = outs[a].at[2 * px + py, c]
                remote(3 * a + j, slot, slot, sibling).wait_recv()
                fwd = remote(3 * n + 3 * a + j, slot, slot, sibling)
                fwd.start()
                passed.append(fwd)
        for a in range(n):
            for j, (px, py) in enumerate(chips):
                slot = outs[a].at[2 * px + py, 1 - c]
                remote(3 * n + 3 * a + j, slot, slot, sibling).wait_recv()
        for cp in own:
            cp.wait()
        for cp in first + passed:
            cp.wait_send()

    return pl.pallas_call(
        body, name="gather_weights",
        in_specs=[ANY] * n, out_specs=[ANY] * n,
        out_shape=[jax.ShapeDtypeStruct((N_CHIPS,) + s.shape, s.dtype) for s in shards],
        scratch_shapes=[pltpu.SemaphoreType.DMA((7 * n,)), pltpu.SemaphoreType.DMA((7 * n,))],
        compiler_params=pltpu.CompilerParams(has_side_effects=True),
    )(*shards)


def _pair_swap(g0, g1):
    n = len(g0)

    def body(*refs):
        r0, r1, outs = refs[:n], refs[n:2 * n], refs[2 * n:3 * n]
        send_sems, recv_sems = refs[3 * n:]
        x, y, c, _ = _place()
        sibling = (x, y, 1 - c)

        def remote(a, src):
            return pltpu.make_async_remote_copy(src_ref=src, dst_ref=outs[a], send_sem=send_sems.at[a],
                                                recv_sem=recv_sems.at[a], device_id=sibling, device_id_type=MESH)

        @pl.when(c == 0)
        def _():
            for a in range(n):
                remote(a, r1[a]).start()

        @pl.when(c == 1)
        def _():
            for a in range(n):
                remote(a, r0[a]).start()

        for a in range(n):
            remote(a, r0[a]).wait()

    return pl.pallas_call(
        body, name="pair_swap",
        in_specs=[ANY] * (2 * n), out_specs=[ANY] * n,
        out_shape=[jax.ShapeDtypeStruct(a.shape, a.dtype) for a in g0],
        scratch_shapes=[pltpu.SemaphoreType.DMA((n,)), pltpu.SemaphoreType.DMA((n,))],
        compiler_params=pltpu.CompilerParams(has_side_effects=True),
    )(*g0, *g1)


def _shard_tiles(arrays, steps):
    return [a.shape[-2] // steps for a in arrays]


def _pair_add(g0, g1, got):
    n = len(got)
    steps = 16
    tiles = _shard_tiles(got, steps)

    def body(*refs):
        r0, r1, rg, outs = refs[:n], refs[n:2 * n], refs[2 * n:3 * n], refs[3 * n:]
        c = lax.axis_index("c")

        @pl.when(c == 0)
        def _():
            for a in range(n):
                outs[a][...] = (r0[a][...] + rg[a][...]).astype(BF16)

        @pl.when(c == 1)
        def _():
            for a in range(n):
                outs[a][...] = (r1[a][...] + rg[a][...]).astype(BF16)

    specs = [pl.BlockSpec((N_CHIPS, t, a.shape[-1]), lambda i: (0, i, 0)) for a, t in zip(got, tiles)]
    return pl.pallas_call(
        body, name="pair_add", grid=(steps,),
        in_specs=specs * 3, out_specs=specs,
        out_shape=[jax.ShapeDtypeStruct(a.shape, BF16) for a in got],
        compiler_params=_params(1),
    )(*g0, *g1, *got)


def _chip_exchange(parts):
    n = len(parts)

    def body(*refs):
        ins, outs = refs[:n], refs[n:2 * n]
        send_sems, recv_sems = refs[2 * n:]
        x, y, c, chips = _place()

        def remote(a, j, px, py):
            return pltpu.make_async_remote_copy(src_ref=ins[a].at[2 * px + py], dst_ref=outs[a].at[j],
                                                send_sem=send_sems.at[3 * a + j], recv_sem=recv_sems.at[3 * a + j],
                                                device_id=(px, py, c), device_id_type=MESH)

        sends = [remote(a, j, px, py) for a in range(n) for j, (px, py) in enumerate(chips)]
        for cp in sends:
            cp.start()
        for cp in sends:
            cp.wait_recv()
        for cp in sends:
            cp.wait_send()

    return pl.pallas_call(
        body, name="chip_exchange",
        in_specs=[ANY] * n, out_specs=[ANY] * n,
        out_shape=[jax.ShapeDtypeStruct((N_CHIPS - 1,) + a.shape[1:], a.dtype) for a in parts],
        scratch_shapes=[pltpu.SemaphoreType.DMA((3 * n,)), pltpu.SemaphoreType.DMA((3 * n,))],
        compiler_params=pltpu.CompilerParams(has_side_effects=True),
    )(*parts)


def _chip_sum(parts, recv, place):
    n = len(parts)
    steps = 16
    tiles = _shard_tiles(parts, steps)

    def body(place_ref, *refs):
        own, got, outs = refs[:n], refs[n:2 * n], refs[2 * n:]
        for a in range(n):
            tot = own[a][...].astype(F32)
            for j in range(N_CHIPS - 1):
                tot = tot + got[a][j].astype(F32)
            outs[a][...] = tot

    own_specs = [pl.BlockSpec((None, t, a.shape[-1]), lambda i, p: (p[0], i, 0)) for a, t in zip(parts, tiles)]
    got_specs = [pl.BlockSpec((N_CHIPS - 1, t, a.shape[-1]), lambda i, p: (0, i, 0)) for a, t in zip(parts, tiles)]
    out_specs = [pl.BlockSpec((None, t, a.shape[-1]), lambda i, p: (p[1], i, 0)) for a, t in zip(parts, tiles)]
    return pl.pallas_call(
        body, name="chip_sum",
        grid_spec=pltpu.PrefetchScalarGridSpec(num_scalar_prefetch=1, grid=(steps,),
                                               in_specs=own_specs + got_specs, out_specs=out_specs),
        out_shape=[jax.ShapeDtypeStruct((2,) + a.shape[1:], F32) for a in parts],
        compiler_params=_params(1),
    )(place, *parts, *recv)


def _layer_swap(halves):
    n = len(halves)

    def body(*refs):
        ins, outs = refs[:n], refs[n:2 * n]
        send_sems, recv_sems = refs[2 * n:]
        x, y, c, _ = _place()
        sends = [pltpu.make_async_remote_copy(src_ref=ins[a].at[c], dst_ref=outs[a].at[c], send_sem=send_sems.at[a],
                                              recv_sem=recv_sems.at[a], device_id=(x, y, 1 - c), device_id_type=MESH)
                 for a in range(n)]
        for cp in sends:
            cp.start()
        for a in range(n):
            slot = outs[a].at[1 - c]
            pltpu.make_async_remote_copy(src_ref=slot, dst_ref=slot, send_sem=send_sems.at[a], recv_sem=recv_sems.at[a],
                                         device_id=(x, y, 1 - c), device_id_type=MESH).wait_recv()
        for cp in sends:
            cp.wait_send()

    return pl.pallas_call(
        body, name="layer_swap",
        in_specs=[ANY] * n, out_specs=[ANY] * n,
        out_shape=[jax.ShapeDtypeStruct(a.shape, a.dtype) for a in halves],
        input_output_aliases={a: a for a in range(n)},
        scratch_shapes=[pltpu.SemaphoreType.DMA((n,)), pltpu.SemaphoreType.DMA((n,))],
        compiler_params=pltpu.CompilerParams(has_side_effects=True),
    )(*halves)


def _adamw_math(w, g, m, v):
    m = ADAM_B1 * m + (1.0 - ADAM_B1) * g
    v = ADAM_B2 * v + (1.0 - ADAM_B2) * (g * g)
    m_hat = m / (1.0 - ADAM_B1 ** ADAM_STEP)
    v_hat = v / (1.0 - ADAM_B2 ** ADAM_STEP)
    delta = -ADAM_LR * (m_hat / (jnp.sqrt(v_hat) + ADAM_EPS) + ADAM_WD * w)
    return delta, m, v


def _adamw(ws, gs, ms, vs):
    n = len(ws)
    steps = 16
    tiles = _shard_tiles(ws, steps)

    def body(*refs):
        w_r, g_r, m_r, v_r = refs[:n], refs[n:2 * n], refs[2 * n:3 * n], refs[3 * n:4 * n]
        g_o, d_o, m_o, v_o = refs[4 * n:5 * n], refs[5 * n:6 * n], refs[6 * n:7 * n], refs[7 * n:]
        for a in range(n):
            g = g_r[a][...]
            d, m, v = _adamw_math(w_r[a][...], g, m_r[a][...], v_r[a][...])
            g_o[a][...] = g
            d_o[a][...] = d
            m_o[a][...] = m
            v_o[a][...] = v

    specs = [pl.BlockSpec((2, t, a.shape[-1]), lambda i: (0, i, 0)) for a, t in zip(ws, tiles)]
    outs = pl.pallas_call(
        body, name="adamw", grid=(steps,),
        in_specs=specs * 4, out_specs=specs * 4,
        out_shape=[jax.ShapeDtypeStruct(a.shape, F32) for a in ws] * 4,
        compiler_params=_params(1),
    )(*ws, *gs, *ms, *vs)
    return outs[:n], outs[n:2 * n], outs[2 * n:3 * n], outs[3 * n:]


SMALL_W = 512


def _small_allreduce_adamw(g, w, m, v, cw_w, cw_m, cw_v, cw_row0):
    R = g.shape[0]
    n_l = cw_w.shape[0]

    def body(g_ref, w_ref, m_ref, v_ref, cww_ref, cwm_ref, cwv_ref,
             gs_ref, d_ref, mo_ref, vo_ref, cg_ref, cd_ref, cmo_ref, cvo_ref,
             slots_ref, send_sems, recv_sems):
        x, y, c, _ = _place()
        me = 4 * x + 2 * y + c
        slots_ref[me] = g_ref[...]
        sends = []
        for d in range(1, 8):
            px, py, pc = x ^ (d >> 2), y ^ ((d >> 1) & 1), c ^ (d & 1)
            cp = pltpu.make_async_remote_copy(src_ref=g_ref, dst_ref=slots_ref.at[me], send_sem=send_sems.at[d - 1],
                                              recv_sem=recv_sems.at[d - 1], device_id=(px, py, pc), device_id_type=MESH)
            cp.start()
            sends.append(cp)
        for d in range(1, 8):
            px, py, pc = x ^ (d >> 2), y ^ ((d >> 1) & 1), c ^ (d & 1)
            slot = slots_ref.at[4 * px + 2 * py + pc]
            pltpu.make_async_remote_copy(src_ref=slot, dst_ref=slot, send_sem=send_sems.at[d - 1],
                                         recv_sem=recv_sems.at[d - 1], device_id=(px, py, pc),
                                         device_id_type=MESH).wait_recv()
        for cp in sends:
            cp.wait_send()
        tot = slots_ref[0]
        for k in range(1, 8):
            tot = tot + slots_ref[k]
        gs_ref[...] = tot
        dl, mn, vn = _adamw_math(w_ref[...], tot, m_ref[...], v_ref[...])
        d_ref[...] = dl
        mo_ref[...] = mn
        vo_ref[...] = vn
        chip = 2 * x + y
        for l in range(n_l):
            rows = tot[cw_row0[l]:cw_row0[l] + CONV_HALO, :]
            mine = rows[:, 0:128]
            for k in range(1, N_CHIPS):
                mine = jnp.where(chip == k, rows[:, 128 * k:128 * (k + 1)], mine)
            cg_ref[l] = mine
            dl, mn, vn = _adamw_math(cww_ref[l], mine, cwm_ref[l], cwv_ref[l])
            cd_ref[l] = dl
            cmo_ref[l] = mn
            cvo_ref[l] = vn

    vm = pl.BlockSpec(memory_space=pltpu.VMEM)
    small = jax.ShapeDtypeStruct((R, SMALL_W), F32)
    conv = jax.ShapeDtypeStruct(cw_w.shape, F32)
    return pl.pallas_call(
        body, name="small_allreduce_adamw",
        in_specs=[vm] * 7, out_specs=[vm] * 8,
        out_shape=[small] * 4 + [conv] * 4,
        scratch_shapes=[pltpu.VMEM((8, R, SMALL_W), F32), pltpu.SemaphoreType.DMA((7,)), pltpu.SemaphoreType.DMA((7,))],
        compiler_params=pltpu.CompilerParams(has_side_effects=True, vmem_limit_bytes=VMEM_LIMIT),
    )(g, w, m, v, cw_w, cw_m, cw_v)


SMALL_LAYOUT = (("conv_w", CONV_HALO), ("norm1_g", 2), ("norm2_g", 2), ("conv_b", 1), ("conv_ln_g", 1),
                ("conv_ln_b", 1), ("q_norm_g", 1), ("k_norm_g", 1), ("b_f", 1))
SMALL_ROWS = sum(r for _, r in SMALL_LAYOUT)
SMALL_ROWS_PAD = 48
LOSS_ROW = SMALL_ROWS


def _pack_small(per_layer):
    blocks = []
    for d in per_layer:
        rows = []
        for name, r in SMALL_LAYOUT:
            if name == "conv_w":
                a = d.get(name)
                a = jnp.zeros((r, SMALL_W), F32) if a is None else jnp.pad(a, ((0, r - a.shape[0]), (0, 0)))
            else:
                a = d[name].reshape(-1)
                a = jnp.pad(a, (0, r * SMALL_W - a.shape[0])).reshape(r, SMALL_W)
            rows.append(a)
        rows.append(jnp.zeros((SMALL_ROWS_PAD - SMALL_ROWS, SMALL_W), F32))
        blocks.append(jnp.concatenate(rows, axis=0))
    return jnp.concatenate(blocks, axis=0)


def _unpack_small(packed, name, size):
    n_l = packed.shape[0] // SMALL_ROWS_PAD
    row0 = 0
    for nm, r in SMALL_LAYOUT:
        if nm == name:
            break
        row0 += r
    out = [packed[l * SMALL_ROWS_PAD + row0:l * SMALL_ROWS_PAD + row0 + r].reshape(-1)[:size] for l in range(n_l)]
    return jnp.stack(out)


SMALL_SIZES = dict(norm1_g=D_MODEL, norm2_g=D_MODEL, conv_b=CONV_CH, conv_ln_g=CONV_CH, conv_ln_b=CONV_CH,
                   q_norm_g=HEAD_DIM, k_norm_g=HEAD_DIM, b_f=ATT_HEADS)
SMALL_KEYS = dict(norm1_g="g1", norm2_g="g2", conv_b="cvb", conv_ln_g="lng", conv_ln_b="lnb",
                  q_norm_g="qg", k_norm_g="kg", b_f="bf", conv_w="cw")
CONV_W_ROW0 = 0


def kernel(x, norm1_g, w_in, b_f, q_norm_g, k_norm_g, conv_w, conv_b, conv_ln_g, conv_ln_b, w_o, norm2_g, w_mlp_in, w_mlp_out, loss_target, m_norm1_g, m_w_in, m_b_f, m_q_norm_g, m_k_norm_g, m_conv_w, m_conv_b, m_conv_ln_g, m_conv_ln_b, m_w_o, m_norm2_g, m_w_mlp_in, m_w_mlp_out, v_norm1_g, v_w_in, v_b_f, v_q_norm_g, v_k_norm_g, v_conv_w, v_conv_b, v_conv_ln_g, v_conv_ln_b, v_w_o, v_norm2_g, v_w_mlp_in, v_w_mlp_out):
    n_l = w_in.shape[0]
    wide = w_in.shape[2]

    gw_in, gw_o, gw_1, gw_2, g_cw = _gather_weights(
        [w_in.astype(BF16), w_o.astype(BF16), w_mlp_in.astype(BF16), w_mlp_out.astype(BF16), conv_w])
    layers = []
    for l in range(n_l):
        win = _win_to_internal(jnp.concatenate([gw_in[k, l] for k in range(N_CHIPS)], axis=-1))
        cw = jnp.concatenate([g_cw[k, l] for k in range(N_CHIPS)], axis=-1)
        layers.append(_layer_params(l, win, gw_o, gw_1, gw_2, cw, norm1_g, b_f, q_norm_g, k_norm_g,
                                    conv_b, conv_ln_g, conv_ln_b, norm2_g))

    loss, dx, bigs, smalls = _local_step(x[0], loss_target[0], layers)

    def shard_major(b):
        gin = _win_to_global(b["win"]).reshape(D_MODEL, N_CHIPS, wide).transpose(1, 0, 2)
        return [gin, b["wo"].reshape(N_CHIPS, D_MODEL // N_CHIPS, D_MODEL), b["w1"],
                b["w2"].reshape(N_CHIPS, D_FF // N_CHIPS, D_MODEL)]

    g0, g1 = shard_major(bigs[0]), shard_major(bigs[1])
    got = _pair_swap(g0, g1)
    parts = _pair_add(g0, g1, got)
    recv = _chip_exchange(parts)
    place = jnp.stack([2 * lax.axis_index("x") + lax.axis_index("y"), lax.axis_index("c")]).astype(jnp.int32)
    g_big = _layer_swap(_chip_sum(parts, recv, place))
    big_w = [w_in, w_o, w_mlp_in, w_mlp_out]
    big_m = [m_w_in, m_w_o, m_w_mlp_in, m_w_mlp_out]
    big_v = [v_w_in, v_w_o, v_w_mlp_in, v_w_mlp_out]
    g_big, d_big, nm_big, nv_big = _adamw(big_w, g_big, big_m, big_v)

    env = dict(norm1_g=(norm1_g, m_norm1_g, v_norm1_g), norm2_g=(norm2_g, m_norm2_g, v_norm2_g),
               conv_b=(conv_b, m_conv_b, v_conv_b), conv_ln_g=(conv_ln_g, m_conv_ln_g, v_conv_ln_g),
               conv_ln_b=(conv_ln_b, m_conv_ln_b, v_conv_ln_b), q_norm_g=(q_norm_g, m_q_norm_g, v_q_norm_g),
               k_norm_g=(k_norm_g, m_k_norm_g, v_k_norm_g), b_f=(b_f, m_b_f, v_b_f))
    g_pack = _pack_small([{nm: s[key] for nm, key in SMALL_KEYS.items()} for s in smalls])
    g_pack = g_pack.at[LOSS_ROW, 0].set(loss)
    packs = [_pack_small([{nm: env[nm][t][l] for nm in env} for l in range(n_l)]) for t in range(3)]
    pad_cw = lambda a: jnp.pad(a, ((0, 0), (0, CONV_HALO - CONV_TAPS), (0, 0)))
    cw_row0 = tuple(l * SMALL_ROWS_PAD + CONV_W_ROW0 for l in range(n_l))
    gs, ds, ms, vs, cg, cd, cm, cv = _small_allreduce_adamw(
        g_pack, packs[0], packs[1], packs[2], pad_cw(conv_w), pad_cw(m_conv_w), pad_cw(v_conv_w), cw_row0)

    def small_out(packed, conv):
        o = {nm: _unpack_small(packed, nm, sz) for nm, sz in SMALL_SIZES.items()}
        o["conv_w"] = conv[:, 0:CONV_TAPS, :]
        return o

    def ordered(small, big):
        return (small["norm1_g"], big[0], small["b_f"], small["q_norm_g"], small["k_norm_g"], small["conv_w"],
                small["conv_b"], small["conv_ln_g"], small["conv_ln_b"], big[1], small["norm2_g"], big[2], big[3])

    return (gs[LOSS_ROW, 0], dx[None],
            *ordered(small_out(gs, cg), g_big), *ordered(small_out(ds, cd), d_big),
            *ordered(small_out(ms, cm), nm_big), *ordered(small_out(vs, cv), nv_big))
```

```python
import functools

import jax
import jax.numpy as jnp
from jax import lax
from jax.experimental import pallas as pl
from jax.experimental.pallas import tpu as pltpu

F32 = jnp.float32
BF16 = jnp.bfloat16

D_MODEL = 1024
ATT_HEADS = 8
HEAD_DIM = 64
ATT_W = ATT_HEADS * HEAD_DIM
CONV_CH = 512
CONV_TAPS = 31
CONV_HALO = 32
D_FF = 4 * D_MODEL
N_IN = 3 * ATT_W + ATT_HEADS + 2 * CONV_CH
O_Q, O_K, O_V, O_A, O_G, O_F = 0, 512, 1024, 1536, 2048, 2560
N_INP = O_F + 128
EPS = 1e-6
QK_SCALE = 0.125

ADAM_LR = 0.001
ADAM_B1 = 0.9
ADAM_B2 = 0.999
ADAM_EPS = 1e-08
ADAM_WD = 0.01
ADAM_STEP = 10

N_CHIPS = 4
VMEM_LIMIT = 52 * 1024 * 1024
MESH = pl.DeviceIdType.MESH
ANY = pl.BlockSpec(memory_space=pl.ANY)


def _params(n_axes, **kw):
    return pltpu.CompilerParams(dimension_semantics=("arbitrary",) * n_axes,
                                vmem_limit_bytes=VMEM_LIMIT, **kw)


def _dot(a, b):
    return jnp.dot(a, b, preferred_element_type=F32)


def _dot_nt(a, b):
    return lax.dot_general(a, b, (((1,), (1,)), ((), ())), preferred_element_type=F32)


def _dot_tn(a, b):
    return lax.dot_general(a, b, (((0,), (0,)), ((), ())), preferred_element_type=F32)


def _split3(a):
    a1 = a.astype(BF16)
    r = a - a1.astype(F32)
    a2 = r.astype(BF16)
    a3 = (r - a2.astype(F32)).astype(BF16)
    return a1, a2, a3


def _dot_hi_r(a, b_exact):
    return sum(_dot(p, b_exact) for p in _split3(a))


def _dot_hi_l(a_exact, b):
    return sum(_dot(a_exact, p) for p in _split3(b))


def _sigmoid(x):
    return 1.0 / (1.0 + jnp.exp(-x))


def _head_blockdiag():
    i = jnp.arange(ATT_W) // HEAD_DIM
    return (i[:, None] == i[None, :]).astype(BF16)


AUG_LANES = 8


def _aug_place(first):
    piece = jnp.arange(3 * 128)[:, None] // 128
    h = jnp.arange(3 * 128)[:, None] % 128
    lane = jnp.arange(ATT_W)[None, :]
    return ((h < ATT_HEADS) & (lane == 128 * (h // 2) + AUG_LANES * (h % 2) + first + piece)).astype(BF16)


def _aug_ones(first):
    lane = jnp.arange(ATT_W) % 128
    pos = lane % AUG_LANES
    return ((lane < 2 * AUG_LANES) & (pos >= first) & (pos < first + 3)).astype(F32).reshape(1, ATT_W)


def _head_fold():
    i = jnp.arange(ATT_W)[:, None] % HEAD_DIM
    j = jnp.arange(128)[None, :]
    return (i == j).astype(BF16)


def _head_pick():
    i = jnp.arange(ATT_W)[:, None]
    h = jnp.arange(128)[None, :]
    return (i == h * HEAD_DIM).astype(BF16)


def _tril(n):
    r = jnp.arange(n)
    return (r[:, None] >= r[None, :]).astype(BF16)


SUBLANES = 8


def _fill_row_shifts(buf_ref, shifts_ref, tm):
    n = tm + CONV_HALO - SUBLANES
    for b in range(1, SUBLANES):
        shifts_ref[b - 1, 0:n, :] = buf_ref[pl.ds(b, n), :]


def _row_shifted(buf_ref, shifts_ref, offset, rows, base=0):
    a, b = divmod(offset, SUBLANES)
    start = pl.multiple_of(base + SUBLANES * a, SUBLANES)
    if b == 0:
        return buf_ref[pl.ds(start, rows), :]
    return shifts_ref[b - 1, pl.ds(start, rows), :]


CONV_ROWS = 32


def _mixer_in_fwd(x, g1, win, qg, kg, bf, cw, cvb, lng, lnb):
    S = x.shape[0]
    TM = min(512, S)
    nb = S // TM

    def body(x_ref, g1_ref, win_ref, qg_ref, kg_ref, bf_ref, cw_ref, cvb_ref, lng_ref, lnb_ref,
             bd_ref, tri_ref, pq_ref, pk_ref, oq_ref, ok_ref,
             u1_ref, proj_ref, q_ref, k_ref, v_ref, qa_ref, ka_ref, h0_ref, h1_ref, h3_ref,
             carry_ref, hbuf_ref, hs_ref):
        i = pl.program_id(0)

        @pl.when(i == 0)
        def _():
            carry_ref[...] = jnp.zeros_like(carry_ref)
            hbuf_ref[0:CONV_HALO, :] = jnp.zeros((CONV_HALO, CONV_CH), F32)

        @pl.when(i > 0)
        def _():
            hbuf_ref[0:CONV_HALO, :] = hbuf_ref[TM:TM + CONV_HALO, :]

        xv = x_ref[...]
        r = lax.rsqrt(jnp.mean(xv * xv, axis=-1, keepdims=True) + EPS)
        u = (xv * r * g1_ref[...]).astype(BF16)
        u1_ref[...] = u
        proj_ref[...] = _dot(u, win_ref[...])

        def headnorm(raw, gain):
            ss = _dot_hi_r(raw * raw, bd_ref[...]) * (1.0 / HEAD_DIM)
            return raw * lax.rsqrt(ss + EPS) * gain

        q_ref[...] = (headnorm(proj_ref[:, O_Q:O_Q + ATT_W], qg_ref[...]) * QK_SCALE).astype(BF16)
        k_ref[...] = headnorm(proj_ref[:, O_K:O_K + ATT_W], kg_ref[...]).astype(BF16)
        v_ref[...] = proj_ref[:, O_V:O_V + ATT_W].astype(BF16)

        zf = proj_ref[:, O_F:O_F + 128] + bf_ref[...]
        logf = jnp.minimum(zf, 0.0) - jnp.log(1.0 + jnp.exp(-jnp.abs(zf)))
        lane = lax.broadcasted_iota(jnp.int32, (TM, 128), 1)
        logf = jnp.where(lane < ATT_HEADS, logf, 0.0)
        c8 = _dot_hi_l(tri_ref[...], logf) + carry_ref[...]
        carry_ref[...] = c8[TM - 1:TM, :]
        pieces = jnp.concatenate(_split3(c8), axis=1)
        qa_ref[...] = (_dot(pieces, pq_ref[...]) + oq_ref[...]).astype(BF16)
        ka_ref[...] = (ok_ref[...] - _dot(pieces, pk_ref[...])).astype(BF16)

        h0 = proj_ref[:, O_A:O_A + CONV_CH] * _sigmoid(proj_ref[:, O_G:O_G + CONV_CH])
        h0_ref[...] = h0
        hbuf_ref[CONV_HALO:CONV_HALO + TM, :] = h0
        _fill_row_shifts(hbuf_ref, hs_ref, TM)
        acc = jnp.zeros((TM, CONV_CH), F32) + cvb_ref[...]
        for j in range(CONV_TAPS):
            acc = acc + cw_ref[j:j + 1, :] * _row_shifted(hbuf_ref, hs_ref, CONV_HALO - CONV_TAPS + 1 + j, TM)
        h1_ref[...] = acc
        mu = jnp.mean(acc, axis=-1, keepdims=True)
        d = acc - mu
        var = jnp.mean(d * d, axis=-1, keepdims=True)
        h2 = d * lax.rsqrt(var + EPS) * lng_ref[...] + lnb_ref[...]
        h3_ref[...] = (h2 * _sigmoid(h2)).astype(BF16)

    row = lambda w: pl.BlockSpec((TM, w), lambda i: (i, 0))
    full = lambda a: pl.BlockSpec(a.shape, lambda i: (0,) * a.ndim)
    ins = (x, g1, win, qg, kg, bf, cw, cvb, lng, lnb, _head_blockdiag(), _tril(TM),
           _aug_place(0), _aug_place(3), _aug_ones(3), _aug_ones(0))
    return pl.pallas_call(
        body, name="mixer_in_fwd", grid=(nb,),
        in_specs=[row(D_MODEL)] + [full(a) for a in ins[1:]],
        out_specs=[row(D_MODEL), row(N_INP), row(ATT_W), row(ATT_W), row(ATT_W), row(ATT_W), row(ATT_W),
                   row(CONV_CH), row(CONV_CH), row(CONV_CH)],
        out_shape=[jax.ShapeDtypeStruct((S, D_MODEL), BF16),
                   jax.ShapeDtypeStruct((S, N_INP), F32),
                   jax.ShapeDtypeStruct((S, ATT_W), BF16),
                   jax.ShapeDtypeStruct((S, ATT_W), BF16),
                   jax.ShapeDtypeStruct((S, ATT_W), BF16),
                   jax.ShapeDtypeStruct((S, ATT_W), BF16),
                   jax.ShapeDtypeStruct((S, ATT_W), BF16),
                   jax.ShapeDtypeStruct((S, CONV_CH), F32),
                   jax.ShapeDtypeStruct((S, CONV_CH), F32),
                   jax.ShapeDtypeStruct((S, CONV_CH), BF16)],
        scratch_shapes=[pltpu.VMEM((1, 128), F32), pltpu.VMEM((TM + CONV_HALO, CONV_CH), F32),
                        pltpu.VMEM((SUBLANES - 1, TM + CONV_HALO, CONV_CH), F32)],
        compiler_params=_params(1),
    )(*ins)


def _pair_heads(lo, alo, x, xa):
    z = jnp.zeros_like(x)
    return (jnp.concatenate([jnp.where(lo, x, z), jnp.where(alo, xa, z)], axis=1),
            jnp.concatenate([jnp.where(lo, z, x), jnp.where(alo, z, xa)], axis=1))


def _attn_fwd(q, qa, k, ka, v):
    S = q.shape[0]
    T = min(1024, S)
    nq = S // T

    def body(q_ref, qa_ref, k_ref, ka_ref, v_ref, o_ref, lse_ref):
        qi = pl.program_id(1)
        lane = lax.broadcasted_iota(jnp.int32, (T, 128), 1)
        lo = lane < HEAD_DIM
        qm = _pair_heads(lo, lane < AUG_LANES, q_ref[...], qa_ref[...])
        tril = (lax.broadcasted_iota(jnp.int32, (T, T), 0) >= lax.broadcasted_iota(jnp.int32, (T, T), 1))

        def step(kj, carry, masked):
            off = pl.multiple_of(kj * T, T)
            kb = jnp.concatenate([k_ref[pl.ds(off, T), :], ka_ref[pl.ds(off, T), :]], axis=1)
            vb = v_ref[pl.ds(off, T), :]
            new = []
            for h in range(2):
                m, l, acc = carry[3 * h:3 * h + 3]
                s = _dot_nt(qm[h], kb)
                if masked:
                    s = jnp.where(tril, s, -1e30)
                m_new = jnp.maximum(m, jnp.max(s, axis=-1, keepdims=True))
                alpha = jnp.exp(m - m_new)
                p = jnp.exp(s - m_new)
                l = alpha * l + jnp.sum(p, axis=-1, keepdims=True)
                acc = alpha * acc + _dot(p.astype(BF16), vb)
                new += [m_new, l, acc]
            return tuple(new)

        init = (jnp.full((T, 1), -1e30, F32), jnp.zeros((T, 1), F32), jnp.zeros((T, 128), F32)) * 2
        carry = lax.fori_loop(0, qi, lambda kj, c: step(kj, c, False), init)
        m0, l0, a0, m1, l1, a1 = step(qi, carry, True)
        o_ref[...] = jnp.where(lo, a0 / l0, a1 / l1).astype(BF16)
        lse_ref[...] = jnp.where(lo, m0 + jnp.log(l0), m1 + jnp.log(l1))

    qblk = pl.BlockSpec((T, 128), lambda hp, qi: (qi, hp))
    seq = pl.BlockSpec((S, 128), lambda hp, qi: (0, hp))
    return pl.pallas_call(
        body, name="attn_fwd", grid=(ATT_HEADS // 2, nq),
        in_specs=[qblk, qblk, seq, seq, seq],
        out_specs=[qblk, qblk],
        out_shape=[jax.ShapeDtypeStruct((S, ATT_W), BF16), jax.ShapeDtypeStruct((S, ATT_W), F32)],
        compiler_params=_params(2),
    )(q, qa, k, ka, v)


def _attn_bwd(q, qa, k, ka, v, do, lse, dlt):
    S = q.shape[0]
    T = min(512, S)
    nq = S // T

    def body(q_ref, qa_ref, do_ref, lse_ref, dlt_ref, k_ref, ka_ref, v_ref, dq_ref, dk_ref, dv_ref, dc_ref, dr_ref):
        kj = pl.program_id(1)

        @pl.when(kj == 0)
        def _():
            dq_ref[...] = jnp.zeros_like(dq_ref)
            dr_ref[...] = jnp.zeros_like(dr_ref)

        lane = lax.broadcasted_iota(jnp.int32, (T, 128), 1)
        lo = lane < HEAD_DIM
        alo = lane < AUG_LANES
        tril = (lax.broadcasted_iota(jnp.int32, (T, T), 0) >= lax.broadcasted_iota(jnp.int32, (T, T), 1))
        kb = k_ref[...]
        kaug = jnp.concatenate([kb, ka_ref[...]], axis=1)
        vb = v_ref[...]

        def step(qi, carry, masked):
            off = pl.multiple_of(qi * T, T)
            qb = q_ref[pl.ds(off, T), :]
            dob = do_ref[pl.ds(off, T), :]
            lseb = lse_ref[pl.ds(off, T), :]
            dltb = dlt_ref[pl.ds(off, T), :]
            qm = _pair_heads(lo, alo, qb, qa_ref[pl.ds(off, T), :])
            zero = jnp.zeros_like(qb)
            new, dqs, drs = [], [], []
            for h in range(2):
                dk_a, dv_a, dc_a = carry[3 * h:3 * h + 3]
                cl = HEAD_DIM * h
                dom = jnp.where(lo, dob, zero) if h == 0 else jnp.where(lo, zero, dob)
                s = _dot_nt(qm[h], kaug)
                if masked:
                    s = jnp.where(tril, s, -1e30)
                p = jnp.exp(s - lseb[:, cl:cl + 1])
                dp = _dot_nt(dom, vb)
                ds = p * (dp - dltb[:, cl:cl + 1])
                pb = p.astype(BF16)
                dsb = ds.astype(BF16)
                dv_a = dv_a + _dot_tn(pb, dob)
                dk_a = dk_a + _dot_tn(dsb, qb)
                dc_a = dc_a + jnp.sum(ds, axis=0, keepdims=True)
                dqs.append(_dot(dsb, kb))
                drs.append(jnp.sum(ds, axis=1, keepdims=True))
                new += [dk_a, dv_a, dc_a]
            dq_ref[pl.ds(off, T), :] += jnp.where(lo, dqs[0], dqs[1])
            dr_ref[pl.ds(off, T), :] += jnp.where(lo, drs[0], drs[1])
            return tuple(new)

        init = (jnp.zeros((T, 128), F32), jnp.zeros((T, 128), F32), jnp.zeros((1, T), F32)) * 2
        carry = step(kj, init, True)
        carry = lax.fori_loop(kj + 1, nq, lambda qi, c: step(qi, c, False), carry)
        dk_ref[...] = jnp.where(lo, carry[0], carry[3])
        dv_ref[...] = jnp.where(lo, carry[1], carry[4])
        dc_ref[0:1, :] = -carry[2]
        dc_ref[1:2, :] = -carry[5]

    seq = pl.BlockSpec((S, 128), lambda hp, kj: (0, hp))
    kblk = pl.BlockSpec((T, 128), lambda hp, kj: (kj, hp))
    return pl.pallas_call(
        body, name="attn_bwd", grid=(ATT_HEADS // 2, nq),
        in_specs=[seq, seq, seq, seq, seq, kblk, kblk, kblk],
        out_specs=[seq, kblk, kblk, pl.BlockSpec((None, 2, T), lambda hp, kj: (hp, 0, kj)), seq],
        out_shape=[jax.ShapeDtypeStruct((S, ATT_W), F32), jax.ShapeDtypeStruct((S, ATT_W), F32),
                   jax.ShapeDtypeStruct((S, ATT_W), F32), jax.ShapeDtypeStruct((ATT_HEADS // 2, 2, S), F32),
                   jax.ShapeDtypeStruct((S, ATT_W), F32)],
        compiler_params=_params(2),
    )(q, qa, do, lse, dlt, k, ka, v)


def _wo_spec(wo4):
    return pl.BlockSpec(wo4.shape, lambda i: (0, 0, 0))


def _wo_halves(wo_ref):
    half = N_CHIPS // 2
    return (wo_ref[0:half].reshape(ATT_W, D_MODEL), wo_ref[half:N_CHIPS].reshape(CONV_CH, D_MODEL))


def _mixer_out_fwd(x, att, h3, wo4, g2):
    S = x.shape[0]
    TM = min(512, S)

    def body(x_ref, att_ref, h3_ref, wo_ref, g2_ref, x2_ref, u2_ref):
        wa, wc = _wo_halves(wo_ref)
        x2 = x_ref[...] + _dot(att_ref[...], wa) + _dot(h3_ref[...], wc)
        x2_ref[...] = x2
        r = lax.rsqrt(jnp.mean(x2 * x2, axis=-1, keepdims=True) + EPS)
        u2_ref[...] = (x2 * r * g2_ref[...]).astype(BF16)

    row = lambda w: pl.BlockSpec((TM, w), lambda i: (i, 0))
    full = lambda a: pl.BlockSpec(a.shape, lambda i: (0,) * a.ndim)
    return pl.pallas_call(
        body, name="mixer_out_fwd", grid=(S // TM,),
        in_specs=[row(D_MODEL), row(ATT_W), row(CONV_CH), _wo_spec(wo4), full(g2)],
        out_specs=[row(D_MODEL), row(D_MODEL)],
        out_shape=[jax.ShapeDtypeStruct((S, D_MODEL), F32), jax.ShapeDtypeStruct((S, D_MODEL), BF16)],
        compiler_params=_params(1),
    )(x, att, h3, wo4, g2)


def _mlp_w_specs():
    return [pl.BlockSpec((None, D_MODEL, D_FF // N_CHIPS), lambda i, f: (f, 0, 0)),
            pl.BlockSpec((None, D_FF // N_CHIPS, D_MODEL), lambda i, f: (f, 0, 0))]


def _mlp_fwd(x2, u2, w1, w2):
    S = x2.shape[0]
    TM = min(512, S)
    TF = 1024

    def body(x2_ref, u2_ref, w1_ref, w2_ref, x3_ref, z_ref, hh_ref):
        f = pl.program_id(1)

        @pl.when(f == 0)
        def _():
            x3_ref[...] = x2_ref[...]

        z = _dot(u2_ref[...], w1_ref[...])
        z_ref[...] = z
        zr = jnp.maximum(z, 0.0)
        hh = (zr * zr).astype(BF16)
        hh_ref[...] = hh
        x3_ref[...] += _dot(hh, w2_ref[...])

    return pl.pallas_call(
        body, name="mlp_fwd", grid=(S // TM, D_FF // TF),
        in_specs=[pl.BlockSpec((TM, D_MODEL), lambda i, f: (i, 0)), pl.BlockSpec((TM, D_MODEL), lambda i, f: (i, 0))]
        + _mlp_w_specs(),
        out_specs=[pl.BlockSpec((TM, D_MODEL), lambda i, f: (i, 0)), pl.BlockSpec((TM, TF), lambda i, f: (i, f)),
                   pl.BlockSpec((TM, TF), lambda i, f: (i, f))],
        out_shape=[jax.ShapeDtypeStruct((S, D_MODEL), F32), jax.ShapeDtypeStruct((S, D_FF), F32),
                   jax.ShapeDtypeStruct((S, D_FF), BF16)],
        compiler_params=_params(2),
    )(x2, u2, w1, w2)


def _loss_fwd_bwd(y, t):
    S = y.shape[0]
    TM = min(512, S)

    def body(y_ref, t_ref, dy_ref, loss_ref):
        @pl.when(pl.program_id(0) == 0)
        def _():
            loss_ref[...] = jnp.zeros_like(loss_ref)

        d = y_ref[...] - t_ref[...]
        dy_ref[...] = d * (1.0 / D_MODEL)
        loss_ref[...] += jnp.sum(d * d)

    row = pl.BlockSpec((TM, D_MODEL), lambda i: (i, 0))
    return pl.pallas_call(
        body, name="loss", grid=(S // TM,),
        in_specs=[row, row], out_specs=[row, pl.BlockSpec((8, 128), lambda i: (0, 0))],
        out_shape=[jax.ShapeDtypeStruct((S, D_MODEL), F32), jax.ShapeDtypeStruct((8, 128), F32)],
        compiler_params=_params(1),
    )(y, t)


def _mlp_bwd(dx3, z, x2, g2, w1, w2):
    S = dx3.shape[0]
    TM = min(512, S)
    TF = 1024
    nf = D_FF // TF

    def body(dx3_ref, z_ref, x2_ref, g2_ref, w1_ref, w2_ref, dz_ref, dx2_ref, dg2_ref, du2_ref):
        i = pl.program_id(0)
        f = pl.program_id(1)

        @pl.when((i == 0) & (f == 0))
        def _():
            dg2_ref[...] = jnp.zeros_like(dg2_ref)

        @pl.when(f == 0)
        def _():
            du2_ref[...] = jnp.zeros_like(du2_ref)

        dhh = _dot_nt(dx3_ref[...].astype(BF16), w2_ref[...])
        dz = (dhh * (2.0 * jnp.maximum(z_ref[...], 0.0))).astype(BF16)
        dz_ref[...] = dz
        du2_ref[...] += _dot_nt(dz, w1_ref[...])

        @pl.when(f == nf - 1)
        def _():
            x2 = x2_ref[...]
            r = lax.rsqrt(jnp.mean(x2 * x2, axis=-1, keepdims=True) + EPS)
            n = x2 * r
            du2 = du2_ref[...]
            t = du2 * g2_ref[...]
            dx2_ref[...] = dx3_ref[...] + r * (t - n * jnp.mean(t * n, axis=-1, keepdims=True))
            dg2_ref[0:1, :] += jnp.sum(du2 * n, axis=0, keepdims=True)

    rowi = pl.BlockSpec((TM, D_MODEL), lambda i, f: (i, 0))
    return pl.pallas_call(
        body, name="mlp_bwd", grid=(S // TM, nf),
        in_specs=[rowi, pl.BlockSpec((TM, TF), lambda i, f: (i, f)), rowi,
                  pl.BlockSpec((1, D_MODEL), lambda i, f: (0, 0))] + _mlp_w_specs(),
        out_specs=[pl.BlockSpec((TM, TF), lambda i, f: (i, f)), rowi, pl.BlockSpec((8, D_MODEL), lambda i, f: (0, 0))],
        out_shape=[jax.ShapeDtypeStruct((S, D_FF), BF16), jax.ShapeDtypeStruct((S, D_MODEL), F32),
                   jax.ShapeDtypeStruct((8, D_MODEL), F32)],
        scratch_shapes=[pltpu.VMEM((TM, D_MODEL), F32)],
        compiler_params=_params(2),
    )(dx3, z, x2, g2, w1, w2)


def _matmul_tn(a, b, col_shards=1):
    S, I = a.shape
    J = b.shape[1]
    TI = min(I, 1024)
    TJ = 1024 if J % 1024 == 0 else 896
    TS = min(S, 1024)
    nk = S // TS
    per = J // col_shards // TJ

    def body(a_ref, b_ref, o_ref):
        @pl.when(pl.program_id(2) == 0)
        def _():
            o_ref[...] = jnp.zeros_like(o_ref)

        o_ref[...] += _dot_tn(a_ref[...].astype(BF16), b_ref[...].astype(BF16))

    return pl.pallas_call(
        body, name="matmul_tn", grid=(I // TI, J // TJ, nk),
        in_specs=[pl.BlockSpec((TS, TI), lambda i, j, k: (k, i)), pl.BlockSpec((TS, TJ), lambda i, j, k: (k, j))],
        out_specs=pl.BlockSpec((None, TI, TJ), lambda i, j, k: (j // per, i, j % per)),
        out_shape=jax.ShapeDtypeStruct((col_shards, I, J // col_shards), F32),
        compiler_params=_params(3),
    )(a, b)


def _mixer_out_bwd(dx2, wo4, att, h1, lng, lnb):
    S = dx2.shape[0]
    TM = min(512, S)

    def body(dx2_ref, wo_ref, att_ref, h1_ref, lng_ref, lnb_ref, bd_ref, datt_ref, dlt_ref, dh1_ref, sm_ref):
        @pl.when(pl.program_id(0) == 0)
        def _():
            sm_ref[...] = jnp.zeros_like(sm_ref)

        dxb = dx2_ref[...].astype(BF16)
        wa, wc = _wo_halves(wo_ref)
        datt = _dot_nt(dxb, wa)
        datt_ref[...] = datt.astype(BF16)
        dlt_ref[...] = _dot_hi_r(datt * att_ref[...].astype(F32), bd_ref[...])
        dh3 = _dot_nt(dxb, wc)
        h1 = h1_ref[...]
        mu = jnp.mean(h1, axis=-1, keepdims=True)
        d = h1 - mu
        rstd = lax.rsqrt(jnp.mean(d * d, axis=-1, keepdims=True) + EPS)
        n = d * rstd
        h2 = n * lng_ref[...] + lnb_ref[...]
        sg = _sigmoid(h2)
        dh2 = dh3 * (sg * (1.0 + h2 * (1.0 - sg)))
        dn = dh2 * lng_ref[...]
        dh1 = rstd * (dn - jnp.mean(dn, axis=-1, keepdims=True) - n * jnp.mean(dn * n, axis=-1, keepdims=True))
        dh1_ref[...] = dh1
        sm_ref[0:1, :] += jnp.sum(dh2 * n, axis=0, keepdims=True)
        sm_ref[1:2, :] += jnp.sum(dh2, axis=0, keepdims=True)
        sm_ref[2:3, :] += jnp.sum(dh1, axis=0, keepdims=True)

    row = lambda w: pl.BlockSpec((TM, w), lambda i: (i, 0))
    full = lambda a: pl.BlockSpec(a.shape, lambda i: (0,) * a.ndim)
    bd = _head_blockdiag()
    return pl.pallas_call(
        body, name="mixer_out_bwd", grid=(S // TM,),
        in_specs=[row(D_MODEL), _wo_spec(wo4), row(ATT_W), row(CONV_CH), full(lng), full(lnb), full(bd)],
        out_specs=[row(ATT_W), row(ATT_W), row(CONV_CH), pl.BlockSpec((8, CONV_CH), lambda i: (0, 0))],
        out_shape=[jax.ShapeDtypeStruct((S, ATT_W), BF16), jax.ShapeDtypeStruct((S, ATT_W), F32),
                   jax.ShapeDtypeStruct((S, CONV_CH), F32), jax.ShapeDtypeStruct((8, CONV_CH), F32)],
        compiler_params=_params(1),
    )(dx2, wo4, att, h1, lng, lnb, bd)


def _conv_glu_bwd(dh1, h0, proj, cw):
    S = dh1.shape[0]
    TM = min(512, S)
    nb = S // TM
    lead = CONV_HALO - CONV_TAPS + 1

    def body(dh1_ref, dnx_ref, h0_ref, hpv_ref, a_ref, g_ref, cw_ref, dag_ref, dcw_ref,
             dbuf_ref, hbuf_ref, ds_ref, hs_ref, dh0_ref, dcw8_ref):
        i = pl.program_id(0)

        @pl.when(i == 0)
        def _():
            dcw8_ref[...] = jnp.zeros_like(dcw8_ref)

        dbuf_ref[0:TM, :] = dh1_ref[...]
        dbuf_ref[TM:TM + CONV_HALO, :] = jnp.where(i < nb - 1, dnx_ref[0:CONV_HALO, :], 0.0)
        hbuf_ref[0:CONV_HALO, :] = jnp.where(i > 0, hpv_ref[TM - CONV_HALO:TM, :], 0.0)
        hbuf_ref[CONV_HALO:CONV_HALO + TM, :] = h0_ref[...]
        _fill_row_shifts(dbuf_ref, ds_ref, TM)
        _fill_row_shifts(hbuf_ref, hs_ref, TM)

        def conv_rows(step, _):
            r0 = pl.multiple_of(step * CONV_ROWS, CONV_ROWS)
            dh1 = dbuf_ref[pl.ds(r0, CONV_ROWS), :]
            part = jnp.zeros((CONV_ROWS, CONV_CH), F32)
            for j in range(CONV_TAPS):
                part = part + cw_ref[j:j + 1, :] * _row_shifted(dbuf_ref, ds_ref, CONV_TAPS - 1 - j, CONV_ROWS, r0)
                prod = dh1 * _row_shifted(hbuf_ref, hs_ref, lead + j, CONV_ROWS, r0)
                dcw8_ref[j] += jnp.sum(prod.reshape(CONV_ROWS // SUBLANES, SUBLANES, CONV_CH), axis=0)
            dh0_ref[pl.ds(r0, CONV_ROWS), :] = part
            return 0

        lax.fori_loop(0, TM // CONV_ROWS, conv_rows, 0)

        @pl.when(i == nb - 1)
        def _():
            dcw_ref[...] = jnp.sum(dcw8_ref[...], axis=1)

        dh0 = dh0_ref[...]
        sg = _sigmoid(g_ref[...])
        dag_ref[:, 0:CONV_CH] = (dh0 * sg).astype(BF16)
        dag_ref[:, CONV_CH:2 * CONV_CH] = (dh0 * a_ref[...] * sg * (1.0 - sg)).astype(BF16)

    blk = lambda fn: pl.BlockSpec((TM, CONV_CH), fn)
    return pl.pallas_call(
        body, name="conv_glu_bwd", grid=(nb,),
        in_specs=[blk(lambda i: (i, 0)), blk(lambda i: (jnp.minimum(i + 1, nb - 1), 0)),
                  blk(lambda i: (i, 0)), blk(lambda i: (jnp.maximum(i - 1, 0), 0)),
                  blk(lambda i: (i, O_A // CONV_CH)), blk(lambda i: (i, O_G // CONV_CH)),
                  pl.BlockSpec(cw.shape, lambda i: (0, 0))],
        out_specs=[pl.BlockSpec((TM, 2 * CONV_CH), lambda i: (i, 0)), pl.BlockSpec((CONV_HALO, CONV_CH), lambda i: (0, 0))],
        out_shape=[jax.ShapeDtypeStruct((S, 2 * CONV_CH), BF16), jax.ShapeDtypeStruct((CONV_HALO, CONV_CH), F32)],
        scratch_shapes=[pltpu.VMEM((TM + CONV_HALO, CONV_CH), F32), pltpu.VMEM((TM + CONV_HALO, CONV_CH), F32),
                        pltpu.VMEM((SUBLANES - 1, TM + CONV_HALO, CONV_CH), F32),
                        pltpu.VMEM((SUBLANES - 1, TM + CONV_HALO, CONV_CH), F32),
                        pltpu.VMEM((TM, CONV_CH), F32), pltpu.VMEM((CONV_HALO, SUBLANES, CONV_CH), F32)],
        compiler_params=_params(1),
    )(dh1, dh1, h0, h0, proj, proj, cw)


def _mixer_in_bwd(x, dx2, proj, dq, dk, dv, dag, dct, drb, g1, win, qg, kg, bf):
    S = x.shape[0]
    TM = min(256, S)
    nb = S // TM

    def body(x_ref, dx2_ref, qr_ref, kr_ref, fz_ref, dq_ref, dk_ref, dv_ref, dag_ref, dct_ref, drb_ref,
             g1_ref, win_ref, qg_ref, kg_ref, bf_ref, bd_ref, fold_ref, triu_ref, pick_ref,
             dproj_ref, dx_ref, dg1_ref, sm_ref, carry_ref, gsum_ref):
        i = pl.program_id(0)

        @pl.when(i == 0)
        def _():
            carry_ref[...] = jnp.zeros_like(carry_ref)
            gsum_ref[...] = jnp.zeros_like(gsum_ref)
            dg1_ref[...] = jnp.zeros_like(dg1_ref)
            sm_ref[...] = jnp.zeros_like(sm_ref)

        def headnorm_bwd(raw, dy, gain, scale, row):
            rs = lax.rsqrt(_dot_hi_r(raw * raw, bd_ref[...]) * (1.0 / HEAD_DIM) + EPS)
            n = raw * rs
            gsum_ref[row:row + 1, :] += jnp.sum(dy * n, axis=0, keepdims=True) * scale
            dn = dy * (gain * scale)
            return rs * (dn - n * (_dot_hi_r(dn * n, bd_ref[...]) * (1.0 / HEAD_DIM)))

        dproj_ref[:, O_Q:O_Q + ATT_W] = headnorm_bwd(qr_ref[...], dq_ref[...], qg_ref[...], QK_SCALE, 0).astype(BF16)
        dproj_ref[:, O_K:O_K + ATT_W] = headnorm_bwd(kr_ref[...], dk_ref[...], kg_ref[...], 1.0, 1).astype(BF16)
        dproj_ref[:, O_V:O_V + ATT_W] = dv_ref[...].astype(BF16)
        dproj_ref[:, O_A:O_A + 2 * CONV_CH] = dag_ref[...]

        dc8 = jnp.concatenate([dct_ref[...], jnp.zeros((128 - ATT_HEADS, TM), F32)], axis=0).T
        dc8 = dc8 + _dot_hi_r(drb_ref[...], pick_ref[...])
        dlogf = _dot_hi_l(triu_ref[...], dc8) + carry_ref[...]
        carry_ref[...] = dlogf[0:1, :]
        df = dlogf * _sigmoid(-(fz_ref[...] + bf_ref[...]))
        dproj_ref[:, O_F:O_F + 128] = df.astype(BF16)
        sm_ref[2:3, :] += jnp.sum(df, axis=0, keepdims=True)

        du1 = _dot_nt(dproj_ref[...], win_ref[...])
        xv = x_ref[...]
        r = lax.rsqrt(jnp.mean(xv * xv, axis=-1, keepdims=True) + EPS)
        n1 = xv * r
        t = du1 * g1_ref[...]
        dx_ref[...] = dx2_ref[...] + r * (t - n1 * jnp.mean(t * n1, axis=-1, keepdims=True))
        dg1_ref[0:1, :] += jnp.sum(du1 * n1, axis=0, keepdims=True)

        @pl.when(i == nb - 1)
        def _():
            sm_ref[0:2, :] = _dot_hi_r(gsum_ref[0:8, :], fold_ref[...])[0:2, :]

    rev = lambda w, cb=0: pl.BlockSpec((TM, w), lambda i: (nb - 1 - i, cb))
    full = lambda a: pl.BlockSpec(a.shape, lambda i: (0,) * a.ndim)
    bd, fold, triu = _head_blockdiag(), _head_fold(), _tril(TM).T
    consts = (g1, win, qg, kg, bf, bd, fold, triu, _head_pick())
    return pl.pallas_call(
        body, name="mixer_in_bwd", grid=(nb,),
        in_specs=[rev(D_MODEL), rev(D_MODEL), rev(ATT_W, O_Q // ATT_W), rev(ATT_W, O_K // ATT_W), rev(128, O_F // 128),
                  rev(ATT_W), rev(ATT_W), rev(ATT_W), rev(2 * CONV_CH),
                  pl.BlockSpec((ATT_HEADS, TM), lambda i: (0, nb - 1 - i)), rev(ATT_W)] + [full(a) for a in consts],
        out_specs=[rev(N_INP), rev(D_MODEL), pl.BlockSpec((8, D_MODEL), lambda i: (0, 0)),
                   pl.BlockSpec((8, 128), lambda i: (0, 0))],
        out_shape=[jax.ShapeDtypeStruct((S, N_INP), BF16), jax.ShapeDtypeStruct((S, D_MODEL), F32),
                   jax.ShapeDtypeStruct((8, D_MODEL), F32), jax.ShapeDtypeStruct((8, 128), F32)],
        scratch_shapes=[pltpu.VMEM((1, 128), F32), pltpu.VMEM((8, ATT_W), F32)],
        compiler_params=_params(1),
    )(x, dx2, proj, proj, proj, dq, dk, dv, dag, dct, drb, *consts)


def _layer_fwd(x, early, late):
    p = early(x)
    u1, proj, q, k, v, qa, ka, h0, h1, h3 = _mixer_in_fwd(
        x, p["g1"], p["win"], p["qg"], p["kg"], p["bf"], p["cw"], p["cvb"], p["lng"], p["lnb"])
    att, lse = _attn_fwd(q, qa, k, ka, v)
    p = dict(p, **late(att))
    x2, u2 = _mixer_out_fwd(x, att, h3, p["wo"], p["g2"])
    x3, z, hh = _mlp_fwd(x2, u2, p["w1"], p["w2"])
    saved = dict(x=x, u1=u1, proj=proj, q=q, k=k, v=v, qa=qa, ka=ka, h0=h0, h1=h1, h3=h3, att=att, lse=lse,
                 x2=x2, u2=u2, z=z, hh=hh)
    return x3, saved, p


def _layer_bwd(dx3, s, p):
    dz, dx2, dg2 = _mlp_bwd(dx3, s["z"], s["x2"], p["g2"], p["w1"], p["w2"])
    g_w2 = _matmul_tn(s["hh"], dx3)
    g_w1 = _matmul_tn(s["u2"], dz, col_shards=N_CHIPS)
    datt, dlt, dh1, sm_c = _mixer_out_bwd(dx2, p["wo"], s["att"], s["h1"], p["lng"], p["lnb"])
    g_wo = jnp.concatenate([_matmul_tn(s["att"], dx2)[0], _matmul_tn(s["h3"], dx2)[0]], axis=0)
    dag, dcw = _conv_glu_bwd(dh1, s["h0"], s["proj"], p["cw"])
    dq, dk, dv, dc4, drb = _attn_bwd(s["q"], s["qa"], s["k"], s["ka"], s["v"], datt, s["lse"], dlt)
    dct = dc4.reshape(ATT_HEADS, dc4.shape[2])
    dproj, dx, dg1, sm_a = _mixer_in_bwd(s["x"], dx2, s["proj"], dq, dk, dv, dag, dct, drb,
                                         p["g1"], p["win"], p["qg"], p["kg"], p["bf"])
    g_win = _matmul_tn(s["u1"], dproj)[0]
    big = dict(win=g_win, wo=g_wo, w1=g_w1, w2=g_w2[0])
    small = dict(g1=dg1[0], g2=dg2[0], lng=sm_c[0], lnb=sm_c[1], cvb=sm_c[2], cw=dcw[0:CONV_TAPS],
                 qg=sm_a[0, 0:HEAD_DIM], kg=sm_a[1, 0:HEAD_DIM], bf=sm_a[2, 0:ATT_HEADS])
    return dx, big, small


def _local_step(x, target, weights):
    saved, layers = [], []
    h = x
    for early, late in weights:
        h, s, p = _layer_fwd(h, early, late)
        saved.append(s)
        layers.append(p)
    dy, loss_acc = _loss_fwd_bwd(h, target)
    loss = loss_acc[0, 0] * (0.5 / D_MODEL)
    bigs, smalls = [], []
    d = dy
    for p, s in zip(reversed(layers), reversed(saved)):
        d, big, small = _layer_bwd(d, s, p)
        bigs.append(big)
        smalls.append(small)
    return loss, d, bigs[::-1], smalls[::-1]


def _win_to_internal(w):
    pad = jnp.zeros(w.shape[:-1] + (N_INP - N_IN,), w.dtype)
    return jnp.concatenate([w[..., :1536], w[..., 1544:], w[..., 1536:1544], pad], axis=-1)


def _win_to_global(g):
    return jnp.concatenate([g[..., :1536], g[..., O_F:O_F + ATT_HEADS], g[..., 1536:O_F]], axis=-1)


def _layer_params(l, win, cw, norm1_g, b_f, q_norm_g, k_norm_g, conv_b, conv_ln_g, conv_ln_b, norm2_g):
    row = lambda a: a.reshape(1, -1)
    return dict(
        win=win, cw=jnp.pad(cw, ((0, CONV_HALO - CONV_TAPS), (0, 0))),
        g1=row(norm1_g[l]), g2=row(norm2_g[l]),
        qg=row(jnp.tile(q_norm_g[l], ATT_HEADS)), kg=row(jnp.tile(k_norm_g[l], ATT_HEADS)),
        bf=row(jnp.pad(b_f[l], (0, 128 - ATT_HEADS))),
        cvb=row(conv_b[l]), lng=row(conv_ln_g[l]), lnb=row(conv_ln_b[l]))


def _place():
    x, y, c = lax.axis_index("x"), lax.axis_index("y"), lax.axis_index("c")
    chips = [(1 - x, y), (x, 1 - y), (1 - x, 1 - y)]
    return x, y, c, chips


HBM = pl.BlockSpec(memory_space=pltpu.HBM)
SEM = pl.BlockSpec(memory_space=pltpu.SEMAPHORE)
GATHER_PEERS = N_CHIPS


def _gather_peers():
    x, y, c, chips = _place()
    return [(*chip, c) for chip in chips] + [(x, y, 1 - c)], [2 * px + py for px, py in chips] + [2 * x + y]


def _gather_start(srcs):
    n = len(srcs)

    def body(*refs):
        ins, lands = refs[:n], refs[n:2 * n]
        send_sems, recv_sems, token = refs[2 * n], refs[2 * n + 1], refs[-1]
        me = 2 * lax.axis_index("x") + lax.axis_index("y")
        peers, _ = _gather_peers()
        for g in range(n):
            for j, to in enumerate(peers):
                pltpu.make_async_remote_copy(src_ref=ins[g], dst_ref=lands[g].at[me],
                                             send_sem=send_sems.at[GATHER_PEERS * g + j],
                                             recv_sem=recv_sems.at[GATHER_PEERS * g + j],
                                             device_id=to, device_id_type=MESH).start()
        token[...] = jnp.zeros_like(token)

    lands = [lax.empty((N_CHIPS,) + a.shape, a.dtype) for a in srcs]
    outs = pl.pallas_call(
        body, name="gather_start",
        in_specs=[HBM] * (2 * n),
        out_specs=[SEM, SEM] + [HBM] * (2 * n) + [pl.BlockSpec(memory_space=pltpu.VMEM)],
        out_shape=[pltpu.SemaphoreType.DMA((GATHER_PEERS * n,)), pltpu.SemaphoreType.DMA((GATHER_PEERS * n,))]
        + [pltpu.HBM(a.shape, a.dtype) for a in srcs] + [pltpu.HBM(a.shape, a.dtype) for a in lands]
        + [jax.ShapeDtypeStruct((8, 128), F32)],
        input_output_aliases={i: 2 + i for i in range(2 * n)},
        compiler_params=pltpu.CompilerParams(has_side_effects=pltpu.SideEffectType.DATAFLOW_SIDE_EFFECTING),
    )(*[pltpu.with_memory_space_constraint(a, pltpu.HBM) for a in srcs],
      *[pltpu.with_memory_space_constraint(a, pltpu.HBM) for a in lands])
    return outs[0], outs[1], list(outs[2:2 + n]), list(outs[2 + n:2 + 2 * n]), outs[-1]


def _gather_wait(name, groups, send_sems, recv_sems, srcs, lands, after):
    k = len(groups)

    def body(*refs):
        ins, lnd = refs[:k], refs[k:2 * k]
        ssem, rsem = refs[2 * k], refs[2 * k + 1]
        peers, slots = _gather_peers()
        for i, g in enumerate(groups):
            for j, to in enumerate(peers):
                cp = pltpu.make_async_remote_copy(src_ref=ins[i], dst_ref=lnd[i].at[slots[j]],
                                                  send_sem=ssem.at[GATHER_PEERS * g + j],
                                                  recv_sem=rsem.at[GATHER_PEERS * g + j],
                                                  device_id=to, device_id_type=MESH)
                cp.wait_send()
                cp.wait_recv()

    outs = pl.pallas_call(
        body, name=name,
        in_specs=[HBM] * (2 * k) + [SEM, SEM, ANY],
        out_specs=[HBM] * (2 * k),
        out_shape=[pltpu.HBM(a.shape, a.dtype) for a in srcs] + [pltpu.HBM(a.shape, a.dtype) for a in lands],
        input_output_aliases={i: i for i in range(2 * k)},
        compiler_params=pltpu.CompilerParams(has_side_effects=pltpu.SideEffectType.DATAFLOW_SIDE_EFFECTING),
    )(*srcs, *lands, send_sems, recv_sems, after)
    return list(outs[k:])


def _pair_swap(g0, g1):
    n = len(g0)

    def body(*refs):
        r0, r1, outs = refs[:n], refs[n:2 * n], refs[2 * n:3 * n]
        send_sems, recv_sems = refs[3 * n:]
        x, y, c, _ = _place()
        sibling = (x, y, 1 - c)

        def remote(a, src):
            return pltpu.make_async_remote_copy(src_ref=src, dst_ref=outs[a], send_sem=send_sems.at[a],
                                                recv_sem=recv_sems.at[a], device_id=sibling, device_id_type=MESH)

        @pl.when(c == 0)
        def _():
            for a in range(n):
                remote(a, r1[a]).start()

        @pl.when(c == 1)
        def _():
            for a in range(n):
                remote(a, r0[a]).start()

        for a in range(n):
            remote(a, r0[a]).wait()

    return pl.pallas_call(
        body, name="pair_swap",
        in_specs=[ANY] * (2 * n), out_specs=[ANY] * n,
        out_shape=[jax.ShapeDtypeStruct(a.shape, a.dtype) for a in g0],
        scratch_shapes=[pltpu.SemaphoreType.DMA((n,)), pltpu.SemaphoreType.DMA((n,))],
        compiler_params=pltpu.CompilerParams(has_side_effects=True),
    )(*g0, *g1)


def _shard_tiles(arrays, steps):
    return [a.shape[-2] // steps for a in arrays]


def _pair_add(g0, g1, got):
    n = len(got)
    steps = 16
    tiles = _shard_tiles(got, steps)

    def body(*refs):
        r0, r1, rg, outs = refs[:n], refs[n:2 * n], refs[2 * n:3 * n], refs[3 * n:]
        c = lax.axis_index("c")

        @pl.when(c == 0)
        def _():
            for a in range(n):
                outs[a][...] = (r0[a][...] + rg[a][...]).astype(BF16)

        @pl.when(c == 1)
        def _():
            for a in range(n):
                outs[a][...] = (r1[a][...] + rg[a][...]).astype(BF16)

    specs = [pl.BlockSpec((N_CHIPS, t, a.shape[-1]), lambda i: (0, i, 0)) for a, t in zip(got, tiles)]
    return pl.pallas_call(
        body, name="pair_add", grid=(steps,),
        in_specs=specs * 3, out_specs=specs,
        out_shape=[jax.ShapeDtypeStruct(a.shape, BF16) for a in got],
        compiler_params=_params(1),
    )(*g0, *g1, *got)


def _chip_exchange(parts):
    n = len(parts)

    def body(*refs):
        ins, outs = refs[:n], refs[n:2 * n]
        send_sems, recv_sems = refs[2 * n:]
        x, y, c, chips = _place()

        def remote(a, j, px, py):
            return pltpu.make_async_remote_copy(src_ref=ins[a].at[2 * px + py], dst_ref=outs[a].at[j],
                                                send_sem=send_sems.at[3 * a + j], recv_sem=recv_sems.at[3 * a + j],
                                                device_id=(px, py, c), device_id_type=MESH)

        sends = [remote(a, j, px, py) for a in range(n) for j, (px, py) in enumerate(chips)]
        for cp in sends:
            cp.start()
        for cp in sends:
            cp.wait_recv()
        for cp in sends:
            cp.wait_send()

    return pl.pallas_call(
        body, name="chip_exchange",
        in_specs=[ANY] * n, out_specs=[ANY] * n,
        out_shape=[jax.ShapeDtypeStruct((N_CHIPS - 1,) + a.shape[1:], a.dtype) for a in parts],
        scratch_shapes=[pltpu.SemaphoreType.DMA((3 * n,)), pltpu.SemaphoreType.DMA((3 * n,))],
        compiler_params=pltpu.CompilerParams(has_side_effects=True),
    )(*parts)


def _chip_sum(parts, recv, place):
    n = len(parts)
    steps = 16
    tiles = _shard_tiles(parts, steps)

    def body(place_ref, *refs):
        own, got, outs = refs[:n], refs[n:2 * n], refs[2 * n:]
        for a in range(n):
            tot = own[a][...].astype(F32)
            for j in range(N_CHIPS - 1):
                tot = tot + got[a][j].astype(F32)
            outs[a][...] = tot

    own_specs = [pl.BlockSpec((None, t, a.shape[-1]), lambda i, p: (p[0], i, 0)) for a, t in zip(parts, tiles)]
    got_specs = [pl.BlockSpec((N_CHIPS - 1, t, a.shape[-1]), lambda i, p: (0, i, 0)) for a, t in zip(parts, tiles)]
    out_specs = [pl.BlockSpec((None, t, a.shape[-1]), lambda i, p: (p[1], i, 0)) for a, t in zip(parts, tiles)]
    return pl.pallas_call(
        body, name="chip_sum",
        grid_spec=pltpu.PrefetchScalarGridSpec(num_scalar_prefetch=1, grid=(steps,),
                                               in_specs=own_specs + got_specs, out_specs=out_specs),
        out_shape=[jax.ShapeDtypeStruct((2,) + a.shape[1:], F32) for a in parts],
        compiler_params=_params(1),
    )(place, *parts, *recv)


def _layer_swap(halves):
    n = len(halves)

    def body(*refs):
        ins, outs = refs[:n], refs[n:2 * n]
        send_sems, recv_sems = refs[2 * n:]
        x, y, c, _ = _place()
        sends = [pltpu.make_async_remote_copy(src_ref=ins[a].at[c], dst_ref=outs[a].at[c], send_sem=send_sems.at[a],
                                              recv_sem=recv_sems.at[a], device_id=(x, y, 1 - c), device_id_type=MESH)
                 for a in range(n)]
        for cp in sends:
            cp.start()
        for a in range(n):
            slot = outs[a].at[1 - c]
            pltpu.make_async_remote_copy(src_ref=slot, dst_ref=slot, send_sem=send_sems.at[a], recv_sem=recv_sems.at[a],
                                         device_id=(x, y, 1 - c), device_id_type=MESH).wait_recv()
        for cp in sends:
            cp.wait_send()

    return pl.pallas_call(
        body, name="layer_swap",
        in_specs=[ANY] * n, out_specs=[ANY] * n,
        out_shape=[jax.ShapeDtypeStruct(a.shape, a.dtype) for a in halves],
        input_output_aliases={a: a for a in range(n)},
        scratch_shapes=[pltpu.SemaphoreType.DMA((n,)), pltpu.SemaphoreType.DMA((n,))],
        compiler_params=pltpu.CompilerParams(has_side_effects=True),
    )(*halves)


def _adamw_math(w, g, m, v):
    m = ADAM_B1 * m + (1.0 - ADAM_B1) * g
    v = ADAM_B2 * v + (1.0 - ADAM_B2) * (g * g)
    m_hat = m / (1.0 - ADAM_B1 ** ADAM_STEP)
    v_hat = v / (1.0 - ADAM_B2 ** ADAM_STEP)
    delta = -ADAM_LR * (m_hat / (jnp.sqrt(v_hat) + ADAM_EPS) + ADAM_WD * w)
    return delta, m, v


def _adamw(ws, gs, ms, vs):
    n = len(ws)
    steps = 16
    tiles = _shard_tiles(ws, steps)

    def body(*refs):
        w_r, g_r, m_r, v_r = refs[:n], refs[n:2 * n], refs[2 * n:3 * n], refs[3 * n:4 * n]
        g_o, d_o, m_o, v_o = refs[4 * n:5 * n], refs[5 * n:6 * n], refs[6 * n:7 * n], refs[7 * n:]
        for a in range(n):
            g = g_r[a][...]
            d, m, v = _adamw_math(w_r[a][...], g, m_r[a][...], v_r[a][...])
            g_o[a][...] = g
            d_o[a][...] = d
            m_o[a][...] = m
            v_o[a][...] = v

    specs = [pl.BlockSpec((2, t, a.shape[-1]), lambda i: (0, i, 0)) for a, t in zip(ws, tiles)]
    outs = pl.pallas_call(
        body, name="adamw", grid=(steps,),
        in_specs=specs * 4, out_specs=specs * 4,
        out_shape=[jax.ShapeDtypeStruct(a.shape, F32) for a in ws] * 4,
        compiler_params=_params(1),
    )(*ws, *gs, *ms, *vs)
    return outs[:n], outs[n:2 * n], outs[2 * n:3 * n], outs[3 * n:]


SMALL_W = 512


def _small_allreduce_adamw(g, w, m, v, cw_w, cw_m, cw_v, cw_row0):
    R = g.shape[0]
    n_l = cw_w.shape[0]

    def body(g_ref, w_ref, m_ref, v_ref, cww_ref, cwm_ref, cwv_ref,
             gs_ref, d_ref, mo_ref, vo_ref, cg_ref, cd_ref, cmo_ref, cvo_ref,
             slots_ref, send_sems, recv_sems):
        x, y, c, _ = _place()
        me = 4 * x + 2 * y + c
        slots_ref[me] = g_ref[...]
        sends = []
        for d in range(1, 8):
            px, py, pc = x ^ (d >> 2), y ^ ((d >> 1) & 1), c ^ (d & 1)
            cp = pltpu.make_async_remote_copy(src_ref=g_ref, dst_ref=slots_ref.at[me], send_sem=send_sems.at[d - 1],
                                              recv_sem=recv_sems.at[d - 1], device_id=(px, py, pc), device_id_type=MESH)
            cp.start()
            sends.append(cp)
        for d in range(1, 8):
            px, py, pc = x ^ (d >> 2), y ^ ((d >> 1) & 1), c ^ (d & 1)
            slot = slots_ref.at[4 * px + 2 * py + pc]
            pltpu.make_async_remote_copy(src_ref=slot, dst_ref=slot, send_sem=send_sems.at[d - 1],
                                         recv_sem=recv_sems.at[d - 1], device_id=(px, py, pc),
                                         device_id_type=MESH).wait_recv()
        for cp in sends:
            cp.wait_send()
        tot = slots_ref[0]
        for k in range(1, 8):
            tot = tot + slots_ref[k]
        gs_ref[...] = tot
        dl, mn, vn = _adamw_math(w_ref[...], tot, m_ref[...], v_ref[...])
        d_ref[...] = dl
        mo_ref[...] = mn
        vo_ref[...] = vn
        chip = 2 * x + y
        for l in range(n_l):
            rows = tot[cw_row0[l]:cw_row0[l] + CONV_HALO, :]
            mine = rows[:, 0:128]
            for k in range(1, N_CHIPS):
                mine = jnp.where(chip == k, rows[:, 128 * k:128 * (k + 1)], mine)
            cg_ref[l] = mine
            dl, mn, vn = _adamw_math(cww_ref[l], mine, cwm_ref[l], cwv_ref[l])
            cd_ref[l] = dl
            cmo_ref[l] = mn
            cvo_ref[l] = vn

    vm = pl.BlockSpec(memory_space=pltpu.VMEM)
    small = jax.ShapeDtypeStruct((R, SMALL_W), F32)
    conv = jax.ShapeDtypeStruct(cw_w.shape, F32)
    return pl.pallas_call(
        body, name="small_allreduce_adamw",
        in_specs=[vm] * 7, out_specs=[vm] * 8,
        out_shape=[small] * 4 + [conv] * 4,
        scratch_shapes=[pltpu.VMEM((8, R, SMALL_W), F32), pltpu.SemaphoreType.DMA((7,)), pltpu.SemaphoreType.DMA((7,))],
        compiler_params=pltpu.CompilerParams(has_side_effects=True, vmem_limit_bytes=VMEM_LIMIT),
    )(g, w, m, v, cw_w, cw_m, cw_v)


SMALL_LAYOUT = (("conv_w", CONV_HALO), ("norm1_g", 2), ("norm2_g", 2), ("conv_b", 1), ("conv_ln_g", 1),
                ("conv_ln_b", 1), ("q_norm_g", 1), ("k_norm_g", 1), ("b_f", 1))
SMALL_ROWS = sum(r for _, r in SMALL_LAYOUT)
SMALL_ROWS_PAD = 48
LOSS_ROW = SMALL_ROWS


def _pack_small(per_layer):
    blocks = []
    for d in per_layer:
        rows = []
        for name, r in SMALL_LAYOUT:
            if name == "conv_w":
                a = d.get(name)
                a = jnp.zeros((r, SMALL_W), F32) if a is None else jnp.pad(a, ((0, r - a.shape[0]), (0, 0)))
            else:
                a = d[name].reshape(-1)
                a = jnp.pad(a, (0, r * SMALL_W - a.shape[0])).reshape(r, SMALL_W)
            rows.append(a)
        rows.append(jnp.zeros((SMALL_ROWS_PAD - SMALL_ROWS, SMALL_W), F32))
        blocks.append(jnp.concatenate(rows, axis=0))
    return jnp.concatenate(blocks, axis=0)


def _unpack_small(packed, name, size):
    n_l = packed.shape[0] // SMALL_ROWS_PAD
    row0 = 0
    for nm, r in SMALL_LAYOUT:
        if nm == name:
            break
        row0 += r
    out = [packed[l * SMALL_ROWS_PAD + row0:l * SMALL_ROWS_PAD + row0 + r].reshape(-1)[:size] for l in range(n_l)]
    return jnp.stack(out)


SMALL_SIZES = dict(norm1_g=D_MODEL, norm2_g=D_MODEL, conv_b=CONV_CH, conv_ln_g=CONV_CH, conv_ln_b=CONV_CH,
                   q_norm_g=HEAD_DIM, k_norm_g=HEAD_DIM, b_f=ATT_HEADS)
SMALL_KEYS = dict(norm1_g="g1", norm2_g="g2", conv_b="cvb", conv_ln_g="lng", conv_ln_b="lnb",
                  q_norm_g="qg", k_norm_g="kg", b_f="bf", conv_w="cw")
CONV_W_ROW0 = 0


def kernel(x, norm1_g, w_in, b_f, q_norm_g, k_norm_g, conv_w, conv_b, conv_ln_g, conv_ln_b, w_o, norm2_g, w_mlp_in, w_mlp_out, loss_target, m_norm1_g, m_w_in, m_b_f, m_q_norm_g, m_k_norm_g, m_conv_w, m_conv_b, m_conv_ln_g, m_conv_ln_b, m_w_o, m_norm2_g, m_w_mlp_in, m_w_mlp_out, v_norm1_g, v_w_in, v_b_f, v_q_norm_g, v_k_norm_g, v_conv_w, v_conv_b, v_conv_ln_g, v_conv_ln_b, v_w_o, v_norm2_g, v_w_mlp_in, v_w_mlp_out):
    n_l = w_in.shape[0]
    wide = w_in.shape[2]

    per_layer = lambda l: [w_in[l].astype(BF16), conv_w[l], w_o[l].astype(BF16), w_mlp_in[l].astype(BF16),
                           w_mlp_out[l].astype(BF16)]
    n_w = len(per_layer(0))
    send_sems, recv_sems, srcs, lands, token = _gather_start([a for l in range(n_l) for a in per_layer(l)])

    def layer_weights(l):
        def wait(tag, which, after):
            groups = [n_w * l + i for i in which]
            return _gather_wait(f"gather_wait_{tag}{l}", groups, send_sems, recv_sems,
                                [srcs[g] for g in groups], [lands[g] for g in groups], after)

        def early(after):
            g_in, g_cw = wait("a", (0, 1), token if l == 0 else after)
            win = _win_to_internal(jnp.concatenate([g_in[k] for k in range(N_CHIPS)], axis=-1))
            cw = jnp.concatenate([g_cw[k] for k in range(N_CHIPS)], axis=-1)
            return _layer_params(l, win, cw, norm1_g, b_f, q_norm_g, k_norm_g, conv_b, conv_ln_g, conv_ln_b, norm2_g)

        def late(after):
            wo, w1, w2 = wait("b", (2, 3, 4), after)
            return dict(wo=wo, w1=w1, w2=w2)

        return early, late

    loss, dx, bigs, smalls = _local_step(x[0], loss_target[0], [layer_weights(l) for l in range(n_l)])

    def shard_major(b):
        gin = _win_to_global(b["win"]).reshape(D_MODEL, N_CHIPS, wide).transpose(1, 0, 2)
        return [gin, b["wo"].reshape(N_CHIPS, D_MODEL // N_CHIPS, D_MODEL), b["w1"],
                b["w2"].reshape(N_CHIPS, D_FF // N_CHIPS, D_MODEL)]

    g0, g1 = shard_major(bigs[0]), shard_major(bigs[1])
    got = _pair_swap(g0, g1)
    parts = _pair_add(g0, g1, got)
    recv = _chip_exchange(parts)
    place = jnp.stack([2 * lax.axis_index("x") + lax.axis_index("y"), lax.axis_index("c")]).astype(jnp.int32)
    g_big = _layer_swap(_chip_sum(parts, recv, place))
    big_w = [w_in, w_o, w_mlp_in, w_mlp_out]
    big_m = [m_w_in, m_w_o, m_w_mlp_in, m_w_mlp_out]
    big_v = [v_w_in, v_w_o, v_w_mlp_in, v_w_mlp_out]
    g_big, d_big, nm_big, nv_big = _adamw(big_w, g_big, big_m, big_v)

    env = dict(norm1_g=(norm1_g, m_norm1_g, v_norm1_g), norm2_g=(norm2_g, m_norm2_g, v_norm2_g),
               conv_b=(conv_b, m_conv_b, v_conv_b), conv_ln_g=(conv_ln_g, m_conv_ln_g, v_conv_ln_g),
               conv_ln_b=(conv_ln_b, m_conv_ln_b, v_conv_ln_b), q_norm_g=(q_norm_g, m_q_norm_g, v_q_norm_g),
               k_norm_g=(k_norm_g, m_k_norm_g, v_k_norm_g), b_f=(b_f, m_b_f, v_b_f))
    g_pack = _pack_small([{nm: s[key] for nm, key in SMALL_KEYS.items()} for s in smalls])
    g_pack = g_pack.at[LOSS_ROW, 0].set(loss)
    packs = [_pack_small([{nm: env[nm][t][l] for nm in env} for l in range(n_l)]) for t in range(3)]
    pad_cw = lambda a: jnp.pad(a, ((0, 0), (0, CONV_HALO - CONV_TAPS), (0, 0)))
    cw_row0 = tuple(l * SMALL_ROWS_PAD + CONV_W_ROW0 for l in range(n_l))
    gs, ds, ms, vs, cg, cd, cm, cv = _small_allreduce_adamw(
        g_pack, packs[0], packs[1], packs[2], pad_cw(conv_w), pad_cw(m_conv_w), pad_cw(v_conv_w), cw_row0)

    def small_out(packed, conv):
        o = {nm: _unpack_small(packed, nm, sz) for nm, sz in SMALL_SIZES.items()}
        o["conv_w"] = conv[:, 0:CONV_TAPS, :]
        return o

    def ordered(small, big):
        return (small["norm1_g"], big[0], small["b_f"], small["q_norm_g"], small["k_norm_g"], small["conv_w"],
                small["conv_b"], small["conv_ln_g"], small["conv_ln_b"], big[1], small["norm2_g"], big[2], big[3])

    return (gs[LOSS_ROW, 0], dx[None],
            *ordered(small_out(gs, cg), g_big), *ordered(small_out(ds, cd), d_big),
            *ordered(small_out(ms, cm), nm_big), *ordered(small_out(vs, cv), nv_big))
```

```python
import functools

import jax
import jax.numpy as jnp
from jax import lax
from jax.experimental import pallas as pl
from jax.experimental.pallas import tpu as pltpu

F32 = jnp.float32
BF16 = jnp.bfloat16

D_MODEL = 1024
ATT_HEADS = 8
HEAD_DIM = 64
ATT_W = ATT_HEADS * HEAD_DIM
CONV_CH = 512
CONV_TAPS = 31
CONV_HALO = 32
D_FF = 4 * D_MODEL
N_IN = 3 * ATT_W + ATT_HEADS + 2 * CONV_CH
O_Q, O_K, O_V, O_A, O_G, O_F = 0, 512, 1024, 1536, 2048, 2560
N_INP = O_F + 128
EPS = 1e-6
QK_SCALE = 0.125

ADAM_LR = 0.001
ADAM_B1 = 0.9
ADAM_B2 = 0.999
ADAM_EPS = 1e-08
ADAM_WD = 0.01
ADAM_STEP = 10

N_CHIPS = 4
VMEM_LIMIT = 52 * 1024 * 1024
MESH = pl.DeviceIdType.MESH
ANY = pl.BlockSpec(memory_space=pl.ANY)


def _params(n_axes, **kw):
    return pltpu.CompilerParams(dimension_semantics=("arbitrary",) * n_axes,
                                vmem_limit_bytes=VMEM_LIMIT, **kw)


def _dot(a, b):
    return jnp.dot(a, b, preferred_element_type=F32)


def _dot_nt(a, b):
    return lax.dot_general(a, b, (((1,), (1,)), ((), ())), preferred_element_type=F32)


def _dot_tn(a, b):
    return lax.dot_general(a, b, (((0,), (0,)), ((), ())), preferred_element_type=F32)


def _split3(a):
    a1 = a.astype(BF16)
    r = a - a1.astype(F32)
    a2 = r.astype(BF16)
    a3 = (r - a2.astype(F32)).astype(BF16)
    return a1, a2, a3


def _dot_hi_r(a, b_exact):
    return sum(_dot(p, b_exact) for p in _split3(a))


def _dot_hi_l(a_exact, b):
    return sum(_dot(a_exact, p) for p in _split3(b))


def _sigmoid(x):
    return 1.0 / (1.0 + jnp.exp(-x))


def _head_blockdiag():
    i = jnp.arange(ATT_W) // HEAD_DIM
    return (i[:, None] == i[None, :]).astype(BF16)


AUG_LANES = 8


def _aug_place(first):
    piece = jnp.arange(3 * 128)[:, None] // 128
    h = jnp.arange(3 * 128)[:, None] % 128
    lane = jnp.arange(ATT_W)[None, :]
    return ((h < ATT_HEADS) & (lane == 128 * (h // 2) + AUG_LANES * (h % 2) + first + piece)).astype(BF16)


def _aug_ones(first):
    lane = jnp.arange(ATT_W) % 128
    pos = lane % AUG_LANES
    return ((lane < 2 * AUG_LANES) & (pos >= first) & (pos < first + 3)).astype(F32).reshape(1, ATT_W)


def _head_fold():
    i = jnp.arange(ATT_W)[:, None] % HEAD_DIM
    j = jnp.arange(128)[None, :]
    return (i == j).astype(BF16)


def _head_pick():
    i = jnp.arange(ATT_W)[:, None]
    h = jnp.arange(128)[None, :]
    return (i == h * HEAD_DIM).astype(BF16)


def _tril(n):
    r = jnp.arange(n)
    return (r[:, None] >= r[None, :]).astype(BF16)


SUBLANES = 8


def _fill_row_shifts(buf_ref, shifts_ref, tm):
    n = tm + CONV_HALO - SUBLANES
    for b in range(1, SUBLANES):
        shifts_ref[b - 1, 0:n, :] = buf_ref[pl.ds(b, n), :]


def _row_shifted(buf_ref, shifts_ref, offset, rows, base=0):
    a, b = divmod(offset, SUBLANES)
    start = pl.multiple_of(base + SUBLANES * a, SUBLANES)
    if b == 0:
        return buf_ref[pl.ds(start, rows), :]
    return shifts_ref[b - 1, pl.ds(start, rows), :]


CONV_ROWS = 32


def _mixer_in_fwd(x, g1, win, qg, kg, bf, cw, cvb, lng, lnb):
    S = x.shape[0]
    TM = min(512, S)
    nb = S // TM

    def body(x_ref, g1_ref, win_ref, qg_ref, kg_ref, bf_ref, cw_ref, cvb_ref, lng_ref, lnb_ref,
             bd_ref, tri_ref, pq_ref, pk_ref, oq_ref, ok_ref,
             u1_ref, proj_ref, q_ref, k_ref, v_ref, qa_ref, ka_ref, h0_ref, h1_ref, h3_ref,
             carry_ref, hbuf_ref, hs_ref):
        i = pl.program_id(0)

        @pl.when(i == 0)
        def _():
            carry_ref[...] = jnp.zeros_like(carry_ref)
            hbuf_ref[0:CONV_HALO, :] = jnp.zeros((CONV_HALO, CONV_CH), F32)

        @pl.when(i > 0)
        def _():
            hbuf_ref[0:CONV_HALO, :] = hbuf_ref[TM:TM + CONV_HALO, :]

        xv = x_ref[...]
        r = lax.rsqrt(jnp.mean(xv * xv, axis=-1, keepdims=True) + EPS)
        u = (xv * r * g1_ref[...]).astype(BF16)
        u1_ref[...] = u
        proj_ref[...] = _dot(u, win_ref[...])

        def headnorm(raw, gain):
            ss = _dot_hi_r(raw * raw, bd_ref[...]) * (1.0 / HEAD_DIM)
            return raw * lax.rsqrt(ss + EPS) * gain

        q_ref[...] = (headnorm(proj_ref[:, O_Q:O_Q + ATT_W], qg_ref[...]) * QK_SCALE).astype(BF16)
        k_ref[...] = headnorm(proj_ref[:, O_K:O_K + ATT_W], kg_ref[...]).astype(BF16)
        v_ref[...] = proj_ref[:, O_V:O_V + ATT_W].astype(BF16)

        zf = proj_ref[:, O_F:O_F + 128] + bf_ref[...]
        logf = jnp.minimum(zf, 0.0) - jnp.log(1.0 + jnp.exp(-jnp.abs(zf)))
        lane = lax.broadcasted_iota(jnp.int32, (TM, 128), 1)
        logf = jnp.where(lane < ATT_HEADS, logf, 0.0)
        c8 = _dot_hi_l(tri_ref[...], logf) + carry_ref[...]
        carry_ref[...] = c8[TM - 1:TM, :]
        pieces = jnp.concatenate(_split3(c8), axis=1)
        qa_ref[...] = (_dot(pieces, pq_ref[...]) + oq_ref[...]).astype(BF16)
        ka_ref[...] = (ok_ref[...] - _dot(pieces, pk_ref[...])).astype(BF16)

        h0 = proj_ref[:, O_A:O_A + CONV_CH] * _sigmoid(proj_ref[:, O_G:O_G + CONV_CH])
        h0_ref[...] = h0
        hbuf_ref[CONV_HALO:CONV_HALO + TM, :] = h0
        _fill_row_shifts(hbuf_ref, hs_ref, TM)
        acc = jnp.zeros((TM, CONV_CH), F32) + cvb_ref[...]
        for j in range(CONV_TAPS):
            acc = acc + cw_ref[j:j + 1, :] * _row_shifted(hbuf_ref, hs_ref, CONV_HALO - CONV_TAPS + 1 + j, TM)
        h1_ref[...] = acc
        mu = jnp.mean(acc, axis=-1, keepdims=True)
        d = acc - mu
        var = jnp.mean(d * d, axis=-1, keepdims=True)
        h2 = d * lax.rsqrt(var + EPS) * lng_ref[...] + lnb_ref[...]
        h3_ref[...] = (h2 * _sigmoid(h2)).astype(BF16)

    row = lambda w: pl.BlockSpec((TM, w), lambda i: (i, 0))
    full = lambda a: pl.BlockSpec(a.shape, lambda i: (0,) * a.ndim)
    ins = (x, g1, win, qg, kg, bf, cw, cvb, lng, lnb, _head_blockdiag(), _tril(TM),
           _aug_place(0), _aug_place(3), _aug_ones(3), _aug_ones(0))
    return pl.pallas_call(
        body, name="mixer_in_fwd", grid=(nb,),
        in_specs=[row(D_MODEL)] + [full(a) for a in ins[1:]],
        out_specs=[row(D_MODEL), row(N_INP), row(ATT_W), row(ATT_W), row(ATT_W), row(ATT_W), row(ATT_W),
                   row(CONV_CH), row(CONV_CH), row(CONV_CH)],
        out_shape=[jax.ShapeDtypeStruct((S, D_MODEL), BF16),
                   jax.ShapeDtypeStruct((S, N_INP), F32),
                   jax.ShapeDtypeStruct((S, ATT_W), BF16),
                   jax.ShapeDtypeStruct((S, ATT_W), BF16),
                   jax.ShapeDtypeStruct((S, ATT_W), BF16),
                   jax.ShapeDtypeStruct((S, ATT_W), BF16),
                   jax.ShapeDtypeStruct((S, ATT_W), BF16),
                   jax.ShapeDtypeStruct((S, CONV_CH), F32),
                   jax.ShapeDtypeStruct((S, CONV_CH), F32),
                   jax.ShapeDtypeStruct((S, CONV_CH), BF16)],
        scratch_shapes=[pltpu.VMEM((1, 128), F32), pltpu.VMEM((TM + CONV_HALO, CONV_CH), F32),
                        pltpu.VMEM((SUBLANES - 1, TM + CONV_HALO, CONV_CH), F32)],
        compiler_params=_params(1),
    )(*ins)


def _pair_heads(lo, alo, x, xa):
    z = jnp.zeros_like(x)
    return (jnp.concatenate([jnp.where(lo, x, z), jnp.where(alo, xa, z)], axis=1),
            jnp.concatenate([jnp.where(lo, z, x), jnp.where(alo, z, xa)], axis=1))


def _attn_fwd(q, qa, k, ka, v):
    S = q.shape[0]
    T = min(1024, S)
    nq = S // T

    def body(q_ref, qa_ref, k_ref, ka_ref, v_ref, o_ref, lse_ref):
        qi = pl.program_id(1)
        lane = lax.broadcasted_iota(jnp.int32, (T, 128), 1)
        lo = lane < HEAD_DIM
        qm = _pair_heads(lo, lane < AUG_LANES, q_ref[...], qa_ref[...])
        tril = (lax.broadcasted_iota(jnp.int32, (T, T), 0) >= lax.broadcasted_iota(jnp.int32, (T, T), 1))

        def step(kj, carry, masked):
            off = pl.multiple_of(kj * T, T)
            kb = jnp.concatenate([k_ref[pl.ds(off, T), :], ka_ref[pl.ds(off, T), :]], axis=1)
            vb = v_ref[pl.ds(off, T), :]
            new = []
            for h in range(2):
                m, l, acc = carry[3 * h:3 * h + 3]
                s = _dot_nt(qm[h], kb)
                if masked:
                    s = jnp.where(tril, s, -1e30)
                m_new = jnp.maximum(m, jnp.max(s, axis=-1, keepdims=True))
                alpha = jnp.exp(m - m_new)
                p = jnp.exp(s - m_new)
                l = alpha * l + jnp.sum(p, axis=-1, keepdims=True)
                acc = alpha * acc + _dot(p.astype(BF16), vb)
                new += [m_new, l, acc]
            return tuple(new)

        init = (jnp.full((T, 1), -1e30, F32), jnp.zeros((T, 1), F32), jnp.zeros((T, 128), F32)) * 2
        carry = lax.fori_loop(0, qi, lambda kj, c: step(kj, c, False), init)
        m0, l0, a0, m1, l1, a1 = step(qi, carry, True)
        o_ref[...] = jnp.where(lo, a0 / l0, a1 / l1).astype(BF16)
        lse_ref[...] = jnp.where(lo, m0 + jnp.log(l0), m1 + jnp.log(l1))

    qblk = pl.BlockSpec((T, 128), lambda hp, qi: (qi, hp))
    seq = pl.BlockSpec((S, 128), lambda hp, qi: (0, hp))
    return pl.pallas_call(
        body, name="attn_fwd", grid=(ATT_HEADS // 2, nq),
        in_specs=[qblk, qblk, seq, seq, seq],
        out_specs=[qblk, qblk],
        out_shape=[jax.ShapeDtypeStruct((S, ATT_W), BF16), jax.ShapeDtypeStruct((S, ATT_W), F32)],
        compiler_params=_params(2),
    )(q, qa, k, ka, v)


def _attn_bwd(q, qa, k, ka, v, do, lse, dlt):
    S = q.shape[0]
    T = min(512, S)
    nq = S // T

    def body(q_ref, qa_ref, do_ref, lse_ref, dlt_ref, k_ref, ka_ref, v_ref, dq_ref, dk_ref, dv_ref, dc_ref, dr_ref):
        kj = pl.program_id(1)

        @pl.when(kj == 0)
        def _():
            dq_ref[...] = jnp.zeros_like(dq_ref)
            dr_ref[...] = jnp.zeros_like(dr_ref)

        lane = lax.broadcasted_iota(jnp.int32, (T, 128), 1)
        lo = lane < HEAD_DIM
        alo = lane < AUG_LANES
        tril = (lax.broadcasted_iota(jnp.int32, (T, T), 0) >= lax.broadcasted_iota(jnp.int32, (T, T), 1))
        kb = k_ref[...]
        kaug = jnp.concatenate([kb, ka_ref[...]], axis=1)
        vb = v_ref[...]

        def step(qi, carry, masked):
            off = pl.multiple_of(qi * T, T)
            qb = q_ref[pl.ds(off, T), :]
            dob = do_ref[pl.ds(off, T), :]
            lseb = lse_ref[pl.ds(off, T), :]
            dltb = dlt_ref[pl.ds(off, T), :]
            qm = _pair_heads(lo, alo, qb, qa_ref[pl.ds(off, T), :])
            zero = jnp.zeros_like(qb)
            new, dqs, drs = [], [], []
            for h in range(2):
                dk_a, dv_a, dc_a = carry[3 * h:3 * h + 3]
                cl = HEAD_DIM * h
                dom = jnp.where(lo, dob, zero) if h == 0 else jnp.where(lo, zero, dob)
                s = _dot_nt(qm[h], kaug)
                if masked:
                    s = jnp.where(tril, s, -1e30)
                p = jnp.exp(s - lseb[:, cl:cl + 1])
                dp = _dot_nt(dom, vb)
                ds = p * (dp - dltb[:, cl:cl + 1])
                pb = p.astype(BF16)
                dsb = ds.astype(BF16)
                dv_a = dv_a + _dot_tn(pb, dob)
                dk_a = dk_a + _dot_tn(dsb, qb)
                dc_a = dc_a + jnp.sum(ds, axis=0, keepdims=True)
                dqs.append(_dot(dsb, kb))
                drs.append(jnp.sum(ds, axis=1, keepdims=True))
                new += [dk_a, dv_a, dc_a]
            dq_ref[pl.ds(off, T), :] += jnp.where(lo, dqs[0], dqs[1])
            dr_ref[pl.ds(off, T), :] += jnp.where(lo, drs[0], drs[1])
            return tuple(new)

        init = (jnp.zeros((T, 128), F32), jnp.zeros((T, 128), F32), jnp.zeros((1, T), F32)) * 2
        carry = step(kj, init, True)
        carry = lax.fori_loop(kj + 1, nq, lambda qi, c: step(qi, c, False), carry)
        dk_ref[...] = jnp.where(lo, carry[0], carry[3])
        dv_ref[...] = jnp.where(lo, carry[1], carry[4])
        dc_ref[0:1, :] = -carry[2]
        dc_ref[1:2, :] = -carry[5]

    seq = pl.BlockSpec((S, 128), lambda hp, kj: (0, hp))
    kblk = pl.BlockSpec((T, 128), lambda hp, kj: (kj, hp))
    return pl.pallas_call(
        body, name="attn_bwd", grid=(ATT_HEADS // 2, nq),
        in_specs=[seq, seq, seq, seq, seq, kblk, kblk, kblk],
        out_specs=[seq, kblk, kblk, pl.BlockSpec((None, 2, T), lambda hp, kj: (hp, 0, kj)), seq],
        out_shape=[jax.ShapeDtypeStruct((S, ATT_W), F32), jax.ShapeDtypeStruct((S, ATT_W), F32),
                   jax.ShapeDtypeStruct((S, ATT_W), F32), jax.ShapeDtypeStruct((ATT_HEADS // 2, 2, S), F32),
                   jax.ShapeDtypeStruct((S, ATT_W), F32)],
        compiler_params=_params(2),
    )(q, qa, do, lse, dlt, k, ka, v)


def _wo_spec(wo4):
    return pl.BlockSpec(wo4.shape, lambda i: (0, 0, 0))


def _wo_halves(wo_ref):
    half = N_CHIPS // 2
    return (wo_ref[0:half].reshape(ATT_W, D_MODEL), wo_ref[half:N_CHIPS].reshape(CONV_CH, D_MODEL))


def _mixer_out_fwd(x, att, h3, wo4, g2):
    S = x.shape[0]
    TM = min(512, S)

    def body(x_ref, att_ref, h3_ref, wo_ref, g2_ref, x2_ref, u2_ref):
        wa, wc = _wo_halves(wo_ref)
        x2 = x_ref[...] + _dot(att_ref[...], wa) + _dot(h3_ref[...], wc)
        x2_ref[...] = x2
        r = lax.rsqrt(jnp.mean(x2 * x2, axis=-1, keepdims=True) + EPS)
        u2_ref[...] = (x2 * r * g2_ref[...]).astype(BF16)

    row = lambda w: pl.BlockSpec((TM, w), lambda i: (i, 0))
    full = lambda a: pl.BlockSpec(a.shape, lambda i: (0,) * a.ndim)
    return pl.pallas_call(
        body, name="mixer_out_fwd", grid=(S // TM,),
        in_specs=[row(D_MODEL), row(ATT_W), row(CONV_CH), _wo_spec(wo4), full(g2)],
        out_specs=[row(D_MODEL), row(D_MODEL)],
        out_shape=[jax.ShapeDtypeStruct((S, D_MODEL), F32), jax.ShapeDtypeStruct((S, D_MODEL), BF16)],
        compiler_params=_params(1),
    )(x, att, h3, wo4, g2)


def _mlp_w_specs():
    return [pl.BlockSpec((None, D_MODEL, D_FF // N_CHIPS), lambda i, f: (f, 0, 0)),
            pl.BlockSpec((None, D_FF // N_CHIPS, D_MODEL), lambda i, f: (f, 0, 0))]


def _mlp_fwd(x2, u2, w1, w2):
    S = x2.shape[0]
    TM = min(512, S)
    TF = 1024

    def body(x2_ref, u2_ref, w1_ref, w2_ref, x3_ref, z_ref, hh_ref):
        f = pl.program_id(1)

        @pl.when(f == 0)
        def _():
            x3_ref[...] = x2_ref[...]

        z = _dot(u2_ref[...], w1_ref[...])
        z_ref[...] = z
        zr = jnp.maximum(z, 0.0)
        hh = (zr * zr).astype(BF16)
        hh_ref[...] = hh
        x3_ref[...] += _dot(hh, w2_ref[...])

    return pl.pallas_call(
        body, name="mlp_fwd", grid=(S // TM, D_FF // TF),
        in_specs=[pl.BlockSpec((TM, D_MODEL), lambda i, f: (i, 0)), pl.BlockSpec((TM, D_MODEL), lambda i, f: (i, 0))]
        + _mlp_w_specs(),
        out_specs=[pl.BlockSpec((TM, D_MODEL), lambda i, f: (i, 0)), pl.BlockSpec((TM, TF), lambda i, f: (i, f)),
                   pl.BlockSpec((TM, TF), lambda i, f: (i, f))],
        out_shape=[jax.ShapeDtypeStruct((S, D_MODEL), F32), jax.ShapeDtypeStruct((S, D_FF), F32),
                   jax.ShapeDtypeStruct((S, D_FF), BF16)],
        compiler_params=_params(2),
    )(x2, u2, w1, w2)


def _loss_fwd_bwd(y, t):
    S = y.shape[0]
    TM = min(512, S)

    def body(y_ref, t_ref, dy_ref, loss_ref):
        @pl.when(pl.program_id(0) == 0)
        def _():
            loss_ref[...] = jnp.zeros_like(loss_ref)

        d = y_ref[...] - t_ref[...]
        dy_ref[...] = d * (1.0 / D_MODEL)
        loss_ref[...] += jnp.sum(d * d)

    row = pl.BlockSpec((TM, D_MODEL), lambda i: (i, 0))
    return pl.pallas_call(
        body, name="loss", grid=(S // TM,),
        in_specs=[row, row], out_specs=[row, pl.BlockSpec((8, 128), lambda i: (0, 0))],
        out_shape=[jax.ShapeDtypeStruct((S, D_MODEL), F32), jax.ShapeDtypeStruct((8, 128), F32)],
        compiler_params=_params(1),
    )(y, t)


def _mlp_bwd(dx3, z, x2, g2, w1, w2):
    S = dx3.shape[0]
    TM = min(512, S)
    TF = 1024
    nf = D_FF // TF

    def body(dx3_ref, z_ref, x2_ref, g2_ref, w1_ref, w2_ref, dz_ref, dx2_ref, dg2_ref, du2_ref):
        i = pl.program_id(0)
        f = pl.program_id(1)

        @pl.when((i == 0) & (f == 0))
        def _():
            dg2_ref[...] = jnp.zeros_like(dg2_ref)

        @pl.when(f == 0)
        def _():
            du2_ref[...] = jnp.zeros_like(du2_ref)

        dhh = _dot_nt(dx3_ref[...].astype(BF16), w2_ref[...])
        dz = (dhh * (2.0 * jnp.maximum(z_ref[...], 0.0))).astype(BF16)
        dz_ref[...] = dz
        du2_ref[...] += _dot_nt(dz, w1_ref[...])

        @pl.when(f == nf - 1)
        def _():
            x2 = x2_ref[...]
            r = lax.rsqrt(jnp.mean(x2 * x2, axis=-1, keepdims=True) + EPS)
            n = x2 * r
            du2 = du2_ref[...]
            t = du2 * g2_ref[...]
            dx2_ref[...] = dx3_ref[...] + r * (t - n * jnp.mean(t * n, axis=-1, keepdims=True))
            dg2_ref[0:1, :] += jnp.sum(du2 * n, axis=0, keepdims=True)

    rowi = pl.BlockSpec((TM, D_MODEL), lambda i, f: (i, 0))
    return pl.pallas_call(
        body, name="mlp_bwd", grid=(S // TM, nf),
        in_specs=[rowi, pl.BlockSpec((TM, TF), lambda i, f: (i, f)), rowi,
                  pl.BlockSpec((1, D_MODEL), lambda i, f: (0, 0))] + _mlp_w_specs(),
        out_specs=[pl.BlockSpec((TM, TF), lambda i, f: (i, f)), rowi, pl.BlockSpec((8, D_MODEL), lambda i, f: (0, 0))],
        out_shape=[jax.ShapeDtypeStruct((S, D_FF), BF16), jax.ShapeDtypeStruct((S, D_MODEL), F32),
                   jax.ShapeDtypeStruct((8, D_MODEL), F32)],
        scratch_shapes=[pltpu.VMEM((TM, D_MODEL), F32)],
        compiler_params=_params(2),
    )(dx3, z, x2, g2, w1, w2)


def _matmul_tn(a, b, col_shards=1):
    S, I = a.shape
    J = b.shape[1]
    TI = min(I, 1024)
    TJ = 1024 if J % 1024 == 0 else 896
    TS = min(S, 1024)
    nk = S // TS
    per = J // col_shards // TJ

    def body(a_ref, b_ref, o_ref):
        @pl.when(pl.program_id(2) == 0)
        def _():
            o_ref[...] = jnp.zeros_like(o_ref)

        o_ref[...] += _dot_tn(a_ref[...].astype(BF16), b_ref[...].astype(BF16))

    return pl.pallas_call(
        body, name="matmul_tn", grid=(I // TI, J // TJ, nk),
        in_specs=[pl.BlockSpec((TS, TI), lambda i, j, k: (k, i)), pl.BlockSpec((TS, TJ), lambda i, j, k: (k, j))],
        out_specs=pl.BlockSpec((None, TI, TJ), lambda i, j, k: (j // per, i, j % per)),
        out_shape=jax.ShapeDtypeStruct((col_shards, I, J // col_shards), F32),
        compiler_params=_params(3),
    )(a, b)


def _mixer_out_bwd(dx2, wo4, att, h1, lng, lnb):
    S = dx2.shape[0]
    TM = min(512, S)

    def body(dx2_ref, wo_ref, att_ref, h1_ref, lng_ref, lnb_ref, bd_ref, datt_ref, dlt_ref, dh1_ref, sm_ref):
        @pl.when(pl.program_id(0) == 0)
        def _():
            sm_ref[...] = jnp.zeros_like(sm_ref)

        dxb = dx2_ref[...].astype(BF16)
        wa, wc = _wo_halves(wo_ref)
        datt = _dot_nt(dxb, wa)
        datt_ref[...] = datt.astype(BF16)
        dlt_ref[...] = _dot_hi_r(datt * att_ref[...].astype(F32), bd_ref[...])
        dh3 = _dot_nt(dxb, wc)
        h1 = h1_ref[...]
        mu = jnp.mean(h1, axis=-1, keepdims=True)
        d = h1 - mu
        rstd = lax.rsqrt(jnp.mean(d * d, axis=-1, keepdims=True) + EPS)
        n = d * rstd
        h2 = n * lng_ref[...] + lnb_ref[...]
        sg = _sigmoid(h2)
        dh2 = dh3 * (sg * (1.0 + h2 * (1.0 - sg)))
        dn = dh2 * lng_ref[...]
        dh1 = rstd * (dn - jnp.mean(dn, axis=-1, keepdims=True) - n * jnp.mean(dn * n, axis=-1, keepdims=True))
        dh1_ref[...] = dh1
        sm_ref[0:1, :] += jnp.sum(dh2 * n, axis=0, keepdims=True)
        sm_ref[1:2, :] += jnp.sum(dh2, axis=0, keepdims=True)
        sm_ref[2:3, :] += jnp.sum(dh1, axis=0, keepdims=True)

    row = lambda w: pl.BlockSpec((TM, w), lambda i: (i, 0))
    full = lambda a: pl.BlockSpec(a.shape, lambda i: (0,) * a.ndim)
    bd = _head_blockdiag()
    return pl.pallas_call(
        body, name="mixer_out_bwd", grid=(S // TM,),
        in_specs=[row(D_MODEL), _wo_spec(wo4), row(ATT_W), row(CONV_CH), full(lng), full(lnb), full(bd)],
        out_specs=[row(ATT_W), row(ATT_W), row(CONV_CH), pl.BlockSpec((8, CONV_CH), lambda i: (0, 0))],
        out_shape=[jax.ShapeDtypeStruct((S, ATT_W), BF16), jax.ShapeDtypeStruct((S, ATT_W), F32),
                   jax.ShapeDtypeStruct((S, CONV_CH), F32), jax.ShapeDtypeStruct((8, CONV_CH), F32)],
        compiler_params=_params(1),
    )(dx2, wo4, att, h1, lng, lnb, bd)


def _conv_glu_bwd(dh1, h0, proj, cw):
    S = dh1.shape[0]
    TM = min(512, S)
    nb = S // TM
    lead = CONV_HALO - CONV_TAPS + 1

    def body(dh1_ref, dnx_ref, h0_ref, hpv_ref, a_ref, g_ref, cw_ref, dag_ref, dcw_ref,
             dbuf_ref, hbuf_ref, ds_ref, hs_ref, dh0_ref, dcw8_ref):
        i = pl.program_id(0)

        @pl.when(i == 0)
        def _():
            dcw8_ref[...] = jnp.zeros_like(dcw8_ref)

        dbuf_ref[0:TM, :] = dh1_ref[...]
        dbuf_ref[TM:TM + CONV_HALO, :] = jnp.where(i < nb - 1, dnx_ref[0:CONV_HALO, :], 0.0)
        hbuf_ref[0:CONV_HALO, :] = jnp.where(i > 0, hpv_ref[TM - CONV_HALO:TM, :], 0.0)
        hbuf_ref[CONV_HALO:CONV_HALO + TM, :] = h0_ref[...]
        _fill_row_shifts(dbuf_ref, ds_ref, TM)
        _fill_row_shifts(hbuf_ref, hs_ref, TM)

        def conv_rows(step, _):
            r0 = pl.multiple_of(step * CONV_ROWS, CONV_ROWS)
            dh1 = dbuf_ref[pl.ds(r0, CONV_ROWS), :]
            part = jnp.zeros((CONV_ROWS, CONV_CH), F32)
            for j in range(CONV_TAPS):
                part = part + cw_ref[j:j + 1, :] * _row_shifted(dbuf_ref, ds_ref, CONV_TAPS - 1 - j, CONV_ROWS, r0)
                prod = dh1 * _row_shifted(hbuf_ref, hs_ref, lead + j, CONV_ROWS, r0)
                dcw8_ref[j] += jnp.sum(prod.reshape(CONV_ROWS // SUBLANES, SUBLANES, CONV_CH), axis=0)
            dh0_ref[pl.ds(r0, CONV_ROWS), :] = part
            return 0

        lax.fori_loop(0, TM // CONV_ROWS, conv_rows, 0)

        @pl.when(i == nb - 1)
        def _():
            dcw_ref[...] = jnp.sum(dcw8_ref[...], axis=1)

        dh0 = dh0_ref[...]
        sg = _sigmoid(g_ref[...])
        dag_ref[:, 0:CONV_CH] = (dh0 * sg).astype(BF16)
        dag_ref[:, CONV_CH:2 * CONV_CH] = (dh0 * a_ref[...] * sg * (1.0 - sg)).astype(BF16)

    blk = lambda fn: pl.BlockSpec((TM, CONV_CH), fn)
    return pl.pallas_call(
        body, name="conv_glu_bwd", grid=(nb,),
        in_specs=[blk(lambda i: (i, 0)), blk(lambda i: (jnp.minimum(i + 1, nb - 1), 0)),
                  blk(lambda i: (i, 0)), blk(lambda i: (jnp.maximum(i - 1, 0), 0)),
                  blk(lambda i: (i, O_A // CONV_CH)), blk(lambda i: (i, O_G // CONV_CH)),
                  pl.BlockSpec(cw.shape, lambda i: (0, 0))],
        out_specs=[pl.BlockSpec((TM, 2 * CONV_CH), lambda i: (i, 0)), pl.BlockSpec((CONV_HALO, CONV_CH), lambda i: (0, 0))],
        out_shape=[jax.ShapeDtypeStruct((S, 2 * CONV_CH), BF16), jax.ShapeDtypeStruct((CONV_HALO, CONV_CH), F32)],
        scratch_shapes=[pltpu.VMEM((TM + CONV_HALO, CONV_CH), F32), pltpu.VMEM((TM + CONV_HALO, CONV_CH), F32),
                        pltpu.VMEM((SUBLANES - 1, TM + CONV_HALO, CONV_CH), F32),
                        pltpu.VMEM((SUBLANES - 1, TM + CONV_HALO, CONV_CH), F32),
                        pltpu.VMEM((TM, CONV_CH), F32), pltpu.VMEM((CONV_HALO, SUBLANES, CONV_CH), F32)],
        compiler_params=_params(1),
    )(dh1, dh1, h0, h0, proj, proj, cw)


def _mixer_in_bwd(x, dx2, proj, dq, dk, dv, dag, dct, drb, g1, win, qg, kg, bf):
    S = x.shape[0]
    TM = min(256, S)
    nb = S // TM

    def body(x_ref, dx2_ref, qr_ref, kr_ref, fz_ref, dq_ref, dk_ref, dv_ref, dag_ref, dct_ref, drb_ref,
             g1_ref, win_ref, qg_ref, kg_ref, bf_ref, bd_ref, fold_ref, triu_ref, pick_ref,
             dproj_ref, dx_ref, dg1_ref, sm_ref, carry_ref, gsum_ref):
        i = pl.program_id(0)

        @pl.when(i == 0)
        def _():
            carry_ref[...] = jnp.zeros_like(carry_ref)
            gsum_ref[...] = jnp.zeros_like(gsum_ref)
            dg1_ref[...] = jnp.zeros_like(dg1_ref)
            sm_ref[...] = jnp.zeros_like(sm_ref)

        def headnorm_bwd(raw, dy, gain, scale, row):
            rs = lax.rsqrt(_dot_hi_r(raw * raw, bd_ref[...]) * (1.0 / HEAD_DIM) + EPS)
            n = raw * rs
            gsum_ref[row:row + 1, :] += jnp.sum(dy * n, axis=0, keepdims=True) * scale
            dn = dy * (gain * scale)
            return rs * (dn - n * (_dot_hi_r(dn * n, bd_ref[...]) * (1.0 / HEAD_DIM)))

        dproj_ref[:, O_Q:O_Q + ATT_W] = headnorm_bwd(qr_ref[...], dq_ref[...], qg_ref[...], QK_SCALE, 0).astype(BF16)
        dproj_ref[:, O_K:O_K + ATT_W] = headnorm_bwd(kr_ref[...], dk_ref[...], kg_ref[...], 1.0, 1).astype(BF16)
        dproj_ref[:, O_V:O_V + ATT_W] = dv_ref[...].astype(BF16)
        dproj_ref[:, O_A:O_A + 2 * CONV_CH] = dag_ref[...]

        dc8 = jnp.concatenate([dct_ref[...], jnp.zeros((128 - ATT_HEADS, TM), F32)], axis=0).T
        dc8 = dc8 + _dot_hi_r(drb_ref[...], pick_ref[...])
        dlogf = _dot_hi_l(triu_ref[...], dc8) + carry_ref[...]
        carry_ref[...] = dlogf[0:1, :]
        df = dlogf * _sigmoid(-(fz_ref[...] + bf_ref[...]))
        dproj_ref[:, O_F:O_F + 128] = df.astype(BF16)
        sm_ref[2:3, :] += jnp.sum(df, axis=0, keepdims=True)

        du1 = _dot_nt(dproj_ref[...], win_ref[...])
        xv = x_ref[...]
        r = lax.rsqrt(jnp.mean(xv * xv, axis=-1, keepdims=True) + EPS)
        n1 = xv * r
        t = du1 * g1_ref[...]
        dx_ref[...] = dx2_ref[...] + r * (t - n1 * jnp.mean(t * n1, axis=-1, keepdims=True))
        dg1_ref[0:1, :] += jnp.sum(du1 * n1, axis=0, keepdims=True)

        @pl.when(i == nb - 1)
        def _():
            sm_ref[0:2, :] = _dot_hi_r(gsum_ref[0:8, :], fold_ref[...])[0:2, :]

    rev = lambda w, cb=0: pl.BlockSpec((TM, w), lambda i: (nb - 1 - i, cb))
    full = lambda a: pl.BlockSpec(a.shape, lambda i: (0,) * a.ndim)
    bd, fold, triu = _head_blockdiag(), _head_fold(), _tril(TM).T
    consts = (g1, win, qg, kg, bf, bd, fold, triu, _head_pick())
    return pl.pallas_call(
        body, name="mixer_in_bwd", grid=(nb,),
        in_specs=[rev(D_MODEL), rev(D_MODEL), rev(ATT_W, O_Q // ATT_W), rev(ATT_W, O_K // ATT_W), rev(128, O_F // 128),
                  rev(ATT_W), rev(ATT_W), rev(ATT_W), rev(2 * CONV_CH),
                  pl.BlockSpec((ATT_HEADS, TM), lambda i: (0, nb - 1 - i)), rev(ATT_W)] + [full(a) for a in consts],
        out_specs=[rev(N_INP), rev(D_MODEL), pl.BlockSpec((8, D_MODEL), lambda i: (0, 0)),
                   pl.BlockSpec((8, 128), lambda i: (0, 0))],
        out_shape=[jax.ShapeDtypeStruct((S, N_INP), BF16), jax.ShapeDtypeStruct((S, D_MODEL), F32),
                   jax.ShapeDtypeStruct((8, D_MODEL), F32), jax.ShapeDtypeStruct((8, 128), F32)],
        scratch_shapes=[pltpu.VMEM((1, 128), F32), pltpu.VMEM((8, ATT_W), F32)],
        compiler_params=_params(1),
    )(x, dx2, proj, proj, proj, dq, dk, dv, dag, dct, drb, *consts)


def _layer_fwd(x, early, late):
    p = early(x)
    u1, proj, q, k, v, qa, ka, h0, h1, h3 = _mixer_in_fwd(
        x, p["g1"], p["win"], p["qg"], p["kg"], p["bf"], p["cw"], p["cvb"], p["lng"], p["lnb"])
    att, lse = _attn_fwd(q, qa, k, ka, v)
    p = dict(p, **late(att))
    x2, u2 = _mixer_out_fwd(x, att, h3, p["wo"], p["g2"])
    x3, z, hh = _mlp_fwd(x2, u2, p["w1"], p["w2"])
    saved = dict(x=x, u1=u1, proj=proj, q=q, k=k, v=v, qa=qa, ka=ka, h0=h0, h1=h1, h3=h3, att=att, lse=lse,
                 x2=x2, u2=u2, z=z, hh=hh)
    return x3, saved, p


def _tie(a, token):
    return a if token is None else a + token[0:1, 0:1]


def _layer_bwd(dx3, s, p, reduce):
    dz, dx2, dg2 = _mlp_bwd(dx3, s["z"], s["x2"], p["g2"], p["w1"], p["w2"])
    g_w2 = _matmul_tn(s["hh"], dx3)
    g_w1 = _matmul_tn(s["u2"], dz, col_shards=N_CHIPS)
    token = reduce("a", {2: g_w1, 3: g_w2.reshape(N_CHIPS, D_FF // N_CHIPS, D_MODEL)})
    datt, dlt, dh1, sm_c = _mixer_out_bwd(dx2, p["wo"], s["att"], s["h1"], _tie(p["lng"], token), p["lnb"])
    g_wo = jnp.concatenate([_matmul_tn(s["att"], dx2)[0], _matmul_tn(s["h3"], dx2)[0]], axis=0)
    dag, dcw = _conv_glu_bwd(dh1, s["h0"], s["proj"], p["cw"])
    dq, dk, dv, dc4, drb = _attn_bwd(s["q"], s["qa"], s["k"], s["ka"], s["v"], datt, s["lse"], dlt)
    dct = dc4.reshape(ATT_HEADS, dc4.shape[2])
    dproj, dx, dg1, sm_a = _mixer_in_bwd(s["x"], dx2, s["proj"], dq, dk, dv, dag, dct, drb,
                                         p["g1"], p["win"], p["qg"], p["kg"], p["bf"])
    g_win = _win_to_global(_matmul_tn(s["u1"], dproj)[0])
    g_win = g_win.reshape(D_MODEL, N_CHIPS, N_IN // N_CHIPS).transpose(1, 0, 2)
    token = reduce("b", {0: g_win, 1: g_wo.reshape(N_CHIPS, D_MODEL // N_CHIPS, D_MODEL)})
    small = dict(g1=dg1[0], g2=dg2[0], lng=sm_c[0], lnb=sm_c[1], cvb=sm_c[2], cw=dcw[0:CONV_TAPS],
                 qg=sm_a[0, 0:HEAD_DIM], kg=sm_a[1, 0:HEAD_DIM], bf=sm_a[2, 0:ATT_HEADS])
    return dx, small, token


def _local_step(x, target, weights, reduce):
    saved, layers = [], []
    h = x
    for early, late in weights:
        h, s, p = _layer_fwd(h, early, late)
        saved.append(s)
        layers.append(p)
    dy, loss_acc = _loss_fwd_bwd(h, target)
    loss = loss_acc[0, 0] * (0.5 / D_MODEL)
    smalls = []
    d, token = dy, None
    for l in reversed(range(len(layers))):
        d, small, token = _layer_bwd(d, saved[l], dict(layers[l], g2=_tie(layers[l]["g2"], token)), reduce(l))
        smalls.append(small)
    return loss, d, smalls[::-1]


def _win_to_internal(w):
    pad = jnp.zeros(w.shape[:-1] + (N_INP - N_IN,), w.dtype)
    return jnp.concatenate([w[..., :1536], w[..., 1544:], w[..., 1536:1544], pad], axis=-1)


def _win_to_global(g):
    return jnp.concatenate([g[..., :1536], g[..., O_F:O_F + ATT_HEADS], g[..., 1536:O_F]], axis=-1)


def _layer_params(l, win, cw, norm1_g, b_f, q_norm_g, k_norm_g, conv_b, conv_ln_g, conv_ln_b, norm2_g):
    row = lambda a: a.reshape(1, -1)
    return dict(
        win=win, cw=jnp.pad(cw, ((0, CONV_HALO - CONV_TAPS), (0, 0))),
        g1=row(norm1_g[l]), g2=row(norm2_g[l]),
        qg=row(jnp.tile(q_norm_g[l], ATT_HEADS)), kg=row(jnp.tile(k_norm_g[l], ATT_HEADS)),
        bf=row(jnp.pad(b_f[l], (0, 128 - ATT_HEADS))),
        cvb=row(conv_b[l]), lng=row(conv_ln_g[l]), lnb=row(conv_ln_b[l]))


def _place():
    x, y, c = lax.axis_index("x"), lax.axis_index("y"), lax.axis_index("c")
    chips = [(1 - x, y), (x, 1 - y), (1 - x, 1 - y)]
    return x, y, c, chips


HBM = pl.BlockSpec(memory_space=pltpu.HBM)
SEM = pl.BlockSpec(memory_space=pltpu.SEMAPHORE)
GATHER_PEERS = N_CHIPS


def _gather_peers():
    x, y, c, chips = _place()
    return [(*chip, c) for chip in chips] + [(x, y, 1 - c)], [2 * px + py for px, py in chips] + [2 * x + y]


def _gather_start(srcs):
    n = len(srcs)

    def body(*refs):
        ins, lands = refs[:n], refs[n:2 * n]
        send_sems, recv_sems, token = refs[2 * n], refs[2 * n + 1], refs[-1]
        me = 2 * lax.axis_index("x") + lax.axis_index("y")
        peers, _ = _gather_peers()
        for g in range(n):
            for j, to in enumerate(peers):
                pltpu.make_async_remote_copy(src_ref=ins[g], dst_ref=lands[g].at[me],
                                             send_sem=send_sems.at[GATHER_PEERS * g + j],
                                             recv_sem=recv_sems.at[GATHER_PEERS * g + j],
                                             device_id=to, device_id_type=MESH).start()
        token[...] = jnp.zeros_like(token)

    lands = [lax.empty((N_CHIPS,) + a.shape, a.dtype) for a in srcs]
    outs = pl.pallas_call(
        body, name="gather_start",
        in_specs=[HBM] * (2 * n),
        out_specs=[SEM, SEM] + [HBM] * (2 * n) + [pl.BlockSpec(memory_space=pltpu.VMEM)],
        out_shape=[pltpu.SemaphoreType.DMA((GATHER_PEERS * n,)), pltpu.SemaphoreType.DMA((GATHER_PEERS * n,))]
        + [pltpu.HBM(a.shape, a.dtype) for a in srcs] + [pltpu.HBM(a.shape, a.dtype) for a in lands]
        + [jax.ShapeDtypeStruct((8, 128), F32)],
        input_output_aliases={i: 2 + i for i in range(2 * n)},
        compiler_params=pltpu.CompilerParams(has_side_effects=pltpu.SideEffectType.DATAFLOW_SIDE_EFFECTING),
    )(*[pltpu.with_memory_space_constraint(a, pltpu.HBM) for a in srcs],
      *[pltpu.with_memory_space_constraint(a, pltpu.HBM) for a in lands])
    return outs[0], outs[1], list(outs[2:2 + n]), list(outs[2 + n:2 + 2 * n]), outs[-1]


def _gather_wait(name, groups, send_sems, recv_sems, srcs, lands, after):
    k = len(groups)

    def body(*refs):
        ins, lnd = refs[:k], refs[k:2 * k]
        ssem, rsem = refs[2 * k], refs[2 * k + 1]
        peers, slots = _gather_peers()
        for i, g in enumerate(groups):
            for j, to in enumerate(peers):
                cp = pltpu.make_async_remote_copy(src_ref=ins[i], dst_ref=lnd[i].at[slots[j]],
                                                  send_sem=ssem.at[GATHER_PEERS * g + j],
                                                  recv_sem=rsem.at[GATHER_PEERS * g + j],
                                                  device_id=to, device_id_type=MESH)
                cp.wait_send()
                cp.wait_recv()

    outs = pl.pallas_call(
        body, name=name,
        in_specs=[HBM] * (2 * k) + [SEM, SEM, ANY],
        out_specs=[HBM] * (2 * k),
        out_shape=[pltpu.HBM(a.shape, a.dtype) for a in srcs] + [pltpu.HBM(a.shape, a.dtype) for a in lands],
        input_output_aliases={i: i for i in range(2 * k)},
        compiler_params=pltpu.CompilerParams(has_side_effects=pltpu.SideEffectType.DATAFLOW_SIDE_EFFECTING),
    )(*srcs, *lands, send_sems, recv_sems, after)
    return list(outs[k:])


REDUCE_STEPS = 8


def _row_half(a):
    return a.shape[-2] // 2


def _half_swap(name, gs):
    n = len(gs)

    def body(*refs):
        ins, outs = refs[:n], refs[n:2 * n]
        send_sems, recv_sems = refs[2 * n:]
        x, y, c, _ = _place()
        copies = []
        for a in range(n):
            h = _row_half(gs[a])
            copies.append(pltpu.make_async_remote_copy(
                src_ref=ins[a].at[:, pl.ds((1 - c) * h, h), :], dst_ref=outs[a], send_sem=send_sems.at[a],
                recv_sem=recv_sems.at[a], device_id=(x, y, 1 - c), device_id_type=MESH))
        for cp in copies:
            cp.start()
        for cp in copies:
            cp.wait()

    return pl.pallas_call(
        body, name=name,
        in_specs=[ANY] * n, out_specs=[ANY] * n,
        out_shape=[jax.ShapeDtypeStruct((N_CHIPS, _row_half(a), a.shape[-1]), a.dtype) for a in gs],
        scratch_shapes=[pltpu.SemaphoreType.DMA((n,)), pltpu.SemaphoreType.DMA((n,))],
        compiler_params=pltpu.CompilerParams(has_side_effects=True),
    )(*gs)


def _half_specs(gs, row_block):
    tiles = [_row_half(a) // REDUCE_STEPS for a in gs]
    return [pl.BlockSpec((N_CHIPS, t, a.shape[-1]), lambda i, p: (0, row_block(i, p), 0)) for a, t in zip(gs, tiles)]


def _half_add(name, gs, got, place):
    n = len(gs)

    def body(place_ref, *refs):
        own, theirs, outs = refs[:n], refs[n:2 * n], refs[2 * n:]
        for a in range(n):
            outs[a][...] = (own[a][...] + theirs[a][...]).astype(BF16)

    plain = _half_specs(gs, lambda i, p: i)
    return pl.pallas_call(
        body, name=name,
        grid_spec=pltpu.PrefetchScalarGridSpec(
            num_scalar_prefetch=1, grid=(REDUCE_STEPS,),
            in_specs=_half_specs(gs, lambda i, p: p[1] * REDUCE_STEPS + i) + plain, out_specs=plain),
        out_shape=[jax.ShapeDtypeStruct(a.shape, BF16) for a in got],
        compiler_params=_params(1),
    )(place, *gs, *got)


def _exchange_copies(parts, lands, send_sems, recv_sems):
    x, y, c, chips = _place()
    return [pltpu.make_async_remote_copy(src_ref=parts[a].at[2 * px + py], dst_ref=lands[a].at[j],
                                         send_sem=send_sems.at[3 * a + j], recv_sem=recv_sems.at[3 * a + j],
                                         device_id=(px, py, c), device_id_type=MESH)
            for a in range(len(parts)) for j, (px, py) in enumerate(chips)]


def _exchange_start(name, parts):
    n = len(parts)

    def body(*refs):
        _ = [cp.start() for cp in _exchange_copies(refs[:n], refs[n:2 * n], refs[2 * n], refs[2 * n + 1])]
        refs[-1][...] = jnp.zeros_like(refs[-1])

    lands = [lax.empty((N_CHIPS - 1,) + a.shape[1:], a.dtype) for a in parts]
    outs = pl.pallas_call(
        body, name=name,
        in_specs=[HBM] * (2 * n),
        out_specs=[SEM, SEM] + [HBM] * (2 * n) + [pl.BlockSpec(memory_space=pltpu.VMEM)],
        out_shape=[pltpu.SemaphoreType.DMA((3 * n,)), pltpu.SemaphoreType.DMA((3 * n,))]
        + [pltpu.HBM(a.shape, a.dtype) for a in parts] + [pltpu.HBM(a.shape, a.dtype) for a in lands]
        + [jax.ShapeDtypeStruct((8, 128), F32)],
        input_output_aliases={i: 2 + i for i in range(2 * n)},
        compiler_params=pltpu.CompilerParams(has_side_effects=pltpu.SideEffectType.DATAFLOW_SIDE_EFFECTING),
    )(*[pltpu.with_memory_space_constraint(a, pltpu.HBM) for a in parts],
      *[pltpu.with_memory_space_constraint(a, pltpu.HBM) for a in lands])
    return outs[0], outs[1], list(outs[2:2 + n]), list(outs[2 + n:2 + 2 * n]), outs[-1]


def _exchange_wait(name, send_sems, recv_sems, parts, lands, after):
    n = len(parts)

    def body(*refs):
        for cp in _exchange_copies(refs[:n], refs[n:2 * n], refs[2 * n], refs[2 * n + 1]):
            cp.wait_send()
            cp.wait_recv()

    outs = pl.pallas_call(
        body, name=name,
        in_specs=[HBM] * (2 * n) + [SEM, SEM, ANY],
        out_specs=[HBM] * (2 * n),
        out_shape=[pltpu.HBM(a.shape, a.dtype) for a in parts] + [pltpu.HBM(a.shape, a.dtype) for a in lands],
        input_output_aliases={i: i for i in range(2 * n)},
        compiler_params=pltpu.CompilerParams(has_side_effects=pltpu.SideEffectType.DATAFLOW_SIDE_EFFECTING),
    )(*parts, *lands, send_sems, recv_sems, after)
    return list(outs[:n]), list(outs[n:])


def _chip_sum(name, parts, lands, sums, place, layer):
    n = len(parts)
    tiles = [a.shape[-2] // REDUCE_STEPS for a in parts]

    def body(place_ref, *refs):
        own, got, outs = refs[:n], refs[n:2 * n], refs[3 * n:]
        for a in range(n):
            tot = own[a][...].astype(F32)
            for j in range(N_CHIPS - 1):
                tot = tot + got[a][j].astype(F32)
            outs[a][...] = tot

    own_specs = [pl.BlockSpec((None, t, a.shape[-1]), lambda i, p: (p[0], i, 0)) for a, t in zip(parts, tiles)]
    got_specs = [pl.BlockSpec((N_CHIPS - 1, t, a.shape[-1]), lambda i, p: (0, i, 0)) for a, t in zip(parts, tiles)]
    out_specs = [pl.BlockSpec((None, t, a.shape[-1]), lambda i, p: (layer, p[1] * REDUCE_STEPS + i, 0))
                 for a, t in zip(parts, tiles)]
    return pl.pallas_call(
        body, name=name,
        grid_spec=pltpu.PrefetchScalarGridSpec(num_scalar_prefetch=1, grid=(REDUCE_STEPS,),
                                               in_specs=own_specs + got_specs + [ANY] * n, out_specs=out_specs),
        out_shape=[jax.ShapeDtypeStruct(a.shape, F32) for a in sums],
        input_output_aliases={1 + 2 * n + a: a for a in range(n)},
        compiler_params=_params(1),
    )(place, *parts, *lands, *sums)


def _half_fill(sums):
    n = len(sums)

    def body(*refs):
        ins, outs = refs[:n], refs[n:2 * n]
        send_sems, recv_sems = refs[2 * n:]
        x, y, c, _ = _place()
        copies = []
        for a in range(n):
            h = _row_half(sums[a])
            copies.append(pltpu.make_async_remote_copy(
                src_ref=ins[a].at[:, pl.ds(c * h, h), :], dst_ref=outs[a].at[:, pl.ds(c * h, h), :],
                send_sem=send_sems.at[a], recv_sem=recv_sems.at[a], device_id=(x, y, 1 - c), device_id_type=MESH))
        for cp in copies:
            cp.start()
        for a in range(n):
            h = _row_half(sums[a])
            theirs = outs[a].at[:, pl.ds((1 - c) * h, h), :]
            pltpu.make_async_remote_copy(src_ref=theirs, dst_ref=theirs, send_sem=send_sems.at[a], recv_sem=recv_sems.at[a],
                                         device_id=(x, y, 1 - c), device_id_type=MESH).wait_recv()
        for cp in copies:
            cp.wait_send()

    return pl.pallas_call(
        body, name="half_fill",
        in_specs=[ANY] * n, out_specs=[ANY] * n,
        out_shape=[jax.ShapeDtypeStruct(a.shape, a.dtype) for a in sums],
        input_output_aliases={a: a for a in range(n)},
        scratch_shapes=[pltpu.SemaphoreType.DMA((n,)), pltpu.SemaphoreType.DMA((n,))],
        compiler_params=pltpu.CompilerParams(has_side_effects=True),
    )(*sums)


def _adamw_math(w, g, m, v):
    m = ADAM_B1 * m + (1.0 - ADAM_B1) * g
    v = ADAM_B2 * v + (1.0 - ADAM_B2) * (g * g)
    m_hat = m / (1.0 - ADAM_B1 ** ADAM_STEP)
    v_hat = v / (1.0 - ADAM_B2 ** ADAM_STEP)
    delta = -ADAM_LR * (m_hat / (jnp.sqrt(v_hat) + ADAM_EPS) + ADAM_WD * w)
    return delta, m, v


def _adamw(ws, gs, ms, vs):
    n = len(ws)
    steps = 16
    tiles = [a.shape[-2] // steps for a in ws]

    def body(*refs):
        w_r, g_r, m_r, v_r = refs[:n], refs[n:2 * n], refs[2 * n:3 * n], refs[3 * n:4 * n]
        g_o, d_o, m_o, v_o = refs[4 * n:5 * n], refs[5 * n:6 * n], refs[6 * n:7 * n], refs[7 * n:]
        for a in range(n):
            g = g_r[a][...]
            d, m, v = _adamw_math(w_r[a][...], g, m_r[a][...], v_r[a][...])
            g_o[a][...] = g
            d_o[a][...] = d
            m_o[a][...] = m
            v_o[a][...] = v

    specs = [pl.BlockSpec((2, t, a.shape[-1]), lambda i: (0, i, 0)) for a, t in zip(ws, tiles)]
    outs = pl.pallas_call(
        body, name="adamw", grid=(steps,),
        in_specs=specs * 4, out_specs=specs * 4,
        out_shape=[jax.ShapeDtypeStruct(a.shape, F32) for a in ws] * 4,
        compiler_params=_params(1),
    )(*ws, *gs, *ms, *vs)
    return outs[:n], outs[n:2 * n], outs[2 * n:3 * n], outs[3 * n:]


SMALL_W = 512


def _small_allreduce_adamw(g, w, m, v, cw_w, cw_m, cw_v, cw_row0):
    R = g.shape[0]
    n_l = cw_w.shape[0]

    def body(g_ref, w_ref, m_ref, v_ref, cww_ref, cwm_ref, cwv_ref,
             gs_ref, d_ref, mo_ref, vo_ref, cg_ref, cd_ref, cmo_ref, cvo_ref,
             slots_ref, send_sems, recv_sems):
        x, y, c, _ = _place()
        me = 4 * x + 2 * y + c
        slots_ref[me] = g_ref[...]
        sends = []
        for d in range(1, 8):
            px, py, pc = x ^ (d >> 2), y ^ ((d >> 1) & 1), c ^ (d & 1)
            cp = pltpu.make_async_remote_copy(src_ref=g_ref, dst_ref=slots_ref.at[me], send_sem=send_sems.at[d - 1],
                                              recv_sem=recv_sems.at[d - 1], device_id=(px, py, pc), device_id_type=MESH)
            cp.start()
            sends.append(cp)
        for d in range(1, 8):
            px, py, pc = x ^ (d >> 2), y ^ ((d >> 1) & 1), c ^ (d & 1)
            slot = slots_ref.at[4 * px + 2 * py + pc]
            pltpu.make_async_remote_copy(src_ref=slot, dst_ref=slot, send_sem=send_sems.at[d - 1],
                                         recv_sem=recv_sems.at[d - 1], device_id=(px, py, pc),
                                         device_id_type=MESH).wait_recv()
        for cp in sends:
            cp.wait_send()
        tot = slots_ref[0]
        for k in range(1, 8):
            tot = tot + slots_ref[k]
        gs_ref[...] = tot
        dl, mn, vn = _adamw_math(w_ref[...], tot, m_ref[...], v_ref[...])
        d_ref[...] = dl
        mo_ref[...] = mn
        vo_ref[...] = vn
        chip = 2 * x + y
        for l in range(n_l):
            rows = tot[cw_row0[l]:cw_row0[l] + CONV_HALO, :]
            mine = rows[:, 0:128]
            for k in range(1, N_CHIPS):
                mine = jnp.where(chip == k, rows[:, 128 * k:128 * (k + 1)], mine)
            cg_ref[l] = mine
            dl, mn, vn = _adamw_math(cww_ref[l], mine, cwm_ref[l], cwv_ref[l])
            cd_ref[l] = dl
            cmo_ref[l] = mn
            cvo_ref[l] = vn

    vm = pl.BlockSpec(memory_space=pltpu.VMEM)
    small = jax.ShapeDtypeStruct((R, SMALL_W), F32)
    conv = jax.ShapeDtypeStruct(cw_w.shape, F32)
    return pl.pallas_call(
        body, name="small_allreduce_adamw",
        in_specs=[vm] * 7, out_specs=[vm] * 8,
        out_shape=[small] * 4 + [conv] * 4,
        scratch_shapes=[pltpu.VMEM((8, R, SMALL_W), F32), pltpu.SemaphoreType.DMA((7,)), pltpu.SemaphoreType.DMA((7,))],
        compiler_params=pltpu.CompilerParams(has_side_effects=True, vmem_limit_bytes=VMEM_LIMIT),
    )(g, w, m, v, cw_w, cw_m, cw_v)


SMALL_LAYOUT = (("conv_w", CONV_HALO), ("norm1_g", 2), ("norm2_g", 2), ("conv_b", 1), ("conv_ln_g", 1),
                ("conv_ln_b", 1), ("q_norm_g", 1), ("k_norm_g", 1), ("b_f", 1))
SMALL_ROWS = sum(r for _, r in SMALL_LAYOUT)
SMALL_ROWS_PAD = 48
LOSS_ROW = SMALL_ROWS


def _pack_small(per_layer):
    blocks = []
    for d in per_layer:
        rows = []
        for name, r in SMALL_LAYOUT:
            if name == "conv_w":
                a = d.get(name)
                a = jnp.zeros((r, SMALL_W), F32) if a is None else jnp.pad(a, ((0, r - a.shape[0]), (0, 0)))
            else:
                a = d[name].reshape(-1)
                a = jnp.pad(a, (0, r * SMALL_W - a.shape[0])).reshape(r, SMALL_W)
            rows.append(a)
        rows.append(jnp.zeros((SMALL_ROWS_PAD - SMALL_ROWS, SMALL_W), F32))
        blocks.append(jnp.concatenate(rows, axis=0))
    return jnp.concatenate(blocks, axis=0)


def _unpack_small(packed, name, size):
    n_l = packed.shape[0] // SMALL_ROWS_PAD
    row0 = 0
    for nm, r in SMALL_LAYOUT:
        if nm == name:
            break
        row0 += r
    out = [packed[l * SMALL_ROWS_PAD + row0:l * SMALL_ROWS_PAD + row0 + r].reshape(-1)[:size] for l in range(n_l)]
    return jnp.stack(out)


SMALL_SIZES = dict(norm1_g=D_MODEL, norm2_g=D_MODEL, conv_b=CONV_CH, conv_ln_g=CONV_CH, conv_ln_b=CONV_CH,
                   q_norm_g=HEAD_DIM, k_norm_g=HEAD_DIM, b_f=ATT_HEADS)
SMALL_KEYS = dict(norm1_g="g1", norm2_g="g2", conv_b="cvb", conv_ln_g="lng", conv_ln_b="lnb",
                  q_norm_g="qg", k_norm_g="kg", b_f="bf", conv_w="cw")
CONV_W_ROW0 = 0


def kernel(x, norm1_g, w_in, b_f, q_norm_g, k_norm_g, conv_w, conv_b, conv_ln_g, conv_ln_b, w_o, norm2_g, w_mlp_in, w_mlp_out, loss_target, m_norm1_g, m_w_in, m_b_f, m_q_norm_g, m_k_norm_g, m_conv_w, m_conv_b, m_conv_ln_g, m_conv_ln_b, m_w_o, m_norm2_g, m_w_mlp_in, m_w_mlp_out, v_norm1_g, v_w_in, v_b_f, v_q_norm_g, v_k_norm_g, v_conv_w, v_conv_b, v_conv_ln_g, v_conv_ln_b, v_w_o, v_norm2_g, v_w_mlp_in, v_w_mlp_out):
    n_l = w_in.shape[0]

    per_layer = lambda l: [w_in[l].astype(BF16), conv_w[l], w_o[l].astype(BF16), w_mlp_in[l].astype(BF16),
                           w_mlp_out[l].astype(BF16)]
    n_w = len(per_layer(0))
    send_sems, recv_sems, srcs, lands, token = _gather_start([a for l in range(n_l) for a in per_layer(l)])

    def layer_weights(l):
        def wait(tag, which, after):
            groups = [n_w * l + i for i in which]
            return _gather_wait(f"gather_wait_{tag}{l}", groups, send_sems, recv_sems,
                                [srcs[g] for g in groups], [lands[g] for g in groups], after)

        def early(after):
            g_in, g_cw = wait("a", (0, 1), token if l == 0 else after)
            win = _win_to_internal(jnp.concatenate([g_in[k] for k in range(N_CHIPS)], axis=-1))
            cw = jnp.concatenate([g_cw[k] for k in range(N_CHIPS)], axis=-1)
            return _layer_params(l, win, cw, norm1_g, b_f, q_norm_g, k_norm_g, conv_b, conv_ln_g, conv_ln_b, norm2_g)

        def late(after):
            wo, w1, w2 = wait("b", (2, 3, 4), after)
            return dict(wo=wo, w1=w1, w2=w2)

        return early, late

    place = jnp.stack([2 * lax.axis_index("x") + lax.axis_index("y"), lax.axis_index("c")]).astype(jnp.int32)
    big_w = [w_in, w_o, w_mlp_in, w_mlp_out]
    pending = []

    def reduce(l):
        def group(tag, grads):
            which, gs = list(grads), list(grads.values())
            got = _half_swap(f"half_swap_{tag}{l}", gs)
            parts = _half_add(f"half_add_{tag}{l}", gs, got, place)
            send, recv, parts, lands, token = _exchange_start(f"exchange_start_{tag}{l}", parts)
            pending.append((f"{tag}{l}", l, which, send, recv, parts, lands))
            return token
        return group

    loss, dx, smalls = _local_step(x[0], loss_target[0], [layer_weights(l) for l in range(n_l)], reduce)
    sums = [lax.empty(w.shape, F32) for w in big_w]
    for tag, l, which, send, recv, parts, lands in pending:
        parts, lands = _exchange_wait(f"exchange_wait_{tag}", send, recv, parts, lands, dx)
        done = _chip_sum(f"chip_sum_{tag}", parts, lands, [sums[i] for i in which], place, l)
        for i, a in zip(which, done):
            sums[i] = a
    g_big = _half_fill(sums)
    big_m = [m_w_in, m_w_o, m_w_mlp_in, m_w_mlp_out]
    big_v = [v_w_in, v_w_o, v_w_mlp_in, v_w_mlp_out]
    g_big, d_big, nm_big, nv_big = _adamw(big_w, g_big, big_m, big_v)

    env = dict(norm1_g=(norm1_g, m_norm1_g, v_norm1_g), norm2_g=(norm2_g, m_norm2_g, v_norm2_g),
               conv_b=(conv_b, m_conv_b, v_conv_b), conv_ln_g=(conv_ln_g, m_conv_ln_g, v_conv_ln_g),
               conv_ln_b=(conv_ln_b, m_conv_ln_b, v_conv_ln_b), q_norm_g=(q_norm_g, m_q_norm_g, v_q_norm_g),
               k_norm_g=(k_norm_g, m_k_norm_g, v_k_norm_g), b_f=(b_f, m_b_f, v_b_f))
    g_pack = _pack_small([{nm: s[key] for nm, key in SMALL_KEYS.items()} for s in smalls])
    g_pack = g_pack.at[LOSS_ROW, 0].set(loss)
    packs = [_pack_small([{nm: env[nm][t][l] for nm in env} for l in range(n_l)]) for t in range(3)]
    pad_cw = lambda a: jnp.pad(a, ((0, 0), (0, CONV_HALO - CONV_TAPS), (0, 0)))
    cw_row0 = tuple(l * SMALL_ROWS_PAD + CONV_W_ROW0 for l in range(n_l))
    gs, ds, ms, vs, cg, cd, cm, cv = _small_allreduce_adamw(
        g_pack, packs[0], packs[1], packs[2], pad_cw(conv_w), pad_cw(m_conv_w), pad_cw(v_conv_w), cw_row0)

    def small_out(packed, conv):
        o = {nm: _unpack_small(packed, nm, sz) for nm, sz in SMALL_SIZES.items()}
        o["conv_w"] = conv[:, 0:CONV_TAPS, :]
        return o

    def ordered(small, big):
        return (small["norm1_g"], big[0], small["b_f"], small["q_norm_g"], small["k_norm_g"], small["conv_w"],
                small["conv_b"], small["conv_ln_g"], small["conv_ln_b"], big[1], small["norm2_g"], big[2], big[3])

    return (gs[LOSS_ROW, 0], dx[None],
            *ordered(small_out(gs, cg), g_big), *ordered(small_out(ds, cd), d_big),
            *ordered(small_out(ms, cm), nm_big), *ordered(small_out(vs, cv), nv_big))
```

```python
import functools

import jax
import jax.numpy as jnp
from jax import lax
from jax.experimental import pallas as pl
from jax.experimental.pallas import tpu as pltpu

F32 = jnp.float32
BF16 = jnp.bfloat16

D_MODEL = 1024
ATT_HEADS = 8
HEAD_DIM = 64
ATT_W = ATT_HEADS * HEAD_DIM
CONV_CH = 512
CONV_TAPS = 31
CONV_HALO = 32
D_FF = 4 * D_MODEL
N_IN = 3 * ATT_W + ATT_HEADS + 2 * CONV_CH
O_Q, O_K, O_V, O_A, O_G, O_F = 0, 512, 1024, 1536, 2048, 2560
N_INP = O_F + 128
EPS = 1e-6
QK_SCALE = 0.125

ADAM_LR = 0.001
ADAM_B1 = 0.9
ADAM_B2 = 0.999
ADAM_EPS = 1e-08
ADAM_WD = 0.01
ADAM_STEP = 10

N_CHIPS = 4
VMEM_LIMIT = 52 * 1024 * 1024
MESH = pl.DeviceIdType.MESH
ANY = pl.BlockSpec(memory_space=pl.ANY)


def _params(n_axes, **kw):
    return pltpu.CompilerParams(dimension_semantics=("arbitrary",) * n_axes,
                                vmem_limit_bytes=VMEM_LIMIT, **kw)


def _dot(a, b):
    return jnp.dot(a, b, preferred_element_type=F32)


def _dot_nt(a, b):
    return lax.dot_general(a, b, (((1,), (1,)), ((), ())), preferred_element_type=F32)


def _dot_tn(a, b):
    return lax.dot_general(a, b, (((0,), (0,)), ((), ())), preferred_element_type=F32)


def _split3(a):
    a1 = a.astype(BF16)
    r = a - a1.astype(F32)
    a2 = r.astype(BF16)
    a3 = (r - a2.astype(F32)).astype(BF16)
    return a1, a2, a3


def _dot_hi_r(a, b_exact):
    return sum(_dot(p, b_exact) for p in _split3(a))


def _dot_hi_l(a_exact, b):
    return sum(_dot(a_exact, p) for p in _split3(b))


def _sigmoid(x):
    return 1.0 / (1.0 + jnp.exp(-x))


def _head_blockdiag():
    i = jnp.arange(ATT_W) // HEAD_DIM
    return (i[:, None] == i[None, :]).astype(BF16)


AUG_LANES = 8


def _aug_place(first):
    piece = jnp.arange(3 * 128)[:, None] // 128
    h = jnp.arange(3 * 128)[:, None] % 128
    lane = jnp.arange(ATT_W)[None, :]
    return ((h < ATT_HEADS) & (lane == 128 * (h // 2) + AUG_LANES * (h % 2) + first + piece)).astype(BF16)


def _aug_ones(first):
    lane = jnp.arange(ATT_W) % 128
    pos = lane % AUG_LANES
    return ((lane < 2 * AUG_LANES) & (pos >= first) & (pos < first + 3)).astype(F32).reshape(1, ATT_W)


def _head_rows():
    h = jnp.arange(2 * ATT_HEADS)[:, None]
    i = jnp.arange(ATT_W)[None, :] // HEAD_DIM
    return (h == i).astype(BF16)


def _head_fold():
    i = jnp.arange(ATT_W)[:, None] % HEAD_DIM
    j = jnp.arange(128)[None, :]
    return (i == j).astype(BF16)


def _head_pick():
    i = jnp.arange(ATT_W)[:, None]
    h = jnp.arange(128)[None, :]
    return (i == h * HEAD_DIM).astype(BF16)


def _tril(n):
    r = jnp.arange(n)
    return (r[:, None] >= r[None, :]).astype(BF16)


SUBLANES = 8


def _fill_row_shifts(buf_ref, shifts_ref, tm):
    n = tm + CONV_HALO - SUBLANES
    for b in range(1, SUBLANES):
        shifts_ref[b - 1, 0:n, :] = buf_ref[pl.ds(b, n), :]


def _row_shifted(buf_ref, shifts_ref, offset, rows, base=0):
    a, b = divmod(offset, SUBLANES)
    start = pl.multiple_of(base + SUBLANES * a, SUBLANES)
    if b == 0:
        return buf_ref[pl.ds(start, rows), :]
    return shifts_ref[b - 1, pl.ds(start, rows), :]


CONV_ROWS = 32


def _mixer_in_fwd(x, g1, win, qg, kg, bf, cw, cvb, lng, lnb):
    S = x.shape[0]
    TM = min(512, S)
    nb = S // TM

    def body(x_ref, g1_ref, win_ref, qg_ref, kg_ref, bf_ref, cw_ref, cvb_ref, lng_ref, lnb_ref,
             bd_ref, tri_ref, pq_ref, pk_ref, oq_ref, ok_ref,
             u1_ref, proj_ref, q_ref, k_ref, v_ref, qa_ref, ka_ref, h0_ref, h1_ref, h3_ref,
             carry_ref, hbuf_ref, hs_ref):
        i = pl.program_id(0)

        @pl.when(i == 0)
        def _():
            carry_ref[...] = jnp.zeros_like(carry_ref)
            hbuf_ref[0:CONV_HALO, :] = jnp.zeros((CONV_HALO, CONV_CH), F32)

        @pl.when(i > 0)
        def _():
            hbuf_ref[0:CONV_HALO, :] = hbuf_ref[TM:TM + CONV_HALO, :]

        xv = x_ref[...]
        r = lax.rsqrt(jnp.mean(xv * xv, axis=-1, keepdims=True) + EPS)
        u = (xv * r * g1_ref[...]).astype(BF16)
        u1_ref[...] = u
        proj_ref[...] = _dot(u, win_ref[...])

        def headnorm(raw, gain):
            ss = _dot_hi_r(raw * raw, bd_ref[...]) * (1.0 / HEAD_DIM)
            return raw * lax.rsqrt(ss + EPS) * gain

        q_ref[...] = (headnorm(proj_ref[:, O_Q:O_Q + ATT_W], qg_ref[...]) * QK_SCALE).astype(BF16)
        k_ref[...] = headnorm(proj_ref[:, O_K:O_K + ATT_W], kg_ref[...]).astype(BF16)
        v_ref[...] = proj_ref[:, O_V:O_V + ATT_W].astype(BF16)

        zf = proj_ref[:, O_F:O_F + 128] + bf_ref[...]
        logf = jnp.minimum(zf, 0.0) - jnp.log(1.0 + jnp.exp(-jnp.abs(zf)))
        lane = lax.broadcasted_iota(jnp.int32, (TM, 128), 1)
        logf = jnp.where(lane < ATT_HEADS, logf, 0.0)
        c8 = _dot_hi_l(tri_ref[...], logf) + carry_ref[...]
        carry_ref[...] = c8[TM - 1:TM, :]
        pieces = jnp.concatenate(_split3(c8), axis=1)
        qa_ref[...] = (_dot(pieces, pq_ref[...]) + oq_ref[...]).astype(BF16)
        ka_ref[...] = (ok_ref[...] - _dot(pieces, pk_ref[...])).astype(BF16)

        h0 = proj_ref[:, O_A:O_A + CONV_CH] * _sigmoid(proj_ref[:, O_G:O_G + CONV_CH])
        h0_ref[...] = h0
        hbuf_ref[CONV_HALO:CONV_HALO + TM, :] = h0
        _fill_row_shifts(hbuf_ref, hs_ref, TM)
        acc = jnp.zeros((TM, CONV_CH), F32) + cvb_ref[...]
        for j in range(CONV_TAPS):
            acc = acc + cw_ref[j:j + 1, :] * _row_shifted(hbuf_ref, hs_ref, CONV_HALO - CONV_TAPS + 1 + j, TM)
        h1_ref[...] = acc
        mu = jnp.mean(acc, axis=-1, keepdims=True)
        d = acc - mu
        var = jnp.mean(d * d, axis=-1, keepdims=True)
        h2 = d * lax.rsqrt(var + EPS) * lng_ref[...] + lnb_ref[...]
        h3_ref[...] = (h2 * _sigmoid(h2)).astype(BF16)

    row = lambda w: pl.BlockSpec((TM, w), lambda i: (i, 0))
    full = lambda a: pl.BlockSpec(a.shape, lambda i: (0,) * a.ndim)
    ins = (x, g1, win, qg, kg, bf, cw, cvb, lng, lnb, _head_blockdiag(), _tril(TM),
           _aug_place(0), _aug_place(3), _aug_ones(3), _aug_ones(0))
    return pl.pallas_call(
        body, name="mixer_in_fwd", grid=(nb,),
        in_specs=[row(D_MODEL)] + [full(a) for a in ins[1:]],
        out_specs=[row(D_MODEL), row(N_INP), row(ATT_W), row(ATT_W), row(ATT_W), row(ATT_W), row(ATT_W),
                   row(CONV_CH), row(CONV_CH), row(CONV_CH)],
        out_shape=[jax.ShapeDtypeStruct((S, D_MODEL), BF16),
                   jax.ShapeDtypeStruct((S, N_INP), F32),
                   jax.ShapeDtypeStruct((S, ATT_W), BF16),
                   jax.ShapeDtypeStruct((S, ATT_W), BF16),
                   jax.ShapeDtypeStruct((S, ATT_W), BF16),
                   jax.ShapeDtypeStruct((S, ATT_W), BF16),
                   jax.ShapeDtypeStruct((S, ATT_W), BF16),
                   jax.ShapeDtypeStruct((S, CONV_CH), F32),
                   jax.ShapeDtypeStruct((S, CONV_CH), F32),
                   jax.ShapeDtypeStruct((S, CONV_CH), BF16)],
        scratch_shapes=[pltpu.VMEM((1, 128), F32), pltpu.VMEM((TM + CONV_HALO, CONV_CH), F32),
                        pltpu.VMEM((SUBLANES - 1, TM + CONV_HALO, CONV_CH), F32)],
        compiler_params=_params(1),
    )(*ins)


def _pair_heads(lo, alo, x, xa):
    z = jnp.zeros_like(x)
    return (jnp.concatenate([jnp.where(lo, x, z), jnp.where(alo, xa, z)], axis=1),
            jnp.concatenate([jnp.where(lo, z, x), jnp.where(alo, z, xa)], axis=1))


def _attn_fwd(q, qa, k, ka, v):
    S = q.shape[0]
    T = min(1024, S)
    nq = S // T

    def body(q_ref, qa_ref, k_ref, ka_ref, v_ref, o_ref, lse_ref):
        qi = pl.program_id(1)
        lane = lax.broadcasted_iota(jnp.int32, (T, 128), 1)
        lo = lane < HEAD_DIM
        qm = _pair_heads(lo, lane < AUG_LANES, q_ref[...], qa_ref[...])
        tril = (lax.broadcasted_iota(jnp.int32, (T, T), 0) >= lax.broadcasted_iota(jnp.int32, (T, T), 1))

        def step(kj, carry, masked):
            off = pl.multiple_of(kj * T, T)
            kb = jnp.concatenate([k_ref[pl.ds(off, T), :], ka_ref[pl.ds(off, T), :]], axis=1)
            vb = v_ref[pl.ds(off, T), :]
            new = []
            for h in range(2):
                m, l, acc = carry[3 * h:3 * h + 3]
                s = _dot_nt(qm[h], kb)
                if masked:
                    s = jnp.where(tril, s, -1e30)
                m_new = jnp.maximum(m, jnp.max(s, axis=-1, keepdims=True))
                alpha = jnp.exp(m - m_new)
                p = jnp.exp(s - m_new)
                l = alpha * l + jnp.sum(p, axis=-1, keepdims=True)
                acc = alpha * acc + _dot(p.astype(BF16), vb)
                new += [m_new, l, acc]
            return tuple(new)

        init = (jnp.full((T, 1), -1e30, F32), jnp.zeros((T, 1), F32), jnp.zeros((T, 128), F32)) * 2
        carry = lax.fori_loop(0, qi, lambda kj, c: step(kj, c, False), init)
        m0, l0, a0, m1, l1, a1 = step(qi, carry, True)
        o_ref[...] = jnp.where(lo, a0 / l0, a1 / l1).astype(BF16)
        lse_t = jnp.where(lo, m0 + jnp.log(l0), m1 + jnp.log(l1)).T
        lse_ref[0:1, :] = lse_t[0:1, :]
        lse_ref[1:2, :] = lse_t[HEAD_DIM:HEAD_DIM + 1, :]

    qblk = pl.BlockSpec((T, 128), lambda hp, qi: (qi, hp))
    seq = pl.BlockSpec((S, 128), lambda hp, qi: (0, hp))
    return pl.pallas_call(
        body, name="attn_fwd", grid=(ATT_HEADS // 2, nq),
        in_specs=[qblk, qblk, seq, seq, seq],
        out_specs=[qblk, pl.BlockSpec((None, 2, T), lambda hp, qi: (hp, 0, qi))],
        out_shape=[jax.ShapeDtypeStruct((S, ATT_W), BF16),
                   jax.ShapeDtypeStruct((ATT_HEADS // 2, 2, S), F32)],
        compiler_params=_params(2),
    )(q, qa, k, ka, v)


def _attn_bwd(q, qa, k, ka, v, do, lse, dlt):
    S = q.shape[0]
    T = min(512, S)
    nq = S // T

    def body(q_ref, qa_ref, do_ref, lse_ref, dlt_ref, k_ref, ka_ref, v_ref, dq_ref, dk_ref, dv_ref, rows_ref, cols_ref):
        kj = pl.program_id(1)

        @pl.when(kj == 0)
        def _():
            dq_ref[...] = jnp.zeros_like(dq_ref)
            rows_ref[...] = jnp.zeros_like(rows_ref)

        lane = lax.broadcasted_iota(jnp.int32, (T, 128), 1)
        lo = lane < HEAD_DIM
        alo = lane < AUG_LANES
        triu = (lax.broadcasted_iota(jnp.int32, (T, T), 0) <= lax.broadcasted_iota(jnp.int32, (T, T), 1))
        kb = k_ref[...]
        kaug = jnp.concatenate([kb, ka_ref[...]], axis=1)
        vb = v_ref[...]

        def step(qi, carry, masked):
            off = pl.multiple_of(qi * T, T)
            qb = q_ref[pl.ds(off, T), :]
            dob = do_ref[pl.ds(off, T), :]
            qm = _pair_heads(lo, alo, qb, qa_ref[pl.ds(off, T), :])
            zero = jnp.zeros_like(qb)
            new, dqs = [], []
            for h in range(2):
                dk_a, dv_a, dc_a = carry[3 * h:3 * h + 3]
                dom = jnp.where(lo, dob, zero) if h == 0 else jnp.where(lo, zero, dob)
                s = _dot_nt(kaug, qm[h])
                if masked:
                    s = jnp.where(triu, s, -1e30)
                p = jnp.exp(s - lse_ref[h:h + 1, pl.ds(off, T)])
                dp = _dot_nt(vb, dom)
                ds = p * (dp - dlt_ref[h:h + 1, pl.ds(off, T)])
                pb = p.astype(BF16)
                dsb = ds.astype(BF16)
                dv_a = dv_a + _dot(pb, dob)
                dk_a = dk_a + _dot(dsb, qb)
                dc_a = dc_a + jnp.sum(ds, axis=1, keepdims=True)
                dqs.append(_dot_tn(dsb, kb))
                rows_ref[h:h + 1, pl.ds(off, T)] += jnp.sum(ds, axis=0, keepdims=True)
                new += [dk_a, dv_a, dc_a]
            dq_ref[pl.ds(off, T), :] += jnp.where(lo, dqs[0], dqs[1])
            return tuple(new)

        init = (jnp.zeros((T, 128), F32), jnp.zeros((T, 128), F32), jnp.zeros((T, 1), F32)) * 2
        carry = step(kj, init, True)
        carry = lax.fori_loop(kj + 1, nq, lambda qi, c: step(qi, c, False), carry)
        dk_ref[...] = jnp.where(lo, carry[0], carry[3])
        dv_ref[...] = jnp.where(lo, carry[1], carry[4])
        cols_ref[...] = -jnp.where(lo, carry[2], carry[5])

    seq = pl.BlockSpec((S, 128), lambda hp, kj: (0, hp))
    rows = pl.BlockSpec((None, 2, S), lambda hp, kj: (hp, 0, 0))
    kblk = pl.BlockSpec((T, 128), lambda hp, kj: (kj, hp))
    return pl.pallas_call(
        body, name="attn_bwd", grid=(ATT_HEADS // 2, nq),
        in_specs=[seq, seq, seq, rows, rows, kblk, kblk, kblk],
        out_specs=[seq, kblk, kblk, rows, kblk],
        out_shape=[jax.ShapeDtypeStruct((S, ATT_W), F32), jax.ShapeDtypeStruct((S, ATT_W), F32),
                   jax.ShapeDtypeStruct((S, ATT_W), F32),
                   jax.ShapeDtypeStruct((ATT_HEADS // 2, 2, S), F32),
                   jax.ShapeDtypeStruct((S, ATT_W), F32)],
        compiler_params=_params(2),
    )(q, qa, do, lse, dlt, k, ka, v)


def _wo_spec(wo4):
    return pl.BlockSpec(wo4.shape, lambda i: (0, 0, 0))


def _wo_halves(wo_ref):
    half = N_CHIPS // 2
    return (wo_ref[0:half].reshape(ATT_W, D_MODEL), wo_ref[half:N_CHIPS].reshape(CONV_CH, D_MODEL))


def _mixer_out_fwd(x, att, h3, wo4, g2):
    S = x.shape[0]
    TM = min(512, S)

    def body(x_ref, att_ref, h3_ref, wo_ref, g2_ref, x2_ref, u2_ref):
        wa, wc = _wo_halves(wo_ref)
        x2 = x_ref[...] + _dot(att_ref[...], wa) + _dot(h3_ref[...], wc)
        x2_ref[...] = x2
        r = lax.rsqrt(jnp.mean(x2 * x2, axis=-1, keepdims=True) + EPS)
        u2_ref[...] = (x2 * r * g2_ref[...]).astype(BF16)

    row = lambda w: pl.BlockSpec((TM, w), lambda i: (i, 0))
    full = lambda a: pl.BlockSpec(a.shape, lambda i: (0,) * a.ndim)
    return pl.pallas_call(
        body, name="mixer_out_fwd", grid=(S // TM,),
        in_specs=[row(D_MODEL), row(ATT_W), row(CONV_CH), _wo_spec(wo4), full(g2)],
        out_specs=[row(D_MODEL), row(D_MODEL)],
        out_shape=[jax.ShapeDtypeStruct((S, D_MODEL), F32), jax.ShapeDtypeStruct((S, D_MODEL), BF16)],
        compiler_params=_params(1),
    )(x, att, h3, wo4, g2)


def _mlp_w_specs():
    return [pl.BlockSpec((None, D_MODEL, D_FF // N_CHIPS), lambda i, f: (f, 0, 0)),
            pl.BlockSpec((None, D_FF // N_CHIPS, D_MODEL), lambda i, f: (f, 0, 0))]


def _mlp_fwd(x2, u2, w1, w2):
    S = x2.shape[0]
    TM = min(1024, S)
    TF = 1024

    def body(x2_ref, u2_ref, w1_ref, w2_ref, x3_ref, z_ref, hh_ref):
        f = pl.program_id(1)

        @pl.when(f == 0)
        def _():
            x3_ref[...] = x2_ref[...]

        z = _dot(u2_ref[...], w1_ref[...])
        z_ref[...] = z
        zr = jnp.maximum(z, 0.0)
        hh = (zr * zr).astype(BF16)
        hh_ref[...] = hh
        x3_ref[...] += _dot(hh, w2_ref[...])

    return pl.pallas_call(
        body, name="mlp_fwd", grid=(S // TM, D_FF // TF),
        in_specs=[pl.BlockSpec((TM, D_MODEL), lambda i, f: (i, 0)), pl.BlockSpec((TM, D_MODEL), lambda i, f: (i, 0))]
        + _mlp_w_specs(),
        out_specs=[pl.BlockSpec((TM, D_MODEL), lambda i, f: (i, 0)), pl.BlockSpec((TM, TF), lambda i, f: (i, f)),
                   pl.BlockSpec((TM, TF), lambda i, f: (i, f))],
        out_shape=[jax.ShapeDtypeStruct((S, D_MODEL), F32), jax.ShapeDtypeStruct((S, D_FF), F32),
                   jax.ShapeDtypeStruct((S, D_FF), BF16)],
        compiler_params=_params(2),
    )(x2, u2, w1, w2)


def _loss_fwd_bwd(y, t):
    S = y.shape[0]
    TM = min(512, S)

    def body(y_ref, t_ref, dy_ref, loss_ref):
        @pl.when(pl.program_id(0) == 0)
        def _():
            loss_ref[...] = jnp.zeros_like(loss_ref)

        d = y_ref[...] - t_ref[...]
        dy_ref[...] = d * (1.0 / D_MODEL)
        loss_ref[...] += jnp.sum(d * d)

    row = pl.BlockSpec((TM, D_MODEL), lambda i: (i, 0))
    return pl.pallas_call(
        body, name="loss", grid=(S // TM,),
        in_specs=[row, row], out_specs=[row, pl.BlockSpec((8, 128), lambda i: (0, 0))],
        out_shape=[jax.ShapeDtypeStruct((S, D_MODEL), F32), jax.ShapeDtypeStruct((8, 128), F32)],
        compiler_params=_params(1),
    )(y, t)


def _mlp_bwd(dx3, z, x2, g2, w1, w2):
    S = dx3.shape[0]
    TM = min(512, S)
    TF = 1024
    nf = D_FF // TF

    def body(dx3_ref, z_ref, x2_ref, g2_ref, w1_ref, w2_ref, dz_ref, dx2_ref, dg2_ref, du2_ref):
        i = pl.program_id(0)
        f = pl.program_id(1)

        @pl.when((i == 0) & (f == 0))
        def _():
            dg2_ref[...] = jnp.zeros_like(dg2_ref)

        @pl.when(f == 0)
        def _():
            du2_ref[...] = jnp.zeros_like(du2_ref)

        dhh = _dot_nt(dx3_ref[...].astype(BF16), w2_ref[...])
        dz = (dhh * (2.0 * jnp.maximum(z_ref[...], 0.0))).astype(BF16)
        dz_ref[...] = dz
        du2_ref[...] += _dot_nt(dz, w1_ref[...])

        @pl.when(f == nf - 1)
        def _():
            x2 = x2_ref[...]
            r = lax.rsqrt(jnp.mean(x2 * x2, axis=-1, keepdims=True) + EPS)
            n = x2 * r
            du2 = du2_ref[...]
            t = du2 * g2_ref[...]
            dx2_ref[...] = dx3_ref[...] + r * (t - n * jnp.mean(t * n, axis=-1, keepdims=True))
            dg2_ref[0:1, :] += jnp.sum(du2 * n, axis=0, keepdims=True)

    rowi = pl.BlockSpec((TM, D_MODEL), lambda i, f: (i, 0))
    return pl.pallas_call(
        body, name="mlp_bwd", grid=(S // TM, nf),
        in_specs=[rowi, pl.BlockSpec((TM, TF), lambda i, f: (i, f)), rowi,
                  pl.BlockSpec((1, D_MODEL), lambda i, f: (0, 0))] + _mlp_w_specs(),
        out_specs=[pl.BlockSpec((TM, TF), lambda i, f: (i, f)), rowi, pl.BlockSpec((8, D_MODEL), lambda i, f: (0, 0))],
        out_shape=[jax.ShapeDtypeStruct((S, D_FF), BF16), jax.ShapeDtypeStruct((S, D_MODEL), F32),
                   jax.ShapeDtypeStruct((8, D_MODEL), F32)],
        scratch_shapes=[pltpu.VMEM((TM, D_MODEL), F32)],
        compiler_params=_params(2),
    )(dx3, z, x2, g2, w1, w2)


def _matmul_tn(a, b, col_shards=1):
    S, I = a.shape
    J = b.shape[1]
    TI = min(I, 1024)
    TJ = 1024 if J % 1024 == 0 else 896
    TS = min(S, 1024)
    nk = S // TS
    per = J // col_shards // TJ

    def body(a_ref, b_ref, o_ref, acc_ref):
        k = pl.program_id(2)

        @pl.when(k == 0)
        def _():
            acc_ref[...] = jnp.zeros_like(acc_ref)

        acc_ref[...] += _dot_tn(a_ref[...].astype(BF16), b_ref[...].astype(BF16))

        @pl.when(k == nk - 1)
        def _():
            o_ref[...] = acc_ref[...].astype(BF16)

    return pl.pallas_call(
        body, name="matmul_tn", grid=(I // TI, J // TJ, nk),
        in_specs=[pl.BlockSpec((TS, TI), lambda i, j, k: (k, i)), pl.BlockSpec((TS, TJ), lambda i, j, k: (k, j))],
        out_specs=pl.BlockSpec((None, TI, TJ), lambda i, j, k: (j // per, i, j % per)),
        out_shape=jax.ShapeDtypeStruct((col_shards, I, J // col_shards), BF16),
        scratch_shapes=[pltpu.VMEM((TI, TJ), F32)],
        compiler_params=_params(3),
    )(a, b)


def _mixer_out_bwd(dx2, wo4, att, h1, lng, lnb):
    S = dx2.shape[0]
    TM = min(512, S)

    def body(dx2_ref, wo_ref, att_ref, h1_ref, lng_ref, lnb_ref, hr_ref, datt_ref, dlt_ref, dh1_ref, sm_ref):
        @pl.when(pl.program_id(0) == 0)
        def _():
            sm_ref[...] = jnp.zeros_like(sm_ref)

        dxb = dx2_ref[...].astype(BF16)
        wa, wc = _wo_halves(wo_ref)
        datt = _dot_nt(dxb, wa)
        datt_ref[...] = datt.astype(BF16)
        prod = datt * att_ref[...].astype(F32)
        dlt_ref[...] = sum(_dot_nt(hr_ref[...], piece) for piece in _split3(prod))[0:ATT_HEADS, :]
        dh3 = _dot_nt(dxb, wc)
        h1 = h1_ref[...]
        mu = jnp.mean(h1, axis=-1, keepdims=True)
        d = h1 - mu
        rstd = lax.rsqrt(jnp.mean(d * d, axis=-1, keepdims=True) + EPS)
        n = d * rstd
        h2 = n * lng_ref[...] + lnb_ref[...]
        sg = _sigmoid(h2)
        dh2 = dh3 * (sg * (1.0 + h2 * (1.0 - sg)))
        dn = dh2 * lng_ref[...]
        dh1 = rstd * (dn - jnp.mean(dn, axis=-1, keepdims=True) - n * jnp.mean(dn * n, axis=-1, keepdims=True))
        dh1_ref[...] = dh1
        sm_ref[0:1, :] += jnp.sum(dh2 * n, axis=0, keepdims=True)
        sm_ref[1:2, :] += jnp.sum(dh2, axis=0, keepdims=True)
        sm_ref[2:3, :] += jnp.sum(dh1, axis=0, keepdims=True)

    row = lambda w: pl.BlockSpec((TM, w), lambda i: (i, 0))
    full = lambda a: pl.BlockSpec(a.shape, lambda i: (0,) * a.ndim)
    hr = _head_rows()
    return pl.pallas_call(
        body, name="mixer_out_bwd", grid=(S // TM,),
        in_specs=[row(D_MODEL), _wo_spec(wo4), row(ATT_W), row(CONV_CH), full(lng), full(lnb), full(hr)],
        out_specs=[row(ATT_W), pl.BlockSpec((ATT_HEADS, TM), lambda i: (0, i)), row(CONV_CH),
                   pl.BlockSpec((8, CONV_CH), lambda i: (0, 0))],
        out_shape=[jax.ShapeDtypeStruct((S, ATT_W), BF16), jax.ShapeDtypeStruct((ATT_HEADS, S), F32),
                   jax.ShapeDtypeStruct((S, CONV_CH), F32), jax.ShapeDtypeStruct((8, CONV_CH), F32)],
        compiler_params=_params(1),
    )(dx2, wo4, att, h1, lng, lnb, hr)


def _conv_glu_bwd(dh1, h0, proj, cw):
    S = dh1.shape[0]
    TM = min(512, S)
    nb = S // TM
    lead = CONV_HALO - CONV_TAPS + 1

    def body(dh1_ref, dnx_ref, h0_ref, hpv_ref, a_ref, g_ref, cw_ref, dag_ref, dcw_ref,
             dbuf_ref, hbuf_ref, ds_ref, hs_ref, dh0_ref, dcw8_ref):
        i = pl.program_id(0)

        @pl.when(i == 0)
        def _():
            dcw8_ref[...] = jnp.zeros_like(dcw8_ref)

        dbuf_ref[0:TM, :] = dh1_ref[...]
        dbuf_ref[TM:TM + CONV_HALO, :] = jnp.where(i < nb - 1, dnx_ref[0:CONV_HALO, :], 0.0)
        hbuf_ref[0:CONV_HALO, :] = jnp.where(i > 0, hpv_ref[TM - CONV_HALO:TM, :], 0.0)
        hbuf_ref[CONV_HALO:CONV_HALO + TM, :] = h0_ref[...]
        _fill_row_shifts(dbuf_ref, ds_ref, TM)
        _fill_row_shifts(hbuf_ref, hs_ref, TM)

        def conv_rows(step, _):
            r0 = pl.multiple_of(step * CONV_ROWS, CONV_ROWS)
            dh1 = dbuf_ref[pl.ds(r0, CONV_ROWS), :]
            part = jnp.zeros((CONV_ROWS, CONV_CH), F32)
            for j in range(CONV_TAPS):
                part = part + cw_ref[j:j + 1, :] * _row_shifted(dbuf_ref, ds_ref, CONV_TAPS - 1 - j, CONV_ROWS, r0)
                prod = dh1 * _row_shifted(hbuf_ref, hs_ref, lead + j, CONV_ROWS, r0)
                dcw8_ref[j] += jnp.sum(prod.reshape(CONV_ROWS // SUBLANES, SUBLANES, CONV_CH), axis=0)
            dh0_ref[pl.ds(r0, CONV_ROWS), :] = part
            return 0

        lax.fori_loop(0, TM // CONV_ROWS, conv_rows, 0)

        @pl.when(i == nb - 1)
        def _():
            dcw_ref[...] = jnp.sum(dcw8_ref[...], axis=1)

        dh0 = dh0_ref[...]
        sg = _sigmoid(g_ref[...])
        dag_ref[:, 0:CONV_CH] = (dh0 * sg).astype(BF16)
        dag_ref[:, CONV_CH:2 * CONV_CH] = (dh0 * a_ref[...] * sg * (1.0 - sg)).astype(BF16)

    blk = lambda fn: pl.BlockSpec((TM, CONV_CH), fn)
    return pl.pallas_call(
        body, name="conv_glu_bwd", grid=(nb,),
        in_specs=[blk(lambda i: (i, 0)), blk(lambda i: (jnp.minimum(i + 1, nb - 1), 0)),
                  blk(lambda i: (i, 0)), blk(lambda i: (jnp.maximum(i - 1, 0), 0)),
                  blk(lambda i: (i, O_A // CONV_CH)), blk(lambda i: (i, O_G // CONV_CH)),
                  pl.BlockSpec(cw.shape, lambda i: (0, 0))],
        out_specs=[pl.BlockSpec((TM, 2 * CONV_CH), lambda i: (i, 0)), pl.BlockSpec((CONV_HALO, CONV_CH), lambda i: (0, 0))],
        out_shape=[jax.ShapeDtypeStruct((S, 2 * CONV_CH), BF16), jax.ShapeDtypeStruct((CONV_HALO, CONV_CH), F32)],
        scratch_shapes=[pltpu.VMEM((TM + CONV_HALO, CONV_CH), F32), pltpu.VMEM((TM + CONV_HALO, CONV_CH), F32),
                        pltpu.VMEM((SUBLANES - 1, TM + CONV_HALO, CONV_CH), F32),
                        pltpu.VMEM((SUBLANES - 1, TM + CONV_HALO, CONV_CH), F32),
                        pltpu.VMEM((TM, CONV_CH), F32), pltpu.VMEM((CONV_HALO, SUBLANES, CONV_CH), F32)],
        compiler_params=_params(1),
    )(dh1, dh1, h0, h0, proj, proj, cw)


def _mixer_in_bwd(x, dx2, proj, dq, dk, dv, dag, dct, drb, g1, win, qg, kg, bf):
    S = x.shape[0]
    TM = min(256, S)
    nb = S // TM

    def body(x_ref, dx2_ref, qr_ref, kr_ref, fz_ref, dq_ref, dk_ref, dv_ref, dag_ref, dct_ref, drb_ref,
             g1_ref, win_ref, qg_ref, kg_ref, bf_ref, bd_ref, fold_ref, triu_ref, pick_ref,
             dproj_ref, dx_ref, dg1_ref, sm_ref, carry_ref, gsum_ref):
        i = pl.program_id(0)

        @pl.when(i == 0)
        def _():
            carry_ref[...] = jnp.zeros_like(carry_ref)
            gsum_ref[...] = jnp.zeros_like(gsum_ref)
            dg1_ref[...] = jnp.zeros_like(dg1_ref)
            sm_ref[...] = jnp.zeros_like(sm_ref)

        def headnorm_bwd(raw, dy, gain, scale, row):
            rs = lax.rsqrt(_dot_hi_r(raw * raw, bd_ref[...]) * (1.0 / HEAD_DIM) + EPS)
            n = raw * rs
            gsum_ref[row:row + 1, :] += jnp.sum(dy * n, axis=0, keepdims=True) * scale
            dn = dy * (gain * scale)
            return rs * (dn - n * (_dot_hi_r(dn * n, bd_ref[...]) * (1.0 / HEAD_DIM)))

        dproj_ref[:, O_Q:O_Q + ATT_W] = headnorm_bwd(qr_ref[...], dq_ref[...], qg_ref[...], QK_SCALE, 0).astype(BF16)
        dproj_ref[:, O_K:O_K + ATT_W] = headnorm_bwd(kr_ref[...], dk_ref[...], kg_ref[...], 1.0, 1).astype(BF16)
        dproj_ref[:, O_V:O_V + ATT_W] = dv_ref[...].astype(BF16)
        dproj_ref[:, O_A:O_A + 2 * CONV_CH] = dag_ref[...]

        dc8 = jnp.concatenate([dct_ref[...], jnp.zeros((128 - ATT_HEADS, TM), F32)], axis=0).T
        dc8 = dc8 + _dot_hi_r(drb_ref[...], pick_ref[...])
        dlogf = _dot_hi_l(triu_ref[...], dc8) + carry_ref[...]
        carry_ref[...] = dlogf[0:1, :]
        df = dlogf * _sigmoid(-(fz_ref[...] + bf_ref[...]))
        dproj_ref[:, O_F:O_F + 128] = df.astype(BF16)
        sm_ref[2:3, :] += jnp.sum(df, axis=0, keepdims=True)

        du1 = _dot_nt(dproj_ref[...], win_ref[...])
        xv = x_ref[...]
        r = lax.rsqrt(jnp.mean(xv * xv, axis=-1, keepdims=True) + EPS)
        n1 = xv * r
        t = du1 * g1_ref[...]
        dx_ref[...] = dx2_ref[...] + r * (t - n1 * jnp.mean(t * n1, axis=-1, keepdims=True))
        dg1_ref[0:1, :] += jnp.sum(du1 * n1, axis=0, keepdims=True)

        @pl.when(i == nb - 1)
        def _():
            sm_ref[0:2, :] = _dot_hi_r(gsum_ref[0:8, :], fold_ref[...])[0:2, :]

    rev = lambda w, cb=0: pl.BlockSpec((TM, w), lambda i: (nb - 1 - i, cb))
    full = lambda a: pl.BlockSpec(a.shape, lambda i: (0,) * a.ndim)
    bd, fold, triu = _head_blockdiag(), _head_fold(), _tril(TM).T
    consts = (g1, win, qg, kg, bf, bd, fold, triu, _head_pick())
    return pl.pallas_call(
        body, name="mixer_in_bwd", grid=(nb,),
        in_specs=[rev(D_MODEL), rev(D_MODEL), rev(ATT_W, O_Q // ATT_W), rev(ATT_W, O_K // ATT_W), rev(128, O_F // 128),
                  rev(ATT_W), rev(ATT_W), rev(ATT_W), rev(2 * CONV_CH),
                  pl.BlockSpec((ATT_HEADS, TM), lambda i: (0, nb - 1 - i)), rev(ATT_W)] + [full(a) for a in consts],
        out_specs=[rev(N_INP), rev(D_MODEL), pl.BlockSpec((8, D_MODEL), lambda i: (0, 0)),
                   pl.BlockSpec((8, 128), lambda i: (0, 0))],
        out_shape=[jax.ShapeDtypeStruct((S, N_INP), BF16), jax.ShapeDtypeStruct((S, D_MODEL), F32),
                   jax.ShapeDtypeStruct((8, D_MODEL), F32), jax.ShapeDtypeStruct((8, 128), F32)],
        scratch_shapes=[pltpu.VMEM((1, 128), F32), pltpu.VMEM((8, ATT_W), F32)],
        compiler_params=_params(1),
    )(x, dx2, proj, proj, proj, dq, dk, dv, dag, dct, drb, *consts)


def _layer_fwd(x, early, late):
    p = early(x)
    u1, proj, q, k, v, qa, ka, h0, h1, h3 = _mixer_in_fwd(
        x, p["g1"], p["win"], p["qg"], p["kg"], p["bf"], p["cw"], p["cvb"], p["lng"], p["lnb"])
    att, lse = _attn_fwd(q, qa, k, ka, v)
    p = dict(p, **late(att))
    x2, u2 = _mixer_out_fwd(x, att, h3, p["wo"], p["g2"])
    x3, z, hh = _mlp_fwd(x2, u2, p["w1"], p["w2"])
    saved = dict(x=x, u1=u1, proj=proj, q=q, k=k, v=v, qa=qa, ka=ka, h0=h0, h1=h1, h3=h3, att=att, lse=lse,
                 x2=x2, u2=u2, z=z, hh=hh)
    return x3, saved, p


def _tie(a, token):
    return a if token is None else a + token[0:1, 0:1]


def _layer_bwd(dx3, s, p, reduce):
    dz, dx2, dg2 = _mlp_bwd(dx3, s["z"], s["x2"], p["g2"], p["w1"], p["w2"])
    g_w2 = _matmul_tn(s["hh"], dx3)
    g_w1 = _matmul_tn(s["u2"], dz, col_shards=N_CHIPS)
    token = reduce("a", {2: g_w1, 3: g_w2.reshape(N_CHIPS, D_FF // N_CHIPS, D_MODEL)})
    datt, dlt, dh1, sm_c = _mixer_out_bwd(dx2, p["wo"], s["att"], s["h1"], _tie(p["lng"], token), p["lnb"])
    g_wo = jnp.concatenate([_matmul_tn(s["att"], dx2)[0], _matmul_tn(s["h3"], dx2)[0]], axis=0)
    dag, dcw = _conv_glu_bwd(dh1, s["h0"], s["proj"], p["cw"])
    dq, dk, dv, dc4, drb = _attn_bwd(s["q"], s["qa"], s["k"], s["ka"], s["v"], datt, s["lse"],
                                     dlt.reshape(ATT_HEADS // 2, 2, dlt.shape[1]))
    dct = dc4.reshape(ATT_HEADS, dc4.shape[2])
    dproj, dx, dg1, sm_a = _mixer_in_bwd(s["x"], dx2, s["proj"], dq, dk, dv, dag, dct, drb,
                                         p["g1"], p["win"], p["qg"], p["kg"], p["bf"])
    g_win = _win_to_global(_matmul_tn(s["u1"], dproj)[0])
    g_win = g_win.reshape(D_MODEL, N_CHIPS, N_IN // N_CHIPS).transpose(1, 0, 2)
    token = reduce("b", {0: g_win, 1: g_wo.reshape(N_CHIPS, D_MODEL // N_CHIPS, D_MODEL)})
    small = dict(g1=dg1[0], g2=dg2[0], lng=sm_c[0], lnb=sm_c[1], cvb=sm_c[2], cw=dcw[0:CONV_TAPS],
                 qg=sm_a[0, 0:HEAD_DIM], kg=sm_a[1, 0:HEAD_DIM], bf=sm_a[2, 0:ATT_HEADS])
    return dx, small, token


def _local_step(x, target, weights, reduce):
    saved, layers = [], []
    h = x
    for early, late in weights:
        h, s, p = _layer_fwd(h, early, late)
        saved.append(s)
        layers.append(p)
    dy, loss_acc = _loss_fwd_bwd(h, target)
    loss = loss_acc[0, 0] * (0.5 / D_MODEL)
    smalls = []
    d, token = dy, None
    for l in reversed(range(len(layers))):
        d, small, token = _layer_bwd(d, saved[l], dict(layers[l], g2=_tie(layers[l]["g2"], token)), reduce(l))
        smalls.append(small)
    return loss, d, smalls[::-1]


def _win_to_internal(w):
    pad = jnp.zeros(w.shape[:-1] + (N_INP - N_IN,), w.dtype)
    return jnp.concatenate([w[..., :1536], w[..., 1544:], w[..., 1536:1544], pad], axis=-1)


def _win_to_global(g):
    return jnp.concatenate([g[..., :1536], g[..., O_F:O_F + ATT_HEADS], g[..., 1536:O_F]], axis=-1)


def _layer_params(l, win, cw, norm1_g, b_f, q_norm_g, k_norm_g, conv_b, conv_ln_g, conv_ln_b, norm2_g):
    row = lambda a: a.reshape(1, -1)
    return dict(
        win=win, cw=jnp.pad(cw, ((0, CONV_HALO - CONV_TAPS), (0, 0))),
        g1=row(norm1_g[l]), g2=row(norm2_g[l]),
        qg=row(jnp.tile(q_norm_g[l], ATT_HEADS)), kg=row(jnp.tile(k_norm_g[l], ATT_HEADS)),
        bf=row(jnp.pad(b_f[l], (0, 128 - ATT_HEADS))),
        cvb=row(conv_b[l]), lng=row(conv_ln_g[l]), lnb=row(conv_ln_b[l]))


def _place():
    x, y, c = lax.axis_index("x"), lax.axis_index("y"), lax.axis_index("c")
    chips = [(1 - x, y), (x, 1 - y), (1 - x, 1 - y)]
    return x, y, c, chips


HBM = pl.BlockSpec(memory_space=pltpu.HBM)
SEM = pl.BlockSpec(memory_space=pltpu.SEMAPHORE)
GATHER_PEERS = N_CHIPS


def _gather_peers():
    x, y, c, chips = _place()
    return [(*chip, c) for chip in chips] + [(x, y, 1 - c)], [2 * px + py for px, py in chips] + [2 * x + y]


def _gather_start(srcs):
    n = len(srcs)

    def body(*refs):
        ins, lands = refs[:n], refs[n:2 * n]
        send_sems, recv_sems, token = refs[2 * n], refs[2 * n + 1], refs[-1]
        me = 2 * lax.axis_index("x") + lax.axis_index("y")
        peers, _ = _gather_peers()
        for g in range(n):
            for j, to in enumerate(peers):
                pltpu.make_async_remote_copy(src_ref=ins[g], dst_ref=lands[g].at[me],
                                             send_sem=send_sems.at[GATHER_PEERS * g + j],
                                             recv_sem=recv_sems.at[GATHER_PEERS * g + j],
                                             device_id=to, device_id_type=MESH).start()
        token[...] = jnp.zeros_like(token)

    lands = [lax.empty((N_CHIPS,) + a.shape, a.dtype) for a in srcs]
    outs = pl.pallas_call(
        body, name="gather_start",
        in_specs=[HBM] * (2 * n),
        out_specs=[SEM, SEM] + [HBM] * (2 * n) + [pl.BlockSpec(memory_space=pltpu.VMEM)],
        out_shape=[pltpu.SemaphoreType.DMA((GATHER_PEERS * n,)), pltpu.SemaphoreType.DMA((GATHER_PEERS * n,))]
        + [pltpu.HBM(a.shape, a.dtype) for a in srcs] + [pltpu.HBM(a.shape, a.dtype) for a in lands]
        + [jax.ShapeDtypeStruct((8, 128), F32)],
        input_output_aliases={i: 2 + i for i in range(2 * n)},
        compiler_params=pltpu.CompilerParams(has_side_effects=pltpu.SideEffectType.DATAFLOW_SIDE_EFFECTING),
    )(*[pltpu.with_memory_space_constraint(a, pltpu.HBM) for a in srcs],
      *[pltpu.with_memory_space_constraint(a, pltpu.HBM) for a in lands])
    return outs[0], outs[1], list(outs[2:2 + n]), list(outs[2 + n:2 + 2 * n]), outs[-1]


def _gather_wait(name, groups, send_sems, recv_sems, srcs, lands, after):
    k = len(groups)

    def body(*refs):
        ins, lnd = refs[:k], refs[k:2 * k]
        ssem, rsem = refs[2 * k], refs[2 * k + 1]
        peers, slots = _gather_peers()
        for i, g in enumerate(groups):
            for j, to in enumerate(peers):
                cp = pltpu.make_async_remote_copy(src_ref=ins[i], dst_ref=lnd[i].at[slots[j]],
                                                  send_sem=ssem.at[GATHER_PEERS * g + j],
                                                  recv_sem=rsem.at[GATHER_PEERS * g + j],
                                                  device_id=to, device_id_type=MESH)
                cp.wait_send()
                cp.wait_recv()

    outs = pl.pallas_call(
        body, name=name,
        in_specs=[HBM] * (2 * k) + [SEM, SEM, ANY],
        out_specs=[HBM] * (2 * k),
        out_shape=[pltpu.HBM(a.shape, a.dtype) for a in srcs] + [pltpu.HBM(a.shape, a.dtype) for a in lands],
        input_output_aliases={i: i for i in range(2 * k)},
        compiler_params=pltpu.CompilerParams(has_side_effects=pltpu.SideEffectType.DATAFLOW_SIDE_EFFECTING),
    )(*srcs, *lands, send_sems, recv_sems, after)
    return list(outs[k:])


REDUCE_STEPS = 8


def _row_half(a):
    return a.shape[-2] // 2


def _half_swap(name, gs):
    n = len(gs)

    def body(*refs):
        ins, outs = refs[:n], refs[n:2 * n]
        send_sems, recv_sems = refs[2 * n:]
        x, y, c, _ = _place()
        copies = []
        for a in range(n):
            h = _row_half(gs[a])
            copies.append(pltpu.make_async_remote_copy(
                src_ref=ins[a].at[:, pl.ds((1 - c) * h, h), :], dst_ref=outs[a], send_sem=send_sems.at[a],
                recv_sem=recv_sems.at[a], device_id=(x, y, 1 - c), device_id_type=MESH))
        for cp in copies:
            cp.start()
        for cp in copies:
            cp.wait()

    return pl.pallas_call(
        body, name=name,
        in_specs=[ANY] * n, out_specs=[ANY] * n,
        out_shape=[jax.ShapeDtypeStruct((N_CHIPS, _row_half(a), a.shape[-1]), a.dtype) for a in gs],
        scratch_shapes=[pltpu.SemaphoreType.DMA((n,)), pltpu.SemaphoreType.DMA((n,))],
        compiler_params=pltpu.CompilerParams(has_side_effects=True),
    )(*gs)


def _half_specs(gs, row_block):
    tiles = [_row_half(a) // REDUCE_STEPS for a in gs]
    return [pl.BlockSpec((N_CHIPS, t, a.shape[-1]), lambda i, p: (0, row_block(i, p), 0)) for a, t in zip(gs, tiles)]


def _half_add(name, gs, got, place):
    n = len(gs)

    def body(place_ref, *refs):
        own, theirs, outs = refs[:n], refs[n:2 * n], refs[2 * n:]
        for a in range(n):
            outs[a][...] = (own[a][...].astype(F32) + theirs[a][...].astype(F32)).astype(BF16)

    plain = _half_specs(gs, lambda i, p: i)
    return pl.pallas_call(
        body, name=name,
        grid_spec=pltpu.PrefetchScalarGridSpec(
            num_scalar_prefetch=1, grid=(REDUCE_STEPS,),
            in_specs=_half_specs(gs, lambda i, p: p[1] * REDUCE_STEPS + i) + plain, out_specs=plain),
        out_shape=[jax.ShapeDtypeStruct(a.shape, BF16) for a in got],
        compiler_params=_params(1),
    )(place, *gs, *got)


def _exchange_copies(parts, lands, send_sems, recv_sems):
    x, y, c, chips = _place()
    return [pltpu.make_async_remote_copy(src_ref=parts[a].at[2 * px + py], dst_ref=lands[a].at[j],
                                         send_sem=send_sems.at[3 * a + j], recv_sem=recv_sems.at[3 * a + j],
                                         device_id=(px, py, c), device_id_type=MESH)
            for a in range(len(parts)) for j, (px, py) in enumerate(chips)]


def _exchange_start(name, parts):
    n = len(parts)

    def body(*refs):
        _ = [cp.start() for cp in _exchange_copies(refs[:n], refs[n:2 * n], refs[2 * n], refs[2 * n + 1])]
        refs[-1][...] = jnp.zeros_like(refs[-1])

    lands = [lax.empty((N_CHIPS - 1,) + a.shape[1:], a.dtype) for a in parts]
    outs = pl.pallas_call(
        body, name=name,
        in_specs=[HBM] * (2 * n),
        out_specs=[SEM, SEM] + [HBM] * (2 * n) + [pl.BlockSpec(memory_space=pltpu.VMEM)],
        out_shape=[pltpu.SemaphoreType.DMA((3 * n,)), pltpu.SemaphoreType.DMA((3 * n,))]
        + [pltpu.HBM(a.shape, a.dtype) for a in parts] + [pltpu.HBM(a.shape, a.dtype) for a in lands]
        + [jax.ShapeDtypeStruct((8, 128), F32)],
        input_output_aliases={i: 2 + i for i in range(2 * n)},
        compiler_params=pltpu.CompilerParams(has_side_effects=pltpu.SideEffectType.DATAFLOW_SIDE_EFFECTING),
    )(*[pltpu.with_memory_space_constraint(a, pltpu.HBM) for a in parts],
      *[pltpu.with_memory_space_constraint(a, pltpu.HBM) for a in lands])
    return outs[0], outs[1], list(outs[2:2 + n]), list(outs[2 + n:2 + 2 * n]), outs[-1]


def _exchange_wait(name, send_sems, recv_sems, parts, lands, after):
    n = len(parts)

    def body(*refs):
        for cp in _exchange_copies(refs[:n], refs[n:2 * n], refs[2 * n], refs[2 * n + 1]):
            cp.wait_send()
            cp.wait_recv()

    outs = pl.pallas_call(
        body, name=name,
        in_specs=[HBM] * (2 * n) + [SEM, SEM, ANY],
        out_specs=[HBM] * (2 * n),
        out_shape=[pltpu.HBM(a.shape, a.dtype) for a in parts] + [pltpu.HBM(a.shape, a.dtype) for a in lands],
        input_output_aliases={i: i for i in range(2 * n)},
        compiler_params=pltpu.CompilerParams(has_side_effects=pltpu.SideEffectType.DATAFLOW_SIDE_EFFECTING),
    )(*parts, *lands, send_sems, recv_sems, after)
    return list(outs[:n]), list(outs[n:])


def _chip_sum(name, parts, lands, sums, place, layer):
    n = len(parts)
    tiles = [a.shape[-2] // REDUCE_STEPS for a in parts]

    def body(place_ref, *refs):
        own, got, outs = refs[:n], refs[n:2 * n], refs[3 * n:]
        for a in range(n):
            tot = own[a][...].astype(F32)
            for j in range(N_CHIPS - 1):
                tot = tot + got[a][j].astype(F32)
            outs[a][...] = tot

    own_specs = [pl.BlockSpec((None, t, a.shape[-1]), lambda i, p: (p[0], i, 0)) for a, t in zip(parts, tiles)]
    got_specs = [pl.BlockSpec((N_CHIPS - 1, t, a.shape[-1]), lambda i, p: (0, i, 0)) for a, t in zip(parts, tiles)]
    out_specs = [pl.BlockSpec((None, t, a.shape[-1]), lambda i, p: (layer, p[1] * REDUCE_STEPS + i, 0))
                 for a, t in zip(parts, tiles)]
    return pl.pallas_call(
        body, name=name,
        grid_spec=pltpu.PrefetchScalarGridSpec(num_scalar_prefetch=1, grid=(REDUCE_STEPS,),
                                               in_specs=own_specs + got_specs + [ANY] * n, out_specs=out_specs),
        out_shape=[jax.ShapeDtypeStruct(a.shape, F32) for a in sums],
        input_output_aliases={1 + 2 * n + a: a for a in range(n)},
        compiler_params=_params(1),
    )(place, *parts, *lands, *sums)


def _half_fill(sums):
    n = len(sums)

    def body(*refs):
        ins, outs = refs[:n], refs[n:2 * n]
        send_sems, recv_sems = refs[2 * n:]
        x, y, c, _ = _place()
        copies = []
        for a in range(n):
            h = _row_half(sums[a])
            copies.append(pltpu.make_async_remote_copy(
                src_ref=ins[a].at[:, pl.ds(c * h, h), :], dst_ref=outs[a].at[:, pl.ds(c * h, h), :],
                send_sem=send_sems.at[a], recv_sem=recv_sems.at[a], device_id=(x, y, 1 - c), device_id_type=MESH))
        for cp in copies:
            cp.start()
        for a in range(n):
            h = _row_half(sums[a])
            theirs = outs[a].at[:, pl.ds((1 - c) * h, h), :]
            pltpu.make_async_remote_copy(src_ref=theirs, dst_ref=theirs, send_sem=send_sems.at[a], recv_sem=recv_sems.at[a],
                                         device_id=(x, y, 1 - c), device_id_type=MESH).wait_recv()
        for cp in copies:
            cp.wait_send()

    return pl.pallas_call(
        body, name="half_fill",
        in_specs=[ANY] * n, out_specs=[ANY] * n,
        out_shape=[jax.ShapeDtypeStruct(a.shape, a.dtype) for a in sums],
        input_output_aliases={a: a for a in range(n)},
        scratch_shapes=[pltpu.SemaphoreType.DMA((n,)), pltpu.SemaphoreType.DMA((n,))],
        compiler_params=pltpu.CompilerParams(has_side_effects=True),
    )(*sums)


def _adamw_math(w, g, m, v):
    m = ADAM_B1 * m + (1.0 - ADAM_B1) * g
    v = ADAM_B2 * v + (1.0 - ADAM_B2) * (g * g)
    m_hat = m / (1.0 - ADAM_B1 ** ADAM_STEP)
    v_hat = v / (1.0 - ADAM_B2 ** ADAM_STEP)
    delta = -ADAM_LR * (m_hat / (jnp.sqrt(v_hat) + ADAM_EPS) + ADAM_WD * w)
    return delta, m, v


def _adamw(ws, gs, ms, vs):
    n = len(ws)
    steps = 16
    tiles = [a.shape[-2] // steps for a in ws]

    def body(*refs):
        w_r, g_r, m_r, v_r = refs[:n], refs[n:2 * n], refs[2 * n:3 * n], refs[3 * n:4 * n]
        g_o, d_o, m_o, v_o = refs[4 * n:5 * n], refs[5 * n:6 * n], refs[6 * n:7 * n], refs[7 * n:]
        for a in range(n):
            g = g_r[a][...]
            d, m, v = _adamw_math(w_r[a][...], g, m_r[a][...], v_r[a][...])
            g_o[a][...] = g
            d_o[a][...] = d
            m_o[a][...] = m
            v_o[a][...] = v

    specs = [pl.BlockSpec((2, t, a.shape[-1]), lambda i: (0, i, 0)) for a, t in zip(ws, tiles)]
    outs = pl.pallas_call(
        body, name="adamw", grid=(steps,),
        in_specs=specs * 4, out_specs=specs * 4,
        out_shape=[jax.ShapeDtypeStruct(a.shape, F32) for a in ws] * 4,
        compiler_params=_params(1),
    )(*ws, *gs, *ms, *vs)
    return outs[:n], outs[n:2 * n], outs[2 * n:3 * n], outs[3 * n:]


SMALL_W = 512


def _small_allreduce_adamw(g, w, m, v, cw_w, cw_m, cw_v, cw_row0):
    R = g.shape[0]
    n_l = cw_w.shape[0]

    def body(g_ref, w_ref, m_ref, v_ref, cww_ref, cwm_ref, cwv_ref,
             gs_ref, d_ref, mo_ref, vo_ref, cg_ref, cd_ref, cmo_ref, cvo_ref,
             slots_ref, send_sems, recv_sems):
        x, y, c, _ = _place()
        me = 4 * x + 2 * y + c
        slots_ref[me] = g_ref[...]
        sends = []
        for d in range(1, 8):
            px, py, pc = x ^ (d >> 2), y ^ ((d >> 1) & 1), c ^ (d & 1)
            cp = pltpu.make_async_remote_copy(src_ref=g_ref, dst_ref=slots_ref.at[me], send_sem=send_sems.at[d - 1],
                                              recv_sem=recv_sems.at[d - 1], device_id=(px, py, pc), device_id_type=MESH)
            cp.start()
            sends.append(cp)
        for d in range(1, 8):
            px, py, pc = x ^ (d >> 2), y ^ ((d >> 1) & 1), c ^ (d & 1)
            slot = slots_ref.at[4 * px + 2 * py + pc]
            pltpu.make_async_remote_copy(src_ref=slot, dst_ref=slot, send_sem=send_sems.at[d - 1],
                                         recv_sem=recv_sems.at[d - 1], device_id=(px, py, pc),
                                         device_id_type=MESH).wait_recv()
        for cp in sends:
            cp.wait_send()
        tot = slots_ref[0]
        for k in range(1, 8):
            tot = tot + slots_ref[k]
        gs_ref[...] = tot
        dl, mn, vn = _adamw_math(w_ref[...], tot, m_ref[...], v_ref[...])
        d_ref[...] = dl
        mo_ref[...] = mn
        vo_ref[...] = vn
        chip = 2 * x + y
        for l in range(n_l):
            rows = tot[cw_row0[l]:cw_row0[l] + CONV_HALO, :]
            mine = rows[:, 0:128]
            for k in range(1, N_CHIPS):
                mine = jnp.where(chip == k, rows[:, 128 * k:128 * (k + 1)], mine)
            cg_ref[l] = mine
            dl, mn, vn = _adamw_math(cww_ref[l], mine, cwm_ref[l], cwv_ref[l])
            cd_ref[l] = dl
            cmo_ref[l] = mn
            cvo_ref[l] = vn

    vm = pl.BlockSpec(memory_space=pltpu.VMEM)
    small = jax.ShapeDtypeStruct((R, SMALL_W), F32)
    conv = jax.ShapeDtypeStruct(cw_w.shape, F32)
    return pl.pallas_call(
        body, name="small_allreduce_adamw",
        in_specs=[vm] * 7, out_specs=[vm] * 8,
        out_shape=[small] * 4 + [conv] * 4,
        scratch_shapes=[pltpu.VMEM((8, R, SMALL_W), F32), pltpu.SemaphoreType.DMA((7,)), pltpu.SemaphoreType.DMA((7,))],
        compiler_params=pltpu.CompilerParams(has_side_effects=True, vmem_limit_bytes=VMEM_LIMIT),
    )(g, w, m, v, cw_w, cw_m, cw_v)


SMALL_LAYOUT = (("conv_w", CONV_HALO), ("norm1_g", 2), ("norm2_g", 2), ("conv_b", 1), ("conv_ln_g", 1),
                ("conv_ln_b", 1), ("q_norm_g", 1), ("k_norm_g", 1), ("b_f", 1))
SMALL_ROWS = sum(r for _, r in SMALL_LAYOUT)
SMALL_ROWS_PAD = 48
LOSS_ROW = SMALL_ROWS


def _pack_small(per_layer):
    blocks = []
    for d in per_layer:
        rows = []
        for name, r in SMALL_LAYOUT:
            if name == "conv_w":
                a = d.get(name)
                a = jnp.zeros((r, SMALL_W), F32) if a is None else jnp.pad(a, ((0, r - a.shape[0]), (0, 0)))
            else:
                a = d[name].reshape(-1)
                a = jnp.pad(a, (0, r * SMALL_W - a.shape[0])).reshape(r, SMALL_W)
            rows.append(a)
        rows.append(jnp.zeros((SMALL_ROWS_PAD - SMALL_ROWS, SMALL_W), F32))
        blocks.append(jnp.concatenate(rows, axis=0))
    return jnp.concatenate(blocks, axis=0)


def _unpack_small(packed, name, size):
    n_l = packed.shape[0] // SMALL_ROWS_PAD
    row0 = 0
    for nm, r in SMALL_LAYOUT:
        if nm == name:
            break
        row0 += r
    out = [packed[l * SMALL_ROWS_PAD + row0:l * SMALL_ROWS_PAD + row0 + r].reshape(-1)[:size] for l in range(n_l)]
    return jnp.stack(out)


SMALL_SIZES = dict(norm1_g=D_MODEL, norm2_g=D_MODEL, conv_b=CONV_CH, conv_ln_g=CONV_CH, conv_ln_b=CONV_CH,
                   q_norm_g=HEAD_DIM, k_norm_g=HEAD_DIM, b_f=ATT_HEADS)
SMALL_KEYS = dict(norm1_g="g1", norm2_g="g2", conv_b="cvb", conv_ln_g="lng", conv_ln_b="lnb",
                  q_norm_g="qg", k_norm_g="kg", b_f="bf", conv_w="cw")
CONV_W_ROW0 = 0


def kernel(x, norm1_g, w_in, b_f, q_norm_g, k_norm_g, conv_w, conv_b, conv_ln_g, conv_ln_b, w_o, norm2_g, w_mlp_in, w_mlp_out, loss_target, m_norm1_g, m_w_in, m_b_f, m_q_norm_g, m_k_norm_g, m_conv_w, m_conv_b, m_conv_ln_g, m_conv_ln_b, m_w_o, m_norm2_g, m_w_mlp_in, m_w_mlp_out, v_norm1_g, v_w_in, v_b_f, v_q_norm_g, v_k_norm_g, v_conv_w, v_conv_b, v_conv_ln_g, v_conv_ln_b, v_w_o, v_norm2_g, v_w_mlp_in, v_w_mlp_out):
    n_l = w_in.shape[0]

    per_layer = lambda l: [w_in[l].astype(BF16), conv_w[l], w_o[l].astype(BF16), w_mlp_in[l].astype(BF16),
                           w_mlp_out[l].astype(BF16)]
    n_w = len(per_layer(0))
    send_sems, recv_sems, srcs, lands, token = _gather_start([a for l in range(n_l) for a in per_layer(l)])

    def layer_weights(l):
        def wait(tag, which, after):
            groups = [n_w * l + i for i in which]
            return _gather_wait(f"gather_wait_{tag}{l}", groups, send_sems, recv_sems,
                                [srcs[g] for g in groups], [lands[g] for g in groups], after)

        def early(after):
            g_in, g_cw = wait("a", (0, 1), token if l == 0 else after)
            win = _win_to_internal(jnp.concatenate([g_in[k] for k in range(N_CHIPS)], axis=-1))
            cw = jnp.concatenate([g_cw[k] for k in range(N_CHIPS)], axis=-1)
            return _layer_params(l, win, cw, norm1_g, b_f, q_norm_g, k_norm_g, conv_b, conv_ln_g, conv_ln_b, norm2_g)

        def late(after):
            wo, w1, w2 = wait("b", (2, 3, 4), after)
            return dict(wo=wo, w1=w1, w2=w2)

        return early, late

    place = jnp.stack([2 * lax.axis_index("x") + lax.axis_index("y"), lax.axis_index("c")]).astype(jnp.int32)
    big_w = [w_in, w_o, w_mlp_in, w_mlp_out]
    pending = []

    def reduce(l):
        def group(tag, grads):
            which, gs = list(grads), list(grads.values())
            got = _half_swap(f"half_swap_{tag}{l}", gs)
            parts = _half_add(f"half_add_{tag}{l}", gs, got, place)
            send, recv, parts, lands, token = _exchange_start(f"exchange_start_{tag}{l}", parts)
            pending.append((f"{tag}{l}", l, which, send, recv, parts, lands))
            return token
        return group

    loss, dx, smalls = _local_step(x[0], loss_target[0], [layer_weights(l) for l in range(n_l)], reduce)
    sums = [lax.empty(w.shape, F32) for w in big_w]
    for tag, l, which, send, recv, parts, lands in pending:
        parts, lands = _exchange_wait(f"exchange_wait_{tag}", send, recv, parts, lands, dx)
        done = _chip_sum(f"chip_sum_{tag}", parts, lands, [sums[i] for i in which], place, l)
        for i, a in zip(which, done):
            sums[i] = a
    g_big = _half_fill(sums)
    big_m = [m_w_in, m_w_o, m_w_mlp_in, m_w_mlp_out]
    big_v = [v_w_in, v_w_o, v_w_mlp_in, v_w_mlp_out]
    g_big, d_big, nm_big, nv_big = _adamw(big_w, g_big, big_m, big_v)

    env = dict(norm1_g=(norm1_g, m_norm1_g, v_norm1_g), norm2_g=(norm2_g, m_norm2_g, v_norm2_g),
               conv_b=(conv_b, m_conv_b, v_conv_b), conv_ln_g=(conv_ln_g, m_conv_ln_g, v_conv_ln_g),
               conv_ln_b=(conv_ln_b, m_conv_ln_b, v_conv_ln_b), q_norm_g=(q_norm_g, m_q_norm_g, v_q_norm_g),
               k_norm_g=(k_norm_g, m_k_norm_g, v_k_norm_g), b_f=(b_f, m_b_f, v_b_f))
    g_pack = _pack_small([{nm: s[key] for nm, key in SMALL_KEYS.items()} for s in smalls])
    g_pack = g_pack.at[LOSS_ROW, 0].set(loss)
    packs = [_pack_small([{nm: env[nm][t][l] for nm in env} for l in range(n_l)]) for t in range(3)]
    pad_cw = lambda a: jnp.pad(a, ((0, 0), (0, CONV_HALO - CONV_TAPS), (0, 0)))
    cw_row0 = tuple(l * SMALL_ROWS_PAD + CONV_W_ROW0 for l in range(n_l))
    gs, ds, ms, vs, cg, cd, cm, cv = _small_allreduce_adamw(
        g_pack, packs[0], packs[1], packs[2], pad_cw(conv_w), pad_cw(m_conv_w), pad_cw(v_conv_w), cw_row0)

    def small_out(packed, conv):
        o = {nm: _unpack_small(packed, nm, sz) for nm, sz in SMALL_SIZES.items()}
        o["conv_w"] = conv[:, 0:CONV_TAPS, :]
        return o

    def ordered(small, big):
        return (small["norm1_g"], big[0], small["b_f"], small["q_norm_g"], small["k_norm_g"], small["conv_w"],
                small["conv_b"], small["conv_ln_g"], small["conv_ln_b"], big[1], small["norm2_g"], big[2], big[3])

    return (gs[LOSS_ROW, 0], dx[None],
            *ordered(small_out(gs, cg), g_big), *ordered(small_out(ds, cd), d_big),
            *ordered(small_out(ms, cm), nm_big), *ordered(small_out(vs, cv), nv_big))
```

```python
import functools

import jax
import jax.numpy as jnp
from jax import lax
from jax.experimental import pallas as pl
from jax.experimental.pallas import tpu as pltpu

F32 = jnp.float32
BF16 = jnp.bfloat16

D_MODEL = 1024
ATT_HEADS = 8
HEAD_DIM = 64
ATT_W = ATT_HEADS * HEAD_DIM
CONV_CH = 512
CONV_TAPS = 31
CONV_HALO = 32
D_FF = 4 * D_MODEL
N_IN = 3 * ATT_W + ATT_HEADS + 2 * CONV_CH
O_Q, O_K, O_V, O_A, O_G, O_F = 0, 512, 1024, 1536, 2048, 2560
N_INP = O_F + 128
EPS = 1e-6
QK_SCALE = 0.125

ADAM_LR = 0.001
ADAM_B1 = 0.9
ADAM_B2 = 0.999
ADAM_EPS = 1e-08
ADAM_WD = 0.01
ADAM_STEP = 10

N_CHIPS = 4
VMEM_LIMIT = 52 * 1024 * 1024
MESH = pl.DeviceIdType.MESH
ANY = pl.BlockSpec(memory_space=pl.ANY)


def _params(n_axes, **kw):
    return pltpu.CompilerParams(dimension_semantics=("arbitrary",) * n_axes,
                                vmem_limit_bytes=VMEM_LIMIT, **kw)


def _dot(a, b):
    return jnp.dot(a, b, preferred_element_type=F32)


def _dot_nt(a, b):
    return lax.dot_general(a, b, (((1,), (1,)), ((), ())), preferred_element_type=F32)


def _dot_tn(a, b):
    return lax.dot_general(a, b, (((0,), (0,)), ((), ())), preferred_element_type=F32)


def _split3(a):
    a1 = a.astype(BF16)
    r = a - a1.astype(F32)
    a2 = r.astype(BF16)
    a3 = (r - a2.astype(F32)).astype(BF16)
    return a1, a2, a3


def _dot_hi_r(a, b_exact):
    return sum(_dot(p, b_exact) for p in _split3(a))


def _dot_hi_l(a_exact, b):
    return sum(_dot(a_exact, p) for p in _split3(b))


def _sigmoid(x):
    return 1.0 / (1.0 + jnp.exp(-x))


def _head_blockdiag():
    i = jnp.arange(ATT_W) // HEAD_DIM
    return (i[:, None] == i[None, :]).astype(BF16)


AUG_LANES = 8


def _aug_place(first):
    piece = jnp.arange(3 * 128)[:, None] // 128
    h = jnp.arange(3 * 128)[:, None] % 128
    lane = jnp.arange(ATT_W)[None, :]
    return ((h < ATT_HEADS) & (lane == 128 * (h // 2) + AUG_LANES * (h % 2) + first + piece)).astype(BF16)


def _aug_ones(first):
    lane = jnp.arange(ATT_W) % 128
    pos = lane % AUG_LANES
    return ((lane < 2 * AUG_LANES) & (pos >= first) & (pos < first + 3)).astype(F32).reshape(1, ATT_W)


def _head_rows():
    h = jnp.arange(2 * ATT_HEADS)[:, None]
    i = jnp.arange(ATT_W)[None, :] // HEAD_DIM
    return (h == i).astype(BF16)


def _head_fold():
    i = jnp.arange(ATT_W)[:, None] % HEAD_DIM
    j = jnp.arange(128)[None, :]
    return (i == j).astype(BF16)


def _head_pick():
    i = jnp.arange(ATT_W)[:, None]
    h = jnp.arange(128)[None, :]
    return (i == h * HEAD_DIM).astype(BF16)


def _tril(n):
    r = jnp.arange(n)
    return (r[:, None] >= r[None, :]).astype(BF16)


SUBLANES = 8


def _fill_row_shifts(buf_ref, shifts_ref, tm):
    n = tm + CONV_HALO - SUBLANES
    for b in range(1, SUBLANES):
        shifts_ref[b - 1, 0:n, :] = buf_ref[pl.ds(b, n), :]


def _row_shifted(buf_ref, shifts_ref, offset, rows, base=0):
    a, b = divmod(offset, SUBLANES)
    start = pl.multiple_of(base + SUBLANES * a, SUBLANES)
    if b == 0:
        return buf_ref[pl.ds(start, rows), :]
    return shifts_ref[b - 1, pl.ds(start, rows), :]


CONV_ROWS = 32


def _mixer_in_fwd(x, g1, win, qg, kg, bf, cw, cvb, lng, lnb):
    S = x.shape[0]
    TM = min(512, S)
    nb = S // TM

    def body(x_ref, g1_ref, win_ref, qg_ref, kg_ref, bf_ref, cw_ref, cvb_ref, lng_ref, lnb_ref,
             bd_ref, tri_ref, pq_ref, pk_ref, oq_ref, ok_ref,
             u1_ref, proj_ref, q_ref, k_ref, v_ref, qa_ref, ka_ref, h0_ref, h1_ref, h3_ref,
             carry_ref, hbuf_ref, hs_ref):
        i = pl.program_id(0)

        @pl.when(i == 0)
        def _():
            carry_ref[...] = jnp.zeros_like(carry_ref)
            hbuf_ref[0:CONV_HALO, :] = jnp.zeros((CONV_HALO, CONV_CH), F32)

        @pl.when(i > 0)
        def _():
            hbuf_ref[0:CONV_HALO, :] = hbuf_ref[TM:TM + CONV_HALO, :]

        xv = x_ref[...]
        r = lax.rsqrt(jnp.mean(xv * xv, axis=-1, keepdims=True) + EPS)
        u = (xv * r * g1_ref[...]).astype(BF16)
        u1_ref[...] = u
        proj_ref[...] = _dot(u, win_ref[...])

        def headnorm(raw, gain):
            ss = _dot_hi_r(raw * raw, bd_ref[...]) * (1.0 / HEAD_DIM)
            return raw * lax.rsqrt(ss + EPS) * gain

        q_ref[...] = (headnorm(proj_ref[:, O_Q:O_Q + ATT_W], qg_ref[...]) * QK_SCALE).astype(BF16)
        k_ref[...] = headnorm(proj_ref[:, O_K:O_K + ATT_W], kg_ref[...]).astype(BF16)
        v_ref[...] = proj_ref[:, O_V:O_V + ATT_W].astype(BF16)

        zf = proj_ref[:, O_F:O_F + 128] + bf_ref[...]
        logf = jnp.minimum(zf, 0.0) - jnp.log(1.0 + jnp.exp(-jnp.abs(zf)))
        lane = lax.broadcasted_iota(jnp.int32, (TM, 128), 1)
        logf = jnp.where(lane < ATT_HEADS, logf, 0.0)
        c8 = _dot_hi_l(tri_ref[...], logf) + carry_ref[...]
        carry_ref[...] = c8[TM - 1:TM, :]
        pieces = jnp.concatenate(_split3(c8), axis=1)
        qa_ref[...] = (_dot(pieces, pq_ref[...]) + oq_ref[...]).astype(BF16)
        ka_ref[...] = (ok_ref[...] - _dot(pieces, pk_ref[...])).astype(BF16)

        h0 = proj_ref[:, O_A:O_A + CONV_CH] * _sigmoid(proj_ref[:, O_G:O_G + CONV_CH])
        h0_ref[...] = h0
        hbuf_ref[CONV_HALO:CONV_HALO + TM, :] = h0
        _fill_row_shifts(hbuf_ref, hs_ref, TM)
        acc = jnp.zeros((TM, CONV_CH), F32) + cvb_ref[...]
        for j in range(CONV_TAPS):
            acc = acc + cw_ref[j:j + 1, :] * _row_shifted(hbuf_ref, hs_ref, CONV_HALO - CONV_TAPS + 1 + j, TM)
        h1_ref[...] = acc
        mu = jnp.mean(acc, axis=-1, keepdims=True)
        d = acc - mu
        var = jnp.mean(d * d, axis=-1, keepdims=True)
        h2 = d * lax.rsqrt(var + EPS) * lng_ref[...] + lnb_ref[...]
        h3_ref[...] = (h2 * _sigmoid(h2)).astype(BF16)

    row = lambda w: pl.BlockSpec((TM, w), lambda i: (i, 0))
    full = lambda a: pl.BlockSpec(a.shape, lambda i: (0,) * a.ndim)
    ins = (x, g1, win, qg, kg, bf, cw, cvb, lng, lnb, _head_blockdiag(), _tril(TM),
           _aug_place(0), _aug_place(3), _aug_ones(3), _aug_ones(0))
    return pl.pallas_call(
        body, name="mixer_in_fwd", grid=(nb,),
        in_specs=[row(D_MODEL)] + [full(a) for a in ins[1:]],
        out_specs=[row(D_MODEL), row(N_INP), row(ATT_W), row(ATT_W), row(ATT_W), row(ATT_W), row(ATT_W),
                   row(CONV_CH), row(CONV_CH), row(CONV_CH)],
        out_shape=[jax.ShapeDtypeStruct((S, D_MODEL), BF16),
                   jax.ShapeDtypeStruct((S, N_INP), F32),
                   jax.ShapeDtypeStruct((S, ATT_W), BF16),
                   jax.ShapeDtypeStruct((S, ATT_W), BF16),
                   jax.ShapeDtypeStruct((S, ATT_W), BF16),
                   jax.ShapeDtypeStruct((S, ATT_W), BF16),
                   jax.ShapeDtypeStruct((S, ATT_W), BF16),
                   jax.ShapeDtypeStruct((S, CONV_CH), F32),
                   jax.ShapeDtypeStruct((S, CONV_CH), F32),
                   jax.ShapeDtypeStruct((S, CONV_CH), BF16)],
        scratch_shapes=[pltpu.VMEM((1, 128), F32), pltpu.VMEM((TM + CONV_HALO, CONV_CH), F32),
                        pltpu.VMEM((SUBLANES - 1, TM + CONV_HALO, CONV_CH), F32)],
        compiler_params=_params(1),
    )(*ins)


def _pair_heads(lo, alo, x, xa):
    z = jnp.zeros_like(x)
    return (jnp.concatenate([jnp.where(lo, x, z), jnp.where(alo, xa, z)], axis=1),
            jnp.concatenate([jnp.where(lo, z, x), jnp.where(alo, z, xa)], axis=1))


def _attn_fwd(q, qa, k, ka, v):
    S = q.shape[0]
    T = min(1024, S)
    nq = S // T

    def body(q_ref, qa_ref, k_ref, ka_ref, v_ref, o_ref, lse_ref):
        qi = pl.program_id(1)
        lane = lax.broadcasted_iota(jnp.int32, (T, 128), 1)
        lo = lane < HEAD_DIM
        qm = _pair_heads(lo, lane < AUG_LANES, q_ref[...], qa_ref[...])
        tril = (lax.broadcasted_iota(jnp.int32, (T, T), 0) >= lax.broadcasted_iota(jnp.int32, (T, T), 1))

        def step(kj, carry, masked):
            off = pl.multiple_of(kj * T, T)
            kb = jnp.concatenate([k_ref[pl.ds(off, T), :], ka_ref[pl.ds(off, T), :]], axis=1)
            vb = v_ref[pl.ds(off, T), :]
            new = []
            for h in range(2):
                m, l, acc = carry[3 * h:3 * h + 3]
                s = _dot_nt(qm[h], kb)
                if masked:
                    s = jnp.where(tril, s, -1e30)
                m_new = jnp.maximum(m, jnp.max(s, axis=-1, keepdims=True))
                alpha = jnp.exp(m - m_new)
                p = jnp.exp(s - m_new)
                l = alpha * l + jnp.sum(p, axis=-1, keepdims=True)
                acc = alpha * acc + _dot(p.astype(BF16), vb)
                new += [m_new, l, acc]
            return tuple(new)

        init = (jnp.full((T, 1), -1e30, F32), jnp.zeros((T, 1), F32), jnp.zeros((T, 128), F32)) * 2
        carry = lax.fori_loop(0, qi, lambda kj, c: step(kj, c, False), init)
        m0, l0, a0, m1, l1, a1 = step(qi, carry, True)
        o_ref[...] = jnp.where(lo, a0 / l0, a1 / l1).astype(BF16)
        lse_t = jnp.where(lo, m0 + jnp.log(l0), m1 + jnp.log(l1)).T
        lse_ref[0:1, :] = lse_t[0:1, :]
        lse_ref[1:2, :] = lse_t[HEAD_DIM:HEAD_DIM + 1, :]

    qblk = pl.BlockSpec((T, 128), lambda hp, qi: (qi, hp))
    seq = pl.BlockSpec((S, 128), lambda hp, qi: (0, hp))
    return pl.pallas_call(
        body, name="attn_fwd", grid=(ATT_HEADS // 2, nq),
        in_specs=[qblk, qblk, seq, seq, seq],
        out_specs=[qblk, pl.BlockSpec((None, 2, T), lambda hp, qi: (hp, 0, qi))],
        out_shape=[jax.ShapeDtypeStruct((S, ATT_W), BF16),
                   jax.ShapeDtypeStruct((ATT_HEADS // 2, 2, S), F32)],
        compiler_params=_params(2),
    )(q, qa, k, ka, v)


def _attn_bwd(q, qa, k, ka, v, do, lse, dlt):
    S = q.shape[0]
    T = min(512, S)
    nq = S // T

    def body(q_ref, qa_ref, do_ref, lse_ref, dlt_ref, k_ref, ka_ref, v_ref, dq_ref, dk_ref, dv_ref, rows_ref, cols_ref):
        kj = pl.program_id(1)

        @pl.when(kj == 0)
        def _():
            dq_ref[...] = jnp.zeros_like(dq_ref)
            rows_ref[...] = jnp.zeros_like(rows_ref)

        lane = lax.broadcasted_iota(jnp.int32, (T, 128), 1)
        lo = lane < HEAD_DIM
        alo = lane < AUG_LANES
        triu = (lax.broadcasted_iota(jnp.int32, (T, T), 0) <= lax.broadcasted_iota(jnp.int32, (T, T), 1))
        kb = k_ref[...]
        kaug = jnp.concatenate([kb, ka_ref[...]], axis=1)
        vb = v_ref[...]

        def step(qi, carry, masked):
            off = pl.multiple_of(qi * T, T)
            qb = q_ref[pl.ds(off, T), :]
            dob = do_ref[pl.ds(off, T), :]
            qm = _pair_heads(lo, alo, qb, qa_ref[pl.ds(off, T), :])
            zero = jnp.zeros_like(qb)
            new, dqs = [], []
            for h in range(2):
                dk_a, dv_a, dc_a = carry[3 * h:3 * h + 3]
                dom = jnp.where(lo, dob, zero) if h == 0 else jnp.where(lo, zero, dob)
                s = _dot_nt(kaug, qm[h])
                if masked:
                    s = jnp.where(triu, s, -1e30)
                p = jnp.exp(s - lse_ref[h:h + 1, pl.ds(off, T)])
                dp = _dot_nt(vb, dom)
                ds = p * (dp - dlt_ref[h:h + 1, pl.ds(off, T)])
                pb = p.astype(BF16)
                dsb = ds.astype(BF16)
                dv_a = dv_a + _dot(pb, dob)
                dk_a = dk_a + _dot(dsb, qb)
                dc_a = dc_a + jnp.sum(ds, axis=1, keepdims=True)
                dqs.append(_dot_tn(dsb, kb))
                rows_ref[h:h + 1, pl.ds(off, T)] += jnp.sum(ds, axis=0, keepdims=True)
                new += [dk_a, dv_a, dc_a]
            dq_ref[pl.ds(off, T), :] += jnp.where(lo, dqs[0], dqs[1])
            return tuple(new)

        init = (jnp.zeros((T, 128), F32), jnp.zeros((T, 128), F32), jnp.zeros((T, 1), F32)) * 2
        carry = step(kj, init, True)
        carry = lax.fori_loop(kj + 1, nq, lambda qi, c: step(qi, c, False), carry)
        dk_ref[...] = jnp.where(lo, carry[0], carry[3])
        dv_ref[...] = jnp.where(lo, carry[1], carry[4])
        cols_ref[...] = -jnp.where(lo, carry[2], carry[5])

    seq = pl.BlockSpec((S, 128), lambda hp, kj: (0, hp))
    rows = pl.BlockSpec((None, 2, S), lambda hp, kj: (hp, 0, 0))
    kblk = pl.BlockSpec((T, 128), lambda hp, kj: (kj, hp))
    return pl.pallas_call(
        body, name="attn_bwd", grid=(ATT_HEADS // 2, nq),
        in_specs=[seq, seq, seq, rows, rows, kblk, kblk, kblk],
        out_specs=[seq, kblk, kblk, rows, kblk],
        out_shape=[jax.ShapeDtypeStruct((S, ATT_W), F32), jax.ShapeDtypeStruct((S, ATT_W), F32),
                   jax.ShapeDtypeStruct((S, ATT_W), F32),
                   jax.ShapeDtypeStruct((ATT_HEADS // 2, 2, S), F32),
                   jax.ShapeDtypeStruct((S, ATT_W), F32)],
        compiler_params=_params(2),
    )(q, qa, do, lse, dlt, k, ka, v)


def _wo_spec(wo4):
    return pl.BlockSpec(wo4.shape, lambda i: (0, 0, 0))


def _wo_halves(wo_ref):
    half = N_CHIPS // 2
    return (wo_ref[0:half].reshape(ATT_W, D_MODEL), wo_ref[half:N_CHIPS].reshape(CONV_CH, D_MODEL))


def _mixer_out_fwd(x, att, h3, wo4, g2):
    S = x.shape[0]
    TM = min(512, S)

    def body(x_ref, att_ref, h3_ref, wo_ref, g2_ref, x2_ref, u2_ref):
        wa, wc = _wo_halves(wo_ref)
        x2 = x_ref[...] + _dot(att_ref[...], wa) + _dot(h3_ref[...], wc)
        x2_ref[...] = x2
        r = lax.rsqrt(jnp.mean(x2 * x2, axis=-1, keepdims=True) + EPS)
        u2_ref[...] = (x2 * r * g2_ref[...]).astype(BF16)

    row = lambda w: pl.BlockSpec((TM, w), lambda i: (i, 0))
    full = lambda a: pl.BlockSpec(a.shape, lambda i: (0,) * a.ndim)
    return pl.pallas_call(
        body, name="mixer_out_fwd", grid=(S // TM,),
        in_specs=[row(D_MODEL), row(ATT_W), row(CONV_CH), _wo_spec(wo4), full(g2)],
        out_specs=[row(D_MODEL), row(D_MODEL)],
        out_shape=[jax.ShapeDtypeStruct((S, D_MODEL), F32), jax.ShapeDtypeStruct((S, D_MODEL), BF16)],
        compiler_params=_params(1),
    )(x, att, h3, wo4, g2)


def _mlp_w_specs():
    return [pl.BlockSpec((None, D_MODEL, D_FF // N_CHIPS), lambda i, f: (f, 0, 0)),
            pl.BlockSpec((None, D_FF // N_CHIPS, D_MODEL), lambda i, f: (f, 0, 0))]


def _mlp_fwd(x2, u2, w1, w2, target=None):
    S = x2.shape[0]
    head = target is not None
    TM = min(512 if head else 1024, S)
    TF = 1024
    nf = D_FF // TF

    def body(*refs):
        x2_ref, u2_ref, w1_ref, w2_ref = refs[:4]
        x3_ref, z_ref, hh_ref = refs[4 + head:7 + head]
        i = pl.program_id(0)
        f = pl.program_id(1)

        @pl.when(f == 0)
        def _():
            x3_ref[...] = x2_ref[...]

        z = _dot(u2_ref[...], w1_ref[...])
        z_ref[...] = z
        zr = jnp.maximum(z, 0.0)
        hh = (zr * zr).astype(BF16)
        hh_ref[...] = hh
        x3_ref[...] += _dot(hh, w2_ref[...])

        if head:
            t_ref, loss_ref = refs[4], refs[8]

            @pl.when((i == 0) & (f == 0))
            def _():
                loss_ref[...] = jnp.zeros_like(loss_ref)

            @pl.when(f == nf - 1)
            def _():
                d = x3_ref[...] - t_ref[...]
                x3_ref[...] = d * (1.0 / D_MODEL)
                loss_ref[...] += jnp.sum(d * d)

    rows = pl.BlockSpec((TM, D_MODEL), lambda i, f: (i, 0))
    tile = pl.BlockSpec((TM, TF), lambda i, f: (i, f))
    return pl.pallas_call(
        body, name="mlp_fwd_loss" if head else "mlp_fwd", grid=(S // TM, nf),
        in_specs=[rows, rows] + _mlp_w_specs() + [rows] * head,
        out_specs=[rows, tile, tile] + [pl.BlockSpec((8, 128), lambda i, f: (0, 0))] * head,
        out_shape=[jax.ShapeDtypeStruct((S, D_MODEL), F32), jax.ShapeDtypeStruct((S, D_FF), F32),
                   jax.ShapeDtypeStruct((S, D_FF), BF16)] + [jax.ShapeDtypeStruct((8, 128), F32)] * head,
        compiler_params=_params(2),
    )(x2, u2, w1, w2, *([target] if head else []))


def _mlp_bwd(dx3, z, x2, g2, w1, w2):
    S = dx3.shape[0]
    TM = min(512, S)
    TF = 1024
    nf = D_FF // TF

    def body(dx3_ref, z_ref, x2_ref, g2_ref, w1_ref, w2_ref, dz_ref, dx2_ref, dg2_ref, du2_ref):
        i = pl.program_id(0)
        f = pl.program_id(1)

        @pl.when((i == 0) & (f == 0))
        def _():
            dg2_ref[...] = jnp.zeros_like(dg2_ref)

        @pl.when(f == 0)
        def _():
            du2_ref[...] = jnp.zeros_like(du2_ref)

        dhh = _dot_nt(dx3_ref[...].astype(BF16), w2_ref[...])
        dz = (dhh * (2.0 * jnp.maximum(z_ref[...], 0.0))).astype(BF16)
        dz_ref[...] = dz
        du2_ref[...] += _dot_nt(dz, w1_ref[...])

        @pl.when(f == nf - 1)
        def _():
            x2 = x2_ref[...]
            r = lax.rsqrt(jnp.mean(x2 * x2, axis=-1, keepdims=True) + EPS)
            n = x2 * r
            du2 = du2_ref[...]
            t = du2 * g2_ref[...]
            dx2_ref[...] = dx3_ref[...] + r * (t - n * jnp.mean(t * n, axis=-1, keepdims=True))
            dg2_ref[0:1, :] += jnp.sum(du2 * n, axis=0, keepdims=True)

    rowi = pl.BlockSpec((TM, D_MODEL), lambda i, f: (i, 0))
    return pl.pallas_call(
        body, name="mlp_bwd", grid=(S // TM, nf),
        in_specs=[rowi, pl.BlockSpec((TM, TF), lambda i, f: (i, f)), rowi,
                  pl.BlockSpec((1, D_MODEL), lambda i, f: (0, 0))] + _mlp_w_specs(),
        out_specs=[pl.BlockSpec((TM, TF), lambda i, f: (i, f)), rowi, pl.BlockSpec((8, D_MODEL), lambda i, f: (0, 0))],
        out_shape=[jax.ShapeDtypeStruct((S, D_FF), BF16), jax.ShapeDtypeStruct((S, D_MODEL), F32),
                   jax.ShapeDtypeStruct((8, D_MODEL), F32)],
        scratch_shapes=[pltpu.VMEM((TM, D_MODEL), F32)],
        compiler_params=_params(2),
    )(dx3, z, x2, g2, w1, w2)


def _matmul_tn(a, b, col_shards=1):
    S, I = a.shape
    J = b.shape[1]
    TI = min(I, 1024)
    TJ = 1024 if J % 1024 == 0 else 896
    TS = min(S, 1024)
    nk = S // TS
    per = J // col_shards // TJ

    def body(a_ref, b_ref, o_ref, acc_ref):
        k = pl.program_id(2)

        @pl.when(k == 0)
        def _():
            acc_ref[...] = jnp.zeros_like(acc_ref)

        acc_ref[...] += _dot_tn(a_ref[...].astype(BF16), b_ref[...].astype(BF16))

        @pl.when(k == nk - 1)
        def _():
            o_ref[...] = acc_ref[...].astype(BF16)

    return pl.pallas_call(
        body, name="matmul_tn", grid=(I // TI, J // TJ, nk),
        in_specs=[pl.BlockSpec((TS, TI), lambda i, j, k: (k, i)), pl.BlockSpec((TS, TJ), lambda i, j, k: (k, j))],
        out_specs=pl.BlockSpec((None, TI, TJ), lambda i, j, k: (j // per, i, j % per)),
        out_shape=jax.ShapeDtypeStruct((col_shards, I, J // col_shards), BF16),
        scratch_shapes=[pltpu.VMEM((TI, TJ), F32)],
        compiler_params=_params(3),
    )(a, b)


def _mixer_out_bwd(dx2, wo4, att, h1, lng, lnb):
    S = dx2.shape[0]
    TM = min(512, S)

    def body(dx2_ref, wo_ref, att_ref, h1_ref, lng_ref, lnb_ref, hr_ref, datt_ref, dlt_ref, dh1_ref, sm_ref):
        @pl.when(pl.program_id(0) == 0)
        def _():
            sm_ref[...] = jnp.zeros_like(sm_ref)

        dxb = dx2_ref[...].astype(BF16)
        wa, wc = _wo_halves(wo_ref)
        datt = _dot_nt(dxb, wa)
        datt_ref[...] = datt.astype(BF16)
        prod = datt * att_ref[...].astype(F32)
        dlt_ref[...] = sum(_dot_nt(hr_ref[...], piece) for piece in _split3(prod))[0:ATT_HEADS, :]
        dh3 = _dot_nt(dxb, wc)
        h1 = h1_ref[...]
        mu = jnp.mean(h1, axis=-1, keepdims=True)
        d = h1 - mu
        rstd = lax.rsqrt(jnp.mean(d * d, axis=-1, keepdims=True) + EPS)
        n = d * rstd
        h2 = n * lng_ref[...] + lnb_ref[...]
        sg = _sigmoid(h2)
        dh2 = dh3 * (sg * (1.0 + h2 * (1.0 - sg)))
        dn = dh2 * lng_ref[...]
        dh1 = rstd * (dn - jnp.mean(dn, axis=-1, keepdims=True) - n * jnp.mean(dn * n, axis=-1, keepdims=True))
        dh1_ref[...] = dh1
        sm_ref[0:1, :] += jnp.sum(dh2 * n, axis=0, keepdims=True)
        sm_ref[1:2, :] += jnp.sum(dh2, axis=0, keepdims=True)
        sm_ref[2:3, :] += jnp.sum(dh1, axis=0, keepdims=True)

    row = lambda w: pl.BlockSpec((TM, w), lambda i: (i, 0))
    full = lambda a: pl.BlockSpec(a.shape, lambda i: (0,) * a.ndim)
    hr = _head_rows()
    return pl.pallas_call(
        body, name="mixer_out_bwd", grid=(S // TM,),
        in_specs=[row(D_MODEL), _wo_spec(wo4), row(ATT_W), row(CONV_CH), full(lng), full(lnb), full(hr)],
        out_specs=[row(ATT_W), pl.BlockSpec((ATT_HEADS, TM), lambda i: (0, i)), row(CONV_CH),
                   pl.BlockSpec((8, CONV_CH), lambda i: (0, 0))],
        out_shape=[jax.ShapeDtypeStruct((S, ATT_W), BF16), jax.ShapeDtypeStruct((ATT_HEADS, S), F32),
                   jax.ShapeDtypeStruct((S, CONV_CH), F32), jax.ShapeDtypeStruct((8, CONV_CH), F32)],
        compiler_params=_params(1),
    )(dx2, wo4, att, h1, lng, lnb, hr)


def _conv_glu_bwd(dh1, h0, proj, cw):
    S = dh1.shape[0]
    TM = min(512, S)
    nb = S // TM
    lead = CONV_HALO - CONV_TAPS + 1

    def body(dh1_ref, dnx_ref, h0_ref, hpv_ref, a_ref, g_ref, cw_ref, dag_ref, dcw_ref,
             dbuf_ref, hbuf_ref, ds_ref, hs_ref, dh0_ref, dcw8_ref):
        i = pl.program_id(0)

        @pl.when(i == 0)
        def _():
            dcw8_ref[...] = jnp.zeros_like(dcw8_ref)

        dbuf_ref[0:TM, :] = dh1_ref[...]
        dbuf_ref[TM:TM + CONV_HALO, :] = jnp.where(i < nb - 1, dnx_ref[0:CONV_HALO, :], 0.0)
        hbuf_ref[0:CONV_HALO, :] = jnp.where(i > 0, hpv_ref[TM - CONV_HALO:TM, :], 0.0)
        hbuf_ref[CONV_HALO:CONV_HALO + TM, :] = h0_ref[...]
        _fill_row_shifts(dbuf_ref, ds_ref, TM)
        _fill_row_shifts(hbuf_ref, hs_ref, TM)

        def conv_rows(step, _):
            r0 = pl.multiple_of(step * CONV_ROWS, CONV_ROWS)
            dh1 = dbuf_ref[pl.ds(r0, CONV_ROWS), :]
            part = jnp.zeros((CONV_ROWS, CONV_CH), F32)
            for j in range(CONV_TAPS):
                part = part + cw_ref[j:j + 1, :] * _row_shifted(dbuf_ref, ds_ref, CONV_TAPS - 1 - j, CONV_ROWS, r0)
                prod = dh1 * _row_shifted(hbuf_ref, hs_ref, lead + j, CONV_ROWS, r0)
                dcw8_ref[j] += jnp.sum(prod.reshape(CONV_ROWS // SUBLANES, SUBLANES, CONV_CH), axis=0)
            dh0_ref[pl.ds(r0, CONV_ROWS), :] = part
            return 0

        lax.fori_loop(0, TM // CONV_ROWS, conv_rows, 0)

        @pl.when(i == nb - 1)
        def _():
            dcw_ref[...] = jnp.sum(dcw8_ref[...], axis=1)

        dh0 = dh0_ref[...]
        sg = _sigmoid(g_ref[...])
        dag_ref[:, 0:CONV_CH] = (dh0 * sg).astype(BF16)
        dag_ref[:, CONV_CH:2 * CONV_CH] = (dh0 * a_ref[...] * sg * (1.0 - sg)).astype(BF16)

    blk = lambda fn: pl.BlockSpec((TM, CONV_CH), fn)
    return pl.pallas_call(
        body, name="conv_glu_bwd", grid=(nb,),
        in_specs=[blk(lambda i: (i, 0)), blk(lambda i: (jnp.minimum(i + 1, nb - 1), 0)),
                  blk(lambda i: (i, 0)), blk(lambda i: (jnp.maximum(i - 1, 0), 0)),
                  blk(lambda i: (i, O_A // CONV_CH)), blk(lambda i: (i, O_G // CONV_CH)),
                  pl.BlockSpec(cw.shape, lambda i: (0, 0))],
        out_specs=[pl.BlockSpec((TM, 2 * CONV_CH), lambda i: (i, 0)), pl.BlockSpec((CONV_HALO, CONV_CH), lambda i: (0, 0))],
        out_shape=[jax.ShapeDtypeStruct((S, 2 * CONV_CH), BF16), jax.ShapeDtypeStruct((CONV_HALO, CONV_CH), F32)],
        scratch_shapes=[pltpu.VMEM((TM + CONV_HALO, CONV_CH), F32), pltpu.VMEM((TM + CONV_HALO, CONV_CH), F32),
                        pltpu.VMEM((SUBLANES - 1, TM + CONV_HALO, CONV_CH), F32),
                        pltpu.VMEM((SUBLANES - 1, TM + CONV_HALO, CONV_CH), F32),
                        pltpu.VMEM((TM, CONV_CH), F32), pltpu.VMEM((CONV_HALO, SUBLANES, CONV_CH), F32)],
        compiler_params=_params(1),
    )(dh1, dh1, h0, h0, proj, proj, cw)


def _mixer_in_bwd(x, dx2, proj, dq, dk, dv, dag, dct, drb, g1, win, qg, kg, bf):
    S = x.shape[0]
    TM = min(512, S)
    nb = S // TM

    def body(x_ref, dx2_ref, qr_ref, kr_ref, fz_ref, dq_ref, dk_ref, dv_ref, dag_ref, dct_ref, drb_ref,
             g1_ref, win_ref, qg_ref, kg_ref, bf_ref, bd_ref, fold_ref, triu_ref, pick_ref,
             dproj_ref, dx_ref, dg1_ref, sm_ref, carry_ref, gsum_ref):
        i = pl.program_id(0)

        @pl.when(i == 0)
        def _():
            carry_ref[...] = jnp.zeros_like(carry_ref)
            gsum_ref[...] = jnp.zeros_like(gsum_ref)
            dg1_ref[...] = jnp.zeros_like(dg1_ref)
            sm_ref[...] = jnp.zeros_like(sm_ref)

        def headnorm_bwd(raw, dy, gain, scale, row):
            rs = lax.rsqrt(_dot_hi_r(raw * raw, bd_ref[...]) * (1.0 / HEAD_DIM) + EPS)
            n = raw * rs
            gsum_ref[row:row + 1, :] += jnp.sum(dy * n, axis=0, keepdims=True) * scale
            dn = dy * (gain * scale)
            return rs * (dn - n * (_dot_hi_r(dn * n, bd_ref[...]) * (1.0 / HEAD_DIM)))

        dproj_ref[:, O_Q:O_Q + ATT_W] = headnorm_bwd(qr_ref[...], dq_ref[...], qg_ref[...], QK_SCALE, 0).astype(BF16)
        dproj_ref[:, O_K:O_K + ATT_W] = headnorm_bwd(kr_ref[...], dk_ref[...], kg_ref[...], 1.0, 1).astype(BF16)
        dproj_ref[:, O_V:O_V + ATT_W] = dv_ref[...].astype(BF16)
        dproj_ref[:, O_A:O_A + 2 * CONV_CH] = dag_ref[...]

        dc8 = jnp.concatenate([dct_ref[...], jnp.zeros((128 - ATT_HEADS, TM), F32)], axis=0).T
        dc8 = dc8 + _dot_hi_r(drb_ref[...], pick_ref[...])
        dlogf = _dot_hi_l(triu_ref[...], dc8) + carry_ref[...]
        carry_ref[...] = dlogf[0:1, :]
        df = dlogf * _sigmoid(-(fz_ref[...] + bf_ref[...]))
        dproj_ref[:, O_F:O_F + 128] = df.astype(BF16)
        sm_ref[2:3, :] += jnp.sum(df, axis=0, keepdims=True)

        du1 = _dot_nt(dproj_ref[...], win_ref[...])
        xv = x_ref[...]
        r = lax.rsqrt(jnp.mean(xv * xv, axis=-1, keepdims=True) + EPS)
        n1 = xv * r
        t = du1 * g1_ref[...]
        dx_ref[...] = dx2_ref[...] + r * (t - n1 * jnp.mean(t * n1, axis=-1, keepdims=True))
        dg1_ref[0:1, :] += jnp.sum(du1 * n1, axis=0, keepdims=True)

        @pl.when(i == nb - 1)
        def _():
            sm_ref[0:2, :] = _dot_hi_r(gsum_ref[0:8, :], fold_ref[...])[0:2, :]

    rev = lambda w, cb=0: pl.BlockSpec((TM, w), lambda i: (nb - 1 - i, cb))
    full = lambda a: pl.BlockSpec(a.shape, lambda i: (0,) * a.ndim)
    bd, fold, triu = _head_blockdiag(), _head_fold(), _tril(TM).T
    consts = (g1, win, qg, kg, bf, bd, fold, triu, _head_pick())
    return pl.pallas_call(
        body, name="mixer_in_bwd", grid=(nb,),
        in_specs=[rev(D_MODEL), rev(D_MODEL), rev(ATT_W, O_Q // ATT_W), rev(ATT_W, O_K // ATT_W), rev(128, O_F // 128),
                  rev(ATT_W), rev(ATT_W), rev(ATT_W), rev(2 * CONV_CH),
                  pl.BlockSpec((ATT_HEADS, TM), lambda i: (0, nb - 1 - i)), rev(ATT_W)] + [full(a) for a in consts],
        out_specs=[rev(N_INP), rev(D_MODEL), pl.BlockSpec((8, D_MODEL), lambda i: (0, 0)),
                   pl.BlockSpec((8, 128), lambda i: (0, 0))],
        out_shape=[jax.ShapeDtypeStruct((S, N_INP), BF16), jax.ShapeDtypeStruct((S, D_MODEL), F32),
                   jax.ShapeDtypeStruct((8, D_MODEL), F32), jax.ShapeDtypeStruct((8, 128), F32)],
        scratch_shapes=[pltpu.VMEM((1, 128), F32), pltpu.VMEM((8, ATT_W), F32)],
        compiler_params=_params(1),
    )(x, dx2, proj, proj, proj, dq, dk, dv, dag, dct, drb, *consts)


def _layer_fwd(x, early, late, target=None):
    p = early(x)
    u1, proj, q, k, v, qa, ka, h0, h1, h3 = _mixer_in_fwd(
        x, p["g1"], p["win"], p["qg"], p["kg"], p["bf"], p["cw"], p["cvb"], p["lng"], p["lnb"])
    att, lse = _attn_fwd(q, qa, k, ka, v)
    p = dict(p, **late(att))
    x2, u2 = _mixer_out_fwd(x, att, h3, p["wo"], p["g2"])
    x3, z, hh, *loss_acc = _mlp_fwd(x2, u2, p["w1"], p["w2"], target)
    saved = dict(x=x, u1=u1, proj=proj, q=q, k=k, v=v, qa=qa, ka=ka, h0=h0, h1=h1, h3=h3, att=att, lse=lse,
                 x2=x2, u2=u2, z=z, hh=hh)
    return (x3 if target is None else (x3, loss_acc[0])), saved, p


def _tie(a, token):
    return a if token is None else a + token[0:1, 0:1]


def _layer_bwd(dx3, s, p, reduce):
    dz, dx2, dg2 = _mlp_bwd(dx3, s["z"], s["x2"], p["g2"], p["w1"], p["w2"])
    g_w2 = _matmul_tn(s["hh"], dx3)
    g_w1 = _matmul_tn(s["u2"], dz, col_shards=N_CHIPS)
    token = reduce("a", {2: g_w1, 3: g_w2.reshape(N_CHIPS, D_FF // N_CHIPS, D_MODEL)})
    datt, dlt, dh1, sm_c = _mixer_out_bwd(dx2, p["wo"], s["att"], s["h1"], _tie(p["lng"], token), p["lnb"])
    g_wo = jnp.concatenate([_matmul_tn(s["att"], dx2)[0], _matmul_tn(s["h3"], dx2)[0]], axis=0)
    dag, dcw = _conv_glu_bwd(dh1, s["h0"], s["proj"], p["cw"])
    dq, dk, dv, dc4, drb = _attn_bwd(s["q"], s["qa"], s["k"], s["ka"], s["v"], datt, s["lse"],
                                     dlt.reshape(ATT_HEADS // 2, 2, dlt.shape[1]))
    dct = dc4.reshape(ATT_HEADS, dc4.shape[2])
    dproj, dx, dg1, sm_a = _mixer_in_bwd(s["x"], dx2, s["proj"], dq, dk, dv, dag, dct, drb,
                                         p["g1"], p["win"], p["qg"], p["kg"], p["bf"])
    g_win = _win_to_global(_matmul_tn(s["u1"], dproj)[0])
    g_win = g_win.reshape(D_MODEL, N_CHIPS, N_IN // N_CHIPS).transpose(1, 0, 2)
    token = reduce("b", {0: g_win, 1: g_wo.reshape(N_CHIPS, D_MODEL // N_CHIPS, D_MODEL)})
    small = dict(g1=dg1[0], g2=dg2[0], lng=sm_c[0], lnb=sm_c[1], cvb=sm_c[2], cw=dcw[0:CONV_TAPS],
                 qg=sm_a[0, 0:HEAD_DIM], kg=sm_a[1, 0:HEAD_DIM], bf=sm_a[2, 0:ATT_HEADS])
    return dx, small, token


def _local_step(x, target, weights, reduce):
    saved, layers = [], []
    h = x
    for l, (early, late) in enumerate(weights):
        h, s, p = _layer_fwd(h, early, late, target if l == len(weights) - 1 else None)
        saved.append(s)
        layers.append(p)
    dy, loss_acc = h
    loss = loss_acc[0, 0] * (0.5 / D_MODEL)
    smalls = []
    d, token = dy, None
    for l in reversed(range(len(layers))):
        d, small, token = _layer_bwd(d, saved[l], dict(layers[l], g2=_tie(layers[l]["g2"], token)), reduce(l))
        smalls.append(small)
    return loss, d, smalls[::-1]


def _win_to_internal(w):
    pad = jnp.zeros(w.shape[:-1] + (N_INP - N_IN,), w.dtype)
    return jnp.concatenate([w[..., :1536], w[..., 1544:], w[..., 1536:1544], pad], axis=-1)


def _win_to_global(g):
    return jnp.concatenate([g[..., :1536], g[..., O_F:O_F + ATT_HEADS], g[..., 1536:O_F]], axis=-1)


def _layer_params(l, win, cw, norm1_g, b_f, q_norm_g, k_norm_g, conv_b, conv_ln_g, conv_ln_b, norm2_g):
    row = lambda a: a.reshape(1, -1)
    return dict(
        win=win, cw=jnp.pad(cw, ((0, CONV_HALO - CONV_TAPS), (0, 0))),
        g1=row(norm1_g[l]), g2=row(norm2_g[l]),
        qg=row(jnp.tile(q_norm_g[l], ATT_HEADS)), kg=row(jnp.tile(k_norm_g[l], ATT_HEADS)),
        bf=row(jnp.pad(b_f[l], (0, 128 - ATT_HEADS))),
        cvb=row(conv_b[l]), lng=row(conv_ln_g[l]), lnb=row(conv_ln_b[l]))


def _place():
    x, y, c = lax.axis_index("x"), lax.axis_index("y"), lax.axis_index("c")
    chips = [(1 - x, y), (x, 1 - y), (1 - x, 1 - y)]
    return x, y, c, chips


HBM = pl.BlockSpec(memory_space=pltpu.HBM)
SEM = pl.BlockSpec(memory_space=pltpu.SEMAPHORE)
GATHER_PEERS = N_CHIPS


def _gather_peers():
    x, y, c, chips = _place()
    return [(*chip, c) for chip in chips] + [(x, y, 1 - c)], [2 * px + py for px, py in chips] + [2 * x + y]


def _gather_start(srcs):
    n = len(srcs)

    def body(*refs):
        ins, lands = refs[:n], refs[n:2 * n]
        send_sems, recv_sems, token = refs[2 * n], refs[2 * n + 1], refs[-1]
        me = 2 * lax.axis_index("x") + lax.axis_index("y")
        peers, _ = _gather_peers()
        for g in range(n):
            for j, to in enumerate(peers):
                pltpu.make_async_remote_copy(src_ref=ins[g], dst_ref=lands[g].at[me],
                                             send_sem=send_sems.at[GATHER_PEERS * g + j],
                                             recv_sem=recv_sems.at[GATHER_PEERS * g + j],
                                             device_id=to, device_id_type=MESH).start()
        token[...] = jnp.zeros_like(token)

    lands = [lax.empty((N_CHIPS,) + a.shape, a.dtype) for a in srcs]
    outs = pl.pallas_call(
        body, name="gather_start",
        in_specs=[HBM] * (2 * n),
        out_specs=[SEM, SEM] + [HBM] * (2 * n) + [pl.BlockSpec(memory_space=pltpu.VMEM)],
        out_shape=[pltpu.SemaphoreType.DMA((GATHER_PEERS * n,)), pltpu.SemaphoreType.DMA((GATHER_PEERS * n,))]
        + [pltpu.HBM(a.shape, a.dtype) for a in srcs] + [pltpu.HBM(a.shape, a.dtype) for a in lands]
        + [jax.ShapeDtypeStruct((8, 128), F32)],
        input_output_aliases={i: 2 + i for i in range(2 * n)},
        compiler_params=pltpu.CompilerParams(has_side_effects=pltpu.SideEffectType.DATAFLOW_SIDE_EFFECTING),
    )(*[pltpu.with_memory_space_constraint(a, pltpu.HBM) for a in srcs],
      *[pltpu.with_memory_space_constraint(a, pltpu.HBM) for a in lands])
    return outs[0], outs[1], list(outs[2:2 + n]), list(outs[2 + n:2 + 2 * n]), outs[-1]


def _gather_wait(name, groups, send_sems, recv_sems, srcs, lands, after):
    k = len(groups)

    def body(*refs):
        ins, lnd = refs[:k], refs[k:2 * k]
        ssem, rsem = refs[2 * k], refs[2 * k + 1]
        peers, slots = _gather_peers()
        for i, g in enumerate(groups):
            for j, to in enumerate(peers):
                cp = pltpu.make_async_remote_copy(src_ref=ins[i], dst_ref=lnd[i].at[slots[j]],
                                                  send_sem=ssem.at[GATHER_PEERS * g + j],
                                                  recv_sem=rsem.at[GATHER_PEERS * g + j],
                                                  device_id=to, device_id_type=MESH)
                cp.wait_send()
                cp.wait_recv()

    outs = pl.pallas_call(
        body, name=name,
        in_specs=[HBM] * (2 * k) + [SEM, SEM, ANY],
        out_specs=[HBM] * (2 * k),
        out_shape=[pltpu.HBM(a.shape, a.dtype) for a in srcs] + [pltpu.HBM(a.shape, a.dtype) for a in lands],
        input_output_aliases={i: i for i in range(2 * k)},
        compiler_params=pltpu.CompilerParams(has_side_effects=pltpu.SideEffectType.DATAFLOW_SIDE_EFFECTING),
    )(*srcs, *lands, send_sems, recv_sems, after)
    return list(outs[k:])


REDUCE_STEPS = 8


def _row_half(a):
    return a.shape[-2] // 2


def _half_swap(name, gs):
    n = len(gs)

    def body(*refs):
        ins, outs = refs[:n], refs[n:2 * n]
        send_sems, recv_sems = refs[2 * n:]
        x, y, c, _ = _place()
        copies = []
        for a in range(n):
            h = _row_half(gs[a])
            copies.append(pltpu.make_async_remote_copy(
                src_ref=ins[a].at[:, pl.ds((1 - c) * h, h), :], dst_ref=outs[a], send_sem=send_sems.at[a],
                recv_sem=recv_sems.at[a], device_id=(x, y, 1 - c), device_id_type=MESH))
        for cp in copies:
            cp.start()
        for cp in copies:
            cp.wait()

    return pl.pallas_call(
        body, name=name,
        in_specs=[ANY] * n, out_specs=[ANY] * n,
        out_shape=[jax.ShapeDtypeStruct((N_CHIPS, _row_half(a), a.shape[-1]), a.dtype) for a in gs],
        scratch_shapes=[pltpu.SemaphoreType.DMA((n,)), pltpu.SemaphoreType.DMA((n,))],
        compiler_params=pltpu.CompilerParams(has_side_effects=True),
    )(*gs)


def _half_specs(gs, row_block):
    tiles = [_row_half(a) // REDUCE_STEPS for a in gs]
    return [pl.BlockSpec((N_CHIPS, t, a.shape[-1]), lambda i, p: (0, row_block(i, p), 0)) for a, t in zip(gs, tiles)]


def _half_add(name, gs, got, place):
    n = len(gs)

    def body(place_ref, *refs):
        own, theirs, outs = refs[:n], refs[n:2 * n], refs[2 * n:]
        for a in range(n):
            outs[a][...] = (own[a][...].astype(F32) + theirs[a][...].astype(F32)).astype(BF16)

    plain = _half_specs(gs, lambda i, p: i)
    return pl.pallas_call(
        body, name=name,
        grid_spec=pltpu.PrefetchScalarGridSpec(
            num_scalar_prefetch=1, grid=(REDUCE_STEPS,),
            in_specs=_half_specs(gs, lambda i, p: p[1] * REDUCE_STEPS + i) + plain, out_specs=plain),
        out_shape=[jax.ShapeDtypeStruct(a.shape, BF16) for a in got],
        compiler_params=_params(1),
    )(place, *gs, *got)


def _exchange_copies(parts, lands, send_sems, recv_sems):
    x, y, c, chips = _place()
    return [pltpu.make_async_remote_copy(src_ref=parts[a].at[2 * px + py], dst_ref=lands[a].at[j],
                                         send_sem=send_sems.at[3 * a + j], recv_sem=recv_sems.at[3 * a + j],
                                         device_id=(px, py, c), device_id_type=MESH)
            for a in range(len(parts)) for j, (px, py) in enumerate(chips)]


def _exchange_start(name, parts):
    n = len(parts)

    def body(*refs):
        _ = [cp.start() for cp in _exchange_copies(refs[:n], refs[n:2 * n], refs[2 * n], refs[2 * n + 1])]
        refs[-1][...] = jnp.zeros_like(refs[-1])

    lands = [lax.empty((N_CHIPS - 1,) + a.shape[1:], a.dtype) for a in parts]
    outs = pl.pallas_call(
        body, name=name,
        in_specs=[HBM] * (2 * n),
        out_specs=[SEM, SEM] + [HBM] * (2 * n) + [pl.BlockSpec(memory_space=pltpu.VMEM)],
        out_shape=[pltpu.SemaphoreType.DMA((3 * n,)), pltpu.SemaphoreType.DMA((3 * n,))]
        + [pltpu.HBM(a.shape, a.dtype) for a in parts] + [pltpu.HBM(a.shape, a.dtype) for a in lands]
        + [jax.ShapeDtypeStruct((8, 128), F32)],
        input_output_aliases={i: 2 + i for i in range(2 * n)},
        compiler_params=pltpu.CompilerParams(has_side_effects=pltpu.SideEffectType.DATAFLOW_SIDE_EFFECTING),
    )(*[pltpu.with_memory_space_constraint(a, pltpu.HBM) for a in parts],
      *[pltpu.with_memory_space_constraint(a, pltpu.HBM) for a in lands])
    return outs[0], outs[1], list(outs[2:2 + n]), list(outs[2 + n:2 + 2 * n]), outs[-1]


def _exchange_wait(name, send_sems, recv_sems, parts, lands, after):
    n = len(parts)

    def body(*refs):
        for cp in _exchange_copies(refs[:n], refs[n:2 * n], refs[2 * n], refs[2 * n + 1]):
            cp.wait_send()
            cp.wait_recv()

    outs = pl.pallas_call(
        body, name=name,
        in_specs=[HBM] * (2 * n) + [SEM, SEM, ANY],
        out_specs=[HBM] * (2 * n),
        out_shape=[pltpu.HBM(a.shape, a.dtype) for a in parts] + [pltpu.HBM(a.shape, a.dtype) for a in lands],
        input_output_aliases={i: i for i in range(2 * n)},
        compiler_params=pltpu.CompilerParams(has_side_effects=pltpu.SideEffectType.DATAFLOW_SIDE_EFFECTING),
    )(*parts, *lands, send_sems, recv_sems, after)
    return list(outs[:n]), list(outs[n:])


def _chip_sum(name, parts, lands, sums, place, layer):
    n = len(parts)
    tiles = [a.shape[-2] // REDUCE_STEPS for a in parts]

    def body(place_ref, *refs):
        own, got, outs = refs[:n], refs[n:2 * n], refs[3 * n:]
        for a in range(n):
            tot = own[a][...].astype(F32)
            for j in range(N_CHIPS - 1):
                tot = tot + got[a][j].astype(F32)
            outs[a][...] = tot

    own_specs = [pl.BlockSpec((None, t, a.shape[-1]), lambda i, p: (p[0], i, 0)) for a, t in zip(parts, tiles)]
    got_specs = [pl.BlockSpec((N_CHIPS - 1, t, a.shape[-1]), lambda i, p: (0, i, 0)) for a, t in zip(parts, tiles)]
    out_specs = [pl.BlockSpec((None, t, a.shape[-1]), lambda i, p: (layer, p[1] * REDUCE_STEPS + i, 0))
                 for a, t in zip(parts, tiles)]
    return pl.pallas_call(
        body, name=name,
        grid_spec=pltpu.PrefetchScalarGridSpec(num_scalar_prefetch=1, grid=(REDUCE_STEPS,),
                                               in_specs=own_specs + got_specs + [ANY] * n, out_specs=out_specs),
        out_shape=[jax.ShapeDtypeStruct(a.shape, F32) for a in sums],
        input_output_aliases={1 + 2 * n + a: a for a in range(n)},
        compiler_params=_params(1),
    )(place, *parts, *lands, *sums)


def _half_fill(sums):
    n = len(sums)

    def body(*refs):
        ins, outs = refs[:n], refs[n:2 * n]
        send_sems, recv_sems = refs[2 * n:]
        x, y, c, _ = _place()
        copies = []
        for a in range(n):
            h = _row_half(sums[a])
            copies.append(pltpu.make_async_remote_copy(
                src_ref=ins[a].at[:, pl.ds(c * h, h), :], dst_ref=outs[a].at[:, pl.ds(c * h, h), :],
                send_sem=send_sems.at[a], recv_sem=recv_sems.at[a], device_id=(x, y, 1 - c), device_id_type=MESH))
        for cp in copies:
            cp.start()
        for a in range(n):
            h = _row_half(sums[a])
            theirs = outs[a].at[:, pl.ds((1 - c) * h, h), :]
            pltpu.make_async_remote_copy(src_ref=theirs, dst_ref=theirs, send_sem=send_sems.at[a], recv_sem=recv_sems.at[a],
                                         device_id=(x, y, 1 - c), device_id_type=MESH).wait_recv()
        for cp in copies:
            cp.wait_send()

    return pl.pallas_call(
        body, name="half_fill",
        in_specs=[ANY] * n, out_specs=[ANY] * n,
        out_shape=[jax.ShapeDtypeStruct(a.shape, a.dtype) for a in sums],
        input_output_aliases={a: a for a in range(n)},
        scratch_shapes=[pltpu.SemaphoreType.DMA((n,)), pltpu.SemaphoreType.DMA((n,))],
        compiler_params=pltpu.CompilerParams(has_side_effects=True),
    )(*sums)


def _adamw_math(w, g, m, v):
    m = ADAM_B1 * m + (1.0 - ADAM_B1) * g
    v = ADAM_B2 * v + (1.0 - ADAM_B2) * (g * g)
    m_hat = m / (1.0 - ADAM_B1 ** ADAM_STEP)
    v_hat = v / (1.0 - ADAM_B2 ** ADAM_STEP)
    delta = -ADAM_LR * (m_hat / (jnp.sqrt(v_hat) + ADAM_EPS) + ADAM_WD * w)
    return delta, m, v


def _adamw(ws, gs, ms, vs):
    n = len(ws)
    steps = 16
    tiles = [a.shape[-2] // steps for a in ws]

    def body(*refs):
        w_r, g_r, m_r, v_r = refs[:n], refs[n:2 * n], refs[2 * n:3 * n], refs[3 * n:4 * n]
        g_o, d_o, m_o, v_o = refs[4 * n:5 * n], refs[5 * n:6 * n], refs[6 * n:7 * n], refs[7 * n:]
        for a in range(n):
            g = g_r[a][...]
            d, m, v = _adamw_math(w_r[a][...], g, m_r[a][...], v_r[a][...])
            g_o[a][...] = g
            d_o[a][...] = d
            m_o[a][...] = m
            v_o[a][...] = v

    specs = [pl.BlockSpec((2, t, a.shape[-1]), lambda i: (0, i, 0)) for a, t in zip(ws, tiles)]
    outs = pl.pallas_call(
        body, name="adamw", grid=(steps,),
        in_specs=specs * 4, out_specs=specs * 4,
        out_shape=[jax.ShapeDtypeStruct(a.shape, F32) for a in ws] * 4,
        compiler_params=_params(1),
    )(*ws, *gs, *ms, *vs)
    return outs[:n], outs[n:2 * n], outs[2 * n:3 * n], outs[3 * n:]


SMALL_W = 512


def _small_allreduce_adamw(g, w, m, v, cw_w, cw_m, cw_v, cw_row0):
    R = g.shape[0]
    n_l = cw_w.shape[0]

    def body(g_ref, w_ref, m_ref, v_ref, cww_ref, cwm_ref, cwv_ref,
             gs_ref, d_ref, mo_ref, vo_ref, cg_ref, cd_ref, cmo_ref, cvo_ref,
             slots_ref, send_sems, recv_sems):
        x, y, c, _ = _place()
        me = 4 * x + 2 * y + c
        slots_ref[me] = g_ref[...]
        sends = []
        for d in range(1, 8):
            px, py, pc = x ^ (d >> 2), y ^ ((d >> 1) & 1), c ^ (d & 1)
            cp = pltpu.make_async_remote_copy(src_ref=g_ref, dst_ref=slots_ref.at[me], send_sem=send_sems.at[d - 1],
                                              recv_sem=recv_sems.at[d - 1], device_id=(px, py, pc), device_id_type=MESH)
            cp.start()
            sends.append(cp)
        for d in range(1, 8):
            px, py, pc = x ^ (d >> 2), y ^ ((d >> 1) & 1), c ^ (d & 1)
            slot = slots_ref.at[4 * px + 2 * py + pc]
            pltpu.make_async_remote_copy(src_ref=slot, dst_ref=slot, send_sem=send_sems.at[d - 1],
                                         recv_sem=recv_sems.at[d - 1], device_id=(px, py, pc),
                                         device_id_type=MESH).wait_recv()
        for cp in sends:
            cp.wait_send()
        tot = slots_ref[0]
        for k in range(1, 8):
            tot = tot + slots_ref[k]
        gs_ref[...] = tot
        dl, mn, vn = _adamw_math(w_ref[...], tot, m_ref[...], v_ref[...])
        d_ref[...] = dl
        mo_ref[...] = mn
        vo_ref[...] = vn
        chip = 2 * x + y
        for l in range(n_l):
            rows = tot[cw_row0[l]:cw_row0[l] + CONV_HALO, :]
            mine = rows[:, 0:128]
            for k in range(1, N_CHIPS):
                mine = jnp.where(chip == k, rows[:, 128 * k:128 * (k + 1)], mine)
            cg_ref[l] = mine
            dl, mn, vn = _adamw_math(cww_ref[l], mine, cwm_ref[l], cwv_ref[l])
            cd_ref[l] = dl
            cmo_ref[l] = mn
            cvo_ref[l] = vn

    vm = pl.BlockSpec(memory_space=pltpu.VMEM)
    small = jax.ShapeDtypeStruct((R, SMALL_W), F32)
    conv = jax.ShapeDtypeStruct(cw_w.shape, F32)
    return pl.pallas_call(
        body, name="small_allreduce_adamw",
        in_specs=[vm] * 7, out_specs=[vm] * 8,
        out_shape=[small] * 4 + [conv] * 4,
        scratch_shapes=[pltpu.VMEM((8, R, SMALL_W), F32), pltpu.SemaphoreType.DMA((7,)), pltpu.SemaphoreType.DMA((7,))],
        compiler_params=pltpu.CompilerParams(has_side_effects=True, vmem_limit_bytes=VMEM_LIMIT),
    )(g, w, m, v, cw_w, cw_m, cw_v)


SMALL_LAYOUT = (("conv_w", CONV_HALO), ("norm1_g", 2), ("norm2_g", 2), ("conv_b", 1), ("conv_ln_g", 1),
                ("conv_ln_b", 1), ("q_norm_g", 1), ("k_norm_g", 1), ("b_f", 1))
SMALL_ROWS = sum(r for _, r in SMALL_LAYOUT)
SMALL_ROWS_PAD = 48
LOSS_ROW = SMALL_ROWS


def _pack_small(per_layer):
    flat = []
    for d in per_layer:
        for name, r in SMALL_LAYOUT:
            n = r * SMALL_W
            a = d.get(name)
            if a is None:
                flat.append(jnp.zeros((n,), F32))
                continue
            flat.append(a.reshape(-1))
            if a.size < n:
                flat.append(jnp.zeros((n - a.size,), F32))
        spare = (SMALL_ROWS_PAD - SMALL_ROWS) * SMALL_W
        if "spare" in d:
            flat.append(d["spare"].reshape(-1))
            spare -= d["spare"].size
        flat.append(jnp.zeros((spare,), F32))
    return jnp.concatenate(flat).reshape(-1, SMALL_W)


def _unpack_small(packed, name, size):
    row0 = 0
    for nm, r in SMALL_LAYOUT:
        if nm == name:
            break
        row0 += r
    per_layer = packed.reshape(-1, SMALL_ROWS_PAD * SMALL_W)
    return per_layer[:, row0 * SMALL_W:row0 * SMALL_W + size]


SMALL_SIZES = dict(norm1_g=D_MODEL, norm2_g=D_MODEL, conv_b=CONV_CH, conv_ln_g=CONV_CH, conv_ln_b=CONV_CH,
                   q_norm_g=HEAD_DIM, k_norm_g=HEAD_DIM, b_f=ATT_HEADS)
SMALL_KEYS = dict(norm1_g="g1", norm2_g="g2", conv_b="cvb", conv_ln_g="lng", conv_ln_b="lnb",
                  q_norm_g="qg", k_norm_g="kg", b_f="bf", conv_w="cw")
CONV_W_ROW0 = 0


def kernel(x, norm1_g, w_in, b_f, q_norm_g, k_norm_g, conv_w, conv_b, conv_ln_g, conv_ln_b, w_o, norm2_g, w_mlp_in, w_mlp_out, loss_target, m_norm1_g, m_w_in, m_b_f, m_q_norm_g, m_k_norm_g, m_conv_w, m_conv_b, m_conv_ln_g, m_conv_ln_b, m_w_o, m_norm2_g, m_w_mlp_in, m_w_mlp_out, v_norm1_g, v_w_in, v_b_f, v_q_norm_g, v_k_norm_g, v_conv_w, v_conv_b, v_conv_ln_g, v_conv_ln_b, v_w_o, v_norm2_g, v_w_mlp_in, v_w_mlp_out):
    n_l = w_in.shape[0]

    per_layer = lambda l: [w_in[l].astype(BF16), conv_w[l], w_o[l].astype(BF16), w_mlp_in[l].astype(BF16),
                           w_mlp_out[l].astype(BF16)]
    n_w = len(per_layer(0))
    send_sems, recv_sems, srcs, lands, token = _gather_start([a for l in range(n_l) for a in per_layer(l)])

    def layer_weights(l):
        def wait(tag, which, after):
            groups = [n_w * l + i for i in which]
            return _gather_wait(f"gather_wait_{tag}{l}", groups, send_sems, recv_sems,
                                [srcs[g] for g in groups], [lands[g] for g in groups], after)

        def early(after):
            g_in, g_cw = wait("a", (0, 1), token if l == 0 else after)
            win = _win_to_internal(jnp.concatenate([g_in[k] for k in range(N_CHIPS)], axis=-1))
            cw = jnp.concatenate([g_cw[k] for k in range(N_CHIPS)], axis=-1)
            return _layer_params(l, win, cw, norm1_g, b_f, q_norm_g, k_norm_g, conv_b, conv_ln_g, conv_ln_b, norm2_g)

        def late(after):
            wo, w1, w2 = wait("b", (2, 3, 4), after)
            return dict(wo=wo, w1=w1, w2=w2)

        return early, late

    place = jnp.stack([2 * lax.axis_index("x") + lax.axis_index("y"), lax.axis_index("c")]).astype(jnp.int32)
    big_w = [w_in, w_o, w_mlp_in, w_mlp_out]
    pending = []

    def reduce(l):
        def group(tag, grads):
            which, gs = list(grads), list(grads.values())
            got = _half_swap(f"half_swap_{tag}{l}", gs)
            parts = _half_add(f"half_add_{tag}{l}", gs, got, place)
            send, recv, parts, lands, token = _exchange_start(f"exchange_start_{tag}{l}", parts)
            pending.append((f"{tag}{l}", l, which, send, recv, parts, lands))
            return token
        return group

    loss, dx, smalls = _local_step(x[0], loss_target[0], [layer_weights(l) for l in range(n_l)], reduce)
    sums = [lax.empty(w.shape, F32) for w in big_w]
    for tag, l, which, send, recv, parts, lands in pending:
        parts, lands = _exchange_wait(f"exchange_wait_{tag}", send, recv, parts, lands, dx)
        done = _chip_sum(f"chip_sum_{tag}", parts, lands, [sums[i] for i in which], place, l)
        for i, a in zip(which, done):
            sums[i] = a
    g_big = _half_fill(sums)
    big_m = [m_w_in, m_w_o, m_w_mlp_in, m_w_mlp_out]
    big_v = [v_w_in, v_w_o, v_w_mlp_in, v_w_mlp_out]
    g_big, d_big, nm_big, nv_big = _adamw(big_w, g_big, big_m, big_v)

    env = dict(norm1_g=(norm1_g, m_norm1_g, v_norm1_g), norm2_g=(norm2_g, m_norm2_g, v_norm2_g),
               conv_b=(conv_b, m_conv_b, v_conv_b), conv_ln_g=(conv_ln_g, m_conv_ln_g, v_conv_ln_g),
               conv_ln_b=(conv_ln_b, m_conv_ln_b, v_conv_ln_b), q_norm_g=(q_norm_g, m_q_norm_g, v_q_norm_g),
               k_norm_g=(k_norm_g, m_k_norm_g, v_k_norm_g), b_f=(b_f, m_b_f, v_b_f))
    g_dicts = [{nm: s[key] for nm, key in SMALL_KEYS.items()} for s in smalls]
    g_dicts[0]["spare"] = loss
    g_pack = _pack_small(g_dicts)
    packs = [_pack_small([{nm: env[nm][t][l] for nm in env} for l in range(n_l)]) for t in range(3)]
    pad_cw = lambda a: jnp.pad(a, ((0, 0), (0, CONV_HALO - CONV_TAPS), (0, 0)))
    cw_row0 = tuple(l * SMALL_ROWS_PAD + CONV_W_ROW0 for l in range(n_l))
    gs, ds, ms, vs, cg, cd, cm, cv = _small_allreduce_adamw(
        g_pack, packs[0], packs[1], packs[2], pad_cw(conv_w), pad_cw(m_conv_w), pad_cw(v_conv_w), cw_row0)

    def small_out(packed, conv):
        o = {nm: _unpack_small(packed, nm, sz) for nm, sz in SMALL_SIZES.items()}
        o["conv_w"] = conv[:, 0:CONV_TAPS, :]
        return o

    def ordered(small, big):
        return (small["norm1_g"], big[0], small["b_f"], small["q_norm_g"], small["k_norm_g"], small["conv_w"],
                small["conv_b"], small["conv_ln_g"], small["conv_ln_b"], big[1], small["norm2_g"], big[2], big[3])

    return (gs[LOSS_ROW, 0], dx[None],
            *ordered(small_out(gs, cg), g_big), *ordered(small_out(ds, cd), d_big),
            *ordered(small_out(ms, cm), nm_big), *ordered(small_out(vs, cv), nv_big))
```

```python
import functools

import jax
import jax.numpy as jnp
from jax import lax
from jax.experimental import pallas as pl
from jax.experimental.pallas import tpu as pltpu

F32 = jnp.float32
BF16 = jnp.bfloat16

D_MODEL = 1024
ATT_HEADS = 8
HEAD_DIM = 64
ATT_W = ATT_HEADS * HEAD_DIM
CONV_CH = 512
CONV_TAPS = 31
CONV_HALO = 32
D_FF = 4 * D_MODEL
N_IN = 3 * ATT_W + ATT_HEADS + 2 * CONV_CH
O_Q, O_K, O_V, O_A, O_G, O_F = 0, 512, 1024, 1536, 2048, 2560
N_INP = O_F + 128
EPS = 1e-6
QK_SCALE = 0.125

ADAM_LR = 0.001
ADAM_B1 = 0.9
ADAM_B2 = 0.999
ADAM_EPS = 1e-08
ADAM_WD = 0.01
ADAM_STEP = 10

N_CHIPS = 4
VMEM_LIMIT = 52 * 1024 * 1024
MESH = pl.DeviceIdType.MESH
ANY = pl.BlockSpec(memory_space=pl.ANY)


def _params(n_axes, **kw):
    return pltpu.CompilerParams(dimension_semantics=("arbitrary",) * n_axes,
                                vmem_limit_bytes=VMEM_LIMIT, **kw)


def _dot(a, b):
    return jnp.dot(a, b, preferred_element_type=F32)


def _dot_nt(a, b):
    return lax.dot_general(a, b, (((1,), (1,)), ((), ())), preferred_element_type=F32)


def _dot_tn(a, b):
    return lax.dot_general(a, b, (((0,), (0,)), ((), ())), preferred_element_type=F32)


def _split3(a):
    a1 = a.astype(BF16)
    r = a - a1.astype(F32)
    a2 = r.astype(BF16)
    a3 = (r - a2.astype(F32)).astype(BF16)
    return a1, a2, a3


def _dot_hi_r(a, b_exact):
    return sum(_dot(p, b_exact) for p in _split3(a))


def _head_sums(a, blockdiag):
    a1 = a.astype(BF16)
    a2 = (a - a1.astype(F32)).astype(BF16)
    return _dot(a1, blockdiag) + _dot(a2, blockdiag)


def _dot_hi_l(a_exact, b):
    return sum(_dot(a_exact, p) for p in _split3(b))


def _sigmoid(x):
    return 1.0 / (1.0 + jnp.exp(-x))


def _head_blockdiag():
    i = jnp.arange(ATT_W) // HEAD_DIM
    return (i[:, None] == i[None, :]).astype(BF16)


AUG_LANES = 8


def _aug_place(first):
    piece = jnp.arange(3 * 128)[:, None] // 128
    h = jnp.arange(3 * 128)[:, None] % 128
    lane = jnp.arange(ATT_W)[None, :]
    return ((h < ATT_HEADS) & (lane == 128 * (h // 2) + AUG_LANES * (h % 2) + first + piece)).astype(BF16)


def _aug_ones(first):
    lane = jnp.arange(ATT_W) % 128
    pos = lane % AUG_LANES
    return ((lane < 2 * AUG_LANES) & (pos >= first) & (pos < first + 3)).astype(F32).reshape(1, ATT_W)


def _head_rows():
    h = jnp.arange(2 * ATT_HEADS)[:, None]
    i = jnp.arange(ATT_W)[None, :] // HEAD_DIM
    return (h == i).astype(BF16)


def _head_fold():
    i = jnp.arange(ATT_W)[:, None] % HEAD_DIM
    j = jnp.arange(128)[None, :]
    return (i == j).astype(BF16)


def _head_pick():
    i = jnp.arange(ATT_W)[:, None]
    h = jnp.arange(128)[None, :]
    return (i == h * HEAD_DIM).astype(BF16)


def _tril(n):
    r = jnp.arange(n)
    return (r[:, None] >= r[None, :]).astype(BF16)


SUBLANES = 8


def _fill_row_shifts(buf_ref, shifts_ref, tm):
    n = tm + CONV_HALO - SUBLANES
    for b in range(1, SUBLANES):
        shifts_ref[b - 1, 0:n, :] = buf_ref[pl.ds(b, n), :]


def _row_shifted(buf_ref, shifts_ref, offset, rows, base=0):
    a, b = divmod(offset, SUBLANES)
    start = pl.multiple_of(base + SUBLANES * a, SUBLANES)
    if b == 0:
        return buf_ref[pl.ds(start, rows), :]
    return shifts_ref[b - 1, pl.ds(start, rows), :]


CONV_ROWS = 32


def _mixer_in_fwd(x, g1, win, qg, kg, bf, cw, cvb, lng, lnb):
    S = x.shape[0]
    TM = min(512, S)
    nb = S // TM

    def body(x_ref, g1_ref, win_ref, qg_ref, kg_ref, bf_ref, cw_ref, cvb_ref, lng_ref, lnb_ref,
             bd_ref, tri_ref, pq_ref, pk_ref, oq_ref, ok_ref,
             u1_ref, proj_ref, q_ref, k_ref, v_ref, qa_ref, ka_ref, h0_ref, h1_ref, h3_ref,
             carry_ref, hbuf_ref, hs_ref):
        i = pl.program_id(0)

        @pl.when(i == 0)
        def _():
            carry_ref[...] = jnp.zeros_like(carry_ref)
            hbuf_ref[0:CONV_HALO, :] = jnp.zeros((CONV_HALO, CONV_CH), F32)

        @pl.when(i > 0)
        def _():
            hbuf_ref[0:CONV_HALO, :] = hbuf_ref[TM:TM + CONV_HALO, :]

        xv = x_ref[...]
        r = lax.rsqrt(jnp.mean(xv * xv, axis=-1, keepdims=True) + EPS)
        u = (xv * r * g1_ref[...]).astype(BF16)
        u1_ref[...] = u
        proj_ref[...] = _dot(u, win_ref[...])

        def headnorm(raw, gain):
            ss = _head_sums(raw * raw, bd_ref[...]) * (1.0 / HEAD_DIM)
            return raw * lax.rsqrt(ss + EPS) * gain

        q_ref[...] = (headnorm(proj_ref[:, O_Q:O_Q + ATT_W], qg_ref[...]) * QK_SCALE).astype(BF16)
        k_ref[...] = headnorm(proj_ref[:, O_K:O_K + ATT_W], kg_ref[...]).astype(BF16)
        v_ref[...] = proj_ref[:, O_V:O_V + ATT_W].astype(BF16)

        zf = proj_ref[:, O_F:O_F + 128] + bf_ref[...]
        logf = jnp.minimum(zf, 0.0) - jnp.log(1.0 + jnp.exp(-jnp.abs(zf)))
        lane = lax.broadcasted_iota(jnp.int32, (TM, 128), 1)
        logf = jnp.where(lane < ATT_HEADS, logf, 0.0)
        c8 = _dot_hi_l(tri_ref[...], logf) + carry_ref[...]
        carry_ref[...] = c8[TM - 1:TM, :]
        pieces = jnp.concatenate(_split3(c8), axis=1)
        qa_ref[...] = (_dot(pieces, pq_ref[...]) + oq_ref[...]).astype(BF16)
        ka_ref[...] = (ok_ref[...] - _dot(pieces, pk_ref[...])).astype(BF16)

        h0 = proj_ref[:, O_A:O_A + CONV_CH] * _sigmoid(proj_ref[:, O_G:O_G + CONV_CH])
        h0_ref[...] = h0
        hbuf_ref[CONV_HALO:CONV_HALO + TM, :] = h0
        _fill_row_shifts(hbuf_ref, hs_ref, TM)
        acc = jnp.zeros((TM, CONV_CH), F32) + cvb_ref[...]
        for j in range(CONV_TAPS):
            acc = acc + cw_ref[j:j + 1, :] * _row_shifted(hbuf_ref, hs_ref, CONV_HALO - CONV_TAPS + 1 + j, TM)
        h1_ref[...] = acc
        mu = jnp.mean(acc, axis=-1, keepdims=True)
        d = acc - mu
        var = jnp.mean(d * d, axis=-1, keepdims=True)
        h2 = d * lax.rsqrt(var + EPS) * lng_ref[...] + lnb_ref[...]
        h3_ref[...] = (h2 * _sigmoid(h2)).astype(BF16)

    row = lambda w: pl.BlockSpec((TM, w), lambda i: (i, 0))
    full = lambda a: pl.BlockSpec(a.shape, lambda i: (0,) * a.ndim)
    ins = (x, g1, win, qg, kg, bf, cw, cvb, lng, lnb, _head_blockdiag(), _tril(TM),
           _aug_place(0), _aug_place(3), _aug_ones(3), _aug_ones(0))
    return pl.pallas_call(
        body, name="mixer_in_fwd", grid=(nb,),
        in_specs=[row(D_MODEL)] + [full(a) for a in ins[1:]],
        out_specs=[row(D_MODEL), row(N_INP), row(ATT_W), row(ATT_W), row(ATT_W), row(ATT_W), row(ATT_W),
                   row(CONV_CH), row(CONV_CH), row(CONV_CH)],
        out_shape=[jax.ShapeDtypeStruct((S, D_MODEL), BF16),
                   jax.ShapeDtypeStruct((S, N_INP), F32),
                   jax.ShapeDtypeStruct((S, ATT_W), BF16),
                   jax.ShapeDtypeStruct((S, ATT_W), BF16),
                   jax.ShapeDtypeStruct((S, ATT_W), BF16),
                   jax.ShapeDtypeStruct((S, ATT_W), BF16),
                   jax.ShapeDtypeStruct((S, ATT_W), BF16),
                   jax.ShapeDtypeStruct((S, CONV_CH), F32),
                   jax.ShapeDtypeStruct((S, CONV_CH), F32),
                   jax.ShapeDtypeStruct((S, CONV_CH), BF16)],
        scratch_shapes=[pltpu.VMEM((1, 128), F32), pltpu.VMEM((TM + CONV_HALO, CONV_CH), F32),
                        pltpu.VMEM((SUBLANES - 1, TM + CONV_HALO, CONV_CH), F32)],
        compiler_params=_params(1),
    )(*ins)


def _pair_heads(lo, alo, x, xa):
    z = jnp.zeros_like(x)
    return (jnp.concatenate([jnp.where(lo, x, z), jnp.where(alo, xa, z)], axis=1),
            jnp.concatenate([jnp.where(lo, z, x), jnp.where(alo, z, xa)], axis=1))


def _attn_fwd(q, qa, k, ka, v):
    S = q.shape[0]
    T = min(1024, S)
    nq = S // T

    def body(q_ref, qa_ref, k_ref, ka_ref, v_ref, o_ref, lse_ref):
        qi = pl.program_id(1)
        lane = lax.broadcasted_iota(jnp.int32, (T, 128), 1)
        lo = lane < HEAD_DIM
        qm = _pair_heads(lo, lane < AUG_LANES, q_ref[...], qa_ref[...])
        tril = (lax.broadcasted_iota(jnp.int32, (T, T), 0) >= lax.broadcasted_iota(jnp.int32, (T, T), 1))

        def step(kj, carry, masked):
            off = pl.multiple_of(kj * T, T)
            kb = jnp.concatenate([k_ref[pl.ds(off, T), :], ka_ref[pl.ds(off, T), :]], axis=1)
            vb = v_ref[pl.ds(off, T), :]
            new = []
            for h in range(2):
                m, l, acc = carry[3 * h:3 * h + 3]
                s = _dot_nt(qm[h], kb)
                if masked:
                    s = jnp.where(tril, s, -1e30)
                m_new = jnp.maximum(m, jnp.max(s, axis=-1, keepdims=True))
                alpha = jnp.exp(m - m_new)
                p = jnp.exp(s - m_new)
                l = alpha * l + jnp.sum(p, axis=-1, keepdims=True)
                acc = alpha * acc + _dot(p.astype(BF16), vb)
                new += [m_new, l, acc]
            return tuple(new)

        init = (jnp.full((T, 1), -1e30, F32), jnp.zeros((T, 1), F32), jnp.zeros((T, 128), F32)) * 2
        carry = lax.fori_loop(0, qi, lambda kj, c: step(kj, c, False), init)
        m0, l0, a0, m1, l1, a1 = step(qi, carry, True)
        o_ref[...] = jnp.where(lo, a0 / l0, a1 / l1).astype(BF16)
        lse_t = jnp.where(lo, m0 + jnp.log(l0), m1 + jnp.log(l1)).T
        lse_ref[0:1, :] = lse_t[0:1, :]
        lse_ref[1:2, :] = lse_t[HEAD_DIM:HEAD_DIM + 1, :]

    qblk = pl.BlockSpec((T, 128), lambda hp, qi: (qi, hp))
    seq = pl.BlockSpec((S, 128), lambda hp, qi: (0, hp))
    return pl.pallas_call(
        body, name="attn_fwd", grid=(ATT_HEADS // 2, nq),
        in_specs=[qblk, qblk, seq, seq, seq],
        out_specs=[qblk, pl.BlockSpec((None, 2, T), lambda hp, qi: (hp, 0, qi))],
        out_shape=[jax.ShapeDtypeStruct((S, ATT_W), BF16),
                   jax.ShapeDtypeStruct((ATT_HEADS // 2, 2, S), F32)],
        compiler_params=_params(2),
    )(q, qa, k, ka, v)


def _attn_bwd(q, qa, k, ka, v, do, lse, dlt):
    S = q.shape[0]
    T = min(512, S)
    nq = S // T

    def body(q_ref, qa_ref, do_ref, lse_ref, dlt_ref, k_ref, ka_ref, v_ref, dq_ref, dk_ref, dv_ref, rows_ref, cols_ref):
        kj = pl.program_id(1)

        @pl.when(kj == 0)
        def _():
            dq_ref[...] = jnp.zeros_like(dq_ref)
            rows_ref[...] = jnp.zeros_like(rows_ref)

        lane = lax.broadcasted_iota(jnp.int32, (T, 128), 1)
        lo = lane < HEAD_DIM
        alo = lane < AUG_LANES
        triu = (lax.broadcasted_iota(jnp.int32, (T, T), 0) <= lax.broadcasted_iota(jnp.int32, (T, T), 1))
        kb = k_ref[...]
        kaug = jnp.concatenate([kb, ka_ref[...]], axis=1)
        vb = v_ref[...]

        def step(qi, carry, masked):
            off = pl.multiple_of(qi * T, T)
            qb = q_ref[pl.ds(off, T), :]
            dob = do_ref[pl.ds(off, T), :]
            qm = _pair_heads(lo, alo, qb, qa_ref[pl.ds(off, T), :])
            zero = jnp.zeros_like(qb)
            new, dqs = [], []
            for h in range(2):
                dk_a, dv_a, dc_a = carry[3 * h:3 * h + 3]
                dom = jnp.where(lo, dob, zero) if h == 0 else jnp.where(lo, zero, dob)
                s = _dot_nt(kaug, qm[h])
                if masked:
                    s = jnp.where(triu, s, -1e30)
                p = jnp.exp(s - lse_ref[h:h + 1, pl.ds(off, T)])
                dp = _dot_nt(vb, dom)
                ds = p * (dp - dlt_ref[h:h + 1, pl.ds(off, T)])
                pb = p.astype(BF16)
                dsb = ds.astype(BF16)
                dv_a = dv_a + _dot(pb, dob)
                dk_a = dk_a + _dot(dsb, qb)
                dc_a = dc_a + jnp.sum(ds, axis=1, keepdims=True)
                dqs.append(_dot_tn(dsb, kb))
                rows_ref[h:h + 1, pl.ds(off, T)] += jnp.sum(ds, axis=0, keepdims=True)
                new += [dk_a, dv_a, dc_a]
            dq_ref[pl.ds(off, T), :] += jnp.where(lo, dqs[0], dqs[1])
            return tuple(new)

        init = (jnp.zeros((T, 128), F32), jnp.zeros((T, 128), F32), jnp.zeros((T, 1), F32)) * 2
        carry = step(kj, init, True)
        carry = lax.fori_loop(kj + 1, nq, lambda qi, c: step(qi, c, False), carry)
        dk_ref[...] = jnp.where(lo, carry[0], carry[3])
        dv_ref[...] = jnp.where(lo, carry[1], carry[4])
        cols_ref[...] = -jnp.where(lo, carry[2], carry[5])

    seq = pl.BlockSpec((S, 128), lambda hp, kj: (0, hp))
    rows = pl.BlockSpec((None, 2, S), lambda hp, kj: (hp, 0, 0))
    kblk = pl.BlockSpec((T, 128), lambda hp, kj: (kj, hp))
    return pl.pallas_call(
        body, name="attn_bwd", grid=(ATT_HEADS // 2, nq),
        in_specs=[seq, seq, seq, rows, rows, kblk, kblk, kblk],
        out_specs=[seq, kblk, kblk, rows, kblk],
        out_shape=[jax.ShapeDtypeStruct((S, ATT_W), F32), jax.ShapeDtypeStruct((S, ATT_W), F32),
                   jax.ShapeDtypeStruct((S, ATT_W), F32),
                   jax.ShapeDtypeStruct((ATT_HEADS // 2, 2, S), F32),
                   jax.ShapeDtypeStruct((S, ATT_W), F32)],
        compiler_params=_params(2),
    )(q, qa, do, lse, dlt, k, ka, v)


def _wo_spec(wo4):
    return pl.BlockSpec(wo4.shape, lambda i: (0, 0, 0))


def _wo_halves(wo_ref):
    half = N_CHIPS // 2
    return (wo_ref[0:half].reshape(ATT_W, D_MODEL), wo_ref[half:N_CHIPS].reshape(CONV_CH, D_MODEL))


def _mixer_out_fwd(x, att, h3, wo4, g2):
    S = x.shape[0]
    TM = min(512, S)

    def body(x_ref, att_ref, h3_ref, wo_ref, g2_ref, x2_ref, u2_ref):
        wa, wc = _wo_halves(wo_ref)
        x2 = x_ref[...] + _dot(att_ref[...], wa) + _dot(h3_ref[...], wc)
        x2_ref[...] = x2
        r = lax.rsqrt(jnp.mean(x2 * x2, axis=-1, keepdims=True) + EPS)
        u2_ref[...] = (x2 * r * g2_ref[...]).astype(BF16)

    row = lambda w: pl.BlockSpec((TM, w), lambda i: (i, 0))
    full = lambda a: pl.BlockSpec(a.shape, lambda i: (0,) * a.ndim)
    return pl.pallas_call(
        body, name="mixer_out_fwd", grid=(S // TM,),
        in_specs=[row(D_MODEL), row(ATT_W), row(CONV_CH), _wo_spec(wo4), full(g2)],
        out_specs=[row(D_MODEL), row(D_MODEL)],
        out_shape=[jax.ShapeDtypeStruct((S, D_MODEL), F32), jax.ShapeDtypeStruct((S, D_MODEL), BF16)],
        compiler_params=_params(1),
    )(x, att, h3, wo4, g2)


def _mlp_w_specs():
    return [pl.BlockSpec((None, D_MODEL, D_FF // N_CHIPS), lambda i, f: (f, 0, 0)),
            pl.BlockSpec((None, D_FF // N_CHIPS, D_MODEL), lambda i, f: (f, 0, 0))]


def _mlp_fwd(x2, u2, w1, w2, target=None):
    S = x2.shape[0]
    head = target is not None
    TM = min(1024, S)
    TF = 1024
    nf = D_FF // TF

    def body(*refs):
        x2_ref, u2_ref, w1_ref, w2_ref = refs[:4]
        x3_ref, z_ref, hh_ref = refs[4 + head:7 + head]
        i = pl.program_id(0)
        f = pl.program_id(1)

        @pl.when(f == 0)
        def _():
            x3_ref[...] = x2_ref[...]

        z = _dot(u2_ref[...], w1_ref[...])
        z_ref[...] = z
        zr = jnp.maximum(z, 0.0)
        hh = (zr * zr).astype(BF16)
        hh_ref[...] = hh
        x3_ref[...] += _dot(hh, w2_ref[...])

        if head:
            t_ref, loss_ref = refs[4], refs[8]

            @pl.when((i == 0) & (f == 0))
            def _():
                loss_ref[...] = jnp.zeros_like(loss_ref)

            @pl.when(f == nf - 1)
            def _():
                d = x3_ref[...] - t_ref[...]
                x3_ref[...] = d * (1.0 / D_MODEL)
                loss_ref[...] += jnp.sum(d * d)

    rows = pl.BlockSpec((TM, D_MODEL), lambda i, f: (i, 0))
    once = pl.BlockSpec((TM, D_MODEL), lambda i, f: (i, 0), pipeline_mode=pl.Buffered(1))
    tile = pl.BlockSpec((TM, TF), lambda i, f: (i, f))
    return pl.pallas_call(
        body, name="mlp_fwd_loss" if head else "mlp_fwd", grid=(S // TM, nf),
        in_specs=[once if head else rows, rows] + _mlp_w_specs() + [once] * head,
        out_specs=[rows, tile, tile] + [pl.BlockSpec((8, 128), lambda i, f: (0, 0))] * head,
        out_shape=[jax.ShapeDtypeStruct((S, D_MODEL), F32), jax.ShapeDtypeStruct((S, D_FF), F32),
                   jax.ShapeDtypeStruct((S, D_FF), BF16)] + [jax.ShapeDtypeStruct((8, 128), F32)] * head,
        compiler_params=_params(2),
    )(x2, u2, w1, w2, *([target] if head else []))


def _mlp_bwd(dx3, z, x2, g2, w1, w2):
    S = dx3.shape[0]
    TM = min(512, S)
    TF = 1024
    nf = D_FF // TF

    def body(dx3_ref, z_ref, x2_ref, g2_ref, w1_ref, w2_ref, dz_ref, dx2_ref, dg2_ref, du2_ref):
        i = pl.program_id(0)
        f = pl.program_id(1)

        @pl.when((i == 0) & (f == 0))
        def _():
            dg2_ref[...] = jnp.zeros_like(dg2_ref)

        @pl.when(f == 0)
        def _():
            du2_ref[...] = jnp.zeros_like(du2_ref)

        dhh = _dot_nt(dx3_ref[...].astype(BF16), w2_ref[...])
        dz = (dhh * (2.0 * jnp.maximum(z_ref[...], 0.0))).astype(BF16)
        dz_ref[...] = dz
        du2_ref[...] += _dot_nt(dz, w1_ref[...])

        @pl.when(f == nf - 1)
        def _():
            x2 = x2_ref[...]
            r = lax.rsqrt(jnp.mean(x2 * x2, axis=-1, keepdims=True) + EPS)
            n = x2 * r
            du2 = du2_ref[...]
            t = du2 * g2_ref[...]
            dx2_ref[...] = dx3_ref[...] + r * (t - n * jnp.mean(t * n, axis=-1, keepdims=True))
            dg2_ref[0:1, :] += jnp.sum(du2 * n, axis=0, keepdims=True)

    rowi = pl.BlockSpec((TM, D_MODEL), lambda i, f: (i, 0))
    return pl.pallas_call(
        body, name="mlp_bwd", grid=(S // TM, nf),
        in_specs=[rowi, pl.BlockSpec((TM, TF), lambda i, f: (i, f)), rowi,
                  pl.BlockSpec((1, D_MODEL), lambda i, f: (0, 0))] + _mlp_w_specs(),
        out_specs=[pl.BlockSpec((TM, TF), lambda i, f: (i, f)), rowi, pl.BlockSpec((8, D_MODEL), lambda i, f: (0, 0))],
        out_shape=[jax.ShapeDtypeStruct((S, D_FF), BF16), jax.ShapeDtypeStruct((S, D_MODEL), F32),
                   jax.ShapeDtypeStruct((8, D_MODEL), F32)],
        scratch_shapes=[pltpu.VMEM((TM, D_MODEL), F32)],
        compiler_params=_params(2),
    )(dx3, z, x2, g2, w1, w2)


def _matmul_tn(a, b, col_shards=1):
    S, I = a.shape
    J = b.shape[1]
    TI = min(I, 1024)
    TJ = 1024 if J % 1024 == 0 else 896
    TS = min(S, 1024)
    nk = S // TS
    per = J // col_shards // TJ

    def body(a_ref, b_ref, o_ref, acc_ref):
        k = pl.program_id(2)

        @pl.when(k == 0)
        def _():
            acc_ref[...] = jnp.zeros_like(acc_ref)

        acc_ref[...] += _dot_tn(a_ref[...].astype(BF16), b_ref[...].astype(BF16))

        @pl.when(k == nk - 1)
        def _():
            o_ref[...] = acc_ref[...].astype(BF16)

    return pl.pallas_call(
        body, name="matmul_tn", grid=(I // TI, J // TJ, nk),
        in_specs=[pl.BlockSpec((TS, TI), lambda i, j, k: (k, i)), pl.BlockSpec((TS, TJ), lambda i, j, k: (k, j))],
        out_specs=pl.BlockSpec((None, TI, TJ), lambda i, j, k: (j // per, i, j % per)),
        out_shape=jax.ShapeDtypeStruct((col_shards, I, J // col_shards), BF16),
        scratch_shapes=[pltpu.VMEM((TI, TJ), F32)],
        compiler_params=_params(3),
    )(a, b)


def _mixer_out_bwd(dx2, wo4, att, h1, lng, lnb):
    S = dx2.shape[0]
    TM = min(512, S)

    def body(dx2_ref, wo_ref, att_ref, h1_ref, lng_ref, lnb_ref, hr_ref, datt_ref, dlt_ref, dh1_ref, sm_ref):
        @pl.when(pl.program_id(0) == 0)
        def _():
            sm_ref[...] = jnp.zeros_like(sm_ref)

        dxb = dx2_ref[...].astype(BF16)
        wa, wc = _wo_halves(wo_ref)
        datt = _dot_nt(dxb, wa)
        datt_ref[...] = datt.astype(BF16)
        prod = datt * att_ref[...].astype(F32)
        dlt_ref[...] = sum(_dot_nt(hr_ref[...], piece) for piece in _split3(prod))[0:ATT_HEADS, :]
        dh3 = _dot_nt(dxb, wc)
        h1 = h1_ref[...]
        mu = jnp.mean(h1, axis=-1, keepdims=True)
        d = h1 - mu
        rstd = lax.rsqrt(jnp.mean(d * d, axis=-1, keepdims=True) + EPS)
        n = d * rstd
        h2 = n * lng_ref[...] + lnb_ref[...]
        sg = _sigmoid(h2)
        dh2 = dh3 * (sg * (1.0 + h2 * (1.0 - sg)))
        dn = dh2 * lng_ref[...]
        dh1 = rstd * (dn - jnp.mean(dn, axis=-1, keepdims=True) - n * jnp.mean(dn * n, axis=-1, keepdims=True))
        dh1_ref[...] = dh1
        sm_ref[0:1, :] += jnp.sum(dh2 * n, axis=0, keepdims=True)
        sm_ref[1:2, :] += jnp.sum(dh2, axis=0, keepdims=True)
        sm_ref[2:3, :] += jnp.sum(dh1, axis=0, keepdims=True)

    row = lambda w: pl.BlockSpec((TM, w), lambda i: (i, 0))
    full = lambda a: pl.BlockSpec(a.shape, lambda i: (0,) * a.ndim)
    hr = _head_rows()
    return pl.pallas_call(
        body, name="mixer_out_bwd", grid=(S // TM,),
        in_specs=[row(D_MODEL), _wo_spec(wo4), row(ATT_W), row(CONV_CH), full(lng), full(lnb), full(hr)],
        out_specs=[row(ATT_W), pl.BlockSpec((ATT_HEADS, TM), lambda i: (0, i)), row(CONV_CH),
                   pl.BlockSpec((8, CONV_CH), lambda i: (0, 0))],
        out_shape=[jax.ShapeDtypeStruct((S, ATT_W), BF16), jax.ShapeDtypeStruct((ATT_HEADS, S), F32),
                   jax.ShapeDtypeStruct((S, CONV_CH), F32), jax.ShapeDtypeStruct((8, CONV_CH), F32)],
        compiler_params=_params(1),
    )(dx2, wo4, att, h1, lng, lnb, hr)


def _conv_glu_bwd(dh1, h0, proj, cw):
    S = dh1.shape[0]
    TM = min(512, S)
    nb = S // TM
    lead = CONV_HALO - CONV_TAPS + 1

    def body(dh1_ref, dnx_ref, h0_ref, hpv_ref, a_ref, g_ref, cw_ref, dag_ref, dcw_ref,
             dbuf_ref, hbuf_ref, ds_ref, hs_ref, dh0_ref, dcw8_ref):
        i = pl.program_id(0)

        @pl.when(i == 0)
        def _():
            dcw8_ref[...] = jnp.zeros_like(dcw8_ref)

        dbuf_ref[0:TM, :] = dh1_ref[...]
        dbuf_ref[TM:TM + CONV_HALO, :] = jnp.where(i < nb - 1, dnx_ref[0:CONV_HALO, :], 0.0)
        hbuf_ref[0:CONV_HALO, :] = jnp.where(i > 0, hpv_ref[TM - CONV_HALO:TM, :], 0.0)
        hbuf_ref[CONV_HALO:CONV_HALO + TM, :] = h0_ref[...]
        _fill_row_shifts(dbuf_ref, ds_ref, TM)
        _fill_row_shifts(hbuf_ref, hs_ref, TM)

        def conv_rows(step, _):
            r0 = pl.multiple_of(step * CONV_ROWS, CONV_ROWS)
            dh1 = dbuf_ref[pl.ds(r0, CONV_ROWS), :]
            part = jnp.zeros((CONV_ROWS, CONV_CH), F32)
            for j in range(CONV_TAPS):
                part = part + cw_ref[j:j + 1, :] * _row_shifted(dbuf_ref, ds_ref, CONV_TAPS - 1 - j, CONV_ROWS, r0)
                prod = dh1 * _row_shifted(hbuf_ref, hs_ref, lead + j, CONV_ROWS, r0)
                dcw8_ref[j] += jnp.sum(prod.reshape(CONV_ROWS // SUBLANES, SUBLANES, CONV_CH), axis=0)
            dh0_ref[pl.ds(r0, CONV_ROWS), :] = part
            return 0

        lax.fori_loop(0, TM // CONV_ROWS, conv_rows, 0)

        @pl.when(i == nb - 1)
        def _():
            dcw_ref[...] = jnp.sum(dcw8_ref[...], axis=1)

        dh0 = dh0_ref[...]
        sg = _sigmoid(g_ref[...])
        dag_ref[:, 0:CONV_CH] = (dh0 * sg).astype(BF16)
        dag_ref[:, CONV_CH:2 * CONV_CH] = (dh0 * a_ref[...] * sg * (1.0 - sg)).astype(BF16)

    blk = lambda fn: pl.BlockSpec((TM, CONV_CH), fn)
    return pl.pallas_call(
        body, name="conv_glu_bwd", grid=(nb,),
        in_specs=[blk(lambda i: (i, 0)), blk(lambda i: (jnp.minimum(i + 1, nb - 1), 0)),
                  blk(lambda i: (i, 0)), blk(lambda i: (jnp.maximum(i - 1, 0), 0)),
                  blk(lambda i: (i, O_A // CONV_CH)), blk(lambda i: (i, O_G // CONV_CH)),
                  pl.BlockSpec(cw.shape, lambda i: (0, 0))],
        out_specs=[pl.BlockSpec((TM, 2 * CONV_CH), lambda i: (i, 0)), pl.BlockSpec((CONV_HALO, CONV_CH), lambda i: (0, 0))],
        out_shape=[jax.ShapeDtypeStruct((S, 2 * CONV_CH), BF16), jax.ShapeDtypeStruct((CONV_HALO, CONV_CH), F32)],
        scratch_shapes=[pltpu.VMEM((TM + CONV_HALO, CONV_CH), F32), pltpu.VMEM((TM + CONV_HALO, CONV_CH), F32),
                        pltpu.VMEM((SUBLANES - 1, TM + CONV_HALO, CONV_CH), F32),
                        pltpu.VMEM((SUBLANES - 1, TM + CONV_HALO, CONV_CH), F32),
                        pltpu.VMEM((TM, CONV_CH), F32), pltpu.VMEM((CONV_HALO, SUBLANES, CONV_CH), F32)],
        compiler_params=_params(1),
    )(dh1, dh1, h0, h0, proj, proj, cw)


def _mixer_in_bwd(x, dx2, proj, dq, dk, dv, dag, dct, drb, g1, win, qg, kg, bf):
    S = x.shape[0]
    TM = min(512, S)
    nb = S // TM

    def body(x_ref, dx2_ref, qr_ref, kr_ref, fz_ref, dq_ref, dk_ref, dv_ref, dag_ref, dct_ref, drb_ref,
             g1_ref, win_ref, qg_ref, kg_ref, bf_ref, bd_ref, fold_ref, triu_ref, pick_ref,
             dproj_ref, dx_ref, dg1_ref, sm_ref, carry_ref, gsum_ref):
        i = pl.program_id(0)

        @pl.when(i == 0)
        def _():
            carry_ref[...] = jnp.zeros_like(carry_ref)
            gsum_ref[...] = jnp.zeros_like(gsum_ref)
            dg1_ref[...] = jnp.zeros_like(dg1_ref)
            sm_ref[...] = jnp.zeros_like(sm_ref)

        def headnorm_bwd(raw, dy, gain, scale, row):
            rs = lax.rsqrt(_head_sums(raw * raw, bd_ref[...]) * (1.0 / HEAD_DIM) + EPS)
            n = raw * rs
            gsum_ref[row:row + 1, :] += jnp.sum(dy * n, axis=0, keepdims=True) * scale
            dn = dy * (gain * scale)
            return rs * (dn - n * (_head_sums(dn * n, bd_ref[...]) * (1.0 / HEAD_DIM)))

        dproj_ref[:, O_Q:O_Q + ATT_W] = headnorm_bwd(qr_ref[...], dq_ref[...], qg_ref[...], QK_SCALE, 0).astype(BF16)
        dproj_ref[:, O_K:O_K + ATT_W] = headnorm_bwd(kr_ref[...], dk_ref[...], kg_ref[...], 1.0, 1).astype(BF16)
        dproj_ref[:, O_V:O_V + ATT_W] = dv_ref[...].astype(BF16)
        dproj_ref[:, O_A:O_A + 2 * CONV_CH] = dag_ref[...]

        dc8 = jnp.concatenate([dct_ref[...], jnp.zeros((128 - ATT_HEADS, TM), F32)], axis=0).T
        dc8 = dc8 + _dot_hi_r(drb_ref[...], pick_ref[...])
        dlogf = _dot_hi_l(triu_ref[...], dc8) + carry_ref[...]
        carry_ref[...] = dlogf[0:1, :]
        df = dlogf * _sigmoid(-(fz_ref[...] + bf_ref[...]))
        dproj_ref[:, O_F:O_F + 128] = df.astype(BF16)
        sm_ref[2:3, :] += jnp.sum(df, axis=0, keepdims=True)

        du1 = _dot_nt(dproj_ref[...], win_ref[...])
        xv = x_ref[...]
        r = lax.rsqrt(jnp.mean(xv * xv, axis=-1, keepdims=True) + EPS)
        n1 = xv * r
        t = du1 * g1_ref[...]
        dx_ref[...] = dx2_ref[...] + r * (t - n1 * jnp.mean(t * n1, axis=-1, keepdims=True))
        dg1_ref[0:1, :] += jnp.sum(du1 * n1, axis=0, keepdims=True)

        @pl.when(i == nb - 1)
        def _():
            sm_ref[0:2, :] = _dot_hi_r(gsum_ref[0:8, :], fold_ref[...])[0:2, :]

    rev = lambda w, cb=0: pl.BlockSpec((TM, w), lambda i: (nb - 1 - i, cb))
    full = lambda a: pl.BlockSpec(a.shape, lambda i: (0,) * a.ndim)
    bd, fold, triu = _head_blockdiag(), _head_fold(), _tril(TM).T
    consts = (g1, win, qg, kg, bf, bd, fold, triu, _head_pick())
    return pl.pallas_call(
        body, name="mixer_in_bwd", grid=(nb,),
        in_specs=[rev(D_MODEL), rev(D_MODEL), rev(ATT_W, O_Q // ATT_W), rev(ATT_W, O_K // ATT_W), rev(128, O_F // 128),
                  rev(ATT_W), rev(ATT_W), rev(ATT_W), rev(2 * CONV_CH),
                  pl.BlockSpec((ATT_HEADS, TM), lambda i: (0, nb - 1 - i)), rev(ATT_W)] + [full(a) for a in consts],
        out_specs=[rev(N_INP), rev(D_MODEL), pl.BlockSpec((8, D_MODEL), lambda i: (0, 0)),
                   pl.BlockSpec((8, 128), lambda i: (0, 0))],
        out_shape=[jax.ShapeDtypeStruct((S, N_INP), BF16), jax.ShapeDtypeStruct((S, D_MODEL), F32),
                   jax.ShapeDtypeStruct((8, D_MODEL), F32), jax.ShapeDtypeStruct((8, 128), F32)],
        scratch_shapes=[pltpu.VMEM((1, 128), F32), pltpu.VMEM((8, ATT_W), F32)],
        compiler_params=_params(1),
    )(x, dx2, proj, proj, proj, dq, dk, dv, dag, dct, drb, *consts)


def _layer_fwd(x, early, late, target=None):
    p = early(x)
    u1, proj, q, k, v, qa, ka, h0, h1, h3 = _mixer_in_fwd(
        x, p["g1"], p["win"], p["qg"], p["kg"], p["bf"], p["cw"], p["cvb"], p["lng"], p["lnb"])
    att, lse = _attn_fwd(q, qa, k, ka, v)
    p = dict(p, **late(att))
    x2, u2 = _mixer_out_fwd(x, att, h3, p["wo"], p["g2"])
    x3, z, hh, *loss_acc = _mlp_fwd(x2, u2, p["w1"], p["w2"], target)
    saved = dict(x=x, u1=u1, proj=proj, q=q, k=k, v=v, qa=qa, ka=ka, h0=h0, h1=h1, h3=h3, att=att, lse=lse,
                 x2=x2, u2=u2, z=z, hh=hh)
    return (x3 if target is None else (x3, loss_acc[0])), saved, p


def _tie(a, token):
    return a if token is None else a + token[0:1, 0:1]


def _layer_bwd(dx3, s, p, reduce):
    dz, dx2, dg2 = _mlp_bwd(dx3, s["z"], s["x2"], p["g2"], p["w1"], p["w2"])
    g_w2 = _matmul_tn(s["hh"], dx3)
    g_w1 = _matmul_tn(s["u2"], dz, col_shards=N_CHIPS)
    token = reduce("a", {2: g_w1, 3: g_w2.reshape(N_CHIPS, D_FF // N_CHIPS, D_MODEL)})
    datt, dlt, dh1, sm_c = _mixer_out_bwd(dx2, p["wo"], s["att"], s["h1"], _tie(p["lng"], token), p["lnb"])
    g_wo = jnp.concatenate([_matmul_tn(s["att"], dx2)[0], _matmul_tn(s["h3"], dx2)[0]], axis=0)
    dag, dcw = _conv_glu_bwd(dh1, s["h0"], s["proj"], p["cw"])
    dq, dk, dv, dc4, drb = _attn_bwd(s["q"], s["qa"], s["k"], s["ka"], s["v"], datt, s["lse"],
                                     dlt.reshape(ATT_HEADS // 2, 2, dlt.shape[1]))
    dct = dc4.reshape(ATT_HEADS, dc4.shape[2])
    dproj, dx, dg1, sm_a = _mixer_in_bwd(s["x"], dx2, s["proj"], dq, dk, dv, dag, dct, drb,
                                         p["g1"], p["win"], p["qg"], p["kg"], p["bf"])
    g_win = _win_to_global(_matmul_tn(s["u1"], dproj)[0])
    g_win = g_win.reshape(D_MODEL, N_CHIPS, N_IN // N_CHIPS).transpose(1, 0, 2)
    token = reduce("b", {0: g_win, 1: g_wo.reshape(N_CHIPS, D_MODEL // N_CHIPS, D_MODEL)})
    small = dict(g1=dg1[0], g2=dg2[0], lng=sm_c[0], lnb=sm_c[1], cvb=sm_c[2], cw=dcw[0:CONV_TAPS],
                 qg=sm_a[0, 0:HEAD_DIM], kg=sm_a[1, 0:HEAD_DIM], bf=sm_a[2, 0:ATT_HEADS])
    return dx, small, token


def _local_step(x, target, weights, reduce):
    saved, layers = [], []
    h = x
    for l, (early, late) in enumerate(weights):
        h, s, p = _layer_fwd(h, early, late, target if l == len(weights) - 1 else None)
        saved.append(s)
        layers.append(p)
    dy, loss_acc = h
    loss = loss_acc[0, 0] * (0.5 / D_MODEL)
    smalls = []
    d, token = dy, None
    for l in reversed(range(len(layers))):
        d, small, token = _layer_bwd(d, saved[l], dict(layers[l], g2=_tie(layers[l]["g2"], token)), reduce(l))
        smalls.append(small)
    return loss, d, smalls[::-1]


def _win_to_internal(w):
    pad = jnp.zeros(w.shape[:-1] + (N_INP - N_IN,), w.dtype)
    return jnp.concatenate([w[..., :1536], w[..., 1544:], w[..., 1536:1544], pad], axis=-1)


def _win_to_global(g):
    return jnp.concatenate([g[..., :1536], g[..., O_F:O_F + ATT_HEADS], g[..., 1536:O_F]], axis=-1)


def _layer_params(l, win, cw, norm1_g, b_f, q_norm_g, k_norm_g, conv_b, conv_ln_g, conv_ln_b, norm2_g):
    row = lambda a: a.reshape(1, -1)
    return dict(
        win=win, cw=jnp.pad(cw, ((0, CONV_HALO - CONV_TAPS), (0, 0))),
        g1=row(norm1_g[l]), g2=row(norm2_g[l]),
        qg=row(jnp.tile(q_norm_g[l], ATT_HEADS)), kg=row(jnp.tile(k_norm_g[l], ATT_HEADS)),
        bf=row(jnp.pad(b_f[l], (0, 128 - ATT_HEADS))),
        cvb=row(conv_b[l]), lng=row(conv_ln_g[l]), lnb=row(conv_ln_b[l]))


def _place():
    x, y, c = lax.axis_index("x"), lax.axis_index("y"), lax.axis_index("c")
    chips = [(1 - x, y), (x, 1 - y), (1 - x, 1 - y)]
    return x, y, c, chips


HBM = pl.BlockSpec(memory_space=pltpu.HBM)
SEM = pl.BlockSpec(memory_space=pltpu.SEMAPHORE)
GATHER_PEERS = N_CHIPS


def _gather_peers():
    x, y, c, chips = _place()
    return [(*chip, c) for chip in chips] + [(x, y, 1 - c)], [2 * px + py for px, py in chips] + [2 * x + y]


def _gather_start(srcs):
    n = len(srcs)

    def body(*refs):
        ins, lands = refs[:n], refs[n:2 * n]
        send_sems, recv_sems, token = refs[2 * n], refs[2 * n + 1], refs[-1]
        me = 2 * lax.axis_index("x") + lax.axis_index("y")
        peers, _ = _gather_peers()
        for g in range(n):
            for j, to in enumerate(peers):
                pltpu.make_async_remote_copy(src_ref=ins[g], dst_ref=lands[g].at[me],
                                             send_sem=send_sems.at[GATHER_PEERS * g + j],
                                             recv_sem=recv_sems.at[GATHER_PEERS * g + j],
                                             device_id=to, device_id_type=MESH).start()
        token[...] = jnp.zeros_like(token)

    lands = [lax.empty((N_CHIPS,) + a.shape, a.dtype) for a in srcs]
    outs = pl.pallas_call(
        body, name="gather_start",
        in_specs=[HBM] * (2 * n),
        out_specs=[SEM, SEM] + [HBM] * (2 * n) + [pl.BlockSpec(memory_space=pltpu.VMEM)],
        out_shape=[pltpu.SemaphoreType.DMA((GATHER_PEERS * n,)), pltpu.SemaphoreType.DMA((GATHER_PEERS * n,))]
        + [pltpu.HBM(a.shape, a.dtype) for a in srcs] + [pltpu.HBM(a.shape, a.dtype) for a in lands]
        + [jax.ShapeDtypeStruct((8, 128), F32)],
        input_output_aliases={i: 2 + i for i in range(2 * n)},
        compiler_params=pltpu.CompilerParams(has_side_effects=pltpu.SideEffectType.DATAFLOW_SIDE_EFFECTING),
    )(*[pltpu.with_memory_space_constraint(a, pltpu.HBM) for a in srcs],
      *[pltpu.with_memory_space_constraint(a, pltpu.HBM) for a in lands])
    return outs[0], outs[1], list(outs[2:2 + n]), list(outs[2 + n:2 + 2 * n]), outs[-1]


def _gather_wait(name, groups, send_sems, recv_sems, srcs, lands, after):
    k = len(groups)

    def body(*refs):
        ins, lnd = refs[:k], refs[k:2 * k]
        ssem, rsem = refs[2 * k], refs[2 * k + 1]
        peers, slots = _gather_peers()
        for i, g in enumerate(groups):
            for j, to in enumerate(peers):
                cp = pltpu.make_async_remote_copy(src_ref=ins[i], dst_ref=lnd[i].at[slots[j]],
                                                  send_sem=ssem.at[GATHER_PEERS * g + j],
                                                  recv_sem=rsem.at[GATHER_PEERS * g + j],
                                                  device_id=to, device_id_type=MESH)
                cp.wait_send()
                cp.wait_recv()

    outs = pl.pallas_call(
        body, name=name,
        in_specs=[HBM] * (2 * k) + [SEM, SEM, ANY],
        out_specs=[HBM] * (2 * k),
        out_shape=[pltpu.HBM(a.shape, a.dtype) for a in srcs] + [pltpu.HBM(a.shape, a.dtype) for a in lands],
        input_output_aliases={i: i for i in range(2 * k)},
        compiler_params=pltpu.CompilerParams(has_side_effects=pltpu.SideEffectType.DATAFLOW_SIDE_EFFECTING),
    )(*srcs, *lands, send_sems, recv_sems, after)
    return list(outs[k:])


REDUCE_STEPS = 8


def _row_half(a):
    return a.shape[-2] // 2


def _half_swap(name, gs):
    n = len(gs)

    def body(*refs):
        ins, outs = refs[:n], refs[n:2 * n]
        send_sems, recv_sems = refs[2 * n:]
        x, y, c, _ = _place()
        copies = []
        for a in range(n):
            h = _row_half(gs[a])
            copies.append(pltpu.make_async_remote_copy(
                src_ref=ins[a].at[:, pl.ds((1 - c) * h, h), :], dst_ref=outs[a], send_sem=send_sems.at[a],
                recv_sem=recv_sems.at[a], device_id=(x, y, 1 - c), device_id_type=MESH))
        for cp in copies:
            cp.start()
        for cp in copies:
            cp.wait()

    return pl.pallas_call(
        body, name=name,
        in_specs=[ANY] * n, out_specs=[ANY] * n,
        out_shape=[jax.ShapeDtypeStruct((N_CHIPS, _row_half(a), a.shape[-1]), a.dtype) for a in gs],
        scratch_shapes=[pltpu.SemaphoreType.DMA((n,)), pltpu.SemaphoreType.DMA((n,))],
        compiler_params=pltpu.CompilerParams(has_side_effects=True),
    )(*gs)


def _half_specs(gs, row_block):
    tiles = [_row_half(a) // REDUCE_STEPS for a in gs]
    return [pl.BlockSpec((N_CHIPS, t, a.shape[-1]), lambda i, p: (0, row_block(i, p), 0)) for a, t in zip(gs, tiles)]


def _half_add(name, gs, got, place):
    n = len(gs)

    def body(place_ref, *refs):
        own, theirs, outs = refs[:n], refs[n:2 * n], refs[2 * n:]
        for a in range(n):
            outs[a][...] = (own[a][...].astype(F32) + theirs[a][...].astype(F32)).astype(BF16)

    plain = _half_specs(gs, lambda i, p: i)
    return pl.pallas_call(
        body, name=name,
        grid_spec=pltpu.PrefetchScalarGridSpec(
            num_scalar_prefetch=1, grid=(REDUCE_STEPS,),
            in_specs=_half_specs(gs, lambda i, p: p[1] * REDUCE_STEPS + i) + plain, out_specs=plain),
        out_shape=[jax.ShapeDtypeStruct(a.shape, BF16) for a in got],
        compiler_params=_params(1),
    )(place, *gs, *got)


def _exchange_copies(parts, lands, send_sems, recv_sems):
    x, y, c, chips = _place()
    return [pltpu.make_async_remote_copy(src_ref=parts[a].at[2 * px + py], dst_ref=lands[a].at[j],
                                         send_sem=send_sems.at[3 * a + j], recv_sem=recv_sems.at[3 * a + j],
                                         device_id=(px, py, c), device_id_type=MESH)
            for a in range(len(parts)) for j, (px, py) in enumerate(chips)]


def _exchange_start(name, parts):
    n = len(parts)

    def body(*refs):
        _ = [cp.start() for cp in _exchange_copies(refs[:n], refs[n:2 * n], refs[2 * n], refs[2 * n + 1])]
        refs[-1][...] = jnp.zeros_like(refs[-1])

    lands = [lax.empty((N_CHIPS - 1,) + a.shape[1:], a.dtype) for a in parts]
    outs = pl.pallas_call(
        body, name=name,
        in_specs=[HBM] * (2 * n),
        out_specs=[SEM, SEM] + [HBM] * (2 * n) + [pl.BlockSpec(memory_space=pltpu.VMEM)],
        out_shape=[pltpu.SemaphoreType.DMA((3 * n,)), pltpu.SemaphoreType.DMA((3 * n,))]
        + [pltpu.HBM(a.shape, a.dtype) for a in parts] + [pltpu.HBM(a.shape, a.dtype) for a in lands]
        + [jax.ShapeDtypeStruct((8, 128), F32)],
        input_output_aliases={i: 2 + i for i in range(2 * n)},
        compiler_params=pltpu.CompilerParams(has_side_effects=pltpu.SideEffectType.DATAFLOW_SIDE_EFFECTING),
    )(*[pltpu.with_memory_space_constraint(a, pltpu.HBM) for a in parts],
      *[pltpu.with_memory_space_constraint(a, pltpu.HBM) for a in lands])
    return outs[0], outs[1], list(outs[2:2 + n]), list(outs[2 + n:2 + 2 * n]), outs[-1]


def _exchange_wait(name, send_sems, recv_sems, parts, lands, after):
    n = len(parts)

    def body(*refs):
        for cp in _exchange_copies(refs[:n], refs[n:2 * n], refs[2 * n], refs[2 * n + 1]):
            cp.wait_send()
            cp.wait_recv()

    outs = pl.pallas_call(
        body, name=name,
        in_specs=[HBM] * (2 * n) + [SEM, SEM, ANY],
        out_specs=[HBM] * (2 * n),
        out_shape=[pltpu.HBM(a.shape, a.dtype) for a in parts] + [pltpu.HBM(a.shape, a.dtype) for a in lands],
        input_output_aliases={i: i for i in range(2 * n)},
        compiler_params=pltpu.CompilerParams(has_side_effects=pltpu.SideEffectType.DATAFLOW_SIDE_EFFECTING),
    )(*parts, *lands, send_sems, recv_sems, after)
    return list(outs[:n]), list(outs[n:])


def _chip_sum(name, parts, lands, sums, place, layer):
    n = len(parts)
    tiles = [a.shape[-2] // REDUCE_STEPS for a in parts]

    def body(place_ref, *refs):
        own, got, outs = refs[:n], refs[n:2 * n], refs[3 * n:]
        for a in range(n):
            tot = own[a][...].astype(F32)
            for j in range(N_CHIPS - 1):
                tot = tot + got[a][j].astype(F32)
            outs[a][...] = tot

    own_specs = [pl.BlockSpec((None, t, a.shape[-1]), lambda i, p: (p[0], i, 0)) for a, t in zip(parts, tiles)]
    got_specs = [pl.BlockSpec((N_CHIPS - 1, t, a.shape[-1]), lambda i, p: (0, i, 0)) for a, t in zip(parts, tiles)]
    out_specs = [pl.BlockSpec((None, t, a.shape[-1]), lambda i, p: (layer, p[1] * REDUCE_STEPS + i, 0))
                 for a, t in zip(parts, tiles)]
    return pl.pallas_call(
        body, name=name,
        grid_spec=pltpu.PrefetchScalarGridSpec(num_scalar_prefetch=1, grid=(REDUCE_STEPS,),
                                               in_specs=own_specs + got_specs + [ANY] * n, out_specs=out_specs),
        out_shape=[jax.ShapeDtypeStruct(a.shape, F32) for a in sums],
        input_output_aliases={1 + 2 * n + a: a for a in range(n)},
        compiler_params=_params(1),
    )(place, *parts, *lands, *sums)


def _half_fill(sums):
    n = len(sums)

    def body(*refs):
        ins, outs = refs[:n], refs[n:2 * n]
        send_sems, recv_sems = refs[2 * n:]
        x, y, c, _ = _place()
        copies = []
        for a in range(n):
            h = _row_half(sums[a])
            copies.append(pltpu.make_async_remote_copy(
                src_ref=ins[a].at[:, pl.ds(c * h, h), :], dst_ref=outs[a].at[:, pl.ds(c * h, h), :],
                send_sem=send_sems.at[a], recv_sem=recv_sems.at[a], device_id=(x, y, 1 - c), device_id_type=MESH))
        for cp in copies:
            cp.start()
        for a in range(n):
            h = _row_half(sums[a])
            theirs = outs[a].at[:, pl.ds((1 - c) * h, h), :]
            pltpu.make_async_remote_copy(src_ref=theirs, dst_ref=theirs, send_sem=send_sems.at[a], recv_sem=recv_sems.at[a],
                                         device_id=(x, y, 1 - c), device_id_type=MESH).wait_recv()
        for cp in copies:
            cp.wait_send()

    return pl.pallas_call(
        body, name="half_fill",
        in_specs=[ANY] * n, out_specs=[ANY] * n,
        out_shape=[jax.ShapeDtypeStruct(a.shape, a.dtype) for a in sums],
        input_output_aliases={a: a for a in range(n)},
        scratch_shapes=[pltpu.SemaphoreType.DMA((n,)), pltpu.SemaphoreType.DMA((n,))],
        compiler_params=pltpu.CompilerParams(has_side_effects=True),
    )(*sums)


def _adamw_math(w, g, m, v):
    m = ADAM_B1 * m + (1.0 - ADAM_B1) * g
    v = ADAM_B2 * v + (1.0 - ADAM_B2) * (g * g)
    m_hat = m / (1.0 - ADAM_B1 ** ADAM_STEP)
    v_hat = v / (1.0 - ADAM_B2 ** ADAM_STEP)
    delta = -ADAM_LR * (m_hat / (jnp.sqrt(v_hat) + ADAM_EPS) + ADAM_WD * w)
    return delta, m, v


def _adamw(ws, gs, ms, vs):
    n = len(ws)
    steps = 16
    tiles = [a.shape[-2] // steps for a in ws]

    def body(*refs):
        w_r, g_r, m_r, v_r = refs[:n], refs[n:2 * n], refs[2 * n:3 * n], refs[3 * n:4 * n]
        g_o, d_o, m_o, v_o = refs[4 * n:5 * n], refs[5 * n:6 * n], refs[6 * n:7 * n], refs[7 * n:]
        for a in range(n):
            g = g_r[a][...]
            d, m, v = _adamw_math(w_r[a][...], g, m_r[a][...], v_r[a][...])
            g_o[a][...] = g
            d_o[a][...] = d
            m_o[a][...] = m
            v_o[a][...] = v

    specs = [pl.BlockSpec((2, t, a.shape[-1]), lambda i: (0, i, 0)) for a, t in zip(ws, tiles)]
    outs = pl.pallas_call(
        body, name="adamw", grid=(steps,),
        in_specs=specs * 4, out_specs=specs * 4,
        out_shape=[jax.ShapeDtypeStruct(a.shape, F32) for a in ws] * 4,
        compiler_params=_params(1),
    )(*ws, *gs, *ms, *vs)
    return outs[:n], outs[n:2 * n], outs[2 * n:3 * n], outs[3 * n:]


SMALL_W = 512


def _small_allreduce_adamw(g, w, m, v, cw_w, cw_m, cw_v, cw_row0):
    R = g.shape[0]
    n_l = cw_w.shape[0]

    def body(g_ref, w_ref, m_ref, v_ref, cww_ref, cwm_ref, cwv_ref,
             gs_ref, d_ref, mo_ref, vo_ref, cg_ref, cd_ref, cmo_ref, cvo_ref,
             slots_ref, send_sems, recv_sems):
        x, y, c, _ = _place()
        me = 4 * x + 2 * y + c
        slots_ref[me] = g_ref[...]
        sends = []
        for d in range(1, 8):
            px, py, pc = x ^ (d >> 2), y ^ ((d >> 1) & 1), c ^ (d & 1)
            cp = pltpu.make_async_remote_copy(src_ref=g_ref, dst_ref=slots_ref.at[me], send_sem=send_sems.at[d - 1],
                                              recv_sem=recv_sems.at[d - 1], device_id=(px, py, pc), device_id_type=MESH)
            cp.start()
            sends.append(cp)
        for d in range(1, 8):
            px, py, pc = x ^ (d >> 2), y ^ ((d >> 1) & 1), c ^ (d & 1)
            slot = slots_ref.at[4 * px + 2 * py + pc]
            pltpu.make_async_remote_copy(src_ref=slot, dst_ref=slot, send_sem=send_sems.at[d - 1],
                                         recv_sem=recv_sems.at[d - 1], device_id=(px, py, pc),
                                         device_id_type=MESH).wait_recv()
        for cp in sends:
            cp.wait_send()
        tot = slots_ref[0]
        for k in range(1, 8):
            tot = tot + slots_ref[k]
        gs_ref[...] = tot
        dl, mn, vn = _adamw_math(w_ref[...], tot, m_ref[...], v_ref[...])
        d_ref[...] = dl
        mo_ref[...] = mn
        vo_ref[...] = vn
        chip = 2 * x + y
        for l in range(n_l):
            rows = tot[cw_row0[l]:cw_row0[l] + CONV_HALO, :]
            mine = rows[:, 0:128]
            for k in range(1, N_CHIPS):
                mine = jnp.where(chip == k, rows[:, 128 * k:128 * (k + 1)], mine)
            cg_ref[l] = mine
            dl, mn, vn = _adamw_math(cww_ref[l], mine, cwm_ref[l], cwv_ref[l])
            cd_ref[l] = dl
            cmo_ref[l] = mn
            cvo_ref[l] = vn

    vm = pl.BlockSpec(memory_space=pltpu.VMEM)
    small = jax.ShapeDtypeStruct((R, SMALL_W), F32)
    conv = jax.ShapeDtypeStruct(cw_w.shape, F32)
    return pl.pallas_call(
        body, name="small_allreduce_adamw",
        in_specs=[vm] * 7, out_specs=[vm] * 8,
        out_shape=[small] * 4 + [conv] * 4,
        scratch_shapes=[pltpu.VMEM((8, R, SMALL_W), F32), pltpu.SemaphoreType.DMA((7,)), pltpu.SemaphoreType.DMA((7,))],
        compiler_params=pltpu.CompilerParams(has_side_effects=True, vmem_limit_bytes=VMEM_LIMIT),
    )(g, w, m, v, cw_w, cw_m, cw_v)


SMALL_LAYOUT = (("conv_w", CONV_HALO), ("norm1_g", 2), ("norm2_g", 2), ("conv_b", 1), ("conv_ln_g", 1),
                ("conv_ln_b", 1), ("q_norm_g", 1), ("k_norm_g", 1), ("b_f", 1))
SMALL_ROWS = sum(r for _, r in SMALL_LAYOUT)
SMALL_ROWS_PAD = 48
LOSS_ROW = SMALL_ROWS


def _pack_small(per_layer):
    flat = []
    for d in per_layer:
        for name, r in SMALL_LAYOUT:
            n = r * SMALL_W
            a = d.get(name)
            if a is None:
                flat.append(jnp.zeros((n,), F32))
                continue
            flat.append(a.reshape(-1))
            if a.size < n:
                flat.append(jnp.zeros((n - a.size,), F32))
        spare = (SMALL_ROWS_PAD - SMALL_ROWS) * SMALL_W
        if "spare" in d:
            flat.append(d["spare"].reshape(-1))
            spare -= d["spare"].size
        flat.append(jnp.zeros((spare,), F32))
    return jnp.concatenate(flat).reshape(-1, SMALL_W)


def _unpack_small(packed, name, size):
    row0 = 0
    for nm, r in SMALL_LAYOUT:
        if nm == name:
            break
        row0 += r
    per_layer = packed.reshape(-1, SMALL_ROWS_PAD * SMALL_W)
    return per_layer[:, row0 * SMALL_W:row0 * SMALL_W + size]


SMALL_SIZES = dict(norm1_g=D_MODEL, norm2_g=D_MODEL, conv_b=CONV_CH, conv_ln_g=CONV_CH, conv_ln_b=CONV_CH,
                   q_norm_g=HEAD_DIM, k_norm_g=HEAD_DIM, b_f=ATT_HEADS)
SMALL_KEYS = dict(norm1_g="g1", norm2_g="g2", conv_b="cvb", conv_ln_g="lng", conv_ln_b="lnb",
                  q_norm_g="qg", k_norm_g="kg", b_f="bf", conv_w="cw")
CONV_W_ROW0 = 0


def kernel(x, norm1_g, w_in, b_f, q_norm_g, k_norm_g, conv_w, conv_b, conv_ln_g, conv_ln_b, w_o, norm2_g, w_mlp_in, w_mlp_out, loss_target, m_norm1_g, m_w_in, m_b_f, m_q_norm_g, m_k_norm_g, m_conv_w, m_conv_b, m_conv_ln_g, m_conv_ln_b, m_w_o, m_norm2_g, m_w_mlp_in, m_w_mlp_out, v_norm1_g, v_w_in, v_b_f, v_q_norm_g, v_k_norm_g, v_conv_w, v_conv_b, v_conv_ln_g, v_conv_ln_b, v_w_o, v_norm2_g, v_w_mlp_in, v_w_mlp_out):
    n_l = w_in.shape[0]

    per_layer = lambda l: [w_in[l].astype(BF16), conv_w[l], w_o[l].astype(BF16), w_mlp_in[l].astype(BF16),
                           w_mlp_out[l].astype(BF16)]
    n_w = len(per_layer(0))
    send_sems, recv_sems, srcs, lands, token = _gather_start([a for l in range(n_l) for a in per_layer(l)])

    def layer_weights(l):
        def wait(tag, which, after):
            groups = [n_w * l + i for i in which]
            return _gather_wait(f"gather_wait_{tag}{l}", groups, send_sems, recv_sems,
                                [srcs[g] for g in groups], [lands[g] for g in groups], after)

        def early(after):
            g_in, g_cw = wait("a", (0, 1), token if l == 0 else after)
            win = _win_to_internal(jnp.concatenate([g_in[k] for k in range(N_CHIPS)], axis=-1))
            cw = jnp.concatenate([g_cw[k] for k in range(N_CHIPS)], axis=-1)
            return _layer_params(l, win, cw, norm1_g, b_f, q_norm_g, k_norm_g, conv_b, conv_ln_g, conv_ln_b, norm2_g)

        def late(after):
            wo, w1, w2 = wait("b", (2, 3, 4), after)
            return dict(wo=wo, w1=w1, w2=w2)

        return early, late

    place = jnp.stack([2 * lax.axis_index("x") + lax.axis_index("y"), lax.axis_index("c")]).astype(jnp.int32)
    big_w = [w_in, w_o, w_mlp_in, w_mlp_out]
    pending = []

    def reduce(l):
        def group(tag, grads):
            which, gs = list(grads), list(grads.values())
            got = _half_swap(f"half_swap_{tag}{l}", gs)
            parts = _half_add(f"half_add_{tag}{l}", gs, got, place)
            send, recv, parts, lands, token = _exchange_start(f"exchange_start_{tag}{l}", parts)
            pending.append((f"{tag}{l}", l, which, send, recv, parts, lands))
            return token
        return group

    loss, dx, smalls = _local_step(x[0], loss_target[0], [layer_weights(l) for l in range(n_l)], reduce)
    sums = [lax.empty(w.shape, F32) for w in big_w]
    for tag, l, which, send, recv, parts, lands in pending:
        parts, lands = _exchange_wait(f"exchange_wait_{tag}", send, recv, parts, lands, dx)
        done = _chip_sum(f"chip_sum_{tag}", parts, lands, [sums[i] for i in which], place, l)
        for i, a in zip(which, done):
            sums[i] = a
    g_big = _half_fill(sums)
    big_m = [m_w_in, m_w_o, m_w_mlp_in, m_w_mlp_out]
    big_v = [v_w_in, v_w_o, v_w_mlp_in, v_w_mlp_out]
    g_big, d_big, nm_big, nv_big = _adamw(big_w, g_big, big_m, big_v)

    env = dict(norm1_g=(norm1_g, m_norm1_g, v_norm1_g), norm2_g=(norm2_g, m_norm2_g, v_norm2_g),
               conv_b=(conv_b, m_conv_b, v_conv_b), conv_ln_g=(conv_ln_g, m_conv_ln_g, v_conv_ln_g),
               conv_ln_b=(conv_ln_b, m_conv_ln_b, v_conv_ln_b), q_norm_g=(q_norm_g, m_q_norm_g, v_q_norm_g),
               k_norm_g=(k_norm_g, m_k_norm_g, v_k_norm_g), b_f=(b_f, m_b_f, v_b_f))
    g_dicts = [{nm: s[key] for nm, key in SMALL_KEYS.items()} for s in smalls]
    g_dicts[0]["spare"] = loss
    g_pack = _pack_small(g_dicts)
    packs = [_pack_small([{nm: env[nm][t][l] for nm in env} for l in range(n_l)]) for t in range(3)]
    pad_cw = lambda a: jnp.pad(a, ((0, 0), (0, CONV_HALO - CONV_TAPS), (0, 0)))
    cw_row0 = tuple(l * SMALL_ROWS_PAD + CONV_W_ROW0 for l in range(n_l))
    gs, ds, ms, vs, cg, cd, cm, cv = _small_allreduce_adamw(
        g_pack, packs[0], packs[1], packs[2], pad_cw(conv_w), pad_cw(m_conv_w), pad_cw(v_conv_w), cw_row0)

    def small_out(packed, conv):
        o = {nm: _unpack_small(packed, nm, sz) for nm, sz in SMALL_SIZES.items()}
        o["conv_w"] = conv[:, 0:CONV_TAPS, :]
        return o

    def ordered(small, big):
        return (small["norm1_g"], big[0], small["b_f"], small["q_norm_g"], small["k_norm_g"], small["conv_w"],
                small["conv_b"], small["conv_ln_g"], small["conv_ln_b"], big[1], small["norm2_g"], big[2], big[3])

    return (gs[LOSS_ROW, 0], dx[None],
            *ordered(small_out(gs, cg), g_big), *ordered(small_out(ds, cd), d_big),
            *ordered(small_out(ms, cm), nm_big), *ordered(small_out(vs, cv), nv_big))
```

```python
import jax
import jax.numpy as jnp
from jax import lax
from jax.experimental import pallas as pl
from jax.experimental.pallas import tpu as pltpu

F32 = jnp.float32
BF16 = jnp.bfloat16

D_MODEL = 1024
ATT_HEADS = 8
HEAD_DIM = 64
ATT_W = ATT_HEADS * HEAD_DIM
CONV_CH = 512
CONV_TAPS = 31
CONV_HALO = 32
D_FF = 4 * D_MODEL
N_IN = 3 * ATT_W + ATT_HEADS + 2 * CONV_CH
O_Q, O_K, O_V, O_A, O_G, O_F = 0, 512, 1024, 1536, 2048, 2560
N_INP = O_F + 128
EPS = 1e-6
QK_SCALE = 0.125

ADAM_LR = 0.001
ADAM_B1 = 0.9
ADAM_B2 = 0.999
ADAM_EPS = 1e-08
ADAM_WD = 0.01
ADAM_STEP = 10

N_CHIPS = 4
VMEM_LIMIT = 52 * 1024 * 1024
MESH = pl.DeviceIdType.MESH
ANY = pl.BlockSpec(memory_space=pl.ANY)


def _params(n_axes, **kw):
    return pltpu.CompilerParams(dimension_semantics=("arbitrary",) * n_axes,
                                vmem_limit_bytes=VMEM_LIMIT, **kw)


def _dot(a, b):
    return jnp.dot(a, b, preferred_element_type=F32)


def _dot_nt(a, b):
    return lax.dot_general(a, b, (((1,), (1,)), ((), ())), preferred_element_type=F32)


def _dot_tn(a, b):
    return lax.dot_general(a, b, (((0,), (0,)), ((), ())), preferred_element_type=F32)


def _split3(a):
    a1 = a.astype(BF16)
    r = a - a1.astype(F32)
    a2 = r.astype(BF16)
    a3 = (r - a2.astype(F32)).astype(BF16)
    return a1, a2, a3


def _dot_hi_r(a, b_exact):
    return sum(_dot(p, b_exact) for p in _split3(a))


def _head_sums(a, blockdiag):
    a1 = a.astype(BF16)
    a2 = (a - a1.astype(F32)).astype(BF16)
    return _dot(a1, blockdiag) + _dot(a2, blockdiag)


def _dot_hi_l(a_exact, b):
    return sum(_dot(a_exact, p) for p in _split3(b))


def _sigmoid(x):
    return 1.0 / (1.0 + jnp.exp(-x))


def _head_blockdiag():
    i = jnp.arange(ATT_W) // HEAD_DIM
    return (i[:, None] == i[None, :]).astype(BF16)


AUG_LANES = 8


def _aug_place(first):
    piece = jnp.arange(3 * 128)[:, None] // 128
    h = jnp.arange(3 * 128)[:, None] % 128
    lane = jnp.arange(ATT_W)[None, :]
    return ((h < ATT_HEADS) & (lane == 128 * (h // 2) + AUG_LANES * (h % 2) + first + piece)).astype(BF16)


def _aug_ones(first):
    lane = jnp.arange(ATT_W) % 128
    pos = lane % AUG_LANES
    return ((lane < 2 * AUG_LANES) & (pos >= first) & (pos < first + 3)).astype(F32).reshape(1, ATT_W)


def _head_rows():
    h = jnp.arange(2 * ATT_HEADS)[:, None]
    i = jnp.arange(ATT_W)[None, :] // HEAD_DIM
    return (h == i).astype(BF16)


def _head_fold():
    i = jnp.arange(ATT_W)[:, None] % HEAD_DIM
    j = jnp.arange(128)[None, :]
    return (i == j).astype(BF16)


def _head_pick():
    i = jnp.arange(ATT_W)[:, None]
    h = jnp.arange(128)[None, :]
    return (i == h * HEAD_DIM).astype(BF16)


def _tril(n):
    r = jnp.arange(n)
    return (r[:, None] >= r[None, :]).astype(BF16)


SUBLANES = 8


def _fill_row_shifts(buf_ref, shifts_ref, tm):
    n = tm + CONV_HALO - SUBLANES
    for b in range(1, SUBLANES):
        shifts_ref[b - 1, 0:n, :] = buf_ref[pl.ds(b, n), :]


def _row_shifted(buf_ref, shifts_ref, offset, rows, base=0):
    a, b = divmod(offset, SUBLANES)
    start = pl.multiple_of(base + SUBLANES * a, SUBLANES)
    if b == 0:
        return buf_ref[pl.ds(start, rows), :]
    return shifts_ref[b - 1, pl.ds(start, rows), :]


CONV_ROWS = 32


def _mixer_in_fwd(x, g1, win, qg, kg, bf, cw, cvb, lng, lnb):
    S = x.shape[0]
    TM = min(512, S)
    nb = S // TM

    def body(x_ref, g1_ref, win_ref, qg_ref, kg_ref, bf_ref, cw_ref, cvb_ref, lng_ref, lnb_ref,
             bd_ref, tri_ref, pq_ref, pk_ref, oq_ref, ok_ref,
             u1_ref, proj_ref, q_ref, k_ref, v_ref, qa_ref, ka_ref, h0_ref, h1_ref, h3_ref,
             carry_ref, hbuf_ref, hs_ref):
        i = pl.program_id(0)

        @pl.when(i == 0)
        def _():
            carry_ref[...] = jnp.zeros_like(carry_ref)
            hbuf_ref[0:CONV_HALO, :] = jnp.zeros((CONV_HALO, CONV_CH), F32)

        @pl.when(i > 0)
        def _():
            hbuf_ref[0:CONV_HALO, :] = hbuf_ref[TM:TM + CONV_HALO, :]

        xv = x_ref[...]
        r = lax.rsqrt(jnp.mean(xv * xv, axis=-1, keepdims=True) + EPS)
        u = (xv * r * g1_ref[...]).astype(BF16)
        u1_ref[...] = u
        proj_ref[...] = _dot(u, win_ref[...])

        def headnorm(raw, gain):
            ss = _head_sums(raw * raw, bd_ref[...]) * (1.0 / HEAD_DIM)
            return raw * lax.rsqrt(ss + EPS) * gain

        q_ref[...] = (headnorm(proj_ref[:, O_Q:O_Q + ATT_W], qg_ref[...]) * QK_SCALE).astype(BF16)
        k_ref[...] = headnorm(proj_ref[:, O_K:O_K + ATT_W], kg_ref[...]).astype(BF16)
        v_ref[...] = proj_ref[:, O_V:O_V + ATT_W].astype(BF16)

        zf = proj_ref[:, O_F:O_F + 128] + bf_ref[...]
        logf = jnp.minimum(zf, 0.0) - jnp.log(1.0 + jnp.exp(-jnp.abs(zf)))
        lane = lax.broadcasted_iota(jnp.int32, (TM, 128), 1)
        logf = jnp.where(lane < ATT_HEADS, logf, 0.0)
        c8 = _dot_hi_l(tri_ref[...], logf) + carry_ref[...]
        carry_ref[...] = c8[TM - 1:TM, :]
        pieces = jnp.concatenate(_split3(c8), axis=1)
        qa_ref[...] = (_dot(pieces, pq_ref[...]) + oq_ref[...]).astype(BF16)
        ka_ref[...] = (ok_ref[...] - _dot(pieces, pk_ref[...])).astype(BF16)

        h0 = proj_ref[:, O_A:O_A + CONV_CH] * _sigmoid(proj_ref[:, O_G:O_G + CONV_CH])
        h0_ref[...] = h0
        hbuf_ref[CONV_HALO:CONV_HALO + TM, :] = h0
        _fill_row_shifts(hbuf_ref, hs_ref, TM)
        acc = jnp.zeros((TM, CONV_CH), F32) + cvb_ref[...]
        for j in range(CONV_TAPS):
            acc = acc + cw_ref[j:j + 1, :] * _row_shifted(hbuf_ref, hs_ref, CONV_HALO - CONV_TAPS + 1 + j, TM)
        h1_ref[...] = acc
        mu = jnp.mean(acc, axis=-1, keepdims=True)
        d = acc - mu
        var = jnp.mean(d * d, axis=-1, keepdims=True)
        h2 = d * lax.rsqrt(var + EPS) * lng_ref[...] + lnb_ref[...]
        h3_ref[...] = (h2 * _sigmoid(h2)).astype(BF16)

    row = lambda w: pl.BlockSpec((TM, w), lambda i: (i, 0))
    full = lambda a: pl.BlockSpec(a.shape, lambda i: (0,) * a.ndim)
    ins = (x, g1, win, qg, kg, bf, cw, cvb, lng, lnb, _head_blockdiag(), _tril(TM),
           _aug_place(0), _aug_place(3), _aug_ones(3), _aug_ones(0))
    return pl.pallas_call(
        body, name="mixer_in_fwd", grid=(nb,),
        in_specs=[row(D_MODEL)] + [full(a) for a in ins[1:]],
        out_specs=[row(D_MODEL), row(N_INP), row(ATT_W), row(ATT_W), row(ATT_W), row(ATT_W), row(ATT_W),
                   row(CONV_CH), row(CONV_CH), row(CONV_CH)],
        out_shape=[jax.ShapeDtypeStruct((S, D_MODEL), BF16),
                   jax.ShapeDtypeStruct((S, N_INP), F32),
                   jax.ShapeDtypeStruct((S, ATT_W), BF16),
                   jax.ShapeDtypeStruct((S, ATT_W), BF16),
                   jax.ShapeDtypeStruct((S, ATT_W), BF16),
                   jax.ShapeDtypeStruct((S, ATT_W), BF16),
                   jax.ShapeDtypeStruct((S, ATT_W), BF16),
                   jax.ShapeDtypeStruct((S, CONV_CH), F32),
                   jax.ShapeDtypeStruct((S, CONV_CH), F32),
                   jax.ShapeDtypeStruct((S, CONV_CH), BF16)],
        scratch_shapes=[pltpu.VMEM((1, 128), F32), pltpu.VMEM((TM + CONV_HALO, CONV_CH), F32),
                        pltpu.VMEM((SUBLANES - 1, TM + CONV_HALO, CONV_CH), F32)],
        compiler_params=_params(1),
    )(*ins)


def _pair_heads(lo, alo, x, xa):
    z = jnp.zeros_like(x)
    return (jnp.concatenate([jnp.where(lo, x, z), jnp.where(alo, xa, z)], axis=1),
            jnp.concatenate([jnp.where(lo, z, x), jnp.where(alo, z, xa)], axis=1))


def _attn_fwd(q, qa, k, ka, v):
    S = q.shape[0]
    T = min(1024, S)
    nq = S // T

    def body(q_ref, qa_ref, k_ref, ka_ref, v_ref, o_ref, lse_ref):
        qi = pl.program_id(1)
        lane = lax.broadcasted_iota(jnp.int32, (T, 128), 1)
        lo = lane < HEAD_DIM
        qm = _pair_heads(lo, lane < AUG_LANES, q_ref[...], qa_ref[...])
        tril = (lax.broadcasted_iota(jnp.int32, (T, T), 0) >= lax.broadcasted_iota(jnp.int32, (T, T), 1))

        def step(kj, carry, masked):
            off = pl.multiple_of(kj * T, T)
            kb = jnp.concatenate([k_ref[pl.ds(off, T), :], ka_ref[pl.ds(off, T), :]], axis=1)
            vb = v_ref[pl.ds(off, T), :]
            new = []
            for h in range(2):
                m, l, acc = carry[3 * h:3 * h + 3]
                s = _dot_nt(qm[h], kb)
                if masked:
                    s = jnp.where(tril, s, -1e30)
                m_new = jnp.maximum(m, jnp.max(s, axis=-1, keepdims=True))
                alpha = jnp.exp(m - m_new)
                p = jnp.exp(s - m_new)
                l = alpha * l + jnp.sum(p, axis=-1, keepdims=True)
                acc = alpha * acc + _dot(p.astype(BF16), vb)
                new += [m_new, l, acc]
            return tuple(new)

        init = (jnp.full((T, 1), -1e30, F32), jnp.zeros((T, 1), F32), jnp.zeros((T, 128), F32)) * 2
        carry = lax.fori_loop(0, qi, lambda kj, c: step(kj, c, False), init)
        m0, l0, a0, m1, l1, a1 = step(qi, carry, True)
        o_ref[...] = jnp.where(lo, a0 / l0, a1 / l1).astype(BF16)
        lse_t = jnp.where(lo, m0 + jnp.log(l0), m1 + jnp.log(l1)).T
        lse_ref[0:1, :] = lse_t[0:1, :]
        lse_ref[1:2, :] = lse_t[HEAD_DIM:HEAD_DIM + 1, :]

    qblk = pl.BlockSpec((T, 128), lambda hp, qi: (qi, hp))
    seq = pl.BlockSpec((S, 128), lambda hp, qi: (0, hp))
    return pl.pallas_call(
        body, name="attn_fwd", grid=(ATT_HEADS // 2, nq),
        in_specs=[qblk, qblk, seq, seq, seq],
        out_specs=[qblk, pl.BlockSpec((None, 2, T), lambda hp, qi: (hp, 0, qi))],
        out_shape=[jax.ShapeDtypeStruct((S, ATT_W), BF16),
                   jax.ShapeDtypeStruct((ATT_HEADS // 2, 2, S), F32)],
        compiler_params=_params(2),
    )(q, qa, k, ka, v)


def _attn_bwd(q, qa, k, ka, v, do, lse, dlt):
    S = q.shape[0]
    T = min(512, S)
    nq = S // T

    def body(q_ref, qa_ref, do_ref, lse_ref, dlt_ref, k_ref, ka_ref, v_ref, dq_ref, dk_ref, dv_ref, rows_ref, cols_ref):
        kj = pl.program_id(1)

        @pl.when(kj == 0)
        def _():
            dq_ref[...] = jnp.zeros_like(dq_ref)
            rows_ref[...] = jnp.zeros_like(rows_ref)

        lane = lax.broadcasted_iota(jnp.int32, (T, 128), 1)
        lo = lane < HEAD_DIM
        alo = lane < AUG_LANES
        triu = (lax.broadcasted_iota(jnp.int32, (T, T), 0) <= lax.broadcasted_iota(jnp.int32, (T, T), 1))
        kb = k_ref[...]
        kaug = jnp.concatenate([kb, ka_ref[...]], axis=1)
        vb = v_ref[...]

        def step(qi, carry, masked):
            off = pl.multiple_of(qi * T, T)
            qb = q_ref[pl.ds(off, T), :]
            dob = do_ref[pl.ds(off, T), :]
            qm = _pair_heads(lo, alo, qb, qa_ref[pl.ds(off, T), :])
            zero = jnp.zeros_like(qb)
            new, dqs = [], []
            for h in range(2):
                dk_a, dv_a, dc_a = carry[3 * h:3 * h + 3]
                dom = jnp.where(lo, dob, zero) if h == 0 else jnp.where(lo, zero, dob)
                s = _dot_nt(kaug, qm[h])
                if masked:
                    s = jnp.where(triu, s, -1e30)
                p = jnp.exp(s - lse_ref[h:h + 1, pl.ds(off, T)])
                dp = _dot_nt(vb, dom)
                ds = p * (dp - dlt_ref[h:h + 1, pl.ds(off, T)])
                pb = p.astype(BF16)
                dsb = ds.astype(BF16)
                dv_a = dv_a + _dot(pb, dob)
                dk_a = dk_a + _dot(dsb, qb)
                dc_a = dc_a + jnp.sum(ds, axis=1, keepdims=True)
                dqs.append(_dot_tn(dsb, kb))
                rows_ref[h:h + 1, pl.ds(off, T)] += jnp.sum(ds, axis=0, keepdims=True)
                new += [dk_a, dv_a, dc_a]
            dq_ref[pl.ds(off, T), :] += jnp.where(lo, dqs[0], dqs[1])
            return tuple(new)

        init = (jnp.zeros((T, 128), F32), jnp.zeros((T, 128), F32), jnp.zeros((T, 1), F32)) * 2
        carry = step(kj, init, True)
        carry = lax.fori_loop(kj + 1, nq, lambda qi, c: step(qi, c, False), carry)
        dk_ref[...] = jnp.where(lo, carry[0], carry[3])
        dv_ref[...] = jnp.where(lo, carry[1], carry[4])
        cols_ref[...] = -jnp.where(lo, carry[2], carry[5])

    seq = pl.BlockSpec((S, 128), lambda hp, kj: (0, hp))
    rows = pl.BlockSpec((None, 2, S), lambda hp, kj: (hp, 0, 0))
    kblk = pl.BlockSpec((T, 128), lambda hp, kj: (kj, hp))
    return pl.pallas_call(
        body, name="attn_bwd", grid=(ATT_HEADS // 2, nq),
        in_specs=[seq, seq, seq, rows, rows, kblk, kblk, kblk],
        out_specs=[seq, kblk, kblk, rows, kblk],
        out_shape=[jax.ShapeDtypeStruct((S, ATT_W), F32), jax.ShapeDtypeStruct((S, ATT_W), F32),
                   jax.ShapeDtypeStruct((S, ATT_W), F32),
                   jax.ShapeDtypeStruct((ATT_HEADS // 2, 2, S), F32),
                   jax.ShapeDtypeStruct((S, ATT_W), F32)],
        compiler_params=_params(2),
    )(q, qa, do, lse, dlt, k, ka, v)


def _wo_spec(wo4):
    return pl.BlockSpec(wo4.shape, lambda i: (0, 0, 0))


def _wo_halves(wo_ref):
    half = N_CHIPS // 2
    return (wo_ref[0:half].reshape(ATT_W, D_MODEL), wo_ref[half:N_CHIPS].reshape(CONV_CH, D_MODEL))


def _mixer_out_fwd(x, att, h3, wo4, g2):
    S = x.shape[0]
    TM = min(512, S)

    def body(x_ref, att_ref, h3_ref, wo_ref, g2_ref, x2_ref, u2_ref):
        wa, wc = _wo_halves(wo_ref)
        x2 = x_ref[...] + _dot(att_ref[...], wa) + _dot(h3_ref[...], wc)
        x2_ref[...] = x2
        r = lax.rsqrt(jnp.mean(x2 * x2, axis=-1, keepdims=True) + EPS)
        u2_ref[...] = (x2 * r * g2_ref[...]).astype(BF16)

    row = lambda w: pl.BlockSpec((TM, w), lambda i: (i, 0))
    full = lambda a: pl.BlockSpec(a.shape, lambda i: (0,) * a.ndim)
    return pl.pallas_call(
        body, name="mixer_out_fwd", grid=(S // TM,),
        in_specs=[row(D_MODEL), row(ATT_W), row(CONV_CH), _wo_spec(wo4), full(g2)],
        out_specs=[row(D_MODEL), row(D_MODEL)],
        out_shape=[jax.ShapeDtypeStruct((S, D_MODEL), F32), jax.ShapeDtypeStruct((S, D_MODEL), BF16)],
        compiler_params=_params(1),
    )(x, att, h3, wo4, g2)


def _mlp_w_specs():
    return [pl.BlockSpec((None, D_MODEL, D_FF // N_CHIPS), lambda i, f: (f, 0, 0)),
            pl.BlockSpec((None, D_FF // N_CHIPS, D_MODEL), lambda i, f: (f, 0, 0))]


def _mlp_fwd(x2, u2, w1, w2, target=None):
    S = x2.shape[0]
    head = target is not None
    TM = min(1024, S)
    TF = 1024
    nf = D_FF // TF

    def body(*refs):
        x2_ref, u2_ref, w1_ref, w2_ref = refs[:4]
        x3_ref, z_ref, hh_ref = refs[4 + head:7 + head]
        i = pl.program_id(0)
        f = pl.program_id(1)

        @pl.when(f == 0)
        def _():
            x3_ref[...] = x2_ref[...]

        z = _dot(u2_ref[...], w1_ref[...])
        z_ref[...] = z
        zr = jnp.maximum(z, 0.0)
        hh = (zr * zr).astype(BF16)
        hh_ref[...] = hh
        x3_ref[...] += _dot(hh, w2_ref[...])

        if head:
            t_ref, loss_ref = refs[4], refs[8]

            @pl.when((i == 0) & (f == 0))
            def _():
                loss_ref[...] = jnp.zeros_like(loss_ref)

            @pl.when(f == nf - 1)
            def _():
                d = x3_ref[...] - t_ref[...]
                x3_ref[...] = d * (1.0 / D_MODEL)
                loss_ref[...] += jnp.sum(d * d)

    rows = pl.BlockSpec((TM, D_MODEL), lambda i, f: (i, 0))
    once = pl.BlockSpec((TM, D_MODEL), lambda i, f: (i, 0), pipeline_mode=pl.Buffered(1))
    tile = pl.BlockSpec((TM, TF), lambda i, f: (i, f))
    return pl.pallas_call(
        body, name="mlp_fwd_loss" if head else "mlp_fwd", grid=(S // TM, nf),
        in_specs=[once if head else rows, rows] + _mlp_w_specs() + [once] * head,
        out_specs=[rows, tile, tile] + [pl.BlockSpec((8, 128), lambda i, f: (0, 0))] * head,
        out_shape=[jax.ShapeDtypeStruct((S, D_MODEL), F32), jax.ShapeDtypeStruct((S, D_FF), F32),
                   jax.ShapeDtypeStruct((S, D_FF), BF16)] + [jax.ShapeDtypeStruct((8, 128), F32)] * head,
        compiler_params=_params(2),
    )(x2, u2, w1, w2, *([target] if head else []))


def _mlp_bwd(dx3, z, x2, g2, w1, w2):
    S = dx3.shape[0]
    TM = min(1024, S)
    TF = 1024
    nf = D_FF // TF

    def body(dx3_ref, z_ref, x2_ref, g2_ref, w1_ref, w2_ref, dz_ref, dx2_ref, dg2_ref, du2_ref):
        i = pl.program_id(0)
        f = pl.program_id(1)

        @pl.when((i == 0) & (f == 0))
        def _():
            dg2_ref[...] = jnp.zeros_like(dg2_ref)

        @pl.when(f == 0)
        def _():
            du2_ref[...] = jnp.zeros_like(du2_ref)

        dhh = _dot_nt(dx3_ref[...].astype(BF16), w2_ref[...])
        dz = (dhh * (2.0 * jnp.maximum(z_ref[...], 0.0))).astype(BF16)
        dz_ref[...] = dz
        du2_ref[...] += _dot_nt(dz, w1_ref[...])

        @pl.when(f == nf - 1)
        def _():
            x2 = x2_ref[...]
            r = lax.rsqrt(jnp.mean(x2 * x2, axis=-1, keepdims=True) + EPS)
            n = x2 * r
            du2 = du2_ref[...]
            t = du2 * g2_ref[...]
            dx2_ref[...] = dx3_ref[...] + r * (t - n * jnp.mean(t * n, axis=-1, keepdims=True))
            dg2_ref[0:1, :] += jnp.sum(du2 * n, axis=0, keepdims=True)

    rowi = pl.BlockSpec((TM, D_MODEL), lambda i, f: (i, 0))
    held = pl.BlockSpec((TM, D_MODEL), lambda i, f: (i, 0), pipeline_mode=pl.Buffered(1))
    return pl.pallas_call(
        body, name="mlp_bwd", grid=(S // TM, nf),
        in_specs=[held, pl.BlockSpec((TM, TF), lambda i, f: (i, f)), held,
                  pl.BlockSpec((1, D_MODEL), lambda i, f: (0, 0))] + _mlp_w_specs(),
        out_specs=[pl.BlockSpec((TM, TF), lambda i, f: (i, f)), rowi, pl.BlockSpec((8, D_MODEL), lambda i, f: (0, 0))],
        out_shape=[jax.ShapeDtypeStruct((S, D_FF), BF16), jax.ShapeDtypeStruct((S, D_MODEL), F32),
                   jax.ShapeDtypeStruct((8, D_MODEL), F32)],
        scratch_shapes=[pltpu.VMEM((TM, D_MODEL), F32)],
        compiler_params=_params(2),
    )(dx3, z, x2, g2, w1, w2)


def _matmul_tn(a, b, col_shards=1):
    S, I = a.shape
    J = b.shape[1]
    TI = min(I, 1024)
    TJ = 1024 if J % 1024 == 0 else 896
    TS = min(S, 1024)
    nk = S // TS
    per = J // col_shards // TJ

    def body(a_ref, b_ref, o_ref, acc_ref):
        k = pl.program_id(2)

        @pl.when(k == 0)
        def _():
            acc_ref[...] = jnp.zeros_like(acc_ref)

        acc_ref[...] += _dot_tn(a_ref[...].astype(BF16), b_ref[...].astype(BF16))

        @pl.when(k == nk - 1)
        def _():
            o_ref[...] = acc_ref[...].astype(BF16)

    return pl.pallas_call(
        body, name="matmul_tn", grid=(I // TI, J // TJ, nk),
        in_specs=[pl.BlockSpec((TS, TI), lambda i, j, k: (k, i)), pl.BlockSpec((TS, TJ), lambda i, j, k: (k, j))],
        out_specs=pl.BlockSpec((None, TI, TJ), lambda i, j, k: (j // per, i, j % per)),
        out_shape=jax.ShapeDtypeStruct((col_shards, I, J // col_shards), BF16),
        scratch_shapes=[pltpu.VMEM((TI, TJ), F32)],
        compiler_params=_params(3),
    )(a, b)


def _mixer_out_bwd(dx2, wo4, att, h1, lng, lnb):
    S = dx2.shape[0]
    TM = min(512, S)

    def body(dx2_ref, wo_ref, att_ref, h1_ref, lng_ref, lnb_ref, hr_ref, datt_ref, dlt_ref, dh1_ref, sm_ref):
        @pl.when(pl.program_id(0) == 0)
        def _():
            sm_ref[...] = jnp.zeros_like(sm_ref)

        dxb = dx2_ref[...].astype(BF16)
        wa, wc = _wo_halves(wo_ref)
        datt = _dot_nt(dxb, wa)
        datt_ref[...] = datt.astype(BF16)
        prod = datt * att_ref[...].astype(F32)
        dlt_ref[...] = sum(_dot_nt(hr_ref[...], piece) for piece in _split3(prod))[0:ATT_HEADS, :]
        dh3 = _dot_nt(dxb, wc)
        h1 = h1_ref[...]
        mu = jnp.mean(h1, axis=-1, keepdims=True)
        d = h1 - mu
        rstd = lax.rsqrt(jnp.mean(d * d, axis=-1, keepdims=True) + EPS)
        n = d * rstd
        h2 = n * lng_ref[...] + lnb_ref[...]
        sg = _sigmoid(h2)
        dh2 = dh3 * (sg * (1.0 + h2 * (1.0 - sg)))
        dn = dh2 * lng_ref[...]
        dh1 = rstd * (dn - jnp.mean(dn, axis=-1, keepdims=True) - n * jnp.mean(dn * n, axis=-1, keepdims=True))
        dh1_ref[...] = dh1
        sm_ref[0:1, :] += jnp.sum(dh2 * n, axis=0, keepdims=True)
        sm_ref[1:2, :] += jnp.sum(dh2, axis=0, keepdims=True)
        sm_ref[2:3, :] += jnp.sum(dh1, axis=0, keepdims=True)

    row = lambda w: pl.BlockSpec((TM, w), lambda i: (i, 0))
    full = lambda a: pl.BlockSpec(a.shape, lambda i: (0,) * a.ndim)
    hr = _head_rows()
    return pl.pallas_call(
        body, name="mixer_out_bwd", grid=(S // TM,),
        in_specs=[row(D_MODEL), _wo_spec(wo4), row(ATT_W), row(CONV_CH), full(lng), full(lnb), full(hr)],
        out_specs=[row(ATT_W), pl.BlockSpec((ATT_HEADS, TM), lambda i: (0, i)), row(CONV_CH),
                   pl.BlockSpec((8, CONV_CH), lambda i: (0, 0))],
        out_shape=[jax.ShapeDtypeStruct((S, ATT_W), BF16), jax.ShapeDtypeStruct((ATT_HEADS, S), F32),
                   jax.ShapeDtypeStruct((S, CONV_CH), F32), jax.ShapeDtypeStruct((8, CONV_CH), F32)],
        compiler_params=_params(1),
    )(dx2, wo4, att, h1, lng, lnb, hr)


def _conv_glu_bwd(dh1, h0, proj, cw):
    S = dh1.shape[0]
    TM = min(512, S)
    nb = S // TM
    lead = CONV_HALO - CONV_TAPS + 1

    def body(dh1_ref, dnx_ref, h0_ref, hpv_ref, a_ref, g_ref, cw_ref, dag_ref, dcw_ref,
             dbuf_ref, hbuf_ref, ds_ref, hs_ref, dh0_ref, dcw8_ref):
        i = pl.program_id(0)

        @pl.when(i == 0)
        def _():
            dcw8_ref[...] = jnp.zeros_like(dcw8_ref)

        dbuf_ref[0:TM, :] = dh1_ref[...]
        dbuf_ref[TM:TM + CONV_HALO, :] = jnp.where(i < nb - 1, dnx_ref[0:CONV_HALO, :], 0.0)
        hbuf_ref[0:CONV_HALO, :] = jnp.where(i > 0, hpv_ref[TM - CONV_HALO:TM, :], 0.0)
        hbuf_ref[CONV_HALO:CONV_HALO + TM, :] = h0_ref[...]
        _fill_row_shifts(dbuf_ref, ds_ref, TM)
        _fill_row_shifts(hbuf_ref, hs_ref, TM)

        def conv_rows(step, _):
            r0 = pl.multiple_of(step * CONV_ROWS, CONV_ROWS)
            dh1 = dbuf_ref[pl.ds(r0, CONV_ROWS), :]
            part = jnp.zeros((CONV_ROWS, CONV_CH), F32)
            for j in range(CONV_TAPS):
                part = part + cw_ref[j:j + 1, :] * _row_shifted(dbuf_ref, ds_ref, CONV_TAPS - 1 - j, CONV_ROWS, r0)
                prod = dh1 * _row_shifted(hbuf_ref, hs_ref, lead + j, CONV_ROWS, r0)
                dcw8_ref[j] += jnp.sum(prod.reshape(CONV_ROWS // SUBLANES, SUBLANES, CONV_CH), axis=0)
            dh0_ref[pl.ds(r0, CONV_ROWS), :] = part
            return 0

        lax.fori_loop(0, TM // CONV_ROWS, conv_rows, 0)

        @pl.when(i == nb - 1)
        def _():
            dcw_ref[...] = jnp.sum(dcw8_ref[...], axis=1)

        dh0 = dh0_ref[...]
        sg = _sigmoid(g_ref[...])
        dag_ref[:, 0:CONV_CH] = (dh0 * sg).astype(BF16)
        dag_ref[:, CONV_CH:2 * CONV_CH] = (dh0 * a_ref[...] * sg * (1.0 - sg)).astype(BF16)

    blk = lambda fn: pl.BlockSpec((TM, CONV_CH), fn)
    return pl.pallas_call(
        body, name="conv_glu_bwd", grid=(nb,),
        in_specs=[blk(lambda i: (i, 0)), blk(lambda i: (jnp.minimum(i + 1, nb - 1), 0)),
                  blk(lambda i: (i, 0)), blk(lambda i: (jnp.maximum(i - 1, 0), 0)),
                  blk(lambda i: (i, O_A // CONV_CH)), blk(lambda i: (i, O_G // CONV_CH)),
                  pl.BlockSpec(cw.shape, lambda i: (0, 0))],
        out_specs=[pl.BlockSpec((TM, 2 * CONV_CH), lambda i: (i, 0)), pl.BlockSpec((CONV_HALO, CONV_CH), lambda i: (0, 0))],
        out_shape=[jax.ShapeDtypeStruct((S, 2 * CONV_CH), BF16), jax.ShapeDtypeStruct((CONV_HALO, CONV_CH), F32)],
        scratch_shapes=[pltpu.VMEM((TM + CONV_HALO, CONV_CH), F32), pltpu.VMEM((TM + CONV_HALO, CONV_CH), F32),
                        pltpu.VMEM((SUBLANES - 1, TM + CONV_HALO, CONV_CH), F32),
                        pltpu.VMEM((SUBLANES - 1, TM + CONV_HALO, CONV_CH), F32),
                        pltpu.VMEM((TM, CONV_CH), F32), pltpu.VMEM((CONV_HALO, SUBLANES, CONV_CH), F32)],
        compiler_params=_params(1),
    )(dh1, dh1, h0, h0, proj, proj, cw)


def _mixer_in_bwd(x, dx2, proj, dq, dk, dv, dag, dct, drb, g1, win, qg, kg, bf):
    S = x.shape[0]
    TM = min(512, S)
    nb = S // TM

    def body(x_ref, dx2_ref, qr_ref, kr_ref, fz_ref, dq_ref, dk_ref, dv_ref, dag_ref, dct_ref, drb_ref,
             g1_ref, win_ref, qg_ref, kg_ref, bf_ref, bd_ref, fold_ref, triu_ref, pick_ref,
             dproj_ref, dx_ref, dg1_ref, sm_ref, carry_ref, gsum_ref):
        i = pl.program_id(0)

        @pl.when(i == 0)
        def _():
            carry_ref[...] = jnp.zeros_like(carry_ref)
            gsum_ref[...] = jnp.zeros_like(gsum_ref)
            dg1_ref[...] = jnp.zeros_like(dg1_ref)
            sm_ref[...] = jnp.zeros_like(sm_ref)

        def headnorm_bwd(raw, dy, gain, scale, row):
            rs = lax.rsqrt(_head_sums(raw * raw, bd_ref[...]) * (1.0 / HEAD_DIM) + EPS)
            n = raw * rs
            gsum_ref[row:row + 1, :] += jnp.sum(dy * n, axis=0, keepdims=True) * scale
            dn = dy * (gain * scale)
            return rs * (dn - n * (_head_sums(dn * n, bd_ref[...]) * (1.0 / HEAD_DIM)))

        dproj_ref[:, O_Q:O_Q + ATT_W] = headnorm_bwd(qr_ref[...], dq_ref[...], qg_ref[...], QK_SCALE, 0).astype(BF16)
        dproj_ref[:, O_K:O_K + ATT_W] = headnorm_bwd(kr_ref[...], dk_ref[...], kg_ref[...], 1.0, 1).astype(BF16)
        dproj_ref[:, O_V:O_V + ATT_W] = dv_ref[...].astype(BF16)
        dproj_ref[:, O_A:O_A + 2 * CONV_CH] = dag_ref[...]

        dc8 = jnp.concatenate([dct_ref[...], jnp.zeros((128 - ATT_HEADS, TM), F32)], axis=0).T
        dc8 = dc8 + _dot_hi_r(drb_ref[...], pick_ref[...])
        dlogf = _dot_hi_l(triu_ref[...], dc8) + carry_ref[...]
        carry_ref[...] = dlogf[0:1, :]
        df = dlogf * _sigmoid(-(fz_ref[...] + bf_ref[...]))
        dproj_ref[:, O_F:O_F + 128] = df.astype(BF16)
        sm_ref[2:3, :] += jnp.sum(df, axis=0, keepdims=True)

        du1 = _dot_nt(dproj_ref[...], win_ref[...])
        xv = x_ref[...]
        r = lax.rsqrt(jnp.mean(xv * xv, axis=-1, keepdims=True) + EPS)
        n1 = xv * r
        t = du1 * g1_ref[...]
        dx_ref[...] = dx2_ref[...] + r * (t - n1 * jnp.mean(t * n1, axis=-1, keepdims=True))
        dg1_ref[0:1, :] += jnp.sum(du1 * n1, axis=0, keepdims=True)

        @pl.when(i == nb - 1)
        def _():
            sm_ref[0:2, :] = _dot_hi_r(gsum_ref[0:8, :], fold_ref[...])[0:2, :]

    rev = lambda w, cb=0: pl.BlockSpec((TM, w), lambda i: (nb - 1 - i, cb))
    full = lambda a: pl.BlockSpec(a.shape, lambda i: (0,) * a.ndim)
    bd, fold, triu = _head_blockdiag(), _head_fold(), _tril(TM).T
    consts = (g1, win, qg, kg, bf, bd, fold, triu, _head_pick())
    return pl.pallas_call(
        body, name="mixer_in_bwd", grid=(nb,),
        in_specs=[rev(D_MODEL), rev(D_MODEL), rev(ATT_W, O_Q // ATT_W), rev(ATT_W, O_K // ATT_W), rev(128, O_F // 128),
                  rev(ATT_W), rev(ATT_W), rev(ATT_W), rev(2 * CONV_CH),
                  pl.BlockSpec((ATT_HEADS, TM), lambda i: (0, nb - 1 - i)), rev(ATT_W)] + [full(a) for a in consts],
        out_specs=[rev(N_INP), rev(D_MODEL), pl.BlockSpec((8, D_MODEL), lambda i: (0, 0)),
                   pl.BlockSpec((8, 128), lambda i: (0, 0))],
        out_shape=[jax.ShapeDtypeStruct((S, N_INP), BF16), jax.ShapeDtypeStruct((S, D_MODEL), F32),
                   jax.ShapeDtypeStruct((8, D_MODEL), F32), jax.ShapeDtypeStruct((8, 128), F32)],
        scratch_shapes=[pltpu.VMEM((1, 128), F32), pltpu.VMEM((8, ATT_W), F32)],
        compiler_params=_params(1),
    )(x, dx2, proj, proj, proj, dq, dk, dv, dag, dct, drb, *consts)


def _layer_fwd(x, early, late, target=None):
    p = early(x)
    u1, proj, q, k, v, qa, ka, h0, h1, h3 = _mixer_in_fwd(
        x, p["g1"], p["win"], p["qg"], p["kg"], p["bf"], p["cw"], p["cvb"], p["lng"], p["lnb"])
    att, lse = _attn_fwd(q, qa, k, ka, v)
    p = dict(p, **late(att))
    x2, u2 = _mixer_out_fwd(x, att, h3, p["wo"], p["g2"])
    x3, z, hh, *loss_acc = _mlp_fwd(x2, u2, p["w1"], p["w2"], target)
    saved = dict(x=x, u1=u1, proj=proj, q=q, k=k, v=v, qa=qa, ka=ka, h0=h0, h1=h1, h3=h3, att=att, lse=lse,
                 x2=x2, u2=u2, z=z, hh=hh)
    return (x3 if target is None else (x3, loss_acc[0])), saved, p


def _tie(a, token):
    return a if token is None else a + token[0:1, 0:1]


def _layer_bwd(dx3, s, p, reduce):
    dz, dx2, dg2 = _mlp_bwd(dx3, s["z"], s["x2"], p["g2"], p["w1"], p["w2"])
    g_w2 = _matmul_tn(s["hh"], dx3)
    g_w1 = _matmul_tn(s["u2"], dz, col_shards=N_CHIPS)
    token = reduce("a", {2: g_w1, 3: g_w2.reshape(N_CHIPS, D_FF // N_CHIPS, D_MODEL)})
    datt, dlt, dh1, sm_c = _mixer_out_bwd(dx2, p["wo"], s["att"], s["h1"], _tie(p["lng"], token), p["lnb"])
    g_wo = jnp.concatenate([_matmul_tn(s["att"], dx2)[0], _matmul_tn(s["h3"], dx2)[0]], axis=0)
    dag, dcw = _conv_glu_bwd(dh1, s["h0"], s["proj"], p["cw"])
    dq, dk, dv, dc4, drb = _attn_bwd(s["q"], s["qa"], s["k"], s["ka"], s["v"], datt, s["lse"],
                                     dlt.reshape(ATT_HEADS // 2, 2, dlt.shape[1]))
    dct = dc4.reshape(ATT_HEADS, dc4.shape[2])
    dproj, dx, dg1, sm_a = _mixer_in_bwd(s["x"], dx2, s["proj"], dq, dk, dv, dag, dct, drb,
                                         p["g1"], p["win"], p["qg"], p["kg"], p["bf"])
    g_win = _win_to_global(_matmul_tn(s["u1"], dproj)[0])
    g_win = g_win.reshape(D_MODEL, N_CHIPS, N_IN // N_CHIPS).transpose(1, 0, 2)
    token = reduce("b", {0: g_win, 1: g_wo.reshape(N_CHIPS, D_MODEL // N_CHIPS, D_MODEL)})
    small = dict(g1=dg1[0], g2=dg2[0], lng=sm_c[0], lnb=sm_c[1], cvb=sm_c[2], cw=dcw[0:CONV_TAPS],
                 qg=sm_a[0, 0:HEAD_DIM], kg=sm_a[1, 0:HEAD_DIM], bf=sm_a[2, 0:ATT_HEADS])
    return dx, small, token


def _local_step(x, target, weights, reduce):
    saved, layers = [], []
    h = x
    for l, (early, late) in enumerate(weights):
        h, s, p = _layer_fwd(h, early, late, target if l == len(weights) - 1 else None)
        saved.append(s)
        layers.append(p)
    dy, loss_acc = h
    loss = loss_acc[0, 0] * (0.5 / D_MODEL)
    smalls = []
    d, token = dy, None
    for l in reversed(range(len(layers))):
        d, small, token = _layer_bwd(d, saved[l], dict(layers[l], g2=_tie(layers[l]["g2"], token)), reduce(l))
        smalls.append(small)
    return loss, d, smalls[::-1]


def _win_to_internal(w):
    pad = jnp.zeros(w.shape[:-1] + (N_INP - N_IN,), w.dtype)
    return jnp.concatenate([w[..., :1536], w[..., 1544:], w[..., 1536:1544], pad], axis=-1)


def _win_to_global(g):
    return jnp.concatenate([g[..., :1536], g[..., O_F:O_F + ATT_HEADS], g[..., 1536:O_F]], axis=-1)


def _layer_params(l, win, cw, norm1_g, b_f, q_norm_g, k_norm_g, conv_b, conv_ln_g, conv_ln_b, norm2_g):
    row = lambda a: a.reshape(1, -1)
    return dict(
        win=win, cw=jnp.pad(cw, ((0, CONV_HALO - CONV_TAPS), (0, 0))),
        g1=row(norm1_g[l]), g2=row(norm2_g[l]),
        qg=row(jnp.tile(q_norm_g[l], ATT_HEADS)), kg=row(jnp.tile(k_norm_g[l], ATT_HEADS)),
        bf=row(jnp.pad(b_f[l], (0, 128 - ATT_HEADS))),
        cvb=row(conv_b[l]), lng=row(conv_ln_g[l]), lnb=row(conv_ln_b[l]))


def _place():
    x, y, c = lax.axis_index("x"), lax.axis_index("y"), lax.axis_index("c")
    chips = [(1 - x, y), (x, 1 - y), (1 - x, 1 - y)]
    return x, y, c, chips


HBM = pl.BlockSpec(memory_space=pltpu.HBM)
SEM = pl.BlockSpec(memory_space=pltpu.SEMAPHORE)
GATHER_PEERS = N_CHIPS


def _gather_peers():
    x, y, c, chips = _place()
    return [(*chip, c) for chip in chips] + [(x, y, 1 - c)], [2 * px + py for px, py in chips] + [2 * x + y]


def _gather_start(srcs):
    n = len(srcs)

    def body(*refs):
        ins, lands = refs[:n], refs[n:2 * n]
        send_sems, recv_sems, token = refs[2 * n], refs[2 * n + 1], refs[-1]
        me = 2 * lax.axis_index("x") + lax.axis_index("y")
        peers, _ = _gather_peers()
        for g in range(n):
            for j, to in enumerate(peers):
                pltpu.make_async_remote_copy(src_ref=ins[g], dst_ref=lands[g].at[me],
                                             send_sem=send_sems.at[GATHER_PEERS * g + j],
                                             recv_sem=recv_sems.at[GATHER_PEERS * g + j],
                                             device_id=to, device_id_type=MESH).start()
        token[...] = jnp.zeros_like(token)

    lands = [lax.empty((N_CHIPS,) + a.shape, a.dtype) for a in srcs]
    outs = pl.pallas_call(
        body, name="gather_start",
        in_specs=[HBM] * (2 * n),
        out_specs=[SEM, SEM] + [HBM] * (2 * n) + [pl.BlockSpec(memory_space=pltpu.VMEM)],
        out_shape=[pltpu.SemaphoreType.DMA((GATHER_PEERS * n,)), pltpu.SemaphoreType.DMA((GATHER_PEERS * n,))]
        + [pltpu.HBM(a.shape, a.dtype) for a in srcs] + [pltpu.HBM(a.shape, a.dtype) for a in lands]
        + [jax.ShapeDtypeStruct((8, 128), F32)],
        input_output_aliases={i: 2 + i for i in range(2 * n)},
        compiler_params=pltpu.CompilerParams(has_side_effects=pltpu.SideEffectType.DATAFLOW_SIDE_EFFECTING),
    )(*[pltpu.with_memory_space_constraint(a, pltpu.HBM) for a in srcs],
      *[pltpu.with_memory_space_constraint(a, pltpu.HBM) for a in lands])
    return outs[0], outs[1], list(outs[2:2 + n]), list(outs[2 + n:2 + 2 * n]), outs[-1]


def _gather_wait(name, groups, send_sems, recv_sems, srcs, lands, after):
    k = len(groups)

    def body(*refs):
        ins, lnd = refs[:k], refs[k:2 * k]
        ssem, rsem = refs[2 * k], refs[2 * k + 1]
        peers, slots = _gather_peers()
        for i, g in enumerate(groups):
            for j, to in enumerate(peers):
                cp = pltpu.make_async_remote_copy(src_ref=ins[i], dst_ref=lnd[i].at[slots[j]],
                                                  send_sem=ssem.at[GATHER_PEERS * g + j],
                                                  recv_sem=rsem.at[GATHER_PEERS * g + j],
                                                  device_id=to, device_id_type=MESH)
                cp.wait_send()
                cp.wait_recv()

    outs = pl.pallas_call(
        body, name=name,
        in_specs=[HBM] * (2 * k) + [SEM, SEM, ANY],
        out_specs=[HBM] * (2 * k),
        out_shape=[pltpu.HBM(a.shape, a.dtype) for a in srcs] + [pltpu.HBM(a.shape, a.dtype) for a in lands],
        input_output_aliases={i: i for i in range(2 * k)},
        compiler_params=pltpu.CompilerParams(has_side_effects=pltpu.SideEffectType.DATAFLOW_SIDE_EFFECTING),
    )(*srcs, *lands, send_sems, recv_sems, after)
    return list(outs[k:])


REDUCE_STEPS = 8


def _row_half(a):
    return a.shape[-2] // 2


def _half_swap(name, gs):
    n = len(gs)

    def body(*refs):
        ins, outs = refs[:n], refs[n:2 * n]
        send_sems, recv_sems = refs[2 * n:]
        x, y, c, _ = _place()
        copies = []
        for a in range(n):
            h = _row_half(gs[a])
            copies.append(pltpu.make_async_remote_copy(
                src_ref=ins[a].at[:, pl.ds((1 - c) * h, h), :], dst_ref=outs[a], send_sem=send_sems.at[a],
                recv_sem=recv_sems.at[a], device_id=(x, y, 1 - c), device_id_type=MESH))
        for cp in copies:
            cp.start()
        for cp in copies:
            cp.wait()

    return pl.pallas_call(
        body, name=name,
        in_specs=[ANY] * n, out_specs=[ANY] * n,
        out_shape=[jax.ShapeDtypeStruct((N_CHIPS, _row_half(a), a.shape[-1]), a.dtype) for a in gs],
        scratch_shapes=[pltpu.SemaphoreType.DMA((n,)), pltpu.SemaphoreType.DMA((n,))],
        compiler_params=pltpu.CompilerParams(has_side_effects=True),
    )(*gs)


def _half_specs(gs, row_block):
    tiles = [_row_half(a) // REDUCE_STEPS for a in gs]
    return [pl.BlockSpec((N_CHIPS, t, a.shape[-1]), lambda i, p: (0, row_block(i, p), 0)) for a, t in zip(gs, tiles)]


def _half_add(name, gs, got, place):
    n = len(gs)

    def body(place_ref, *refs):
        own, theirs, outs = refs[:n], refs[n:2 * n], refs[2 * n:]
        for a in range(n):
            outs[a][...] = (own[a][...].astype(F32) + theirs[a][...].astype(F32)).astype(BF16)

    plain = _half_specs(gs, lambda i, p: i)
    return pl.pallas_call(
        body, name=name,
        grid_spec=pltpu.PrefetchScalarGridSpec(
            num_scalar_prefetch=1, grid=(REDUCE_STEPS,),
            in_specs=_half_specs(gs, lambda i, p: p[1] * REDUCE_STEPS + i) + plain, out_specs=plain),
        out_shape=[jax.ShapeDtypeStruct(a.shape, BF16) for a in got],
        compiler_params=_params(1),
    )(place, *gs, *got)


def _exchange_copies(parts, lands, send_sems, recv_sems):
    x, y, c, chips = _place()
    return [pltpu.make_async_remote_copy(src_ref=parts[a].at[2 * px + py], dst_ref=lands[a].at[j],
                                         send_sem=send_sems.at[3 * a + j], recv_sem=recv_sems.at[3 * a + j],
                                         device_id=(px, py, c), device_id_type=MESH)
            for a in range(len(parts)) for j, (px, py) in enumerate(chips)]


def _exchange_start(name, parts):
    n = len(parts)

    def body(*refs):
        _ = [cp.start() for cp in _exchange_copies(refs[:n], refs[n:2 * n], refs[2 * n], refs[2 * n + 1])]
        refs[-1][...] = jnp.zeros_like(refs[-1])

    lands = [lax.empty((N_CHIPS - 1,) + a.shape[1:], a.dtype) for a in parts]
    outs = pl.pallas_call(
        body, name=name,
        in_specs=[HBM] * (2 * n),
        out_specs=[SEM, SEM] + [HBM] * (2 * n) + [pl.BlockSpec(memory_space=pltpu.VMEM)],
        out_shape=[pltpu.SemaphoreType.DMA((3 * n,)), pltpu.SemaphoreType.DMA((3 * n,))]
        + [pltpu.HBM(a.shape, a.dtype) for a in parts] + [pltpu.HBM(a.shape, a.dtype) for a in lands]
        + [jax.ShapeDtypeStruct((8, 128), F32)],
        input_output_aliases={i: 2 + i for i in range(2 * n)},
        compiler_params=pltpu.CompilerParams(has_side_effects=pltpu.SideEffectType.DATAFLOW_SIDE_EFFECTING),
    )(*[pltpu.with_memory_space_constraint(a, pltpu.HBM) for a in parts],
      *[pltpu.with_memory_space_constraint(a, pltpu.HBM) for a in lands])
    return outs[0], outs[1], list(outs[2:2 + n]), list(outs[2 + n:2 + 2 * n]), outs[-1]


def _exchange_wait(name, send_sems, recv_sems, parts, lands, after):
    n = len(parts)

    def body(*refs):
        for cp in _exchange_copies(refs[:n], refs[n:2 * n], refs[2 * n], refs[2 * n + 1]):
            cp.wait_send()
            cp.wait_recv()

    outs = pl.pallas_call(
        body, name=name,
        in_specs=[HBM] * (2 * n) + [SEM, SEM, ANY],
        out_specs=[HBM] * (2 * n),
        out_shape=[pltpu.HBM(a.shape, a.dtype) for a in parts] + [pltpu.HBM(a.shape, a.dtype) for a in lands],
        input_output_aliases={i: i for i in range(2 * n)},
        compiler_params=pltpu.CompilerParams(has_side_effects=pltpu.SideEffectType.DATAFLOW_SIDE_EFFECTING),
    )(*parts, *lands, send_sems, recv_sems, after)
    return list(outs[:n]), list(outs[n:])


def _chip_sum(name, parts, lands, sums, place, layer):
    n = len(parts)
    tiles = [a.shape[-2] // REDUCE_STEPS for a in parts]

    def body(place_ref, *refs):
        own, got, outs = refs[:n], refs[n:2 * n], refs[3 * n:]
        for a in range(n):
            tot = own[a][...].astype(F32)
            for j in range(N_CHIPS - 1):
                tot = tot + got[a][j].astype(F32)
            outs[a][...] = tot

    own_specs = [pl.BlockSpec((None, t, a.shape[-1]), lambda i, p: (p[0], i, 0)) for a, t in zip(parts, tiles)]
    got_specs = [pl.BlockSpec((N_CHIPS - 1, t, a.shape[-1]), lambda i, p: (0, i, 0)) for a, t in zip(parts, tiles)]
    out_specs = [pl.BlockSpec((None, t, a.shape[-1]), lambda i, p: (layer, p[1] * REDUCE_STEPS + i, 0))
                 for a, t in zip(parts, tiles)]
    return pl.pallas_call(
        body, name=name,
        grid_spec=pltpu.PrefetchScalarGridSpec(num_scalar_prefetch=1, grid=(REDUCE_STEPS,),
                                               in_specs=own_specs + got_specs + [ANY] * n, out_specs=out_specs),
        out_shape=[jax.ShapeDtypeStruct(a.shape, F32) for a in sums],
        input_output_aliases={1 + 2 * n + a: a for a in range(n)},
        compiler_params=_params(1),
    )(place, *parts, *lands, *sums)


def _half_fill(sums):
    n = len(sums)

    def body(*refs):
        ins, outs = refs[:n], refs[n:2 * n]
        send_sems, recv_sems = refs[2 * n:]
        x, y, c, _ = _place()
        copies = []
        for a in range(n):
            h = _row_half(sums[a])
            copies.append(pltpu.make_async_remote_copy(
                src_ref=ins[a].at[:, pl.ds(c * h, h), :], dst_ref=outs[a].at[:, pl.ds(c * h, h), :],
                send_sem=send_sems.at[a], recv_sem=recv_sems.at[a], device_id=(x, y, 1 - c), device_id_type=MESH))
        for cp in copies:
            cp.start()
        for a in range(n):
            h = _row_half(sums[a])
            theirs = outs[a].at[:, pl.ds((1 - c) * h, h), :]
            pltpu.make_async_remote_copy(src_ref=theirs, dst_ref=theirs, send_sem=send_sems.at[a], recv_sem=recv_sems.at[a],
                                         device_id=(x, y, 1 - c), device_id_type=MESH).wait_recv()
        for cp in copies:
            cp.wait_send()

    return pl.pallas_call(
        body, name="half_fill",
        in_specs=[ANY] * n, out_specs=[ANY] * n,
        out_shape=[jax.ShapeDtypeStruct(a.shape, a.dtype) for a in sums],
        input_output_aliases={a: a for a in range(n)},
        scratch_shapes=[pltpu.SemaphoreType.DMA((n,)), pltpu.SemaphoreType.DMA((n,))],
        compiler_params=pltpu.CompilerParams(has_side_effects=True),
    )(*sums)


def _adamw_math(w, g, m, v):
    m = ADAM_B1 * m + (1.0 - ADAM_B1) * g
    v = ADAM_B2 * v + (1.0 - ADAM_B2) * (g * g)
    m_hat = m / (1.0 - ADAM_B1 ** ADAM_STEP)
    v_hat = v / (1.0 - ADAM_B2 ** ADAM_STEP)
    delta = -ADAM_LR * (m_hat / (jnp.sqrt(v_hat) + ADAM_EPS) + ADAM_WD * w)
    return delta, m, v


def _adamw(ws, gs, ms, vs):
    n = len(ws)
    steps = 16
    tiles = [a.shape[-2] // steps for a in ws]

    def body(*refs):
        w_r, g_r, m_r, v_r = refs[:n], refs[n:2 * n], refs[2 * n:3 * n], refs[3 * n:4 * n]
        g_o, d_o, m_o, v_o = refs[4 * n:5 * n], refs[5 * n:6 * n], refs[6 * n:7 * n], refs[7 * n:]
        for a in range(n):
            g = g_r[a][...]
            d, m, v = _adamw_math(w_r[a][...], g, m_r[a][...], v_r[a][...])
            g_o[a][...] = g
            d_o[a][...] = d
            m_o[a][...] = m
            v_o[a][...] = v

    specs = [pl.BlockSpec((2, t, a.shape[-1]), lambda i: (0, i, 0)) for a, t in zip(ws, tiles)]
    outs = pl.pallas_call(
        body, name="adamw", grid=(steps,),
        in_specs=specs * 4, out_specs=specs * 4,
        out_shape=[jax.ShapeDtypeStruct(a.shape, F32) for a in ws] * 4,
        compiler_params=_params(1),
    )(*ws, *gs, *ms, *vs)
    return outs[:n], outs[n:2 * n], outs[2 * n:3 * n], outs[3 * n:]


SMALL_W = 512


def _small_allreduce_adamw(g, w, m, v, cw_w, cw_m, cw_v, cw_row0):
    R = g.shape[0]
    n_l = cw_w.shape[0]

    def body(g_ref, w_ref, m_ref, v_ref, cww_ref, cwm_ref, cwv_ref,
             gs_ref, d_ref, mo_ref, vo_ref, cg_ref, cd_ref, cmo_ref, cvo_ref,
             slots_ref, send_sems, recv_sems):
        x, y, c, _ = _place()
        me = 4 * x + 2 * y + c
        slots_ref[me] = g_ref[...]
        sends = []
        for d in range(1, 8):
            px, py, pc = x ^ (d >> 2), y ^ ((d >> 1) & 1), c ^ (d & 1)
            cp = pltpu.make_async_remote_copy(src_ref=g_ref, dst_ref=slots_ref.at[me], send_sem=send_sems.at[d - 1],
                                              recv_sem=recv_sems.at[d - 1], device_id=(px, py, pc), device_id_type=MESH)
            cp.start()
            sends.append(cp)
        for d in range(1, 8):
            px, py, pc = x ^ (d >> 2), y ^ ((d >> 1) & 1), c ^ (d & 1)
            slot = slots_ref.at[4 * px + 2 * py + pc]
            pltpu.make_async_remote_copy(src_ref=slot, dst_ref=slot, send_sem=send_sems.at[d - 1],
                                         recv_sem=recv_sems.at[d - 1], device_id=(px, py, pc),
                                         device_id_type=MESH).wait_recv()
        for cp in sends:
            cp.wait_send()
        tot = slots_ref[0]
        for k in range(1, 8):
            tot = tot + slots_ref[k]
        gs_ref[...] = tot
        dl, mn, vn = _adamw_math(w_ref[...], tot, m_ref[...], v_ref[...])
        d_ref[...] = dl
        mo_ref[...] = mn
        vo_ref[...] = vn
        chip = 2 * x + y
        for l in range(n_l):
            rows = tot[cw_row0[l]:cw_row0[l] + CONV_HALO, :]
            mine = rows[:, 0:128]
            for k in range(1, N_CHIPS):
                mine = jnp.where(chip == k, rows[:, 128 * k:128 * (k + 1)], mine)
            cg_ref[l] = mine
            dl, mn, vn = _adamw_math(cww_ref[l], mine, cwm_ref[l], cwv_ref[l])
            cd_ref[l] = dl
            cmo_ref[l] = mn
            cvo_ref[l] = vn

    vm = pl.BlockSpec(memory_space=pltpu.VMEM)
    small = jax.ShapeDtypeStruct((R, SMALL_W), F32)
    conv = jax.ShapeDtypeStruct(cw_w.shape, F32)
    return pl.pallas_call(
        body, name="small_allreduce_adamw",
        in_specs=[vm] * 7, out_specs=[vm] * 8,
        out_shape=[small] * 4 + [conv] * 4,
        scratch_shapes=[pltpu.VMEM((8, R, SMALL_W), F32), pltpu.SemaphoreType.DMA((7,)), pltpu.SemaphoreType.DMA((7,))],
        compiler_params=pltpu.CompilerParams(has_side_effects=True, vmem_limit_bytes=VMEM_LIMIT),
    )(g, w, m, v, cw_w, cw_m, cw_v)


SMALL_LAYOUT = (("conv_w", CONV_HALO), ("norm1_g", 2), ("norm2_g", 2), ("conv_b", 1), ("conv_ln_g", 1),
                ("conv_ln_b", 1), ("q_norm_g", 1), ("k_norm_g", 1), ("b_f", 1))
SMALL_ROWS = sum(r for _, r in SMALL_LAYOUT)
SMALL_ROWS_PAD = 48
LOSS_ROW = SMALL_ROWS


def _pack_small(per_layer):
    flat = []
    for d in per_layer:
        for name, r in SMALL_LAYOUT:
            n = r * SMALL_W
            a = d.get(name)
            if a is None:
                flat.append(jnp.zeros((n,), F32))
                continue
            flat.append(a.reshape(-1))
            if a.size < n:
                flat.append(jnp.zeros((n - a.size,), F32))
        spare = (SMALL_ROWS_PAD - SMALL_ROWS) * SMALL_W
        if "spare" in d:
            flat.append(d["spare"].reshape(-1))
            spare -= d["spare"].size
        flat.append(jnp.zeros((spare,), F32))
    return jnp.concatenate(flat).reshape(-1, SMALL_W)


def _unpack_small(packed, name, size):
    row0 = 0
    for nm, r in SMALL_LAYOUT:
        if nm == name:
            break
        row0 += r
    per_layer = packed.reshape(-1, SMALL_ROWS_PAD * SMALL_W)
    return per_layer[:, row0 * SMALL_W:row0 * SMALL_W + size]


SMALL_SIZES = dict(norm1_g=D_MODEL, norm2_g=D_MODEL, conv_b=CONV_CH, conv_ln_g=CONV_CH, conv_ln_b=CONV_CH,
                   q_norm_g=HEAD_DIM, k_norm_g=HEAD_DIM, b_f=ATT_HEADS)
SMALL_KEYS = dict(norm1_g="g1", norm2_g="g2", conv_b="cvb", conv_ln_g="lng", conv_ln_b="lnb",
                  q_norm_g="qg", k_norm_g="kg", b_f="bf", conv_w="cw")
CONV_W_ROW0 = 0


def kernel(x, norm1_g, w_in, b_f, q_norm_g, k_norm_g, conv_w, conv_b, conv_ln_g, conv_ln_b, w_o, norm2_g, w_mlp_in, w_mlp_out, loss_target, m_norm1_g, m_w_in, m_b_f, m_q_norm_g, m_k_norm_g, m_conv_w, m_conv_b, m_conv_ln_g, m_conv_ln_b, m_w_o, m_norm2_g, m_w_mlp_in, m_w_mlp_out, v_norm1_g, v_w_in, v_b_f, v_q_norm_g, v_k_norm_g, v_conv_w, v_conv_b, v_conv_ln_g, v_conv_ln_b, v_w_o, v_norm2_g, v_w_mlp_in, v_w_mlp_out):
    n_l = w_in.shape[0]

    per_layer = lambda l: [w_in[l].astype(BF16), conv_w[l], w_o[l].astype(BF16), w_mlp_in[l].astype(BF16),
                           w_mlp_out[l].astype(BF16)]
    n_w = len(per_layer(0))
    send_sems, recv_sems, srcs, lands, token = _gather_start([a for l in range(n_l) for a in per_layer(l)])

    def layer_weights(l):
        def wait(tag, which, after):
            groups = [n_w * l + i for i in which]
            return _gather_wait(f"gather_wait_{tag}{l}", groups, send_sems, recv_sems,
                                [srcs[g] for g in groups], [lands[g] for g in groups], after)

        def early(after):
            g_in, g_cw = wait("a", (0, 1), token if l == 0 else after)
            win = _win_to_internal(jnp.concatenate([g_in[k] for k in range(N_CHIPS)], axis=-1))
            cw = jnp.concatenate([g_cw[k] for k in range(N_CHIPS)], axis=-1)
            return _layer_params(l, win, cw, norm1_g, b_f, q_norm_g, k_norm_g, conv_b, conv_ln_g, conv_ln_b, norm2_g)

        def late(after):
            wo, w1, w2 = wait("b", (2, 3, 4), after)
            return dict(wo=wo, w1=w1, w2=w2)

        return early, late

    place = jnp.stack([2 * lax.axis_index("x") + lax.axis_index("y"), lax.axis_index("c")]).astype(jnp.int32)
    big_w = [w_in, w_o, w_mlp_in, w_mlp_out]
    pending = []

    def reduce(l):
        def group(tag, grads):
            which, gs = list(grads), list(grads.values())
            got = _half_swap(f"half_swap_{tag}{l}", gs)
            parts = _half_add(f"half_add_{tag}{l}", gs, got, place)
            send, recv, parts, lands, token = _exchange_start(f"exchange_start_{tag}{l}", parts)
            pending.append((f"{tag}{l}", l, which, send, recv, parts, lands))
            return token
        return group

    loss, dx, smalls = _local_step(x[0], loss_target[0], [layer_weights(l) for l in range(n_l)], reduce)
    sums = [lax.empty(w.shape, F32) for w in big_w]
    for tag, l, which, send, recv, parts, lands in pending:
        parts, lands = _exchange_wait(f"exchange_wait_{tag}", send, recv, parts, lands, dx)
        done = _chip_sum(f"chip_sum_{tag}", parts, lands, [sums[i] for i in which], place, l)
        for i, a in zip(which, done):
            sums[i] = a
    g_big = _half_fill(sums)
    big_m = [m_w_in, m_w_o, m_w_mlp_in, m_w_mlp_out]
    big_v = [v_w_in, v_w_o, v_w_mlp_in, v_w_mlp_out]
    g_big, d_big, nm_big, nv_big = _adamw(big_w, g_big, big_m, big_v)

    env = dict(norm1_g=(norm1_g, m_norm1_g, v_norm1_g), norm2_g=(norm2_g, m_norm2_g, v_norm2_g),
               conv_b=(conv_b, m_conv_b, v_conv_b), conv_ln_g=(conv_ln_g, m_conv_ln_g, v_conv_ln_g),
               conv_ln_b=(conv_ln_b, m_conv_ln_b, v_conv_ln_b), q_norm_g=(q_norm_g, m_q_norm_g, v_q_norm_g),
               k_norm_g=(k_norm_g, m_k_norm_g, v_k_norm_g), b_f=(b_f, m_b_f, v_b_f))
    g_dicts = [{nm: s[key] for nm, key in SMALL_KEYS.items()} for s in smalls]
    g_dicts[0]["spare"] = loss
    g_pack = _pack_small(g_dicts)
    packs = [_pack_small([{nm: env[nm][t][l] for nm in env} for l in range(n_l)]) for t in range(3)]
    pad_cw = lambda a: jnp.pad(a, ((0, 0), (0, CONV_HALO - CONV_TAPS), (0, 0)))
    cw_row0 = tuple(l * SMALL_ROWS_PAD + CONV_W_ROW0 for l in range(n_l))
    gs, ds, ms, vs, cg, cd, cm, cv = _small_allreduce_adamw(
        g_pack, packs[0], packs[1], packs[2], pad_cw(conv_w), pad_cw(m_conv_w), pad_cw(v_conv_w), cw_row0)

    def small_out(packed, conv):
        o = {nm: _unpack_small(packed, nm, sz) for nm, sz in SMALL_SIZES.items()}
        o["conv_w"] = conv[:, 0:CONV_TAPS, :]
        return o

    def ordered(small, big):
        return (small["norm1_g"], big[0], small["b_f"], small["q_norm_g"], small["k_norm_g"], small["conv_w"],
                small["conv_b"], small["conv_ln_g"], small["conv_ln_b"], big[1], small["norm2_g"], big[2], big[3])

    return (gs[LOSS_ROW, 0], dx[None],
            *ordered(small_out(gs, cg), g_big), *ordered(small_out(ds, cd), d_big),
            *ordered(small_out(ms, cm), nm_big), *ordered(small_out(vs, cv), nv_big))
```

```python
import jax
import jax.numpy as jnp
from jax import lax
from jax.experimental import pallas as pl
from jax.experimental.pallas import tpu as pltpu

F32 = jnp.float32
BF16 = jnp.bfloat16

D_MODEL = 1024
ATT_HEADS = 8
HEAD_DIM = 64
ATT_W = ATT_HEADS * HEAD_DIM
CONV_CH = 512
CONV_TAPS = 31
CONV_HALO = 32
D_FF = 4 * D_MODEL
N_IN = 3 * ATT_W + ATT_HEADS + 2 * CONV_CH
O_Q, O_K, O_V, O_A, O_G, O_F = 0, 512, 1024, 1536, 2048, 2560
N_INP = O_F + 128
EPS = 1e-6
QK_SCALE = 0.125

ADAM_LR = 0.001
ADAM_B1 = 0.9
ADAM_B2 = 0.999
ADAM_EPS = 1e-08
ADAM_WD = 0.01
ADAM_STEP = 10

N_CHIPS = 4
VMEM_LIMIT = 52 * 1024 * 1024
MESH = pl.DeviceIdType.MESH
ANY = pl.BlockSpec(memory_space=pl.ANY)


def _params(n_axes, **kw):
    return pltpu.CompilerParams(dimension_semantics=("arbitrary",) * n_axes,
                                vmem_limit_bytes=VMEM_LIMIT, **kw)


def _dot(a, b):
    return jnp.dot(a, b, preferred_element_type=F32)


def _dot_nt(a, b):
    return lax.dot_general(a, b, (((1,), (1,)), ((), ())), preferred_element_type=F32)


def _dot_tn(a, b):
    return lax.dot_general(a, b, (((0,), (0,)), ((), ())), preferred_element_type=F32)


def _split3(a):
    a1 = a.astype(BF16)
    r = a - a1.astype(F32)
    a2 = r.astype(BF16)
    a3 = (r - a2.astype(F32)).astype(BF16)
    return a1, a2, a3


def _dot_hi_r(a, b_exact):
    return sum(_dot(p, b_exact) for p in _split3(a))


def _head_sums(a, blockdiag):
    a1 = a.astype(BF16)
    a2 = (a - a1.astype(F32)).astype(BF16)
    return _dot(a1, blockdiag) + _dot(a2, blockdiag)


def _dot_hi_l(a_exact, b):
    return sum(_dot(a_exact, p) for p in _split3(b))


def _sigmoid(x):
    return 1.0 / (1.0 + jnp.exp(-x))


def _head_blockdiag():
    i = jnp.arange(ATT_W) // HEAD_DIM
    return (i[:, None] == i[None, :]).astype(BF16)


AUG_LANES = 8


def _aug_place(first):
    piece = jnp.arange(3 * 128)[:, None] // 128
    h = jnp.arange(3 * 128)[:, None] % 128
    lane = jnp.arange(ATT_W)[None, :]
    return ((h < ATT_HEADS) & (lane == 128 * (h // 2) + AUG_LANES * (h % 2) + first + piece)).astype(BF16)


def _aug_ones(first):
    lane = jnp.arange(ATT_W) % 128
    pos = lane % AUG_LANES
    return ((lane < 2 * AUG_LANES) & (pos >= first) & (pos < first + 3)).astype(F32).reshape(1, ATT_W)


def _head_rows():
    h = jnp.arange(2 * ATT_HEADS)[:, None]
    i = jnp.arange(ATT_W)[None, :] // HEAD_DIM
    return (h == i).astype(BF16)


def _head_fold():
    i = jnp.arange(ATT_W)[:, None] % HEAD_DIM
    j = jnp.arange(128)[None, :]
    return (i == j).astype(BF16)


def _head_pick():
    i = jnp.arange(ATT_W)[:, None]
    h = jnp.arange(128)[None, :]
    return (i == h * HEAD_DIM).astype(BF16)


def _tril(n):
    r = jnp.arange(n)
    return (r[:, None] >= r[None, :]).astype(BF16)


SUBLANES = 8


def _fill_row_shifts(buf_ref, shifts_ref, tm):
    n = tm + CONV_HALO - SUBLANES
    for b in range(1, SUBLANES):
        shifts_ref[b - 1, 0:n, :] = buf_ref[pl.ds(b, n), :]


def _row_shifted(buf_ref, shifts_ref, offset, rows, base=0):
    a, b = divmod(offset, SUBLANES)
    start = pl.multiple_of(base + SUBLANES * a, SUBLANES)
    if b == 0:
        return buf_ref[pl.ds(start, rows), :]
    return shifts_ref[b - 1, pl.ds(start, rows), :]


CONV_ROWS = 32


def _mixer_in_fwd(x, g1, win, qg, kg, bf, cw, cvb, lng, lnb):
    S = x.shape[0]
    TM = min(512, S)
    nb = S // TM

    def body(x_ref, g1_ref, win_ref, qg_ref, kg_ref, bf_ref, cw_ref, cvb_ref, lng_ref, lnb_ref,
             bd_ref, tri_ref, pq_ref, pk_ref, oq_ref, ok_ref,
             u1_ref, proj_ref, q_ref, k_ref, v_ref, qa_ref, ka_ref, h0_ref, h1_ref, h3_ref,
             carry_ref, hbuf_ref, hs_ref):
        i = pl.program_id(0)

        @pl.when(i == 0)
        def _():
            carry_ref[...] = jnp.zeros_like(carry_ref)
            hbuf_ref[0:CONV_HALO, :] = jnp.zeros((CONV_HALO, CONV_CH), F32)

        @pl.when(i > 0)
        def _():
            hbuf_ref[0:CONV_HALO, :] = hbuf_ref[TM:TM + CONV_HALO, :]

        xv = x_ref[...]
        r = lax.rsqrt(jnp.mean(xv * xv, axis=-1, keepdims=True) + EPS)
        u = (xv * r * g1_ref[...]).astype(BF16)
        u1_ref[...] = u
        proj_ref[...] = _dot(u, win_ref[...])

        def headnorm(raw, gain):
            ss = _head_sums(raw * raw, bd_ref[...]) * (1.0 / HEAD_DIM)
            return raw * lax.rsqrt(ss + EPS) * gain

        q_ref[...] = (headnorm(proj_ref[:, O_Q:O_Q + ATT_W], qg_ref[...]) * QK_SCALE).astype(BF16)
        k_ref[...] = headnorm(proj_ref[:, O_K:O_K + ATT_W], kg_ref[...]).astype(BF16)
        v_ref[...] = proj_ref[:, O_V:O_V + ATT_W].astype(BF16)

        zf = proj_ref[:, O_F:O_F + 128] + bf_ref[...]
        logf = jnp.minimum(zf, 0.0) - jnp.log(1.0 + jnp.exp(-jnp.abs(zf)))
        lane = lax.broadcasted_iota(jnp.int32, (TM, 128), 1)
        logf = jnp.where(lane < ATT_HEADS, logf, 0.0)
        c8 = _dot_hi_l(tri_ref[...], logf) + carry_ref[...]
        carry_ref[...] = c8[TM - 1:TM, :]
        pieces = jnp.concatenate(_split3(c8), axis=1)
        qa_ref[...] = (_dot(pieces, pq_ref[...]) + oq_ref[...]).astype(BF16)
        ka_ref[...] = (ok_ref[...] - _dot(pieces, pk_ref[...])).astype(BF16)

        h0 = proj_ref[:, O_A:O_A + CONV_CH] * _sigmoid(proj_ref[:, O_G:O_G + CONV_CH])
        h0_ref[...] = h0
        hbuf_ref[CONV_HALO:CONV_HALO + TM, :] = h0
        _fill_row_shifts(hbuf_ref, hs_ref, TM)
        acc = jnp.zeros((TM, CONV_CH), F32) + cvb_ref[...]
        for j in range(CONV_TAPS):
            acc = acc + cw_ref[j:j + 1, :] * _row_shifted(hbuf_ref, hs_ref, CONV_HALO - CONV_TAPS + 1 + j, TM)
        h1_ref[...] = acc
        mu = jnp.mean(acc, axis=-1, keepdims=True)
        d = acc - mu
        var = jnp.mean(d * d, axis=-1, keepdims=True)
        h2 = d * lax.rsqrt(var + EPS) * lng_ref[...] + lnb_ref[...]
        h3_ref[...] = (h2 * _sigmoid(h2)).astype(BF16)

    row = lambda w: pl.BlockSpec((TM, w), lambda i: (i, 0))
    full = lambda a: pl.BlockSpec(a.shape, lambda i: (0,) * a.ndim)
    ins = (x, g1, win, qg, kg, bf, cw, cvb, lng, lnb, _head_blockdiag(), _tril(TM),
           _aug_place(0), _aug_place(3), _aug_ones(3), _aug_ones(0))
    return pl.pallas_call(
        body, name="mixer_in_fwd", grid=(nb,),
        in_specs=[row(D_MODEL)] + [full(a) for a in ins[1:]],
        out_specs=[row(D_MODEL), row(N_INP), row(ATT_W), row(ATT_W), row(ATT_W), row(ATT_W), row(ATT_W),
                   row(CONV_CH), row(CONV_CH), row(CONV_CH)],
        out_shape=[jax.ShapeDtypeStruct((S, D_MODEL), BF16),
                   jax.ShapeDtypeStruct((S, N_INP), F32),
                   jax.ShapeDtypeStruct((S, ATT_W), BF16),
                   jax.ShapeDtypeStruct((S, ATT_W), BF16),
                   jax.ShapeDtypeStruct((S, ATT_W), BF16),
                   jax.ShapeDtypeStruct((S, ATT_W), BF16),
                   jax.ShapeDtypeStruct((S, ATT_W), BF16),
                   jax.ShapeDtypeStruct((S, CONV_CH), F32),
                   jax.ShapeDtypeStruct((S, CONV_CH), F32),
                   jax.ShapeDtypeStruct((S, CONV_CH), BF16)],
        scratch_shapes=[pltpu.VMEM((1, 128), F32), pltpu.VMEM((TM + CONV_HALO, CONV_CH), F32),
                        pltpu.VMEM((SUBLANES - 1, TM + CONV_HALO, CONV_CH), F32)],
        compiler_params=_params(1),
    )(*ins)


def _pair_heads(lo, alo, x, xa):
    z = jnp.zeros_like(x)
    return (jnp.concatenate([jnp.where(lo, x, z), jnp.where(alo, xa, z)], axis=1),
            jnp.concatenate([jnp.where(lo, z, x), jnp.where(alo, z, xa)], axis=1))


def _attn_fwd(q, qa, k, ka, v):
    S = q.shape[0]
    T = min(1024, S)
    nq = S // T

    def body(q_ref, qa_ref, k_ref, ka_ref, v_ref, o_ref, lse_ref):
        qi = pl.program_id(1)
        lane = lax.broadcasted_iota(jnp.int32, (T, 128), 1)
        lo = lane < HEAD_DIM
        qm = _pair_heads(lo, lane < AUG_LANES, q_ref[...], qa_ref[...])
        tril = (lax.broadcasted_iota(jnp.int32, (T, T), 0) >= lax.broadcasted_iota(jnp.int32, (T, T), 1))

        def step(kj, carry, masked):
            off = pl.multiple_of(kj * T, T)
            kb = jnp.concatenate([k_ref[pl.ds(off, T), :], ka_ref[pl.ds(off, T), :]], axis=1)
            vb = v_ref[pl.ds(off, T), :]
            new = []
            for h in range(2):
                m, l, acc = carry[3 * h:3 * h + 3]
                s = _dot_nt(qm[h], kb)
                if masked:
                    s = jnp.where(tril, s, -1e30)
                m_new = jnp.maximum(m, jnp.max(s, axis=-1, keepdims=True))
                alpha = jnp.exp(m - m_new)
                p = jnp.exp(s - m_new)
                l = alpha * l + jnp.sum(p, axis=-1, keepdims=True)
                acc = alpha * acc + _dot(p.astype(BF16), vb)
                new += [m_new, l, acc]
            return tuple(new)

        init = (jnp.full((T, 1), -1e30, F32), jnp.zeros((T, 1), F32), jnp.zeros((T, 128), F32)) * 2
        carry = lax.fori_loop(0, qi, lambda kj, c: step(kj, c, False), init)
        m0, l0, a0, m1, l1, a1 = step(qi, carry, True)
        o_ref[...] = jnp.where(lo, a0 / l0, a1 / l1).astype(BF16)
        lse_t = jnp.where(lo, m0 + jnp.log(l0), m1 + jnp.log(l1)).T
        lse_ref[0:1, :] = lse_t[0:1, :]
        lse_ref[1:2, :] = lse_t[HEAD_DIM:HEAD_DIM + 1, :]

    qblk = pl.BlockSpec((T, 128), lambda hp, qi: (qi, hp))
    seq = pl.BlockSpec((S, 128), lambda hp, qi: (0, hp))
    return pl.pallas_call(
        body, name="attn_fwd", grid=(ATT_HEADS // 2, nq),
        in_specs=[qblk, qblk, seq, seq, seq],
        out_specs=[qblk, pl.BlockSpec((None, 2, T), lambda hp, qi: (hp, 0, qi))],
        out_shape=[jax.ShapeDtypeStruct((S, ATT_W), BF16),
                   jax.ShapeDtypeStruct((ATT_HEADS // 2, 2, S), F32)],
        compiler_params=_params(2),
    )(q, qa, k, ka, v)


def _attn_bwd(q, qa, k, ka, v, do, lse, dlt):
    S = q.shape[0]
    T = min(512, S)
    nq = S // T

    def body(q_ref, qa_ref, do_ref, lse_ref, dlt_ref, k_ref, ka_ref, v_ref, dq_ref, dk_ref, dv_ref, rows_ref, cols_ref):
        kj = pl.program_id(1)

        @pl.when(kj == 0)
        def _():
            dq_ref[...] = jnp.zeros_like(dq_ref)
            rows_ref[...] = jnp.zeros_like(rows_ref)

        lane = lax.broadcasted_iota(jnp.int32, (T, 128), 1)
        lo = lane < HEAD_DIM
        alo = lane < AUG_LANES
        triu = (lax.broadcasted_iota(jnp.int32, (T, T), 0) <= lax.broadcasted_iota(jnp.int32, (T, T), 1))
        kb = k_ref[...]
        kaug = jnp.concatenate([kb, ka_ref[...]], axis=1)
        vb = v_ref[...]

        def step(qi, carry, masked):
            off = pl.multiple_of(qi * T, T)
            qb = q_ref[pl.ds(off, T), :]
            dob = do_ref[pl.ds(off, T), :]
            qm = _pair_heads(lo, alo, qb, qa_ref[pl.ds(off, T), :])
            zero = jnp.zeros_like(qb)
            new, dqs = [], []
            for h in range(2):
                dk_a, dv_a, dc_a = carry[3 * h:3 * h + 3]
                dom = jnp.where(lo, dob, zero) if h == 0 else jnp.where(lo, zero, dob)
                s = _dot_nt(kaug, qm[h])
                if masked:
                    s = jnp.where(triu, s, -1e30)
                p = jnp.exp(s - lse_ref[h:h + 1, pl.ds(off, T)])
                dp = _dot_nt(vb, dom)
                ds = p * (dp - dlt_ref[h:h + 1, pl.ds(off, T)])
                pb = p.astype(BF16)
                dsb = ds.astype(BF16)
                dv_a = dv_a + _dot(pb, dob)
                dk_a = dk_a + _dot(dsb, qb)
                dc_a = dc_a + jnp.sum(ds, axis=1, keepdims=True)
                dqs.append(_dot_tn(dsb, kb))
                rows_ref[h:h + 1, pl.ds(off, T)] += jnp.sum(ds, axis=0, keepdims=True)
                new += [dk_a, dv_a, dc_a]
            dq_ref[pl.ds(off, T), :] += jnp.where(lo, dqs[0], dqs[1])
            return tuple(new)

        init = (jnp.zeros((T, 128), F32), jnp.zeros((T, 128), F32), jnp.zeros((T, 1), F32)) * 2
        carry = step(kj, init, True)
        carry = lax.fori_loop(kj + 1, nq, lambda qi, c: step(qi, c, False), carry)
        dk_ref[...] = jnp.where(lo, carry[0], carry[3])
        dv_ref[...] = jnp.where(lo, carry[1], carry[4])
        cols_ref[...] = -jnp.where(lo, carry[2], carry[5])

    seq = pl.BlockSpec((S, 128), lambda hp, kj: (0, hp))
    rows = pl.BlockSpec((None, 2, S), lambda hp, kj: (hp, 0, 0))
    kblk = pl.BlockSpec((T, 128), lambda hp, kj: (kj, hp))
    return pl.pallas_call(
        body, name="attn_bwd", grid=(ATT_HEADS // 2, nq),
        in_specs=[seq, seq, seq, rows, rows, kblk, kblk, kblk],
        out_specs=[seq, kblk, kblk, rows, kblk],
        out_shape=[jax.ShapeDtypeStruct((S, ATT_W), F32), jax.ShapeDtypeStruct((S, ATT_W), F32),
                   jax.ShapeDtypeStruct((S, ATT_W), F32),
                   jax.ShapeDtypeStruct((ATT_HEADS // 2, 2, S), F32),
                   jax.ShapeDtypeStruct((S, ATT_W), F32)],
        compiler_params=_params(2),
    )(q, qa, do, lse, dlt, k, ka, v)


def _wo_spec(wo4):
    return pl.BlockSpec(wo4.shape, lambda i: (0, 0, 0))


def _wo_halves(wo_ref):
    half = N_CHIPS // 2
    return (wo_ref[0:half].reshape(ATT_W, D_MODEL), wo_ref[half:N_CHIPS].reshape(CONV_CH, D_MODEL))


def _mixer_out_fwd(x, att, h3, wo4, g2):
    S = x.shape[0]
    TM = min(512, S)

    def body(x_ref, att_ref, h3_ref, wo_ref, g2_ref, x2_ref, u2_ref):
        wa, wc = _wo_halves(wo_ref)
        x2 = x_ref[...] + _dot(att_ref[...], wa) + _dot(h3_ref[...], wc)
        x2_ref[...] = x2
        r = lax.rsqrt(jnp.mean(x2 * x2, axis=-1, keepdims=True) + EPS)
        u2_ref[...] = (x2 * r * g2_ref[...]).astype(BF16)

    row = lambda w: pl.BlockSpec((TM, w), lambda i: (i, 0))
    full = lambda a: pl.BlockSpec(a.shape, lambda i: (0,) * a.ndim)
    return pl.pallas_call(
        body, name="mixer_out_fwd", grid=(S // TM,),
        in_specs=[row(D_MODEL), row(ATT_W), row(CONV_CH), _wo_spec(wo4), full(g2)],
        out_specs=[row(D_MODEL), row(D_MODEL)],
        out_shape=[jax.ShapeDtypeStruct((S, D_MODEL), F32), jax.ShapeDtypeStruct((S, D_MODEL), BF16)],
        compiler_params=_params(1),
    )(x, att, h3, wo4, g2)


def _mlp_w_specs():
    return [pl.BlockSpec((None, D_MODEL, D_FF // N_CHIPS), lambda i, f: (f, 0, 0)),
            pl.BlockSpec((None, D_FF // N_CHIPS, D_MODEL), lambda i, f: (f, 0, 0))]


def _mlp_fwd(x2, u2, w1, w2, target=None):
    S = x2.shape[0]
    head = target is not None
    TM = min(1024, S)
    TF = 1024
    nf = D_FF // TF

    def body(*refs):
        x2_ref, u2_ref, w1_ref, w2_ref = refs[:4]
        x3_ref, z_ref, hh_ref = refs[4 + head:7 + head]
        i = pl.program_id(0)
        f = pl.program_id(1)

        @pl.when(f == 0)
        def _():
            x3_ref[...] = x2_ref[...]

        z = _dot(u2_ref[...], w1_ref[...])
        z_ref[...] = z
        zr = jnp.maximum(z, 0.0)
        hh = (zr * zr).astype(BF16)
        hh_ref[...] = hh
        x3_ref[...] += _dot(hh, w2_ref[...])

        if head:
            t_ref, loss_ref = refs[4], refs[8]

            @pl.when((i == 0) & (f == 0))
            def _():
                loss_ref[...] = jnp.zeros_like(loss_ref)

            @pl.when(f == nf - 1)
            def _():
                d = x3_ref[...] - t_ref[...]
                x3_ref[...] = d * (1.0 / D_MODEL)
                loss_ref[...] += jnp.sum(d * d)

    rows = pl.BlockSpec((TM, D_MODEL), lambda i, f: (i, 0))
    once = pl.BlockSpec((TM, D_MODEL), lambda i, f: (i, 0), pipeline_mode=pl.Buffered(1))
    tile = pl.BlockSpec((TM, TF), lambda i, f: (i, f))
    return pl.pallas_call(
        body, name="mlp_fwd_loss" if head else "mlp_fwd", grid=(S // TM, nf),
        in_specs=[once if head else rows, rows] + _mlp_w_specs() + [once] * head,
        out_specs=[rows, tile, tile] + [pl.BlockSpec((8, 128), lambda i, f: (0, 0))] * head,
        out_shape=[jax.ShapeDtypeStruct((S, D_MODEL), F32), jax.ShapeDtypeStruct((S, D_FF), F32),
                   jax.ShapeDtypeStruct((S, D_FF), BF16)] + [jax.ShapeDtypeStruct((8, 128), F32)] * head,
        compiler_params=_params(2),
    )(x2, u2, w1, w2, *([target] if head else []))


def _mlp_bwd(dx3, z, x2, g2, w1, w2):
    S = dx3.shape[0]
    TM = min(1024, S)
    TF = 1024
    nf = D_FF // TF

    def body(dx3_ref, z_ref, x2_ref, g2_ref, w1_ref, w2_ref, dz_ref, dx2_ref, dg2_ref, du2_ref):
        i = pl.program_id(0)
        f = pl.program_id(1)

        @pl.when((i == 0) & (f == 0))
        def _():
            dg2_ref[...] = jnp.zeros_like(dg2_ref)

        @pl.when(f == 0)
        def _():
            du2_ref[...] = jnp.zeros_like(du2_ref)

        dhh = _dot_nt(dx3_ref[...].astype(BF16), w2_ref[...])
        dz = (dhh * (2.0 * jnp.maximum(z_ref[...], 0.0))).astype(BF16)
        dz_ref[...] = dz
        du2_ref[...] += _dot_nt(dz, w1_ref[...])

        @pl.when(f == nf - 1)
        def _():
            x2 = x2_ref[...]
            r = lax.rsqrt(jnp.mean(x2 * x2, axis=-1, keepdims=True) + EPS)
            n = x2 * r
            du2 = du2_ref[...]
            t = du2 * g2_ref[...]
            dx2_ref[...] = dx3_ref[...] + r * (t - n * jnp.mean(t * n, axis=-1, keepdims=True))
            dg2_ref[0:1, :] += jnp.sum(du2 * n, axis=0, keepdims=True)

    rowi = pl.BlockSpec((TM, D_MODEL), lambda i, f: (i, 0))
    held = pl.BlockSpec((TM, D_MODEL), lambda i, f: (i, 0), pipeline_mode=pl.Buffered(1))
    return pl.pallas_call(
        body, name="mlp_bwd", grid=(S // TM, nf),
        in_specs=[held, pl.BlockSpec((TM, TF), lambda i, f: (i, f)), held,
                  pl.BlockSpec((1, D_MODEL), lambda i, f: (0, 0))] + _mlp_w_specs(),
        out_specs=[pl.BlockSpec((TM, TF), lambda i, f: (i, f)), rowi, pl.BlockSpec((8, D_MODEL), lambda i, f: (0, 0))],
        out_shape=[jax.ShapeDtypeStruct((S, D_FF), BF16), jax.ShapeDtypeStruct((S, D_MODEL), F32),
                   jax.ShapeDtypeStruct((8, D_MODEL), F32)],
        scratch_shapes=[pltpu.VMEM((TM, D_MODEL), F32)],
        compiler_params=_params(2),
    )(dx3, z, x2, g2, w1, w2)


def _matmul_tn(a, b, col_shards=1):
    S, I = a.shape
    J = b.shape[1]
    TI = min(I, 1024)
    TJ = 1024 if J % 1024 == 0 else 896
    TS = min(S, 1024)
    nk = S // TS
    per = J // col_shards // TJ

    def body(a_ref, b_ref, o_ref, acc_ref):
        k = pl.program_id(2)

        @pl.when(k == 0)
        def _():
            acc_ref[...] = jnp.zeros_like(acc_ref)

        acc_ref[...] += _dot_tn(a_ref[...].astype(BF16), b_ref[...].astype(BF16))

        @pl.when(k == nk - 1)
        def _():
            o_ref[...] = acc_ref[...].astype(BF16)

    return pl.pallas_call(
        body, name="matmul_tn", grid=(I // TI, J // TJ, nk),
        in_specs=[pl.BlockSpec((TS, TI), lambda i, j, k: (k, i)), pl.BlockSpec((TS, TJ), lambda i, j, k: (k, j))],
        out_specs=pl.BlockSpec((None, TI, TJ), lambda i, j, k: (j // per, i, j % per)),
        out_shape=jax.ShapeDtypeStruct((col_shards, I, J // col_shards), BF16),
        scratch_shapes=[pltpu.VMEM((TI, TJ), F32)],
        compiler_params=_params(3),
    )(a, b)


def _mixer_out_bwd(dx2, wo4, att, h1, lng, lnb):
    S = dx2.shape[0]
    TM = min(512, S)

    def body(dx2_ref, wo_ref, att_ref, h1_ref, lng_ref, lnb_ref, hr_ref, datt_ref, dlt_ref, dh1_ref, sm_ref):
        @pl.when(pl.program_id(0) == 0)
        def _():
            sm_ref[...] = jnp.zeros_like(sm_ref)

        dxb = dx2_ref[...].astype(BF16)
        wa, wc = _wo_halves(wo_ref)
        datt = _dot_nt(dxb, wa)
        datt_ref[...] = datt.astype(BF16)
        prod = datt * att_ref[...].astype(F32)
        dlt_ref[...] = sum(_dot_nt(hr_ref[...], piece) for piece in _split3(prod))[0:ATT_HEADS, :]
        dh3 = _dot_nt(dxb, wc)
        h1 = h1_ref[...]
        mu = jnp.mean(h1, axis=-1, keepdims=True)
        d = h1 - mu
        rstd = lax.rsqrt(jnp.mean(d * d, axis=-1, keepdims=True) + EPS)
        n = d * rstd
        h2 = n * lng_ref[...] + lnb_ref[...]
        sg = _sigmoid(h2)
        dh2 = dh3 * (sg * (1.0 + h2 * (1.0 - sg)))
        dn = dh2 * lng_ref[...]
        dh1 = rstd * (dn - jnp.mean(dn, axis=-1, keepdims=True) - n * jnp.mean(dn * n, axis=-1, keepdims=True))
        dh1_ref[...] = dh1
        sm_ref[0:1, :] += jnp.sum(dh2 * n, axis=0, keepdims=True)
        sm_ref[1:2, :] += jnp.sum(dh2, axis=0, keepdims=True)
        sm_ref[2:3, :] += jnp.sum(dh1, axis=0, keepdims=True)

    row = lambda w: pl.BlockSpec((TM, w), lambda i: (i, 0))
    full = lambda a: pl.BlockSpec(a.shape, lambda i: (0,) * a.ndim)
    hr = _head_rows()
    return pl.pallas_call(
        body, name="mixer_out_bwd", grid=(S // TM,),
        in_specs=[row(D_MODEL), _wo_spec(wo4), row(ATT_W), row(CONV_CH), full(lng), full(lnb), full(hr)],
        out_specs=[row(ATT_W), pl.BlockSpec((ATT_HEADS, TM), lambda i: (0, i)), row(CONV_CH),
                   pl.BlockSpec((8, CONV_CH), lambda i: (0, 0))],
        out_shape=[jax.ShapeDtypeStruct((S, ATT_W), BF16), jax.ShapeDtypeStruct((ATT_HEADS, S), F32),
                   jax.ShapeDtypeStruct((S, CONV_CH), F32), jax.ShapeDtypeStruct((8, CONV_CH), F32)],
        compiler_params=_params(1),
    )(dx2, wo4, att, h1, lng, lnb, hr)


def _conv_glu_bwd(dh1, h0, proj, cw):
    S = dh1.shape[0]
    TM = min(512, S)
    nb = S // TM
    lead = CONV_HALO - CONV_TAPS + 1

    def body(dh1_ref, dnx_ref, h0_ref, hpv_ref, a_ref, g_ref, cw_ref, dag_ref, dcw_ref,
             dbuf_ref, hbuf_ref, ds_ref, hs_ref, dh0_ref, dcw8_ref):
        i = pl.program_id(0)

        @pl.when(i == 0)
        def _():
            dcw8_ref[...] = jnp.zeros_like(dcw8_ref)

        dbuf_ref[0:TM, :] = dh1_ref[...]
        dbuf_ref[TM:TM + CONV_HALO, :] = jnp.where(i < nb - 1, dnx_ref[0:CONV_HALO, :], 0.0)
        hbuf_ref[0:CONV_HALO, :] = jnp.where(i > 0, hpv_ref[TM - CONV_HALO:TM, :], 0.0)
        hbuf_ref[CONV_HALO:CONV_HALO + TM, :] = h0_ref[...]
        _fill_row_shifts(dbuf_ref, ds_ref, TM)
        _fill_row_shifts(hbuf_ref, hs_ref, TM)

        def conv_rows(step, _):
            r0 = pl.multiple_of(step * CONV_ROWS, CONV_ROWS)
            dh1 = dbuf_ref[pl.ds(r0, CONV_ROWS), :]
            part = jnp.zeros((CONV_ROWS, CONV_CH), F32)
            for j in range(CONV_TAPS):
                part = part + cw_ref[j:j + 1, :] * _row_shifted(dbuf_ref, ds_ref, CONV_TAPS - 1 - j, CONV_ROWS, r0)
                prod = dh1 * _row_shifted(hbuf_ref, hs_ref, lead + j, CONV_ROWS, r0)
                dcw8_ref[j] += jnp.sum(prod.reshape(CONV_ROWS // SUBLANES, SUBLANES, CONV_CH), axis=0)
            dh0_ref[pl.ds(r0, CONV_ROWS), :] = part
            return 0

        lax.fori_loop(0, TM // CONV_ROWS, conv_rows, 0)

        @pl.when(i == nb - 1)
        def _():
            dcw_ref[...] = jnp.sum(dcw8_ref[...], axis=1)

        dh0 = dh0_ref[...]
        sg = _sigmoid(g_ref[...])
        dag_ref[:, 0:CONV_CH] = (dh0 * sg).astype(BF16)
        dag_ref[:, CONV_CH:2 * CONV_CH] = (dh0 * a_ref[...] * sg * (1.0 - sg)).astype(BF16)

    blk = lambda fn: pl.BlockSpec((TM, CONV_CH), fn)
    return pl.pallas_call(
        body, name="conv_glu_bwd", grid=(nb,),
        in_specs=[blk(lambda i: (i, 0)), blk(lambda i: (jnp.minimum(i + 1, nb - 1), 0)),
                  blk(lambda i: (i, 0)), blk(lambda i: (jnp.maximum(i - 1, 0), 0)),
                  blk(lambda i: (i, O_A // CONV_CH)), blk(lambda i: (i, O_G // CONV_CH)),
                  pl.BlockSpec(cw.shape, lambda i: (0, 0))],
        out_specs=[pl.BlockSpec((TM, 2 * CONV_CH), lambda i: (i, 0)), pl.BlockSpec((CONV_HALO, CONV_CH), lambda i: (0, 0))],
        out_shape=[jax.ShapeDtypeStruct((S, 2 * CONV_CH), BF16), jax.ShapeDtypeStruct((CONV_HALO, CONV_CH), F32)],
        scratch_shapes=[pltpu.VMEM((TM + CONV_HALO, CONV_CH), F32), pltpu.VMEM((TM + CONV_HALO, CONV_CH), F32),
                        pltpu.VMEM((SUBLANES - 1, TM + CONV_HALO, CONV_CH), F32),
                        pltpu.VMEM((SUBLANES - 1, TM + CONV_HALO, CONV_CH), F32),
                        pltpu.VMEM((TM, CONV_CH), F32), pltpu.VMEM((CONV_HALO, SUBLANES, CONV_CH), F32)],
        compiler_params=_params(1),
    )(dh1, dh1, h0, h0, proj, proj, cw)


def _mixer_in_bwd(x, dx2, proj, dq, dk, dv, dag, dct, drb, g1, win, qg, kg, bf):
    S = x.shape[0]
    TM = min(512, S)
    nb = S // TM

    def body(x_ref, dx2_ref, qr_ref, kr_ref, fz_ref, dq_ref, dk_ref, dv_ref, dag_ref, dct_ref, drb_ref,
             g1_ref, win_ref, qg_ref, kg_ref, bf_ref, bd_ref, fold_ref, triu_ref, pick_ref,
             dproj_ref, dx_ref, dg1_ref, sm_ref, carry_ref, gsum_ref):
        i = pl.program_id(0)

        @pl.when(i == 0)
        def _():
            carry_ref[...] = jnp.zeros_like(carry_ref)
            gsum_ref[...] = jnp.zeros_like(gsum_ref)
            dg1_ref[...] = jnp.zeros_like(dg1_ref)
            sm_ref[...] = jnp.zeros_like(sm_ref)

        def headnorm_bwd(raw, dy, gain, scale, row):
            rs = lax.rsqrt(_head_sums(raw * raw, bd_ref[...]) * (1.0 / HEAD_DIM) + EPS)
            n = raw * rs
            gsum_ref[row:row + 1, :] += jnp.sum(dy * n, axis=0, keepdims=True) * scale
            dn = dy * (gain * scale)
            return rs * (dn - n * (_head_sums(dn * n, bd_ref[...]) * (1.0 / HEAD_DIM)))

        dproj_ref[:, O_Q:O_Q + ATT_W] = headnorm_bwd(qr_ref[...], dq_ref[...], qg_ref[...], QK_SCALE, 0).astype(BF16)
        dproj_ref[:, O_K:O_K + ATT_W] = headnorm_bwd(kr_ref[...], dk_ref[...], kg_ref[...], 1.0, 1).astype(BF16)
        dproj_ref[:, O_V:O_V + ATT_W] = dv_ref[...].astype(BF16)
        dproj_ref[:, O_A:O_A + 2 * CONV_CH] = dag_ref[...]

        dc8 = jnp.concatenate([dct_ref[...], jnp.zeros((128 - ATT_HEADS, TM), F32)], axis=0).T
        dc8 = dc8 + _dot_hi_r(drb_ref[...], pick_ref[...])
        dlogf = _dot_hi_l(triu_ref[...], dc8) + carry_ref[...]
        carry_ref[...] = dlogf[0:1, :]
        df = dlogf * _sigmoid(-(fz_ref[...] + bf_ref[...]))
        dproj_ref[:, O_F:O_F + 128] = df.astype(BF16)
        sm_ref[2:3, :] += jnp.sum(df, axis=0, keepdims=True)

        du1 = _dot_nt(dproj_ref[...], win_ref[...])
        xv = x_ref[...]
        r = lax.rsqrt(jnp.mean(xv * xv, axis=-1, keepdims=True) + EPS)
        n1 = xv * r
        t = du1 * g1_ref[...]
        dx_ref[...] = dx2_ref[...] + r * (t - n1 * jnp.mean(t * n1, axis=-1, keepdims=True))
        dg1_ref[0:1, :] += jnp.sum(du1 * n1, axis=0, keepdims=True)

        @pl.when(i == nb - 1)
        def _():
            sm_ref[0:2, :] = _dot_hi_r(gsum_ref[0:8, :], fold_ref[...])[0:2, :]

    rev = lambda w, cb=0: pl.BlockSpec((TM, w), lambda i: (nb - 1 - i, cb))
    full = lambda a: pl.BlockSpec(a.shape, lambda i: (0,) * a.ndim)
    bd, fold, triu = _head_blockdiag(), _head_fold(), _tril(TM).T
    consts = (g1, win, qg, kg, bf, bd, fold, triu, _head_pick())
    return pl.pallas_call(
        body, name="mixer_in_bwd", grid=(nb,),
        in_specs=[rev(D_MODEL), rev(D_MODEL), rev(ATT_W, O_Q // ATT_W), rev(ATT_W, O_K // ATT_W), rev(128, O_F // 128),
                  rev(ATT_W), rev(ATT_W), rev(ATT_W), rev(2 * CONV_CH),
                  pl.BlockSpec((ATT_HEADS, TM), lambda i: (0, nb - 1 - i)), rev(ATT_W)] + [full(a) for a in consts],
        out_specs=[rev(N_INP), rev(D_MODEL), pl.BlockSpec((8, D_MODEL), lambda i: (0, 0)),
                   pl.BlockSpec((8, 128), lambda i: (0, 0))],
        out_shape=[jax.ShapeDtypeStruct((S, N_INP), BF16), jax.ShapeDtypeStruct((S, D_MODEL), F32),
                   jax.ShapeDtypeStruct((8, D_MODEL), F32), jax.ShapeDtypeStruct((8, 128), F32)],
        scratch_shapes=[pltpu.VMEM((1, 128), F32), pltpu.VMEM((8, ATT_W), F32)],
        compiler_params=_params(1),
    )(x, dx2, proj, proj, proj, dq, dk, dv, dag, dct, drb, *consts)


def _layer_fwd(x, early, late, target=None):
    p = early(x)
    u1, proj, q, k, v, qa, ka, h0, h1, h3 = _mixer_in_fwd(
        x, p["g1"], p["win"], p["qg"], p["kg"], p["bf"], p["cw"], p["cvb"], p["lng"], p["lnb"])
    att, lse = _attn_fwd(q, qa, k, ka, v)
    p = dict(p, **late(att))
    x2, u2 = _mixer_out_fwd(x, att, h3, p["wo"], p["g2"])
    x3, z, hh, *loss_acc = _mlp_fwd(x2, u2, p["w1"], p["w2"], target)
    saved = dict(x=x, u1=u1, proj=proj, q=q, k=k, v=v, qa=qa, ka=ka, h0=h0, h1=h1, h3=h3, att=att, lse=lse,
                 x2=x2, u2=u2, z=z, hh=hh)
    return (x3 if target is None else (x3, loss_acc[0])), saved, p


def _tie(a, token):
    return a if token is None else a + token[0:1, 0:1]


def _layer_bwd(dx3, s, p, reduce):
    dz, dx2, dg2 = _mlp_bwd(dx3, s["z"], s["x2"], p["g2"], p["w1"], p["w2"])
    g_w2 = _matmul_tn(s["hh"], dx3)
    g_w1 = _matmul_tn(s["u2"], dz, col_shards=N_CHIPS)
    token = reduce("a", {2: g_w1, 3: g_w2.reshape(N_CHIPS, D_FF // N_CHIPS, D_MODEL)})
    datt, dlt, dh1, sm_c = _mixer_out_bwd(dx2, p["wo"], s["att"], s["h1"], _tie(p["lng"], token), p["lnb"])
    g_wo = jnp.concatenate([_matmul_tn(s["att"], dx2)[0], _matmul_tn(s["h3"], dx2)[0]], axis=0)
    dag, dcw = _conv_glu_bwd(dh1, s["h0"], s["proj"], p["cw"])
    dq, dk, dv, dc4, drb = _attn_bwd(s["q"], s["qa"], s["k"], s["ka"], s["v"], datt, s["lse"],
                                     dlt.reshape(ATT_HEADS // 2, 2, dlt.shape[1]))
    dct = dc4.reshape(ATT_HEADS, dc4.shape[2])
    dproj, dx, dg1, sm_a = _mixer_in_bwd(s["x"], dx2, s["proj"], dq, dk, dv, dag, dct, drb,
                                         p["g1"], p["win"], p["qg"], p["kg"], p["bf"])
    g_win = _win_to_global(_matmul_tn(s["u1"], dproj)[0])
    g_win = g_win.reshape(D_MODEL, N_CHIPS, N_IN // N_CHIPS).transpose(1, 0, 2)
    token = reduce("b", {0: g_win, 1: g_wo.reshape(N_CHIPS, D_MODEL // N_CHIPS, D_MODEL)})
    small = dict(g1=dg1[0], g2=dg2[0], lng=sm_c[0], lnb=sm_c[1], cvb=sm_c[2], cw=dcw[0:CONV_TAPS],
                 qg=sm_a[0, 0:HEAD_DIM], kg=sm_a[1, 0:HEAD_DIM], bf=sm_a[2, 0:ATT_HEADS])
    return dx, small, token


def _local_step(x, target, weights, reduce):
    saved, layers = [], []
    h = x
    for l, (early, late) in enumerate(weights):
        h, s, p = _layer_fwd(h, early, late, target if l == len(weights) - 1 else None)
        saved.append(s)
        layers.append(p)
    dy, loss_acc = h
    loss = loss_acc[0, 0] * (0.5 / D_MODEL)
    smalls = []
    d, token = dy, None
    for l in reversed(range(len(layers))):
        d, small, token = _layer_bwd(d, saved[l], dict(layers[l], g2=_tie(layers[l]["g2"], token)), reduce(l))
        smalls.append(small)
    return loss, d, smalls[::-1]


def _win_to_internal(w):
    pad = jnp.zeros(w.shape[:-1] + (N_INP - N_IN,), w.dtype)
    return jnp.concatenate([w[..., :1536], w[..., 1544:], w[..., 1536:1544], pad], axis=-1)


def _win_to_global(g):
    return jnp.concatenate([g[..., :1536], g[..., O_F:O_F + ATT_HEADS], g[..., 1536:O_F]], axis=-1)


def _layer_params(l, win, cw, norm1_g, b_f, q_norm_g, k_norm_g, conv_b, conv_ln_g, conv_ln_b, norm2_g):
    row = lambda a: a.reshape(1, -1)
    return dict(
        win=win, cw=jnp.pad(cw, ((0, CONV_HALO - CONV_TAPS), (0, 0))),
        g1=row(norm1_g[l]), g2=row(norm2_g[l]),
        qg=row(jnp.tile(q_norm_g[l], ATT_HEADS)), kg=row(jnp.tile(k_norm_g[l], ATT_HEADS)),
        bf=row(jnp.pad(b_f[l], (0, 128 - ATT_HEADS))),
        cvb=row(conv_b[l]), lng=row(conv_ln_g[l]), lnb=row(conv_ln_b[l]))


def _place():
    x, y, c = lax.axis_index("x"), lax.axis_index("y"), lax.axis_index("c")
    chips = [(1 - x, y), (x, 1 - y), (1 - x, 1 - y)]
    return x, y, c, chips


HBM = pl.BlockSpec(memory_space=pltpu.HBM)
SEM = pl.BlockSpec(memory_space=pltpu.SEMAPHORE)
GATHER_PEERS = N_CHIPS


def _gather_peers():
    x, y, c, chips = _place()
    return [(*chip, c) for chip in chips] + [(x, y, 1 - c)], [2 * px + py for px, py in chips] + [2 * x + y]


def _gather_start(srcs):
    n = len(srcs)

    def body(*refs):
        ins, lands = refs[:n], refs[n:2 * n]
        send_sems, recv_sems, token = refs[2 * n], refs[2 * n + 1], refs[-1]
        me = 2 * lax.axis_index("x") + lax.axis_index("y")
        peers, _ = _gather_peers()
        for g in range(n):
            for j, to in enumerate(peers):
                pltpu.make_async_remote_copy(src_ref=ins[g], dst_ref=lands[g].at[me],
                                             send_sem=send_sems.at[GATHER_PEERS * g + j],
                                             recv_sem=recv_sems.at[GATHER_PEERS * g + j],
                                             device_id=to, device_id_type=MESH).start()
        token[...] = jnp.zeros_like(token)

    lands = [lax.empty((N_CHIPS,) + a.shape, a.dtype) for a in srcs]
    outs = pl.pallas_call(
        body, name="gather_start",
        in_specs=[HBM] * (2 * n),
        out_specs=[SEM, SEM] + [HBM] * (2 * n) + [pl.BlockSpec(memory_space=pltpu.VMEM)],
        out_shape=[pltpu.SemaphoreType.DMA((GATHER_PEERS * n,)), pltpu.SemaphoreType.DMA((GATHER_PEERS * n,))]
        + [pltpu.HBM(a.shape, a.dtype) for a in srcs] + [pltpu.HBM(a.shape, a.dtype) for a in lands]
        + [jax.ShapeDtypeStruct((8, 128), F32)],
        input_output_aliases={i: 2 + i for i in range(2 * n)},
        compiler_params=pltpu.CompilerParams(has_side_effects=pltpu.SideEffectType.DATAFLOW_SIDE_EFFECTING),
    )(*[pltpu.with_memory_space_constraint(a, pltpu.HBM) for a in srcs],
      *[pltpu.with_memory_space_constraint(a, pltpu.HBM) for a in lands])
    return outs[0], outs[1], list(outs[2:2 + n]), list(outs[2 + n:2 + 2 * n]), outs[-1]


def _gather_wait(name, groups, send_sems, recv_sems, srcs, lands, after):
    k = len(groups)

    def body(*refs):
        ins, lnd = refs[:k], refs[k:2 * k]
        ssem, rsem = refs[2 * k], refs[2 * k + 1]
        peers, slots = _gather_peers()
        for i, g in enumerate(groups):
            for j, to in enumerate(peers):
                cp = pltpu.make_async_remote_copy(src_ref=ins[i], dst_ref=lnd[i].at[slots[j]],
                                                  send_sem=ssem.at[GATHER_PEERS * g + j],
                                                  recv_sem=rsem.at[GATHER_PEERS * g + j],
                                                  device_id=to, device_id_type=MESH)
                cp.wait_send()
                cp.wait_recv()

    outs = pl.pallas_call(
        body, name=name,
        in_specs=[HBM] * (2 * k) + [SEM, SEM, ANY],
        out_specs=[HBM] * (2 * k),
        out_shape=[pltpu.HBM(a.shape, a.dtype) for a in srcs] + [pltpu.HBM(a.shape, a.dtype) for a in lands],
        input_output_aliases={i: i for i in range(2 * k)},
        compiler_params=pltpu.CompilerParams(has_side_effects=pltpu.SideEffectType.DATAFLOW_SIDE_EFFECTING),
    )(*srcs, *lands, send_sems, recv_sems, after)
    return list(outs[k:])


REDUCE_STEPS = 8


def _row_half(a):
    return a.shape[-2] // 2


def _half_swap(name, gs):
    n = len(gs)

    def body(*refs):
        ins, outs = refs[:n], refs[n:2 * n]
        send_sems, recv_sems = refs[2 * n:]
        x, y, c, _ = _place()
        copies = []
        for a in range(n):
            h = _row_half(gs[a])
            copies.append(pltpu.make_async_remote_copy(
                src_ref=ins[a].at[:, pl.ds((1 - c) * h, h), :], dst_ref=outs[a], send_sem=send_sems.at[a],
                recv_sem=recv_sems.at[a], device_id=(x, y, 1 - c), device_id_type=MESH))
        for cp in copies:
            cp.start()
        for cp in copies:
            cp.wait()

    return pl.pallas_call(
        body, name=name,
        in_specs=[ANY] * n, out_specs=[ANY] * n,
        out_shape=[jax.ShapeDtypeStruct((N_CHIPS, _row_half(a), a.shape[-1]), a.dtype) for a in gs],
        scratch_shapes=[pltpu.SemaphoreType.DMA((n,)), pltpu.SemaphoreType.DMA((n,))],
        compiler_params=pltpu.CompilerParams(has_side_effects=True),
    )(*gs)


def _half_specs(gs, row_block):
    tiles = [_row_half(a) // REDUCE_STEPS for a in gs]
    return [pl.BlockSpec((N_CHIPS, t, a.shape[-1]), lambda i, p: (0, row_block(i, p), 0)) for a, t in zip(gs, tiles)]


def _half_add(name, gs, got, place):
    n = len(gs)

    def body(place_ref, *refs):
        own, theirs, outs = refs[:n], refs[n:2 * n], refs[2 * n:]
        for a in range(n):
            outs[a][...] = (own[a][...].astype(F32) + theirs[a][...].astype(F32)).astype(BF16)

    plain = _half_specs(gs, lambda i, p: i)
    return pl.pallas_call(
        body, name=name,
        grid_spec=pltpu.PrefetchScalarGridSpec(
            num_scalar_prefetch=1, grid=(REDUCE_STEPS,),
            in_specs=_half_specs(gs, lambda i, p: p[1] * REDUCE_STEPS + i) + plain, out_specs=plain),
        out_shape=[jax.ShapeDtypeStruct(a.shape, BF16) for a in got],
        compiler_params=_params(1),
    )(place, *gs, *got)


def _exchange_copies(parts, lands, send_sems, recv_sems):
    x, y, c, chips = _place()
    return [pltpu.make_async_remote_copy(src_ref=parts[a].at[2 * px + py], dst_ref=lands[a].at[j],
                                         send_sem=send_sems.at[3 * a + j], recv_sem=recv_sems.at[3 * a + j],
                                         device_id=(px, py, c), device_id_type=MESH)
            for a in range(len(parts)) for j, (px, py) in enumerate(chips)]


def _exchange_start(name, parts):
    n = len(parts)

    def body(*refs):
        _ = [cp.start() for cp in _exchange_copies(refs[:n], refs[n:2 * n], refs[2 * n], refs[2 * n + 1])]
        refs[-1][...] = jnp.zeros_like(refs[-1])

    lands = [lax.empty((N_CHIPS - 1,) + a.shape[1:], a.dtype) for a in parts]
    outs = pl.pallas_call(
        body, name=name,
        in_specs=[HBM] * (2 * n),
        out_specs=[SEM, SEM] + [HBM] * (2 * n) + [pl.BlockSpec(memory_space=pltpu.VMEM)],
        out_shape=[pltpu.SemaphoreType.DMA((3 * n,)), pltpu.SemaphoreType.DMA((3 * n,))]
        + [pltpu.HBM(a.shape, a.dtype) for a in parts] + [pltpu.HBM(a.shape, a.dtype) for a in lands]
        + [jax.ShapeDtypeStruct((8, 128), F32)],
        input_output_aliases={i: 2 + i for i in range(2 * n)},
        compiler_params=pltpu.CompilerParams(has_side_effects=pltpu.SideEffectType.DATAFLOW_SIDE_EFFECTING),
    )(*[pltpu.with_memory_space_constraint(a, pltpu.HBM) for a in parts],
      *[pltpu.with_memory_space_constraint(a, pltpu.HBM) for a in lands])
    return outs[0], outs[1], list(outs[2:2 + n]), list(outs[2 + n:2 + 2 * n]), outs[-1]


def _exchange_wait(name, send_sems, recv_sems, parts, lands, after):
    n = len(parts)

    def body(*refs):
        for cp in _exchange_copies(refs[:n], refs[n:2 * n], refs[2 * n], refs[2 * n + 1]):
            cp.wait_send()
            cp.wait_recv()

    outs = pl.pallas_call(
        body, name=name,
        in_specs=[HBM] * (2 * n) + [SEM, SEM, ANY],
        out_specs=[HBM] * (2 * n),
        out_shape=[pltpu.HBM(a.shape, a.dtype) for a in parts] + [pltpu.HBM(a.shape, a.dtype) for a in lands],
        input_output_aliases={i: i for i in range(2 * n)},
        compiler_params=pltpu.CompilerParams(has_side_effects=pltpu.SideEffectType.DATAFLOW_SIDE_EFFECTING),
    )(*parts, *lands, send_sems, recv_sems, after)
    return list(outs[:n]), list(outs[n:])


def _chip_sum(name, parts, lands, sums, place, layer):
    n = len(parts)
    tiles = [a.shape[-2] // REDUCE_STEPS for a in parts]

    def body(place_ref, *refs):
        own, got, outs = refs[:n], refs[n:2 * n], refs[3 * n:]
        for a in range(n):
            tot = own[a][...].astype(F32)
            for j in range(N_CHIPS - 1):
                tot = tot + got[a][j].astype(F32)
            outs[a][...] = tot

    own_specs = [pl.BlockSpec((None, t, a.shape[-1]), lambda i, p: (p[0], i, 0)) for a, t in zip(parts, tiles)]
    got_specs = [pl.BlockSpec((N_CHIPS - 1, t, a.shape[-1]), lambda i, p: (0, i, 0)) for a, t in zip(parts, tiles)]
    out_specs = [pl.BlockSpec((None, t, a.shape[-1]), lambda i, p: (layer, p[1] * REDUCE_STEPS + i, 0))
                 for a, t in zip(parts, tiles)]
    return pl.pallas_call(
        body, name=name,
        grid_spec=pltpu.PrefetchScalarGridSpec(num_scalar_prefetch=1, grid=(REDUCE_STEPS,),
                                               in_specs=own_specs + got_specs + [ANY] * n, out_specs=out_specs),
        out_shape=[jax.ShapeDtypeStruct(a.shape, F32) for a in sums],
        input_output_aliases={1 + 2 * n + a: a for a in range(n)},
        compiler_params=_params(1),
    )(place, *parts, *lands, *sums)


def _half_fill(name, sums, layer):
    n = len(sums)

    def body(*refs):
        ins, outs = refs[:n], refs[n:2 * n]
        send_sems, recv_sems = refs[2 * n:]
        x, y, c, _ = _place()
        copies = []
        for a in range(n):
            h = _row_half(sums[a])
            copies.append(pltpu.make_async_remote_copy(
                src_ref=ins[a].at[layer, pl.ds(c * h, h), :], dst_ref=outs[a].at[layer, pl.ds(c * h, h), :],
                send_sem=send_sems.at[a], recv_sem=recv_sems.at[a], device_id=(x, y, 1 - c), device_id_type=MESH))
        for cp in copies:
            cp.start()
        for a in range(n):
            h = _row_half(sums[a])
            theirs = outs[a].at[layer, pl.ds((1 - c) * h, h), :]
            pltpu.make_async_remote_copy(src_ref=theirs, dst_ref=theirs, send_sem=send_sems.at[a], recv_sem=recv_sems.at[a],
                                         device_id=(x, y, 1 - c), device_id_type=MESH).wait_recv()
        for cp in copies:
            cp.wait_send()

    return pl.pallas_call(
        body, name=name,
        in_specs=[ANY] * n, out_specs=[ANY] * n,
        out_shape=[jax.ShapeDtypeStruct(a.shape, a.dtype) for a in sums],
        input_output_aliases={a: a for a in range(n)},
        scratch_shapes=[pltpu.SemaphoreType.DMA((n,)), pltpu.SemaphoreType.DMA((n,))],
        compiler_params=pltpu.CompilerParams(has_side_effects=True),
    )(*sums)


def _adamw_math(w, g, m, v):
    m = ADAM_B1 * m + (1.0 - ADAM_B1) * g
    v = ADAM_B2 * v + (1.0 - ADAM_B2) * (g * g)
    m_hat = m / (1.0 - ADAM_B1 ** ADAM_STEP)
    v_hat = v / (1.0 - ADAM_B2 ** ADAM_STEP)
    delta = -ADAM_LR * (m_hat / (jnp.sqrt(v_hat) + ADAM_EPS) + ADAM_WD * w)
    return delta, m, v


def _adamw(name, ws, gs, ms, vs, layer, prev=None):
    n = len(ws)
    steps = 8
    tiles = [a.shape[-2] // steps for a in ws]

    def body(*refs):
        w_r, g_r, m_r, v_r = refs[:n], refs[n:2 * n], refs[2 * n:3 * n], refs[3 * n:4 * n]
        g_o, d_o, m_o, v_o = (refs[-4 * n:][k * n:(k + 1) * n] for k in range(4))
        for a in range(n):
            g = g_r[a][...]
            d, m, v = _adamw_math(w_r[a][...], g, m_r[a][...], v_r[a][...])
            g_o[a][...] = g
            d_o[a][...] = d
            m_o[a][...] = m
            v_o[a][...] = v

    specs = [pl.BlockSpec((None, t, a.shape[-1]), lambda i: (layer, i, 0)) for a, t in zip(ws, tiles)]
    held = [] if prev is None else [buf for kind in zip(*prev) for buf in kind]
    outs = pl.pallas_call(
        body, name=name, grid=(steps,),
        in_specs=specs * 4 + [ANY] * len(held), out_specs=specs * 4,
        out_shape=[jax.ShapeDtypeStruct(a.shape, F32) for a in ws] * 4,
        input_output_aliases={4 * n + k: k for k in range(len(held))},
        compiler_params=_params(1),
    )(*ws, *gs, *ms, *vs, *held)
    return [[outs[k * n + a] for k in range(4)] for a in range(n)]


SMALL_W = 512


def _small_allreduce_adamw(g, w, m, v, cw_w, cw_m, cw_v, cw_row0):
    R = g.shape[0]
    n_l = cw_w.shape[0]

    def body(g_ref, w_ref, m_ref, v_ref, cww_ref, cwm_ref, cwv_ref,
             gs_ref, d_ref, mo_ref, vo_ref, cg_ref, cd_ref, cmo_ref, cvo_ref,
             slots_ref, send_sems, recv_sems):
        x, y, c, _ = _place()
        me = 4 * x + 2 * y + c
        slots_ref[me] = g_ref[...]
        sends = []
        for d in range(1, 8):
            px, py, pc = x ^ (d >> 2), y ^ ((d >> 1) & 1), c ^ (d & 1)
            cp = pltpu.make_async_remote_copy(src_ref=g_ref, dst_ref=slots_ref.at[me], send_sem=send_sems.at[d - 1],
                                              recv_sem=recv_sems.at[d - 1], device_id=(px, py, pc), device_id_type=MESH)
            cp.start()
            sends.append(cp)
        for d in range(1, 8):
            px, py, pc = x ^ (d >> 2), y ^ ((d >> 1) & 1), c ^ (d & 1)
            slot = slots_ref.at[4 * px + 2 * py + pc]
            pltpu.make_async_remote_copy(src_ref=slot, dst_ref=slot, send_sem=send_sems.at[d - 1],
                                         recv_sem=recv_sems.at[d - 1], device_id=(px, py, pc),
                                         device_id_type=MESH).wait_recv()
        for cp in sends:
            cp.wait_send()
        tot = slots_ref[0]
        for k in range(1, 8):
            tot = tot + slots_ref[k]
        gs_ref[...] = tot
        dl, mn, vn = _adamw_math(w_ref[...], tot, m_ref[...], v_ref[...])
        d_ref[...] = dl
        mo_ref[...] = mn
        vo_ref[...] = vn
        chip = 2 * x + y
        for l in range(n_l):
            rows = tot[cw_row0[l]:cw_row0[l] + CONV_HALO, :]
            mine = rows[:, 0:128]
            for k in range(1, N_CHIPS):
                mine = jnp.where(chip == k, rows[:, 128 * k:128 * (k + 1)], mine)
            cg_ref[l] = mine
            dl, mn, vn = _adamw_math(cww_ref[l], mine, cwm_ref[l], cwv_ref[l])
            cd_ref[l] = dl
            cmo_ref[l] = mn
            cvo_ref[l] = vn

    vm = pl.BlockSpec(memory_space=pltpu.VMEM)
    small = jax.ShapeDtypeStruct((R, SMALL_W), F32)
    conv = jax.ShapeDtypeStruct(cw_w.shape, F32)
    return pl.pallas_call(
        body, name="small_allreduce_adamw",
        in_specs=[vm] * 7, out_specs=[vm] * 8,
        out_shape=[small] * 4 + [conv] * 4,
        scratch_shapes=[pltpu.VMEM((8, R, SMALL_W), F32), pltpu.SemaphoreType.DMA((7,)), pltpu.SemaphoreType.DMA((7,))],
        compiler_params=pltpu.CompilerParams(has_side_effects=True, vmem_limit_bytes=VMEM_LIMIT),
    )(g, w, m, v, cw_w, cw_m, cw_v)


SMALL_LAYOUT = (("conv_w", CONV_HALO), ("norm1_g", 2), ("norm2_g", 2), ("conv_b", 1), ("conv_ln_g", 1),
                ("conv_ln_b", 1), ("q_norm_g", 1), ("k_norm_g", 1), ("b_f", 1))
SMALL_ROWS = sum(r for _, r in SMALL_LAYOUT)
SMALL_ROWS_PAD = 48
LOSS_ROW = SMALL_ROWS


def _pack_small(per_layer):
    flat = []
    for d in per_layer:
        for name, r in SMALL_LAYOUT:
            n = r * SMALL_W
            a = d.get(name)
            if a is None:
                flat.append(jnp.zeros((n,), F32))
                continue
            flat.append(a.reshape(-1))
            if a.size < n:
                flat.append(jnp.zeros((n - a.size,), F32))
        spare = (SMALL_ROWS_PAD - SMALL_ROWS) * SMALL_W
        if "spare" in d:
            flat.append(d["spare"].reshape(-1))
            spare -= d["spare"].size
        flat.append(jnp.zeros((spare,), F32))
    return jnp.concatenate(flat).reshape(-1, SMALL_W)


def _unpack_small(packed, name, size):
    row0 = 0
    for nm, r in SMALL_LAYOUT:
        if nm == name:
            break
        row0 += r
    per_layer = packed.reshape(-1, SMALL_ROWS_PAD * SMALL_W)
    return per_layer[:, row0 * SMALL_W:row0 * SMALL_W + size]


SMALL_SIZES = dict(norm1_g=D_MODEL, norm2_g=D_MODEL, conv_b=CONV_CH, conv_ln_g=CONV_CH, conv_ln_b=CONV_CH,
                   q_norm_g=HEAD_DIM, k_norm_g=HEAD_DIM, b_f=ATT_HEADS)
SMALL_KEYS = dict(norm1_g="g1", norm2_g="g2", conv_b="cvb", conv_ln_g="lng", conv_ln_b="lnb",
                  q_norm_g="qg", k_norm_g="kg", b_f="bf", conv_w="cw")
CONV_W_ROW0 = 0


def kernel(x, norm1_g, w_in, b_f, q_norm_g, k_norm_g, conv_w, conv_b, conv_ln_g, conv_ln_b, w_o, norm2_g, w_mlp_in, w_mlp_out, loss_target, m_norm1_g, m_w_in, m_b_f, m_q_norm_g, m_k_norm_g, m_conv_w, m_conv_b, m_conv_ln_g, m_conv_ln_b, m_w_o, m_norm2_g, m_w_mlp_in, m_w_mlp_out, v_norm1_g, v_w_in, v_b_f, v_q_norm_g, v_k_norm_g, v_conv_w, v_conv_b, v_conv_ln_g, v_conv_ln_b, v_w_o, v_norm2_g, v_w_mlp_in, v_w_mlp_out):
    n_l = w_in.shape[0]

    per_layer = lambda l: [w_in[l].astype(BF16), conv_w[l], w_o[l].astype(BF16), w_mlp_in[l].astype(BF16),
                           w_mlp_out[l].astype(BF16)]
    n_w = len(per_layer(0))
    send_sems, recv_sems, srcs, lands, token = _gather_start([a for l in range(n_l) for a in per_layer(l)])

    def layer_weights(l):
        def wait(tag, which, after):
            groups = [n_w * l + i for i in which]
            return _gather_wait(f"gather_wait_{tag}{l}", groups, send_sems, recv_sems,
                                [srcs[g] for g in groups], [lands[g] for g in groups], after)

        def early(after):
            g_in, g_cw = wait("a", (0, 1), token if l == 0 else after)
            win = _win_to_internal(jnp.concatenate([g_in[k] for k in range(N_CHIPS)], axis=-1))
            cw = jnp.concatenate([g_cw[k] for k in range(N_CHIPS)], axis=-1)
            return _layer_params(l, win, cw, norm1_g, b_f, q_norm_g, k_norm_g, conv_b, conv_ln_g, conv_ln_b, norm2_g)

        def late(after):
            wo, w1, w2 = wait("b", (2, 3, 4), after)
            return dict(wo=wo, w1=w1, w2=w2)

        return early, late

    place = jnp.stack([2 * lax.axis_index("x") + lax.axis_index("y"), lax.axis_index("c")]).astype(jnp.int32)
    big_w = [w_in, w_o, w_mlp_in, w_mlp_out]
    pending = []

    def reduce(l):
        def group(tag, grads):
            which, gs = list(grads), list(grads.values())
            got = _half_swap(f"half_swap_{tag}{l}", gs)
            parts = _half_add(f"half_add_{tag}{l}", gs, got, place)
            send, recv, parts, lands, token = _exchange_start(f"exchange_start_{tag}{l}", parts)
            pending.append((f"{tag}{l}", l, which, send, recv, parts, lands))
            return token
        return group

    loss, dx, smalls = _local_step(x[0], loss_target[0], [layer_weights(l) for l in range(n_l)], reduce)
    sums = [lax.empty(w.shape, F32) for w in big_w]
    big_m = [m_w_in, m_w_o, m_w_mlp_in, m_w_mlp_out]
    big_v = [v_w_in, v_w_o, v_w_mlp_in, v_w_mlp_out]
    updated = [None] * len(big_w)
    after = dx
    for tag, l, which, send, recv, parts, lands in pending:
        parts, lands = _exchange_wait(f"exchange_wait_{tag}", send, recv, parts, lands, after)
        done = _chip_sum(f"chip_sum_{tag}", parts, lands, [sums[i] for i in which], place, l)
        done = _half_fill(f"half_fill_{tag}", done, l)
        for i, a in zip(which, done):
            sums[i] = a
        prev = None if updated[which[0]] is None else [updated[i] for i in which]
        new = _adamw(f"adamw_{tag}", [big_w[i] for i in which], done, [big_m[i] for i in which],
                     [big_v[i] for i in which], l, prev)
        for i, r in zip(which, new):
            updated[i] = r
        after = new[0][1]
    g_big, d_big, nm_big, nv_big = ([r[k] for r in updated] for k in range(4))

    env = dict(norm1_g=(norm1_g, m_norm1_g, v_norm1_g), norm2_g=(norm2_g, m_norm2_g, v_norm2_g),
               conv_b=(conv_b, m_conv_b, v_conv_b), conv_ln_g=(conv_ln_g, m_conv_ln_g, v_conv_ln_g),
               conv_ln_b=(conv_ln_b, m_conv_ln_b, v_conv_ln_b), q_norm_g=(q_norm_g, m_q_norm_g, v_q_norm_g),
               k_norm_g=(k_norm_g, m_k_norm_g, v_k_norm_g), b_f=(b_f, m_b_f, v_b_f))
    g_dicts = [{nm: s[key] for nm, key in SMALL_KEYS.items()} for s in smalls]
    g_dicts[0]["spare"] = loss
    g_pack = _pack_small(g_dicts)
    packs = [_pack_small([{nm: env[nm][t][l] for nm in env} for l in range(n_l)]) for t in range(3)]
    pad_cw = lambda a: jnp.pad(a, ((0, 0), (0, CONV_HALO - CONV_TAPS), (0, 0)))
    cw_row0 = tuple(l * SMALL_ROWS_PAD + CONV_W_ROW0 for l in range(n_l))
    gs, ds, ms, vs, cg, cd, cm, cv = _small_allreduce_adamw(
        g_pack, packs[0], packs[1], packs[2], pad_cw(conv_w), pad_cw(m_conv_w), pad_cw(v_conv_w), cw_row0)

    def small_out(packed, conv):
        o = {nm: _unpack_small(packed, nm, sz) for nm, sz in SMALL_SIZES.items()}
        o["conv_w"] = conv[:, 0:CONV_TAPS, :]
        return o

    def ordered(small, big):
        return (small["norm1_g"], big[0], small["b_f"], small["q_norm_g"], small["k_norm_g"], small["conv_w"],
                small["conv_b"], small["conv_ln_g"], small["conv_ln_b"], big[1], small["norm2_g"], big[2], big[3])

    return (gs[LOSS_ROW, 0], dx[None],
            *ordered(small_out(gs, cg), g_big), *ordered(small_out(ds, cd), d_big),
            *ordered(small_out(ms, cm), nm_big), *ordered(small_out(vs, cv), nv_big))
```

```python
import jax
import jax.numpy as jnp
from jax import lax
from jax.experimental import pallas as pl
from jax.experimental.pallas import tpu as pltpu

F32 = jnp.float32
BF16 = jnp.bfloat16

D_MODEL = 1024
ATT_HEADS = 8
HEAD_DIM = 64
ATT_W = ATT_HEADS * HEAD_DIM
CONV_CH = 512
CONV_TAPS = 31
CONV_HALO = 32
D_FF = 4 * D_MODEL
N_IN = 3 * ATT_W + ATT_HEADS + 2 * CONV_CH
O_Q, O_K, O_V, O_A, O_G, O_F = 0, 512, 1024, 1536, 2048, 2560
N_INP = O_F + 128
EPS = 1e-6
QK_SCALE = 0.125

ADAM_LR = 0.001
ADAM_B1 = 0.9
ADAM_B2 = 0.999
ADAM_EPS = 1e-08
ADAM_WD = 0.01
ADAM_STEP = 10

N_CHIPS = 4
VMEM_LIMIT = 52 * 1024 * 1024
MESH = pl.DeviceIdType.MESH
ANY = pl.BlockSpec(memory_space=pl.ANY)


def _params(n_axes, **kw):
    return pltpu.CompilerParams(dimension_semantics=("arbitrary",) * n_axes,
                                vmem_limit_bytes=VMEM_LIMIT, **kw)


def _dot(a, b):
    return jnp.dot(a, b, preferred_element_type=F32)


def _dot_nt(a, b):
    return lax.dot_general(a, b, (((1,), (1,)), ((), ())), preferred_element_type=F32)


def _dot_tn(a, b):
    return lax.dot_general(a, b, (((0,), (0,)), ((), ())), preferred_element_type=F32)


def _split3(a):
    a1 = a.astype(BF16)
    r = a - a1.astype(F32)
    a2 = r.astype(BF16)
    a3 = (r - a2.astype(F32)).astype(BF16)
    return a1, a2, a3


def _dot_hi_r(a, b_exact):
    return sum(_dot(p, b_exact) for p in _split3(a))


def _head_sums(a, blockdiag):
    a1 = a.astype(BF16)
    a2 = (a - a1.astype(F32)).astype(BF16)
    return _dot(a1, blockdiag) + _dot(a2, blockdiag)


def _dot_hi_l(a_exact, b):
    return sum(_dot(a_exact, p) for p in _split3(b))


def _sigmoid(x):
    return 1.0 / (1.0 + jnp.exp(-x))


def _head_blockdiag():
    i = jnp.arange(ATT_W) // HEAD_DIM
    return (i[:, None] == i[None, :]).astype(BF16)


AUG_LANES = 8


def _aug_place(first):
    piece = jnp.arange(3 * 128)[:, None] // 128
    h = jnp.arange(3 * 128)[:, None] % 128
    lane = jnp.arange(ATT_W)[None, :]
    return ((h < ATT_HEADS) & (lane == 128 * (h // 2) + AUG_LANES * (h % 2) + first + piece)).astype(BF16)


def _aug_ones(first):
    lane = jnp.arange(ATT_W) % 128
    pos = lane % AUG_LANES
    return ((lane < 2 * AUG_LANES) & (pos >= first) & (pos < first + 3)).astype(F32).reshape(1, ATT_W)


def _head_rows():
    h = jnp.arange(2 * ATT_HEADS)[:, None]
    i = jnp.arange(ATT_W)[None, :] // HEAD_DIM
    return (h == i).astype(BF16)


def _head_fold():
    i = jnp.arange(ATT_W)[:, None] % HEAD_DIM
    j = jnp.arange(128)[None, :]
    return (i == j).astype(BF16)


def _head_pick():
    i = jnp.arange(ATT_W)[:, None]
    h = jnp.arange(128)[None, :]
    return (i == h * HEAD_DIM).astype(BF16)


def _tril(n):
    r = jnp.arange(n)
    return (r[:, None] >= r[None, :]).astype(BF16)


SUBLANES = 8


def _fill_row_shifts(buf_ref, shifts_ref, tm):
    n = tm + CONV_HALO - SUBLANES
    for b in range(1, SUBLANES):
        shifts_ref[b - 1, 0:n, :] = buf_ref[pl.ds(b, n), :]


def _row_shifted(buf_ref, shifts_ref, offset, rows, base=0):
    a, b = divmod(offset, SUBLANES)
    start = pl.multiple_of(base + SUBLANES * a, SUBLANES)
    if b == 0:
        return buf_ref[pl.ds(start, rows), :]
    return shifts_ref[b - 1, pl.ds(start, rows), :]


CONV_ROWS = 32


def _mixer_in_fwd(x, g1, win, qg, kg, bf, cw, cvb, lng, lnb):
    S = x.shape[0]
    TM = min(512, S)
    nb = S // TM

    def body(x_ref, g1_ref, win_ref, qg_ref, kg_ref, bf_ref, cw_ref, cvb_ref, lng_ref, lnb_ref,
             bd_ref, tri_ref, pq_ref, pk_ref, oq_ref, ok_ref,
             u1_ref, proj_ref, q_ref, k_ref, v_ref, qa_ref, ka_ref, h0_ref, h1_ref, h3_ref,
             carry_ref, hbuf_ref, hs_ref):
        i = pl.program_id(0)

        @pl.when(i == 0)
        def _():
            carry_ref[...] = jnp.zeros_like(carry_ref)
            hbuf_ref[0:CONV_HALO, :] = jnp.zeros((CONV_HALO, CONV_CH), F32)

        @pl.when(i > 0)
        def _():
            hbuf_ref[0:CONV_HALO, :] = hbuf_ref[TM:TM + CONV_HALO, :]

        xv = x_ref[...]
        r = lax.rsqrt(jnp.mean(xv * xv, axis=-1, keepdims=True) + EPS)
        u = (xv * r * g1_ref[...]).astype(BF16)
        u1_ref[...] = u
        proj_ref[...] = _dot(u, win_ref[...])

        def headnorm(raw, gain):
            ss = _head_sums(raw * raw, bd_ref[...]) * (1.0 / HEAD_DIM)
            return raw * lax.rsqrt(ss + EPS) * gain

        q_ref[...] = (headnorm(proj_ref[:, O_Q:O_Q + ATT_W], qg_ref[...]) * QK_SCALE).astype(BF16)
        k_ref[...] = headnorm(proj_ref[:, O_K:O_K + ATT_W], kg_ref[...]).astype(BF16)
        v_ref[...] = proj_ref[:, O_V:O_V + ATT_W].astype(BF16)

        zf = proj_ref[:, O_F:O_F + 128] + bf_ref[...]
        logf = jnp.minimum(zf, 0.0) - jnp.log(1.0 + jnp.exp(-jnp.abs(zf)))
        lane = lax.broadcasted_iota(jnp.int32, (TM, 128), 1)
        logf = jnp.where(lane < ATT_HEADS, logf, 0.0)
        c8 = _dot_hi_l(tri_ref[...], logf) + carry_ref[...]
        carry_ref[...] = c8[TM - 1:TM, :]
        pieces = jnp.concatenate(_split3(c8), axis=1)
        qa_ref[...] = (_dot(pieces, pq_ref[...]) + oq_ref[...]).astype(BF16)
        ka_ref[...] = (ok_ref[...] - _dot(pieces, pk_ref[...])).astype(BF16)

        h0 = proj_ref[:, O_A:O_A + CONV_CH] * _sigmoid(proj_ref[:, O_G:O_G + CONV_CH])
        h0_ref[...] = h0
        hbuf_ref[CONV_HALO:CONV_HALO + TM, :] = h0
        _fill_row_shifts(hbuf_ref, hs_ref, TM)
        acc = jnp.zeros((TM, CONV_CH), F32) + cvb_ref[...]
        for j in range(CONV_TAPS):
            acc = acc + cw_ref[j:j + 1, :] * _row_shifted(hbuf_ref, hs_ref, CONV_HALO - CONV_TAPS + 1 + j, TM)
        h1_ref[...] = acc
        mu = jnp.mean(acc, axis=-1, keepdims=True)
        d = acc - mu
        var = jnp.mean(d * d, axis=-1, keepdims=True)
        h2 = d * lax.rsqrt(var + EPS) * lng_ref[...] + lnb_ref[...]
        h3_ref[...] = (h2 * _sigmoid(h2)).astype(BF16)

    row = lambda w: pl.BlockSpec((TM, w), lambda i: (i, 0))
    full = lambda a: pl.BlockSpec(a.shape, lambda i: (0,) * a.ndim)
    ins = (x, g1, win, qg, kg, bf, cw, cvb, lng, lnb, _head_blockdiag(), _tril(TM),
           _aug_place(0), _aug_place(3), _aug_ones(3), _aug_ones(0))
    return pl.pallas_call(
        body, name="mixer_in_fwd", grid=(nb,),
        in_specs=[row(D_MODEL)] + [full(a) for a in ins[1:]],
        out_specs=[row(D_MODEL), row(N_INP), row(ATT_W), row(ATT_W), row(ATT_W), row(ATT_W), row(ATT_W),
                   row(CONV_CH), row(CONV_CH), row(CONV_CH)],
        out_shape=[jax.ShapeDtypeStruct((S, D_MODEL), BF16),
                   jax.ShapeDtypeStruct((S, N_INP), F32),
                   jax.ShapeDtypeStruct((S, ATT_W), BF16),
                   jax.ShapeDtypeStruct((S, ATT_W), BF16),
                   jax.ShapeDtypeStruct((S, ATT_W), BF16),
                   jax.ShapeDtypeStruct((S, ATT_W), BF16),
                   jax.ShapeDtypeStruct((S, ATT_W), BF16),
                   jax.ShapeDtypeStruct((S, CONV_CH), F32),
                   jax.ShapeDtypeStruct((S, CONV_CH), F32),
                   jax.ShapeDtypeStruct((S, CONV_CH), BF16)],
        scratch_shapes=[pltpu.VMEM((1, 128), F32), pltpu.VMEM((TM + CONV_HALO, CONV_CH), F32),
                        pltpu.VMEM((SUBLANES - 1, TM + CONV_HALO, CONV_CH), F32)],
        compiler_params=_params(1),
    )(*ins)


def _pair_heads(lo, alo, x, xa):
    z = jnp.zeros_like(x)
    return (jnp.concatenate([jnp.where(lo, x, z), jnp.where(alo, xa, z)], axis=1),
            jnp.concatenate([jnp.where(lo, z, x), jnp.where(alo, z, xa)], axis=1))


def _attn_fwd(q, qa, k, ka, v):
    S = q.shape[0]
    T = min(1024, S)
    nq = S // T

    def body(q_ref, qa_ref, k_ref, ka_ref, v_ref, o_ref, lse_ref):
        qi = pl.program_id(1)
        lane = lax.broadcasted_iota(jnp.int32, (T, 128), 1)
        lo = lane < HEAD_DIM
        qm = _pair_heads(lo, lane < AUG_LANES, q_ref[...], qa_ref[...])
        tril = (lax.broadcasted_iota(jnp.int32, (T, T), 0) >= lax.broadcasted_iota(jnp.int32, (T, T), 1))

        def step(kj, carry, masked):
            off = pl.multiple_of(kj * T, T)
            kb = jnp.concatenate([k_ref[pl.ds(off, T), :], ka_ref[pl.ds(off, T), :]], axis=1)
            vb = v_ref[pl.ds(off, T), :]
            new = []
            for h in range(2):
                m, l, acc = carry[3 * h:3 * h + 3]
                s = _dot_nt(qm[h], kb)
                if masked:
                    s = jnp.where(tril, s, -1e30)
                m_new = jnp.maximum(m, jnp.max(s, axis=-1, keepdims=True))
                alpha = jnp.exp(m - m_new)
                p = jnp.exp(s - m_new)
                l = alpha * l + jnp.sum(p, axis=-1, keepdims=True)
                acc = alpha * acc + _dot(p.astype(BF16), vb)
                new += [m_new, l, acc]
            return tuple(new)

        init = (jnp.full((T, 1), -1e30, F32), jnp.zeros((T, 1), F32), jnp.zeros((T, 128), F32)) * 2
        carry = lax.fori_loop(0, qi, lambda kj, c: step(kj, c, False), init)
        m0, l0, a0, m1, l1, a1 = step(qi, carry, True)
        o_ref[...] = jnp.where(lo, a0 / l0, a1 / l1).astype(BF16)
        lse_t = jnp.where(lo, m0 + jnp.log(l0), m1 + jnp.log(l1)).T
        lse_ref[0:1, :] = lse_t[0:1, :]
        lse_ref[1:2, :] = lse_t[HEAD_DIM:HEAD_DIM + 1, :]

    qblk = pl.BlockSpec((T, 128), lambda hp, qi: (qi, hp))
    seq = pl.BlockSpec((S, 128), lambda hp, qi: (0, hp))
    return pl.pallas_call(
        body, name="attn_fwd", grid=(ATT_HEADS // 2, nq),
        in_specs=[qblk, qblk, seq, seq, seq],
        out_specs=[qblk, pl.BlockSpec((None, 2, T), lambda hp, qi: (hp, 0, qi))],
        out_shape=[jax.ShapeDtypeStruct((S, ATT_W), BF16),
                   jax.ShapeDtypeStruct((ATT_HEADS // 2, 2, S), F32)],
        compiler_params=_params(2),
    )(q, qa, k, ka, v)


def _attn_bwd(q, qa, k, ka, v, do, lse, dlt):
    S = q.shape[0]
    T = min(512, S)
    nq = S // T

    def body(q_ref, qa_ref, do_ref, lse_ref, dlt_ref, k_ref, ka_ref, v_ref, dq_ref, dk_ref, dv_ref, rows_ref, cols_ref):
        kj = pl.program_id(1)

        @pl.when(kj == 0)
        def _():
            dq_ref[...] = jnp.zeros_like(dq_ref)
            rows_ref[...] = jnp.zeros_like(rows_ref)

        lane = lax.broadcasted_iota(jnp.int32, (T, 128), 1)
        lo = lane < HEAD_DIM
        alo = lane < AUG_LANES
        triu = (lax.broadcasted_iota(jnp.int32, (T, T), 0) <= lax.broadcasted_iota(jnp.int32, (T, T), 1))
        kb = k_ref[...]
        kaug = jnp.concatenate([kb, ka_ref[...]], axis=1)
        vb = v_ref[...]

        def step(qi, carry, masked):
            off = pl.multiple_of(qi * T, T)
            qb = q_ref[pl.ds(off, T), :]
            dob = do_ref[pl.ds(off, T), :]
            qm = _pair_heads(lo, alo, qb, qa_ref[pl.ds(off, T), :])
            zero = jnp.zeros_like(qb)
            new, dqs = [], []
            for h in range(2):
                dk_a, dv_a, dc_a = carry[3 * h:3 * h + 3]
                dom = jnp.where(lo, dob, zero) if h == 0 else jnp.where(lo, zero, dob)
                s = _dot_nt(kaug, qm[h])
                if masked:
                    s = jnp.where(triu, s, -1e30)
                p = jnp.exp(s - lse_ref[h:h + 1, pl.ds(off, T)])
                dp = _dot_nt(vb, dom)
                ds = p * (dp - dlt_ref[h:h + 1, pl.ds(off, T)])
                pb = p.astype(BF16)
                dsb = ds.astype(BF16)
                dv_a = dv_a + _dot(pb, dob)
                dk_a = dk_a + _dot(dsb, qb)
                dc_a = dc_a + jnp.sum(ds, axis=1, keepdims=True)
                dqs.append(_dot_tn(dsb, kb))
                rows_ref[h:h + 1, pl.ds(off, T)] += jnp.sum(ds, axis=0, keepdims=True)
                new += [dk_a, dv_a, dc_a]
            dq_ref[pl.ds(off, T), :] += jnp.where(lo, dqs[0], dqs[1])
            return tuple(new)

        init = (jnp.zeros((T, 128), F32), jnp.zeros((T, 128), F32), jnp.zeros((T, 1), F32)) * 2
        carry = step(kj, init, True)
        carry = lax.fori_loop(kj + 1, nq, lambda qi, c: step(qi, c, False), carry)
        dk_ref[...] = jnp.where(lo, carry[0], carry[3])
        dv_ref[...] = jnp.where(lo, carry[1], carry[4])
        cols_ref[...] = -jnp.where(lo, carry[2], carry[5])

    seq = pl.BlockSpec((S, 128), lambda hp, kj: (0, hp))
    rows = pl.BlockSpec((None, 2, S), lambda hp, kj: (hp, 0, 0))
    kblk = pl.BlockSpec((T, 128), lambda hp, kj: (kj, hp))
    return pl.pallas_call(
        body, name="attn_bwd", grid=(ATT_HEADS // 2, nq),
        in_specs=[seq, seq, seq, rows, rows, kblk, kblk, kblk],
        out_specs=[seq, kblk, kblk, rows, kblk],
        out_shape=[jax.ShapeDtypeStruct((S, ATT_W), F32), jax.ShapeDtypeStruct((S, ATT_W), F32),
                   jax.ShapeDtypeStruct((S, ATT_W), F32),
                   jax.ShapeDtypeStruct((ATT_HEADS // 2, 2, S), F32),
                   jax.ShapeDtypeStruct((S, ATT_W), F32)],
        compiler_params=_params(2),
    )(q, qa, do, lse, dlt, k, ka, v)


def _wo_spec(wo4):
    return pl.BlockSpec(wo4.shape, lambda i: (0, 0, 0))


def _wo_halves(wo_ref):
    half = N_CHIPS // 2
    return (wo_ref[0:half].reshape(ATT_W, D_MODEL), wo_ref[half:N_CHIPS].reshape(CONV_CH, D_MODEL))


def _mixer_out_fwd(x, att, h3, wo4, g2):
    S = x.shape[0]
    TM = min(512, S)

    def body(x_ref, att_ref, h3_ref, wo_ref, g2_ref, x2_ref, u2_ref):
        wa, wc = _wo_halves(wo_ref)
        x2 = x_ref[...] + _dot(att_ref[...], wa) + _dot(h3_ref[...], wc)
        x2_ref[...] = x2
        r = lax.rsqrt(jnp.mean(x2 * x2, axis=-1, keepdims=True) + EPS)
        u2_ref[...] = (x2 * r * g2_ref[...]).astype(BF16)

    row = lambda w: pl.BlockSpec((TM, w), lambda i: (i, 0))
    full = lambda a: pl.BlockSpec(a.shape, lambda i: (0,) * a.ndim)
    return pl.pallas_call(
        body, name="mixer_out_fwd", grid=(S // TM,),
        in_specs=[row(D_MODEL), row(ATT_W), row(CONV_CH), _wo_spec(wo4), full(g2)],
        out_specs=[row(D_MODEL), row(D_MODEL)],
        out_shape=[jax.ShapeDtypeStruct((S, D_MODEL), F32), jax.ShapeDtypeStruct((S, D_MODEL), BF16)],
        compiler_params=_params(1),
    )(x, att, h3, wo4, g2)


def _mlp_w_specs():
    return [pl.BlockSpec((None, D_MODEL, D_FF // N_CHIPS), lambda i, f: (f, 0, 0)),
            pl.BlockSpec((None, D_FF // N_CHIPS, D_MODEL), lambda i, f: (f, 0, 0))]


def _mlp_fwd(x2, u2, w1, w2, target=None):
    S = x2.shape[0]
    head = target is not None
    TM = min(1024, S)
    TF = 1024
    nf = D_FF // TF

    def body(*refs):
        x2_ref, u2_ref, w1_ref, w2_ref = refs[:4]
        x3_ref, z_ref, hh_ref = refs[4 + head:7 + head]
        i = pl.program_id(0)
        f = pl.program_id(1)

        @pl.when(f == 0)
        def _():
            x3_ref[...] = x2_ref[...]

        z = _dot(u2_ref[...], w1_ref[...])
        z_ref[...] = z
        zr = jnp.maximum(z, 0.0)
        hh = (zr * zr).astype(BF16)
        hh_ref[...] = hh
        x3_ref[...] += _dot(hh, w2_ref[...])

        if head:
            t_ref, loss_ref = refs[4], refs[8]

            @pl.when((i == 0) & (f == 0))
            def _():
                loss_ref[...] = jnp.zeros_like(loss_ref)

            @pl.when(f == nf - 1)
            def _():
                d = x3_ref[...] - t_ref[...]
                x3_ref[...] = d * (1.0 / D_MODEL)
                loss_ref[...] += jnp.sum(d * d)

    rows = pl.BlockSpec((TM, D_MODEL), lambda i, f: (i, 0))
    once = pl.BlockSpec((TM, D_MODEL), lambda i, f: (i, 0), pipeline_mode=pl.Buffered(1))
    tile = pl.BlockSpec((TM, TF), lambda i, f: (i, f))
    return pl.pallas_call(
        body, name="mlp_fwd_loss" if head else "mlp_fwd", grid=(S // TM, nf),
        in_specs=[once if head else rows, rows] + _mlp_w_specs() + [once] * head,
        out_specs=[rows, tile, tile] + [pl.BlockSpec((8, 128), lambda i, f: (0, 0))] * head,
        out_shape=[jax.ShapeDtypeStruct((S, D_MODEL), F32), jax.ShapeDtypeStruct((S, D_FF), F32),
                   jax.ShapeDtypeStruct((S, D_FF), BF16)] + [jax.ShapeDtypeStruct((8, 128), F32)] * head,
        compiler_params=_params(2),
    )(x2, u2, w1, w2, *([target] if head else []))


def _mlp_bwd(dx3, z, x2, g2, w1, w2):
    S = dx3.shape[0]
    TM = min(1024, S)
    TF = 1024
    nf = D_FF // TF

    def body(dx3_ref, z_ref, x2_ref, g2_ref, w1_ref, w2_ref, dz_ref, dx2_ref, dg2_ref, du2_ref):
        i = pl.program_id(0)
        f = pl.program_id(1)

        @pl.when((i == 0) & (f == 0))
        def _():
            dg2_ref[...] = jnp.zeros_like(dg2_ref)

        @pl.when(f == 0)
        def _():
            du2_ref[...] = jnp.zeros_like(du2_ref)

        dhh = _dot_nt(dx3_ref[...].astype(BF16), w2_ref[...])
        dz = (dhh * (2.0 * jnp.maximum(z_ref[...], 0.0))).astype(BF16)
        dz_ref[...] = dz
        du2_ref[...] += _dot_nt(dz, w1_ref[...])

        @pl.when(f == nf - 1)
        def _():
            x2 = x2_ref[...]
            r = lax.rsqrt(jnp.mean(x2 * x2, axis=-1, keepdims=True) + EPS)
            n = x2 * r
            du2 = du2_ref[...]
            t = du2 * g2_ref[...]
            dx2_ref[...] = dx3_ref[...] + r * (t - n * jnp.mean(t * n, axis=-1, keepdims=True))
            dg2_ref[0:1, :] += jnp.sum(du2 * n, axis=0, keepdims=True)

    rowi = pl.BlockSpec((TM, D_MODEL), lambda i, f: (i, 0))
    held = pl.BlockSpec((TM, D_MODEL), lambda i, f: (i, 0), pipeline_mode=pl.Buffered(1))
    return pl.pallas_call(
        body, name="mlp_bwd", grid=(S // TM, nf),
        in_specs=[held, pl.BlockSpec((TM, TF), lambda i, f: (i, f)), held,
                  pl.BlockSpec((1, D_MODEL), lambda i, f: (0, 0))] + _mlp_w_specs(),
        out_specs=[pl.BlockSpec((TM, TF), lambda i, f: (i, f)), rowi, pl.BlockSpec((8, D_MODEL), lambda i, f: (0, 0))],
        out_shape=[jax.ShapeDtypeStruct((S, D_FF), BF16), jax.ShapeDtypeStruct((S, D_MODEL), F32),
                   jax.ShapeDtypeStruct((8, D_MODEL), F32)],
        scratch_shapes=[pltpu.VMEM((TM, D_MODEL), F32)],
        compiler_params=_params(2),
    )(dx3, z, x2, g2, w1, w2)


def _matmul_tn(a, b, col_shards=1):
    S, I = a.shape
    J = b.shape[1]
    TI = min(I, 1024)
    TJ = 1024 if J % 1024 == 0 else 896
    TS = min(S, 1024)
    nk = S // TS
    per = J // col_shards // TJ

    def body(a_ref, b_ref, o_ref, acc_ref):
        k = pl.program_id(2)

        @pl.when(k == 0)
        def _():
            acc_ref[...] = jnp.zeros_like(acc_ref)

        acc_ref[...] += _dot_tn(a_ref[...].astype(BF16), b_ref[...].astype(BF16))

        @pl.when(k == nk - 1)
        def _():
            o_ref[...] = acc_ref[...].astype(BF16)

    return pl.pallas_call(
        body, name="matmul_tn", grid=(I // TI, J // TJ, nk),
        in_specs=[pl.BlockSpec((TS, TI), lambda i, j, k: (k, i)), pl.BlockSpec((TS, TJ), lambda i, j, k: (k, j))],
        out_specs=pl.BlockSpec((None, TI, TJ), lambda i, j, k: (j // per, i, j % per)),
        out_shape=jax.ShapeDtypeStruct((col_shards, I, J // col_shards), BF16),
        scratch_shapes=[pltpu.VMEM((TI, TJ), F32)],
        compiler_params=_params(3),
    )(a, b)


def _mixer_out_bwd(dx2, wo4, att, h1, lng, lnb):
    S = dx2.shape[0]
    TM = min(512, S)

    def body(dx2_ref, wo_ref, att_ref, h1_ref, lng_ref, lnb_ref, hr_ref, datt_ref, dlt_ref, dh1_ref, sm_ref):
        @pl.when(pl.program_id(0) == 0)
        def _():
            sm_ref[...] = jnp.zeros_like(sm_ref)

        dxb = dx2_ref[...].astype(BF16)
        wa, wc = _wo_halves(wo_ref)
        datt = _dot_nt(dxb, wa)
        datt_ref[...] = datt.astype(BF16)
        prod = datt * att_ref[...].astype(F32)
        dlt_ref[...] = sum(_dot_nt(hr_ref[...], piece) for piece in _split3(prod))[0:ATT_HEADS, :]
        dh3 = _dot_nt(dxb, wc)
        h1 = h1_ref[...]
        mu = jnp.mean(h1, axis=-1, keepdims=True)
        d = h1 - mu
        rstd = lax.rsqrt(jnp.mean(d * d, axis=-1, keepdims=True) + EPS)
        n = d * rstd
        h2 = n * lng_ref[...] + lnb_ref[...]
        sg = _sigmoid(h2)
        dh2 = dh3 * (sg * (1.0 + h2 * (1.0 - sg)))
        dn = dh2 * lng_ref[...]
        dh1 = rstd * (dn - jnp.mean(dn, axis=-1, keepdims=True) - n * jnp.mean(dn * n, axis=-1, keepdims=True))
        dh1_ref[...] = dh1
        sm_ref[0:1, :] += jnp.sum(dh2 * n, axis=0, keepdims=True)
        sm_ref[1:2, :] += jnp.sum(dh2, axis=0, keepdims=True)
        sm_ref[2:3, :] += jnp.sum(dh1, axis=0, keepdims=True)

    row = lambda w: pl.BlockSpec((TM, w), lambda i: (i, 0))
    full = lambda a: pl.BlockSpec(a.shape, lambda i: (0,) * a.ndim)
    hr = _head_rows()
    return pl.pallas_call(
        body, name="mixer_out_bwd", grid=(S // TM,),
        in_specs=[row(D_MODEL), _wo_spec(wo4), row(ATT_W), row(CONV_CH), full(lng), full(lnb), full(hr)],
        out_specs=[row(ATT_W), pl.BlockSpec((ATT_HEADS, TM), lambda i: (0, i)), row(CONV_CH),
                   pl.BlockSpec((8, CONV_CH), lambda i: (0, 0))],
        out_shape=[jax.ShapeDtypeStruct((S, ATT_W), BF16), jax.ShapeDtypeStruct((ATT_HEADS, S), F32),
                   jax.ShapeDtypeStruct((S, CONV_CH), F32), jax.ShapeDtypeStruct((8, CONV_CH), F32)],
        compiler_params=_params(1),
    )(dx2, wo4, att, h1, lng, lnb, hr)


def _conv_glu_bwd(dh1, h0, proj, cw):
    S = dh1.shape[0]
    TM = min(512, S)
    nb = S // TM
    lead = CONV_HALO - CONV_TAPS + 1

    def body(dh1_ref, dnx_ref, h0_ref, hpv_ref, a_ref, g_ref, cw_ref, dag_ref, dcw_ref,
             dbuf_ref, hbuf_ref, ds_ref, hs_ref, dh0_ref, dcw8_ref):
        i = pl.program_id(0)

        @pl.when(i == 0)
        def _():
            dcw8_ref[...] = jnp.zeros_like(dcw8_ref)

        dbuf_ref[0:TM, :] = dh1_ref[...]
        dbuf_ref[TM:TM + CONV_HALO, :] = jnp.where(i < nb - 1, dnx_ref[0:CONV_HALO, :], 0.0)
        hbuf_ref[0:CONV_HALO, :] = jnp.where(i > 0, hpv_ref[TM - CONV_HALO:TM, :], 0.0)
        hbuf_ref[CONV_HALO:CONV_HALO + TM, :] = h0_ref[...]
        _fill_row_shifts(dbuf_ref, ds_ref, TM)
        _fill_row_shifts(hbuf_ref, hs_ref, TM)

        def conv_rows(step, _):
            r0 = pl.multiple_of(step * CONV_ROWS, CONV_ROWS)
            dh1 = dbuf_ref[pl.ds(r0, CONV_ROWS), :]
            part = jnp.zeros((CONV_ROWS, CONV_CH), F32)
            for j in range(CONV_TAPS):
                part = part + cw_ref[j:j + 1, :] * _row_shifted(dbuf_ref, ds_ref, CONV_TAPS - 1 - j, CONV_ROWS, r0)
                prod = dh1 * _row_shifted(hbuf_ref, hs_ref, lead + j, CONV_ROWS, r0)
                dcw8_ref[j] += jnp.sum(prod.reshape(CONV_ROWS // SUBLANES, SUBLANES, CONV_CH), axis=0)
            dh0_ref[pl.ds(r0, CONV_ROWS), :] = part
            return 0

        lax.fori_loop(0, TM // CONV_ROWS, conv_rows, 0)

        @pl.when(i == nb - 1)
        def _():
            dcw_ref[...] = jnp.sum(dcw8_ref[...], axis=1)

        dh0 = dh0_ref[...]
        sg = _sigmoid(g_ref[...])
        dag_ref[:, 0:CONV_CH] = (dh0 * sg).astype(BF16)
        dag_ref[:, CONV_CH:2 * CONV_CH] = (dh0 * a_ref[...] * sg * (1.0 - sg)).astype(BF16)

    blk = lambda fn: pl.BlockSpec((TM, CONV_CH), fn)
    return pl.pallas_call(
        body, name="conv_glu_bwd", grid=(nb,),
        in_specs=[blk(lambda i: (i, 0)), blk(lambda i: (jnp.minimum(i + 1, nb - 1), 0)),
                  blk(lambda i: (i, 0)), blk(lambda i: (jnp.maximum(i - 1, 0), 0)),
                  blk(lambda i: (i, O_A // CONV_CH)), blk(lambda i: (i, O_G // CONV_CH)),
                  pl.BlockSpec(cw.shape, lambda i: (0, 0))],
        out_specs=[pl.BlockSpec((TM, 2 * CONV_CH), lambda i: (i, 0)), pl.BlockSpec((CONV_HALO, CONV_CH), lambda i: (0, 0))],
        out_shape=[jax.ShapeDtypeStruct((S, 2 * CONV_CH), BF16), jax.ShapeDtypeStruct((CONV_HALO, CONV_CH), F32)],
        scratch_shapes=[pltpu.VMEM((TM + CONV_HALO, CONV_CH), F32), pltpu.VMEM((TM + CONV_HALO, CONV_CH), F32),
                        pltpu.VMEM((SUBLANES - 1, TM + CONV_HALO, CONV_CH), F32),
                        pltpu.VMEM((SUBLANES - 1, TM + CONV_HALO, CONV_CH), F32),
                        pltpu.VMEM((TM, CONV_CH), F32), pltpu.VMEM((CONV_HALO, SUBLANES, CONV_CH), F32)],
        compiler_params=_params(1),
    )(dh1, dh1, h0, h0, proj, proj, cw)


def _mixer_in_bwd(x, dx2, proj, dq, dk, dv, dag, dct, drb, g1, win, qg, kg, bf):
    S = x.shape[0]
    TM = min(512, S)
    nb = S // TM

    def body(x_ref, dx2_ref, qr_ref, kr_ref, fz_ref, dq_ref, dk_ref, dv_ref, dag_ref, dct_ref, drb_ref,
             g1_ref, win_ref, qg_ref, kg_ref, bf_ref, bd_ref, fold_ref, triu_ref, pick_ref,
             dproj_ref, dx_ref, dg1_ref, sm_ref, carry_ref, gsum_ref):
        i = pl.program_id(0)

        @pl.when(i == 0)
        def _():
            carry_ref[...] = jnp.zeros_like(carry_ref)
            gsum_ref[...] = jnp.zeros_like(gsum_ref)
            dg1_ref[...] = jnp.zeros_like(dg1_ref)
            sm_ref[...] = jnp.zeros_like(sm_ref)

        def headnorm_bwd(raw, dy, gain, scale, row):
            rs = lax.rsqrt(_head_sums(raw * raw, bd_ref[...]) * (1.0 / HEAD_DIM) + EPS)
            n = raw * rs
            gsum_ref[row:row + 1, :] += jnp.sum(dy * n, axis=0, keepdims=True) * scale
            dn = dy * (gain * scale)
            return rs * (dn - n * (_head_sums(dn * n, bd_ref[...]) * (1.0 / HEAD_DIM)))

        dproj_ref[:, O_Q:O_Q + ATT_W] = headnorm_bwd(qr_ref[...], dq_ref[...], qg_ref[...], QK_SCALE, 0).astype(BF16)
        dproj_ref[:, O_K:O_K + ATT_W] = headnorm_bwd(kr_ref[...], dk_ref[...], kg_ref[...], 1.0, 1).astype(BF16)
        dproj_ref[:, O_V:O_V + ATT_W] = dv_ref[...].astype(BF16)
        dproj_ref[:, O_A:O_A + 2 * CONV_CH] = dag_ref[...]

        dc8 = jnp.concatenate([dct_ref[...], jnp.zeros((128 - ATT_HEADS, TM), F32)], axis=0).T
        dc8 = dc8 + _dot_hi_r(drb_ref[...], pick_ref[...])
        dlogf = _dot_hi_l(triu_ref[...], dc8) + carry_ref[...]
        carry_ref[...] = dlogf[0:1, :]
        df = dlogf * _sigmoid(-(fz_ref[...] + bf_ref[...]))
        dproj_ref[:, O_F:O_F + 128] = df.astype(BF16)
        sm_ref[2:3, :] += jnp.sum(df, axis=0, keepdims=True)

        du1 = _dot_nt(dproj_ref[...], win_ref[...])
        xv = x_ref[...]
        r = lax.rsqrt(jnp.mean(xv * xv, axis=-1, keepdims=True) + EPS)
        n1 = xv * r
        t = du1 * g1_ref[...]
        dx_ref[...] = dx2_ref[...] + r * (t - n1 * jnp.mean(t * n1, axis=-1, keepdims=True))
        dg1_ref[0:1, :] += jnp.sum(du1 * n1, axis=0, keepdims=True)

        @pl.when(i == nb - 1)
        def _():
            sm_ref[0:2, :] = _dot_hi_r(gsum_ref[0:8, :], fold_ref[...])[0:2, :]

    rev = lambda w, cb=0: pl.BlockSpec((TM, w), lambda i: (nb - 1 - i, cb))
    full = lambda a: pl.BlockSpec(a.shape, lambda i: (0,) * a.ndim)
    bd, fold, triu = _head_blockdiag(), _head_fold(), _tril(TM).T
    consts = (g1, win, qg, kg, bf, bd, fold, triu, _head_pick())
    return pl.pallas_call(
        body, name="mixer_in_bwd", grid=(nb,),
        in_specs=[rev(D_MODEL), rev(D_MODEL), rev(ATT_W, O_Q // ATT_W), rev(ATT_W, O_K // ATT_W), rev(128, O_F // 128),
                  rev(ATT_W), rev(ATT_W), rev(ATT_W), rev(2 * CONV_CH),
                  pl.BlockSpec((ATT_HEADS, TM), lambda i: (0, nb - 1 - i)), rev(ATT_W)] + [full(a) for a in consts],
        out_specs=[rev(N_INP), rev(D_MODEL), pl.BlockSpec((8, D_MODEL), lambda i: (0, 0)),
                   pl.BlockSpec((8, 128), lambda i: (0, 0))],
        out_shape=[jax.ShapeDtypeStruct((S, N_INP), BF16), jax.ShapeDtypeStruct((S, D_MODEL), F32),
                   jax.ShapeDtypeStruct((8, D_MODEL), F32), jax.ShapeDtypeStruct((8, 128), F32)],
        scratch_shapes=[pltpu.VMEM((1, 128), F32), pltpu.VMEM((8, ATT_W), F32)],
        compiler_params=_params(1),
    )(x, dx2, proj, proj, proj, dq, dk, dv, dag, dct, drb, *consts)


def _layer_fwd(x, early, late, target=None):
    p = early(x)
    u1, proj, q, k, v, qa, ka, h0, h1, h3 = _mixer_in_fwd(
        x, p["g1"], p["win"], p["qg"], p["kg"], p["bf"], p["cw"], p["cvb"], p["lng"], p["lnb"])
    att, lse = _attn_fwd(q, qa, k, ka, v)
    p = dict(p, **late(att))
    x2, u2 = _mixer_out_fwd(x, att, h3, p["wo"], p["g2"])
    x3, z, hh, *loss_acc = _mlp_fwd(x2, u2, p["w1"], p["w2"], target)
    saved = dict(x=x, u1=u1, proj=proj, q=q, k=k, v=v, qa=qa, ka=ka, h0=h0, h1=h1, h3=h3, att=att, lse=lse,
                 x2=x2, u2=u2, z=z, hh=hh)
    return (x3 if target is None else (x3, loss_acc[0])), saved, p


def _tie(a, token):
    return a if token is None else a + token[0:1, 0:1]


def _layer_bwd(dx3, s, p, reduce):
    dz, dx2, dg2 = _mlp_bwd(dx3, s["z"], s["x2"], p["g2"], p["w1"], p["w2"])
    g_w2 = _matmul_tn(s["hh"], dx3)
    g_w1 = _matmul_tn(s["u2"], dz, col_shards=N_CHIPS)
    token = reduce("a", {2: g_w1, 3: g_w2.reshape(N_CHIPS, D_FF // N_CHIPS, D_MODEL)})
    datt, dlt, dh1, sm_c = _mixer_out_bwd(dx2, p["wo"], s["att"], s["h1"], _tie(p["lng"], token), p["lnb"])
    g_wo = jnp.concatenate([_matmul_tn(s["att"], dx2)[0], _matmul_tn(s["h3"], dx2)[0]], axis=0)
    dag, dcw = _conv_glu_bwd(dh1, s["h0"], s["proj"], p["cw"])
    dq, dk, dv, dc4, drb = _attn_bwd(s["q"], s["qa"], s["k"], s["ka"], s["v"], datt, s["lse"],
                                     dlt.reshape(ATT_HEADS // 2, 2, dlt.shape[1]))
    dct = dc4.reshape(ATT_HEADS, dc4.shape[2])
    dproj, dx, dg1, sm_a = _mixer_in_bwd(s["x"], dx2, s["proj"], dq, dk, dv, dag, dct, drb,
                                         p["g1"], p["win"], p["qg"], p["kg"], p["bf"])
    g_win = _win_to_global(_matmul_tn(s["u1"], dproj)[0])
    g_win = g_win.reshape(D_MODEL, N_CHIPS, N_IN // N_CHIPS).transpose(1, 0, 2)
    token = reduce("b", {0: g_win, 1: g_wo.reshape(N_CHIPS, D_MODEL // N_CHIPS, D_MODEL)})
    small = dict(g1=dg1[0], g2=dg2[0], lng=sm_c[0], lnb=sm_c[1], cvb=sm_c[2], cw=dcw[0:CONV_TAPS],
                 qg=sm_a[0, 0:HEAD_DIM], kg=sm_a[1, 0:HEAD_DIM], bf=sm_a[2, 0:ATT_HEADS])
    return dx, small, token


def _local_step(x, target, weights, reduce):
    saved, layers = [], []
    h = x
    for l, (early, late) in enumerate(weights):
        h, s, p = _layer_fwd(h, early, late, target if l == len(weights) - 1 else None)
        saved.append(s)
        layers.append(p)
    dy, loss_acc = h
    loss = loss_acc[0, 0] * (0.5 / D_MODEL)
    smalls = []
    d, token = dy, None
    for l in reversed(range(len(layers))):
        d, small, token = _layer_bwd(d, saved[l], dict(layers[l], g2=_tie(layers[l]["g2"], token)), reduce(l))
        smalls.append(small)
    return loss, d, smalls[::-1]


def _win_to_internal(w):
    pad = jnp.zeros(w.shape[:-1] + (N_INP - N_IN,), w.dtype)
    return jnp.concatenate([w[..., :1536], w[..., 1544:], w[..., 1536:1544], pad], axis=-1)


def _win_to_global(g):
    return jnp.concatenate([g[..., :1536], g[..., O_F:O_F + ATT_HEADS], g[..., 1536:O_F]], axis=-1)


def _layer_params(l, win, cw, norm1_g, b_f, q_norm_g, k_norm_g, conv_b, conv_ln_g, conv_ln_b, norm2_g):
    row = lambda a: a.reshape(1, -1)
    return dict(
        win=win, cw=jnp.pad(cw, ((0, CONV_HALO - CONV_TAPS), (0, 0))),
        g1=row(norm1_g[l]), g2=row(norm2_g[l]),
        qg=row(jnp.tile(q_norm_g[l], ATT_HEADS)), kg=row(jnp.tile(k_norm_g[l], ATT_HEADS)),
        bf=row(jnp.pad(b_f[l], (0, 128 - ATT_HEADS))),
        cvb=row(conv_b[l]), lng=row(conv_ln_g[l]), lnb=row(conv_ln_b[l]))


def _place():
    x, y, c = lax.axis_index("x"), lax.axis_index("y"), lax.axis_index("c")
    chips = [(1 - x, y), (x, 1 - y), (1 - x, 1 - y)]
    return x, y, c, chips


HBM = pl.BlockSpec(memory_space=pltpu.HBM)
SEM = pl.BlockSpec(memory_space=pltpu.SEMAPHORE)
GATHER_PEERS = N_CHIPS


def _gather_peers():
    x, y, c, chips = _place()
    return [(*chip, c) for chip in chips] + [(x, y, 1 - c)], [2 * px + py for px, py in chips] + [2 * x + y]


def _gather_start(srcs):
    n = len(srcs)

    def body(*refs):
        ins, lands = refs[:n], refs[n:2 * n]
        send_sems, recv_sems, token = refs[2 * n], refs[2 * n + 1], refs[-1]
        me = 2 * lax.axis_index("x") + lax.axis_index("y")
        peers, _ = _gather_peers()
        for g in range(n):
            for j, to in enumerate(peers):
                pltpu.make_async_remote_copy(src_ref=ins[g], dst_ref=lands[g].at[me],
                                             send_sem=send_sems.at[GATHER_PEERS * g + j],
                                             recv_sem=recv_sems.at[GATHER_PEERS * g + j],
                                             device_id=to, device_id_type=MESH).start()
        token[...] = jnp.zeros_like(token)

    lands = [lax.empty((N_CHIPS,) + a.shape, a.dtype) for a in srcs]
    outs = pl.pallas_call(
        body, name="gather_start",
        in_specs=[HBM] * (2 * n),
        out_specs=[SEM, SEM] + [HBM] * (2 * n) + [pl.BlockSpec(memory_space=pltpu.VMEM)],
        out_shape=[pltpu.SemaphoreType.DMA((GATHER_PEERS * n,)), pltpu.SemaphoreType.DMA((GATHER_PEERS * n,))]
        + [pltpu.HBM(a.shape, a.dtype) for a in srcs] + [pltpu.HBM(a.shape, a.dtype) for a in lands]
        + [jax.ShapeDtypeStruct((8, 128), F32)],
        input_output_aliases={i: 2 + i for i in range(2 * n)},
        compiler_params=pltpu.CompilerParams(has_side_effects=pltpu.SideEffectType.DATAFLOW_SIDE_EFFECTING),
    )(*[pltpu.with_memory_space_constraint(a, pltpu.HBM) for a in srcs],
      *[pltpu.with_memory_space_constraint(a, pltpu.HBM) for a in lands])
    return outs[0], outs[1], list(outs[2:2 + n]), list(outs[2 + n:2 + 2 * n]), outs[-1]


def _gather_wait(name, groups, send_sems, recv_sems, srcs, lands, after):
    k = len(groups)

    def body(*refs):
        ins, lnd = refs[:k], refs[k:2 * k]
        ssem, rsem = refs[2 * k], refs[2 * k + 1]
        peers, slots = _gather_peers()
        for i, g in enumerate(groups):
            for j, to in enumerate(peers):
                cp = pltpu.make_async_remote_copy(src_ref=ins[i], dst_ref=lnd[i].at[slots[j]],
                                                  send_sem=ssem.at[GATHER_PEERS * g + j],
                                                  recv_sem=rsem.at[GATHER_PEERS * g + j],
                                                  device_id=to, device_id_type=MESH)
                cp.wait_send()
                cp.wait_recv()

    outs = pl.pallas_call(
        body, name=name,
        in_specs=[HBM] * (2 * k) + [SEM, SEM, ANY],
        out_specs=[HBM] * (2 * k),
        out_shape=[pltpu.HBM(a.shape, a.dtype) for a in srcs] + [pltpu.HBM(a.shape, a.dtype) for a in lands],
        input_output_aliases={i: i for i in range(2 * k)},
        compiler_params=pltpu.CompilerParams(has_side_effects=pltpu.SideEffectType.DATAFLOW_SIDE_EFFECTING),
    )(*srcs, *lands, send_sems, recv_sems, after)
    return list(outs[k:])


REDUCE_STEPS = 8


def _row_half(a):
    return a.shape[-2] // 2


def _half_swap(name, gs):
    n = len(gs)

    def body(*refs):
        ins, outs = refs[:n], refs[n:2 * n]
        send_sems, recv_sems = refs[2 * n:]
        x, y, c, _ = _place()
        copies = []
        for a in range(n):
            h = _row_half(gs[a])
            copies.append(pltpu.make_async_remote_copy(
                src_ref=ins[a].at[:, pl.ds((1 - c) * h, h), :], dst_ref=outs[a], send_sem=send_sems.at[a],
                recv_sem=recv_sems.at[a], device_id=(x, y, 1 - c), device_id_type=MESH))
        for cp in copies:
            cp.start()
        for cp in copies:
            cp.wait()

    return pl.pallas_call(
        body, name=name,
        in_specs=[ANY] * n, out_specs=[ANY] * n,
        out_shape=[jax.ShapeDtypeStruct((N_CHIPS, _row_half(a), a.shape[-1]), a.dtype) for a in gs],
        scratch_shapes=[pltpu.SemaphoreType.DMA((n,)), pltpu.SemaphoreType.DMA((n,))],
        compiler_params=pltpu.CompilerParams(has_side_effects=True),
    )(*gs)


def _half_specs(gs, row_block):
    tiles = [_row_half(a) // REDUCE_STEPS for a in gs]
    return [pl.BlockSpec((N_CHIPS, t, a.shape[-1]), lambda i, p: (0, row_block(i, p), 0)) for a, t in zip(gs, tiles)]


def _half_add(name, gs, got, place):
    n = len(gs)

    def body(place_ref, *refs):
        own, theirs, outs = refs[:n], refs[n:2 * n], refs[2 * n:]
        for a in range(n):
            outs[a][...] = (own[a][...].astype(F32) + theirs[a][...].astype(F32)).astype(BF16)

    plain = _half_specs(gs, lambda i, p: i)
    return pl.pallas_call(
        body, name=name,
        grid_spec=pltpu.PrefetchScalarGridSpec(
            num_scalar_prefetch=1, grid=(REDUCE_STEPS,),
            in_specs=_half_specs(gs, lambda i, p: p[1] * REDUCE_STEPS + i) + plain, out_specs=plain),
        out_shape=[jax.ShapeDtypeStruct(a.shape, BF16) for a in got],
        compiler_params=_params(1),
    )(place, *gs, *got)


def _exchange_copies(parts, lands, send_sems, recv_sems):
    x, y, c, chips = _place()
    return [pltpu.make_async_remote_copy(src_ref=parts[a].at[2 * px + py], dst_ref=lands[a].at[j],
                                         send_sem=send_sems.at[3 * a + j], recv_sem=recv_sems.at[3 * a + j],
                                         device_id=(px, py, c), device_id_type=MESH)
            for a in range(len(parts)) for j, (px, py) in enumerate(chips)]


def _exchange_start(name, parts):
    n = len(parts)

    def body(*refs):
        _ = [cp.start() for cp in _exchange_copies(refs[:n], refs[n:2 * n], refs[2 * n], refs[2 * n + 1])]
        refs[-1][...] = jnp.zeros_like(refs[-1])

    lands = [lax.empty((N_CHIPS - 1,) + a.shape[1:], a.dtype) for a in parts]
    outs = pl.pallas_call(
        body, name=name,
        in_specs=[HBM] * (2 * n),
        out_specs=[SEM, SEM] + [HBM] * (2 * n) + [pl.BlockSpec(memory_space=pltpu.VMEM)],
        out_shape=[pltpu.SemaphoreType.DMA((3 * n,)), pltpu.SemaphoreType.DMA((3 * n,))]
        + [pltpu.HBM(a.shape, a.dtype) for a in parts] + [pltpu.HBM(a.shape, a.dtype) for a in lands]
        + [jax.ShapeDtypeStruct((8, 128), F32)],
        input_output_aliases={i: 2 + i for i in range(2 * n)},
        compiler_params=pltpu.CompilerParams(has_side_effects=pltpu.SideEffectType.DATAFLOW_SIDE_EFFECTING),
    )(*[pltpu.with_memory_space_constraint(a, pltpu.HBM) for a in parts],
      *[pltpu.with_memory_space_constraint(a, pltpu.HBM) for a in lands])
    return outs[0], outs[1], list(outs[2:2 + n]), list(outs[2 + n:2 + 2 * n]), outs[-1]


def _exchange_wait(name, send_sems, recv_sems, parts, lands, after):
    n = len(parts)

    def body(*refs):
        for cp in _exchange_copies(refs[:n], refs[n:2 * n], refs[2 * n], refs[2 * n + 1]):
            cp.wait_send()
            cp.wait_recv()

    outs = pl.pallas_call(
        body, name=name,
        in_specs=[HBM] * (2 * n) + [SEM, SEM, ANY],
        out_specs=[HBM] * (2 * n),
        out_shape=[pltpu.HBM(a.shape, a.dtype) for a in parts] + [pltpu.HBM(a.shape, a.dtype) for a in lands],
        input_output_aliases={i: i for i in range(2 * n)},
        compiler_params=pltpu.CompilerParams(has_side_effects=pltpu.SideEffectType.DATAFLOW_SIDE_EFFECTING),
    )(*parts, *lands, send_sems, recv_sems, after)
    return list(outs[:n]), list(outs[n:])


def _chip_sum(name, parts, lands, sums, place, layer):
    n = len(parts)
    tiles = [a.shape[-2] // REDUCE_STEPS for a in parts]

    def body(place_ref, *refs):
        own, got, outs = refs[:n], refs[n:2 * n], refs[3 * n:]
        for a in range(n):
            tot = own[a][...].astype(F32)
            for j in range(N_CHIPS - 1):
                tot = tot + got[a][j].astype(F32)
            outs[a][...] = tot

    own_specs = [pl.BlockSpec((None, t, a.shape[-1]), lambda i, p: (p[0], i, 0)) for a, t in zip(parts, tiles)]
    got_specs = [pl.BlockSpec((N_CHIPS - 1, t, a.shape[-1]), lambda i, p: (0, i, 0)) for a, t in zip(parts, tiles)]
    out_specs = [pl.BlockSpec((None, t, a.shape[-1]), lambda i, p: (layer, p[1] * REDUCE_STEPS + i, 0))
                 for a, t in zip(parts, tiles)]
    return pl.pallas_call(
        body, name=name,
        grid_spec=pltpu.PrefetchScalarGridSpec(num_scalar_prefetch=1, grid=(REDUCE_STEPS,),
                                               in_specs=own_specs + got_specs + [ANY] * n, out_specs=out_specs),
        out_shape=[jax.ShapeDtypeStruct(a.shape, F32) for a in sums],
        input_output_aliases={1 + 2 * n + a: a for a in range(n)},
        compiler_params=_params(1),
    )(place, *parts, *lands, *sums)


def _half_fill(name, sums, layer):
    n = len(sums)

    def body(*refs):
        ins, outs = refs[:n], refs[n:2 * n]
        send_sems, recv_sems = refs[2 * n:]
        x, y, c, _ = _place()
        copies = []
        for a in range(n):
            h = _row_half(sums[a])
            copies.append(pltpu.make_async_remote_copy(
                src_ref=ins[a].at[layer, pl.ds(c * h, h), :], dst_ref=outs[a].at[layer, pl.ds(c * h, h), :],
                send_sem=send_sems.at[a], recv_sem=recv_sems.at[a], device_id=(x, y, 1 - c), device_id_type=MESH))
        for cp in copies:
            cp.start()
        for a in range(n):
            h = _row_half(sums[a])
            theirs = outs[a].at[layer, pl.ds((1 - c) * h, h), :]
            pltpu.make_async_remote_copy(src_ref=theirs, dst_ref=theirs, send_sem=send_sems.at[a], recv_sem=recv_sems.at[a],
                                         device_id=(x, y, 1 - c), device_id_type=MESH).wait_recv()
        for cp in copies:
            cp.wait_send()

    return pl.pallas_call(
        body, name=name,
        in_specs=[ANY] * n, out_specs=[ANY] * n,
        out_shape=[jax.ShapeDtypeStruct(a.shape, a.dtype) for a in sums],
        input_output_aliases={a: a for a in range(n)},
        scratch_shapes=[pltpu.SemaphoreType.DMA((n,)), pltpu.SemaphoreType.DMA((n,))],
        compiler_params=pltpu.CompilerParams(has_side_effects=True),
    )(*sums)


def _adamw_math(w, g, m, v):
    m = ADAM_B1 * m + (1.0 - ADAM_B1) * g
    v = ADAM_B2 * v + (1.0 - ADAM_B2) * (g * g)
    m_hat = m / (1.0 - ADAM_B1 ** ADAM_STEP)
    v_hat = v / (1.0 - ADAM_B2 ** ADAM_STEP)
    delta = -ADAM_LR * (m_hat / (jnp.sqrt(v_hat) + ADAM_EPS) + ADAM_WD * w)
    return delta, m, v


def _adamw(name, ws, gs, ms, vs, layer, prev=None):
    n = len(ws)
    steps = 8
    tiles = [a.shape[-2] // steps for a in ws]

    def body(*refs):
        w_r, g_r, m_r, v_r = refs[:n], refs[n:2 * n], refs[2 * n:3 * n], refs[3 * n:4 * n]
        g_o, d_o, m_o, v_o = (refs[-4 * n:][k * n:(k + 1) * n] for k in range(4))
        for a in range(n):
            g = g_r[a][...]
            d, m, v = _adamw_math(w_r[a][...], g, m_r[a][...], v_r[a][...])
            g_o[a][...] = g
            d_o[a][...] = d
            m_o[a][...] = m
            v_o[a][...] = v

    specs = [pl.BlockSpec((None, t, a.shape[-1]), lambda i: (layer, i, 0)) for a, t in zip(ws, tiles)]
    held = [] if prev is None else [buf for kind in zip(*prev) for buf in kind]
    outs = pl.pallas_call(
        body, name=name, grid=(steps,),
        in_specs=specs * 4 + [ANY] * len(held), out_specs=specs * 4,
        out_shape=[jax.ShapeDtypeStruct(a.shape, F32) for a in ws] * 4,
        input_output_aliases={4 * n + k: k for k in range(len(held))},
        compiler_params=_params(1),
    )(*ws, *gs, *ms, *vs, *held)
    return [[outs[k * n + a] for k in range(4)] for a in range(n)]


SMALL_W = 512


def _small_allreduce_adamw(g, w, m, v, cw_w, cw_m, cw_v, cw_row0):
    R = g.shape[0]
    n_l = cw_w.shape[0]

    def body(g_ref, w_ref, m_ref, v_ref, cww_ref, cwm_ref, cwv_ref,
             gs_ref, d_ref, mo_ref, vo_ref, cg_ref, cd_ref, cmo_ref, cvo_ref,
             slots_ref, send_sems, recv_sems):
        x, y, c, _ = _place()
        me = 4 * x + 2 * y + c
        slots_ref[me] = g_ref[...]
        sends = []
        for d in range(1, 8):
            px, py, pc = x ^ (d >> 2), y ^ ((d >> 1) & 1), c ^ (d & 1)
            cp = pltpu.make_async_remote_copy(src_ref=g_ref, dst_ref=slots_ref.at[me], send_sem=send_sems.at[d - 1],
                                              recv_sem=recv_sems.at[d - 1], device_id=(px, py, pc), device_id_type=MESH)
            cp.start()
            sends.append(cp)
        for d in range(1, 8):
            px, py, pc = x ^ (d >> 2), y ^ ((d >> 1) & 1), c ^ (d & 1)
            slot = slots_ref.at[4 * px + 2 * py + pc]
            pltpu.make_async_remote_copy(src_ref=slot, dst_ref=slot, send_sem=send_sems.at[d - 1],
                                         recv_sem=recv_sems.at[d - 1], device_id=(px, py, pc),
                                         device_id_type=MESH).wait_recv()
        for cp in sends:
            cp.wait_send()
        tot = slots_ref[0]
        for k in range(1, 8):
            tot = tot + slots_ref[k]
        gs_ref[...] = tot
        dl, mn, vn = _adamw_math(w_ref[...], tot, m_ref[...], v_ref[...])
        d_ref[...] = dl
        mo_ref[...] = mn
        vo_ref[...] = vn
        chip = 2 * x + y
        for l in range(n_l):
            rows = tot[cw_row0[l]:cw_row0[l] + CONV_HALO, :]
            mine = rows[:, 0:128]
            for k in range(1, N_CHIPS):
                mine = jnp.where(chip == k, rows[:, 128 * k:128 * (k + 1)], mine)
            cg_ref[l] = mine
            dl, mn, vn = _adamw_math(cww_ref[l], mine, cwm_ref[l], cwv_ref[l])
            cd_ref[l] = dl
            cmo_ref[l] = mn
            cvo_ref[l] = vn

    vm = pl.BlockSpec(memory_space=pltpu.VMEM)
    small = jax.ShapeDtypeStruct((R, SMALL_W), F32)
    conv = jax.ShapeDtypeStruct(cw_w.shape, F32)
    return pl.pallas_call(
        body, name="small_allreduce_adamw",
        in_specs=[vm] * 7, out_specs=[vm] * 8,
        out_shape=[small] * 4 + [conv] * 4,
        scratch_shapes=[pltpu.VMEM((8, R, SMALL_W), F32), pltpu.SemaphoreType.DMA((7,)), pltpu.SemaphoreType.DMA((7,))],
        compiler_params=pltpu.CompilerParams(has_side_effects=True, vmem_limit_bytes=VMEM_LIMIT),
    )(g, w, m, v, cw_w, cw_m, cw_v)


SMALL_LAYOUT = (("conv_w", CONV_HALO), ("norm1_g", 2), ("norm2_g", 2), ("conv_b", 1), ("conv_ln_g", 1),
                ("conv_ln_b", 1), ("q_norm_g", 1), ("k_norm_g", 1), ("b_f", 1))
SMALL_ROWS = sum(r for _, r in SMALL_LAYOUT)
SMALL_ROWS_PAD = 48
LOSS_ROW = SMALL_ROWS


def _pack_small(per_layer):
    flat = []
    for d in per_layer:
        for name, r in SMALL_LAYOUT:
            n = r * SMALL_W
            a = d.get(name)
            if a is None:
                flat.append(jnp.zeros((n,), F32))
                continue
            flat.append(a.reshape(-1))
            if a.size < n:
                flat.append(jnp.zeros((n - a.size,), F32))
        spare = (SMALL_ROWS_PAD - SMALL_ROWS) * SMALL_W
        if "spare" in d:
            flat.append(d["spare"].reshape(-1))
            spare -= d["spare"].size
        flat.append(jnp.zeros((spare,), F32))
    return jnp.concatenate(flat).reshape(-1, SMALL_W)


def _unpack_small(packed, name, size):
    row0 = 0
    for nm, r in SMALL_LAYOUT:
        if nm == name:
            break
        row0 += r
    per_layer = packed.reshape(-1, SMALL_ROWS_PAD * SMALL_W)
    return per_layer[:, row0 * SMALL_W:row0 * SMALL_W + size]


SMALL_SIZES = dict(norm1_g=D_MODEL, norm2_g=D_MODEL, conv_b=CONV_CH, conv_ln_g=CONV_CH, conv_ln_b=CONV_CH,
                   q_norm_g=HEAD_DIM, k_norm_g=HEAD_DIM, b_f=ATT_HEADS)
SMALL_KEYS = dict(norm1_g="g1", norm2_g="g2", conv_b="cvb", conv_ln_g="lng", conv_ln_b="lnb",
                  q_norm_g="qg", k_norm_g="kg", b_f="bf", conv_w="cw")
CONV_W_ROW0 = 0


def kernel(x, norm1_g, w_in, b_f, q_norm_g, k_norm_g, conv_w, conv_b, conv_ln_g, conv_ln_b, w_o, norm2_g, w_mlp_in, w_mlp_out, loss_target, m_norm1_g, m_w_in, m_b_f, m_q_norm_g, m_k_norm_g, m_conv_w, m_conv_b, m_conv_ln_g, m_conv_ln_b, m_w_o, m_norm2_g, m_w_mlp_in, m_w_mlp_out, v_norm1_g, v_w_in, v_b_f, v_q_norm_g, v_k_norm_g, v_conv_w, v_conv_b, v_conv_ln_g, v_conv_ln_b, v_w_o, v_norm2_g, v_w_mlp_in, v_w_mlp_out):
    n_l = w_in.shape[0]

    per_layer = lambda l: [w_in[l].astype(BF16), conv_w[l], w_o[l].astype(BF16), w_mlp_in[l].astype(BF16),
                           w_mlp_out[l].astype(BF16)]
    n_w = len(per_layer(0))
    send_sems, recv_sems, srcs, lands, token = _gather_start([a for l in range(n_l) for a in per_layer(l)])

    def layer_weights(l):
        def wait(tag, which, after):
            groups = [n_w * l + i for i in which]
            return _gather_wait(f"gather_wait_{tag}{l}", groups, send_sems, recv_sems,
                                [srcs[g] for g in groups], [lands[g] for g in groups], after)

        def early(after):
            g_in, g_cw = wait("a", (0, 1), token if l == 0 else after)
            win = _win_to_internal(jnp.concatenate([g_in[k] for k in range(N_CHIPS)], axis=-1))
            cw = jnp.concatenate([g_cw[k] for k in range(N_CHIPS)], axis=-1)
            return _layer_params(l, win, cw, norm1_g, b_f, q_norm_g, k_norm_g, conv_b, conv_ln_g, conv_ln_b, norm2_g)

        def late(after):
            wo, w1, w2 = wait("b", (2, 3, 4), after)
            return dict(wo=wo, w1=w1, w2=w2)

        return early, late

    place = jnp.stack([2 * lax.axis_index("x") + lax.axis_index("y"), lax.axis_index("c")]).astype(jnp.int32)
    big_w = [w_in, w_o, w_mlp_in, w_mlp_out]
    pending, started = [], []

    def reduce(l):
        def group(tag, grads):
            which, gs = list(grads), list(grads.values())
            got = _half_swap(f"half_swap_{tag}{l}", gs)
            parts = _half_add(f"half_add_{tag}{l}", gs, got, place)
            send, recv, parts, lands, token = _exchange_start(f"exchange_start_{tag}{l}", parts)
            pending.append((f"{tag}{l}", l, which, send, recv, parts, lands))
            started.append(token)
            return token
        return group

    loss, dx, smalls = _local_step(x[0], loss_target[0], [layer_weights(l) for l in range(n_l)], reduce)
    env = dict(norm1_g=(norm1_g, m_norm1_g, v_norm1_g), norm2_g=(norm2_g, m_norm2_g, v_norm2_g),
               conv_b=(conv_b, m_conv_b, v_conv_b), conv_ln_g=(conv_ln_g, m_conv_ln_g, v_conv_ln_g),
               conv_ln_b=(conv_ln_b, m_conv_ln_b, v_conv_ln_b), q_norm_g=(q_norm_g, m_q_norm_g, v_q_norm_g),
               k_norm_g=(k_norm_g, m_k_norm_g, v_k_norm_g), b_f=(b_f, m_b_f, v_b_f))
    g_dicts = [{nm: s[key] for nm, key in SMALL_KEYS.items()} for s in smalls]
    g_dicts[0]["spare"] = loss
    g_pack = _tie(_pack_small(g_dicts), started[-1])
    packs = [_pack_small([{nm: env[nm][t][l] for nm in env} for l in range(n_l)]) for t in range(3)]
    pad_cw = lambda a: jnp.pad(a, ((0, 0), (0, CONV_HALO - CONV_TAPS), (0, 0)))
    cw_row0 = tuple(l * SMALL_ROWS_PAD + CONV_W_ROW0 for l in range(n_l))
    gs, ds, ms, vs, cg, cd, cm, cv = _small_allreduce_adamw(
        g_pack, packs[0], packs[1], packs[2], pad_cw(conv_w), pad_cw(m_conv_w), pad_cw(v_conv_w), cw_row0)

    sums = [lax.empty(w.shape, F32) for w in big_w]
    big_m = [m_w_in, m_w_o, m_w_mlp_in, m_w_mlp_out]
    big_v = [v_w_in, v_w_o, v_w_mlp_in, v_w_mlp_out]
    updated = [None] * len(big_w)
    after = started[-1] + gs[0:8, 0:128]
    for tag, l, which, send, recv, parts, lands in pending:
        parts, lands = _exchange_wait(f"exchange_wait_{tag}", send, recv, parts, lands, after)
        done = _chip_sum(f"chip_sum_{tag}", parts, lands, [sums[i] for i in which], place, l)
        done = _half_fill(f"half_fill_{tag}", done, l)
        for i, a in zip(which, done):
            sums[i] = a
        prev = None if updated[which[0]] is None else [updated[i] for i in which]
        new = _adamw(f"adamw_{tag}", [big_w[i] for i in which], done, [big_m[i] for i in which],
                     [big_v[i] for i in which], l, prev)
        for i, r in zip(which, new):
            updated[i] = r
        after = new[0][1]
    g_big, d_big, nm_big, nv_big = ([r[k] for r in updated] for k in range(4))

    def small_out(packed, conv):
        o = {nm: _unpack_small(packed, nm, sz) for nm, sz in SMALL_SIZES.items()}
        o["conv_w"] = conv[:, 0:CONV_TAPS, :]
        return o

    def ordered(small, big):
        return (small["norm1_g"], big[0], small["b_f"], small["q_norm_g"], small["k_norm_g"], small["conv_w"],
                small["conv_b"], small["conv_ln_g"], small["conv_ln_b"], big[1], small["norm2_g"], big[2], big[3])

    return (gs[LOSS_ROW, 0], dx[None],
            *ordered(small_out(gs, cg), g_big), *ordered(small_out(ds, cd), d_big),
            *ordered(small_out(ms, cm), nm_big), *ordered(small_out(vs, cv), nv_big))
```

```python
import jax
import jax.numpy as jnp
from jax import lax
from jax.experimental import pallas as pl
from jax.experimental.pallas import tpu as pltpu

F32 = jnp.float32
BF16 = jnp.bfloat16

D_MODEL = 1024
ATT_HEADS = 8
HEAD_DIM = 64
ATT_W = ATT_HEADS * HEAD_DIM
CONV_CH = 512
CONV_TAPS = 31
CONV_HALO = 32
D_FF = 4 * D_MODEL
N_IN = 3 * ATT_W + ATT_HEADS + 2 * CONV_CH
O_Q, O_K, O_V, O_A, O_G, O_F = 0, 512, 1024, 1536, 2048, 2560
N_INP = O_F + 128
EPS = 1e-6
QK_SCALE = 0.125

ADAM_LR = 0.001
ADAM_B1 = 0.9
ADAM_B2 = 0.999
ADAM_EPS = 1e-08
ADAM_WD = 0.01
ADAM_STEP = 10

N_CHIPS = 4
VMEM_LIMIT = 52 * 1024 * 1024
MESH = pl.DeviceIdType.MESH
ANY = pl.BlockSpec(memory_space=pl.ANY)


def _params(n_axes, **kw):
    return pltpu.CompilerParams(dimension_semantics=("arbitrary",) * n_axes,
                                vmem_limit_bytes=VMEM_LIMIT, **kw)


def _dot(a, b):
    return jnp.dot(a, b, preferred_element_type=F32)


def _dot_nt(a, b):
    return lax.dot_general(a, b, (((1,), (1,)), ((), ())), preferred_element_type=F32)


def _dot_tn(a, b):
    return lax.dot_general(a, b, (((0,), (0,)), ((), ())), preferred_element_type=F32)


def _split3(a):
    a1 = a.astype(BF16)
    r = a - a1.astype(F32)
    a2 = r.astype(BF16)
    a3 = (r - a2.astype(F32)).astype(BF16)
    return a1, a2, a3


def _dot_hi_r(a, b_exact):
    return sum(_dot(p, b_exact) for p in _split3(a))


def _head_sums(a, blockdiag):
    a1 = a.astype(BF16)
    a2 = (a - a1.astype(F32)).astype(BF16)
    return _dot(a1, blockdiag) + _dot(a2, blockdiag)


def _dot_hi_l(a_exact, b):
    return sum(_dot(a_exact, p) for p in _split3(b))


def _sigmoid(x):
    return 1.0 / (1.0 + jnp.exp(-x))


def _head_blockdiag():
    i = jnp.arange(ATT_W) // HEAD_DIM
    return (i[:, None] == i[None, :]).astype(BF16)


AUG_LANES = 8


def _aug_place(first):
    piece = jnp.arange(3 * 128)[:, None] // 128
    h = jnp.arange(3 * 128)[:, None] % 128
    lane = jnp.arange(ATT_W)[None, :]
    return ((h < ATT_HEADS) & (lane == 128 * (h // 2) + AUG_LANES * (h % 2) + first + piece)).astype(BF16)


def _aug_ones(first):
    lane = jnp.arange(ATT_W) % 128
    pos = lane % AUG_LANES
    return ((lane < 2 * AUG_LANES) & (pos >= first) & (pos < first + 3)).astype(F32).reshape(1, ATT_W)


def _head_rows():
    h = jnp.arange(2 * ATT_HEADS)[:, None]
    i = jnp.arange(ATT_W)[None, :] // HEAD_DIM
    return (h == i).astype(BF16)


def _head_fold():
    i = jnp.arange(ATT_W)[:, None] % HEAD_DIM
    j = jnp.arange(128)[None, :]
    return (i == j).astype(BF16)


def _head_pick():
    i = jnp.arange(ATT_W)[:, None]
    h = jnp.arange(128)[None, :]
    return (i == h * HEAD_DIM).astype(BF16)


def _tril(n):
    r = jnp.arange(n)
    return (r[:, None] >= r[None, :]).astype(BF16)


SUBLANES = 8


def _fill_row_shifts(buf_ref, shifts_ref, tm):
    n = tm + CONV_HALO - SUBLANES
    for b in range(1, SUBLANES):
        shifts_ref[b - 1, 0:n, :] = buf_ref[pl.ds(b, n), :]


def _row_shifted(buf_ref, shifts_ref, offset, rows, base=0):
    a, b = divmod(offset, SUBLANES)
    start = pl.multiple_of(base + SUBLANES * a, SUBLANES)
    if b == 0:
        return buf_ref[pl.ds(start, rows), :]
    return shifts_ref[b - 1, pl.ds(start, rows), :]


CONV_ROWS = 32


def _mixer_in_fwd(x, g1, win, qg, kg, bf, cw, cvb, lng, lnb):
    S = x.shape[0]
    TM = min(512, S)
    nb = S // TM

    def body(x_ref, g1_ref, win_ref, qg_ref, kg_ref, bf_ref, cw_ref, cvb_ref, lng_ref, lnb_ref,
             bd_ref, tri_ref, pq_ref, pk_ref, oq_ref, ok_ref,
             u1_ref, proj_ref, q_ref, k_ref, v_ref, qa_ref, ka_ref, h0_ref, h1_ref, h3_ref,
             carry_ref, hbuf_ref, hs_ref):
        i = pl.program_id(0)

        @pl.when(i == 0)
        def _():
            carry_ref[...] = jnp.zeros_like(carry_ref)
            hbuf_ref[0:CONV_HALO, :] = jnp.zeros((CONV_HALO, CONV_CH), F32)

        @pl.when(i > 0)
        def _():
            hbuf_ref[0:CONV_HALO, :] = hbuf_ref[TM:TM + CONV_HALO, :]

        xv = x_ref[...]
        r = lax.rsqrt(jnp.mean(xv * xv, axis=-1, keepdims=True) + EPS)
        u = (xv * r * g1_ref[...]).astype(BF16)
        u1_ref[...] = u
        proj_ref[...] = _dot(u, win_ref[...])

        def headnorm(raw, gain):
            ss = _head_sums(raw * raw, bd_ref[...]) * (1.0 / HEAD_DIM)
            return raw * lax.rsqrt(ss + EPS) * gain

        q_ref[...] = (headnorm(proj_ref[:, O_Q:O_Q + ATT_W], qg_ref[...]) * QK_SCALE).astype(BF16)
        k_ref[...] = headnorm(proj_ref[:, O_K:O_K + ATT_W], kg_ref[...]).astype(BF16)
        v_ref[...] = proj_ref[:, O_V:O_V + ATT_W].astype(BF16)

        zf = proj_ref[:, O_F:O_F + 128] + bf_ref[...]
        logf = jnp.minimum(zf, 0.0) - jnp.log(1.0 + jnp.exp(-jnp.abs(zf)))
        lane = lax.broadcasted_iota(jnp.int32, (TM, 128), 1)
        logf = jnp.where(lane < ATT_HEADS, logf, 0.0)
        c8 = _dot_hi_l(tri_ref[...], logf) + carry_ref[...]
        carry_ref[...] = c8[TM - 1:TM, :]
        pieces = jnp.concatenate(_split3(c8), axis=1)
        qa_ref[...] = (_dot(pieces, pq_ref[...]) + oq_ref[...]).astype(BF16)
        ka_ref[...] = (ok_ref[...] - _dot(pieces, pk_ref[...])).astype(BF16)

        h0 = proj_ref[:, O_A:O_A + CONV_CH] * _sigmoid(proj_ref[:, O_G:O_G + CONV_CH])
        h0_ref[...] = h0
        hbuf_ref[CONV_HALO:CONV_HALO + TM, :] = h0
        _fill_row_shifts(hbuf_ref, hs_ref, TM)
        acc = jnp.zeros((TM, CONV_CH), F32) + cvb_ref[...]
        for j in range(CONV_TAPS):
            acc = acc + cw_ref[j:j + 1, :] * _row_shifted(hbuf_ref, hs_ref, CONV_HALO - CONV_TAPS + 1 + j, TM)
        h1_ref[...] = acc
        mu = jnp.mean(acc, axis=-1, keepdims=True)
        d = acc - mu
        var = jnp.mean(d * d, axis=-1, keepdims=True)
        h2 = d * lax.rsqrt(var + EPS) * lng_ref[...] + lnb_ref[...]
        h3_ref[...] = (h2 * _sigmoid(h2)).astype(BF16)

    row = lambda w: pl.BlockSpec((TM, w), lambda i: (i, 0))
    full = lambda a: pl.BlockSpec(a.shape, lambda i: (0,) * a.ndim)
    ins = (x, g1, win, qg, kg, bf, cw, cvb, lng, lnb, _head_blockdiag(), _tril(TM),
           _aug_place(0), _aug_place(3), _aug_ones(3), _aug_ones(0))
    return pl.pallas_call(
        body, name="mixer_in_fwd", grid=(nb,),
        in_specs=[row(D_MODEL)] + [full(a) for a in ins[1:]],
        out_specs=[row(D_MODEL), row(N_INP), row(ATT_W), row(ATT_W), row(ATT_W), row(ATT_W), row(ATT_W),
                   row(CONV_CH), row(CONV_CH), row(CONV_CH)],
        out_shape=[jax.ShapeDtypeStruct((S, D_MODEL), BF16),
                   jax.ShapeDtypeStruct((S, N_INP), F32),
                   jax.ShapeDtypeStruct((S, ATT_W), BF16),
                   jax.ShapeDtypeStruct((S, ATT_W), BF16),
                   jax.ShapeDtypeStruct((S, ATT_W), BF16),
                   jax.ShapeDtypeStruct((S, ATT_W), BF16),
                   jax.ShapeDtypeStruct((S, ATT_W), BF16),
                   jax.ShapeDtypeStruct((S, CONV_CH), F32),
                   jax.ShapeDtypeStruct((S, CONV_CH), F32),
                   jax.ShapeDtypeStruct((S, CONV_CH), BF16)],
        scratch_shapes=[pltpu.VMEM((1, 128), F32), pltpu.VMEM((TM + CONV_HALO, CONV_CH), F32),
                        pltpu.VMEM((SUBLANES - 1, TM + CONV_HALO, CONV_CH), F32)],
        compiler_params=_params(1),
    )(*ins)


def _pair_heads(lo, alo, x, xa):
    z = jnp.zeros_like(x)
    return (jnp.concatenate([jnp.where(lo, x, z), jnp.where(alo, xa, z)], axis=1),
            jnp.concatenate([jnp.where(lo, z, x), jnp.where(alo, z, xa)], axis=1))


def _attn_fwd(q, qa, k, ka, v):
    S = q.shape[0]
    T = min(1024, S)
    nq = S // T

    def body(q_ref, qa_ref, k_ref, ka_ref, v_ref, o_ref, lse_ref):
        qi = pl.program_id(1)
        lane = lax.broadcasted_iota(jnp.int32, (T, 128), 1)
        lo = lane < HEAD_DIM
        qm = _pair_heads(lo, lane < AUG_LANES, q_ref[...], qa_ref[...])
        tril = (lax.broadcasted_iota(jnp.int32, (T, T), 0) >= lax.broadcasted_iota(jnp.int32, (T, T), 1))

        def step(kj, carry, masked):
            off = pl.multiple_of(kj * T, T)
            kb = jnp.concatenate([k_ref[pl.ds(off, T), :], ka_ref[pl.ds(off, T), :]], axis=1)
            vb = v_ref[pl.ds(off, T), :]
            new = []
            for h in range(2):
                m, l, acc = carry[3 * h:3 * h + 3]
                s = _dot_nt(qm[h], kb)
                if masked:
                    s = jnp.where(tril, s, -1e30)
                m_new = jnp.maximum(m, jnp.max(s, axis=-1, keepdims=True))
                alpha = jnp.exp(m - m_new)
                p = jnp.exp(s - m_new)
                l = alpha * l + jnp.sum(p, axis=-1, keepdims=True)
                acc = alpha * acc + _dot(p.astype(BF16), vb)
                new += [m_new, l, acc]
            return tuple(new)

        init = (jnp.full((T, 1), -1e30, F32), jnp.zeros((T, 1), F32), jnp.zeros((T, 128), F32)) * 2
        carry = lax.fori_loop(0, qi, lambda kj, c: step(kj, c, False), init)
        m0, l0, a0, m1, l1, a1 = step(qi, carry, True)
        o_ref[...] = jnp.where(lo, a0 / l0, a1 / l1).astype(BF16)
        lse_t = jnp.where(lo, m0 + jnp.log(l0), m1 + jnp.log(l1)).T
        lse_ref[0:1, :] = lse_t[0:1, :]
        lse_ref[1:2, :] = lse_t[HEAD_DIM:HEAD_DIM + 1, :]

    qblk = pl.BlockSpec((T, 128), lambda hp, qi: (qi, hp))
    seq = pl.BlockSpec((S, 128), lambda hp, qi: (0, hp))
    return pl.pallas_call(
        body, name="attn_fwd", grid=(ATT_HEADS // 2, nq),
        in_specs=[qblk, qblk, seq, seq, seq],
        out_specs=[qblk, pl.BlockSpec((None, 2, T), lambda hp, qi: (hp, 0, qi))],
        out_shape=[jax.ShapeDtypeStruct((S, ATT_W), BF16),
                   jax.ShapeDtypeStruct((ATT_HEADS // 2, 2, S), F32)],
        compiler_params=_params(2),
    )(q, qa, k, ka, v)


def _attn_bwd(q, qa, k, ka, v, do, lse, dlt):
    S = q.shape[0]
    T = min(512, S)
    nq = S // T

    def body(q_ref, qa_ref, do_ref, lse_ref, dlt_ref, k_ref, ka_ref, v_ref, dq_ref, dk_ref, dv_ref, rows_ref, cols_ref):
        kj = pl.program_id(1)

        @pl.when(kj == 0)
        def _():
            dq_ref[...] = jnp.zeros_like(dq_ref)
            rows_ref[...] = jnp.zeros_like(rows_ref)

        lane = lax.broadcasted_iota(jnp.int32, (T, 128), 1)
        lo = lane < HEAD_DIM
        alo = lane < AUG_LANES
        triu = (lax.broadcasted_iota(jnp.int32, (T, T), 0) <= lax.broadcasted_iota(jnp.int32, (T, T), 1))
        kb = k_ref[...]
        kaug = jnp.concatenate([kb, ka_ref[...]], axis=1)
        vb = v_ref[...]

        def step(qi, carry, masked):
            off = pl.multiple_of(qi * T, T)
            qb = q_ref[pl.ds(off, T), :]
            dob = do_ref[pl.ds(off, T), :]
            qm = _pair_heads(lo, alo, qb, qa_ref[pl.ds(off, T), :])
            zero = jnp.zeros_like(qb)
            new, dqs = [], []
            for h in range(2):
                dk_a, dv_a, dc_a = carry[3 * h:3 * h + 3]
                dom = jnp.where(lo, dob, zero) if h == 0 else jnp.where(lo, zero, dob)
                s = _dot_nt(kaug, qm[h])
                if masked:
                    s = jnp.where(triu, s, -1e30)
                p = jnp.exp(s - lse_ref[h:h + 1, pl.ds(off, T)])
                dp = _dot_nt(vb, dom)
                ds = p * (dp - dlt_ref[h:h + 1, pl.ds(off, T)])
                pb = p.astype(BF16)
                dsb = ds.astype(BF16)
                dv_a = dv_a + _dot(pb, dob)
                dk_a = dk_a + _dot(dsb, qb)
                dc_a = dc_a + jnp.sum(ds, axis=1, keepdims=True)
                dqs.append(_dot_tn(dsb, kb))
                rows_ref[h:h + 1, pl.ds(off, T)] += jnp.sum(ds, axis=0, keepdims=True)
                new += [dk_a, dv_a, dc_a]
            dq_ref[pl.ds(off, T), :] += jnp.where(lo, dqs[0], dqs[1])
            return tuple(new)

        init = (jnp.zeros((T, 128), F32), jnp.zeros((T, 128), F32), jnp.zeros((T, 1), F32)) * 2
        carry = step(kj, init, True)
        carry = lax.fori_loop(kj + 1, nq, lambda qi, c: step(qi, c, False), carry)
        dk_ref[...] = jnp.where(lo, carry[0], carry[3])
        dv_ref[...] = jnp.where(lo, carry[1], carry[4])
        cols_ref[...] = -jnp.where(lo, carry[2], carry[5])

    seq = pl.BlockSpec((S, 128), lambda hp, kj: (0, hp))
    rows = pl.BlockSpec((None, 2, S), lambda hp, kj: (hp, 0, 0))
    kblk = pl.BlockSpec((T, 128), lambda hp, kj: (kj, hp))
    return pl.pallas_call(
        body, name="attn_bwd", grid=(ATT_HEADS // 2, nq),
        in_specs=[seq, seq, seq, rows, rows, kblk, kblk, kblk],
        out_specs=[seq, kblk, kblk, rows, kblk],
        out_shape=[jax.ShapeDtypeStruct((S, ATT_W), F32), jax.ShapeDtypeStruct((S, ATT_W), F32),
                   jax.ShapeDtypeStruct((S, ATT_W), F32),
                   jax.ShapeDtypeStruct((ATT_HEADS // 2, 2, S), F32),
                   jax.ShapeDtypeStruct((S, ATT_W), F32)],
        compiler_params=_params(2),
    )(q, qa, do, lse, dlt, k, ka, v)


def _wo_spec(wo4):
    return pl.BlockSpec(wo4.shape, lambda i: (0, 0, 0))


def _wo_halves(wo_ref):
    half = N_CHIPS // 2
    return (wo_ref[0:half].reshape(ATT_W, D_MODEL), wo_ref[half:N_CHIPS].reshape(CONV_CH, D_MODEL))


def _mixer_out_fwd(x, att, h3, wo4, g2):
    S = x.shape[0]
    TM = min(512, S)

    def body(x_ref, att_ref, h3_ref, wo_ref, g2_ref, x2_ref, u2_ref):
        wa, wc = _wo_halves(wo_ref)
        x2 = x_ref[...] + _dot(att_ref[...], wa) + _dot(h3_ref[...], wc)
        x2_ref[...] = x2
        r = lax.rsqrt(jnp.mean(x2 * x2, axis=-1, keepdims=True) + EPS)
        u2_ref[...] = (x2 * r * g2_ref[...]).astype(BF16)

    row = lambda w: pl.BlockSpec((TM, w), lambda i: (i, 0))
    full = lambda a: pl.BlockSpec(a.shape, lambda i: (0,) * a.ndim)
    return pl.pallas_call(
        body, name="mixer_out_fwd", grid=(S // TM,),
        in_specs=[row(D_MODEL), row(ATT_W), row(CONV_CH), _wo_spec(wo4), full(g2)],
        out_specs=[row(D_MODEL), row(D_MODEL)],
        out_shape=[jax.ShapeDtypeStruct((S, D_MODEL), F32), jax.ShapeDtypeStruct((S, D_MODEL), BF16)],
        compiler_params=_params(1),
    )(x, att, h3, wo4, g2)


def _mlp_w_specs():
    return [pl.BlockSpec((None, D_MODEL, D_FF // N_CHIPS), lambda i, f: (f, 0, 0)),
            pl.BlockSpec((None, D_FF // N_CHIPS, D_MODEL), lambda i, f: (f, 0, 0))]


def _mlp_fwd(x2, u2, w1, w2, target=None):
    S = x2.shape[0]
    head = target is not None
    TM = min(1024, S)
    TF = 1024
    nf = D_FF // TF

    def body(*refs):
        x2_ref, u2_ref, w1_ref, w2_ref = refs[:4]
        x3_ref, z_ref, hh_ref = refs[4 + head:7 + head]
        i = pl.program_id(0)
        f = pl.program_id(1)

        @pl.when(f == 0)
        def _():
            x3_ref[...] = x2_ref[...]

        z = _dot(u2_ref[...], w1_ref[...])
        z_ref[...] = z
        zr = jnp.maximum(z, 0.0)
        hh = (zr * zr).astype(BF16)
        hh_ref[...] = hh
        x3_ref[...] += _dot(hh, w2_ref[...])

        if head:
            t_ref, loss_ref = refs[4], refs[8]

            @pl.when((i == 0) & (f == 0))
            def _():
                loss_ref[...] = jnp.zeros_like(loss_ref)

            @pl.when(f == nf - 1)
            def _():
                d = x3_ref[...] - t_ref[...]
                x3_ref[...] = d * (1.0 / D_MODEL)
                loss_ref[...] += jnp.sum(d * d)

    rows = pl.BlockSpec((TM, D_MODEL), lambda i, f: (i, 0))
    once = pl.BlockSpec((TM, D_MODEL), lambda i, f: (i, 0), pipeline_mode=pl.Buffered(1))
    tile = pl.BlockSpec((TM, TF), lambda i, f: (i, f))
    return pl.pallas_call(
        body, name="mlp_fwd_loss" if head else "mlp_fwd", grid=(S // TM, nf),
        in_specs=[once if head else rows, rows] + _mlp_w_specs() + [once] * head,
        out_specs=[rows, tile, tile] + [pl.BlockSpec((8, 128), lambda i, f: (0, 0))] * head,
        out_shape=[jax.ShapeDtypeStruct((S, D_MODEL), F32), jax.ShapeDtypeStruct((S, D_FF), F32),
                   jax.ShapeDtypeStruct((S, D_FF), BF16)] + [jax.ShapeDtypeStruct((8, 128), F32)] * head,
        compiler_params=_params(2),
    )(x2, u2, w1, w2, *([target] if head else []))


def _mlp_bwd(dx3, z, x2, g2, w1, w2):
    S = dx3.shape[0]
    TM = min(1024, S)
    TF = 1024
    nf = D_FF // TF

    def body(dx3_ref, z_ref, x2_ref, g2_ref, w1_ref, w2_ref, dz_ref, dx2_ref, dg2_ref, du2_ref):
        i = pl.program_id(0)
        f = pl.program_id(1)

        @pl.when((i == 0) & (f == 0))
        def _():
            dg2_ref[...] = jnp.zeros_like(dg2_ref)

        @pl.when(f == 0)
        def _():
            du2_ref[...] = jnp.zeros_like(du2_ref)

        dhh = _dot_nt(dx3_ref[...].astype(BF16), w2_ref[...])
        dz = (dhh * (2.0 * jnp.maximum(z_ref[...], 0.0))).astype(BF16)
        dz_ref[...] = dz
        du2_ref[...] += _dot_nt(dz, w1_ref[...])

        @pl.when(f == nf - 1)
        def _():
            x2 = x2_ref[...]
            r = lax.rsqrt(jnp.mean(x2 * x2, axis=-1, keepdims=True) + EPS)
            n = x2 * r
            du2 = du2_ref[...]
            t = du2 * g2_ref[...]
            dx2_ref[...] = dx3_ref[...] + r * (t - n * jnp.mean(t * n, axis=-1, keepdims=True))
            dg2_ref[0:1, :] += jnp.sum(du2 * n, axis=0, keepdims=True)

    rowi = pl.BlockSpec((TM, D_MODEL), lambda i, f: (i, 0))
    held = pl.BlockSpec((TM, D_MODEL), lambda i, f: (i, 0), pipeline_mode=pl.Buffered(1))
    return pl.pallas_call(
        body, name="mlp_bwd", grid=(S // TM, nf),
        in_specs=[held, pl.BlockSpec((TM, TF), lambda i, f: (i, f)), held,
                  pl.BlockSpec((1, D_MODEL), lambda i, f: (0, 0))] + _mlp_w_specs(),
        out_specs=[pl.BlockSpec((TM, TF), lambda i, f: (i, f)), rowi, pl.BlockSpec((8, D_MODEL), lambda i, f: (0, 0))],
        out_shape=[jax.ShapeDtypeStruct((S, D_FF), BF16), jax.ShapeDtypeStruct((S, D_MODEL), F32),
                   jax.ShapeDtypeStruct((8, D_MODEL), F32)],
        scratch_shapes=[pltpu.VMEM((TM, D_MODEL), F32)],
        compiler_params=_params(2),
    )(dx3, z, x2, g2, w1, w2)


def _matmul_tn(a, b, col_shards=1):
    S, I = a.shape
    J = b.shape[1]
    TI = min(I, 1024)
    TJ = 1024 if J % 1024 == 0 else 896
    TS = min(S, 1024)
    nk = S // TS
    per = J // col_shards // TJ

    def body(a_ref, b_ref, o_ref, acc_ref):
        k = pl.program_id(2)

        @pl.when(k == 0)
        def _():
            acc_ref[...] = jnp.zeros_like(acc_ref)

        acc_ref[...] += _dot_tn(a_ref[...].astype(BF16), b_ref[...].astype(BF16))

        @pl.when(k == nk - 1)
        def _():
            o_ref[...] = acc_ref[...].astype(BF16)

    return pl.pallas_call(
        body, name="matmul_tn", grid=(I // TI, J // TJ, nk),
        in_specs=[pl.BlockSpec((TS, TI), lambda i, j, k: (k, i)), pl.BlockSpec((TS, TJ), lambda i, j, k: (k, j))],
        out_specs=pl.BlockSpec((None, TI, TJ), lambda i, j, k: (j // per, i, j % per)),
        out_shape=jax.ShapeDtypeStruct((col_shards, I, J // col_shards), BF16),
        scratch_shapes=[pltpu.VMEM((TI, TJ), F32)],
        compiler_params=_params(3),
    )(a, b)


def _mixer_out_bwd(dx2, wo4, att, h1, lng, lnb):
    S = dx2.shape[0]
    TM = min(512, S)

    def body(dx2_ref, wo_ref, att_ref, h1_ref, lng_ref, lnb_ref, hr_ref, datt_ref, dlt_ref, dh1_ref, sm_ref):
        @pl.when(pl.program_id(0) == 0)
        def _():
            sm_ref[...] = jnp.zeros_like(sm_ref)

        dxb = dx2_ref[...].astype(BF16)
        wa, wc = _wo_halves(wo_ref)
        datt = _dot_nt(dxb, wa)
        datt_ref[...] = datt.astype(BF16)
        prod = datt * att_ref[...].astype(F32)
        dlt_ref[...] = sum(_dot_nt(hr_ref[...], piece) for piece in _split3(prod))[0:ATT_HEADS, :]
        dh3 = _dot_nt(dxb, wc)
        h1 = h1_ref[...]
        mu = jnp.mean(h1, axis=-1, keepdims=True)
        d = h1 - mu
        rstd = lax.rsqrt(jnp.mean(d * d, axis=-1, keepdims=True) + EPS)
        n = d * rstd
        h2 = n * lng_ref[...] + lnb_ref[...]
        sg = _sigmoid(h2)
        dh2 = dh3 * (sg * (1.0 + h2 * (1.0 - sg)))
        dn = dh2 * lng_ref[...]
        dh1 = rstd * (dn - jnp.mean(dn, axis=-1, keepdims=True) - n * jnp.mean(dn * n, axis=-1, keepdims=True))
        dh1_ref[...] = dh1
        sm_ref[0:1, :] += jnp.sum(dh2 * n, axis=0, keepdims=True)
        sm_ref[1:2, :] += jnp.sum(dh2, axis=0, keepdims=True)
        sm_ref[2:3, :] += jnp.sum(dh1, axis=0, keepdims=True)

    row = lambda w: pl.BlockSpec((TM, w), lambda i: (i, 0))
    full = lambda a: pl.BlockSpec(a.shape, lambda i: (0,) * a.ndim)
    hr = _head_rows()
    return pl.pallas_call(
        body, name="mixer_out_bwd", grid=(S // TM,),
        in_specs=[row(D_MODEL), _wo_spec(wo4), row(ATT_W), row(CONV_CH), full(lng), full(lnb), full(hr)],
        out_specs=[row(ATT_W), pl.BlockSpec((ATT_HEADS, TM), lambda i: (0, i)), row(CONV_CH),
                   pl.BlockSpec((8, CONV_CH), lambda i: (0, 0))],
        out_shape=[jax.ShapeDtypeStruct((S, ATT_W), BF16), jax.ShapeDtypeStruct((ATT_HEADS, S), F32),
                   jax.ShapeDtypeStruct((S, CONV_CH), F32), jax.ShapeDtypeStruct((8, CONV_CH), F32)],
        compiler_params=_params(1),
    )(dx2, wo4, att, h1, lng, lnb, hr)


def _conv_glu_bwd(dh1, h0, proj, cw):
    S = dh1.shape[0]
    TM = min(512, S)
    nb = S // TM
    lead = CONV_HALO - CONV_TAPS + 1

    def body(dh1_ref, dnx_ref, h0_ref, hpv_ref, a_ref, g_ref, cw_ref, dag_ref, dcw_ref,
             dbuf_ref, hbuf_ref, ds_ref, hs_ref, dh0_ref, dcw8_ref):
        i = pl.program_id(0)

        @pl.when(i == 0)
        def _():
            dcw8_ref[...] = jnp.zeros_like(dcw8_ref)

        dbuf_ref[0:TM, :] = dh1_ref[...]
        dbuf_ref[TM:TM + CONV_HALO, :] = jnp.where(i < nb - 1, dnx_ref[0:CONV_HALO, :], 0.0)
        hbuf_ref[0:CONV_HALO, :] = jnp.where(i > 0, hpv_ref[TM - CONV_HALO:TM, :], 0.0)
        hbuf_ref[CONV_HALO:CONV_HALO + TM, :] = h0_ref[...]
        _fill_row_shifts(dbuf_ref, ds_ref, TM)
        _fill_row_shifts(hbuf_ref, hs_ref, TM)

        def conv_rows(step, _):
            r0 = pl.multiple_of(step * CONV_ROWS, CONV_ROWS)
            dh1 = dbuf_ref[pl.ds(r0, CONV_ROWS), :]
            part = jnp.zeros((CONV_ROWS, CONV_CH), F32)
            for j in range(CONV_TAPS):
                part = part + cw_ref[j:j + 1, :] * _row_shifted(dbuf_ref, ds_ref, CONV_TAPS - 1 - j, CONV_ROWS, r0)
                prod = dh1 * _row_shifted(hbuf_ref, hs_ref, lead + j, CONV_ROWS, r0)
                dcw8_ref[j] += jnp.sum(prod.reshape(CONV_ROWS // SUBLANES, SUBLANES, CONV_CH), axis=0)
            dh0_ref[pl.ds(r0, CONV_ROWS), :] = part
            return 0

        lax.fori_loop(0, TM // CONV_ROWS, conv_rows, 0)

        @pl.when(i == nb - 1)
        def _():
            dcw_ref[...] = jnp.sum(dcw8_ref[...], axis=1)

        dh0 = dh0_ref[...]
        sg = _sigmoid(g_ref[...])
        dag_ref[:, 0:CONV_CH] = (dh0 * sg).astype(BF16)
        dag_ref[:, CONV_CH:2 * CONV_CH] = (dh0 * a_ref[...] * sg * (1.0 - sg)).astype(BF16)

    blk = lambda fn: pl.BlockSpec((TM, CONV_CH), fn)
    return pl.pallas_call(
        body, name="conv_glu_bwd", grid=(nb,),
        in_specs=[blk(lambda i: (i, 0)), blk(lambda i: (jnp.minimum(i + 1, nb - 1), 0)),
                  blk(lambda i: (i, 0)), blk(lambda i: (jnp.maximum(i - 1, 0), 0)),
                  blk(lambda i: (i, O_A // CONV_CH)), blk(lambda i: (i, O_G // CONV_CH)),
                  pl.BlockSpec(cw.shape, lambda i: (0, 0))],
        out_specs=[pl.BlockSpec((TM, 2 * CONV_CH), lambda i: (i, 0)), pl.BlockSpec((CONV_HALO, CONV_CH), lambda i: (0, 0))],
        out_shape=[jax.ShapeDtypeStruct((S, 2 * CONV_CH), BF16), jax.ShapeDtypeStruct((CONV_HALO, CONV_CH), F32)],
        scratch_shapes=[pltpu.VMEM((TM + CONV_HALO, CONV_CH), F32), pltpu.VMEM((TM + CONV_HALO, CONV_CH), F32),
                        pltpu.VMEM((SUBLANES - 1, TM + CONV_HALO, CONV_CH), F32),
                        pltpu.VMEM((SUBLANES - 1, TM + CONV_HALO, CONV_CH), F32),
                        pltpu.VMEM((TM, CONV_CH), F32), pltpu.VMEM((CONV_HALO, SUBLANES, CONV_CH), F32)],
        compiler_params=_params(1),
    )(dh1, dh1, h0, h0, proj, proj, cw)


def _mixer_in_bwd(x, dx2, proj, dq, dk, dv, dag, dct, drb, g1, win, qg, kg, bf):
    S = x.shape[0]
    TM = min(512, S)
    nb = S // TM

    def body(x_ref, dx2_ref, qr_ref, kr_ref, fz_ref, dq_ref, dk_ref, dv_ref, dag_ref, dct_ref, drb_ref,
             g1_ref, win_ref, qg_ref, kg_ref, bf_ref, bd_ref, fold_ref, triu_ref, pick_ref,
             dproj_ref, dx_ref, dg1_ref, sm_ref, carry_ref, gsum_ref):
        i = pl.program_id(0)

        @pl.when(i == 0)
        def _():
            carry_ref[...] = jnp.zeros_like(carry_ref)
            gsum_ref[...] = jnp.zeros_like(gsum_ref)
            dg1_ref[...] = jnp.zeros_like(dg1_ref)
            sm_ref[...] = jnp.zeros_like(sm_ref)

        def headnorm_bwd(raw, dy, gain, scale, row):
            rs = lax.rsqrt(_head_sums(raw * raw, bd_ref[...]) * (1.0 / HEAD_DIM) + EPS)
            n = raw * rs
            gsum_ref[row:row + 1, :] += jnp.sum(dy * n, axis=0, keepdims=True) * scale
            dn = dy * (gain * scale)
            return rs * (dn - n * (_head_sums(dn * n, bd_ref[...]) * (1.0 / HEAD_DIM)))

        dproj_ref[:, O_Q:O_Q + ATT_W] = headnorm_bwd(qr_ref[...], dq_ref[...], qg_ref[...], QK_SCALE, 0).astype(BF16)
        dproj_ref[:, O_K:O_K + ATT_W] = headnorm_bwd(kr_ref[...], dk_ref[...], kg_ref[...], 1.0, 1).astype(BF16)
        dproj_ref[:, O_V:O_V + ATT_W] = dv_ref[...].astype(BF16)
        dproj_ref[:, O_A:O_A + 2 * CONV_CH] = dag_ref[...]

        dc8 = jnp.concatenate([dct_ref[...], jnp.zeros((128 - ATT_HEADS, TM), F32)], axis=0).T
        dc8 = dc8 + _dot_hi_r(drb_ref[...], pick_ref[...])
        dlogf = _dot_hi_l(triu_ref[...], dc8) + carry_ref[...]
        carry_ref[...] = dlogf[0:1, :]
        df = dlogf * _sigmoid(-(fz_ref[...] + bf_ref[...]))
        dproj_ref[:, O_F:O_F + 128] = df.astype(BF16)
        sm_ref[2:3, :] += jnp.sum(df, axis=0, keepdims=True)

        du1 = _dot_nt(dproj_ref[...], win_ref[...])
        xv = x_ref[...]
        r = lax.rsqrt(jnp.mean(xv * xv, axis=-1, keepdims=True) + EPS)
        n1 = xv * r
        t = du1 * g1_ref[...]
        dx_ref[...] = dx2_ref[...] + r * (t - n1 * jnp.mean(t * n1, axis=-1, keepdims=True))
        dg1_ref[0:1, :] += jnp.sum(du1 * n1, axis=0, keepdims=True)

        @pl.when(i == nb - 1)
        def _():
            sm_ref[0:2, :] = _dot_hi_r(gsum_ref[0:8, :], fold_ref[...])[0:2, :]

    rev = lambda w, cb=0: pl.BlockSpec((TM, w), lambda i: (nb - 1 - i, cb))
    full = lambda a: pl.BlockSpec(a.shape, lambda i: (0,) * a.ndim)
    bd, fold, triu = _head_blockdiag(), _head_fold(), _tril(TM).T
    consts = (g1, win, qg, kg, bf, bd, fold, triu, _head_pick())
    return pl.pallas_call(
        body, name="mixer_in_bwd", grid=(nb,),
        in_specs=[rev(D_MODEL), rev(D_MODEL), rev(ATT_W, O_Q // ATT_W), rev(ATT_W, O_K // ATT_W), rev(128, O_F // 128),
                  rev(ATT_W), rev(ATT_W), rev(ATT_W), rev(2 * CONV_CH),
                  pl.BlockSpec((ATT_HEADS, TM), lambda i: (0, nb - 1 - i)), rev(ATT_W)] + [full(a) for a in consts],
        out_specs=[rev(N_INP), rev(D_MODEL), pl.BlockSpec((8, D_MODEL), lambda i: (0, 0)),
                   pl.BlockSpec((8, 128), lambda i: (0, 0))],
        out_shape=[jax.ShapeDtypeStruct((S, N_INP), BF16), jax.ShapeDtypeStruct((S, D_MODEL), F32),
                   jax.ShapeDtypeStruct((8, D_MODEL), F32), jax.ShapeDtypeStruct((8, 128), F32)],
        scratch_shapes=[pltpu.VMEM((1, 128), F32), pltpu.VMEM((8, ATT_W), F32)],
        compiler_params=_params(1),
    )(x, dx2, proj, proj, proj, dq, dk, dv, dag, dct, drb, *consts)


def _layer_fwd(x, early, late, target=None):
    p = early(x)
    u1, proj, q, k, v, qa, ka, h0, h1, h3 = _mixer_in_fwd(
        x, p["g1"], p["win"], p["qg"], p["kg"], p["bf"], p["cw"], p["cvb"], p["lng"], p["lnb"])
    att, lse = _attn_fwd(q, qa, k, ka, v)
    p = dict(p, **late(att))
    x2, u2 = _mixer_out_fwd(x, att, h3, p["wo"], p["g2"])
    x3, z, hh, *loss_acc = _mlp_fwd(x2, u2, p["w1"], p["w2"], target)
    saved = dict(x=x, u1=u1, proj=proj, q=q, k=k, v=v, qa=qa, ka=ka, h0=h0, h1=h1, h3=h3, att=att, lse=lse,
                 x2=x2, u2=u2, z=z, hh=hh)
    return (x3 if target is None else (x3, loss_acc[0])), saved, p


def _tie(a, token):
    return a if token is None else a + token[0:1, 0:1].astype(a.dtype).reshape((1,) * a.ndim)


def _layer_bwd(dx3, s, p, reduce):
    dz, dx2, dg2 = _mlp_bwd(dx3, s["z"], s["x2"], p["g2"], p["w1"], p["w2"])
    g_w2 = _matmul_tn(s["hh"], dx3)
    g_w1 = _matmul_tn(s["u2"], dz, col_shards=N_CHIPS)
    token = reduce("a", {2: g_w1, 3: g_w2.reshape(N_CHIPS, D_FF // N_CHIPS, D_MODEL)})
    datt, dlt, dh1, sm_c = _mixer_out_bwd(dx2, p["wo"], s["att"], s["h1"], _tie(p["lng"], token), p["lnb"])
    g_wo = jnp.concatenate([_matmul_tn(s["att"], dx2)[0], _matmul_tn(s["h3"], dx2)[0]], axis=0)
    dag, dcw = _conv_glu_bwd(dh1, s["h0"], s["proj"], p["cw"])
    dq, dk, dv, dc4, drb = _attn_bwd(s["q"], s["qa"], s["k"], s["ka"], s["v"], datt, s["lse"],
                                     dlt.reshape(ATT_HEADS // 2, 2, dlt.shape[1]))
    dct = dc4.reshape(ATT_HEADS, dc4.shape[2])
    dproj, dx, dg1, sm_a = _mixer_in_bwd(s["x"], dx2, s["proj"], dq, dk, dv, dag, dct, drb,
                                         p["g1"], p["win"], p["qg"], p["kg"], p["bf"])
    g_win = _win_to_global(_matmul_tn(s["u1"], dproj)[0])
    g_win = g_win.reshape(D_MODEL, N_CHIPS, N_IN // N_CHIPS).transpose(1, 0, 2)
    token = reduce("b", {0: g_win, 1: g_wo.reshape(N_CHIPS, D_MODEL // N_CHIPS, D_MODEL)})
    small = dict(g1=dg1[0], g2=dg2[0], lng=sm_c[0], lnb=sm_c[1], cvb=sm_c[2], cw=dcw[0:CONV_TAPS],
                 qg=sm_a[0, 0:HEAD_DIM], kg=sm_a[1, 0:HEAD_DIM], bf=sm_a[2, 0:ATT_HEADS])
    return dx, small, token


def _local_step(x, target, weights, reduce):
    saved, layers = [], []
    h = x
    for l, (early, late) in enumerate(weights):
        h, s, p = _layer_fwd(h, early, late, target if l == len(weights) - 1 else None)
        saved.append(s)
        layers.append(p)
    dy, loss_acc = h
    loss = loss_acc[0, 0] * (0.5 / D_MODEL)
    smalls = []
    d, token = dy, None
    for l in reversed(range(len(layers))):
        d, small, token = _layer_bwd(d, saved[l], dict(layers[l], g2=_tie(layers[l]["g2"], token)), reduce(l))
        smalls.append(small)
    return loss, d, smalls[::-1]


def _win_to_internal(w):
    pad = jnp.zeros(w.shape[:-1] + (N_INP - N_IN,), w.dtype)
    return jnp.concatenate([w[..., :1536], w[..., 1544:], w[..., 1536:1544], pad], axis=-1)


def _win_to_global(g):
    return jnp.concatenate([g[..., :1536], g[..., O_F:O_F + ATT_HEADS], g[..., 1536:O_F]], axis=-1)


def _layer_params(l, win, cw, norm1_g, b_f, q_norm_g, k_norm_g, conv_b, conv_ln_g, conv_ln_b, norm2_g):
    row = lambda a: a.reshape(1, -1)
    return dict(
        win=win, cw=jnp.pad(cw, ((0, CONV_HALO - CONV_TAPS), (0, 0))),
        g1=row(norm1_g[l]), g2=row(norm2_g[l]),
        qg=row(jnp.tile(q_norm_g[l], ATT_HEADS)), kg=row(jnp.tile(k_norm_g[l], ATT_HEADS)),
        bf=row(jnp.pad(b_f[l], (0, 128 - ATT_HEADS))),
        cvb=row(conv_b[l]), lng=row(conv_ln_g[l]), lnb=row(conv_ln_b[l]))


def _place():
    x, y, c = lax.axis_index("x"), lax.axis_index("y"), lax.axis_index("c")
    chips = [(1 - x, y), (x, 1 - y), (1 - x, 1 - y)]
    return x, y, c, chips


HBM = pl.BlockSpec(memory_space=pltpu.HBM)
SEM = pl.BlockSpec(memory_space=pltpu.SEMAPHORE)
GATHER_PEERS = N_CHIPS


def _gather_peers():
    x, y, c, chips = _place()
    return [(*chip, c) for chip in chips] + [(x, y, 1 - c)], [2 * px + py for px, py in chips] + [2 * x + y]


def _gather_start(srcs):
    n = len(srcs)

    def body(*refs):
        ins, lands = refs[:n], refs[n:2 * n]
        send_sems, recv_sems, token = refs[2 * n], refs[2 * n + 1], refs[-1]
        me = 2 * lax.axis_index("x") + lax.axis_index("y")
        peers, _ = _gather_peers()
        for g in range(n):
            for j, to in enumerate(peers):
                pltpu.make_async_remote_copy(src_ref=ins[g], dst_ref=lands[g].at[me],
                                             send_sem=send_sems.at[GATHER_PEERS * g + j],
                                             recv_sem=recv_sems.at[GATHER_PEERS * g + j],
                                             device_id=to, device_id_type=MESH).start()
        token[...] = jnp.zeros_like(token)

    lands = [lax.empty((N_CHIPS,) + a.shape, a.dtype) for a in srcs]
    outs = pl.pallas_call(
        body, name="gather_start",
        in_specs=[HBM] * (2 * n),
        out_specs=[SEM, SEM] + [HBM] * (2 * n) + [pl.BlockSpec(memory_space=pltpu.VMEM)],
        out_shape=[pltpu.SemaphoreType.DMA((GATHER_PEERS * n,)), pltpu.SemaphoreType.DMA((GATHER_PEERS * n,))]
        + [pltpu.HBM(a.shape, a.dtype) for a in srcs] + [pltpu.HBM(a.shape, a.dtype) for a in lands]
        + [jax.ShapeDtypeStruct((8, 128), F32)],
        input_output_aliases={i: 2 + i for i in range(2 * n)},
        compiler_params=pltpu.CompilerParams(has_side_effects=pltpu.SideEffectType.DATAFLOW_SIDE_EFFECTING),
    )(*[pltpu.with_memory_space_constraint(a, pltpu.HBM) for a in srcs],
      *[pltpu.with_memory_space_constraint(a, pltpu.HBM) for a in lands])
    return outs[0], outs[1], list(outs[2:2 + n]), list(outs[2 + n:2 + 2 * n]), outs[-1]


def _gather_wait(name, groups, send_sems, recv_sems, srcs, lands, after):
    k = len(groups)

    def body(*refs):
        ins, lnd = refs[:k], refs[k:2 * k]
        ssem, rsem = refs[2 * k], refs[2 * k + 1]
        peers, slots = _gather_peers()
        for i, g in enumerate(groups):
            for j, to in enumerate(peers):
                cp = pltpu.make_async_remote_copy(src_ref=ins[i], dst_ref=lnd[i].at[slots[j]],
                                                  send_sem=ssem.at[GATHER_PEERS * g + j],
                                                  recv_sem=rsem.at[GATHER_PEERS * g + j],
                                                  device_id=to, device_id_type=MESH)
                cp.wait_send()
                cp.wait_recv()

    outs = pl.pallas_call(
        body, name=name,
        in_specs=[HBM] * (2 * k) + [SEM, SEM, ANY],
        out_specs=[HBM] * (2 * k),
        out_shape=[pltpu.HBM(a.shape, a.dtype) for a in srcs] + [pltpu.HBM(a.shape, a.dtype) for a in lands],
        input_output_aliases={i: i for i in range(2 * k)},
        compiler_params=pltpu.CompilerParams(has_side_effects=pltpu.SideEffectType.DATAFLOW_SIDE_EFFECTING),
    )(*srcs, *lands, send_sems, recv_sems, after)
    return list(outs[k:])


REDUCE_STEPS = 8


def _row_half(a):
    return a.shape[-2] // 2


def _half_swap(name, gs):
    n = len(gs)

    def body(*refs):
        ins, outs = refs[:n], refs[n:2 * n]
        send_sems, recv_sems = refs[2 * n:]
        x, y, c, _ = _place()
        copies = []
        for a in range(n):
            h = _row_half(gs[a])
            copies.append(pltpu.make_async_remote_copy(
                src_ref=ins[a].at[:, pl.ds((1 - c) * h, h), :], dst_ref=outs[a], send_sem=send_sems.at[a],
                recv_sem=recv_sems.at[a], device_id=(x, y, 1 - c), device_id_type=MESH))
        for cp in copies:
            cp.start()
        for cp in copies:
            cp.wait()

    return pl.pallas_call(
        body, name=name,
        in_specs=[ANY] * n, out_specs=[ANY] * n,
        out_shape=[jax.ShapeDtypeStruct((N_CHIPS, _row_half(a), a.shape[-1]), a.dtype) for a in gs],
        scratch_shapes=[pltpu.SemaphoreType.DMA((n,)), pltpu.SemaphoreType.DMA((n,))],
        compiler_params=pltpu.CompilerParams(has_side_effects=True),
    )(*gs)


def _half_specs(gs, row_block):
    tiles = [_row_half(a) // REDUCE_STEPS for a in gs]
    return [pl.BlockSpec((N_CHIPS, t, a.shape[-1]), lambda i, p: (0, row_block(i, p), 0)) for a, t in zip(gs, tiles)]


def _half_add(name, gs, got, place):
    n = len(gs)

    def body(place_ref, *refs):
        own, theirs, outs = refs[:n], refs[n:2 * n], refs[2 * n:]
        for a in range(n):
            outs[a][...] = (own[a][...].astype(F32) + theirs[a][...].astype(F32)).astype(BF16)

    plain = _half_specs(gs, lambda i, p: i)
    return pl.pallas_call(
        body, name=name,
        grid_spec=pltpu.PrefetchScalarGridSpec(
            num_scalar_prefetch=1, grid=(REDUCE_STEPS,),
            in_specs=_half_specs(gs, lambda i, p: p[1] * REDUCE_STEPS + i) + plain, out_specs=plain),
        out_shape=[jax.ShapeDtypeStruct(a.shape, BF16) for a in got],
        compiler_params=_params(1),
    )(place, *gs, *got)


def _exchange_copies(parts, lands, send_sems, recv_sems):
    x, y, c, chips = _place()
    return [pltpu.make_async_remote_copy(src_ref=parts[a].at[2 * px + py], dst_ref=lands[a].at[j],
                                         send_sem=send_sems.at[3 * a + j], recv_sem=recv_sems.at[3 * a + j],
                                         device_id=(px, py, c), device_id_type=MESH)
            for a in range(len(parts)) for j, (px, py) in enumerate(chips)]


def _exchange_start(name, parts):
    n = len(parts)

    def body(*refs):
        _ = [cp.start() for cp in _exchange_copies(refs[:n], refs[n:2 * n], refs[2 * n], refs[2 * n + 1])]
        refs[-1][...] = jnp.zeros_like(refs[-1])

    lands = [lax.empty((N_CHIPS - 1,) + a.shape[1:], a.dtype) for a in parts]
    outs = pl.pallas_call(
        body, name=name,
        in_specs=[HBM] * (2 * n),
        out_specs=[SEM, SEM] + [HBM] * (2 * n) + [pl.BlockSpec(memory_space=pltpu.VMEM)],
        out_shape=[pltpu.SemaphoreType.DMA((3 * n,)), pltpu.SemaphoreType.DMA((3 * n,))]
        + [pltpu.HBM(a.shape, a.dtype) for a in parts] + [pltpu.HBM(a.shape, a.dtype) for a in lands]
        + [jax.ShapeDtypeStruct((8, 128), F32)],
        input_output_aliases={i: 2 + i for i in range(2 * n)},
        compiler_params=pltpu.CompilerParams(has_side_effects=pltpu.SideEffectType.DATAFLOW_SIDE_EFFECTING),
    )(*[pltpu.with_memory_space_constraint(a, pltpu.HBM) for a in parts],
      *[pltpu.with_memory_space_constraint(a, pltpu.HBM) for a in lands])
    return outs[0], outs[1], list(outs[2:2 + n]), list(outs[2 + n:2 + 2 * n]), outs[-1]


def _exchange_wait(name, send_sems, recv_sems, parts, lands, after):
    n = len(parts)

    def body(*refs):
        for cp in _exchange_copies(refs[:n], refs[n:2 * n], refs[2 * n], refs[2 * n + 1]):
            cp.wait_send()
            cp.wait_recv()

    outs = pl.pallas_call(
        body, name=name,
        in_specs=[HBM] * (2 * n) + [SEM, SEM, ANY],
        out_specs=[HBM] * (2 * n),
        out_shape=[pltpu.HBM(a.shape, a.dtype) for a in parts] + [pltpu.HBM(a.shape, a.dtype) for a in lands],
        input_output_aliases={i: i for i in range(2 * n)},
        compiler_params=pltpu.CompilerParams(has_side_effects=pltpu.SideEffectType.DATAFLOW_SIDE_EFFECTING),
    )(*parts, *lands, send_sems, recv_sems, after)
    return list(outs[:n]), list(outs[n:])


def _chip_sum(name, parts, lands, sums, place, layer):
    n = len(parts)
    tiles = [a.shape[-2] // REDUCE_STEPS for a in parts]

    def body(place_ref, *refs):
        own, got, outs = refs[:n], refs[n:2 * n], refs[3 * n:]
        for a in range(n):
            tot = own[a][...].astype(F32)
            for j in range(N_CHIPS - 1):
                tot = tot + got[a][j].astype(F32)
            outs[a][...] = tot

    own_specs = [pl.BlockSpec((None, t, a.shape[-1]), lambda i, p: (p[0], i, 0)) for a, t in zip(parts, tiles)]
    got_specs = [pl.BlockSpec((N_CHIPS - 1, t, a.shape[-1]), lambda i, p: (0, i, 0)) for a, t in zip(parts, tiles)]
    out_specs = [pl.BlockSpec((None, t, a.shape[-1]), lambda i, p: (layer, p[1] * REDUCE_STEPS + i, 0))
                 for a, t in zip(parts, tiles)]
    return pl.pallas_call(
        body, name=name,
        grid_spec=pltpu.PrefetchScalarGridSpec(num_scalar_prefetch=1, grid=(REDUCE_STEPS,),
                                               in_specs=own_specs + got_specs + [ANY] * n, out_specs=out_specs),
        out_shape=[jax.ShapeDtypeStruct(a.shape, F32) for a in sums],
        input_output_aliases={1 + 2 * n + a: a for a in range(n)},
        compiler_params=_params(1),
    )(place, *parts, *lands, *sums)


def _half_fill(name, sums, layer):
    n = len(sums)

    def body(*refs):
        ins, outs = refs[:n], refs[n:2 * n]
        send_sems, recv_sems = refs[2 * n:]
        x, y, c, _ = _place()
        copies = []
        for a in range(n):
            h = _row_half(sums[a])
            copies.append(pltpu.make_async_remote_copy(
                src_ref=ins[a].at[layer, pl.ds(c * h, h), :], dst_ref=outs[a].at[layer, pl.ds(c * h, h), :],
                send_sem=send_sems.at[a], recv_sem=recv_sems.at[a], device_id=(x, y, 1 - c), device_id_type=MESH))
        for cp in copies:
            cp.start()
        for a in range(n):
            h = _row_half(sums[a])
            theirs = outs[a].at[layer, pl.ds((1 - c) * h, h), :]
            pltpu.make_async_remote_copy(src_ref=theirs, dst_ref=theirs, send_sem=send_sems.at[a], recv_sem=recv_sems.at[a],
                                         device_id=(x, y, 1 - c), device_id_type=MESH).wait_recv()
        for cp in copies:
            cp.wait_send()

    return pl.pallas_call(
        body, name=name,
        in_specs=[ANY] * n, out_specs=[ANY] * n,
        out_shape=[jax.ShapeDtypeStruct(a.shape, a.dtype) for a in sums],
        input_output_aliases={a: a for a in range(n)},
        scratch_shapes=[pltpu.SemaphoreType.DMA((n,)), pltpu.SemaphoreType.DMA((n,))],
        compiler_params=pltpu.CompilerParams(has_side_effects=True),
    )(*sums)


def _adamw_math(w, g, m, v):
    m = ADAM_B1 * m + (1.0 - ADAM_B1) * g
    v = ADAM_B2 * v + (1.0 - ADAM_B2) * (g * g)
    m_hat = m / (1.0 - ADAM_B1 ** ADAM_STEP)
    v_hat = v / (1.0 - ADAM_B2 ** ADAM_STEP)
    delta = -ADAM_LR * (m_hat / (jnp.sqrt(v_hat) + ADAM_EPS) + ADAM_WD * w)
    return delta, m, v


def _adamw(name, ws, gs, ms, vs, layer, prev=None):
    n = len(ws)
    steps = 8
    tiles = [a.shape[-2] // steps for a in ws]

    def body(*refs):
        w_r, g_r, m_r, v_r = refs[:n], refs[n:2 * n], refs[2 * n:3 * n], refs[3 * n:4 * n]
        g_o, d_o, m_o, v_o = (refs[-4 * n:][k * n:(k + 1) * n] for k in range(4))
        for a in range(n):
            g = g_r[a][...]
            d, m, v = _adamw_math(w_r[a][...], g, m_r[a][...], v_r[a][...])
            g_o[a][...] = g
            d_o[a][...] = d
            m_o[a][...] = m
            v_o[a][...] = v

    specs = [pl.BlockSpec((None, t, a.shape[-1]), lambda i: (layer, i, 0)) for a, t in zip(ws, tiles)]
    held = [] if prev is None else [buf for kind in zip(*prev) for buf in kind]
    outs = pl.pallas_call(
        body, name=name, grid=(steps,),
        in_specs=specs * 4 + [ANY] * len(held), out_specs=specs * 4,
        out_shape=[jax.ShapeDtypeStruct(a.shape, F32) for a in ws] * 4,
        input_output_aliases={4 * n + k: k for k in range(len(held))},
        compiler_params=_params(1),
    )(*ws, *gs, *ms, *vs, *held)
    return [[outs[k * n + a] for k in range(4)] for a in range(n)]


SMALL_W = 512


def _small_allreduce_adamw(g, w, m, v, cw_w, cw_m, cw_v, cw_row0):
    R = g.shape[0]
    n_l = cw_w.shape[0]

    def body(g_ref, w_ref, m_ref, v_ref, cww_ref, cwm_ref, cwv_ref,
             gs_ref, d_ref, mo_ref, vo_ref, cg_ref, cd_ref, cmo_ref, cvo_ref,
             slots_ref, send_sems, recv_sems):
        x, y, c, _ = _place()
        me = 4 * x + 2 * y + c
        slots_ref[me] = g_ref[...]
        sends = []
        for d in range(1, 8):
            px, py, pc = x ^ (d >> 2), y ^ ((d >> 1) & 1), c ^ (d & 1)
            cp = pltpu.make_async_remote_copy(src_ref=g_ref, dst_ref=slots_ref.at[me], send_sem=send_sems.at[d - 1],
                                              recv_sem=recv_sems.at[d - 1], device_id=(px, py, pc), device_id_type=MESH)
            cp.start()
            sends.append(cp)
        for d in range(1, 8):
            px, py, pc = x ^ (d >> 2), y ^ ((d >> 1) & 1), c ^ (d & 1)
            slot = slots_ref.at[4 * px + 2 * py + pc]
            pltpu.make_async_remote_copy(src_ref=slot, dst_ref=slot, send_sem=send_sems.at[d - 1],
                                         recv_sem=recv_sems.at[d - 1], device_id=(px, py, pc),
                                         device_id_type=MESH).wait_recv()
        for cp in sends:
            cp.wait_send()
        tot = slots_ref[0]
        for k in range(1, 8):
            tot = tot + slots_ref[k]
        gs_ref[...] = tot
        dl, mn, vn = _adamw_math(w_ref[...], tot, m_ref[...], v_ref[...])
        d_ref[...] = dl
        mo_ref[...] = mn
        vo_ref[...] = vn
        chip = 2 * x + y
        for l in range(n_l):
            rows = tot[cw_row0[l]:cw_row0[l] + CONV_HALO, :]
            mine = rows[:, 0:128]
            for k in range(1, N_CHIPS):
                mine = jnp.where(chip == k, rows[:, 128 * k:128 * (k + 1)], mine)
            cg_ref[l] = mine
            dl, mn, vn = _adamw_math(cww_ref[l], mine, cwm_ref[l], cwv_ref[l])
            cd_ref[l] = dl
            cmo_ref[l] = mn
            cvo_ref[l] = vn

    vm = pl.BlockSpec(memory_space=pltpu.VMEM)
    small = jax.ShapeDtypeStruct((R, SMALL_W), F32)
    conv = jax.ShapeDtypeStruct(cw_w.shape, F32)
    return pl.pallas_call(
        body, name="small_allreduce_adamw",
        in_specs=[vm] * 7, out_specs=[vm] * 8,
        out_shape=[small] * 4 + [conv] * 4,
        scratch_shapes=[pltpu.VMEM((8, R, SMALL_W), F32), pltpu.SemaphoreType.DMA((7,)), pltpu.SemaphoreType.DMA((7,))],
        compiler_params=pltpu.CompilerParams(has_side_effects=True, vmem_limit_bytes=VMEM_LIMIT),
    )(g, w, m, v, cw_w, cw_m, cw_v)


SMALL_LAYOUT = (("conv_w", CONV_HALO), ("norm1_g", 2), ("norm2_g", 2), ("conv_b", 1), ("conv_ln_g", 1),
                ("conv_ln_b", 1), ("q_norm_g", 1), ("k_norm_g", 1), ("b_f", 1))
SMALL_ROWS = sum(r for _, r in SMALL_LAYOUT)
SMALL_ROWS_PAD = 48
LOSS_ROW = SMALL_ROWS


def _pack_small(per_layer):
    flat = []
    for d in per_layer:
        for name, r in SMALL_LAYOUT:
            n = r * SMALL_W
            a = d.get(name)
            if a is None:
                flat.append(jnp.zeros((n,), F32))
                continue
            flat.append(a.reshape(-1))
            if a.size < n:
                flat.append(jnp.zeros((n - a.size,), F32))
        spare = (SMALL_ROWS_PAD - SMALL_ROWS) * SMALL_W
        if "spare" in d:
            flat.append(d["spare"].reshape(-1))
            spare -= d["spare"].size
        flat.append(jnp.zeros((spare,), F32))
    return jnp.concatenate(flat).reshape(-1, SMALL_W)


def _unpack_small(packed, name, size):
    row0 = 0
    for nm, r in SMALL_LAYOUT:
        if nm == name:
            break
        row0 += r
    per_layer = packed.reshape(-1, SMALL_ROWS_PAD * SMALL_W)
    return per_layer[:, row0 * SMALL_W:row0 * SMALL_W + size]


SMALL_SIZES = dict(norm1_g=D_MODEL, norm2_g=D_MODEL, conv_b=CONV_CH, conv_ln_g=CONV_CH, conv_ln_b=CONV_CH,
                   q_norm_g=HEAD_DIM, k_norm_g=HEAD_DIM, b_f=ATT_HEADS)
SMALL_KEYS = dict(norm1_g="g1", norm2_g="g2", conv_b="cvb", conv_ln_g="lng", conv_ln_b="lnb",
                  q_norm_g="qg", k_norm_g="kg", b_f="bf", conv_w="cw")
CONV_W_ROW0 = 0


def kernel(x, norm1_g, w_in, b_f, q_norm_g, k_norm_g, conv_w, conv_b, conv_ln_g, conv_ln_b, w_o, norm2_g, w_mlp_in, w_mlp_out, loss_target, m_norm1_g, m_w_in, m_b_f, m_q_norm_g, m_k_norm_g, m_conv_w, m_conv_b, m_conv_ln_g, m_conv_ln_b, m_w_o, m_norm2_g, m_w_mlp_in, m_w_mlp_out, v_norm1_g, v_w_in, v_b_f, v_q_norm_g, v_k_norm_g, v_conv_w, v_conv_b, v_conv_ln_g, v_conv_ln_b, v_w_o, v_norm2_g, v_w_mlp_in, v_w_mlp_out):
    n_l = w_in.shape[0]

    per_layer = lambda l: [w_in[l].astype(BF16), conv_w[l], w_o[l].astype(BF16), w_mlp_in[l].astype(BF16),
                           w_mlp_out[l].astype(BF16)]
    n_w = len(per_layer(0))
    send_sems, recv_sems, srcs, lands, token = _gather_start([a for l in range(n_l) for a in per_layer(l)])

    def layer_weights(l):
        def wait(tag, which, after):
            groups = [n_w * l + i for i in which]
            return _gather_wait(f"gather_wait_{tag}{l}", groups, send_sems, recv_sems,
                                [srcs[g] for g in groups], [lands[g] for g in groups], after)

        def early(after):
            g_in, g_cw = wait("a", (0, 1), token if l == 0 else after)
            win = _win_to_internal(jnp.concatenate([g_in[k] for k in range(N_CHIPS)], axis=-1))
            cw = jnp.concatenate([g_cw[k] for k in range(N_CHIPS)], axis=-1)
            return _layer_params(l, win, cw, norm1_g, b_f, q_norm_g, k_norm_g, conv_b, conv_ln_g, conv_ln_b, norm2_g)

        def late(after):
            wo, w1, w2 = wait("b", (2, 3, 4), after)
            return dict(wo=wo, w1=w1, w2=w2)

        return early, late

    place = jnp.stack([2 * lax.axis_index("x") + lax.axis_index("y"), lax.axis_index("c")]).astype(jnp.int32)
    big_w = [w_in, w_o, w_mlp_in, w_mlp_out]
    pending, started, last = [], [], []

    def start_group(l, tag, grads, token=None):
        which, gs = list(grads), [_tie(g, token) for g in grads.values()]
        got = _half_swap(f"half_swap_{tag}{l}", gs)
        parts = _half_add(f"half_add_{tag}{l}", gs, got, place)
        send, recv, parts, lands, token = _exchange_start(f"exchange_start_{tag}{l}", parts)
        pending.append((f"{tag}{l}", l, which, send, recv, parts, lands))
        started.append(token)
        return token

    def reduce(l):
        def group(tag, grads):
            if (l, tag) == (0, "b"):
                last.append(grads)
                return None
            return start_group(l, tag, grads)
        return group

    loss, dx, smalls = _local_step(x[0], loss_target[0], [layer_weights(l) for l in range(n_l)], reduce)
    env = dict(norm1_g=(norm1_g, m_norm1_g, v_norm1_g), norm2_g=(norm2_g, m_norm2_g, v_norm2_g),
               conv_b=(conv_b, m_conv_b, v_conv_b), conv_ln_g=(conv_ln_g, m_conv_ln_g, v_conv_ln_g),
               conv_ln_b=(conv_ln_b, m_conv_ln_b, v_conv_ln_b), q_norm_g=(q_norm_g, m_q_norm_g, v_q_norm_g),
               k_norm_g=(k_norm_g, m_k_norm_g, v_k_norm_g), b_f=(b_f, m_b_f, v_b_f))
    g_dicts = [{nm: s[key] for nm, key in SMALL_KEYS.items()} for s in smalls]
    g_dicts[0]["spare"] = loss
    g_pack = _pack_small(g_dicts)
    packs = [_pack_small([{nm: env[nm][t][l] for nm in env} for l in range(n_l)]) for t in range(3)]
    pad_cw = lambda a: jnp.pad(a, ((0, 0), (0, CONV_HALO - CONV_TAPS), (0, 0)))
    cw_row0 = tuple(l * SMALL_ROWS_PAD + CONV_W_ROW0 for l in range(n_l))
    gs, ds, ms, vs, cg, cd, cm, cv = _small_allreduce_adamw(
        g_pack, packs[0], packs[1], packs[2], pad_cw(conv_w), pad_cw(m_conv_w), pad_cw(v_conv_w), cw_row0)

    start_group(0, "b", last[0], gs[SMALL_ROWS_PAD - 1:])
    sums = [lax.empty(w.shape, F32) for w in big_w]
    big_m = [m_w_in, m_w_o, m_w_mlp_in, m_w_mlp_out]
    big_v = [v_w_in, v_w_o, v_w_mlp_in, v_w_mlp_out]
    updated = [None] * len(big_w)
    after = started[-1]
    for tag, l, which, send, recv, parts, lands in pending:
        parts, lands = _exchange_wait(f"exchange_wait_{tag}", send, recv, parts, lands, after)
        done = _chip_sum(f"chip_sum_{tag}", parts, lands, [sums[i] for i in which], place, l)
        done = _half_fill(f"half_fill_{tag}", done, l)
        for i, a in zip(which, done):
            sums[i] = a
        prev = None if updated[which[0]] is None else [updated[i] for i in which]
        new = _adamw(f"adamw_{tag}", [big_w[i] for i in which], done, [big_m[i] for i in which],
                     [big_v[i] for i in which], l, prev)
        for i, r in zip(which, new):
            updated[i] = r
        after = new[0][1]
    g_big, d_big, nm_big, nv_big = ([r[k] for r in updated] for k in range(4))

    def small_out(packed, conv):
        o = {nm: _unpack_small(packed, nm, sz) for nm, sz in SMALL_SIZES.items()}
        o["conv_w"] = conv[:, 0:CONV_TAPS, :]
        return o

    def ordered(small, big):
        return (small["norm1_g"], big[0], small["b_f"], small["q_norm_g"], small["k_norm_g"], small["conv_w"],
                small["conv_b"], small["conv_ln_g"], small["conv_ln_b"], big[1], small["norm2_g"], big[2], big[3])

    return (gs[LOSS_ROW, 0], dx[None],
            *ordered(small_out(gs, cg), g_big), *ordered(small_out(ds, cd), d_big),
            *ordered(small_out(ms, cm), nm_big), *ordered(small_out(vs, cv), nv_big))
```

```python
import jax
import jax.numpy as jnp
from jax import lax
from jax.experimental import pallas as pl
from jax.experimental.pallas import tpu as pltpu

F32 = jnp.float32
BF16 = jnp.bfloat16

D_MODEL = 1024
ATT_HEADS = 8
HEAD_DIM = 64
ATT_W = ATT_HEADS * HEAD_DIM
CONV_CH = 512
CONV_TAPS = 31
CONV_HALO = 32
D_FF = 4 * D_MODEL
N_IN = 3 * ATT_W + ATT_HEADS + 2 * CONV_CH
O_Q, O_K, O_V, O_A, O_G, O_F = 0, 512, 1024, 1536, 2048, 2560
N_INP = O_F + 128
EPS = 1e-6
QK_SCALE = 0.125

ADAM_LR = 0.001
ADAM_B1 = 0.9
ADAM_B2 = 0.999
ADAM_EPS = 1e-08
ADAM_WD = 0.01
ADAM_STEP = 10

N_CHIPS = 4
VMEM_LIMIT = 52 * 1024 * 1024
MESH = pl.DeviceIdType.MESH
ANY = pl.BlockSpec(memory_space=pl.ANY)


def _params(n_axes, **kw):
    return pltpu.CompilerParams(dimension_semantics=("arbitrary",) * n_axes,
                                vmem_limit_bytes=VMEM_LIMIT, **kw)


def _dot(a, b):
    return jnp.dot(a, b, preferred_element_type=F32)


def _dot_nt(a, b):
    return lax.dot_general(a, b, (((1,), (1,)), ((), ())), preferred_element_type=F32)


def _dot_tn(a, b):
    return lax.dot_general(a, b, (((0,), (0,)), ((), ())), preferred_element_type=F32)


def _split3(a):
    a1 = a.astype(BF16)
    r = a - a1.astype(F32)
    a2 = r.astype(BF16)
    a3 = (r - a2.astype(F32)).astype(BF16)
    return a1, a2, a3


def _dot_hi_r(a, b_exact):
    return sum(_dot(p, b_exact) for p in _split3(a))


def _head_sums(a, blockdiag):
    a1 = a.astype(BF16)
    a2 = (a - a1.astype(F32)).astype(BF16)
    return _dot(a1, blockdiag) + _dot(a2, blockdiag)


def _dot_hi_l(a_exact, b):
    return sum(_dot(a_exact, p) for p in _split3(b))


def _sigmoid(x):
    return 1.0 / (1.0 + jnp.exp(-x))


def _head_blockdiag():
    i = jnp.arange(ATT_W) // HEAD_DIM
    return (i[:, None] == i[None, :]).astype(BF16)


AUG_LANES = 8


def _aug_place(first):
    piece = jnp.arange(3 * 128)[:, None] // 128
    h = jnp.arange(3 * 128)[:, None] % 128
    lane = jnp.arange(ATT_W)[None, :]
    return ((h < ATT_HEADS) & (lane == 128 * (h // 2) + AUG_LANES * (h % 2) + first + piece)).astype(BF16)


def _aug_ones(first):
    lane = jnp.arange(ATT_W) % 128
    pos = lane % AUG_LANES
    return ((lane < 2 * AUG_LANES) & (pos >= first) & (pos < first + 3)).astype(F32).reshape(1, ATT_W)


def _head_rows():
    h = jnp.arange(2 * ATT_HEADS)[:, None]
    i = jnp.arange(ATT_W)[None, :] // HEAD_DIM
    return (h == i).astype(BF16)


def _head_fold():
    i = jnp.arange(ATT_W)[:, None] % HEAD_DIM
    j = jnp.arange(128)[None, :]
    return (i == j).astype(BF16)


def _head_pick():
    i = jnp.arange(ATT_W)[:, None]
    h = jnp.arange(128)[None, :]
    return (i == h * HEAD_DIM).astype(BF16)


def _tril(n):
    r = jnp.arange(n)
    return (r[:, None] >= r[None, :]).astype(BF16)


SUBLANES = 8


def _fill_row_shifts(buf_ref, shifts_ref, tm):
    n = tm + CONV_HALO - SUBLANES
    for b in range(1, SUBLANES):
        shifts_ref[b - 1, 0:n, :] = buf_ref[pl.ds(b, n), :]


def _row_shifted(buf_ref, shifts_ref, offset, rows, base=0):
    a, b = divmod(offset, SUBLANES)
    start = pl.multiple_of(base + SUBLANES * a, SUBLANES)
    if b == 0:
        return buf_ref[pl.ds(start, rows), :]
    return shifts_ref[b - 1, pl.ds(start, rows), :]


CONV_ROWS = 32


def _mixer_in_fwd(x, g1, win, qg, kg, bf, cw, cvb, lng, lnb):
    S = x.shape[0]
    TM = min(512, S)
    nb = S // TM

    def body(x_ref, g1_ref, win_ref, qg_ref, kg_ref, bf_ref, cw_ref, cvb_ref, lng_ref, lnb_ref,
             bd_ref, tri_ref, pq_ref, pk_ref, oq_ref, ok_ref,
             u1_ref, proj_ref, q_ref, k_ref, v_ref, qa_ref, ka_ref, h0_ref, h1_ref, h3_ref,
             carry_ref, hbuf_ref, hs_ref):
        i = pl.program_id(0)

        @pl.when(i == 0)
        def _():
            carry_ref[...] = jnp.zeros_like(carry_ref)
            hbuf_ref[0:CONV_HALO, :] = jnp.zeros((CONV_HALO, CONV_CH), F32)

        @pl.when(i > 0)
        def _():
            hbuf_ref[0:CONV_HALO, :] = hbuf_ref[TM:TM + CONV_HALO, :]

        xv = x_ref[...]
        r = lax.rsqrt(jnp.mean(xv * xv, axis=-1, keepdims=True) + EPS)
        u = (xv * r * g1_ref[...]).astype(BF16)
        u1_ref[...] = u
        proj_ref[...] = _dot(u, win_ref[...])

        def headnorm(raw, gain):
            ss = _head_sums(raw * raw, bd_ref[...]) * (1.0 / HEAD_DIM)
            return raw * lax.rsqrt(ss + EPS) * gain

        q_ref[...] = (headnorm(proj_ref[:, O_Q:O_Q + ATT_W], qg_ref[...]) * QK_SCALE).astype(BF16)
        k_ref[...] = headnorm(proj_ref[:, O_K:O_K + ATT_W], kg_ref[...]).astype(BF16)
        v_ref[...] = proj_ref[:, O_V:O_V + ATT_W].astype(BF16)

        zf = proj_ref[:, O_F:O_F + 128] + bf_ref[...]
        logf = jnp.minimum(zf, 0.0) - jnp.log(1.0 + jnp.exp(-jnp.abs(zf)))
        lane = lax.broadcasted_iota(jnp.int32, (TM, 128), 1)
        logf = jnp.where(lane < ATT_HEADS, logf, 0.0)
        c8 = _dot_hi_l(tri_ref[...], logf) + carry_ref[...]
        carry_ref[...] = c8[TM - 1:TM, :]
        pieces = jnp.concatenate(_split3(c8), axis=1)
        qa_ref[...] = (_dot(pieces, pq_ref[...]) + oq_ref[...]).astype(BF16)
        ka_ref[...] = (ok_ref[...] - _dot(pieces, pk_ref[...])).astype(BF16)

        h0 = proj_ref[:, O_A:O_A + CONV_CH] * _sigmoid(proj_ref[:, O_G:O_G + CONV_CH])
        h0_ref[...] = h0
        hbuf_ref[CONV_HALO:CONV_HALO + TM, :] = h0
        _fill_row_shifts(hbuf_ref, hs_ref, TM)
        acc = jnp.zeros((TM, CONV_CH), F32) + cvb_ref[...]
        for j in range(CONV_TAPS):
            acc = acc + cw_ref[j:j + 1, :] * _row_shifted(hbuf_ref, hs_ref, CONV_HALO - CONV_TAPS + 1 + j, TM)
        h1_ref[...] = acc
        mu = jnp.mean(acc, axis=-1, keepdims=True)
        d = acc - mu
        var = jnp.mean(d * d, axis=-1, keepdims=True)
        h2 = d * lax.rsqrt(var + EPS) * lng_ref[...] + lnb_ref[...]
        h3_ref[...] = (h2 * _sigmoid(h2)).astype(BF16)

    row = lambda w: pl.BlockSpec((TM, w), lambda i: (i, 0))
    full = lambda a: pl.BlockSpec(a.shape, lambda i: (0,) * a.ndim)
    ins = (x, g1, win, qg, kg, bf, cw, cvb, lng, lnb, _head_blockdiag(), _tril(TM),
           _aug_place(0), _aug_place(3), _aug_ones(3), _aug_ones(0))
    return pl.pallas_call(
        body, name="mixer_in_fwd", grid=(nb,),
        in_specs=[row(D_MODEL)] + [full(a) for a in ins[1:]],
        out_specs=[row(D_MODEL), row(N_INP), row(ATT_W), row(ATT_W), row(ATT_W), row(ATT_W), row(ATT_W),
                   row(CONV_CH), row(CONV_CH), row(CONV_CH)],
        out_shape=[jax.ShapeDtypeStruct((S, D_MODEL), BF16),
                   jax.ShapeDtypeStruct((S, N_INP), F32),
                   jax.ShapeDtypeStruct((S, ATT_W), BF16),
                   jax.ShapeDtypeStruct((S, ATT_W), BF16),
                   jax.ShapeDtypeStruct((S, ATT_W), BF16),
                   jax.ShapeDtypeStruct((S, ATT_W), BF16),
                   jax.ShapeDtypeStruct((S, ATT_W), BF16),
                   jax.ShapeDtypeStruct((S, CONV_CH), F32),
                   jax.ShapeDtypeStruct((S, CONV_CH), F32),
                   jax.ShapeDtypeStruct((S, CONV_CH), BF16)],
        scratch_shapes=[pltpu.VMEM((1, 128), F32), pltpu.VMEM((TM + CONV_HALO, CONV_CH), F32),
                        pltpu.VMEM((SUBLANES - 1, TM + CONV_HALO, CONV_CH), F32)],
        compiler_params=_params(1),
    )(*ins)


def _pair_heads(lo, alo, x, xa):
    z = jnp.zeros_like(x)
    return (jnp.concatenate([jnp.where(lo, x, z), jnp.where(alo, xa, z)], axis=1),
            jnp.concatenate([jnp.where(lo, z, x), jnp.where(alo, z, xa)], axis=1))


def _attn_fwd(q, qa, k, ka, v):
    S = q.shape[0]
    T = min(1024, S)
    nq = S // T

    def body(q_ref, qa_ref, k_ref, ka_ref, v_ref, o_ref, lse_ref):
        qi = pl.program_id(1)
        lane = lax.broadcasted_iota(jnp.int32, (T, 128), 1)
        lo = lane < HEAD_DIM
        qm = _pair_heads(lo, lane < AUG_LANES, q_ref[...], qa_ref[...])
        tril = (lax.broadcasted_iota(jnp.int32, (T, T), 0) >= lax.broadcasted_iota(jnp.int32, (T, T), 1))

        def step(kj, carry, masked):
            off = pl.multiple_of(kj * T, T)
            kb = jnp.concatenate([k_ref[pl.ds(off, T), :], ka_ref[pl.ds(off, T), :]], axis=1)
            vb = v_ref[pl.ds(off, T), :]
            new = []
            for h in range(2):
                m, l, acc = carry[3 * h:3 * h + 3]
                s = _dot_nt(qm[h], kb)
                if masked:
                    s = jnp.where(tril, s, -1e30)
                m_new = jnp.maximum(m, jnp.max(s, axis=-1, keepdims=True))
                alpha = jnp.exp(m - m_new)
                p = jnp.exp(s - m_new)
                l = alpha * l + jnp.sum(p, axis=-1, keepdims=True)
                acc = alpha * acc + _dot(p.astype(BF16), vb)
                new += [m_new, l, acc]
            return tuple(new)

        init = (jnp.full((T, 1), -1e30, F32), jnp.zeros((T, 1), F32), jnp.zeros((T, 128), F32)) * 2
        carry = lax.fori_loop(0, qi, lambda kj, c: step(kj, c, False), init)
        m0, l0, a0, m1, l1, a1 = step(qi, carry, True)
        o_ref[...] = jnp.where(lo, a0 / l0, a1 / l1).astype(BF16)
        lse_t = jnp.where(lo, m0 + jnp.log(l0), m1 + jnp.log(l1)).T
        lse_ref[0:1, :] = lse_t[0:1, :]
        lse_ref[1:2, :] = lse_t[HEAD_DIM:HEAD_DIM + 1, :]

    qblk = pl.BlockSpec((T, 128), lambda hp, qi: (qi, hp))
    seq = pl.BlockSpec((S, 128), lambda hp, qi: (0, hp))
    return pl.pallas_call(
        body, name="attn_fwd", grid=(ATT_HEADS // 2, nq),
        in_specs=[qblk, qblk, seq, seq, seq],
        out_specs=[qblk, pl.BlockSpec((None, 2, T), lambda hp, qi: (hp, 0, qi))],
        out_shape=[jax.ShapeDtypeStruct((S, ATT_W), BF16),
                   jax.ShapeDtypeStruct((ATT_HEADS // 2, 2, S), F32)],
        compiler_params=_params(2),
    )(q, qa, k, ka, v)


def _attn_bwd(q, qa, k, ka, v, do, lse, dlt):
    S = q.shape[0]
    T = min(1024, S)
    nq = S // T

    def body(q_ref, qa_ref, do_ref, lse_ref, dlt_ref, k_ref, ka_ref, v_ref, dq_ref, dk_ref, dv_ref, rows_ref, cols_ref):
        kj = pl.program_id(1)

        @pl.when(kj == 0)
        def _():
            dq_ref[...] = jnp.zeros_like(dq_ref)
            rows_ref[...] = jnp.zeros_like(rows_ref)

        lane = lax.broadcasted_iota(jnp.int32, (T, 128), 1)
        lo = lane < HEAD_DIM
        alo = lane < AUG_LANES
        triu = (lax.broadcasted_iota(jnp.int32, (T, T), 0) <= lax.broadcasted_iota(jnp.int32, (T, T), 1))
        kb = k_ref[...]
        kaug = jnp.concatenate([kb, ka_ref[...]], axis=1)
        vb = v_ref[...]

        def step(qi, carry, masked):
            off = pl.multiple_of(qi * T, T)
            qb = q_ref[pl.ds(off, T), :]
            dob = do_ref[pl.ds(off, T), :]
            qm = _pair_heads(lo, alo, qb, qa_ref[pl.ds(off, T), :])
            zero = jnp.zeros_like(qb)
            new, dqs = [], []
            for h in range(2):
                dk_a, dv_a, dc_a = carry[3 * h:3 * h + 3]
                dom = jnp.where(lo, dob, zero) if h == 0 else jnp.where(lo, zero, dob)
                s = _dot_nt(kaug, qm[h])
                if masked:
                    s = jnp.where(triu, s, -1e30)
                p = jnp.exp(s - lse_ref[h:h + 1, pl.ds(off, T)])
                dp = _dot_nt(vb, dom)
                ds = p * (dp - dlt_ref[h:h + 1, pl.ds(off, T)])
                pb = p.astype(BF16)
                dsb = ds.astype(BF16)
                dv_a = dv_a + _dot(pb, dob)
                dk_a = dk_a + _dot(dsb, qb)
                dc_a = dc_a + jnp.sum(ds, axis=1, keepdims=True)
                dqs.append(_dot_tn(dsb, kb))
                rows_ref[h:h + 1, pl.ds(off, T)] += jnp.sum(ds, axis=0, keepdims=True)
                new += [dk_a, dv_a, dc_a]
            dq_ref[pl.ds(off, T), :] += jnp.where(lo, dqs[0], dqs[1])
            return tuple(new)

        init = (jnp.zeros((T, 128), F32), jnp.zeros((T, 128), F32), jnp.zeros((T, 1), F32)) * 2
        carry = step(kj, init, True)
        carry = lax.fori_loop(kj + 1, nq, lambda qi, c: step(qi, c, False), carry)
        dk_ref[...] = jnp.where(lo, carry[0], carry[3])
        dv_ref[...] = jnp.where(lo, carry[1], carry[4])
        cols_ref[...] = -jnp.where(lo, carry[2], carry[5])

    seq = pl.BlockSpec((S, 128), lambda hp, kj: (0, hp))
    rows = pl.BlockSpec((None, 2, S), lambda hp, kj: (hp, 0, 0))
    kblk = pl.BlockSpec((T, 128), lambda hp, kj: (kj, hp))
    return pl.pallas_call(
        body, name="attn_bwd", grid=(ATT_HEADS // 2, nq),
        in_specs=[seq, seq, seq, rows, rows, kblk, kblk, kblk],
        out_specs=[seq, kblk, kblk, rows, kblk],
        out_shape=[jax.ShapeDtypeStruct((S, ATT_W), F32), jax.ShapeDtypeStruct((S, ATT_W), F32),
                   jax.ShapeDtypeStruct((S, ATT_W), F32),
                   jax.ShapeDtypeStruct((ATT_HEADS // 2, 2, S), F32),
                   jax.ShapeDtypeStruct((S, ATT_W), F32)],
        compiler_params=_params(2),
    )(q, qa, do, lse, dlt, k, ka, v)


def _wo_spec(wo4):
    return pl.BlockSpec(wo4.shape, lambda i: (0, 0, 0))


def _wo_halves(wo_ref):
    half = N_CHIPS // 2
    return (wo_ref[0:half].reshape(ATT_W, D_MODEL), wo_ref[half:N_CHIPS].reshape(CONV_CH, D_MODEL))


def _mixer_out_fwd(x, att, h3, wo4, g2):
    S = x.shape[0]
    TM = min(512, S)

    def body(x_ref, att_ref, h3_ref, wo_ref, g2_ref, x2_ref, u2_ref):
        wa, wc = _wo_halves(wo_ref)
        x2 = x_ref[...] + _dot(att_ref[...], wa) + _dot(h3_ref[...], wc)
        x2_ref[...] = x2
        r = lax.rsqrt(jnp.mean(x2 * x2, axis=-1, keepdims=True) + EPS)
        u2_ref[...] = (x2 * r * g2_ref[...]).astype(BF16)

    row = lambda w: pl.BlockSpec((TM, w), lambda i: (i, 0))
    full = lambda a: pl.BlockSpec(a.shape, lambda i: (0,) * a.ndim)
    return pl.pallas_call(
        body, name="mixer_out_fwd", grid=(S // TM,),
        in_specs=[row(D_MODEL), row(ATT_W), row(CONV_CH), _wo_spec(wo4), full(g2)],
        out_specs=[row(D_MODEL), row(D_MODEL)],
        out_shape=[jax.ShapeDtypeStruct((S, D_MODEL), F32), jax.ShapeDtypeStruct((S, D_MODEL), BF16)],
        compiler_params=_params(1),
    )(x, att, h3, wo4, g2)


def _mlp_w_specs():
    return [pl.BlockSpec((None, D_MODEL, D_FF // N_CHIPS), lambda i, f: (f, 0, 0)),
            pl.BlockSpec((None, D_FF // N_CHIPS, D_MODEL), lambda i, f: (f, 0, 0))]


def _mlp_fwd(x2, u2, w1, w2, target=None):
    S = x2.shape[0]
    head = target is not None
    TM = min(1024, S)
    TF = 1024
    nf = D_FF // TF

    def body(*refs):
        x2_ref, u2_ref, w1_ref, w2_ref = refs[:4]
        x3_ref, z_ref, hh_ref = refs[4 + head:7 + head]
        i = pl.program_id(0)
        f = pl.program_id(1)

        @pl.when(f == 0)
        def _():
            x3_ref[...] = x2_ref[...]

        z = _dot(u2_ref[...], w1_ref[...])
        z_ref[...] = z
        zr = jnp.maximum(z, 0.0)
        hh = (zr * zr).astype(BF16)
        hh_ref[...] = hh
        x3_ref[...] += _dot(hh, w2_ref[...])

        if head:
            t_ref, loss_ref = refs[4], refs[8]

            @pl.when((i == 0) & (f == 0))
            def _():
                loss_ref[...] = jnp.zeros_like(loss_ref)

            @pl.when(f == nf - 1)
            def _():
                d = x3_ref[...] - t_ref[...]
                x3_ref[...] = d * (1.0 / D_MODEL)
                loss_ref[...] += jnp.sum(d * d)

    rows = pl.BlockSpec((TM, D_MODEL), lambda i, f: (i, 0))
    once = pl.BlockSpec((TM, D_MODEL), lambda i, f: (i, 0), pipeline_mode=pl.Buffered(1))
    tile = pl.BlockSpec((TM, TF), lambda i, f: (i, f))
    return pl.pallas_call(
        body, name="mlp_fwd_loss" if head else "mlp_fwd", grid=(S // TM, nf),
        in_specs=[once if head else rows, rows] + _mlp_w_specs() + [once] * head,
        out_specs=[rows, tile, tile] + [pl.BlockSpec((8, 128), lambda i, f: (0, 0))] * head,
        out_shape=[jax.ShapeDtypeStruct((S, D_MODEL), F32), jax.ShapeDtypeStruct((S, D_FF), F32),
                   jax.ShapeDtypeStruct((S, D_FF), BF16)] + [jax.ShapeDtypeStruct((8, 128), F32)] * head,
        compiler_params=_params(2),
    )(x2, u2, w1, w2, *([target] if head else []))


def _mlp_bwd(dx3, z, x2, g2, w1, w2):
    S = dx3.shape[0]
    TM = min(1024, S)
    TF = 1024
    nf = D_FF // TF

    def body(dx3_ref, z_ref, x2_ref, g2_ref, w1_ref, w2_ref, dz_ref, dx2_ref, dg2_ref, du2_ref):
        i = pl.program_id(0)
        f = pl.program_id(1)

        @pl.when((i == 0) & (f == 0))
        def _():
            dg2_ref[...] = jnp.zeros_like(dg2_ref)

        @pl.when(f == 0)
        def _():
            du2_ref[...] = jnp.zeros_like(du2_ref)

        dhh = _dot_nt(dx3_ref[...].astype(BF16), w2_ref[...])
        dz = (dhh * (2.0 * jnp.maximum(z_ref[...], 0.0))).astype(BF16)
        dz_ref[...] = dz
        du2_ref[...] += _dot_nt(dz, w1_ref[...])

        @pl.when(f == nf - 1)
        def _():
            x2 = x2_ref[...]
            r = lax.rsqrt(jnp.mean(x2 * x2, axis=-1, keepdims=True) + EPS)
            n = x2 * r
            du2 = du2_ref[...]
            t = du2 * g2_ref[...]
            dx2_ref[...] = dx3_ref[...] + r * (t - n * jnp.mean(t * n, axis=-1, keepdims=True))
            dg2_ref[0:1, :] += jnp.sum(du2 * n, axis=0, keepdims=True)

    rowi = pl.BlockSpec((TM, D_MODEL), lambda i, f: (i, 0))
    held = pl.BlockSpec((TM, D_MODEL), lambda i, f: (i, 0), pipeline_mode=pl.Buffered(1))
    return pl.pallas_call(
        body, name="mlp_bwd", grid=(S // TM, nf),
        in_specs=[held, pl.BlockSpec((TM, TF), lambda i, f: (i, f)), held,
                  pl.BlockSpec((1, D_MODEL), lambda i, f: (0, 0))] + _mlp_w_specs(),
        out_specs=[pl.BlockSpec((TM, TF), lambda i, f: (i, f)), rowi, pl.BlockSpec((8, D_MODEL), lambda i, f: (0, 0))],
        out_shape=[jax.ShapeDtypeStruct((S, D_FF), BF16), jax.ShapeDtypeStruct((S, D_MODEL), F32),
                   jax.ShapeDtypeStruct((8, D_MODEL), F32)],
        scratch_shapes=[pltpu.VMEM((TM, D_MODEL), F32)],
        compiler_params=_params(2),
    )(dx3, z, x2, g2, w1, w2)


def _matmul_tn(a, b, col_shards=1):
    S, I = a.shape
    J = b.shape[1]
    TI = min(I, 1024)
    TJ = 1024 if J % 1024 == 0 else 896
    TS = min(S, 1024)
    nk = S // TS
    per = J // col_shards // TJ

    def body(a_ref, b_ref, o_ref, acc_ref):
        k = pl.program_id(2)

        @pl.when(k == 0)
        def _():
            acc_ref[...] = jnp.zeros_like(acc_ref)

        acc_ref[...] += _dot_tn(a_ref[...].astype(BF16), b_ref[...].astype(BF16))

        @pl.when(k == nk - 1)
        def _():
            o_ref[...] = acc_ref[...].astype(BF16)

    return pl.pallas_call(
        body, name="matmul_tn", grid=(I // TI, J // TJ, nk),
        in_specs=[pl.BlockSpec((TS, TI), lambda i, j, k: (k, i)), pl.BlockSpec((TS, TJ), lambda i, j, k: (k, j))],
        out_specs=pl.BlockSpec((None, TI, TJ), lambda i, j, k: (j // per, i, j % per)),
        out_shape=jax.ShapeDtypeStruct((col_shards, I, J // col_shards), BF16),
        scratch_shapes=[pltpu.VMEM((TI, TJ), F32)],
        compiler_params=_params(3),
    )(a, b)


def _mixer_out_bwd(dx2, wo4, att, h1, lng, lnb):
    S = dx2.shape[0]
    TM = min(512, S)

    def body(dx2_ref, wo_ref, att_ref, h1_ref, lng_ref, lnb_ref, hr_ref, datt_ref, dlt_ref, dh1_ref, sm_ref):
        @pl.when(pl.program_id(0) == 0)
        def _():
            sm_ref[...] = jnp.zeros_like(sm_ref)

        dxb = dx2_ref[...].astype(BF16)
        wa, wc = _wo_halves(wo_ref)
        datt = _dot_nt(dxb, wa)
        datt_ref[...] = datt.astype(BF16)
        prod = datt * att_ref[...].astype(F32)
        dlt_ref[...] = sum(_dot_nt(hr_ref[...], piece) for piece in _split3(prod))[0:ATT_HEADS, :]
        dh3 = _dot_nt(dxb, wc)
        h1 = h1_ref[...]
        mu = jnp.mean(h1, axis=-1, keepdims=True)
        d = h1 - mu
        rstd = lax.rsqrt(jnp.mean(d * d, axis=-1, keepdims=True) + EPS)
        n = d * rstd
        h2 = n * lng_ref[...] + lnb_ref[...]
        sg = _sigmoid(h2)
        dh2 = dh3 * (sg * (1.0 + h2 * (1.0 - sg)))
        dn = dh2 * lng_ref[...]
        dh1 = rstd * (dn - jnp.mean(dn, axis=-1, keepdims=True) - n * jnp.mean(dn * n, axis=-1, keepdims=True))
        dh1_ref[...] = dh1
        sm_ref[0:1, :] += jnp.sum(dh2 * n, axis=0, keepdims=True)
        sm_ref[1:2, :] += jnp.sum(dh2, axis=0, keepdims=True)
        sm_ref[2:3, :] += jnp.sum(dh1, axis=0, keepdims=True)

    row = lambda w: pl.BlockSpec((TM, w), lambda i: (i, 0))
    full = lambda a: pl.BlockSpec(a.shape, lambda i: (0,) * a.ndim)
    hr = _head_rows()
    return pl.pallas_call(
        body, name="mixer_out_bwd", grid=(S // TM,),
        in_specs=[row(D_MODEL), _wo_spec(wo4), row(ATT_W), row(CONV_CH), full(lng), full(lnb), full(hr)],
        out_specs=[row(ATT_W), pl.BlockSpec((ATT_HEADS, TM), lambda i: (0, i)), row(CONV_CH),
                   pl.BlockSpec((8, CONV_CH), lambda i: (0, 0))],
        out_shape=[jax.ShapeDtypeStruct((S, ATT_W), BF16), jax.ShapeDtypeStruct((ATT_HEADS, S), F32),
                   jax.ShapeDtypeStruct((S, CONV_CH), F32), jax.ShapeDtypeStruct((8, CONV_CH), F32)],
        compiler_params=_params(1),
    )(dx2, wo4, att, h1, lng, lnb, hr)


def _conv_glu_bwd(dh1, h0, proj, cw):
    S = dh1.shape[0]
    TM = min(512, S)
    nb = S // TM
    lead = CONV_HALO - CONV_TAPS + 1

    def body(dh1_ref, dnx_ref, h0_ref, hpv_ref, a_ref, g_ref, cw_ref, dag_ref, dcw_ref,
             dbuf_ref, hbuf_ref, ds_ref, hs_ref, dh0_ref, dcw8_ref):
        i = pl.program_id(0)

        @pl.when(i == 0)
        def _():
            dcw8_ref[...] = jnp.zeros_like(dcw8_ref)

        dbuf_ref[0:TM, :] = dh1_ref[...]
        dbuf_ref[TM:TM + CONV_HALO, :] = jnp.where(i < nb - 1, dnx_ref[0:CONV_HALO, :], 0.0)
        hbuf_ref[0:CONV_HALO, :] = jnp.where(i > 0, hpv_ref[TM - CONV_HALO:TM, :], 0.0)
        hbuf_ref[CONV_HALO:CONV_HALO + TM, :] = h0_ref[...]
        _fill_row_shifts(dbuf_ref, ds_ref, TM)
        _fill_row_shifts(hbuf_ref, hs_ref, TM)

        def conv_rows(step, _):
            r0 = pl.multiple_of(step * CONV_ROWS, CONV_ROWS)
            dh1 = dbuf_ref[pl.ds(r0, CONV_ROWS), :]
            part = jnp.zeros((CONV_ROWS, CONV_CH), F32)
            for j in range(CONV_TAPS):
                part = part + cw_ref[j:j + 1, :] * _row_shifted(dbuf_ref, ds_ref, CONV_TAPS - 1 - j, CONV_ROWS, r0)
                prod = dh1 * _row_shifted(hbuf_ref, hs_ref, lead + j, CONV_ROWS, r0)
                dcw8_ref[j] += jnp.sum(prod.reshape(CONV_ROWS // SUBLANES, SUBLANES, CONV_CH), axis=0)
            dh0_ref[pl.ds(r0, CONV_ROWS), :] = part
            return 0

        lax.fori_loop(0, TM // CONV_ROWS, conv_rows, 0)

        @pl.when(i == nb - 1)
        def _():
            dcw_ref[...] = jnp.sum(dcw8_ref[...], axis=1)

        dh0 = dh0_ref[...]
        sg = _sigmoid(g_ref[...])
        dag_ref[:, 0:CONV_CH] = (dh0 * sg).astype(BF16)
        dag_ref[:, CONV_CH:2 * CONV_CH] = (dh0 * a_ref[...] * sg * (1.0 - sg)).astype(BF16)

    blk = lambda fn: pl.BlockSpec((TM, CONV_CH), fn)
    return pl.pallas_call(
        body, name="conv_glu_bwd", grid=(nb,),
        in_specs=[blk(lambda i: (i, 0)), blk(lambda i: (jnp.minimum(i + 1, nb - 1), 0)),
                  blk(lambda i: (i, 0)), blk(lambda i: (jnp.maximum(i - 1, 0), 0)),
                  blk(lambda i: (i, O_A // CONV_CH)), blk(lambda i: (i, O_G // CONV_CH)),
                  pl.BlockSpec(cw.shape, lambda i: (0, 0))],
        out_specs=[pl.BlockSpec((TM, 2 * CONV_CH), lambda i: (i, 0)), pl.BlockSpec((CONV_HALO, CONV_CH), lambda i: (0, 0))],
        out_shape=[jax.ShapeDtypeStruct((S, 2 * CONV_CH), BF16), jax.ShapeDtypeStruct((CONV_HALO, CONV_CH), F32)],
        scratch_shapes=[pltpu.VMEM((TM + CONV_HALO, CONV_CH), F32), pltpu.VMEM((TM + CONV_HALO, CONV_CH), F32),
                        pltpu.VMEM((SUBLANES - 1, TM + CONV_HALO, CONV_CH), F32),
                        pltpu.VMEM((SUBLANES - 1, TM + CONV_HALO, CONV_CH), F32),
                        pltpu.VMEM((TM, CONV_CH), F32), pltpu.VMEM((CONV_HALO, SUBLANES, CONV_CH), F32)],
        compiler_params=_params(1),
    )(dh1, dh1, h0, h0, proj, proj, cw)


def _mixer_in_bwd(x, dx2, proj, dq, dk, dv, dag, dct, drb, g1, win, qg, kg, bf):
    S = x.shape[0]
    TM = min(512, S)
    nb = S // TM

    def body(x_ref, dx2_ref, qr_ref, kr_ref, fz_ref, dq_ref, dk_ref, dv_ref, dag_ref, dct_ref, drb_ref,
             g1_ref, win_ref, qg_ref, kg_ref, bf_ref, bd_ref, fold_ref, triu_ref, pick_ref,
             dproj_ref, dx_ref, dg1_ref, sm_ref, carry_ref, gsum_ref):
        i = pl.program_id(0)

        @pl.when(i == 0)
        def _():
            carry_ref[...] = jnp.zeros_like(carry_ref)
            gsum_ref[...] = jnp.zeros_like(gsum_ref)
            dg1_ref[...] = jnp.zeros_like(dg1_ref)
            sm_ref[...] = jnp.zeros_like(sm_ref)

        def headnorm_bwd(raw, dy, gain, scale, row):
            rs = lax.rsqrt(_head_sums(raw * raw, bd_ref[...]) * (1.0 / HEAD_DIM) + EPS)
            n = raw * rs
            gsum_ref[row:row + 1, :] += jnp.sum(dy * n, axis=0, keepdims=True) * scale
            dn = dy * (gain * scale)
            return rs * (dn - n * (_head_sums(dn * n, bd_ref[...]) * (1.0 / HEAD_DIM)))

        dproj_ref[:, O_Q:O_Q + ATT_W] = headnorm_bwd(qr_ref[...], dq_ref[...], qg_ref[...], QK_SCALE, 0).astype(BF16)
        dproj_ref[:, O_K:O_K + ATT_W] = headnorm_bwd(kr_ref[...], dk_ref[...], kg_ref[...], 1.0, 1).astype(BF16)
        dproj_ref[:, O_V:O_V + ATT_W] = dv_ref[...].astype(BF16)
        dproj_ref[:, O_A:O_A + 2 * CONV_CH] = dag_ref[...]

        dc8 = jnp.concatenate([dct_ref[...], jnp.zeros((128 - ATT_HEADS, TM), F32)], axis=0).T
        dc8 = dc8 + _dot_hi_r(drb_ref[...], pick_ref[...])
        dlogf = _dot_hi_l(triu_ref[...], dc8) + carry_ref[...]
        carry_ref[...] = dlogf[0:1, :]
        df = dlogf * _sigmoid(-(fz_ref[...] + bf_ref[...]))
        dproj_ref[:, O_F:O_F + 128] = df.astype(BF16)
        sm_ref[2:3, :] += jnp.sum(df, axis=0, keepdims=True)

        du1 = _dot_nt(dproj_ref[...], win_ref[...])
        xv = x_ref[...]
        r = lax.rsqrt(jnp.mean(xv * xv, axis=-1, keepdims=True) + EPS)
        n1 = xv * r
        t = du1 * g1_ref[...]
        dx_ref[...] = dx2_ref[...] + r * (t - n1 * jnp.mean(t * n1, axis=-1, keepdims=True))
        dg1_ref[0:1, :] += jnp.sum(du1 * n1, axis=0, keepdims=True)

        @pl.when(i == nb - 1)
        def _():
            sm_ref[0:2, :] = _dot_hi_r(gsum_ref[0:8, :], fold_ref[...])[0:2, :]

    rev = lambda w, cb=0: pl.BlockSpec((TM, w), lambda i: (nb - 1 - i, cb))
    full = lambda a: pl.BlockSpec(a.shape, lambda i: (0,) * a.ndim)
    bd, fold, triu = _head_blockdiag(), _head_fold(), _tril(TM).T
    consts = (g1, win, qg, kg, bf, bd, fold, triu, _head_pick())
    return pl.pallas_call(
        body, name="mixer_in_bwd", grid=(nb,),
        in_specs=[rev(D_MODEL), rev(D_MODEL), rev(ATT_W, O_Q // ATT_W), rev(ATT_W, O_K // ATT_W), rev(128, O_F // 128),
                  rev(ATT_W), rev(ATT_W), rev(ATT_W), rev(2 * CONV_CH),
                  pl.BlockSpec((ATT_HEADS, TM), lambda i: (0, nb - 1 - i)), rev(ATT_W)] + [full(a) for a in consts],
        out_specs=[rev(N_INP), rev(D_MODEL), pl.BlockSpec((8, D_MODEL), lambda i: (0, 0)),
                   pl.BlockSpec((8, 128), lambda i: (0, 0))],
        out_shape=[jax.ShapeDtypeStruct((S, N_INP), BF16), jax.ShapeDtypeStruct((S, D_MODEL), F32),
                   jax.ShapeDtypeStruct((8, D_MODEL), F32), jax.ShapeDtypeStruct((8, 128), F32)],
        scratch_shapes=[pltpu.VMEM((1, 128), F32), pltpu.VMEM((8, ATT_W), F32)],
        compiler_params=_params(1),
    )(x, dx2, proj, proj, proj, dq, dk, dv, dag, dct, drb, *consts)


def _layer_fwd(x, early, late, target=None):
    p = early(x)
    u1, proj, q, k, v, qa, ka, h0, h1, h3 = _mixer_in_fwd(
        x, p["g1"], p["win"], p["qg"], p["kg"], p["bf"], p["cw"], p["cvb"], p["lng"], p["lnb"])
    att, lse = _attn_fwd(q, qa, k, ka, v)
    p = dict(p, **late(att))
    x2, u2 = _mixer_out_fwd(x, att, h3, p["wo"], p["g2"])
    x3, z, hh, *loss_acc = _mlp_fwd(x2, u2, p["w1"], p["w2"], target)
    saved = dict(x=x, u1=u1, proj=proj, q=q, k=k, v=v, qa=qa, ka=ka, h0=h0, h1=h1, h3=h3, att=att, lse=lse,
                 x2=x2, u2=u2, z=z, hh=hh)
    return (x3 if target is None else (x3, loss_acc[0])), saved, p


def _tie(a, token):
    return a if token is None else a + token[0:1, 0:1].astype(a.dtype).reshape((1,) * a.ndim)


def _layer_bwd(dx3, s, p, reduce):
    dz, dx2, dg2 = _mlp_bwd(dx3, s["z"], s["x2"], p["g2"], p["w1"], p["w2"])
    g_w2 = _matmul_tn(s["hh"], dx3)
    g_w1 = _matmul_tn(s["u2"], dz, col_shards=N_CHIPS)
    token = reduce("a", {2: g_w1, 3: g_w2.reshape(N_CHIPS, D_FF // N_CHIPS, D_MODEL)})
    datt, dlt, dh1, sm_c = _mixer_out_bwd(dx2, p["wo"], s["att"], s["h1"], _tie(p["lng"], token), p["lnb"])
    g_wo = jnp.concatenate([_matmul_tn(s["att"], dx2)[0], _matmul_tn(s["h3"], dx2)[0]], axis=0)
    dag, dcw = _conv_glu_bwd(dh1, s["h0"], s["proj"], p["cw"])
    dq, dk, dv, dc4, drb = _attn_bwd(s["q"], s["qa"], s["k"], s["ka"], s["v"], datt, s["lse"],
                                     dlt.reshape(ATT_HEADS // 2, 2, dlt.shape[1]))
    dct = dc4.reshape(ATT_HEADS, dc4.shape[2])
    dproj, dx, dg1, sm_a = _mixer_in_bwd(s["x"], dx2, s["proj"], dq, dk, dv, dag, dct, drb,
                                         p["g1"], p["win"], p["qg"], p["kg"], p["bf"])
    g_win = _win_to_global(_matmul_tn(s["u1"], dproj)[0])
    g_win = g_win.reshape(D_MODEL, N_CHIPS, N_IN // N_CHIPS).transpose(1, 0, 2)
    token = reduce("b", {0: g_win, 1: g_wo.reshape(N_CHIPS, D_MODEL // N_CHIPS, D_MODEL)})
    small = dict(g1=dg1[0], g2=dg2[0], lng=sm_c[0], lnb=sm_c[1], cvb=sm_c[2], cw=dcw[0:CONV_TAPS],
                 qg=sm_a[0, 0:HEAD_DIM], kg=sm_a[1, 0:HEAD_DIM], bf=sm_a[2, 0:ATT_HEADS])
    return dx, small, token


def _local_step(x, target, weights, reduce):
    saved, layers = [], []
    h = x
    for l, (early, late) in enumerate(weights):
        h, s, p = _layer_fwd(h, early, late, target if l == len(weights) - 1 else None)
        saved.append(s)
        layers.append(p)
    dy, loss_acc = h
    loss = loss_acc[0, 0] * (0.5 / D_MODEL)
    smalls = []
    d, token = dy, None
    for l in reversed(range(len(layers))):
        d, small, token = _layer_bwd(d, saved[l], dict(layers[l], g2=_tie(layers[l]["g2"], token)), reduce(l))
        smalls.append(small)
    return loss, d, smalls[::-1]


def _win_to_internal(w):
    pad = jnp.zeros(w.shape[:-1] + (N_INP - N_IN,), w.dtype)
    return jnp.concatenate([w[..., :1536], w[..., 1544:], w[..., 1536:1544], pad], axis=-1)


def _win_to_global(g):
    return jnp.concatenate([g[..., :1536], g[..., O_F:O_F + ATT_HEADS], g[..., 1536:O_F]], axis=-1)


def _layer_params(l, win, cw, norm1_g, b_f, q_norm_g, k_norm_g, conv_b, conv_ln_g, conv_ln_b, norm2_g):
    row = lambda a: a.reshape(1, -1)
    return dict(
        win=win, cw=jnp.pad(cw, ((0, CONV_HALO - CONV_TAPS), (0, 0))),
        g1=row(norm1_g[l]), g2=row(norm2_g[l]),
        qg=row(jnp.tile(q_norm_g[l], ATT_HEADS)), kg=row(jnp.tile(k_norm_g[l], ATT_HEADS)),
        bf=row(jnp.pad(b_f[l], (0, 128 - ATT_HEADS))),
        cvb=row(conv_b[l]), lng=row(conv_ln_g[l]), lnb=row(conv_ln_b[l]))


def _place():
    x, y, c = lax.axis_index("x"), lax.axis_index("y"), lax.axis_index("c")
    chips = [(1 - x, y), (x, 1 - y), (1 - x, 1 - y)]
    return x, y, c, chips


HBM = pl.BlockSpec(memory_space=pltpu.HBM)
SEM = pl.BlockSpec(memory_space=pltpu.SEMAPHORE)
GATHER_PEERS = N_CHIPS


def _gather_peers():
    x, y, c, chips = _place()
    return [(*chip, c) for chip in chips] + [(x, y, 1 - c)], [2 * px + py for px, py in chips] + [2 * x + y]


def _gather_start(srcs):
    n = len(srcs)

    def body(*refs):
        ins, lands = refs[:n], refs[n:2 * n]
        send_sems, recv_sems, token = refs[2 * n], refs[2 * n + 1], refs[-1]
        me = 2 * lax.axis_index("x") + lax.axis_index("y")
        peers, _ = _gather_peers()
        for g in range(n):
            for j, to in enumerate(peers):
                pltpu.make_async_remote_copy(src_ref=ins[g], dst_ref=lands[g].at[me],
                                             send_sem=send_sems.at[GATHER_PEERS * g + j],
                                             recv_sem=recv_sems.at[GATHER_PEERS * g + j],
                                             device_id=to, device_id_type=MESH).start()
        token[...] = jnp.zeros_like(token)

    lands = [lax.empty((N_CHIPS,) + a.shape, a.dtype) for a in srcs]
    outs = pl.pallas_call(
        body, name="gather_start",
        in_specs=[HBM] * (2 * n),
        out_specs=[SEM, SEM] + [HBM] * (2 * n) + [pl.BlockSpec(memory_space=pltpu.VMEM)],
        out_shape=[pltpu.SemaphoreType.DMA((GATHER_PEERS * n,)), pltpu.SemaphoreType.DMA((GATHER_PEERS * n,))]
        + [pltpu.HBM(a.shape, a.dtype) for a in srcs] + [pltpu.HBM(a.shape, a.dtype) for a in lands]
        + [jax.ShapeDtypeStruct((8, 128), F32)],
        input_output_aliases={i: 2 + i for i in range(2 * n)},
        compiler_params=pltpu.CompilerParams(has_side_effects=pltpu.SideEffectType.DATAFLOW_SIDE_EFFECTING),
    )(*[pltpu.with_memory_space_constraint(a, pltpu.HBM) for a in srcs],
      *[pltpu.with_memory_space_constraint(a, pltpu.HBM) for a in lands])
    return outs[0], outs[1], list(outs[2:2 + n]), list(outs[2 + n:2 + 2 * n]), outs[-1]


def _gather_wait(name, groups, send_sems, recv_sems, srcs, lands, after):
    k = len(groups)

    def body(*refs):
        ins, lnd = refs[:k], refs[k:2 * k]
        ssem, rsem = refs[2 * k], refs[2 * k + 1]
        peers, slots = _gather_peers()
        for i, g in enumerate(groups):
            for j, to in enumerate(peers):
                cp = pltpu.make_async_remote_copy(src_ref=ins[i], dst_ref=lnd[i].at[slots[j]],
                                                  send_sem=ssem.at[GATHER_PEERS * g + j],
                                                  recv_sem=rsem.at[GATHER_PEERS * g + j],
                                                  device_id=to, device_id_type=MESH)
                cp.wait_send()
                cp.wait_recv()

    outs = pl.pallas_call(
        body, name=name,
        in_specs=[HBM] * (2 * k) + [SEM, SEM, ANY],
        out_specs=[HBM] * (2 * k),
        out_shape=[pltpu.HBM(a.shape, a.dtype) for a in srcs] + [pltpu.HBM(a.shape, a.dtype) for a in lands],
        input_output_aliases={i: i for i in range(2 * k)},
        compiler_params=pltpu.CompilerParams(has_side_effects=pltpu.SideEffectType.DATAFLOW_SIDE_EFFECTING),
    )(*srcs, *lands, send_sems, recv_sems, after)
    return list(outs[k:])


REDUCE_STEPS = 8


def _row_half(a):
    return a.shape[-2] // 2


def _half_swap(name, gs):
    n = len(gs)

    def body(*refs):
        ins, outs = refs[:n], refs[n:2 * n]
        send_sems, recv_sems = refs[2 * n:]
        x, y, c, _ = _place()
        copies = []
        for a in range(n):
            h = _row_half(gs[a])
            copies.append(pltpu.make_async_remote_copy(
                src_ref=ins[a].at[:, pl.ds((1 - c) * h, h), :], dst_ref=outs[a], send_sem=send_sems.at[a],
                recv_sem=recv_sems.at[a], device_id=(x, y, 1 - c), device_id_type=MESH))
        for cp in copies:
            cp.start()
        for cp in copies:
            cp.wait()

    return pl.pallas_call(
        body, name=name,
        in_specs=[ANY] * n, out_specs=[ANY] * n,
        out_shape=[jax.ShapeDtypeStruct((N_CHIPS, _row_half(a), a.shape[-1]), a.dtype) for a in gs],
        scratch_shapes=[pltpu.SemaphoreType.DMA((n,)), pltpu.SemaphoreType.DMA((n,))],
        compiler_params=pltpu.CompilerParams(has_side_effects=True),
    )(*gs)


def _half_specs(gs, row_block):
    tiles = [_row_half(a) // REDUCE_STEPS for a in gs]
    return [pl.BlockSpec((N_CHIPS, t, a.shape[-1]), lambda i, p: (0, row_block(i, p), 0)) for a, t in zip(gs, tiles)]


def _half_add(name, gs, got, place):
    n = len(gs)

    def body(place_ref, *refs):
        own, theirs, outs = refs[:n], refs[n:2 * n], refs[2 * n:]
        for a in range(n):
            outs[a][...] = (own[a][...].astype(F32) + theirs[a][...].astype(F32)).astype(BF16)

    plain = _half_specs(gs, lambda i, p: i)
    return pl.pallas_call(
        body, name=name,
        grid_spec=pltpu.PrefetchScalarGridSpec(
            num_scalar_prefetch=1, grid=(REDUCE_STEPS,),
            in_specs=_half_specs(gs, lambda i, p: p[1] * REDUCE_STEPS + i) + plain, out_specs=plain),
        out_shape=[jax.ShapeDtypeStruct(a.shape, BF16) for a in got],
        compiler_params=_params(1),
    )(place, *gs, *got)


def _exchange_copies(parts, lands, send_sems, recv_sems):
    x, y, c, chips = _place()
    return [pltpu.make_async_remote_copy(src_ref=parts[a].at[2 * px + py], dst_ref=lands[a].at[j],
                                         send_sem=send_sems.at[3 * a + j], recv_sem=recv_sems.at[3 * a + j],
                                         device_id=(px, py, c), device_id_type=MESH)
            for a in range(len(parts)) for j, (px, py) in enumerate(chips)]


def _exchange_start(name, parts):
    n = len(parts)

    def body(*refs):
        _ = [cp.start() for cp in _exchange_copies(refs[:n], refs[n:2 * n], refs[2 * n], refs[2 * n + 1])]
        refs[-1][...] = jnp.zeros_like(refs[-1])

    lands = [lax.empty((N_CHIPS - 1,) + a.shape[1:], a.dtype) for a in parts]
    outs = pl.pallas_call(
        body, name=name,
        in_specs=[HBM] * (2 * n),
        out_specs=[SEM, SEM] + [HBM] * (2 * n) + [pl.BlockSpec(memory_space=pltpu.VMEM)],
        out_shape=[pltpu.SemaphoreType.DMA((3 * n,)), pltpu.SemaphoreType.DMA((3 * n,))]
        + [pltpu.HBM(a.shape, a.dtype) for a in parts] + [pltpu.HBM(a.shape, a.dtype) for a in lands]
        + [jax.ShapeDtypeStruct((8, 128), F32)],
        input_output_aliases={i: 2 + i for i in range(2 * n)},
        compiler_params=pltpu.CompilerParams(has_side_effects=pltpu.SideEffectType.DATAFLOW_SIDE_EFFECTING),
    )(*[pltpu.with_memory_space_constraint(a, pltpu.HBM) for a in parts],
      *[pltpu.with_memory_space_constraint(a, pltpu.HBM) for a in lands])
    return outs[0], outs[1], list(outs[2:2 + n]), list(outs[2 + n:2 + 2 * n]), outs[-1]


def _exchange_wait(name, send_sems, recv_sems, parts, lands, after):
    n = len(parts)

    def body(*refs):
        for cp in _exchange_copies(refs[:n], refs[n:2 * n], refs[2 * n], refs[2 * n + 1]):
            cp.wait_send()
            cp.wait_recv()

    outs = pl.pallas_call(
        body, name=name,
        in_specs=[HBM] * (2 * n) + [SEM, SEM, ANY],
        out_specs=[HBM] * (2 * n),
        out_shape=[pltpu.HBM(a.shape, a.dtype) for a in parts] + [pltpu.HBM(a.shape, a.dtype) for a in lands],
        input_output_aliases={i: i for i in range(2 * n)},
        compiler_params=pltpu.CompilerParams(has_side_effects=pltpu.SideEffectType.DATAFLOW_SIDE_EFFECTING),
    )(*parts, *lands, send_sems, recv_sems, after)
    return list(outs[:n]), list(outs[n:])


def _chip_sum(name, parts, lands, sums, place, layer):
    n = len(parts)
    tiles = [a.shape[-2] // REDUCE_STEPS for a in parts]

    def body(place_ref, *refs):
        own, got, outs = refs[:n], refs[n:2 * n], refs[3 * n:]
        for a in range(n):
            tot = own[a][...].astype(F32)
            for j in range(N_CHIPS - 1):
                tot = tot + got[a][j].astype(F32)
            outs[a][...] = tot

    own_specs = [pl.BlockSpec((None, t, a.shape[-1]), lambda i, p: (p[0], i, 0)) for a, t in zip(parts, tiles)]
    got_specs = [pl.BlockSpec((N_CHIPS - 1, t, a.shape[-1]), lambda i, p: (0, i, 0)) for a, t in zip(parts, tiles)]
    out_specs = [pl.BlockSpec((None, t, a.shape[-1]), lambda i, p: (layer, p[1] * REDUCE_STEPS + i, 0))
                 for a, t in zip(parts, tiles)]
    return pl.pallas_call(
        body, name=name,
        grid_spec=pltpu.PrefetchScalarGridSpec(num_scalar_prefetch=1, grid=(REDUCE_STEPS,),
                                               in_specs=own_specs + got_specs + [ANY] * n, out_specs=out_specs),
        out_shape=[jax.ShapeDtypeStruct(a.shape, F32) for a in sums],
        input_output_aliases={1 + 2 * n + a: a for a in range(n)},
        compiler_params=_params(1),
    )(place, *parts, *lands, *sums)


def _half_fill(name, sums, layer):
    n = len(sums)

    def body(*refs):
        ins, outs = refs[:n], refs[n:2 * n]
        send_sems, recv_sems = refs[2 * n:]
        x, y, c, _ = _place()
        copies = []
        for a in range(n):
            h = _row_half(sums[a])
            copies.append(pltpu.make_async_remote_copy(
                src_ref=ins[a].at[layer, pl.ds(c * h, h), :], dst_ref=outs[a].at[layer, pl.ds(c * h, h), :],
                send_sem=send_sems.at[a], recv_sem=recv_sems.at[a], device_id=(x, y, 1 - c), device_id_type=MESH))
        for cp in copies:
            cp.start()
        for a in range(n):
            h = _row_half(sums[a])
            theirs = outs[a].at[layer, pl.ds((1 - c) * h, h), :]
            pltpu.make_async_remote_copy(src_ref=theirs, dst_ref=theirs, send_sem=send_sems.at[a], recv_sem=recv_sems.at[a],
                                         device_id=(x, y, 1 - c), device_id_type=MESH).wait_recv()
        for cp in copies:
            cp.wait_send()

    return pl.pallas_call(
        body, name=name,
        in_specs=[ANY] * n, out_specs=[ANY] * n,
        out_shape=[jax.ShapeDtypeStruct(a.shape, a.dtype) for a in sums],
        input_output_aliases={a: a for a in range(n)},
        scratch_shapes=[pltpu.SemaphoreType.DMA((n,)), pltpu.SemaphoreType.DMA((n,))],
        compiler_params=pltpu.CompilerParams(has_side_effects=True),
    )(*sums)


def _adamw_math(w, g, m, v):
    m = ADAM_B1 * m + (1.0 - ADAM_B1) * g
    v = ADAM_B2 * v + (1.0 - ADAM_B2) * (g * g)
    m_hat = m / (1.0 - ADAM_B1 ** ADAM_STEP)
    v_hat = v / (1.0 - ADAM_B2 ** ADAM_STEP)
    delta = -ADAM_LR * (m_hat / (jnp.sqrt(v_hat) + ADAM_EPS) + ADAM_WD * w)
    return delta, m, v


def _adamw(name, ws, gs, ms, vs, layer, prev=None):
    n = len(ws)
    steps = 8
    tiles = [a.shape[-2] // steps for a in ws]

    def body(*refs):
        w_r, g_r, m_r, v_r = refs[:n], refs[n:2 * n], refs[2 * n:3 * n], refs[3 * n:4 * n]
        g_o, d_o, m_o, v_o = (refs[-4 * n:][k * n:(k + 1) * n] for k in range(4))
        for a in range(n):
            g = g_r[a][...]
            d, m, v = _adamw_math(w_r[a][...], g, m_r[a][...], v_r[a][...])
            g_o[a][...] = g
            d_o[a][...] = d
            m_o[a][...] = m
            v_o[a][...] = v

    specs = [pl.BlockSpec((None, t, a.shape[-1]), lambda i: (layer, i, 0)) for a, t in zip(ws, tiles)]
    held = [] if prev is None else [buf for kind in zip(*prev) for buf in kind]
    outs = pl.pallas_call(
        body, name=name, grid=(steps,),
        in_specs=specs * 4 + [ANY] * len(held), out_specs=specs * 4,
        out_shape=[jax.ShapeDtypeStruct(a.shape, F32) for a in ws] * 4,
        input_output_aliases={4 * n + k: k for k in range(len(held))},
        compiler_params=_params(1),
    )(*ws, *gs, *ms, *vs, *held)
    return [[outs[k * n + a] for k in range(4)] for a in range(n)]


SMALL_W = 512


def _small_allreduce_adamw(g, w, m, v, cw_w, cw_m, cw_v, cw_row0):
    R = g.shape[0]
    n_l = cw_w.shape[0]

    def body(g_ref, w_ref, m_ref, v_ref, cww_ref, cwm_ref, cwv_ref,
             gs_ref, d_ref, mo_ref, vo_ref, cg_ref, cd_ref, cmo_ref, cvo_ref,
             slots_ref, send_sems, recv_sems):
        x, y, c, _ = _place()
        me = 4 * x + 2 * y + c
        slots_ref[me] = g_ref[...]
        sends = []
        for d in range(1, 8):
            px, py, pc = x ^ (d >> 2), y ^ ((d >> 1) & 1), c ^ (d & 1)
            cp = pltpu.make_async_remote_copy(src_ref=g_ref, dst_ref=slots_ref.at[me], send_sem=send_sems.at[d - 1],
                                              recv_sem=recv_sems.at[d - 1], device_id=(px, py, pc), device_id_type=MESH)
            cp.start()
            sends.append(cp)
        for d in range(1, 8):
            px, py, pc = x ^ (d >> 2), y ^ ((d >> 1) & 1), c ^ (d & 1)
            slot = slots_ref.at[4 * px + 2 * py + pc]
            pltpu.make_async_remote_copy(src_ref=slot, dst_ref=slot, send_sem=send_sems.at[d - 1],
                                         recv_sem=recv_sems.at[d - 1], device_id=(px, py, pc),
                                         device_id_type=MESH).wait_recv()
        for cp in sends:
            cp.wait_send()
        tot = slots_ref[0]
        for k in range(1, 8):
            tot = tot + slots_ref[k]
        gs_ref[...] = tot
        dl, mn, vn = _adamw_math(w_ref[...], tot, m_ref[...], v_ref[...])
        d_ref[...] = dl
        mo_ref[...] = mn
        vo_ref[...] = vn
        chip = 2 * x + y
        for l in range(n_l):
            rows = tot[cw_row0[l]:cw_row0[l] + CONV_HALO, :]
            mine = rows[:, 0:128]
            for k in range(1, N_CHIPS):
                mine = jnp.where(chip == k, rows[:, 128 * k:128 * (k + 1)], mine)
            cg_ref[l] = mine
            dl, mn, vn = _adamw_math(cww_ref[l], mine, cwm_ref[l], cwv_ref[l])
            cd_ref[l] = dl
            cmo_ref[l] = mn
            cvo_ref[l] = vn

    vm = pl.BlockSpec(memory_space=pltpu.VMEM)
    small = jax.ShapeDtypeStruct((R, SMALL_W), F32)
    conv = jax.ShapeDtypeStruct(cw_w.shape, F32)
    return pl.pallas_call(
        body, name="small_allreduce_adamw",
        in_specs=[vm] * 7, out_specs=[vm] * 8,
        out_shape=[small] * 4 + [conv] * 4,
        scratch_shapes=[pltpu.VMEM((8, R, SMALL_W), F32), pltpu.SemaphoreType.DMA((7,)), pltpu.SemaphoreType.DMA((7,))],
        compiler_params=pltpu.CompilerParams(has_side_effects=True, vmem_limit_bytes=VMEM_LIMIT),
    )(g, w, m, v, cw_w, cw_m, cw_v)


SMALL_LAYOUT = (("conv_w", CONV_HALO), ("norm1_g", 2), ("norm2_g", 2), ("conv_b", 1), ("conv_ln_g", 1),
                ("conv_ln_b", 1), ("q_norm_g", 1), ("k_norm_g", 1), ("b_f", 1))
SMALL_ROWS = sum(r for _, r in SMALL_LAYOUT)
SMALL_ROWS_PAD = 48
LOSS_ROW = SMALL_ROWS


def _pack_small(per_layer):
    flat = []
    for d in per_layer:
        for name, r in SMALL_LAYOUT:
            n = r * SMALL_W
            a = d.get(name)
            if a is None:
                flat.append(jnp.zeros((n,), F32))
                continue
            flat.append(a.reshape(-1))
            if a.size < n:
                flat.append(jnp.zeros((n - a.size,), F32))
        spare = (SMALL_ROWS_PAD - SMALL_ROWS) * SMALL_W
        if "spare" in d:
            flat.append(d["spare"].reshape(-1))
            spare -= d["spare"].size
        flat.append(jnp.zeros((spare,), F32))
    return jnp.concatenate(flat).reshape(-1, SMALL_W)


def _unpack_small(packed, name, size):
    row0 = 0
    for nm, r in SMALL_LAYOUT:
        if nm == name:
            break
        row0 += r
    per_layer = packed.reshape(-1, SMALL_ROWS_PAD * SMALL_W)
    return per_layer[:, row0 * SMALL_W:row0 * SMALL_W + size]


SMALL_SIZES = dict(norm1_g=D_MODEL, norm2_g=D_MODEL, conv_b=CONV_CH, conv_ln_g=CONV_CH, conv_ln_b=CONV_CH,
                   q_norm_g=HEAD_DIM, k_norm_g=HEAD_DIM, b_f=ATT_HEADS)
SMALL_KEYS = dict(norm1_g="g1", norm2_g="g2", conv_b="cvb", conv_ln_g="lng", conv_ln_b="lnb",
                  q_norm_g="qg", k_norm_g="kg", b_f="bf", conv_w="cw")
CONV_W_ROW0 = 0


def kernel(x, norm1_g, w_in, b_f, q_norm_g, k_norm_g, conv_w, conv_b, conv_ln_g, conv_ln_b, w_o, norm2_g, w_mlp_in, w_mlp_out, loss_target, m_norm1_g, m_w_in, m_b_f, m_q_norm_g, m_k_norm_g, m_conv_w, m_conv_b, m_conv_ln_g, m_conv_ln_b, m_w_o, m_norm2_g, m_w_mlp_in, m_w_mlp_out, v_norm1_g, v_w_in, v_b_f, v_q_norm_g, v_k_norm_g, v_conv_w, v_conv_b, v_conv_ln_g, v_conv_ln_b, v_w_o, v_norm2_g, v_w_mlp_in, v_w_mlp_out):
    n_l = w_in.shape[0]

    per_layer = lambda l: [w_in[l].astype(BF16), conv_w[l], w_o[l].astype(BF16), w_mlp_in[l].astype(BF16),
                           w_mlp_out[l].astype(BF16)]
    n_w = len(per_layer(0))
    send_sems, recv_sems, srcs, lands, token = _gather_start([a for l in range(n_l) for a in per_layer(l)])

    def layer_weights(l):
        def wait(tag, which, after):
            groups = [n_w * l + i for i in which]
            return _gather_wait(f"gather_wait_{tag}{l}", groups, send_sems, recv_sems,
                                [srcs[g] for g in groups], [lands[g] for g in groups], after)

        def early(after):
            g_in, g_cw = wait("a", (0, 1), token if l == 0 else after)
            win = _win_to_internal(jnp.concatenate([g_in[k] for k in range(N_CHIPS)], axis=-1))
            cw = jnp.concatenate([g_cw[k] for k in range(N_CHIPS)], axis=-1)
            return _layer_params(l, win, cw, norm1_g, b_f, q_norm_g, k_norm_g, conv_b, conv_ln_g, conv_ln_b, norm2_g)

        def late(after):
            wo, w1, w2 = wait("b", (2, 3, 4), after)
            return dict(wo=wo, w1=w1, w2=w2)

        return early, late

    place = jnp.stack([2 * lax.axis_index("x") + lax.axis_index("y"), lax.axis_index("c")]).astype(jnp.int32)
    big_w = [w_in, w_o, w_mlp_in, w_mlp_out]
    pending, started, last = [], [], []

    def start_group(l, tag, grads, token=None):
        which, gs = list(grads), [_tie(g, token) for g in grads.values()]
        got = _half_swap(f"half_swap_{tag}{l}", gs)
        parts = _half_add(f"half_add_{tag}{l}", gs, got, place)
        send, recv, parts, lands, token = _exchange_start(f"exchange_start_{tag}{l}", parts)
        pending.append((f"{tag}{l}", l, which, send, recv, parts, lands))
        started.append(token)
        return token

    def reduce(l):
        def group(tag, grads):
            if (l, tag) == (0, "b"):
                last.append(grads)
                return None
            return start_group(l, tag, grads)
        return group

    loss, dx, smalls = _local_step(x[0], loss_target[0], [layer_weights(l) for l in range(n_l)], reduce)
    env = dict(norm1_g=(norm1_g, m_norm1_g, v_norm1_g), norm2_g=(norm2_g, m_norm2_g, v_norm2_g),
               conv_b=(conv_b, m_conv_b, v_conv_b), conv_ln_g=(conv_ln_g, m_conv_ln_g, v_conv_ln_g),
               conv_ln_b=(conv_ln_b, m_conv_ln_b, v_conv_ln_b), q_norm_g=(q_norm_g, m_q_norm_g, v_q_norm_g),
               k_norm_g=(k_norm_g, m_k_norm_g, v_k_norm_g), b_f=(b_f, m_b_f, v_b_f))
    g_dicts = [{nm: s[key] for nm, key in SMALL_KEYS.items()} for s in smalls]
    g_dicts[0]["spare"] = loss
    g_pack = _pack_small(g_dicts)
    packs = [_pack_small([{nm: env[nm][t][l] for nm in env} for l in range(n_l)]) for t in range(3)]
    pad_cw = lambda a: jnp.pad(a, ((0, 0), (0, CONV_HALO - CONV_TAPS), (0, 0)))
    cw_row0 = tuple(l * SMALL_ROWS_PAD + CONV_W_ROW0 for l in range(n_l))
    gs, ds, ms, vs, cg, cd, cm, cv = _small_allreduce_adamw(
        g_pack, packs[0], packs[1], packs[2], pad_cw(conv_w), pad_cw(m_conv_w), pad_cw(v_conv_w), cw_row0)

    start_group(0, "b", last[0], gs[SMALL_ROWS_PAD - 1:])
    sums = [lax.empty(w.shape, F32) for w in big_w]
    big_m = [m_w_in, m_w_o, m_w_mlp_in, m_w_mlp_out]
    big_v = [v_w_in, v_w_o, v_w_mlp_in, v_w_mlp_out]
    updated = [None] * len(big_w)
    after = started[-1]
    for tag, l, which, send, recv, parts, lands in pending:
        parts, lands = _exchange_wait(f"exchange_wait_{tag}", send, recv, parts, lands, after)
        done = _chip_sum(f"chip_sum_{tag}", parts, lands, [sums[i] for i in which], place, l)
        done = _half_fill(f"half_fill_{tag}", done, l)
        for i, a in zip(which, done):
            sums[i] = a
        prev = None if updated[which[0]] is None else [updated[i] for i in which]
        new = _adamw(f"adamw_{tag}", [big_w[i] for i in which], done, [big_m[i] for i in which],
                     [big_v[i] for i in which], l, prev)
        for i, r in zip(which, new):
            updated[i] = r
        after = new[0][1]
    g_big, d_big, nm_big, nv_big = ([r[k] for r in updated] for k in range(4))

    def small_out(packed, conv):
        o = {nm: _unpack_small(packed, nm, sz) for nm, sz in SMALL_SIZES.items()}
        o["conv_w"] = conv[:, 0:CONV_TAPS, :]
        return o

    def ordered(small, big):
        return (small["norm1_g"], big[0], small["b_f"], small["q_norm_g"], small["k_norm_g"], small["conv_w"],
                small["conv_b"], small["conv_ln_g"], small["conv_ln_b"], big[1], small["norm2_g"], big[2], big[3])

    return (gs[LOSS_ROW, 0], dx[None],
            *ordered(small_out(gs, cg), g_big), *ordered(small_out(ds, cd), d_big),
            *ordered(small_out(ms, cm), nm_big), *ordered(small_out(vs, cv), nv_big))
```

```python
import jax
import jax.numpy as jnp
from jax import lax
from jax.experimental import pallas as pl
from jax.experimental.pallas import tpu as pltpu

F32 = jnp.float32
BF16 = jnp.bfloat16

D_MODEL = 1024
ATT_HEADS = 8
HEAD_DIM = 64
ATT_W = ATT_HEADS * HEAD_DIM
CONV_CH = 512
CONV_TAPS = 31
CONV_HALO = 32
D_FF = 4 * D_MODEL
N_IN = 3 * ATT_W + ATT_HEADS + 2 * CONV_CH
O_Q, O_K, O_V, O_A, O_G, O_F = 0, 512, 1024, 1536, 2048, 2560
N_INP = O_F + 128
EPS = 1e-6
QK_SCALE = 0.125

ADAM_LR = 0.001
ADAM_B1 = 0.9
ADAM_B2 = 0.999
ADAM_EPS = 1e-08
ADAM_WD = 0.01
ADAM_STEP = 10

N_CHIPS = 4
VMEM_LIMIT = 52 * 1024 * 1024
MESH = pl.DeviceIdType.MESH
ANY = pl.BlockSpec(memory_space=pl.ANY)


def _params(n_axes, **kw):
    return pltpu.CompilerParams(dimension_semantics=("arbitrary",) * n_axes,
                                vmem_limit_bytes=VMEM_LIMIT, **kw)


def _dot(a, b):
    return jnp.dot(a, b, preferred_element_type=F32)


def _dot_nt(a, b):
    return lax.dot_general(a, b, (((1,), (1,)), ((), ())), preferred_element_type=F32)


def _dot_tn(a, b):
    return lax.dot_general(a, b, (((0,), (0,)), ((), ())), preferred_element_type=F32)


def _split3(a):
    a1 = a.astype(BF16)
    r = a - a1.astype(F32)
    a2 = r.astype(BF16)
    a3 = (r - a2.astype(F32)).astype(BF16)
    return a1, a2, a3


def _dot_hi_r(a, b_exact):
    return sum(_dot(p, b_exact) for p in _split3(a))


def _head_sums(a, blockdiag):
    a1 = a.astype(BF16)
    a2 = (a - a1.astype(F32)).astype(BF16)
    return _dot(a1, blockdiag) + _dot(a2, blockdiag)


def _dot_hi_l(a_exact, b):
    return sum(_dot(a_exact, p) for p in _split3(b))


def _sigmoid(x):
    return 1.0 / (1.0 + jnp.exp(-x))


def _head_blockdiag():
    i = jnp.arange(ATT_W) // HEAD_DIM
    return (i[:, None] == i[None, :]).astype(BF16)


AUG_LANES = 8


def _aug_place(first):
    piece = jnp.arange(3 * 128)[:, None] // 128
    h = jnp.arange(3 * 128)[:, None] % 128
    lane = jnp.arange(ATT_W)[None, :]
    return ((h < ATT_HEADS) & (lane == 128 * (h // 2) + AUG_LANES * (h % 2) + first + piece)).astype(BF16)


def _aug_ones(first):
    lane = jnp.arange(ATT_W) % 128
    pos = lane % AUG_LANES
    return ((lane < 2 * AUG_LANES) & (pos >= first) & (pos < first + 3)).astype(F32).reshape(1, ATT_W)


def _head_rows():
    h = jnp.arange(2 * ATT_HEADS)[:, None]
    i = jnp.arange(ATT_W)[None, :] // HEAD_DIM
    return (h == i).astype(BF16)


def _head_fold():
    i = jnp.arange(ATT_W)[:, None] % HEAD_DIM
    j = jnp.arange(128)[None, :]
    return (i == j).astype(BF16)


def _head_pick():
    i = jnp.arange(ATT_W)[:, None]
    h = jnp.arange(128)[None, :]
    return (i == h * HEAD_DIM).astype(BF16)


def _tril(n):
    r = jnp.arange(n)
    return (r[:, None] >= r[None, :]).astype(BF16)


SUBLANES = 8


def _fill_row_shifts(buf_ref, shifts_ref, tm):
    n = tm + CONV_HALO - SUBLANES
    for b in range(1, SUBLANES):
        shifts_ref[b - 1, 0:n, :] = buf_ref[pl.ds(b, n), :]


def _row_shifted(buf_ref, shifts_ref, offset, rows, base=0):
    a, b = divmod(offset, SUBLANES)
    start = pl.multiple_of(base + SUBLANES * a, SUBLANES)
    if b == 0:
        return buf_ref[pl.ds(start, rows), :]
    return shifts_ref[b - 1, pl.ds(start, rows), :]


CONV_ROWS = 32


def _mixer_in_fwd(x, g1, win, qg, kg, bf, cw, cvb, lng, lnb):
    S = x.shape[0]
    TM = min(512, S)
    nb = S // TM

    def body(x_ref, g1_ref, win_ref, qg_ref, kg_ref, bf_ref, cw_ref, cvb_ref, lng_ref, lnb_ref,
             bd_ref, tri_ref, pq_ref, pk_ref, oq_ref, ok_ref,
             u1_ref, proj_ref, q_ref, k_ref, v_ref, qa_ref, ka_ref, h0_ref, h1_ref, h3_ref,
             carry_ref, hbuf_ref, hs_ref):
        i = pl.program_id(0)

        @pl.when(i == 0)
        def _():
            carry_ref[...] = jnp.zeros_like(carry_ref)
            hbuf_ref[0:CONV_HALO, :] = jnp.zeros((CONV_HALO, CONV_CH), F32)

        @pl.when(i > 0)
        def _():
            hbuf_ref[0:CONV_HALO, :] = hbuf_ref[TM:TM + CONV_HALO, :]

        xv = x_ref[...]
        r = lax.rsqrt(jnp.mean(xv * xv, axis=-1, keepdims=True) + EPS)
        u = (xv * r * g1_ref[...]).astype(BF16)
        u1_ref[...] = u
        proj_ref[...] = _dot(u, win_ref[...])

        def headnorm(raw, gain):
            ss = _head_sums(raw * raw, bd_ref[...]) * (1.0 / HEAD_DIM)
            return raw * lax.rsqrt(ss + EPS) * gain

        q_ref[...] = (headnorm(proj_ref[:, O_Q:O_Q + ATT_W], qg_ref[...]) * QK_SCALE).astype(BF16)
        k_ref[...] = headnorm(proj_ref[:, O_K:O_K + ATT_W], kg_ref[...]).astype(BF16)
        v_ref[...] = proj_ref[:, O_V:O_V + ATT_W].astype(BF16)

        zf = proj_ref[:, O_F:O_F + 128] + bf_ref[...]
        logf = jnp.minimum(zf, 0.0) - jnp.log(1.0 + jnp.exp(-jnp.abs(zf)))
        lane = lax.broadcasted_iota(jnp.int32, (TM, 128), 1)
        logf = jnp.where(lane < ATT_HEADS, logf, 0.0)
        c8 = _dot_hi_l(tri_ref[...], logf) + carry_ref[...]
        carry_ref[...] = c8[TM - 1:TM, :]
        pieces = jnp.concatenate(_split3(c8), axis=1)
        qa_ref[...] = (_dot(pieces, pq_ref[...]) + oq_ref[...]).astype(BF16)
        ka_ref[...] = (ok_ref[...] - _dot(pieces, pk_ref[...])).astype(BF16)

        h0 = proj_ref[:, O_A:O_A + CONV_CH] * _sigmoid(proj_ref[:, O_G:O_G + CONV_CH])
        h0_ref[...] = h0
        hbuf_ref[CONV_HALO:CONV_HALO + TM, :] = h0
        _fill_row_shifts(hbuf_ref, hs_ref, TM)
        acc = jnp.zeros((TM, CONV_CH), F32) + cvb_ref[...]
        for j in range(CONV_TAPS):
            acc = acc + cw_ref[j:j + 1, :] * _row_shifted(hbuf_ref, hs_ref, CONV_HALO - CONV_TAPS + 1 + j, TM)
        h1_ref[...] = acc
        mu = jnp.mean(acc, axis=-1, keepdims=True)
        d = acc - mu
        var = jnp.mean(d * d, axis=-1, keepdims=True)
        h2 = d * lax.rsqrt(var + EPS) * lng_ref[...] + lnb_ref[...]
        h3_ref[...] = (h2 * _sigmoid(h2)).astype(BF16)

    row = lambda w: pl.BlockSpec((TM, w), lambda i: (i, 0))
    full = lambda a: pl.BlockSpec(a.shape, lambda i: (0,) * a.ndim)
    ins = (x, g1, win, qg, kg, bf, cw, cvb, lng, lnb, _head_blockdiag(), _tril(TM),
           _aug_place(0), _aug_place(3), _aug_ones(3), _aug_ones(0))
    return pl.pallas_call(
        body, name="mixer_in_fwd", grid=(nb,),
        in_specs=[row(D_MODEL)] + [full(a) for a in ins[1:]],
        out_specs=[row(D_MODEL), row(N_INP), row(ATT_W), row(ATT_W), row(ATT_W), row(ATT_W), row(ATT_W),
                   row(CONV_CH), row(CONV_CH), row(CONV_CH)],
        out_shape=[jax.ShapeDtypeStruct((S, D_MODEL), BF16),
                   jax.ShapeDtypeStruct((S, N_INP), F32),
                   jax.ShapeDtypeStruct((S, ATT_W), BF16),
                   jax.ShapeDtypeStruct((S, ATT_W), BF16),
                   jax.ShapeDtypeStruct((S, ATT_W), BF16),
                   jax.ShapeDtypeStruct((S, ATT_W), BF16),
                   jax.ShapeDtypeStruct((S, ATT_W), BF16),
                   jax.ShapeDtypeStruct((S, CONV_CH), F32),
                   jax.ShapeDtypeStruct((S, CONV_CH), F32),
                   jax.ShapeDtypeStruct((S, CONV_CH), BF16)],
        scratch_shapes=[pltpu.VMEM((1, 128), F32), pltpu.VMEM((TM + CONV_HALO, CONV_CH), F32),
                        pltpu.VMEM((SUBLANES - 1, TM + CONV_HALO, CONV_CH), F32)],
        compiler_params=_params(1),
    )(*ins)


def _pair_heads(lo, alo, x, xa):
    z = jnp.zeros_like(x)
    return (jnp.concatenate([jnp.where(lo, x, z), jnp.where(alo, xa, z)], axis=1),
            jnp.concatenate([jnp.where(lo, z, x), jnp.where(alo, z, xa)], axis=1))


def _attn_fwd(q, qa, k, ka, v):
    S = q.shape[0]
    T = min(1024, S)
    nq = S // T

    def body(q_ref, qa_ref, k_ref, ka_ref, v_ref, o_ref, lse_ref):
        qi = pl.program_id(1)
        lane = lax.broadcasted_iota(jnp.int32, (T, 128), 1)
        lo = lane < HEAD_DIM
        qm = _pair_heads(lo, lane < AUG_LANES, q_ref[...], qa_ref[...])
        tril = (lax.broadcasted_iota(jnp.int32, (T, T), 0) >= lax.broadcasted_iota(jnp.int32, (T, T), 1))

        def step(kj, carry, masked):
            off = pl.multiple_of(kj * T, T)
            kb = jnp.concatenate([k_ref[pl.ds(off, T), :], ka_ref[pl.ds(off, T), :]], axis=1)
            vb = v_ref[pl.ds(off, T), :]
            new = []
            for h in range(2):
                m, l, acc = carry[3 * h:3 * h + 3]
                s = _dot_nt(qm[h], kb)
                if masked:
                    s = jnp.where(tril, s, -1e30)
                m_new = jnp.maximum(m, jnp.max(s, axis=-1, keepdims=True))
                alpha = jnp.exp(m - m_new)
                p = jnp.exp(s - m_new)
                l = alpha * l + jnp.sum(p, axis=-1, keepdims=True)
                acc = alpha * acc + _dot(p.astype(BF16), vb)
                new += [m_new, l, acc]
            return tuple(new)

        init = (jnp.full((T, 1), -1e30, F32), jnp.zeros((T, 1), F32), jnp.zeros((T, 128), F32)) * 2
        carry = lax.fori_loop(0, qi, lambda kj, c: step(kj, c, False), init)
        m0, l0, a0, m1, l1, a1 = step(qi, carry, True)
        o_ref[...] = jnp.where(lo, a0 / l0, a1 / l1).astype(BF16)
        lse_t = jnp.where(lo, m0 + jnp.log(l0), m1 + jnp.log(l1)).T
        lse_ref[0:1, :] = lse_t[0:1, :]
        lse_ref[1:2, :] = lse_t[HEAD_DIM:HEAD_DIM + 1, :]

    qblk = pl.BlockSpec((T, 128), lambda hp, qi: (qi, hp))
    seq = pl.BlockSpec((S, 128), lambda hp, qi: (0, hp))
    return pl.pallas_call(
        body, name="attn_fwd", grid=(ATT_HEADS // 2, nq),
        in_specs=[qblk, qblk, seq, seq, seq],
        out_specs=[qblk, pl.BlockSpec((None, 2, T), lambda hp, qi: (hp, 0, qi))],
        out_shape=[jax.ShapeDtypeStruct((S, ATT_W), BF16),
                   jax.ShapeDtypeStruct((ATT_HEADS // 2, 2, S), F32)],
        compiler_params=_params(2),
    )(q, qa, k, ka, v)


def _attn_bwd(q, qa, k, ka, v, do, lse, dlt):
    S = q.shape[0]
    T = min(1024, S)
    nq = S // T

    def body(q_ref, qa_ref, do_ref, lse_ref, dlt_ref, k_ref, ka_ref, v_ref, dq_ref, dk_ref, dv_ref, rows_ref, cols_ref):
        kj = pl.program_id(1)

        @pl.when(kj == 0)
        def _():
            dq_ref[...] = jnp.zeros_like(dq_ref)
            rows_ref[...] = jnp.zeros_like(rows_ref)

        lane = lax.broadcasted_iota(jnp.int32, (T, 128), 1)
        lo = lane < HEAD_DIM
        alo = lane < AUG_LANES
        triu = (lax.broadcasted_iota(jnp.int32, (T, T), 0) <= lax.broadcasted_iota(jnp.int32, (T, T), 1))
        kb = k_ref[...]
        kaug = jnp.concatenate([kb, ka_ref[...]], axis=1)
        vb = v_ref[...]

        def step(qi, carry, masked):
            off = pl.multiple_of(qi * T, T)
            qb = q_ref[pl.ds(off, T), :]
            dob = do_ref[pl.ds(off, T), :]
            qm = _pair_heads(lo, alo, qb, qa_ref[pl.ds(off, T), :])
            zero = jnp.zeros_like(qb)
            new, dqs = [], []
            for h in range(2):
                dk_a, dv_a, dc_a = carry[3 * h:3 * h + 3]
                dom = jnp.where(lo, dob, zero) if h == 0 else jnp.where(lo, zero, dob)
                s = _dot_nt(kaug, qm[h])
                if masked:
                    s = jnp.where(triu, s, -1e30)
                p = jnp.exp(s - lse_ref[h:h + 1, pl.ds(off, T)])
                dp = _dot_nt(vb, dom)
                ds = p * (dp - dlt_ref[h:h + 1, pl.ds(off, T)])
                pb = p.astype(BF16)
                dsb = ds.astype(BF16)
                dv_a = dv_a + _dot(pb, dob)
                dk_a = dk_a + _dot(dsb, qb)
                dc_a = dc_a + jnp.sum(ds, axis=1, keepdims=True)
                dqs.append(_dot_tn(dsb, kb))
                rows_ref[h:h + 1, pl.ds(off, T)] += jnp.sum(ds, axis=0, keepdims=True)
                new += [dk_a, dv_a, dc_a]
            dq_ref[pl.ds(off, T), :] += jnp.where(lo, dqs[0], dqs[1])
            return tuple(new)

        init = (jnp.zeros((T, 128), F32), jnp.zeros((T, 128), F32), jnp.zeros((T, 1), F32)) * 2
        carry = step(kj, init, True)
        carry = lax.fori_loop(kj + 1, nq, lambda qi, c: step(qi, c, False), carry)
        dk_ref[...] = jnp.where(lo, carry[0], carry[3])
        dv_ref[...] = jnp.where(lo, carry[1], carry[4])
        cols_ref[...] = -jnp.where(lo, carry[2], carry[5])

    seq = pl.BlockSpec((S, 128), lambda hp, kj: (0, hp))
    rows = pl.BlockSpec((None, 2, S), lambda hp, kj: (hp, 0, 0))
    kblk = pl.BlockSpec((T, 128), lambda hp, kj: (kj, hp))
    return pl.pallas_call(
        body, name="attn_bwd", grid=(ATT_HEADS // 2, nq),
        in_specs=[seq, seq, seq, rows, rows, kblk, kblk, kblk],
        out_specs=[seq, kblk, kblk, rows, kblk],
        out_shape=[jax.ShapeDtypeStruct((S, ATT_W), F32), jax.ShapeDtypeStruct((S, ATT_W), F32),
                   jax.ShapeDtypeStruct((S, ATT_W), F32),
                   jax.ShapeDtypeStruct((ATT_HEADS // 2, 2, S), F32),
                   jax.ShapeDtypeStruct((S, ATT_W), F32)],
        compiler_params=_params(2),
    )(q, qa, do, lse, dlt, k, ka, v)


def _wo_spec(wo4):
    return pl.BlockSpec(wo4.shape, lambda i: (0, 0, 0))


def _wo_halves(wo_ref):
    half = N_CHIPS // 2
    return (wo_ref[0:half].reshape(ATT_W, D_MODEL), wo_ref[half:N_CHIPS].reshape(CONV_CH, D_MODEL))


def _mixer_out_fwd(x, att, h3, wo4, g2):
    S = x.shape[0]
    TM = min(512, S)

    def body(x_ref, att_ref, h3_ref, wo_ref, g2_ref, x2_ref, u2_ref):
        wa, wc = _wo_halves(wo_ref)
        x2 = x_ref[...] + _dot(att_ref[...], wa) + _dot(h3_ref[...], wc)
        x2_ref[...] = x2
        r = lax.rsqrt(jnp.mean(x2 * x2, axis=-1, keepdims=True) + EPS)
        u2_ref[...] = (x2 * r * g2_ref[...]).astype(BF16)

    row = lambda w: pl.BlockSpec((TM, w), lambda i: (i, 0))
    full = lambda a: pl.BlockSpec(a.shape, lambda i: (0,) * a.ndim)
    return pl.pallas_call(
        body, name="mixer_out_fwd", grid=(S // TM,),
        in_specs=[row(D_MODEL), row(ATT_W), row(CONV_CH), _wo_spec(wo4), full(g2)],
        out_specs=[row(D_MODEL), row(D_MODEL)],
        out_shape=[jax.ShapeDtypeStruct((S, D_MODEL), F32), jax.ShapeDtypeStruct((S, D_MODEL), BF16)],
        compiler_params=_params(1),
    )(x, att, h3, wo4, g2)


def _mlp_w_specs():
    return [pl.BlockSpec((None, D_MODEL, D_FF // N_CHIPS), lambda i, f: (f, 0, 0)),
            pl.BlockSpec((None, D_FF // N_CHIPS, D_MODEL), lambda i, f: (f, 0, 0))]


def _mlp_fwd(x2, u2, w1, w2, target=None):
    S = x2.shape[0]
    head = target is not None
    TM = min(1024, S)
    TF = 1024
    nf = D_FF // TF

    def body(*refs):
        x2_ref, u2_ref, w1_ref, w2_ref = refs[:4]
        x3_ref, z_ref, hh_ref = refs[4 + head:7 + head]
        i = pl.program_id(0)
        f = pl.program_id(1)

        @pl.when(f == 0)
        def _():
            x3_ref[...] = x2_ref[...]

        z = _dot(u2_ref[...], w1_ref[...])
        z_ref[...] = z
        zr = jnp.maximum(z, 0.0)
        hh = (zr * zr).astype(BF16)
        hh_ref[...] = hh
        x3_ref[...] += _dot(hh, w2_ref[...])

        if head:
            t_ref, loss_ref = refs[4], refs[8]

            @pl.when((i == 0) & (f == 0))
            def _():
                loss_ref[...] = jnp.zeros_like(loss_ref)

            @pl.when(f == nf - 1)
            def _():
                d = x3_ref[...] - t_ref[...]
                x3_ref[...] = d * (1.0 / D_MODEL)
                loss_ref[...] += jnp.sum(d * d)

    rows = pl.BlockSpec((TM, D_MODEL), lambda i, f: (i, 0))
    once = pl.BlockSpec((TM, D_MODEL), lambda i, f: (i, 0), pipeline_mode=pl.Buffered(1))
    tile = pl.BlockSpec((TM, TF), lambda i, f: (i, f))
    return pl.pallas_call(
        body, name="mlp_fwd_loss" if head else "mlp_fwd", grid=(S // TM, nf),
        in_specs=[once if head else rows, rows] + _mlp_w_specs() + [once] * head,
        out_specs=[rows, tile, tile] + [pl.BlockSpec((8, 128), lambda i, f: (0, 0))] * head,
        out_shape=[jax.ShapeDtypeStruct((S, D_MODEL), F32), jax.ShapeDtypeStruct((S, D_FF), F32),
                   jax.ShapeDtypeStruct((S, D_FF), BF16)] + [jax.ShapeDtypeStruct((8, 128), F32)] * head,
        compiler_params=_params(2),
    )(x2, u2, w1, w2, *([target] if head else []))


def _mlp_bwd(dx3, z, x2, g2, w1, w2):
    S = dx3.shape[0]
    TM = min(1024, S)
    TF = 1024
    nf = D_FF // TF

    def body(dx3_ref, z_ref, x2_ref, g2_ref, w1_ref, w2_ref, dz_ref, dx2_ref, dg2_ref, du2_ref):
        i = pl.program_id(0)
        f = pl.program_id(1)

        @pl.when((i == 0) & (f == 0))
        def _():
            dg2_ref[...] = jnp.zeros_like(dg2_ref)

        @pl.when(f == 0)
        def _():
            du2_ref[...] = jnp.zeros_like(du2_ref)

        dhh = _dot_nt(dx3_ref[...].astype(BF16), w2_ref[...])
        dz = (dhh * (2.0 * jnp.maximum(z_ref[...], 0.0))).astype(BF16)
        dz_ref[...] = dz
        du2_ref[...] += _dot_nt(dz, w1_ref[...])

        @pl.when(f == nf - 1)
        def _():
            x2 = x2_ref[...]
            r = lax.rsqrt(jnp.mean(x2 * x2, axis=-1, keepdims=True) + EPS)
            n = x2 * r
            du2 = du2_ref[...]
            t = du2 * g2_ref[...]
            dx2_ref[...] = dx3_ref[...] + r * (t - n * jnp.mean(t * n, axis=-1, keepdims=True))
            dg2_ref[0:1, :] += jnp.sum(du2 * n, axis=0, keepdims=True)

    rowi = pl.BlockSpec((TM, D_MODEL), lambda i, f: (i, 0))
    held = pl.BlockSpec((TM, D_MODEL), lambda i, f: (i, 0), pipeline_mode=pl.Buffered(1))
    return pl.pallas_call(
        body, name="mlp_bwd", grid=(S // TM, nf),
        in_specs=[held, pl.BlockSpec((TM, TF), lambda i, f: (i, f)), held,
                  pl.BlockSpec((1, D_MODEL), lambda i, f: (0, 0))] + _mlp_w_specs(),
        out_specs=[pl.BlockSpec((TM, TF), lambda i, f: (i, f)), rowi, pl.BlockSpec((8, D_MODEL), lambda i, f: (0, 0))],
        out_shape=[jax.ShapeDtypeStruct((S, D_FF), BF16), jax.ShapeDtypeStruct((S, D_MODEL), F32),
                   jax.ShapeDtypeStruct((8, D_MODEL), F32)],
        scratch_shapes=[pltpu.VMEM((TM, D_MODEL), F32)],
        compiler_params=_params(2),
    )(dx3, z, x2, g2, w1, w2)


def _matmul_tn(a, b, col_shards=1):
    S, I = a.shape
    J = b.shape[1]
    TI = min(I, 1024)
    TJ = 1024 if J % 1024 == 0 else 896
    TS = min(S, 2048)
    nk = S // TS
    per = J // col_shards // TJ

    def body(a_ref, b_ref, o_ref, acc_ref):
        k = pl.program_id(2)

        @pl.when(k == 0)
        def _():
            acc_ref[...] = jnp.zeros_like(acc_ref)

        acc_ref[...] += _dot_tn(a_ref[...].astype(BF16), b_ref[...].astype(BF16))

        @pl.when(k == nk - 1)
        def _():
            o_ref[...] = acc_ref[...].astype(BF16)

    return pl.pallas_call(
        body, name="matmul_tn", grid=(I // TI, J // TJ, nk),
        in_specs=[pl.BlockSpec((TS, TI), lambda i, j, k: (k, i)), pl.BlockSpec((TS, TJ), lambda i, j, k: (k, j))],
        out_specs=pl.BlockSpec((None, TI, TJ), lambda i, j, k: (j // per, i, j % per)),
        out_shape=jax.ShapeDtypeStruct((col_shards, I, J // col_shards), BF16),
        scratch_shapes=[pltpu.VMEM((TI, TJ), F32)],
        compiler_params=_params(3),
    )(a, b)


def _mixer_out_bwd(dx2, wo4, att, h1, lng, lnb):
    S = dx2.shape[0]
    TM = min(512, S)

    def body(dx2_ref, wo_ref, att_ref, h1_ref, lng_ref, lnb_ref, hr_ref, datt_ref, dlt_ref, dh1_ref, sm_ref):
        @pl.when(pl.program_id(0) == 0)
        def _():
            sm_ref[...] = jnp.zeros_like(sm_ref)

        dxb = dx2_ref[...].astype(BF16)
        wa, wc = _wo_halves(wo_ref)
        datt = _dot_nt(dxb, wa)
        datt_ref[...] = datt.astype(BF16)
        prod = datt * att_ref[...].astype(F32)
        dlt_ref[...] = sum(_dot_nt(hr_ref[...], piece) for piece in _split3(prod))[0:ATT_HEADS, :]
        dh3 = _dot_nt(dxb, wc)
        h1 = h1_ref[...]
        mu = jnp.mean(h1, axis=-1, keepdims=True)
        d = h1 - mu
        rstd = lax.rsqrt(jnp.mean(d * d, axis=-1, keepdims=True) + EPS)
        n = d * rstd
        h2 = n * lng_ref[...] + lnb_ref[...]
        sg = _sigmoid(h2)
        dh2 = dh3 * (sg * (1.0 + h2 * (1.0 - sg)))
        dn = dh2 * lng_ref[...]
        dh1 = rstd * (dn - jnp.mean(dn, axis=-1, keepdims=True) - n * jnp.mean(dn * n, axis=-1, keepdims=True))
        dh1_ref[...] = dh1
        sm_ref[0:1, :] += jnp.sum(dh2 * n, axis=0, keepdims=True)
        sm_ref[1:2, :] += jnp.sum(dh2, axis=0, keepdims=True)
        sm_ref[2:3, :] += jnp.sum(dh1, axis=0, keepdims=True)

    row = lambda w: pl.BlockSpec((TM, w), lambda i: (i, 0))
    full = lambda a: pl.BlockSpec(a.shape, lambda i: (0,) * a.ndim)
    hr = _head_rows()
    return pl.pallas_call(
        body, name="mixer_out_bwd", grid=(S // TM,),
        in_specs=[row(D_MODEL), _wo_spec(wo4), row(ATT_W), row(CONV_CH), full(lng), full(lnb), full(hr)],
        out_specs=[row(ATT_W), pl.BlockSpec((ATT_HEADS, TM), lambda i: (0, i)), row(CONV_CH),
                   pl.BlockSpec((8, CONV_CH), lambda i: (0, 0))],
        out_shape=[jax.ShapeDtypeStruct((S, ATT_W), BF16), jax.ShapeDtypeStruct((ATT_HEADS, S), F32),
                   jax.ShapeDtypeStruct((S, CONV_CH), F32), jax.ShapeDtypeStruct((8, CONV_CH), F32)],
        compiler_params=_params(1),
    )(dx2, wo4, att, h1, lng, lnb, hr)


def _conv_glu_bwd(dh1, h0, proj, cw):
    S = dh1.shape[0]
    TM = min(512, S)
    nb = S // TM
    lead = CONV_HALO - CONV_TAPS + 1

    def body(dh1_ref, dnx_ref, h0_ref, hpv_ref, a_ref, g_ref, cw_ref, dag_ref, dcw_ref,
             dbuf_ref, hbuf_ref, ds_ref, hs_ref, dh0_ref, dcw8_ref):
        i = pl.program_id(0)

        @pl.when(i == 0)
        def _():
            dcw8_ref[...] = jnp.zeros_like(dcw8_ref)

        dbuf_ref[0:TM, :] = dh1_ref[...]
        dbuf_ref[TM:TM + CONV_HALO, :] = jnp.where(i < nb - 1, dnx_ref[0:CONV_HALO, :], 0.0)
        hbuf_ref[0:CONV_HALO, :] = jnp.where(i > 0, hpv_ref[TM - CONV_HALO:TM, :], 0.0)
        hbuf_ref[CONV_HALO:CONV_HALO + TM, :] = h0_ref[...]
        _fill_row_shifts(dbuf_ref, ds_ref, TM)
        _fill_row_shifts(hbuf_ref, hs_ref, TM)

        def conv_rows(step, _):
            r0 = pl.multiple_of(step * CONV_ROWS, CONV_ROWS)
            dh1 = dbuf_ref[pl.ds(r0, CONV_ROWS), :]
            part = jnp.zeros((CONV_ROWS, CONV_CH), F32)
            for j in range(CONV_TAPS):
                part = part + cw_ref[j:j + 1, :] * _row_shifted(dbuf_ref, ds_ref, CONV_TAPS - 1 - j, CONV_ROWS, r0)
                prod = dh1 * _row_shifted(hbuf_ref, hs_ref, lead + j, CONV_ROWS, r0)
                dcw8_ref[j] += jnp.sum(prod.reshape(CONV_ROWS // SUBLANES, SUBLANES, CONV_CH), axis=0)
            dh0_ref[pl.ds(r0, CONV_ROWS), :] = part
            return 0

        lax.fori_loop(0, TM // CONV_ROWS, conv_rows, 0)

        @pl.when(i == nb - 1)
        def _():
            dcw_ref[...] = jnp.sum(dcw8_ref[...], axis=1)

        dh0 = dh0_ref[...]
        sg = _sigmoid(g_ref[...])
        dag_ref[:, 0:CONV_CH] = (dh0 * sg).astype(BF16)
        dag_ref[:, CONV_CH:2 * CONV_CH] = (dh0 * a_ref[...] * sg * (1.0 - sg)).astype(BF16)

    blk = lambda fn: pl.BlockSpec((TM, CONV_CH), fn)
    return pl.pallas_call(
        body, name="conv_glu_bwd", grid=(nb,),
        in_specs=[blk(lambda i: (i, 0)), blk(lambda i: (jnp.minimum(i + 1, nb - 1), 0)),
                  blk(lambda i: (i, 0)), blk(lambda i: (jnp.maximum(i - 1, 0), 0)),
                  blk(lambda i: (i, O_A // CONV_CH)), blk(lambda i: (i, O_G // CONV_CH)),
                  pl.BlockSpec(cw.shape, lambda i: (0, 0))],
        out_specs=[pl.BlockSpec((TM, 2 * CONV_CH), lambda i: (i, 0)), pl.BlockSpec((CONV_HALO, CONV_CH), lambda i: (0, 0))],
        out_shape=[jax.ShapeDtypeStruct((S, 2 * CONV_CH), BF16), jax.ShapeDtypeStruct((CONV_HALO, CONV_CH), F32)],
        scratch_shapes=[pltpu.VMEM((TM + CONV_HALO, CONV_CH), F32), pltpu.VMEM((TM + CONV_HALO, CONV_CH), F32),
                        pltpu.VMEM((SUBLANES - 1, TM + CONV_HALO, CONV_CH), F32),
                        pltpu.VMEM((SUBLANES - 1, TM + CONV_HALO, CONV_CH), F32),
                        pltpu.VMEM((TM, CONV_CH), F32), pltpu.VMEM((CONV_HALO, SUBLANES, CONV_CH), F32)],
        compiler_params=_params(1),
    )(dh1, dh1, h0, h0, proj, proj, cw)


def _mixer_in_bwd(x, dx2, proj, dq, dk, dv, dag, dct, drb, g1, win, qg, kg, bf):
    S = x.shape[0]
    TM = min(512, S)
    nb = S // TM

    def body(x_ref, dx2_ref, qr_ref, kr_ref, fz_ref, dq_ref, dk_ref, dv_ref, dag_ref, dct_ref, drb_ref,
             g1_ref, win_ref, qg_ref, kg_ref, bf_ref, bd_ref, fold_ref, triu_ref, pick_ref,
             dproj_ref, dx_ref, dg1_ref, sm_ref, carry_ref, gsum_ref):
        i = pl.program_id(0)

        @pl.when(i == 0)
        def _():
            carry_ref[...] = jnp.zeros_like(carry_ref)
            gsum_ref[...] = jnp.zeros_like(gsum_ref)
            dg1_ref[...] = jnp.zeros_like(dg1_ref)
            sm_ref[...] = jnp.zeros_like(sm_ref)

        def headnorm_bwd(raw, dy, gain, scale, row):
            rs = lax.rsqrt(_head_sums(raw * raw, bd_ref[...]) * (1.0 / HEAD_DIM) + EPS)
            n = raw * rs
            gsum_ref[row:row + 1, :] += jnp.sum(dy * n, axis=0, keepdims=True) * scale
            dn = dy * (gain * scale)
            return rs * (dn - n * (_head_sums(dn * n, bd_ref[...]) * (1.0 / HEAD_DIM)))

        dproj_ref[:, O_Q:O_Q + ATT_W] = headnorm_bwd(qr_ref[...], dq_ref[...], qg_ref[...], QK_SCALE, 0).astype(BF16)
        dproj_ref[:, O_K:O_K + ATT_W] = headnorm_bwd(kr_ref[...], dk_ref[...], kg_ref[...], 1.0, 1).astype(BF16)
        dproj_ref[:, O_V:O_V + ATT_W] = dv_ref[...].astype(BF16)
        dproj_ref[:, O_A:O_A + 2 * CONV_CH] = dag_ref[...]

        dc8 = jnp.concatenate([dct_ref[...], jnp.zeros((128 - ATT_HEADS, TM), F32)], axis=0).T
        dc8 = dc8 + _dot_hi_r(drb_ref[...], pick_ref[...])
        dlogf = _dot_hi_l(triu_ref[...], dc8) + carry_ref[...]
        carry_ref[...] = dlogf[0:1, :]
        df = dlogf * _sigmoid(-(fz_ref[...] + bf_ref[...]))
        dproj_ref[:, O_F:O_F + 128] = df.astype(BF16)
        sm_ref[2:3, :] += jnp.sum(df, axis=0, keepdims=True)

        du1 = _dot_nt(dproj_ref[...], win_ref[...])
        xv = x_ref[...]
        r = lax.rsqrt(jnp.mean(xv * xv, axis=-1, keepdims=True) + EPS)
        n1 = xv * r
        t = du1 * g1_ref[...]
        dx_ref[...] = dx2_ref[...] + r * (t - n1 * jnp.mean(t * n1, axis=-1, keepdims=True))
        dg1_ref[0:1, :] += jnp.sum(du1 * n1, axis=0, keepdims=True)

        @pl.when(i == nb - 1)
        def _():
            sm_ref[0:2, :] = _dot_hi_r(gsum_ref[0:8, :], fold_ref[...])[0:2, :]

    rev = lambda w, cb=0: pl.BlockSpec((TM, w), lambda i: (nb - 1 - i, cb))
    full = lambda a: pl.BlockSpec(a.shape, lambda i: (0,) * a.ndim)
    bd, fold, triu = _head_blockdiag(), _head_fold(), _tril(TM).T
    consts = (g1, win, qg, kg, bf, bd, fold, triu, _head_pick())
    return pl.pallas_call(
        body, name="mixer_in_bwd", grid=(nb,),
        in_specs=[rev(D_MODEL), rev(D_MODEL), rev(ATT_W, O_Q // ATT_W), rev(ATT_W, O_K // ATT_W), rev(128, O_F // 128),
                  rev(ATT_W), rev(ATT_W), rev(ATT_W), rev(2 * CONV_CH),
                  pl.BlockSpec((ATT_HEADS, TM), lambda i: (0, nb - 1 - i)), rev(ATT_W)] + [full(a) for a in consts],
        out_specs=[rev(N_INP), rev(D_MODEL), pl.BlockSpec((8, D_MODEL), lambda i: (0, 0)),
                   pl.BlockSpec((8, 128), lambda i: (0, 0))],
        out_shape=[jax.ShapeDtypeStruct((S, N_INP), BF16), jax.ShapeDtypeStruct((S, D_MODEL), F32),
                   jax.ShapeDtypeStruct((8, D_MODEL), F32), jax.ShapeDtypeStruct((8, 128), F32)],
        scratch_shapes=[pltpu.VMEM((1, 128), F32), pltpu.VMEM((8, ATT_W), F32)],
        compiler_params=_params(1),
    )(x, dx2, proj, proj, proj, dq, dk, dv, dag, dct, drb, *consts)


def _layer_fwd(x, early, late, target=None):
    p = early(x)
    u1, proj, q, k, v, qa, ka, h0, h1, h3 = _mixer_in_fwd(
        x, p["g1"], p["win"], p["qg"], p["kg"], p["bf"], p["cw"], p["cvb"], p["lng"], p["lnb"])
    att, lse = _attn_fwd(q, qa, k, ka, v)
    p = dict(p, **late(att))
    x2, u2 = _mixer_out_fwd(x, att, h3, p["wo"], p["g2"])
    x3, z, hh, *loss_acc = _mlp_fwd(x2, u2, p["w1"], p["w2"], target)
    saved = dict(x=x, u1=u1, proj=proj, q=q, k=k, v=v, qa=qa, ka=ka, h0=h0, h1=h1, h3=h3, att=att, lse=lse,
                 x2=x2, u2=u2, z=z, hh=hh)
    return (x3 if target is None else (x3, loss_acc[0])), saved, p


def _tie(a, token):
    return a if token is None else a + token[0:1, 0:1].astype(a.dtype).reshape((1,) * a.ndim)


def _layer_bwd(dx3, s, p, reduce):
    dz, dx2, dg2 = _mlp_bwd(dx3, s["z"], s["x2"], p["g2"], p["w1"], p["w2"])
    g_w2 = _matmul_tn(s["hh"], dx3)
    g_w1 = _matmul_tn(s["u2"], dz, col_shards=N_CHIPS)
    token = reduce("a", {2: g_w1, 3: g_w2.reshape(N_CHIPS, D_FF // N_CHIPS, D_MODEL)})
    datt, dlt, dh1, sm_c = _mixer_out_bwd(dx2, p["wo"], s["att"], s["h1"], _tie(p["lng"], token), p["lnb"])
    g_wo = jnp.concatenate([_matmul_tn(s["att"], dx2)[0], _matmul_tn(s["h3"], dx2)[0]], axis=0)
    dag, dcw = _conv_glu_bwd(dh1, s["h0"], s["proj"], p["cw"])
    dq, dk, dv, dc4, drb = _attn_bwd(s["q"], s["qa"], s["k"], s["ka"], s["v"], datt, s["lse"],
                                     dlt.reshape(ATT_HEADS // 2, 2, dlt.shape[1]))
    dct = dc4.reshape(ATT_HEADS, dc4.shape[2])
    dproj, dx, dg1, sm_a = _mixer_in_bwd(s["x"], dx2, s["proj"], dq, dk, dv, dag, dct, drb,
                                         p["g1"], p["win"], p["qg"], p["kg"], p["bf"])
    g_win = _win_to_global(_matmul_tn(s["u1"], dproj)[0])
    g_win = g_win.reshape(D_MODEL, N_CHIPS, N_IN // N_CHIPS).transpose(1, 0, 2)
    token = reduce("b", {0: g_win, 1: g_wo.reshape(N_CHIPS, D_MODEL // N_CHIPS, D_MODEL)})
    small = dict(g1=dg1[0], g2=dg2[0], lng=sm_c[0], lnb=sm_c[1], cvb=sm_c[2], cw=dcw[0:CONV_TAPS],
                 qg=sm_a[0, 0:HEAD_DIM], kg=sm_a[1, 0:HEAD_DIM], bf=sm_a[2, 0:ATT_HEADS])
    return dx, small, token


def _local_step(x, target, weights, reduce):
    saved, layers = [], []
    h = x
    for l, (early, late) in enumerate(weights):
        h, s, p = _layer_fwd(h, early, late, target if l == len(weights) - 1 else None)
        saved.append(s)
        layers.append(p)
    dy, loss_acc = h
    loss = loss_acc[0, 0] * (0.5 / D_MODEL)
    smalls = []
    d, token = dy, None
    for l in reversed(range(len(layers))):
        d, small, token = _layer_bwd(d, saved[l], dict(layers[l], g2=_tie(layers[l]["g2"], token)), reduce(l))
        smalls.append(small)
    return loss, d, smalls[::-1]


def _win_to_internal(w):
    pad = jnp.zeros(w.shape[:-1] + (N_INP - N_IN,), w.dtype)
    return jnp.concatenate([w[..., :1536], w[..., 1544:], w[..., 1536:1544], pad], axis=-1)


def _win_to_global(g):
    return jnp.concatenate([g[..., :1536], g[..., O_F:O_F + ATT_HEADS], g[..., 1536:O_F]], axis=-1)


def _layer_params(l, win, cw, norm1_g, b_f, q_norm_g, k_norm_g, conv_b, conv_ln_g, conv_ln_b, norm2_g):
    row = lambda a: a.reshape(1, -1)
    return dict(
        win=win, cw=jnp.pad(cw, ((0, CONV_HALO - CONV_TAPS), (0, 0))),
        g1=row(norm1_g[l]), g2=row(norm2_g[l]),
        qg=row(jnp.tile(q_norm_g[l], ATT_HEADS)), kg=row(jnp.tile(k_norm_g[l], ATT_HEADS)),
        bf=row(jnp.pad(b_f[l], (0, 128 - ATT_HEADS))),
        cvb=row(conv_b[l]), lng=row(conv_ln_g[l]), lnb=row(conv_ln_b[l]))


def _place():
    x, y, c = lax.axis_index("x"), lax.axis_index("y"), lax.axis_index("c")
    chips = [(1 - x, y), (x, 1 - y), (1 - x, 1 - y)]
    return x, y, c, chips


HBM = pl.BlockSpec(memory_space=pltpu.HBM)
SEM = pl.BlockSpec(memory_space=pltpu.SEMAPHORE)
GATHER_PEERS = N_CHIPS


def _gather_peers():
    x, y, c, chips = _place()
    return [(*chip, c) for chip in chips] + [(x, y, 1 - c)], [2 * px + py for px, py in chips] + [2 * x + y]


def _gather_start(srcs):
    n = len(srcs)

    def body(*refs):
        ins, lands = refs[:n], refs[n:2 * n]
        send_sems, recv_sems, token = refs[2 * n], refs[2 * n + 1], refs[-1]
        me = 2 * lax.axis_index("x") + lax.axis_index("y")
        peers, _ = _gather_peers()
        for g in range(n):
            for j, to in enumerate(peers):
                pltpu.make_async_remote_copy(src_ref=ins[g], dst_ref=lands[g].at[me],
                                             send_sem=send_sems.at[GATHER_PEERS * g + j],
                                             recv_sem=recv_sems.at[GATHER_PEERS * g + j],
                                             device_id=to, device_id_type=MESH).start()
        token[...] = jnp.zeros_like(token)

    lands = [lax.empty((N_CHIPS,) + a.shape, a.dtype) for a in srcs]
    outs = pl.pallas_call(
        body, name="gather_start",
        in_specs=[HBM] * (2 * n),
        out_specs=[SEM, SEM] + [HBM] * (2 * n) + [pl.BlockSpec(memory_space=pltpu.VMEM)],
        out_shape=[pltpu.SemaphoreType.DMA((GATHER_PEERS * n,)), pltpu.SemaphoreType.DMA((GATHER_PEERS * n,))]
        + [pltpu.HBM(a.shape, a.dtype) for a in srcs] + [pltpu.HBM(a.shape, a.dtype) for a in lands]
        + [jax.ShapeDtypeStruct((8, 128), F32)],
        input_output_aliases={i: 2 + i for i in range(2 * n)},
        compiler_params=pltpu.CompilerParams(has_side_effects=pltpu.SideEffectType.DATAFLOW_SIDE_EFFECTING),
    )(*[pltpu.with_memory_space_constraint(a, pltpu.HBM) for a in srcs],
      *[pltpu.with_memory_space_constraint(a, pltpu.HBM) for a in lands])
    return outs[0], outs[1], list(outs[2:2 + n]), list(outs[2 + n:2 + 2 * n]), outs[-1]


def _gather_wait(name, groups, send_sems, recv_sems, srcs, lands, after):
    k = len(groups)

    def body(*refs):
        ins, lnd = refs[:k], refs[k:2 * k]
        ssem, rsem = refs[2 * k], refs[2 * k + 1]
        peers, slots = _gather_peers()
        for i, g in enumerate(groups):
            for j, to in enumerate(peers):
                cp = pltpu.make_async_remote_copy(src_ref=ins[i], dst_ref=lnd[i].at[slots[j]],
                                                  send_sem=ssem.at[GATHER_PEERS * g + j],
                                                  recv_sem=rsem.at[GATHER_PEERS * g + j],
                                                  device_id=to, device_id_type=MESH)
                cp.wait_send()
                cp.wait_recv()

    outs = pl.pallas_call(
        body, name=name,
        in_specs=[HBM] * (2 * k) + [SEM, SEM, ANY],
        out_specs=[HBM] * (2 * k),
        out_shape=[pltpu.HBM(a.shape, a.dtype) for a in srcs] + [pltpu.HBM(a.shape, a.dtype) for a in lands],
        input_output_aliases={i: i for i in range(2 * k)},
        compiler_params=pltpu.CompilerParams(has_side_effects=pltpu.SideEffectType.DATAFLOW_SIDE_EFFECTING),
    )(*srcs, *lands, send_sems, recv_sems, after)
    return list(outs[k:])


REDUCE_STEPS = 8


def _row_half(a):
    return a.shape[-2] // 2


def _half_swap(name, gs):
    n = len(gs)

    def body(*refs):
        ins, outs = refs[:n], refs[n:2 * n]
        send_sems, recv_sems = refs[2 * n:]
        x, y, c, _ = _place()
        copies = []
        for a in range(n):
            h = _row_half(gs[a])
            copies.append(pltpu.make_async_remote_copy(
                src_ref=ins[a].at[:, pl.ds((1 - c) * h, h), :], dst_ref=outs[a], send_sem=send_sems.at[a],
                recv_sem=recv_sems.at[a], device_id=(x, y, 1 - c), device_id_type=MESH))
        for cp in copies:
            cp.start()
        for cp in copies:
            cp.wait()

    return pl.pallas_call(
        body, name=name,
        in_specs=[ANY] * n, out_specs=[ANY] * n,
        out_shape=[jax.ShapeDtypeStruct((N_CHIPS, _row_half(a), a.shape[-1]), a.dtype) for a in gs],
        scratch_shapes=[pltpu.SemaphoreType.DMA((n,)), pltpu.SemaphoreType.DMA((n,))],
        compiler_params=pltpu.CompilerParams(has_side_effects=True),
    )(*gs)


def _half_specs(gs, row_block):
    tiles = [_row_half(a) // REDUCE_STEPS for a in gs]
    return [pl.BlockSpec((N_CHIPS, t, a.shape[-1]), lambda i, p: (0, row_block(i, p), 0)) for a, t in zip(gs, tiles)]


def _half_add(name, gs, got, place):
    n = len(gs)

    def body(place_ref, *refs):
        own, theirs, outs = refs[:n], refs[n:2 * n], refs[2 * n:]
        for a in range(n):
            outs[a][...] = (own[a][...].astype(F32) + theirs[a][...].astype(F32)).astype(BF16)

    plain = _half_specs(gs, lambda i, p: i)
    return pl.pallas_call(
        body, name=name,
        grid_spec=pltpu.PrefetchScalarGridSpec(
            num_scalar_prefetch=1, grid=(REDUCE_STEPS,),
            in_specs=_half_specs(gs, lambda i, p: p[1] * REDUCE_STEPS + i) + plain, out_specs=plain),
        out_shape=[jax.ShapeDtypeStruct(a.shape, BF16) for a in got],
        compiler_params=_params(1),
    )(place, *gs, *got)


def _exchange_copies(parts, lands, send_sems, recv_sems):
    x, y, c, chips = _place()
    return [pltpu.make_async_remote_copy(src_ref=parts[a].at[2 * px + py], dst_ref=lands[a].at[j],
                                         send_sem=send_sems.at[3 * a + j], recv_sem=recv_sems.at[3 * a + j],
                                         device_id=(px, py, c), device_id_type=MESH)
            for a in range(len(parts)) for j, (px, py) in enumerate(chips)]


def _exchange_start(name, parts):
    n = len(parts)

    def body(*refs):
        _ = [cp.start() for cp in _exchange_copies(refs[:n], refs[n:2 * n], refs[2 * n], refs[2 * n + 1])]
        refs[-1][...] = jnp.zeros_like(refs[-1])

    lands = [lax.empty((N_CHIPS - 1,) + a.shape[1:], a.dtype) for a in parts]
    outs = pl.pallas_call(
        body, name=name,
        in_specs=[HBM] * (2 * n),
        out_specs=[SEM, SEM] + [HBM] * (2 * n) + [pl.BlockSpec(memory_space=pltpu.VMEM)],
        out_shape=[pltpu.SemaphoreType.DMA((3 * n,)), pltpu.SemaphoreType.DMA((3 * n,))]
        + [pltpu.HBM(a.shape, a.dtype) for a in parts] + [pltpu.HBM(a.shape, a.dtype) for a in lands]
        + [jax.ShapeDtypeStruct((8, 128), F32)],
        input_output_aliases={i: 2 + i for i in range(2 * n)},
        compiler_params=pltpu.CompilerParams(has_side_effects=pltpu.SideEffectType.DATAFLOW_SIDE_EFFECTING),
    )(*[pltpu.with_memory_space_constraint(a, pltpu.HBM) for a in parts],
      *[pltpu.with_memory_space_constraint(a, pltpu.HBM) for a in lands])
    return outs[0], outs[1], list(outs[2:2 + n]), list(outs[2 + n:2 + 2 * n]), outs[-1]


def _exchange_wait(name, send_sems, recv_sems, parts, lands, after):
    n = len(parts)

    def body(*refs):
        for cp in _exchange_copies(refs[:n], refs[n:2 * n], refs[2 * n], refs[2 * n + 1]):
            cp.wait_send()
            cp.wait_recv()

    outs = pl.pallas_call(
        body, name=name,
        in_specs=[HBM] * (2 * n) + [SEM, SEM, ANY],
        out_specs=[HBM] * (2 * n),
        out_shape=[pltpu.HBM(a.shape, a.dtype) for a in parts] + [pltpu.HBM(a.shape, a.dtype) for a in lands],
        input_output_aliases={i: i for i in range(2 * n)},
        compiler_params=pltpu.CompilerParams(has_side_effects=pltpu.SideEffectType.DATAFLOW_SIDE_EFFECTING),
    )(*parts, *lands, send_sems, recv_sems, after)
    return list(outs[:n]), list(outs[n:])


def _chip_sum(name, parts, lands, sums, place, layer):
    n = len(parts)
    tiles = [a.shape[-2] // REDUCE_STEPS for a in parts]

    def body(place_ref, *refs):
        own, got, outs = refs[:n], refs[n:2 * n], refs[3 * n:]
        for a in range(n):
            tot = own[a][...].astype(F32)
            for j in range(N_CHIPS - 1):
                tot = tot + got[a][j].astype(F32)
            outs[a][...] = tot

    own_specs = [pl.BlockSpec((None, t, a.shape[-1]), lambda i, p: (p[0], i, 0)) for a, t in zip(parts, tiles)]
    got_specs = [pl.BlockSpec((N_CHIPS - 1, t, a.shape[-1]), lambda i, p: (0, i, 0)) for a, t in zip(parts, tiles)]
    out_specs = [pl.BlockSpec((None, t, a.shape[-1]), lambda i, p: (layer, p[1] * REDUCE_STEPS + i, 0))
                 for a, t in zip(parts, tiles)]
    return pl.pallas_call(
        body, name=name,
        grid_spec=pltpu.PrefetchScalarGridSpec(num_scalar_prefetch=1, grid=(REDUCE_STEPS,),
                                               in_specs=own_specs + got_specs + [ANY] * n, out_specs=out_specs),
        out_shape=[jax.ShapeDtypeStruct(a.shape, F32) for a in sums],
        input_output_aliases={1 + 2 * n + a: a for a in range(n)},
        compiler_params=_params(1),
    )(place, *parts, *lands, *sums)


def _half_fill(name, sums, layer):
    n = len(sums)

    def body(*refs):
        ins, outs = refs[:n], refs[n:2 * n]
        send_sems, recv_sems = refs[2 * n:]
        x, y, c, _ = _place()
        copies = []
        for a in range(n):
            h = _row_half(sums[a])
            copies.append(pltpu.make_async_remote_copy(
                src_ref=ins[a].at[layer, pl.ds(c * h, h), :], dst_ref=outs[a].at[layer, pl.ds(c * h, h), :],
                send_sem=send_sems.at[a], recv_sem=recv_sems.at[a], device_id=(x, y, 1 - c), device_id_type=MESH))
        for cp in copies:
            cp.start()
        for a in range(n):
            h = _row_half(sums[a])
            theirs = outs[a].at[layer, pl.ds((1 - c) * h, h), :]
            pltpu.make_async_remote_copy(src_ref=theirs, dst_ref=theirs, send_sem=send_sems.at[a], recv_sem=recv_sems.at[a],
                                         device_id=(x, y, 1 - c), device_id_type=MESH).wait_recv()
        for cp in copies:
            cp.wait_send()

    return pl.pallas_call(
        body, name=name,
        in_specs=[ANY] * n, out_specs=[ANY] * n,
        out_shape=[jax.ShapeDtypeStruct(a.shape, a.dtype) for a in sums],
        input_output_aliases={a: a for a in range(n)},
        scratch_shapes=[pltpu.SemaphoreType.DMA((n,)), pltpu.SemaphoreType.DMA((n,))],
        compiler_params=pltpu.CompilerParams(has_side_effects=True),
    )(*sums)


def _adamw_math(w, g, m, v):
    m = ADAM_B1 * m + (1.0 - ADAM_B1) * g
    v = ADAM_B2 * v + (1.0 - ADAM_B2) * (g * g)
    m_hat = m / (1.0 - ADAM_B1 ** ADAM_STEP)
    v_hat = v / (1.0 - ADAM_B2 ** ADAM_STEP)
    delta = -ADAM_LR * (m_hat / (jnp.sqrt(v_hat) + ADAM_EPS) + ADAM_WD * w)
    return delta, m, v


def _adamw(name, ws, gs, ms, vs, layer, prev=None):
    n = len(ws)
    steps = 8
    tiles = [a.shape[-2] // steps for a in ws]

    def body(*refs):
        w_r, g_r, m_r, v_r = refs[:n], refs[n:2 * n], refs[2 * n:3 * n], refs[3 * n:4 * n]
        g_o, d_o, m_o, v_o = (refs[-4 * n:][k * n:(k + 1) * n] for k in range(4))
        for a in range(n):
            g = g_r[a][...]
            d, m, v = _adamw_math(w_r[a][...], g, m_r[a][...], v_r[a][...])
            g_o[a][...] = g
            d_o[a][...] = d
            m_o[a][...] = m
            v_o[a][...] = v

    specs = [pl.BlockSpec((None, t, a.shape[-1]), lambda i: (layer, i, 0)) for a, t in zip(ws, tiles)]
    held = [] if prev is None else [buf for kind in zip(*prev) for buf in kind]
    outs = pl.pallas_call(
        body, name=name, grid=(steps,),
        in_specs=specs * 4 + [ANY] * len(held), out_specs=specs * 4,
        out_shape=[jax.ShapeDtypeStruct(a.shape, F32) for a in ws] * 4,
        input_output_aliases={4 * n + k: k for k in range(len(held))},
        compiler_params=_params(1),
    )(*ws, *gs, *ms, *vs, *held)
    return [[outs[k * n + a] for k in range(4)] for a in range(n)]


SMALL_W = 512


def _small_allreduce_adamw(g, w, m, v, cw_w, cw_m, cw_v, cw_row0):
    R = g.shape[0]
    n_l = cw_w.shape[0]

    def body(g_ref, w_ref, m_ref, v_ref, cww_ref, cwm_ref, cwv_ref,
             gs_ref, d_ref, mo_ref, vo_ref, cg_ref, cd_ref, cmo_ref, cvo_ref,
             slots_ref, send_sems, recv_sems):
        x, y, c, _ = _place()
        me = 4 * x + 2 * y + c
        slots_ref[me] = g_ref[...]
        sends = []
        for d in range(1, 8):
            px, py, pc = x ^ (d >> 2), y ^ ((d >> 1) & 1), c ^ (d & 1)
            cp = pltpu.make_async_remote_copy(src_ref=g_ref, dst_ref=slots_ref.at[me], send_sem=send_sems.at[d - 1],
                                              recv_sem=recv_sems.at[d - 1], device_id=(px, py, pc), device_id_type=MESH)
            cp.start()
            sends.append(cp)
        for d in range(1, 8):
            px, py, pc = x ^ (d >> 2), y ^ ((d >> 1) & 1), c ^ (d & 1)
            slot = slots_ref.at[4 * px + 2 * py + pc]
            pltpu.make_async_remote_copy(src_ref=slot, dst_ref=slot, send_sem=send_sems.at[d - 1],
                                         recv_sem=recv_sems.at[d - 1], device_id=(px, py, pc),
                                         device_id_type=MESH).wait_recv()
        for cp in sends:
            cp.wait_send()
        tot = slots_ref[0]
        for k in range(1, 8):
            tot = tot + slots_ref[k]
        gs_ref[...] = tot
        dl, mn, vn = _adamw_math(w_ref[...], tot, m_ref[...], v_ref[...])
        d_ref[...] = dl
        mo_ref[...] = mn
        vo_ref[...] = vn
        chip = 2 * x + y
        for l in range(n_l):
            rows = tot[cw_row0[l]:cw_row0[l] + CONV_HALO, :]
            mine = rows[:, 0:128]
            for k in range(1, N_CHIPS):
                mine = jnp.where(chip == k, rows[:, 128 * k:128 * (k + 1)], mine)
            cg_ref[l] = mine
            dl, mn, vn = _adamw_math(cww_ref[l], mine, cwm_ref[l], cwv_ref[l])
            cd_ref[l] = dl
            cmo_ref[l] = mn
            cvo_ref[l] = vn

    vm = pl.BlockSpec(memory_space=pltpu.VMEM)
    small = jax.ShapeDtypeStruct((R, SMALL_W), F32)
    conv = jax.ShapeDtypeStruct(cw_w.shape, F32)
    return pl.pallas_call(
        body, name="small_allreduce_adamw",
        in_specs=[vm] * 7, out_specs=[vm] * 8,
        out_shape=[small] * 4 + [conv] * 4,
        scratch_shapes=[pltpu.VMEM((8, R, SMALL_W), F32), pltpu.SemaphoreType.DMA((7,)), pltpu.SemaphoreType.DMA((7,))],
        compiler_params=pltpu.CompilerParams(has_side_effects=True, vmem_limit_bytes=VMEM_LIMIT),
    )(g, w, m, v, cw_w, cw_m, cw_v)


SMALL_LAYOUT = (("conv_w", CONV_HALO), ("norm1_g", 2), ("norm2_g", 2), ("conv_b", 1), ("conv_ln_g", 1),
                ("conv_ln_b", 1), ("q_norm_g", 1), ("k_norm_g", 1), ("b_f", 1))
SMALL_ROWS = sum(r for _, r in SMALL_LAYOUT)
SMALL_ROWS_PAD = 48
LOSS_ROW = SMALL_ROWS


def _pack_small(per_layer):
    flat = []
    for d in per_layer:
        for name, r in SMALL_LAYOUT:
            n = r * SMALL_W
            a = d.get(name)
            if a is None:
                flat.append(jnp.zeros((n,), F32))
                continue
            flat.append(a.reshape(-1))
            if a.size < n:
                flat.append(jnp.zeros((n - a.size,), F32))
        spare = (SMALL_ROWS_PAD - SMALL_ROWS) * SMALL_W
        if "spare" in d:
            flat.append(d["spare"].reshape(-1))
            spare -= d["spare"].size
        flat.append(jnp.zeros((spare,), F32))
    return jnp.concatenate(flat).reshape(-1, SMALL_W)


def _unpack_small(packed, name, size):
    row0 = 0
    for nm, r in SMALL_LAYOUT:
        if nm == name:
            break
        row0 += r
    per_layer = packed.reshape(-1, SMALL_ROWS_PAD * SMALL_W)
    return per_layer[:, row0 * SMALL_W:row0 * SMALL_W + size]


SMALL_SIZES = dict(norm1_g=D_MODEL, norm2_g=D_MODEL, conv_b=CONV_CH, conv_ln_g=CONV_CH, conv_ln_b=CONV_CH,
                   q_norm_g=HEAD_DIM, k_norm_g=HEAD_DIM, b_f=ATT_HEADS)
SMALL_KEYS = dict(norm1_g="g1", norm2_g="g2", conv_b="cvb", conv_ln_g="lng", conv_ln_b="lnb",
                  q_norm_g="qg", k_norm_g="kg", b_f="bf", conv_w="cw")
CONV_W_ROW0 = 0


def kernel(x, norm1_g, w_in, b_f, q_norm_g, k_norm_g, conv_w, conv_b, conv_ln_g, conv_ln_b, w_o, norm2_g, w_mlp_in, w_mlp_out, loss_target, m_norm1_g, m_w_in, m_b_f, m_q_norm_g, m_k_norm_g, m_conv_w, m_conv_b, m_conv_ln_g, m_conv_ln_b, m_w_o, m_norm2_g, m_w_mlp_in, m_w_mlp_out, v_norm1_g, v_w_in, v_b_f, v_q_norm_g, v_k_norm_g, v_conv_w, v_conv_b, v_conv_ln_g, v_conv_ln_b, v_w_o, v_norm2_g, v_w_mlp_in, v_w_mlp_out):
    n_l = w_in.shape[0]

    per_layer = lambda l: [w_in[l].astype(BF16), conv_w[l], w_o[l].astype(BF16), w_mlp_in[l].astype(BF16),
                           w_mlp_out[l].astype(BF16)]
    n_w = len(per_layer(0))
    send_sems, recv_sems, srcs, lands, token = _gather_start([a for l in range(n_l) for a in per_layer(l)])

    def layer_weights(l):
        def wait(tag, which, after):
            groups = [n_w * l + i for i in which]
            return _gather_wait(f"gather_wait_{tag}{l}", groups, send_sems, recv_sems,
                                [srcs[g] for g in groups], [lands[g] for g in groups], after)

        def early(after):
            g_in, g_cw = wait("a", (0, 1), token if l == 0 else after)
            win = _win_to_internal(jnp.concatenate([g_in[k] for k in range(N_CHIPS)], axis=-1))
            cw = jnp.concatenate([g_cw[k] for k in range(N_CHIPS)], axis=-1)
            return _layer_params(l, win, cw, norm1_g, b_f, q_norm_g, k_norm_g, conv_b, conv_ln_g, conv_ln_b, norm2_g)

        def late(after):
            wo, w1, w2 = wait("b", (2, 3, 4), after)
            return dict(wo=wo, w1=w1, w2=w2)

        return early, late

    place = jnp.stack([2 * lax.axis_index("x") + lax.axis_index("y"), lax.axis_index("c")]).astype(jnp.int32)
    big_w = [w_in, w_o, w_mlp_in, w_mlp_out]
    pending, started, last = [], [], []

    def start_group(l, tag, grads, token=None):
        which, gs = list(grads), [_tie(g, token) for g in grads.values()]
        got = _half_swap(f"half_swap_{tag}{l}", gs)
        parts = _half_add(f"half_add_{tag}{l}", gs, got, place)
        send, recv, parts, lands, token = _exchange_start(f"exchange_start_{tag}{l}", parts)
        pending.append((f"{tag}{l}", l, which, send, recv, parts, lands))
        started.append(token)
        return token

    def reduce(l):
        def group(tag, grads):
            if (l, tag) == (0, "b"):
                last.append(grads)
                return None
            return start_group(l, tag, grads)
        return group

    loss, dx, smalls = _local_step(x[0], loss_target[0], [layer_weights(l) for l in range(n_l)], reduce)
    env = dict(norm1_g=(norm1_g, m_norm1_g, v_norm1_g), norm2_g=(norm2_g, m_norm2_g, v_norm2_g),
               conv_b=(conv_b, m_conv_b, v_conv_b), conv_ln_g=(conv_ln_g, m_conv_ln_g, v_conv_ln_g),
               conv_ln_b=(conv_ln_b, m_conv_ln_b, v_conv_ln_b), q_norm_g=(q_norm_g, m_q_norm_g, v_q_norm_g),
               k_norm_g=(k_norm_g, m_k_norm_g, v_k_norm_g), b_f=(b_f, m_b_f, v_b_f))
    g_dicts = [{nm: s[key] for nm, key in SMALL_KEYS.items()} for s in smalls]
    g_dicts[0]["spare"] = loss
    g_pack = _pack_small(g_dicts)
    packs = [_pack_small([{nm: env[nm][t][l] for nm in env} for l in range(n_l)]) for t in range(3)]
    pad_cw = lambda a: jnp.pad(a, ((0, 0), (0, CONV_HALO - CONV_TAPS), (0, 0)))
    cw_row0 = tuple(l * SMALL_ROWS_PAD + CONV_W_ROW0 for l in range(n_l))
    gs, ds, ms, vs, cg, cd, cm, cv = _small_allreduce_adamw(
        g_pack, packs[0], packs[1], packs[2], pad_cw(conv_w), pad_cw(m_conv_w), pad_cw(v_conv_w), cw_row0)

    start_group(0, "b", last[0], gs[SMALL_ROWS_PAD - 1:])
    sums = [lax.empty(w.shape, F32) for w in big_w]
    big_m = [m_w_in, m_w_o, m_w_mlp_in, m_w_mlp_out]
    big_v = [v_w_in, v_w_o, v_w_mlp_in, v_w_mlp_out]
    updated = [None] * len(big_w)
    after = started[-1]
    for tag, l, which, send, recv, parts, lands in pending:
        parts, lands = _exchange_wait(f"exchange_wait_{tag}", send, recv, parts, lands, after)
        done = _chip_sum(f"chip_sum_{tag}", parts, lands, [sums[i] for i in which], place, l)
        done = _half_fill(f"half_fill_{tag}", done, l)
        for i, a in zip(which, done):
            sums[i] = a
        prev = None if updated[which[0]] is None else [updated[i] for i in which]
        new = _adamw(f"adamw_{tag}", [big_w[i] for i in which], done, [big_m[i] for i in which],
                     [big_v[i] for i in which], l, prev)
        for i, r in zip(which, new):
            updated[i] = r
        after = new[0][1]
    g_big, d_big, nm_big, nv_big = ([r[k] for r in updated] for k in range(4))

    def small_out(packed, conv):
        o = {nm: _unpack_small(packed, nm, sz) for nm, sz in SMALL_SIZES.items()}
        o["conv_w"] = conv[:, 0:CONV_TAPS, :]
        return o

    def ordered(small, big):
        return (small["norm1_g"], big[0], small["b_f"], small["q_norm_g"], small["k_norm_g"], small["conv_w"],
                small["conv_b"], small["conv_ln_g"], small["conv_ln_b"], big[1], small["norm2_g"], big[2], big[3])

    return (gs[LOSS_ROW, 0], dx[None],
            *ordered(small_out(gs, cg), g_big), *ordered(small_out(ds, cd), d_big),
            *ordered(small_out(ms, cm), nm_big), *ordered(small_out(vs, cv), nv_big))
```

```python
import jax
import jax.numpy as jnp
from jax import lax
from jax.experimental import pallas as pl
from jax.experimental.pallas import tpu as pltpu

F32 = jnp.float32
BF16 = jnp.bfloat16

D_MODEL = 1024
ATT_HEADS = 8
HEAD_DIM = 64
ATT_W = ATT_HEADS * HEAD_DIM
CONV_CH = 512
CONV_TAPS = 31
CONV_HALO = 32
D_FF = 4 * D_MODEL
N_IN = 3 * ATT_W + ATT_HEADS + 2 * CONV_CH
O_Q, O_K, O_V, O_A, O_G, O_F = 0, 512, 1024, 1536, 2048, 2560
N_INP = O_F + 128
EPS = 1e-6
QK_SCALE = 0.125

ADAM_LR = 0.001
ADAM_B1 = 0.9
ADAM_B2 = 0.999
ADAM_EPS = 1e-08
ADAM_WD = 0.01
ADAM_STEP = 10

N_CHIPS = 4
VMEM_LIMIT = 52 * 1024 * 1024
MESH = pl.DeviceIdType.MESH
ANY = pl.BlockSpec(memory_space=pl.ANY)


def _params(n_axes, **kw):
    return pltpu.CompilerParams(dimension_semantics=("arbitrary",) * n_axes,
                                vmem_limit_bytes=VMEM_LIMIT, **kw)


def _dot(a, b):
    return jnp.dot(a, b, preferred_element_type=F32)


def _dot_nt(a, b):
    return lax.dot_general(a, b, (((1,), (1,)), ((), ())), preferred_element_type=F32)


def _dot_tn(a, b):
    return lax.dot_general(a, b, (((0,), (0,)), ((), ())), preferred_element_type=F32)


def _split3(a):
    a1 = a.astype(BF16)
    r = a - a1.astype(F32)
    a2 = r.astype(BF16)
    a3 = (r - a2.astype(F32)).astype(BF16)
    return a1, a2, a3


def _dot_hi_r(a, b_exact):
    return sum(_dot(p, b_exact) for p in _split3(a))


def _head_sums(a, blockdiag):
    a1 = a.astype(BF16)
    a2 = (a - a1.astype(F32)).astype(BF16)
    return _dot(a1, blockdiag) + _dot(a2, blockdiag)


def _dot_hi_l(a_exact, b):
    return sum(_dot(a_exact, p) for p in _split3(b))


def _sigmoid(x):
    return 1.0 / (1.0 + jnp.exp(-x))


def _head_blockdiag():
    i = jnp.arange(ATT_W) // HEAD_DIM
    return (i[:, None] == i[None, :]).astype(BF16)


AUG_LANES = 8


def _aug_place(first):
    piece = jnp.arange(3 * 128)[:, None] // 128
    h = jnp.arange(3 * 128)[:, None] % 128
    lane = jnp.arange(ATT_W)[None, :]
    return ((h < ATT_HEADS) & (lane == 128 * (h // 2) + AUG_LANES * (h % 2) + first + piece)).astype(BF16)


def _aug_ones(first):
    lane = jnp.arange(ATT_W) % 128
    pos = lane % AUG_LANES
    return ((lane < 2 * AUG_LANES) & (pos >= first) & (pos < first + 3)).astype(F32).reshape(1, ATT_W)


def _head_rows():
    h = jnp.arange(2 * ATT_HEADS)[:, None]
    i = jnp.arange(ATT_W)[None, :] // HEAD_DIM
    return (h == i).astype(BF16)


def _head_fold():
    i = jnp.arange(ATT_W)[:, None] % HEAD_DIM
    j = jnp.arange(128)[None, :]
    return (i == j).astype(BF16)


def _head_pick():
    i = jnp.arange(ATT_W)[:, None]
    h = jnp.arange(128)[None, :]
    return (i == h * HEAD_DIM).astype(BF16)


def _tril(n):
    r = jnp.arange(n)
    return (r[:, None] >= r[None, :]).astype(BF16)


SUBLANES = 8


def _fill_row_shifts(buf_ref, shifts_ref, tm):
    n = tm + CONV_HALO - SUBLANES
    for b in range(1, SUBLANES):
        shifts_ref[b - 1, 0:n, :] = buf_ref[pl.ds(b, n), :]


def _row_shifted(buf_ref, shifts_ref, offset, rows, base=0):
    a, b = divmod(offset, SUBLANES)
    start = pl.multiple_of(base + SUBLANES * a, SUBLANES)
    if b == 0:
        return buf_ref[pl.ds(start, rows), :]
    return shifts_ref[b - 1, pl.ds(start, rows), :]


CONV_ROWS = 32


def _mixer_in_fwd(x, g1, win, qg, kg, bf, cw, cvb, lng, lnb):
    S = x.shape[0]
    TM = min(512, S)
    nb = S // TM

    def body(x_ref, g1_ref, win_ref, qg_ref, kg_ref, bf_ref, cw_ref, cvb_ref, lng_ref, lnb_ref,
             bd_ref, tri_ref, pq_ref, pk_ref, oq_ref, ok_ref,
             u1_ref, proj_ref, q_ref, k_ref, v_ref, qa_ref, ka_ref, h0_ref, h1_ref, h3_ref,
             carry_ref, hbuf_ref, hs_ref):
        i = pl.program_id(0)

        @pl.when(i == 0)
        def _():
            carry_ref[...] = jnp.zeros_like(carry_ref)
            hbuf_ref[0:CONV_HALO, :] = jnp.zeros((CONV_HALO, CONV_CH), F32)

        @pl.when(i > 0)
        def _():
            hbuf_ref[0:CONV_HALO, :] = hbuf_ref[TM:TM + CONV_HALO, :]

        xv = x_ref[...]
        r = lax.rsqrt(jnp.mean(xv * xv, axis=-1, keepdims=True) + EPS)
        u = (xv * r * g1_ref[...]).astype(BF16)
        u1_ref[...] = u
        proj_ref[...] = _dot(u, win_ref[...])

        def headnorm(raw, gain):
            ss = _head_sums(raw * raw, bd_ref[...]) * (1.0 / HEAD_DIM)
            return raw * lax.rsqrt(ss + EPS) * gain

        q_ref[...] = (headnorm(proj_ref[:, O_Q:O_Q + ATT_W], qg_ref[...]) * QK_SCALE).astype(BF16)
        k_ref[...] = headnorm(proj_ref[:, O_K:O_K + ATT_W], kg_ref[...]).astype(BF16)
        v_ref[...] = proj_ref[:, O_V:O_V + ATT_W].astype(BF16)

        zf = proj_ref[:, O_F:O_F + 128] + bf_ref[...]
        logf = jnp.minimum(zf, 0.0) - jnp.log(1.0 + jnp.exp(-jnp.abs(zf)))
        lane = lax.broadcasted_iota(jnp.int32, (TM, 128), 1)
        logf = jnp.where(lane < ATT_HEADS, logf, 0.0)
        c8 = _dot_hi_l(tri_ref[...], logf) + carry_ref[...]
        carry_ref[...] = c8[TM - 1:TM, :]
        pieces = jnp.concatenate(_split3(c8), axis=1)
        qa_ref[...] = (_dot(pieces, pq_ref[...]) + oq_ref[...]).astype(BF16)
        ka_ref[...] = (ok_ref[...] - _dot(pieces, pk_ref[...])).astype(BF16)

        h0 = proj_ref[:, O_A:O_A + CONV_CH] * _sigmoid(proj_ref[:, O_G:O_G + CONV_CH])
        h0_ref[...] = h0
        hbuf_ref[CONV_HALO:CONV_HALO + TM, :] = h0
        _fill_row_shifts(hbuf_ref, hs_ref, TM)
        acc = jnp.zeros((TM, CONV_CH), F32) + cvb_ref[...]
        for j in range(CONV_TAPS):
            acc = acc + cw_ref[j:j + 1, :] * _row_shifted(hbuf_ref, hs_ref, CONV_HALO - CONV_TAPS + 1 + j, TM)
        h1_ref[...] = acc
        mu = jnp.mean(acc, axis=-1, keepdims=True)
        d = acc - mu
        var = jnp.mean(d * d, axis=-1, keepdims=True)
        h2 = d * lax.rsqrt(var + EPS) * lng_ref[...] + lnb_ref[...]
        h3_ref[...] = (h2 * _sigmoid(h2)).astype(BF16)

    row = lambda w: pl.BlockSpec((TM, w), lambda i: (i, 0))
    full = lambda a: pl.BlockSpec(a.shape, lambda i: (0,) * a.ndim)
    ins = (x, g1, win, qg, kg, bf, cw, cvb, lng, lnb, _head_blockdiag(), _tril(TM),
           _aug_place(0), _aug_place(3), _aug_ones(3), _aug_ones(0))
    return pl.pallas_call(
        body, name="mixer_in_fwd", grid=(nb,),
        in_specs=[row(D_MODEL)] + [full(a) for a in ins[1:]],
        out_specs=[row(D_MODEL), row(N_INP), row(ATT_W), row(ATT_W), row(ATT_W), row(ATT_W), row(ATT_W),
                   row(CONV_CH), row(CONV_CH), row(CONV_CH)],
        out_shape=[jax.ShapeDtypeStruct((S, D_MODEL), BF16),
                   jax.ShapeDtypeStruct((S, N_INP), F32),
                   jax.ShapeDtypeStruct((S, ATT_W), BF16),
                   jax.ShapeDtypeStruct((S, ATT_W), BF16),
                   jax.ShapeDtypeStruct((S, ATT_W), BF16),
                   jax.ShapeDtypeStruct((S, ATT_W), BF16),
                   jax.ShapeDtypeStruct((S, ATT_W), BF16),
                   jax.ShapeDtypeStruct((S, CONV_CH), F32),
                   jax.ShapeDtypeStruct((S, CONV_CH), F32),
                   jax.ShapeDtypeStruct((S, CONV_CH), BF16)],
        scratch_shapes=[pltpu.VMEM((1, 128), F32), pltpu.VMEM((TM + CONV_HALO, CONV_CH), F32),
                        pltpu.VMEM((SUBLANES - 1, TM + CONV_HALO, CONV_CH), F32)],
        compiler_params=_params(1),
    )(*ins)


def _pair_heads(lo, alo, x, xa):
    z = jnp.zeros_like(x)
    return (jnp.concatenate([jnp.where(lo, x, z), jnp.where(alo, xa, z)], axis=1),
            jnp.concatenate([jnp.where(lo, z, x), jnp.where(alo, z, xa)], axis=1))


def _attn_fwd(q, qa, k, ka, v):
    S = q.shape[0]
    T = min(1024, S)
    nq = S // T

    def body(q_ref, qa_ref, k_ref, ka_ref, v_ref, o_ref, lse_ref):
        qi = pl.program_id(1)
        lane = lax.broadcasted_iota(jnp.int32, (T, 128), 1)
        lo = lane < HEAD_DIM
        qm = _pair_heads(lo, lane < AUG_LANES, q_ref[...], qa_ref[...])
        tril = (lax.broadcasted_iota(jnp.int32, (T, T), 0) >= lax.broadcasted_iota(jnp.int32, (T, T), 1))

        def step(kj, carry, masked):
            off = pl.multiple_of(kj * T, T)
            kb = jnp.concatenate([k_ref[pl.ds(off, T), :], ka_ref[pl.ds(off, T), :]], axis=1)
            vb = v_ref[pl.ds(off, T), :]
            new = []
            for h in range(2):
                m, l, acc = carry[3 * h:3 * h + 3]
                s = _dot_nt(qm[h], kb)
                if masked:
                    s = jnp.where(tril, s, -1e30)
                m_new = jnp.maximum(m, jnp.max(s, axis=-1, keepdims=True))
                alpha = jnp.exp(m - m_new)
                p = jnp.exp(s - m_new)
                l = alpha * l + jnp.sum(p, axis=-1, keepdims=True)
                acc = alpha * acc + _dot(p.astype(BF16), vb)
                new += [m_new, l, acc]
            return tuple(new)

        init = (jnp.full((T, 1), -1e30, F32), jnp.zeros((T, 1), F32), jnp.zeros((T, 128), F32)) * 2
        carry = lax.fori_loop(0, qi, lambda kj, c: step(kj, c, False), init)
        m0, l0, a0, m1, l1, a1 = step(qi, carry, True)
        o_ref[...] = jnp.where(lo, a0 / l0, a1 / l1).astype(BF16)
        lse_t = jnp.where(lo, m0 + jnp.log(l0), m1 + jnp.log(l1)).T
        lse_ref[0:1, :] = lse_t[0:1, :]
        lse_ref[1:2, :] = lse_t[HEAD_DIM:HEAD_DIM + 1, :]

    qblk = pl.BlockSpec((T, 128), lambda hp, qi: (qi, hp))
    seq = pl.BlockSpec((S, 128), lambda hp, qi: (0, hp))
    return pl.pallas_call(
        body, name="attn_fwd", grid=(ATT_HEADS // 2, nq),
        in_specs=[qblk, qblk, seq, seq, seq],
        out_specs=[qblk, pl.BlockSpec((None, 2, T), lambda hp, qi: (hp, 0, qi))],
        out_shape=[jax.ShapeDtypeStruct((S, ATT_W), BF16),
                   jax.ShapeDtypeStruct((ATT_HEADS // 2, 2, S), F32)],
        compiler_params=_params(2),
    )(q, qa, k, ka, v)


def _attn_bwd(q, qa, k, ka, v, do, lse, dlt):
    S = q.shape[0]
    T = min(1024, S)
    nq = S // T

    def body(q_ref, qa_ref, do_ref, lse_ref, dlt_ref, k_ref, ka_ref, v_ref, dq_ref, dk_ref, dv_ref, rows_ref, cols_ref):
        kj = pl.program_id(1)

        @pl.when(kj == 0)
        def _():
            dq_ref[...] = jnp.zeros_like(dq_ref)
            rows_ref[...] = jnp.zeros_like(rows_ref)

        lane = lax.broadcasted_iota(jnp.int32, (T, 128), 1)
        lo = lane < HEAD_DIM
        alo = lane < AUG_LANES
        triu = (lax.broadcasted_iota(jnp.int32, (T, T), 0) <= lax.broadcasted_iota(jnp.int32, (T, T), 1))
        kb = k_ref[...]
        kaug = jnp.concatenate([kb, ka_ref[...]], axis=1)
        vb = v_ref[...]

        def step(qi, carry, masked):
            off = pl.multiple_of(qi * T, T)
            qb = q_ref[pl.ds(off, T), :]
            dob = do_ref[pl.ds(off, T), :]
            qm = _pair_heads(lo, alo, qb, qa_ref[pl.ds(off, T), :])
            zero = jnp.zeros_like(qb)
            new, dqs = [], []
            for h in range(2):
                dk_a, dv_a, dc_a = carry[3 * h:3 * h + 3]
                dom = jnp.where(lo, dob, zero) if h == 0 else jnp.where(lo, zero, dob)
                s = _dot_nt(kaug, qm[h])
                if masked:
                    s = jnp.where(triu, s, -1e30)
                p = jnp.exp(s - lse_ref[h:h + 1, pl.ds(off, T)])
                dp = _dot_nt(vb, dom)
                ds = p * (dp - dlt_ref[h:h + 1, pl.ds(off, T)])
                pb = p.astype(BF16)
                dsb = ds.astype(BF16)
                dv_a = dv_a + _dot(pb, dob)
                dk_a = dk_a + _dot(dsb, qb)
                dc_a = dc_a + jnp.sum(ds, axis=1, keepdims=True)
                dqs.append(_dot_tn(dsb, kb))
                rows_ref[h:h + 1, pl.ds(off, T)] += jnp.sum(ds, axis=0, keepdims=True)
                new += [dk_a, dv_a, dc_a]
            dq_ref[pl.ds(off, T), :] += jnp.where(lo, dqs[0], dqs[1])
            return tuple(new)

        init = (jnp.zeros((T, 128), F32), jnp.zeros((T, 128), F32), jnp.zeros((T, 1), F32)) * 2
        carry = step(kj, init, True)
        carry = lax.fori_loop(kj + 1, nq, lambda qi, c: step(qi, c, False), carry)
        dk_ref[...] = jnp.where(lo, carry[0], carry[3])
        dv_ref[...] = jnp.where(lo, carry[1], carry[4])
        cols_ref[...] = -jnp.where(lo, carry[2], carry[5])

    seq = pl.BlockSpec((S, 128), lambda hp, kj: (0, hp))
    rows = pl.BlockSpec((None, 2, S), lambda hp, kj: (hp, 0, 0))
    kblk = pl.BlockSpec((T, 128), lambda hp, kj: (kj, hp))
    return pl.pallas_call(
        body, name="attn_bwd", grid=(ATT_HEADS // 2, nq),
        in_specs=[seq, seq, seq, rows, rows, kblk, kblk, kblk],
        out_specs=[seq, kblk, kblk, rows, kblk],
        out_shape=[jax.ShapeDtypeStruct((S, ATT_W), F32), jax.ShapeDtypeStruct((S, ATT_W), F32),
                   jax.ShapeDtypeStruct((S, ATT_W), F32),
                   jax.ShapeDtypeStruct((ATT_HEADS // 2, 2, S), F32),
                   jax.ShapeDtypeStruct((S, ATT_W), F32)],
        compiler_params=_params(2),
    )(q, qa, do, lse, dlt, k, ka, v)


def _wo_spec(wo4):
    return pl.BlockSpec(wo4.shape, lambda i: (0, 0, 0))


def _wo_halves(wo_ref):
    half = N_CHIPS // 2
    return (wo_ref[0:half].reshape(ATT_W, D_MODEL), wo_ref[half:N_CHIPS].reshape(CONV_CH, D_MODEL))


def _mixer_out_fwd(x, att, h3, wo4, g2):
    S = x.shape[0]
    TM = min(512, S)

    def body(x_ref, att_ref, h3_ref, wo_ref, g2_ref, x2_ref, u2_ref):
        wa, wc = _wo_halves(wo_ref)
        x2 = x_ref[...] + _dot(att_ref[...], wa) + _dot(h3_ref[...], wc)
        x2_ref[...] = x2
        r = lax.rsqrt(jnp.mean(x2 * x2, axis=-1, keepdims=True) + EPS)
        u2_ref[...] = (x2 * r * g2_ref[...]).astype(BF16)

    row = lambda w: pl.BlockSpec((TM, w), lambda i: (i, 0))
    full = lambda a: pl.BlockSpec(a.shape, lambda i: (0,) * a.ndim)
    return pl.pallas_call(
        body, name="mixer_out_fwd", grid=(S // TM,),
        in_specs=[row(D_MODEL), row(ATT_W), row(CONV_CH), _wo_spec(wo4), full(g2)],
        out_specs=[row(D_MODEL), row(D_MODEL)],
        out_shape=[jax.ShapeDtypeStruct((S, D_MODEL), F32), jax.ShapeDtypeStruct((S, D_MODEL), BF16)],
        compiler_params=_params(1),
    )(x, att, h3, wo4, g2)


def _mlp_w_specs():
    return [pl.BlockSpec((None, D_MODEL, D_FF // N_CHIPS), lambda i, f: (f, 0, 0)),
            pl.BlockSpec((None, D_FF // N_CHIPS, D_MODEL), lambda i, f: (f, 0, 0))]


def _mlp_fwd(x2, u2, w1, w2, target=None):
    S = x2.shape[0]
    head = target is not None
    TM = min(1024, S)
    TF = 1024
    nf = D_FF // TF

    def body(*refs):
        x2_ref, u2_ref, w1_ref, w2_ref = refs[:4]
        x3_ref, z_ref, hh_ref = refs[4 + head:7 + head]
        i = pl.program_id(0)
        f = pl.program_id(1)

        @pl.when(f == 0)
        def _():
            x3_ref[...] = x2_ref[...]

        z = _dot(u2_ref[...], w1_ref[...])
        z_ref[...] = z
        zr = jnp.maximum(z, 0.0)
        hh = (zr * zr).astype(BF16)
        hh_ref[...] = hh
        x3_ref[...] += _dot(hh, w2_ref[...])

        if head:
            t_ref, loss_ref = refs[4], refs[8]

            @pl.when((i == 0) & (f == 0))
            def _():
                loss_ref[...] = jnp.zeros_like(loss_ref)

            @pl.when(f == nf - 1)
            def _():
                d = x3_ref[...] - t_ref[...]
                x3_ref[...] = d * (1.0 / D_MODEL)
                loss_ref[...] += jnp.sum(d * d)

    rows = pl.BlockSpec((TM, D_MODEL), lambda i, f: (i, 0))
    once = pl.BlockSpec((TM, D_MODEL), lambda i, f: (i, 0), pipeline_mode=pl.Buffered(1))
    tile = pl.BlockSpec((TM, TF), lambda i, f: (i, f))
    return pl.pallas_call(
        body, name="mlp_fwd_loss" if head else "mlp_fwd", grid=(S // TM, nf),
        in_specs=[once if head else rows, rows] + _mlp_w_specs() + [once] * head,
        out_specs=[rows, tile, tile] + [pl.BlockSpec((8, 128), lambda i, f: (0, 0))] * head,
        out_shape=[jax.ShapeDtypeStruct((S, D_MODEL), F32), jax.ShapeDtypeStruct((S, D_FF), F32),
                   jax.ShapeDtypeStruct((S, D_FF), BF16)] + [jax.ShapeDtypeStruct((8, 128), F32)] * head,
        compiler_params=_params(2),
    )(x2, u2, w1, w2, *([target] if head else []))


def _mlp_bwd(dx3, z, x2, g2, w1, w2):
    S = dx3.shape[0]
    TM = min(1024, S)
    TF = 1024
    nf = D_FF // TF

    def body(dx3_ref, z_ref, x2_ref, g2_ref, w1_ref, w2_ref, dz_ref, dx2_ref, dg2_ref, du2_ref):
        i = pl.program_id(0)
        f = pl.program_id(1)

        @pl.when((i == 0) & (f == 0))
        def _():
            dg2_ref[...] = jnp.zeros_like(dg2_ref)

        @pl.when(f == 0)
        def _():
            du2_ref[...] = jnp.zeros_like(du2_ref)

        dhh = _dot_nt(dx3_ref[...].astype(BF16), w2_ref[...])
        dz = (dhh * (2.0 * jnp.maximum(z_ref[...], 0.0))).astype(BF16)
        dz_ref[...] = dz
        du2_ref[...] += _dot_nt(dz, w1_ref[...])

        @pl.when(f == nf - 1)
        def _():
            x2 = x2_ref[...]
            r = lax.rsqrt(jnp.mean(x2 * x2, axis=-1, keepdims=True) + EPS)
            n = x2 * r
            du2 = du2_ref[...]
            t = du2 * g2_ref[...]
            dx2_ref[...] = dx3_ref[...] + r * (t - n * jnp.mean(t * n, axis=-1, keepdims=True))
            dg2_ref[0:1, :] += jnp.sum(du2 * n, axis=0, keepdims=True)

    rowi = pl.BlockSpec((TM, D_MODEL), lambda i, f: (i, 0))
    held = pl.BlockSpec((TM, D_MODEL), lambda i, f: (i, 0), pipeline_mode=pl.Buffered(1))
    return pl.pallas_call(
        body, name="mlp_bwd", grid=(S // TM, nf),
        in_specs=[held, pl.BlockSpec((TM, TF), lambda i, f: (i, f)), held,
                  pl.BlockSpec((1, D_MODEL), lambda i, f: (0, 0))] + _mlp_w_specs(),
        out_specs=[pl.BlockSpec((TM, TF), lambda i, f: (i, f)), rowi, pl.BlockSpec((8, D_MODEL), lambda i, f: (0, 0))],
        out_shape=[jax.ShapeDtypeStruct((S, D_FF), BF16), jax.ShapeDtypeStruct((S, D_MODEL), F32),
                   jax.ShapeDtypeStruct((8, D_MODEL), F32)],
        scratch_shapes=[pltpu.VMEM((TM, D_MODEL), F32)],
        compiler_params=_params(2),
    )(dx3, z, x2, g2, w1, w2)


def _matmul_tn(a, b, col_shards=1):
    S, I = a.shape
    J = b.shape[1]
    TI = min(I, 1024)
    TJ = 1024 if J % 1024 == 0 else 896
    TS = min(S, 2048)
    nk = S // TS
    per = J // col_shards // TJ

    def body(a_ref, b_ref, o_ref, acc_ref):
        k = pl.program_id(2)

        @pl.when(k == 0)
        def _():
            acc_ref[...] = jnp.zeros_like(acc_ref)

        acc_ref[...] += _dot_tn(a_ref[...].astype(BF16), b_ref[...].astype(BF16))

        @pl.when(k == nk - 1)
        def _():
            o_ref[...] = acc_ref[...].astype(BF16)

    return pl.pallas_call(
        body, name="matmul_tn", grid=(I // TI, J // TJ, nk),
        in_specs=[pl.BlockSpec((TS, TI), lambda i, j, k: (k, i)), pl.BlockSpec((TS, TJ), lambda i, j, k: (k, j))],
        out_specs=pl.BlockSpec((None, TI, TJ), lambda i, j, k: (j // per, i, j % per)),
        out_shape=jax.ShapeDtypeStruct((col_shards, I, J // col_shards), BF16),
        scratch_shapes=[pltpu.VMEM((TI, TJ), F32)],
        compiler_params=_params(3),
    )(a, b)


def _mixer_out_bwd(dx2, wo4, att, h1, lng, lnb):
    S = dx2.shape[0]
    TM = min(512, S)

    def body(dx2_ref, wo_ref, att_ref, h1_ref, lng_ref, lnb_ref, hr_ref, datt_ref, dlt_ref, dh1_ref, sm_ref):
        @pl.when(pl.program_id(0) == 0)
        def _():
            sm_ref[...] = jnp.zeros_like(sm_ref)

        dxb = dx2_ref[...].astype(BF16)
        wa, wc = _wo_halves(wo_ref)
        datt = _dot_nt(dxb, wa)
        datt_ref[...] = datt.astype(BF16)
        prod = datt * att_ref[...].astype(F32)
        dlt_ref[...] = sum(_dot_nt(hr_ref[...], piece) for piece in _split3(prod))[0:ATT_HEADS, :]
        dh3 = _dot_nt(dxb, wc)
        h1 = h1_ref[...]
        mu = jnp.mean(h1, axis=-1, keepdims=True)
        d = h1 - mu
        rstd = lax.rsqrt(jnp.mean(d * d, axis=-1, keepdims=True) + EPS)
        n = d * rstd
        h2 = n * lng_ref[...] + lnb_ref[...]
        sg = _sigmoid(h2)
        dh2 = dh3 * (sg * (1.0 + h2 * (1.0 - sg)))
        dn = dh2 * lng_ref[...]
        dh1 = rstd * (dn - jnp.mean(dn, axis=-1, keepdims=True) - n * jnp.mean(dn * n, axis=-1, keepdims=True))
        dh1_ref[...] = dh1
        sm_ref[0:1, :] += jnp.sum(dh2 * n, axis=0, keepdims=True)
        sm_ref[1:2, :] += jnp.sum(dh2, axis=0, keepdims=True)
        sm_ref[2:3, :] += jnp.sum(dh1, axis=0, keepdims=True)

    row = lambda w: pl.BlockSpec((TM, w), lambda i: (i, 0))
    full = lambda a: pl.BlockSpec(a.shape, lambda i: (0,) * a.ndim)
    hr = _head_rows()
    return pl.pallas_call(
        body, name="mixer_out_bwd", grid=(S // TM,),
        in_specs=[row(D_MODEL), _wo_spec(wo4), row(ATT_W), row(CONV_CH), full(lng), full(lnb), full(hr)],
        out_specs=[row(ATT_W), pl.BlockSpec((ATT_HEADS, TM), lambda i: (0, i)), row(CONV_CH),
                   pl.BlockSpec((8, CONV_CH), lambda i: (0, 0))],
        out_shape=[jax.ShapeDtypeStruct((S, ATT_W), BF16), jax.ShapeDtypeStruct((ATT_HEADS, S), F32),
                   jax.ShapeDtypeStruct((S, CONV_CH), F32), jax.ShapeDtypeStruct((8, CONV_CH), F32)],
        compiler_params=_params(1),
    )(dx2, wo4, att, h1, lng, lnb, hr)


def _conv_glu_bwd(dh1, h0, proj, cw):
    S = dh1.shape[0]
    TM = min(512, S)
    nb = S // TM
    lead = CONV_HALO - CONV_TAPS + 1

    def body(dh1_ref, dnx_ref, h0_ref, hpv_ref, a_ref, g_ref, cw_ref, dag_ref, dcw_ref,
             dbuf_ref, hbuf_ref, ds_ref, hs_ref, dh0_ref, dcw8_ref):
        i = pl.program_id(0)

        @pl.when(i == 0)
        def _():
            dcw8_ref[...] = jnp.zeros_like(dcw8_ref)

        dbuf_ref[0:TM, :] = dh1_ref[...]
        dbuf_ref[TM:TM + CONV_HALO, :] = jnp.where(i < nb - 1, dnx_ref[0:CONV_HALO, :], 0.0)
        hbuf_ref[0:CONV_HALO, :] = jnp.where(i > 0, hpv_ref[TM - CONV_HALO:TM, :], 0.0)
        hbuf_ref[CONV_HALO:CONV_HALO + TM, :] = h0_ref[...]
        _fill_row_shifts(dbuf_ref, ds_ref, TM)
        _fill_row_shifts(hbuf_ref, hs_ref, TM)

        def conv_rows(step, _):
            r0 = pl.multiple_of(step * CONV_ROWS, CONV_ROWS)
            dh1 = dbuf_ref[pl.ds(r0, CONV_ROWS), :]
            part = jnp.zeros((CONV_ROWS, CONV_CH), F32)
            for j in range(CONV_TAPS):
                part = part + cw_ref[j:j + 1, :] * _row_shifted(dbuf_ref, ds_ref, CONV_TAPS - 1 - j, CONV_ROWS, r0)
                prod = dh1 * _row_shifted(hbuf_ref, hs_ref, lead + j, CONV_ROWS, r0)
                dcw8_ref[j] += jnp.sum(prod.reshape(CONV_ROWS // SUBLANES, SUBLANES, CONV_CH), axis=0)
            dh0_ref[pl.ds(r0, CONV_ROWS), :] = part
            return 0

        lax.fori_loop(0, TM // CONV_ROWS, conv_rows, 0)

        @pl.when(i == nb - 1)
        def _():
            dcw_ref[...] = jnp.sum(dcw8_ref[...], axis=1)

        dh0 = dh0_ref[...]
        sg = _sigmoid(g_ref[...])
        dag_ref[:, 0:CONV_CH] = (dh0 * sg).astype(BF16)
        dag_ref[:, CONV_CH:2 * CONV_CH] = (dh0 * a_ref[...] * sg * (1.0 - sg)).astype(BF16)

    blk = lambda fn: pl.BlockSpec((TM, CONV_CH), fn)
    return pl.pallas_call(
        body, name="conv_glu_bwd", grid=(nb,),
        in_specs=[blk(lambda i: (i, 0)), blk(lambda i: (jnp.minimum(i + 1, nb - 1), 0)),
                  blk(lambda i: (i, 0)), blk(lambda i: (jnp.maximum(i - 1, 0), 0)),
                  blk(lambda i: (i, O_A // CONV_CH)), blk(lambda i: (i, O_G // CONV_CH)),
                  pl.BlockSpec(cw.shape, lambda i: (0, 0))],
        out_specs=[pl.BlockSpec((TM, 2 * CONV_CH), lambda i: (i, 0)), pl.BlockSpec((CONV_HALO, CONV_CH), lambda i: (0, 0))],
        out_shape=[jax.ShapeDtypeStruct((S, 2 * CONV_CH), BF16), jax.ShapeDtypeStruct((CONV_HALO, CONV_CH), F32)],
        scratch_shapes=[pltpu.VMEM((TM + CONV_HALO, CONV_CH), F32), pltpu.VMEM((TM + CONV_HALO, CONV_CH), F32),
                        pltpu.VMEM((SUBLANES - 1, TM + CONV_HALO, CONV_CH), F32),
                        pltpu.VMEM((SUBLANES - 1, TM + CONV_HALO, CONV_CH), F32),
                        pltpu.VMEM((TM, CONV_CH), F32), pltpu.VMEM((CONV_HALO, SUBLANES, CONV_CH), F32)],
        compiler_params=_params(1),
    )(dh1, dh1, h0, h0, proj, proj, cw)


def _mixer_in_bwd(x, dx2, proj, dq, dk, dv, dag, dct, drb, g1, win, qg, kg, bf):
    S = x.shape[0]
    TM = min(512, S)
    nb = S // TM

    def body(x_ref, dx2_ref, qr_ref, kr_ref, fz_ref, dq_ref, dk_ref, dv_ref, dag_ref, dct_ref, drb_ref,
             g1_ref, win_ref, qg_ref, kg_ref, bf_ref, bd_ref, fold_ref, triu_ref, pick_ref,
             dproj_ref, dx_ref, dg1_ref, sm_ref, carry_ref, gsum_ref):
        i = pl.program_id(0)

        @pl.when(i == 0)
        def _():
            carry_ref[...] = jnp.zeros_like(carry_ref)
            gsum_ref[...] = jnp.zeros_like(gsum_ref)
            dg1_ref[...] = jnp.zeros_like(dg1_ref)
            sm_ref[...] = jnp.zeros_like(sm_ref)

        def headnorm_bwd(raw, dy, gain, scale, row):
            rs = lax.rsqrt(_head_sums(raw * raw, bd_ref[...]) * (1.0 / HEAD_DIM) + EPS)
            n = raw * rs
            gsum_ref[row:row + 1, :] += jnp.sum(dy * n, axis=0, keepdims=True) * scale
            dn = dy * (gain * scale)
            return rs * (dn - n * (_head_sums(dn * n, bd_ref[...]) * (1.0 / HEAD_DIM)))

        dproj_ref[:, O_Q:O_Q + ATT_W] = headnorm_bwd(qr_ref[...], dq_ref[...], qg_ref[...], QK_SCALE, 0).astype(BF16)
        dproj_ref[:, O_K:O_K + ATT_W] = headnorm_bwd(kr_ref[...], dk_ref[...], kg_ref[...], 1.0, 1).astype(BF16)
        dproj_ref[:, O_V:O_V + ATT_W] = dv_ref[...].astype(BF16)
        dproj_ref[:, O_A:O_A + 2 * CONV_CH] = dag_ref[...]

        dc8 = jnp.concatenate([dct_ref[...], jnp.zeros((128 - ATT_HEADS, TM), F32)], axis=0).T
        dc8 = dc8 + _dot_hi_r(drb_ref[...], pick_ref[...])
        dlogf = _dot_hi_l(triu_ref[...], dc8) + carry_ref[...]
        carry_ref[...] = dlogf[0:1, :]
        df = dlogf * _sigmoid(-(fz_ref[...] + bf_ref[...]))
        dproj_ref[:, O_F:O_F + 128] = df.astype(BF16)
        sm_ref[2:3, :] += jnp.sum(df, axis=0, keepdims=True)

        du1 = _dot_nt(dproj_ref[...], win_ref[...])
        xv = x_ref[...]
        r = lax.rsqrt(jnp.mean(xv * xv, axis=-1, keepdims=True) + EPS)
        n1 = xv * r
        t = du1 * g1_ref[...]
        dx_ref[...] = dx2_ref[...] + r * (t - n1 * jnp.mean(t * n1, axis=-1, keepdims=True))
        dg1_ref[0:1, :] += jnp.sum(du1 * n1, axis=0, keepdims=True)

        @pl.when(i == nb - 1)
        def _():
            sm_ref[0:2, :] = _dot_hi_r(gsum_ref[0:8, :], fold_ref[...])[0:2, :]

    rev = lambda w, cb=0: pl.BlockSpec((TM, w), lambda i: (nb - 1 - i, cb))
    full = lambda a: pl.BlockSpec(a.shape, lambda i: (0,) * a.ndim)
    bd, fold, triu = _head_blockdiag(), _head_fold(), _tril(TM).T
    consts = (g1, win, qg, kg, bf, bd, fold, triu, _head_pick())
    return pl.pallas_call(
        body, name="mixer_in_bwd", grid=(nb,),
        in_specs=[rev(D_MODEL), rev(D_MODEL), rev(ATT_W, O_Q // ATT_W), rev(ATT_W, O_K // ATT_W), rev(128, O_F // 128),
                  rev(ATT_W), rev(ATT_W), rev(ATT_W), rev(2 * CONV_CH),
                  pl.BlockSpec((ATT_HEADS, TM), lambda i: (0, nb - 1 - i)), rev(ATT_W)] + [full(a) for a in consts],
        out_specs=[rev(N_INP), rev(D_MODEL), pl.BlockSpec((8, D_MODEL), lambda i: (0, 0)),
                   pl.BlockSpec((8, 128), lambda i: (0, 0))],
        out_shape=[jax.ShapeDtypeStruct((S, N_INP), BF16), jax.ShapeDtypeStruct((S, D_MODEL), F32),
                   jax.ShapeDtypeStruct((8, D_MODEL), F32), jax.ShapeDtypeStruct((8, 128), F32)],
        scratch_shapes=[pltpu.VMEM((1, 128), F32), pltpu.VMEM((8, ATT_W), F32)],
        compiler_params=_params(1),
    )(x, dx2, proj, proj, proj, dq, dk, dv, dag, dct, drb, *consts)


def _layer_fwd(x, early, late, target=None):
    p = early(x)
    u1, proj, q, k, v, qa, ka, h0, h1, h3 = _mixer_in_fwd(
        x, p["g1"], p["win"], p["qg"], p["kg"], p["bf"], p["cw"], p["cvb"], p["lng"], p["lnb"])
    att, lse = _attn_fwd(q, qa, k, ka, v)
    p = dict(p, **late(att))
    x2, u2 = _mixer_out_fwd(x, att, h3, p["wo"], p["g2"])
    x3, z, hh, *loss_acc = _mlp_fwd(x2, u2, p["w1"], p["w2"], target)
    saved = dict(x=x, u1=u1, proj=proj, q=q, k=k, v=v, qa=qa, ka=ka, h0=h0, h1=h1, h3=h3, att=att, lse=lse,
                 x2=x2, u2=u2, z=z, hh=hh)
    return (x3 if target is None else (x3, loss_acc[0])), saved, p


def _tie(a, token):
    return a if token is None else a + token[0:1, 0:1].astype(a.dtype).reshape((1,) * a.ndim)


def _layer_bwd(dx3, s, p, reduce):
    dz, dx2, dg2 = _mlp_bwd(dx3, s["z"], s["x2"], p["g2"], p["w1"], p["w2"])
    g_w2 = _matmul_tn(s["hh"], dx3)
    g_w1 = _matmul_tn(s["u2"], dz, col_shards=N_CHIPS)
    token = reduce("a", {2: g_w1, 3: g_w2.reshape(N_CHIPS, D_FF // N_CHIPS, D_MODEL)})
    datt, dlt, dh1, sm_c = _mixer_out_bwd(dx2, p["wo"], s["att"], s["h1"], _tie(p["lng"], token), p["lnb"])
    g_wo = jnp.concatenate([_matmul_tn(s["att"], dx2)[0], _matmul_tn(s["h3"], dx2)[0]], axis=0)
    dag, dcw = _conv_glu_bwd(dh1, s["h0"], s["proj"], p["cw"])
    dq, dk, dv, dc4, drb = _attn_bwd(s["q"], s["qa"], s["k"], s["ka"], s["v"], datt, s["lse"],
                                     dlt.reshape(ATT_HEADS // 2, 2, dlt.shape[1]))
    dct = dc4.reshape(ATT_HEADS, dc4.shape[2])
    dproj, dx, dg1, sm_a = _mixer_in_bwd(s["x"], dx2, s["proj"], dq, dk, dv, dag, dct, drb,
                                         p["g1"], p["win"], p["qg"], p["kg"], p["bf"])
    g_win = _win_to_global(_matmul_tn(s["u1"], dproj)[0])
    g_win = g_win.reshape(D_MODEL, N_CHIPS, N_IN // N_CHIPS).transpose(1, 0, 2)
    token = reduce("b", {0: g_win, 1: g_wo.reshape(N_CHIPS, D_MODEL // N_CHIPS, D_MODEL)})
    small = dict(g1=dg1[0], g2=dg2[0], lng=sm_c[0], lnb=sm_c[1], cvb=sm_c[2], cw=dcw[0:CONV_TAPS],
                 qg=sm_a[0, 0:HEAD_DIM], kg=sm_a[1, 0:HEAD_DIM], bf=sm_a[2, 0:ATT_HEADS])
    return dx, small, token


def _local_step(x, target, weights, reduce):
    saved, layers = [], []
    h = x
    for l, (early, late) in enumerate(weights):
        h, s, p = _layer_fwd(h, early, late, target if l == len(weights) - 1 else None)
        saved.append(s)
        layers.append(p)
    dy, loss_acc = h
    loss = loss_acc[0, 0] * (0.5 / D_MODEL)
    smalls = []
    d, token = dy, None
    for l in reversed(range(len(layers))):
        d, small, token = _layer_bwd(d, saved[l], dict(layers[l], g2=_tie(layers[l]["g2"], token)), reduce(l))
        smalls.append(small)
    return loss, d, smalls[::-1]


def _win_to_internal(w):
    pad = jnp.zeros(w.shape[:-1] + (N_INP - N_IN,), w.dtype)
    return jnp.concatenate([w[..., :1536], w[..., 1544:], w[..., 1536:1544], pad], axis=-1)


def _win_to_global(g):
    return jnp.concatenate([g[..., :1536], g[..., O_F:O_F + ATT_HEADS], g[..., 1536:O_F]], axis=-1)


def _layer_params(l, win, cw, norm1_g, b_f, q_norm_g, k_norm_g, conv_b, conv_ln_g, conv_ln_b, norm2_g):
    row = lambda a: a.reshape(1, -1)
    return dict(
        win=win, cw=jnp.pad(cw, ((0, CONV_HALO - CONV_TAPS), (0, 0))),
        g1=row(norm1_g[l]), g2=row(norm2_g[l]),
        qg=row(jnp.tile(q_norm_g[l], ATT_HEADS)), kg=row(jnp.tile(k_norm_g[l], ATT_HEADS)),
        bf=row(jnp.pad(b_f[l], (0, 128 - ATT_HEADS))),
        cvb=row(conv_b[l]), lng=row(conv_ln_g[l]), lnb=row(conv_ln_b[l]))


def _place():
    x, y, c = lax.axis_index("x"), lax.axis_index("y"), lax.axis_index("c")
    chips = [(1 - x, y), (x, 1 - y), (1 - x, 1 - y)]
    return x, y, c, chips


HBM = pl.BlockSpec(memory_space=pltpu.HBM)
SEM = pl.BlockSpec(memory_space=pltpu.SEMAPHORE)
GATHER_PEERS = N_CHIPS


def _gather_peers():
    x, y, c, chips = _place()
    return [(*chip, c) for chip in chips] + [(x, y, 1 - c)], [2 * px + py for px, py in chips] + [2 * x + y]


def _gather_start(srcs):
    n = len(srcs)

    def body(*refs):
        ins, lands = refs[:n], refs[n:2 * n]
        send_sems, recv_sems, token = refs[2 * n], refs[2 * n + 1], refs[-1]
        me = 2 * lax.axis_index("x") + lax.axis_index("y")
        peers, _ = _gather_peers()
        for g in range(n):
            for j, to in enumerate(peers):
                pltpu.make_async_remote_copy(src_ref=ins[g], dst_ref=lands[g].at[me],
                                             send_sem=send_sems.at[GATHER_PEERS * g + j],
                                             recv_sem=recv_sems.at[GATHER_PEERS * g + j],
                                             device_id=to, device_id_type=MESH).start()
        token[...] = jnp.zeros_like(token)

    lands = [lax.empty((N_CHIPS,) + a.shape, a.dtype) for a in srcs]
    outs = pl.pallas_call(
        body, name="gather_start",
        in_specs=[HBM] * (2 * n),
        out_specs=[SEM, SEM] + [HBM] * (2 * n) + [pl.BlockSpec(memory_space=pltpu.VMEM)],
        out_shape=[pltpu.SemaphoreType.DMA((GATHER_PEERS * n,)), pltpu.SemaphoreType.DMA((GATHER_PEERS * n,))]
        + [pltpu.HBM(a.shape, a.dtype) for a in srcs] + [pltpu.HBM(a.shape, a.dtype) for a in lands]
        + [jax.ShapeDtypeStruct((8, 128), F32)],
        input_output_aliases={i: 2 + i for i in range(2 * n)},
        compiler_params=pltpu.CompilerParams(has_side_effects=pltpu.SideEffectType.DATAFLOW_SIDE_EFFECTING),
    )(*[pltpu.with_memory_space_constraint(a, pltpu.HBM) for a in srcs],
      *[pltpu.with_memory_space_constraint(a, pltpu.HBM) for a in lands])
    return outs[0], outs[1], list(outs[2:2 + n]), list(outs[2 + n:2 + 2 * n]), outs[-1]


def _gather_wait(name, groups, send_sems, recv_sems, srcs, lands, after):
    k = len(groups)

    def body(*refs):
        ins, lnd = refs[:k], refs[k:2 * k]
        ssem, rsem = refs[2 * k], refs[2 * k + 1]
        peers, slots = _gather_peers()
        for i, g in enumerate(groups):
            for j, to in enumerate(peers):
                cp = pltpu.make_async_remote_copy(src_ref=ins[i], dst_ref=lnd[i].at[slots[j]],
                                                  send_sem=ssem.at[GATHER_PEERS * g + j],
                                                  recv_sem=rsem.at[GATHER_PEERS * g + j],
                                                  device_id=to, device_id_type=MESH)
                cp.wait_send()
                cp.wait_recv()

    outs = pl.pallas_call(
        body, name=name,
        in_specs=[HBM] * (2 * k) + [SEM, SEM, ANY],
        out_specs=[HBM] * (2 * k),
        out_shape=[pltpu.HBM(a.shape, a.dtype) for a in srcs] + [pltpu.HBM(a.shape, a.dtype) for a in lands],
        input_output_aliases={i: i for i in range(2 * k)},
        compiler_params=pltpu.CompilerParams(has_side_effects=pltpu.SideEffectType.DATAFLOW_SIDE_EFFECTING),
    )(*srcs, *lands, send_sems, recv_sems, after)
    return list(outs[k:])


REDUCE_STEPS = 4


def _row_half(a):
    return a.shape[-2] // 2


def _half_swap(name, gs):
    n = len(gs)

    def body(*refs):
        ins, outs = refs[:n], refs[n:2 * n]
        send_sems, recv_sems = refs[2 * n:]
        x, y, c, _ = _place()
        copies = []
        for a in range(n):
            h = _row_half(gs[a])
            copies.append(pltpu.make_async_remote_copy(
                src_ref=ins[a].at[:, pl.ds((1 - c) * h, h), :], dst_ref=outs[a], send_sem=send_sems.at[a],
                recv_sem=recv_sems.at[a], device_id=(x, y, 1 - c), device_id_type=MESH))
        for cp in copies:
            cp.start()
        for cp in copies:
            cp.wait()

    return pl.pallas_call(
        body, name=name,
        in_specs=[ANY] * n, out_specs=[ANY] * n,
        out_shape=[jax.ShapeDtypeStruct((N_CHIPS, _row_half(a), a.shape[-1]), a.dtype) for a in gs],
        scratch_shapes=[pltpu.SemaphoreType.DMA((n,)), pltpu.SemaphoreType.DMA((n,))],
        compiler_params=pltpu.CompilerParams(has_side_effects=True),
    )(*gs)


def _half_specs(gs, row_block):
    tiles = [_row_half(a) // REDUCE_STEPS for a in gs]
    return [pl.BlockSpec((N_CHIPS, t, a.shape[-1]), lambda i, p: (0, row_block(i, p), 0)) for a, t in zip(gs, tiles)]


def _half_add(name, gs, got, place):
    n = len(gs)

    def body(place_ref, *refs):
        own, theirs, outs = refs[:n], refs[n:2 * n], refs[2 * n:]
        for a in range(n):
            outs[a][...] = (own[a][...].astype(F32) + theirs[a][...].astype(F32)).astype(BF16)

    plain = _half_specs(gs, lambda i, p: i)
    return pl.pallas_call(
        body, name=name,
        grid_spec=pltpu.PrefetchScalarGridSpec(
            num_scalar_prefetch=1, grid=(REDUCE_STEPS,),
            in_specs=_half_specs(gs, lambda i, p: p[1] * REDUCE_STEPS + i) + plain, out_specs=plain),
        out_shape=[jax.ShapeDtypeStruct(a.shape, BF16) for a in got],
        compiler_params=_params(1),
    )(place, *gs, *got)


def _exchange_copies(parts, lands, send_sems, recv_sems):
    x, y, c, chips = _place()
    return [pltpu.make_async_remote_copy(src_ref=parts[a].at[2 * px + py], dst_ref=lands[a].at[j],
                                         send_sem=send_sems.at[3 * a + j], recv_sem=recv_sems.at[3 * a + j],
                                         device_id=(px, py, c), device_id_type=MESH)
            for a in range(len(parts)) for j, (px, py) in enumerate(chips)]


def _exchange_start(name, parts):
    n = len(parts)

    def body(*refs):
        _ = [cp.start() for cp in _exchange_copies(refs[:n], refs[n:2 * n], refs[2 * n], refs[2 * n + 1])]
        refs[-1][...] = jnp.zeros_like(refs[-1])

    lands = [lax.empty((N_CHIPS - 1,) + a.shape[1:], a.dtype) for a in parts]
    outs = pl.pallas_call(
        body, name=name,
        in_specs=[HBM] * (2 * n),
        out_specs=[SEM, SEM] + [HBM] * (2 * n) + [pl.BlockSpec(memory_space=pltpu.VMEM)],
        out_shape=[pltpu.SemaphoreType.DMA((3 * n,)), pltpu.SemaphoreType.DMA((3 * n,))]
        + [pltpu.HBM(a.shape, a.dtype) for a in parts] + [pltpu.HBM(a.shape, a.dtype) for a in lands]
        + [jax.ShapeDtypeStruct((8, 128), F32)],
        input_output_aliases={i: 2 + i for i in range(2 * n)},
        compiler_params=pltpu.CompilerParams(has_side_effects=pltpu.SideEffectType.DATAFLOW_SIDE_EFFECTING),
    )(*[pltpu.with_memory_space_constraint(a, pltpu.HBM) for a in parts],
      *[pltpu.with_memory_space_constraint(a, pltpu.HBM) for a in lands])
    return outs[0], outs[1], list(outs[2:2 + n]), list(outs[2 + n:2 + 2 * n]), outs[-1]


def _exchange_wait(name, send_sems, recv_sems, parts, lands, after):
    n = len(parts)

    def body(*refs):
        for cp in _exchange_copies(refs[:n], refs[n:2 * n], refs[2 * n], refs[2 * n + 1]):
            cp.wait_send()
            cp.wait_recv()

    outs = pl.pallas_call(
        body, name=name,
        in_specs=[HBM] * (2 * n) + [SEM, SEM, ANY],
        out_specs=[HBM] * (2 * n),
        out_shape=[pltpu.HBM(a.shape, a.dtype) for a in parts] + [pltpu.HBM(a.shape, a.dtype) for a in lands],
        input_output_aliases={i: i for i in range(2 * n)},
        compiler_params=pltpu.CompilerParams(has_side_effects=pltpu.SideEffectType.DATAFLOW_SIDE_EFFECTING),
    )(*parts, *lands, send_sems, recv_sems, after)
    return list(outs[:n]), list(outs[n:])


def _chip_sum(name, parts, lands, sums, place, layer):
    n = len(parts)
    tiles = [a.shape[-2] // REDUCE_STEPS for a in parts]

    def body(place_ref, *refs):
        own, got, outs = refs[:n], refs[n:2 * n], refs[3 * n:]
        for a in range(n):
            tot = own[a][...].astype(F32)
            for j in range(N_CHIPS - 1):
                tot = tot + got[a][j].astype(F32)
            outs[a][...] = tot

    own_specs = [pl.BlockSpec((None, t, a.shape[-1]), lambda i, p: (p[0], i, 0)) for a, t in zip(parts, tiles)]
    got_specs = [pl.BlockSpec((N_CHIPS - 1, t, a.shape[-1]), lambda i, p: (0, i, 0)) for a, t in zip(parts, tiles)]
    out_specs = [pl.BlockSpec((None, t, a.shape[-1]), lambda i, p: (layer, p[1] * REDUCE_STEPS + i, 0))
                 for a, t in zip(parts, tiles)]
    return pl.pallas_call(
        body, name=name,
        grid_spec=pltpu.PrefetchScalarGridSpec(num_scalar_prefetch=1, grid=(REDUCE_STEPS,),
                                               in_specs=own_specs + got_specs + [ANY] * n, out_specs=out_specs),
        out_shape=[jax.ShapeDtypeStruct(a.shape, F32) for a in sums],
        input_output_aliases={1 + 2 * n + a: a for a in range(n)},
        compiler_params=_params(1),
    )(place, *parts, *lands, *sums)


def _half_fill(name, sums, layer):
    n = len(sums)

    def body(*refs):
        ins, outs = refs[:n], refs[n:2 * n]
        send_sems, recv_sems = refs[2 * n:]
        x, y, c, _ = _place()
        copies = []
        for a in range(n):
            h = _row_half(sums[a])
            copies.append(pltpu.make_async_remote_copy(
                src_ref=ins[a].at[layer, pl.ds(c * h, h), :], dst_ref=outs[a].at[layer, pl.ds(c * h, h), :],
                send_sem=send_sems.at[a], recv_sem=recv_sems.at[a], device_id=(x, y, 1 - c), device_id_type=MESH))
        for cp in copies:
            cp.start()
        for a in range(n):
            h = _row_half(sums[a])
            theirs = outs[a].at[layer, pl.ds((1 - c) * h, h), :]
            pltpu.make_async_remote_copy(src_ref=theirs, dst_ref=theirs, send_sem=send_sems.at[a], recv_sem=recv_sems.at[a],
                                         device_id=(x, y, 1 - c), device_id_type=MESH).wait_recv()
        for cp in copies:
            cp.wait_send()

    return pl.pallas_call(
        body, name=name,
        in_specs=[ANY] * n, out_specs=[ANY] * n,
        out_shape=[jax.ShapeDtypeStruct(a.shape, a.dtype) for a in sums],
        input_output_aliases={a: a for a in range(n)},
        scratch_shapes=[pltpu.SemaphoreType.DMA((n,)), pltpu.SemaphoreType.DMA((n,))],
        compiler_params=pltpu.CompilerParams(has_side_effects=True),
    )(*sums)


def _adamw_math(w, g, m, v):
    m = ADAM_B1 * m + (1.0 - ADAM_B1) * g
    v = ADAM_B2 * v + (1.0 - ADAM_B2) * (g * g)
    m_hat = m / (1.0 - ADAM_B1 ** ADAM_STEP)
    v_hat = v / (1.0 - ADAM_B2 ** ADAM_STEP)
    delta = -ADAM_LR * (m_hat / (jnp.sqrt(v_hat) + ADAM_EPS) + ADAM_WD * w)
    return delta, m, v


def _adamw(name, ws, gs, ms, vs, layer, prev=None):
    n = len(ws)
    steps = 8
    tiles = [a.shape[-2] // steps for a in ws]

    def body(*refs):
        w_r, g_r, m_r, v_r = refs[:n], refs[n:2 * n], refs[2 * n:3 * n], refs[3 * n:4 * n]
        g_o, d_o, m_o, v_o = (refs[-4 * n:][k * n:(k + 1) * n] for k in range(4))
        for a in range(n):
            g = g_r[a][...]
            d, m, v = _adamw_math(w_r[a][...], g, m_r[a][...], v_r[a][...])
            g_o[a][...] = g
            d_o[a][...] = d
            m_o[a][...] = m
            v_o[a][...] = v

    specs = [pl.BlockSpec((None, t, a.shape[-1]), lambda i: (layer, i, 0)) for a, t in zip(ws, tiles)]
    held = [] if prev is None else [buf for kind in zip(*prev) for buf in kind]
    outs = pl.pallas_call(
        body, name=name, grid=(steps,),
        in_specs=specs * 4 + [ANY] * len(held), out_specs=specs * 4,
        out_shape=[jax.ShapeDtypeStruct(a.shape, F32) for a in ws] * 4,
        input_output_aliases={4 * n + k: k for k in range(len(held))},
        compiler_params=_params(1),
    )(*ws, *gs, *ms, *vs, *held)
    return [[outs[k * n + a] for k in range(4)] for a in range(n)]


SMALL_W = 512


def _small_allreduce_adamw(g, w, m, v, cw_w, cw_m, cw_v, cw_row0):
    R = g.shape[0]
    n_l = cw_w.shape[0]

    def body(g_ref, w_ref, m_ref, v_ref, cww_ref, cwm_ref, cwv_ref,
             gs_ref, d_ref, mo_ref, vo_ref, cg_ref, cd_ref, cmo_ref, cvo_ref,
             slots_ref, send_sems, recv_sems):
        x, y, c, _ = _place()
        me = 4 * x + 2 * y + c
        slots_ref[me] = g_ref[...]
        sends = []
        for d in range(1, 8):
            px, py, pc = x ^ (d >> 2), y ^ ((d >> 1) & 1), c ^ (d & 1)
            cp = pltpu.make_async_remote_copy(src_ref=g_ref, dst_ref=slots_ref.at[me], send_sem=send_sems.at[d - 1],
                                              recv_sem=recv_sems.at[d - 1], device_id=(px, py, pc), device_id_type=MESH)
            cp.start()
            sends.append(cp)
        for d in range(1, 8):
            px, py, pc = x ^ (d >> 2), y ^ ((d >> 1) & 1), c ^ (d & 1)
            slot = slots_ref.at[4 * px + 2 * py + pc]
            pltpu.make_async_remote_copy(src_ref=slot, dst_ref=slot, send_sem=send_sems.at[d - 1],
                                         recv_sem=recv_sems.at[d - 1], device_id=(px, py, pc),
                                         device_id_type=MESH).wait_recv()
        for cp in sends:
            cp.wait_send()
        tot = slots_ref[0]
        for k in range(1, 8):
            tot = tot + slots_ref[k]
        gs_ref[...] = tot
        dl, mn, vn = _adamw_math(w_ref[...], tot, m_ref[...], v_ref[...])
        d_ref[...] = dl
        mo_ref[...] = mn
        vo_ref[...] = vn
        chip = 2 * x + y
        for l in range(n_l):
            rows = tot[cw_row0[l]:cw_row0[l] + CONV_HALO, :]
            mine = rows[:, 0:128]
            for k in range(1, N_CHIPS):
                mine = jnp.where(chip == k, rows[:, 128 * k:128 * (k + 1)], mine)
            cg_ref[l] = mine
            dl, mn, vn = _adamw_math(cww_ref[l], mine, cwm_ref[l], cwv_ref[l])
            cd_ref[l] = dl
            cmo_ref[l] = mn
            cvo_ref[l] = vn

    vm = pl.BlockSpec(memory_space=pltpu.VMEM)
    small = jax.ShapeDtypeStruct((R, SMALL_W), F32)
    conv = jax.ShapeDtypeStruct(cw_w.shape, F32)
    return pl.pallas_call(
        body, name="small_allreduce_adamw",
        in_specs=[vm] * 7, out_specs=[vm] * 8,
        out_shape=[small] * 4 + [conv] * 4,
        scratch_shapes=[pltpu.VMEM((8, R, SMALL_W), F32), pltpu.SemaphoreType.DMA((7,)), pltpu.SemaphoreType.DMA((7,))],
        compiler_params=pltpu.CompilerParams(has_side_effects=True, vmem_limit_bytes=VMEM_LIMIT),
    )(g, w, m, v, cw_w, cw_m, cw_v)


SMALL_LAYOUT = (("conv_w", CONV_HALO), ("norm1_g", 2), ("norm2_g", 2), ("conv_b", 1), ("conv_ln_g", 1),
                ("conv_ln_b", 1), ("q_norm_g", 1), ("k_norm_g", 1), ("b_f", 1))
SMALL_ROWS = sum(r for _, r in SMALL_LAYOUT)
SMALL_ROWS_PAD = 48
LOSS_ROW = SMALL_ROWS


def _pack_small(per_layer):
    flat = []
    for d in per_layer:
        for name, r in SMALL_LAYOUT:
            n = r * SMALL_W
            a = d.get(name)
            if a is None:
                flat.append(jnp.zeros((n,), F32))
                continue
            flat.append(a.reshape(-1))
            if a.size < n:
                flat.append(jnp.zeros((n - a.size,), F32))
        spare = (SMALL_ROWS_PAD - SMALL_ROWS) * SMALL_W
        if "spare" in d:
            flat.append(d["spare"].reshape(-1))
            spare -= d["spare"].size
        flat.append(jnp.zeros((spare,), F32))
    return jnp.concatenate(flat).reshape(-1, SMALL_W)


def _unpack_small(packed, name, size):
    row0 = 0
    for nm, r in SMALL_LAYOUT:
        if nm == name:
            break
        row0 += r
    per_layer = packed.reshape(-1, SMALL_ROWS_PAD * SMALL_W)
    return per_layer[:, row0 * SMALL_W:row0 * SMALL_W + size]


SMALL_SIZES = dict(norm1_g=D_MODEL, norm2_g=D_MODEL, conv_b=CONV_CH, conv_ln_g=CONV_CH, conv_ln_b=CONV_CH,
                   q_norm_g=HEAD_DIM, k_norm_g=HEAD_DIM, b_f=ATT_HEADS)
SMALL_KEYS = dict(norm1_g="g1", norm2_g="g2", conv_b="cvb", conv_ln_g="lng", conv_ln_b="lnb",
                  q_norm_g="qg", k_norm_g="kg", b_f="bf", conv_w="cw")
CONV_W_ROW0 = 0


def kernel(x, norm1_g, w_in, b_f, q_norm_g, k_norm_g, conv_w, conv_b, conv_ln_g, conv_ln_b, w_o, norm2_g, w_mlp_in, w_mlp_out, loss_target, m_norm1_g, m_w_in, m_b_f, m_q_norm_g, m_k_norm_g, m_conv_w, m_conv_b, m_conv_ln_g, m_conv_ln_b, m_w_o, m_norm2_g, m_w_mlp_in, m_w_mlp_out, v_norm1_g, v_w_in, v_b_f, v_q_norm_g, v_k_norm_g, v_conv_w, v_conv_b, v_conv_ln_g, v_conv_ln_b, v_w_o, v_norm2_g, v_w_mlp_in, v_w_mlp_out):
    n_l = w_in.shape[0]

    per_layer = lambda l: [w_in[l].astype(BF16), conv_w[l], w_o[l].astype(BF16), w_mlp_in[l].astype(BF16),
                           w_mlp_out[l].astype(BF16)]
    n_w = len(per_layer(0))
    send_sems, recv_sems, srcs, lands, token = _gather_start([a for l in range(n_l) for a in per_layer(l)])

    def layer_weights(l):
        def wait(tag, which, after):
            groups = [n_w * l + i for i in which]
            return _gather_wait(f"gather_wait_{tag}{l}", groups, send_sems, recv_sems,
                                [srcs[g] for g in groups], [lands[g] for g in groups], after)

        def early(after):
            g_in, g_cw = wait("a", (0, 1), token if l == 0 else after)
            win = _win_to_internal(jnp.concatenate([g_in[k] for k in range(N_CHIPS)], axis=-1))
            cw = jnp.concatenate([g_cw[k] for k in range(N_CHIPS)], axis=-1)
            return _layer_params(l, win, cw, norm1_g, b_f, q_norm_g, k_norm_g, conv_b, conv_ln_g, conv_ln_b, norm2_g)

        def late(after):
            wo, w1, w2 = wait("b", (2, 3, 4), after)
            return dict(wo=wo, w1=w1, w2=w2)

        return early, late

    place = jnp.stack([2 * lax.axis_index("x") + lax.axis_index("y"), lax.axis_index("c")]).astype(jnp.int32)
    big_w = [w_in, w_o, w_mlp_in, w_mlp_out]
    pending, started, last = [], [], []

    def start_group(l, tag, grads, token=None):
        which, gs = list(grads), [_tie(g, token) for g in grads.values()]
        got = _half_swap(f"half_swap_{tag}{l}", gs)
        parts = _half_add(f"half_add_{tag}{l}", gs, got, place)
        send, recv, parts, lands, token = _exchange_start(f"exchange_start_{tag}{l}", parts)
        pending.append((f"{tag}{l}", l, which, send, recv, parts, lands))
        started.append(token)
        return token

    def reduce(l):
        def group(tag, grads):
            if (l, tag) == (0, "b"):
                last.append(grads)
                return None
            return start_group(l, tag, grads)
        return group

    loss, dx, smalls = _local_step(x[0], loss_target[0], [layer_weights(l) for l in range(n_l)], reduce)
    env = dict(norm1_g=(norm1_g, m_norm1_g, v_norm1_g), norm2_g=(norm2_g, m_norm2_g, v_norm2_g),
               conv_b=(conv_b, m_conv_b, v_conv_b), conv_ln_g=(conv_ln_g, m_conv_ln_g, v_conv_ln_g),
               conv_ln_b=(conv_ln_b, m_conv_ln_b, v_conv_ln_b), q_norm_g=(q_norm_g, m_q_norm_g, v_q_norm_g),
               k_norm_g=(k_norm_g, m_k_norm_g, v_k_norm_g), b_f=(b_f, m_b_f, v_b_f))
    g_dicts = [{nm: s[key] for nm, key in SMALL_KEYS.items()} for s in smalls]
    g_dicts[0]["spare"] = loss
    g_pack = _pack_small(g_dicts)
    packs = [_pack_small([{nm: env[nm][t][l] for nm in env} for l in range(n_l)]) for t in range(3)]
    pad_cw = lambda a: jnp.pad(a, ((0, 0), (0, CONV_HALO - CONV_TAPS), (0, 0)))
    cw_row0 = tuple(l * SMALL_ROWS_PAD + CONV_W_ROW0 for l in range(n_l))
    gs, ds, ms, vs, cg, cd, cm, cv = _small_allreduce_adamw(
        g_pack, packs[0], packs[1], packs[2], pad_cw(conv_w), pad_cw(m_conv_w), pad_cw(v_conv_w), cw_row0)

    start_group(0, "b", last[0], gs[SMALL_ROWS_PAD - 1:])
    sums = [lax.empty(w.shape, F32) for w in big_w]
    big_m = [m_w_in, m_w_o, m_w_mlp_in, m_w_mlp_out]
    big_v = [v_w_in, v_w_o, v_w_mlp_in, v_w_mlp_out]
    updated = [None] * len(big_w)
    after = started[-1]
    for tag, l, which, send, recv, parts, lands in pending:
        parts, lands = _exchange_wait(f"exchange_wait_{tag}", send, recv, parts, lands, after)
        done = _chip_sum(f"chip_sum_{tag}", parts, lands, [sums[i] for i in which], place, l)
        done = _half_fill(f"half_fill_{tag}", done, l)
        for i, a in zip(which, done):
            sums[i] = a
        prev = None if updated[which[0]] is None else [updated[i] for i in which]
        new = _adamw(f"adamw_{tag}", [big_w[i] for i in which], done, [big_m[i] for i in which],
                     [big_v[i] for i in which], l, prev)
        for i, r in zip(which, new):
            updated[i] = r
        after = new[0][1]
    g_big, d_big, nm_big, nv_big = ([r[k] for r in updated] for k in range(4))

    def small_out(packed, conv):
        o = {nm: _unpack_small(packed, nm, sz) for nm, sz in SMALL_SIZES.items()}
        o["conv_w"] = conv[:, 0:CONV_TAPS, :]
        return o

    def ordered(small, big):
        return (small["norm1_g"], big[0], small["b_f"], small["q_norm_g"], small["k_norm_g"], small["conv_w"],
                small["conv_b"], small["conv_ln_g"], small["conv_ln_b"], big[1], small["norm2_g"], big[2], big[3])

    return (gs[LOSS_ROW, 0], dx[None],
            *ordered(small_out(gs, cg), g_big), *ordered(small_out(ds, cd), d_big),
            *ordered(small_out(ms, cm), nm_big), *ordered(small_out(vs, cv), nv_big))
```
